```python
import jax, jax.numpy as jnp
from jax import lax
import numpy as np

D_MODEL = 1024
BATCH = 8
SEQ = 2048
DEPTH = 4

N_BRANCH = 4
MIX_WIDTH = D_MODEL // 2
CONF_KERNEL = 31
POOL_WINDOWS = (2, 4, 8, 16)
POOL_GROUPS = len(POOL_WINDOWS)
POOL_GROUP_WIDTH = MIX_WIDTH // POOL_GROUPS
SC_KERNEL = 3
GMLP_CHUNK = 128
GMLP_GROUPS = 4
GMLP_GROUP_WIDTH = MIX_WIDTH // GMLP_GROUPS
D_FF = 4 * D_MODEL
PLE_DIM = 256
EPS = 1e-6

COLS_A = 2 * MIX_WIDTH
COLS_B = MIX_WIDTH
COLS_C = 3 * MIX_WIDTH
COLS_D = 2 * MIX_WIDTH
COLS_G = N_BRANCH * D_MODEL
COLS_IN = COLS_A + COLS_B + COLS_C + COLS_D + COLS_G
SPLITS = (COLS_A, COLS_A + COLS_B, COLS_A + COLS_B + COLS_C, COLS_A + COLS_B + COLS_C + COLS_D)

kernel_name = "hybrid_conv_pool_shortconv_gmlp_block"


def rms_norm(x, g):
    xf = x.astype(jnp.float32)
    y = xf * lax.rsqrt(jnp.mean(xf * xf, axis=-1, keepdims=True) + EPS)
    return (y * g.astype(jnp.float32)).astype(x.dtype)


def layer_norm(x, g, b):
    xf = x.astype(jnp.float32)
    mu = jnp.mean(xf, axis=-1, keepdims=True)
    var = jnp.mean(jnp.square(xf - mu), axis=-1, keepdims=True)
    y = (xf - mu) * lax.rsqrt(var + EPS)
    return (y * g.astype(jnp.float32) + b.astype(jnp.float32)).astype(x.dtype)


def causal_depthwise_conv(x, w):
    K, C = w.shape
    return lax.conv_general_dilated(
        x, w[:, None, :].astype(x.dtype), window_strides=(1,), padding=[(K - 1, 0)],
        dimension_numbers=('NWC', 'WIO', 'NWC'), feature_group_count=C)


def multiscale_pool(u, pool_w, pool_scale):
    S = u.shape[1]
    c = jnp.cumsum(u.astype(jnp.float32), axis=1)
    pos = jnp.arange(S, dtype=jnp.float32)[:, None] + 1.0
    outs = []
    for gi, w in enumerate(POOL_WINDOWS):
        sl = slice(gi * POOL_GROUP_WIDTH, (gi + 1) * POOL_GROUP_WIDTH)
        cg = c[:, :, sl]
        c_shift = jnp.pad(cg, ((0, 0), (w, 0), (0, 0)))[:, :S]
        mean = (cg - c_shift) / jnp.minimum(pos, float(w))
        outs.append(mean.astype(u.dtype) - u[:, :, sl])
    pooled = jnp.stack(outs, axis=2)
    mixed = jnp.einsum('bsgc,gcd->bsgd', pooled, pool_w)
    return mixed.reshape(u.shape) * pool_scale


def spatial_gating(v, ws, bs):
    B, S, _ = v.shape
    n = S // GMLP_CHUNK
    mask = jnp.tril(jnp.ones((GMLP_CHUNK, GMLP_CHUNK), dtype=bool))
    ws_m = jnp.where(mask[None], ws, jnp.zeros_like(ws))
    vc = v.reshape(B, n, GMLP_CHUNK, GMLP_GROUPS, GMLP_GROUP_WIDTH)
    out = jnp.einsum('gts,bnsgc->bntgc', ws_m, vc) + bs.T[:, :, None]
    return out.reshape(B, S, MIX_WIDTH)


def mixer_block(h, w_in, conf_dw, conf_dw_b, conf_ln_g, conf_ln_b, pool_w, pool_scale,
                sc_conv, gmlp_ln_g, gmlp_ln_b, gmlp_ws, gmlp_bs, w_branch, w_out):
    B, S, _ = h.shape
    proj = h @ w_in
    a_in, pool_in, sc_in, g_in, gate_in = jnp.split(proj, SPLITS, axis=-1)
    a, a_gate = jnp.split(a_in, 2, axis=-1)
    ya = causal_depthwise_conv(a * jax.nn.sigmoid(a_gate), conf_dw) + conf_dw_b
    ya = jax.nn.silu(layer_norm(ya, conf_ln_g, conf_ln_b))
    yb = multiscale_pool(pool_in, pool_w, pool_scale)
    bg, cg, hx = jnp.split(sc_in, 3, axis=-1)
    yc = bg * causal_depthwise_conv(cg * hx, sc_conv)
    u, v = jnp.split(g_in, 2, axis=-1)
    yd = u * spatial_gating(layer_norm(v, gmlp_ln_g, gmlp_ln_b), gmlp_ws, gmlp_bs)
    branches = jnp.stack([ya, yb, yc, yd], axis=2)
    z = jnp.einsum('bskw,kwd->bskd', branches, w_branch)
    gates = jax.nn.sigmoid(gate_in.reshape(B, S, N_BRANCH, D_MODEL))
    merged = jnp.sum(gates * z, axis=2)
    return merged @ w_out


def _fwd_setup_inputs(seed: int = 0) -> dict:
    key = jax.random.key(seed)
    ks = jax.random.split(key, 26)
    f = jnp.float32
    nrm = lambda k, shape, fan: jax.random.normal(k, shape, f) * (fan ** -0.5)
    gain = lambda k, shape: 1.0 + 0.02 * jax.random.normal(k, shape, f)
    small = lambda k, shape: 0.02 * jax.random.normal(k, shape, f)
    return {
        "x": jax.random.normal(ks[0], (BATCH, SEQ, D_MODEL), f),
        "p": jax.random.normal(ks[1], (DEPTH, BATCH, SEQ, PLE_DIM), f),
        "norm_mix": gain(ks[2], (DEPTH, D_MODEL)),
        "w_in": nrm(ks[3], (DEPTH, D_MODEL, COLS_IN), D_MODEL),
        "conf_dw": nrm(ks[4], (DEPTH, CONF_KERNEL, MIX_WIDTH), CONF_KERNEL),
        "conf_dw_b": small(ks[5], (DEPTH, MIX_WIDTH)),
        "conf_ln_g": gain(ks[6], (DEPTH, MIX_WIDTH)),
        "conf_ln_b": small(ks[7], (DEPTH, MIX_WIDTH)),
        "pool_w": nrm(ks[8], (DEPTH, POOL_GROUPS, POOL_GROUP_WIDTH, POOL_GROUP_WIDTH), POOL_GROUP_WIDTH),
        "pool_scale": gain(ks[9], (DEPTH, MIX_WIDTH)),
        "sc_conv": nrm(ks[10], (DEPTH, SC_KERNEL, MIX_WIDTH), SC_KERNEL),
        "gmlp_ln_g": gain(ks[11], (DEPTH, MIX_WIDTH)),
        "gmlp_ln_b": small(ks[12], (DEPTH, MIX_WIDTH)),
        "gmlp_ws": nrm(ks[13], (DEPTH, GMLP_GROUPS, GMLP_CHUNK, GMLP_CHUNK), GMLP_CHUNK),
        "gmlp_bs": gain(ks[14], (DEPTH, GMLP_GROUPS, GMLP_CHUNK)),
        "w_branch": nrm(ks[15], (DEPTH, N_BRANCH, MIX_WIDTH, D_MODEL), MIX_WIDTH),
        "w_out": nrm(ks[16], (DEPTH, D_MODEL, D_MODEL), D_MODEL),
        "norm_mlp": gain(ks[17], (DEPTH, D_MODEL)),
        "w_up": nrm(ks[18], (DEPTH, D_MODEL, D_FF), D_MODEL),
        "w_down": nrm(ks[19], (DEPTH, D_FF, D_MODEL), D_FF),
        "norm_ple": gain(ks[20], (DEPTH, D_MODEL)),
        "w_ple": nrm(ks[21], (DEPTH, PLE_DIM, D_MODEL), PLE_DIM),
        "w_ple_gate": nrm(ks[22], (DEPTH, D_MODEL, D_MODEL), D_MODEL),
        "norm_final": gain(ks[23], (D_MODEL,)),
    }


def _fwd_reference(x, p, norm_mix, w_in, conf_dw, conf_dw_b, conf_ln_g, conf_ln_b, pool_w, pool_scale,
              sc_conv, gmlp_ln_g, gmlp_ln_b, gmlp_ws, gmlp_bs, w_branch, w_out,
              norm_mlp, w_up, w_down, norm_ple, w_ple, w_ple_gate, norm_final):
    for i in range(DEPTH):
        h = rms_norm(x, norm_mix[i])
        x = x + mixer_block(h, w_in[i], conf_dw[i], conf_dw_b[i], conf_ln_g[i], conf_ln_b[i],
                            pool_w[i], pool_scale[i], sc_conv[i], gmlp_ln_g[i], gmlp_ln_b[i],
                            gmlp_ws[i], gmlp_bs[i], w_branch[i], w_out[i])
        h = rms_norm(x, norm_mlp[i])
        x = x + jnp.square(jax.nn.relu(h @ w_up[i])) @ w_down[i]
        h = rms_norm(x, norm_ple[i])
        x = x + (p[i] @ w_ple[i]) * jax.nn.sigmoid(h @ w_ple_gate[i])
    return rms_norm(x, norm_final)


import jax as _jax
import jax.numpy as _jnp

TWIN_FORMAT = 'train_step'
FWD_PARAMS = ['x', 'p', 'norm_mix', 'w_in', 'conf_dw', 'conf_dw_b', 'conf_ln_g', 'conf_ln_b', 'pool_w', 'pool_scale', 'sc_conv', 'gmlp_ln_g', 'gmlp_ln_b', 'gmlp_ws', 'gmlp_bs', 'w_branch', 'w_out', 'norm_mlp', 'w_up', 'w_down', 'norm_ple', 'w_ple', 'w_ple_gate', 'norm_final']
TWIN_WEIGHTS = ['norm_mix', 'w_in', 'conf_dw', 'conf_dw_b', 'conf_ln_g', 'conf_ln_b', 'pool_w', 'pool_scale', 'sc_conv', 'gmlp_ln_g', 'gmlp_ln_b', 'gmlp_ws', 'gmlp_bs', 'w_branch', 'w_out', 'norm_mlp', 'w_up', 'w_down', 'norm_ple', 'w_ple', 'w_ple_gate', 'norm_final']
TWIN_DIFF_INPUT = 'x'
TWIN_INPUTS = ['x', 'p', 'norm_mix', 'w_in', 'conf_dw', 'conf_dw_b', 'conf_ln_g', 'conf_ln_b', 'pool_w', 'pool_scale', 'sc_conv', 'gmlp_ln_g', 'gmlp_ln_b', 'gmlp_ws', 'gmlp_bs', 'w_branch', 'w_out', 'norm_mlp', 'w_up', 'w_down', 'norm_ple', 'w_ple', 'w_ple_gate', 'norm_final', 'loss_target', 'm_norm_mix', 'm_w_in', 'm_conf_dw', 'm_conf_dw_b', 'm_conf_ln_g', 'm_conf_ln_b', 'm_pool_w', 'm_pool_scale', 'm_sc_conv', 'm_gmlp_ln_g', 'm_gmlp_ln_b', 'm_gmlp_ws', 'm_gmlp_bs', 'm_w_branch', 'm_w_out', 'm_norm_mlp', 'm_w_up', 'm_w_down', 'm_norm_ple', 'm_w_ple', 'm_w_ple_gate', 'm_norm_final', 'v_norm_mix', 'v_w_in', 'v_conf_dw', 'v_conf_dw_b', 'v_conf_ln_g', 'v_conf_ln_b', 'v_pool_w', 'v_pool_scale', 'v_sc_conv', 'v_gmlp_ln_g', 'v_gmlp_ln_b', 'v_gmlp_ws', 'v_gmlp_bs', 'v_w_branch', 'v_w_out', 'v_norm_mlp', 'v_w_up', 'v_w_down', 'v_norm_ple', 'v_w_ple', 'v_w_ple_gate', 'v_norm_final']
TWIN_OUTPUTS = ['loss', 'grad_x', 'grad_norm_mix', 'grad_w_in', 'grad_conf_dw', 'grad_conf_dw_b', 'grad_conf_ln_g', 'grad_conf_ln_b', 'grad_pool_w', 'grad_pool_scale', 'grad_sc_conv', 'grad_gmlp_ln_g', 'grad_gmlp_ln_b', 'grad_gmlp_ws', 'grad_gmlp_bs', 'grad_w_branch', 'grad_w_out', 'grad_norm_mlp', 'grad_w_up', 'grad_w_down', 'grad_norm_ple', 'grad_w_ple', 'grad_w_ple_gate', 'grad_norm_final', 'delta_norm_mix', 'delta_w_in', 'delta_conf_dw', 'delta_conf_dw_b', 'delta_conf_ln_g', 'delta_conf_ln_b', 'delta_pool_w', 'delta_pool_scale', 'delta_sc_conv', 'delta_gmlp_ln_g', 'delta_gmlp_ln_b', 'delta_gmlp_ws', 'delta_gmlp_bs', 'delta_w_branch', 'delta_w_out', 'delta_norm_mlp', 'delta_w_up', 'delta_w_down', 'delta_norm_ple', 'delta_w_ple', 'delta_w_ple_gate', 'delta_norm_final', 'new_m_norm_mix', 'new_m_w_in', 'new_m_conf_dw', 'new_m_conf_dw_b', 'new_m_conf_ln_g', 'new_m_conf_ln_b', 'new_m_pool_w', 'new_m_pool_scale', 'new_m_sc_conv', 'new_m_gmlp_ln_g', 'new_m_gmlp_ln_b', 'new_m_gmlp_ws', 'new_m_gmlp_bs', 'new_m_w_branch', 'new_m_w_out', 'new_m_norm_mlp', 'new_m_w_up', 'new_m_w_down', 'new_m_norm_ple', 'new_m_w_ple', 'new_m_w_ple_gate', 'new_m_norm_final', 'new_v_norm_mix', 'new_v_w_in', 'new_v_conf_dw', 'new_v_conf_dw_b', 'new_v_conf_ln_g', 'new_v_conf_ln_b', 'new_v_pool_w', 'new_v_pool_scale', 'new_v_sc_conv', 'new_v_gmlp_ln_g', 'new_v_gmlp_ln_b', 'new_v_gmlp_ws', 'new_v_gmlp_bs', 'new_v_w_branch', 'new_v_w_out', 'new_v_norm_mlp', 'new_v_w_up', 'new_v_w_down', 'new_v_norm_ple', 'new_v_w_ple', 'new_v_w_ple_gate', 'new_v_norm_final']
TWIN_LEAF_KINDS = {'loss': 'loss', 'grad_x': 'grad_x', 'grad_norm_mix': 'grad_w', 'grad_w_in': 'grad_w', 'grad_conf_dw': 'grad_w', 'grad_conf_dw_b': 'grad_w', 'grad_conf_ln_g': 'grad_w', 'grad_conf_ln_b': 'grad_w', 'grad_pool_w': 'grad_w', 'grad_pool_scale': 'grad_w', 'grad_sc_conv': 'grad_w', 'grad_gmlp_ln_g': 'grad_w', 'grad_gmlp_ln_b': 'grad_w', 'grad_gmlp_ws': 'grad_w', 'grad_gmlp_bs': 'grad_w', 'grad_w_branch': 'grad_w', 'grad_w_out': 'grad_w', 'grad_norm_mlp': 'grad_w', 'grad_w_up': 'grad_w', 'grad_w_down': 'grad_w', 'grad_norm_ple': 'grad_w', 'grad_w_ple': 'grad_w', 'grad_w_ple_gate': 'grad_w', 'grad_norm_final': 'grad_w', 'delta_norm_mix': 'delta_w', 'delta_w_in': 'delta_w', 'delta_conf_dw': 'delta_w', 'delta_conf_dw_b': 'delta_w', 'delta_conf_ln_g': 'delta_w', 'delta_conf_ln_b': 'delta_w', 'delta_pool_w': 'delta_w', 'delta_pool_scale': 'delta_w', 'delta_sc_conv': 'delta_w', 'delta_gmlp_ln_g': 'delta_w', 'delta_gmlp_ln_b': 'delta_w', 'delta_gmlp_ws': 'delta_w', 'delta_gmlp_bs': 'delta_w', 'delta_w_branch': 'delta_w', 'delta_w_out': 'delta_w', 'delta_norm_mlp': 'delta_w', 'delta_w_up': 'delta_w', 'delta_w_down': 'delta_w', 'delta_norm_ple': 'delta_w', 'delta_w_ple': 'delta_w', 'delta_w_ple_gate': 'delta_w', 'delta_norm_final': 'delta_w', 'new_m_norm_mix': 'new_m', 'new_m_w_in': 'new_m', 'new_m_conf_dw': 'new_m', 'new_m_conf_dw_b': 'new_m', 'new_m_conf_ln_g': 'new_m', 'new_m_conf_ln_b': 'new_m', 'new_m_pool_w': 'new_m', 'new_m_pool_scale': 'new_m', 'new_m_sc_conv': 'new_m', 'new_m_gmlp_ln_g': 'new_m', 'new_m_gmlp_ln_b': 'new_m', 'new_m_gmlp_ws': 'new_m', 'new_m_gmlp_bs': 'new_m', 'new_m_w_branch': 'new_m', 'new_m_w_out': 'new_m', 'new_m_norm_mlp': 'new_m', 'new_m_w_up': 'new_m', 'new_m_w_down': 'new_m', 'new_m_norm_ple': 'new_m', 'new_m_w_ple': 'new_m', 'new_m_w_ple_gate': 'new_m', 'new_m_norm_final': 'new_m', 'new_v_norm_mix': 'new_v', 'new_v_w_in': 'new_v', 'new_v_conf_dw': 'new_v', 'new_v_conf_dw_b': 'new_v', 'new_v_conf_ln_g': 'new_v', 'new_v_conf_ln_b': 'new_v', 'new_v_pool_w': 'new_v', 'new_v_pool_scale': 'new_v', 'new_v_sc_conv': 'new_v', 'new_v_gmlp_ln_g': 'new_v', 'new_v_gmlp_ln_b': 'new_v', 'new_v_gmlp_ws': 'new_v', 'new_v_gmlp_bs': 'new_v', 'new_v_w_branch': 'new_v', 'new_v_w_out': 'new_v', 'new_v_norm_mlp': 'new_v', 'new_v_w_up': 'new_v', 'new_v_w_down': 'new_v', 'new_v_norm_ple': 'new_v', 'new_v_w_ple': 'new_v', 'new_v_w_ple_gate': 'new_v', 'new_v_norm_final': 'new_v'}


def _forward(args):
    return _fwd_reference(*[args[k] for k in FWD_PARAMS])


def _output_shape():
    out = _jax.eval_shape(lambda: _forward(_fwd_setup_inputs(0)))
    return out.shape, out.dtype

N_MICROBATCH = 1
ADAM_LR = 0.001
ADAM_B1 = 0.9
ADAM_B2 = 0.999
ADAM_EPS = 1e-08
ADAM_WD = 0.01
ADAM_STEP = 10
PER_EXAMPLE_BATCH_AXIS = {'x': 0, 'p': 1, 'loss_target': 0}
SHARED_INPUTS = []
_WEIGHT_DTYPES = {'norm_mix': _jnp.float32, 'w_in': _jnp.float32, 'conf_dw': _jnp.float32, 'conf_dw_b': _jnp.float32, 'conf_ln_g': _jnp.float32, 'conf_ln_b': _jnp.float32, 'pool_w': _jnp.float32, 'pool_scale': _jnp.float32, 'sc_conv': _jnp.float32, 'gmlp_ln_g': _jnp.float32, 'gmlp_ln_b': _jnp.float32, 'gmlp_ws': _jnp.float32, 'gmlp_bs': _jnp.float32, 'w_branch': _jnp.float32, 'w_out': _jnp.float32, 'norm_mlp': _jnp.float32, 'w_up': _jnp.float32, 'w_down': _jnp.float32, 'norm_ple': _jnp.float32, 'w_ple': _jnp.float32, 'w_ple_gate': _jnp.float32, 'norm_final': _jnp.float32}
MOMENT_SCALE = {'norm_mix': 1.084626e-01, 'w_in': 3.581734e-02, 'conf_dw': 3.443030e-02, 'conf_dw_b': 8.007924e-02, 'conf_ln_g': 4.170766e-02, 'conf_ln_b': 3.701244e-02, 'pool_w': 4.758440e-02, 'pool_scale': 4.905726e-02, 'sc_conv': 5.674383e-02, 'gmlp_ln_g': 4.084883e-02, 'gmlp_ln_b': 3.743995e-02, 'gmlp_ws': 3.861576e-02, 'gmlp_bs': 5.417157e-02, 'w_branch': 3.701070e-02, 'w_out': 7.420863e-02, 'norm_mlp': 7.973894e-02, 'w_up': 3.987913e-02, 'w_down': 7.170515e-02, 'norm_ple': 1.159465e-02, 'w_ple': 3.025030e-02, 'w_ple_gate': 1.188191e-02, 'norm_final': 1.618125e+01}


def _to_microbatches(a, axis):
    t = _jnp.moveaxis(a, axis, 0)
    t = t.reshape((N_MICROBATCH, t.shape[0] // N_MICROBATCH) + t.shape[1:])
    return _jnp.moveaxis(t, 1, axis + 1)


def setup_inputs(seed: int = 0) -> dict:
    inp = _fwd_setup_inputs(seed)
    key = _jax.random.fold_in(_jax.random.key(seed), 7919)
    shape, _ = _output_shape()
    out = dict(inp)
    out["loss_target"] = _jax.random.normal(_jax.random.fold_in(key, 0), shape, _jnp.float32)
    for i, name in enumerate(TWIN_WEIGHTS):
        w = inp[name].astype(_jnp.float32)
        if MOMENT_SCALE is None:
            s = _jnp.sqrt(_jnp.mean(_jnp.square(w)) + 1e-30)
        else:
            s = MOMENT_SCALE[name]
        km, kv = _jax.random.split(_jax.random.fold_in(key, i + 1))
        out[name] = w
        out["m_" + name] = s * _jax.random.normal(km, w.shape, _jnp.float32)
        out["v_" + name] = (s * s) * _jax.random.uniform(kv, w.shape, _jnp.float32, 0.5, 1.5)
    if N_MICROBATCH > 1:
        for name, axis in PER_EXAMPLE_BATCH_AXIS.items():
            out[name] = _to_microbatches(out[name], axis)
    return {'x': out['x'], 'p': out['p'], 'norm_mix': out['norm_mix'], 'w_in': out['w_in'], 'conf_dw': out['conf_dw'], 'conf_dw_b': out['conf_dw_b'], 'conf_ln_g': out['conf_ln_g'], 'conf_ln_b': out['conf_ln_b'], 'pool_w': out['pool_w'], 'pool_scale': out['pool_scale'], 'sc_conv': out['sc_conv'], 'gmlp_ln_g': out['gmlp_ln_g'], 'gmlp_ln_b': out['gmlp_ln_b'], 'gmlp_ws': out['gmlp_ws'], 'gmlp_bs': out['gmlp_bs'], 'w_branch': out['w_branch'], 'w_out': out['w_out'], 'norm_mlp': out['norm_mlp'], 'w_up': out['w_up'], 'w_down': out['w_down'], 'norm_ple': out['norm_ple'], 'w_ple': out['w_ple'], 'w_ple_gate': out['w_ple_gate'], 'norm_final': out['norm_final'], 'loss_target': out['loss_target'], 'm_norm_mix': out['m_norm_mix'], 'm_w_in': out['m_w_in'], 'm_conf_dw': out['m_conf_dw'], 'm_conf_dw_b': out['m_conf_dw_b'], 'm_conf_ln_g': out['m_conf_ln_g'], 'm_conf_ln_b': out['m_conf_ln_b'], 'm_pool_w': out['m_pool_w'], 'm_pool_scale': out['m_pool_scale'], 'm_sc_conv': out['m_sc_conv'], 'm_gmlp_ln_g': out['m_gmlp_ln_g'], 'm_gmlp_ln_b': out['m_gmlp_ln_b'], 'm_gmlp_ws': out['m_gmlp_ws'], 'm_gmlp_bs': out['m_gmlp_bs'], 'm_w_branch': out['m_w_branch'], 'm_w_out': out['m_w_out'], 'm_norm_mlp': out['m_norm_mlp'], 'm_w_up': out['m_w_up'], 'm_w_down': out['m_w_down'], 'm_norm_ple': out['m_norm_ple'], 'm_w_ple': out['m_w_ple'], 'm_w_ple_gate': out['m_w_ple_gate'], 'm_norm_final': out['m_norm_final'], 'v_norm_mix': out['v_norm_mix'], 'v_w_in': out['v_w_in'], 'v_conf_dw': out['v_conf_dw'], 'v_conf_dw_b': out['v_conf_dw_b'], 'v_conf_ln_g': out['v_conf_ln_g'], 'v_conf_ln_b': out['v_conf_ln_b'], 'v_pool_w': out['v_pool_w'], 'v_pool_scale': out['v_pool_scale'], 'v_sc_conv': out['v_sc_conv'], 'v_gmlp_ln_g': out['v_gmlp_ln_g'], 'v_gmlp_ln_b': out['v_gmlp_ln_b'], 'v_gmlp_ws': out['v_gmlp_ws'], 'v_gmlp_bs': out['v_gmlp_bs'], 'v_w_branch': out['v_w_branch'], 'v_w_out': out['v_w_out'], 'v_norm_mlp': out['v_norm_mlp'], 'v_w_up': out['v_w_up'], 'v_w_down': out['v_w_down'], 'v_norm_ple': out['v_norm_ple'], 'v_w_ple': out['v_w_ple'], 'v_w_ple_gate': out['v_w_ple_gate'], 'v_norm_final': out['v_norm_final']}


def _loss(weights, diff, rest, loss_target):
    with _jax.named_scope("forward"):
        args = {**rest, TWIN_DIFF_INPUT: diff, **{k: w.astype(_WEIGHT_DTYPES[k]) for k, w in weights.items()}}
        y = _forward(args)
    with _jax.named_scope("loss_head"):
        err = _jnp.square(y.astype(_jnp.float32) - loss_target)
        return 0.5 * _jnp.sum(_jnp.mean(err, axis=-1)) if err.ndim else 0.5 * err


def _adamw(w, g, m, v):
    m = ADAM_B1 * m + (1.0 - ADAM_B1) * g
    v = ADAM_B2 * v + (1.0 - ADAM_B2) * _jnp.square(g)
    m_hat = m / (1.0 - ADAM_B1 ** ADAM_STEP)
    v_hat = v / (1.0 - ADAM_B2 ** ADAM_STEP)
    delta = -ADAM_LR * (m_hat / (_jnp.sqrt(v_hat) + ADAM_EPS) + ADAM_WD * w)
    return delta, m, v


def reference(x, p, norm_mix, w_in, conf_dw, conf_dw_b, conf_ln_g, conf_ln_b, pool_w, pool_scale, sc_conv, gmlp_ln_g, gmlp_ln_b, gmlp_ws, gmlp_bs, w_branch, w_out, norm_mlp, w_up, w_down, norm_ple, w_ple, w_ple_gate, norm_final, loss_target, m_norm_mix, m_w_in, m_conf_dw, m_conf_dw_b, m_conf_ln_g, m_conf_ln_b, m_pool_w, m_pool_scale, m_sc_conv, m_gmlp_ln_g, m_gmlp_ln_b, m_gmlp_ws, m_gmlp_bs, m_w_branch, m_w_out, m_norm_mlp, m_w_up, m_w_down, m_norm_ple, m_w_ple, m_w_ple_gate, m_norm_final, v_norm_mix, v_w_in, v_conf_dw, v_conf_dw_b, v_conf_ln_g, v_conf_ln_b, v_pool_w, v_pool_scale, v_sc_conv, v_gmlp_ln_g, v_gmlp_ln_b, v_gmlp_ws, v_gmlp_bs, v_w_branch, v_w_out, v_norm_mlp, v_w_up, v_w_down, v_norm_ple, v_w_ple, v_w_ple_gate, v_norm_final):
    given = dict(x=x, p=p, norm_mix=norm_mix, w_in=w_in, conf_dw=conf_dw, conf_dw_b=conf_dw_b, conf_ln_g=conf_ln_g, conf_ln_b=conf_ln_b, pool_w=pool_w, pool_scale=pool_scale, sc_conv=sc_conv, gmlp_ln_g=gmlp_ln_g, gmlp_ln_b=gmlp_ln_b, gmlp_ws=gmlp_ws, gmlp_bs=gmlp_bs, w_branch=w_branch, w_out=w_out, norm_mlp=norm_mlp, w_up=w_up, w_down=w_down, norm_ple=norm_ple, w_ple=w_ple, w_ple_gate=w_ple_gate, norm_final=norm_final, loss_target=loss_target, m_norm_mix=m_norm_mix, m_w_in=m_w_in, m_conf_dw=m_conf_dw, m_conf_dw_b=m_conf_dw_b, m_conf_ln_g=m_conf_ln_g, m_conf_ln_b=m_conf_ln_b, m_pool_w=m_pool_w, m_pool_scale=m_pool_scale, m_sc_conv=m_sc_conv, m_gmlp_ln_g=m_gmlp_ln_g, m_gmlp_ln_b=m_gmlp_ln_b, m_gmlp_ws=m_gmlp_ws, m_gmlp_bs=m_gmlp_bs, m_w_branch=m_w_branch, m_w_out=m_w_out, m_norm_mlp=m_norm_mlp, m_w_up=m_w_up, m_w_down=m_w_down, m_norm_ple=m_norm_ple, m_w_ple=m_w_ple, m_w_ple_gate=m_w_ple_gate, m_norm_final=m_norm_final, v_norm_mix=v_norm_mix, v_w_in=v_w_in, v_conf_dw=v_conf_dw, v_conf_dw_b=v_conf_dw_b, v_conf_ln_g=v_conf_ln_g, v_conf_ln_b=v_conf_ln_b, v_pool_w=v_pool_w, v_pool_scale=v_pool_scale, v_sc_conv=v_sc_conv, v_gmlp_ln_g=v_gmlp_ln_g, v_gmlp_ln_b=v_gmlp_ln_b, v_gmlp_ws=v_gmlp_ws, v_gmlp_bs=v_gmlp_bs, v_w_branch=v_w_branch, v_w_out=v_w_out, v_norm_mlp=v_norm_mlp, v_w_up=v_w_up, v_w_down=v_w_down, v_norm_ple=v_norm_ple, v_w_ple=v_w_ple, v_w_ple_gate=v_w_ple_gate, v_norm_final=v_norm_final)
    weights = {n: given[n] for n in TWIN_WEIGHTS}
    shared = {n: given[n] for n in SHARED_INPUTS}
    per_example = {n: given[n] for n in ['x', 'p']}
    grad_fn = _jax.value_and_grad(_loss, argnums=(0, 1))

    def one_microbatch(ex, loss_target):
        ex = dict(ex)
        diff = ex.pop(TWIN_DIFF_INPUT)
        return grad_fn(weights, diff, {**shared, **ex}, loss_target)

    if N_MICROBATCH == 1:
        loss, (grad_w, grad_x) = one_microbatch(per_example, given["loss_target"])
    else:
        def body(carry, xs):
            loss_sum, grad_sum = carry
            l_k, (gw_k, gx_k) = one_microbatch(xs[0], xs[1])
            with _jax.named_scope("update"):
                return (loss_sum + l_k, _jax.tree.map(_jnp.add, grad_sum, gw_k)), gx_k

        init = (_jnp.zeros((), _jnp.float32), _jax.tree.map(_jnp.zeros_like, weights))
        (loss, grad_w), grad_x = _jax.lax.scan(body, init, (per_example, given["loss_target"]))
    with _jax.named_scope("update"):
        delta_w, new_m, new_v = {}, {}, {}
        for n in TWIN_WEIGHTS:
            delta_w[n], new_m[n], new_v[n] = _adamw(weights[n], grad_w[n], given["m_" + n], given["v_" + n])
    return (loss, grad_x, *[grad_w[n] for n in TWIN_WEIGHTS], *[delta_w[n] for n in TWIN_WEIGHTS],
            *[new_m[n] for n in TWIN_WEIGHTS], *[new_v[n] for n in TWIN_WEIGHTS])
```

```python
import functools

import jax
import jax.numpy as jnp
from jax import lax
from jax.experimental import pallas as pl
from jax.experimental.pallas import tpu as pltpu

F32 = jnp.float32
BF16 = jnp.bfloat16

DEPTH = 4
T = 2048
D = 1024
W = 512
NDEV = 8
NCHIP = 4
EPS = 1e-6
CONF_K = 31
SC_K = 3
POOL_WINDOWS = (2, 4, 8, 16)
GW = 128
HB = 32
HA = 32
COLS_IN = 8192
MIX_COLS = 4096

ADAM_LR = 0.001
ADAM_B1 = 0.9
ADAM_B2 = 0.999
ADAM_EPS = 1e-08
ADAM_WD = 0.01
ADAM_STEP = 10

VMEM_LIMIT_BYTES = 56 * 1024 * 1024
MESH = pl.DeviceIdType.MESH


def _cp(*sem):
    return pltpu.CompilerParams(dimension_semantics=tuple(sem), vmem_limit_bytes=VMEM_LIMIT_BYTES)


def _sig(x):
    return jax.nn.sigmoid(x)


def _rms(x, g):
    r = lax.rsqrt(jnp.mean(x * x, axis=-1, keepdims=True) + EPS)
    return x * r * g


def _rms_bwd(dh, x, g, dres):
    r = lax.rsqrt(jnp.mean(x * x, axis=-1, keepdims=True) + EPS)
    xh = x * r
    u = dh * g
    dx = r * (u - xh * jnp.mean(u * xh, axis=-1, keepdims=True)) + dres
    dg = jnp.sum(dh * xh, axis=0, keepdims=True)
    return dx, dg


def _ln_stats(x):
    mu = jnp.mean(x, axis=-1, keepdims=True)
    xc = x - mu
    rstd = lax.rsqrt(jnp.mean(xc * xc, axis=-1, keepdims=True) + EPS)
    return xc * rstd, rstd


def _ln_bwd(dxh, xh, rstd):
    return rstd * (dxh - jnp.mean(dxh, axis=-1, keepdims=True) - xh * jnp.mean(dxh * xh, axis=-1, keepdims=True))


def _rowsum(x):
    return jnp.sum(x, axis=0, keepdims=True)


def _mm(a, b3, *, mode, name, outs, trans_b=False, tm=512, tiles=(), params=(), epi=None, reds=()):
    t_, ka = a.shape
    nj, r, c = b3.shape
    kb, nb = (c, r) if trans_b else (r, c)
    nt = t_ // tm
    out_mode = mode == "out"
    if out_mode:
        assert ka == kb and not reds
        grid = (nj, nt)
        a_map = lambda g0, g1: (g1, 0)
        b_map = lambda g0, g1: (g0, 0, 0)
        t_map = lambda g0, g1: (g1, g0)
        width = nj * nb
    else:
        assert ka == nj * kb
        grid = (nt, nj)
        a_map = lambda g0, g1: (g0, g1)
        b_map = lambda g0, g1: (g1, 0, 0)
        t_map = lambda g0, g1: (g0, 0)
        width = nb
    n_t, n_p, n_o, n_r = len(tiles), len(params), len(outs), len(reds)
    use_acc = (not out_mode) and nj > 1
    dims = (((1,), (1,)), ((), ())) if trans_b else (((1,), (0,)), ((), ()))

    def body(a_ref, b_ref, *rest):
        t_refs = rest[:n_t]
        p_refs = rest[n_t:n_t + n_p]
        o_refs = rest[n_t + n_p:n_t + n_p + n_o]
        r_refs = rest[n_t + n_p + n_o:n_t + n_p + n_o + n_r]
        part = lax.dot_general(a_ref[...], b_ref[...], dims, preferred_element_type=F32)
        i = pl.program_id(1 if out_mode else 0)

        def finish(acc):
            if epi is None:
                res, rr = (acc,), ()
            else:
                res, rr = epi(acc, [t[...] for t in t_refs], [p[...] for p in p_refs])
            for o_ref, val in zip(o_refs, res):
                o_ref[...] = val.astype(o_ref.dtype)
            for r_ref, val in zip(r_refs, rr):
                @pl.when(i == 0)
                def _():
                    r_ref[...] = val

                @pl.when(i > 0)
                def _():
                    r_ref[...] += val

        if use_acc:
            acc_ref = rest[-1]
            j = pl.program_id(1)

            @pl.when(j == 0)
            def _():
                acc_ref[...] = part

            @pl.when(j > 0)
            def _():
                acc_ref[...] += part

            @pl.when(j == nj - 1)
            def _():
                finish(acc_ref[...])
        else:
            finish(part)

    const2 = lambda g0, g1: (0, 0)
    in_specs = [pl.BlockSpec((tm, kb), a_map), pl.BlockSpec((None, r, c), b_map)]
    in_specs += [pl.BlockSpec((tm, nb), t_map) for _ in tiles]
    in_specs += [pl.BlockSpec(p.shape, const2) for p in params]
    out_specs = [pl.BlockSpec((tm, nb), t_map) for _ in outs] + [pl.BlockSpec((1, w), const2) for w in reds]
    out_shape = [jax.ShapeDtypeStruct((t_, width), dt) for dt in outs]
    out_shape += [jax.ShapeDtypeStruct((1, w), F32) for w in reds]
    res = pl.pallas_call(
        body, name=name, grid=grid, in_specs=in_specs, out_specs=out_specs, out_shape=out_shape,
        scratch_shapes=[pltpu.VMEM((tm, nb), F32)] if use_acc else [],
        compiler_params=_cp("arbitrary", "arbitrary"),
    )(a, b3, *tiles, *params)
    return res


def _mm_tn(a, g, *, nj, split, name, out_dtype=BF16):
    t_ = a.shape[0]
    if split == "col":
        r, c = a.shape[1], g.shape[1] // nj
        a_spec = pl.BlockSpec((t_, r), lambda j: (0, 0))
        g_spec = pl.BlockSpec((t_, c), lambda j: (0, j))
    else:
        r, c = a.shape[1] // nj, g.shape[1]
        a_spec = pl.BlockSpec((t_, r), lambda j: (0, j))
        g_spec = pl.BlockSpec((t_, c), lambda j: (0, 0))

    def body(a_ref, g_ref, o_ref):
        o_ref[...] = lax.dot_general(a_ref[...], g_ref[...], (((0,), (0,)), ((), ())),
                                     preferred_element_type=F32).astype(o_ref.dtype)

    return pl.pallas_call(
        body, name=name, grid=(nj,), in_specs=[a_spec, g_spec],
        out_specs=pl.BlockSpec((None, r, c), lambda j: (j, 0, 0)),
        out_shape=jax.ShapeDtypeStruct((nj, r, c), out_dtype),
        compiler_params=_cp("arbitrary"),
    )(a, g)


def _epi_res_norm(acc, tiles, params):
    x_new = tiles[0] + acc
    return (x_new, _rms(x_new, params[0])), ()


def _epi_relu2(acc, tiles, params):
    r = jnp.maximum(acc, 0.0)
    return (acc, r * r), ()


def _epi_ple(acc, tiles, params):
    x_new = tiles[0] + tiles[1].astype(F32) * _sig(acc)
    return (x_new, acc, _rms(x_new, params[0])), ()


def _epi_rms_bwd(acc, tiles, params):
    dx, dg = _rms_bwd(acc, tiles[0], params[0], tiles[1])
    return (dx, dx), (dg,)


def _epi_dup(acc, tiles, params):
    return (acc * (2.0 * jnp.maximum(tiles[0].astype(F32), 0.0)),), ()


def _tri_mask():
    row = lax.broadcasted_iota(jnp.int32, (GW, GW), 0)
    col = lax.broadcasted_iota(jnp.int32, (GW, GW), 1)
    return row >= col


def _small_specs(sp_list):
    return [pl.BlockSpec(p.shape, (lambda i: (0, 0)) if p.ndim == 2 else (lambda i: (0, 0, 0))) for p in sp_list]


def _mixer_params(sp):
    return [sp["cw"], sp["cb"], sp["lg"], sp["lb"], sp["pw"], sp["ps"], sp["sc"], sp["gg"], sp["gb"], sp["ws"], sp["bst"]]


def _mixer_fwd(proj, sp, tm=256):
    nt = T // tm
    per = tm // HB

    def body(main_ref, halo_ref, cw, cb, lg, lb, pw, ps, sc, gg, gb, ws, bst, y_ref, ext):
        i = pl.program_id(0)
        keep = (i > 0).astype(F32)

        def mcol(c0):
            return main_ref[:, c0:c0 + W].astype(F32)

        def hcol(c0):
            return halo_ref[:, c0:c0 + W].astype(F32)

        ext[0:HB, :] = hcol(0) * _sig(hcol(W)) * keep
        ext[HB:HB + tm, :] = mcol(0) * _sig(mcol(W))
        ca = jnp.broadcast_to(cb[...], (tm, W))
        for k in range(CONF_K):
            ca = ca + cw[k:k + 1, :] * ext[pl.ds(HB - (CONF_K - 1) + k, tm), :]
        xh, _ = _ln_stats(ca)
        n = xh * lg[...] + lb[...]
        y_ref[:, 0:W] = (n * _sig(n)).astype(BF16)

        pin = mcol(1024)
        ext[0:HB, :] = hcol(1024) * keep
        ext[HB:HB + tm, :] = pin
        pos = (i * tm + lax.broadcasted_iota(jnp.int32, (tm, 1), 0) + 1).astype(F32)
        for g, w in enumerate(POOL_WINDOWS):
            lo = g * GW
            s = ext[pl.ds(HB, tm), lo:lo + GW]
            for j in range(1, w):
                s = s + ext[pl.ds(HB - j, tm), lo:lo + GW]
            pooled = s / jnp.minimum(pos, float(w)) - pin[:, lo:lo + GW]
            mixed = jnp.dot(pooled.astype(BF16), pw[g].astype(BF16), preferred_element_type=F32)
            y_ref[:, W + lo:W + lo + GW] = (mixed * ps[:, lo:lo + GW]).astype(BF16)

        ext[0:HB, :] = hcol(2048) * hcol(2560) * keep
        ext[HB:HB + tm, :] = mcol(2048) * mcol(2560)
        cv = sc[0:1, :] * ext[pl.ds(HB - 2, tm), :]
        cv = cv + sc[1:2, :] * ext[pl.ds(HB - 1, tm), :]
        cv = cv + sc[2:3, :] * ext[pl.ds(HB, tm), :]
        y_ref[:, 2 * W:3 * W] = (mcol(1536) * cv).astype(BF16)

        vh, _ = _ln_stats(mcol(3584))
        vn = (vh * gg[...] + gb[...]).astype(BF16)
        u = mcol(3072)
        tri = _tri_mask()
        for g in range(4):
            lo = g * GW
            wm = jnp.where(tri, ws[g], 0.0).astype(BF16)
            for c in range(tm // GW):
                r0 = c * GW
                sg = jnp.dot(wm, vn[r0:r0 + GW, lo:lo + GW], preferred_element_type=F32) + bst[:, g:g + 1]
                y_ref[r0:r0 + GW, 3 * W + lo:3 * W + lo + GW] = (u[r0:r0 + GW, lo:lo + GW] * sg).astype(BF16)

    plist = _mixer_params(sp)
    in_specs = [pl.BlockSpec((tm, MIX_COLS), lambda i: (i, 0)),
                pl.BlockSpec((HB, MIX_COLS), lambda i: (jnp.maximum(i * per - 1, 0), 0))]
    in_specs += _small_specs(plist)
    return pl.pallas_call(
        body, name="f_mixers", grid=(nt,), in_specs=in_specs,
        out_specs=pl.BlockSpec((tm, 4 * W), lambda i: (i, 0)),
        out_shape=jax.ShapeDtypeStruct((T, 4 * W), BF16),
        scratch_shapes=[pltpu.VMEM((HB + tm, W), F32)],
        compiler_params=_cp("arbitrary"),
    )(proj, proj, *plist)


def _assemble_wb(wb8_ref, wbf_ref):
    for k in range(4):
        for j in range(NDEV):
            wbf_ref[k, :, j * GW:(j + 1) * GW] = wb8_ref[j, k]


def _merge_fwd(y, proj, wb8, tm=256):
    nt = T // tm

    def body(y_ref, gate_ref, wb8_ref, z_ref, m_ref, wbf):
        @pl.when(pl.program_id(0) == 0)
        def _():
            _assemble_wb(wb8_ref, wbf)

        m = jnp.zeros((tm, D), F32)
        for k in range(4):
            zk = jnp.dot(y_ref[:, k * W:(k + 1) * W], wbf[k], preferred_element_type=F32)
            z_ref[:, k * D:(k + 1) * D] = zk.astype(BF16)
            m = m + _sig(gate_ref[:, k * D:(k + 1) * D].astype(F32)) * zk
        m_ref[...] = m.astype(BF16)

    return pl.pallas_call(
        body, name="f_merge", grid=(nt,),
        in_specs=[pl.BlockSpec((tm, 4 * W), lambda i: (i, 0)),
                  pl.BlockSpec((tm, 4 * D), lambda i: (i, 1)),
                  pl.BlockSpec(wb8.shape, lambda i: (0, 0, 0, 0))],
        out_specs=[pl.BlockSpec((tm, 4 * D), lambda i: (i, 0)), pl.BlockSpec((tm, D), lambda i: (i, 0))],
        out_shape=[jax.ShapeDtypeStruct((T, 4 * D), BF16), jax.ShapeDtypeStruct((T, D), BF16)],
        scratch_shapes=[pltpu.VMEM((4, W, D), BF16)],
        compiler_params=_cp("arbitrary"),
    )(y, proj, wb8)


def _merge_bwd(dm, z, proj, wb8, tm=256):
    nt = T // tm

    def body(dm_ref, z_ref, gate_ref, wb8_ref, dz_ref, dp_ref, dy_ref, wbf):
        @pl.when(pl.program_id(0) == 0)
        def _():
            _assemble_wb(wb8_ref, wbf)

        dmv = dm_ref[...].astype(F32)
        for k in range(4):
            s = _sig(gate_ref[:, k * D:(k + 1) * D].astype(F32))
            dzk = (dmv * s).astype(BF16)
            dz_ref[:, k * D:(k + 1) * D] = dzk
            dp_ref[:, k * D:(k + 1) * D] = (dmv * z_ref[:, k * D:(k + 1) * D].astype(F32) * s * (1.0 - s)).astype(BF16)
            dyk = lax.dot_general(dzk, wbf[k], (((1,), (1,)), ((), ())), preferred_element_type=F32)
            dy_ref[:, k * W:(k + 1) * W] = dyk.astype(BF16)

    return pl.pallas_call(
        body, name="b_merge", grid=(nt,),
        in_specs=[pl.BlockSpec((tm, D), lambda i: (i, 0)),
                  pl.BlockSpec((tm, 4 * D), lambda i: (i, 0)),
                  pl.BlockSpec((tm, 4 * D), lambda i: (i, 1)),
                  pl.BlockSpec(wb8.shape, lambda i: (0, 0, 0, 0))],
        out_specs=[pl.BlockSpec((tm, 4 * D), lambda i: (i, 0)),
                   pl.BlockSpec((tm, 4 * D), lambda i: (i, 1)),
                   pl.BlockSpec((tm, 4 * W), lambda i: (i, 0))],
        out_shape=[jax.ShapeDtypeStruct((T, 4 * D), BF16),
                   jax.ShapeDtypeStruct((T, COLS_IN), BF16),
                   jax.ShapeDtypeStruct((T, 4 * W), BF16)],
        scratch_shapes=[pltpu.VMEM((4, W, D), BF16)],
        compiler_params=_cp("arbitrary"),
    )(dm, z, proj, wb8)


def _dw_branch(y, dz):
    def body(y_ref, dz_ref, o_ref):
        res = lax.dot_general(y_ref[...], dz_ref[...], (((0,), (0,)), ((), ())), preferred_element_type=F32)
        for j in range(NDEV):
            o_ref[j] = res[:, j * GW:(j + 1) * GW].astype(BF16)

    return pl.pallas_call(
        body, name="b_dw_branch", grid=(4,),
        in_specs=[pl.BlockSpec((T, W), lambda k: (0, k)), pl.BlockSpec((T, D), lambda k: (0, k))],
        out_specs=pl.BlockSpec((NDEV, None, W, GW), lambda k: (0, k, 0, 0)),
        out_shape=jax.ShapeDtypeStruct((NDEV, 4, W, GW), BF16),
        compiler_params=_cp("arbitrary"),
    )(y, dz)


def _mixer_bwd(proj, dy, dproj, sp, tm=256):
    nt = T // tm
    per = tm // HB
    ne = tm + HA
    last_blk = T // HA - 1

    def body(main_ref, hb_ref, ha_ref, dy_ref, dyh_ref, cw, cb, lg, lb, pw, ps, sc, gg, gb, ws, bst, dp_any,
             dp_ref, dcw_ref, dsc_ref, vec_ref, dpw_ref, dws_ref, dbs_ref, e1, e2, e3):
        del dp_any
        i = pl.program_id(0)
        keep_b = (i > 0).astype(F32)
        keep_a = (i < nt - 1).astype(F32)

        @pl.when(i == 0)
        def _():
            dcw_ref[...] = jnp.zeros_like(dcw_ref)
            dsc_ref[...] = jnp.zeros_like(dsc_ref)
            vec_ref[...] = jnp.zeros_like(vec_ref)
            dpw_ref[...] = jnp.zeros_like(dpw_ref)
            dws_ref[...] = jnp.zeros_like(dws_ref)
            dbs_ref[...] = jnp.zeros_like(dbs_ref)

        def mcol(c0):
            return main_ref[:, c0:c0 + W].astype(F32)

        def hbcol(c0):
            return hb_ref[:, c0:c0 + W].astype(F32)

        def hacol(c0):
            return ha_ref[:, c0:c0 + W].astype(F32)

        def load_dy(c0):
            e2[0:tm, :] = dy_ref[:, c0:c0 + W].astype(F32)
            e2[tm:ne, :] = dyh_ref[:, c0:c0 + W].astype(F32) * keep_a

        a = mcol(0)
        sa = _sig(mcol(W))
        e1[0:HB, :] = hbcol(0) * _sig(hbcol(W)) * keep_b
        e1[HB:HB + tm, :] = a * sa
        e1[HB + tm:HB + ne, :] = hacol(0) * _sig(hacol(W))
        ca = jnp.broadcast_to(cb[...], (ne, W))
        for k in range(CONF_K):
            ca = ca + cw[k:k + 1, :] * e1[pl.ds(HB - (CONF_K - 1) + k, ne), :]
        xh, rstd = _ln_stats(ca)
        nn = xh * lg[...] + lb[...]
        s = _sig(nn)
        load_dy(0)
        dn = e2[0:ne, :] * (s * (1.0 + nn * (1.0 - s)))
        vec_ref[1:2, :] += _rowsum(dn[0:tm] * xh[0:tm])
        vec_ref[2:3, :] += _rowsum(dn[0:tm])
        dca = _ln_bwd(dn * lg[...], xh, rstd)
        e3[0:ne, :] = dca
        dmain = dca[0:tm]
        vec_ref[0:1, :] += _rowsum(dmain)
        for k in range(CONF_K):
            dcw_ref[k:k + 1, :] += _rowsum(dmain * e1[pl.ds(HB - (CONF_K - 1) + k, tm), :])
        dglu = cw[0:1, :] * e3[pl.ds(CONF_K - 1, tm), :]
        for k in range(1, CONF_K):
            dglu = dglu + cw[k:k + 1, :] * e3[pl.ds(CONF_K - 1 - k, tm), :]
        dp_ref[:, 0:W] = (dglu * sa).astype(BF16)
        dp_ref[:, W:2 * W] = (dglu * a * sa * (1.0 - sa)).astype(BF16)

        pin = mcol(1024)
        e1[0:HB, :] = hbcol(1024) * keep_b
        e1[HB:HB + tm, :] = pin
        load_dy(W)
        dyb = e2[0:ne, :]
        pos_m = (i * tm + lax.broadcasted_iota(jnp.int32, (tm, 1), 0) + 1).astype(F32)
        pos_e = (i * tm + lax.broadcasted_iota(jnp.int32, (ne, 1), 0) + 1).astype(F32)
        for g, w in enumerate(POOL_WINDOWS):
            lo = g * GW
            acc = e1[pl.ds(HB, tm), lo:lo + GW]
            for j in range(1, w):
                acc = acc + e1[pl.ds(HB - j, tm), lo:lo + GW]
            pooled = (acc / jnp.minimum(pos_m, float(w)) - pin[:, lo:lo + GW]).astype(BF16)
            pwb = pw[g].astype(BF16)
            mixed = jnp.dot(pooled, pwb, preferred_element_type=F32)
            dyb_g = dyb[:, lo:lo + GW]
            vec_ref[3:4, lo:lo + GW] += _rowsum(dyb_g[0:tm] * mixed)
            dmb = (dyb_g * ps[:, lo:lo + GW]).astype(BF16)
            dpw_ref[g] += lax.dot_general(pooled, dmb[0:tm], (((0,), (0,)), ((), ())), preferred_element_type=F32)
            dpool = lax.dot_general(dmb, pwb, (((1,), (1,)), ((), ())), preferred_element_type=F32)
            e3[0:ne, lo:lo + GW] = dpool / jnp.minimum(pos_e, float(w))
            back = e3[pl.ds(0, tm), lo:lo + GW]
            for j in range(1, w):
                back = back + e3[pl.ds(j, tm), lo:lo + GW]
            dp_ref[:, 1024 + lo:1024 + lo + GW] = (back - dpool[0:tm]).astype(BF16)

        cg = mcol(2048)
        hx = mcol(2560)
        e1[0:HB, :] = hbcol(2048) * hbcol(2560) * keep_b
        e1[HB:HB + tm, :] = cg * hx
        load_dy(2 * W)
        dyc = e2[0:tm, :]
        dconv = dyc * mcol(1536)
        e3[0:tm, :] = dconv
        e3[tm:ne, :] = e2[tm:ne, :] * hacol(1536)
        cv = sc[0:1, :] * e1[pl.ds(HB - 2, tm), :]
        for k in range(1, SC_K):
            cv = cv + sc[k:k + 1, :] * e1[pl.ds(HB - 2 + k, tm), :]
        dp_ref[:, 1536:2048] = (dyc * cv).astype(BF16)
        for k in range(SC_K):
            dsc_ref[k:k + 1, :] += _rowsum(dconv * e1[pl.ds(HB - 2 + k, tm), :])
        dq = sc[0:1, :] * e3[pl.ds(2, tm), :]
        for k in range(1, SC_K):
            dq = dq + sc[k:k + 1, :] * e3[pl.ds(2 - k, tm), :]
        dp_ref[:, 2048:2560] = (dq * hx).astype(BF16)
        dp_ref[:, 2560:3072] = (dq * cg).astype(BF16)

        u = mcol(3072)
        vh, vr = _ln_stats(mcol(3584))
        vn = (vh * gg[...] + gb[...]).astype(BF16)
        dyd = dy_ref[:, 3 * W:4 * W].astype(F32)
        tri = _tri_mask()
        for g in range(4):
            lo = g * GW
            wm = jnp.where(tri, ws[g], 0.0).astype(BF16)
            dws_g = jnp.zeros((GW, GW), F32)
            dbs_g = jnp.zeros((GW, 1), F32)
            for c in range(tm // GW):
                r0 = c * GW
                blk = vn[r0:r0 + GW, lo:lo + GW]
                sg = jnp.dot(wm, blk, preferred_element_type=F32) + bst[:, g:g + 1]
                dyd_b = dyd[r0:r0 + GW, lo:lo + GW]
                dp_ref[r0:r0 + GW, 3072 + lo:3072 + lo + GW] = (dyd_b * sg).astype(BF16)
                dsg = dyd_b * u[r0:r0 + GW, lo:lo + GW]
                dsgb = dsg.astype(BF16)
                dbs_g = dbs_g + jnp.sum(dsg, axis=-1, keepdims=True)
                dws_g = dws_g + lax.dot_general(dsgb, blk, (((1,), (1,)), ((), ())), preferred_element_type=F32)
                e1[r0:r0 + GW, lo:lo + GW] = lax.dot_general(wm, dsgb, (((0,), (0,)), ((), ())),
                                                             preferred_element_type=F32)
            dws_ref[g] += jnp.where(tri, dws_g, 0.0)
            dbs_ref[g] += jnp.broadcast_to(dbs_g, (GW, GW))
        dvn = e1[0:tm, :]
        vec_ref[4:5, :] += _rowsum(dvn * vh)
        vec_ref[5:6, :] += _rowsum(dvn)
        dp_ref[:, 3584:4096] = _ln_bwd(dvn * gg[...], vh, vr).astype(BF16)

    plist = _mixer_params(sp)
    in_specs = [pl.BlockSpec((tm, MIX_COLS), lambda i: (i, 0)),
                pl.BlockSpec((HB, MIX_COLS), lambda i: (jnp.maximum(i * per - 1, 0), 0)),
                pl.BlockSpec((HA, MIX_COLS), lambda i: (jnp.minimum((i + 1) * per, last_blk), 0)),
                pl.BlockSpec((tm, 4 * W), lambda i: (i, 0)),
                pl.BlockSpec((HA, 4 * W), lambda i: (jnp.minimum((i + 1) * per, last_blk), 0))]
    in_specs += _small_specs(plist)
    in_specs += [pl.BlockSpec(memory_space=pl.ANY)]
    z2 = lambda i: (0, 0)
    z3 = lambda i: (0, 0, 0)
    out_specs = [pl.BlockSpec((tm, MIX_COLS), lambda i: (i, 0)),
                 pl.BlockSpec((32, W), z2), pl.BlockSpec((8, W), z2), pl.BlockSpec((8, W), z2),
                 pl.BlockSpec((4, GW, GW), z3), pl.BlockSpec((4, GW, GW), z3), pl.BlockSpec((4, GW, GW), z3)]
    out_shape = [jax.ShapeDtypeStruct((T, COLS_IN), BF16),
                 jax.ShapeDtypeStruct((32, W), F32), jax.ShapeDtypeStruct((8, W), F32),
                 jax.ShapeDtypeStruct((8, W), F32),
                 jax.ShapeDtypeStruct((4, GW, GW), F32), jax.ShapeDtypeStruct((4, GW, GW), F32),
                 jax.ShapeDtypeStruct((4, GW, GW), F32)]
    n_in = 5 + len(plist)
    return pl.pallas_call(
        body, name="b_mixers", grid=(nt,), in_specs=in_specs, out_specs=out_specs, out_shape=out_shape,
        scratch_shapes=[pltpu.VMEM((HB + ne, W), F32), pltpu.VMEM((ne, W), F32), pltpu.VMEM((ne, W), F32)],
        input_output_aliases={n_in: 0},
        compiler_params=_cp("arbitrary"),
    )(proj, proj, proj, dy, dy, *plist, dproj)


def _norm_first(x, g, tm=512):
    def body(x_ref, g_ref, o_ref):
        o_ref[...] = _rms(x_ref[...], g_ref[...]).astype(BF16)

    return pl.pallas_call(
        body, name="f_norm0", grid=(T // tm,),
        in_specs=[pl.BlockSpec((tm, D), lambda i: (i, 0)), pl.BlockSpec((1, D), lambda i: (0, 0))],
        out_specs=pl.BlockSpec((tm, D), lambda i: (i, 0)),
        out_shape=jax.ShapeDtypeStruct((T, D), BF16), compiler_params=_cp("arbitrary"),
    )(x, g)


def _loss_head(x, target, g, tm=256):
    def body(x_ref, t_ref, g_ref, dx_ref, dg_ref, loss_ref):
        i = pl.program_id(0)
        x = x_ref[...]
        r = lax.rsqrt(jnp.mean(x * x, axis=-1, keepdims=True) + EPS)
        xh = x * r
        gv = g_ref[...]
        e = xh * gv - t_ref[...]
        dyv = e * (1.0 / D)
        part = jnp.sum(_rowsum(e * e), axis=-1, keepdims=True) * (0.5 / D)
        u = dyv * gv
        dx_ref[...] = r * (u - xh * jnp.mean(u * xh, axis=-1, keepdims=True))
        dgp = _rowsum(dyv * xh)

        @pl.when(i == 0)
        def _():
            dg_ref[...] = dgp
            loss_ref[...] = jnp.broadcast_to(part, (1, GW))

        @pl.when(i > 0)
        def _():
            dg_ref[...] += dgp
            loss_ref[...] += jnp.broadcast_to(part, (1, GW))

    return pl.pallas_call(
        body, name="loss_head", grid=(T // tm,),
        in_specs=[pl.BlockSpec((tm, D), lambda i: (i, 0)), pl.BlockSpec((tm, D), lambda i: (i, 0)),
                  pl.BlockSpec((1, D), lambda i: (0, 0))],
        out_specs=[pl.BlockSpec((tm, D), lambda i: (i, 0)), pl.BlockSpec((1, D), lambda i: (0, 0)),
                   pl.BlockSpec((1, GW), lambda i: (0, 0))],
        out_shape=[jax.ShapeDtypeStruct((T, D), F32), jax.ShapeDtypeStruct((1, D), F32),
                   jax.ShapeDtypeStruct((1, GW), F32)],
        compiler_params=_cp("arbitrary"),
    )(x, target, g)


def _ple_bwd_elem(dx, gl, pe, tm=512):
    def body(dx_ref, gl_ref, pe_ref, dpe_ref, dgl_ref):
        d = dx_ref[...]
        s = _sig(gl_ref[...].astype(F32))
        dpe_ref[...] = (d * s).astype(BF16)
        dgl_ref[...] = (d * pe_ref[...].astype(F32) * s * (1.0 - s)).astype(BF16)

    spec = pl.BlockSpec((tm, D), lambda i: (i, 0))
    return pl.pallas_call(
        body, name="b_ple_elem", grid=(T // tm,), in_specs=[spec, spec, spec], out_specs=[spec, spec],
        out_shape=[jax.ShapeDtypeStruct((T, D), BF16), jax.ShapeDtypeStruct((T, D), BF16)],
        compiler_params=_cp("arbitrary"),
    )(dx, gl, pe)


def _layer_fwd(x, h1, p_bf, gw, sp, g_next):
    proj, = _mm(h1, gw["w_in"], mode="out", name="f_proj", outs=[BF16])
    y = _mixer_fwd(proj, sp)
    z, merged = _merge_fwd(y, proj, gw["w_branch"])
    x2, h2 = _mm(merged, gw["w_out"].reshape(1, D, D), mode="acc", name="f_out", outs=[F32, BF16],
                 tiles=[x], params=[sp["g_mlp"]], epi=_epi_res_norm)
    up, act = _mm(h2, gw["w_up"], mode="out", name="f_up", outs=[BF16, BF16], epi=_epi_relu2)
    x3, h3 = _mm(act, gw["w_down"].reshape(4, D, D), mode="acc", name="f_down", outs=[F32, BF16],
                 tiles=[x2], params=[sp["g_ple"]], epi=_epi_res_norm)
    pe, = _mm(p_bf, gw["w_ple"], mode="out", name="f_pe", outs=[BF16])
    x4, gl, hn = _mm(h3, gw["w_pleg"].reshape(1, D, D), mode="acc", name="f_gate", outs=[F32, BF16, BF16],
                     tiles=[x3, pe], params=[g_next], epi=_epi_ple)
    saved = dict(x=x, h1=h1, proj=proj, y=y, z=z, merged=merged, x2=x2, h2=h2, up=up, act=act, x3=x3, h3=h3,
                 pe=pe, gl=gl, p=p_bf)
    return x4, hn, saved


def _layer_bwd(dx4, sv, gw, sp):
    dpe, dgl = _ple_bwd_elem(dx4, sv["gl"], sv["pe"])
    dw = {}
    dw["w_ple"] = _mm_tn(sv["p"], dpe, nj=NDEV, split="col", name="b_dw_ple")
    dw["w_pleg"] = _mm_tn(sv["h3"], dgl, nj=NDEV, split="row", name="b_dw_pleg")
    dx3, dx3b, dg_ple = _mm(dgl, gw["w_pleg"].reshape(1, D, D), mode="acc", trans_b=True, name="b_dh3",
                            outs=[F32, BF16], tiles=[sv["x3"], dx4], params=[sp["g_ple"]], epi=_epi_rms_bwd, reds=[D])
    dup, = _mm(dx3b, gw["w_down"], mode="out", trans_b=True, name="b_dact", outs=[BF16],
               tiles=[sv["up"]], epi=_epi_dup)
    dw["w_down"] = _mm_tn(sv["act"], dx3b, nj=NDEV, split="row", name="b_dw_down")
    dw["w_up"] = _mm_tn(sv["h2"], dup, nj=NDEV, split="col", name="b_dw_up")
    dx2, dx2b, dg_mlp = _mm(dup, gw["w_up"], mode="acc", trans_b=True, name="b_dh2",
                            outs=[F32, BF16], tiles=[sv["x2"], dx3], params=[sp["g_mlp"]], epi=_epi_rms_bwd, reds=[D])
    dm, = _mm(dx2b, gw["w_out"].reshape(1, D, D), mode="acc", trans_b=True, name="b_dmerged", outs=[BF16])
    dw["w_out"] = _mm_tn(sv["merged"], dx2b, nj=NDEV, split="row", name="b_dw_out")
    dz, dproj, dy = _merge_bwd(dm, sv["z"], sv["proj"], gw["w_branch"])
    dw["w_branch"] = _dw_branch(sv["y"], dz)
    dproj, dcw, dsc, vec, dpw, dws, dbs = _mixer_bwd(sv["proj"], dy, dproj, sp)
    dw["w_in"] = _mm_tn(sv["h1"], dproj, nj=NDEV, split="col", name="b_dw_in")
    dx, _, dg_mix = _mm(dproj, gw["w_in"], mode="acc", trans_b=True, name="b_dh1",
                        outs=[F32, BF16], tiles=[sv["x"], dx2], params=[sp["g_mix"]], epi=_epi_rms_bwd, reds=[D])
    small = dict(norm_mix=dg_mix[0], conf_dw=dcw[:CONF_K], conf_dw_b=vec[0], conf_ln_g=vec[1], conf_ln_b=vec[2],
                 pool_w=dpw, pool_scale=vec[3], sc_conv=dsc[:SC_K], gmlp_ln_g=vec[4], gmlp_ln_b=vec[5],
                 gmlp_ws=dws, gmlp_bs=dbs[:, :, 0], norm_mlp=dg_mlp[0], norm_ple=dg_ple[0])
    return dx, dw, small


ANY = pl.BlockSpec(memory_space=pl.ANY)


def _mesh_pos():
    return lax.axis_index("x"), lax.axis_index("y"), lax.axis_index("c")


def _other_chips(x, y):
    return [(1 - x, y), (x, 1 - y), (1 - x, 1 - y)]


def _all_gather(shards, name):
    n = len(shards)

    def body(*refs):
        s_refs, o_refs = refs[:n], refs[n:2 * n]
        send_sems, recv_sems, local_sems = refs[2 * n:]
        x, y, c = _mesh_pos()
        me = 4 * x + 2 * y + c
        here = (x, y, c)
        sibling = (x, y, 1 - c)
        chips = _other_chips(x, y)

        def slot(px, py, pc):
            return 4 * px + 2 * py + pc

        def copy(t, k, slot_idx, to, src=None):
            dst = o_refs[t].at[slot_idx]
            return pltpu.make_async_remote_copy(
                src_ref=dst if src is None else src, dst_ref=dst,
                send_sem=send_sems.at[t * 7 + k], recv_sem=recv_sems.at[t * 7 + k],
                device_id=to, device_id_type=MESH)

        mine = [pltpu.make_async_copy(s_refs[t], o_refs[t].at[me], local_sems.at[t]) for t in range(n)]
        for cp in mine:
            cp.start()
        first = []
        for t in range(n):
            for j, chip in enumerate(chips):
                first.append(copy(t, 1 + j, me, (*chip, c), src=s_refs[t]))
        for t in range(n):
            first.append(copy(t, 0, me, sibling, src=s_refs[t]))
        for cp in first:
            cp.start()
        passed = []
        for t in range(n):
            for j, chip in enumerate(chips):
                copy(t, 1 + j, slot(*chip, c), here).wait_recv()
                fwd = copy(t, 4 + j, slot(*chip, c), sibling)
                fwd.start()
                passed.append(fwd)
        for t in range(n):
            copy(t, 0, slot(x, y, 1 - c), here).wait_recv()
            for j, chip in enumerate(chips):
                copy(t, 4 + j, slot(*chip, 1 - c), here).wait_recv()
        for cp in first + passed:
            cp.wait_send()
        for cp in mine:
            cp.wait()

    return pl.pallas_call(
        body, name=name, in_specs=[ANY] * n, out_specs=[ANY] * n,
        out_shape=[jax.ShapeDtypeStruct((NDEV,) + s.shape, s.dtype) for s in shards],
        scratch_shapes=[pltpu.SemaphoreType.DMA((7 * n,)), pltpu.SemaphoreType.DMA((7 * n,)),
                        pltpu.SemaphoreType.DMA((n,))],
    )(*shards)


def _rs_pair_exchange(p4s, name):
    n = len(p4s)

    def body(*refs):
        p_refs, o_refs = refs[:n], refs[n:2 * n]
        send_sems, recv_sems = refs[2 * n:]
        x, y, c = _mesh_pos()
        cps = [pltpu.make_async_remote_copy(
            src_ref=p_refs[t].at[:, 1 - c], dst_ref=o_refs[t], send_sem=send_sems.at[t], recv_sem=recv_sems.at[t],
            device_id=(x, y, 1 - c), device_id_type=MESH) for t in range(n)]
        for cp in cps:
            cp.start()
        for cp in cps:
            cp.wait()

    return pl.pallas_call(
        body, name=name, in_specs=[ANY] * n, out_specs=[ANY] * n,
        out_shape=[jax.ShapeDtypeStruct((NCHIP,) + p.shape[2:], p.dtype) for p in p4s],
        scratch_shapes=[pltpu.SemaphoreType.DMA((n,)), pltpu.SemaphoreType.DMA((n,))],
    )(*p4s)


def _rs_chip_exchange(qs, name):
    n = len(qs)

    def body(*refs):
        q_refs, o_refs = refs[:n], refs[n:2 * n]
        send_sems, recv_sems, local_sems = refs[2 * n:]
        x, y, c = _mesh_pos()
        a_idx = 2 * x + y
        chips = _other_chips(x, y)
        mine = [pltpu.make_async_copy(q_refs[t].at[a_idx], o_refs[t].at[a_idx], local_sems.at[t]) for t in range(n)]
        for cp in mine:
            cp.start()
        sends = []
        for t in range(n):
            for j, chip in enumerate(chips):
                b_idx = 2 * chip[0] + chip[1]
                sends.append(pltpu.make_async_remote_copy(
                    src_ref=q_refs[t].at[b_idx], dst_ref=o_refs[t].at[a_idx],
                    send_sem=send_sems.at[t * 3 + j], recv_sem=recv_sems.at[t * 3 + j],
                    device_id=(*chip, c), device_id_type=MESH))
        for cp in sends:
            cp.start()
        for t in range(n):
            for j, chip in enumerate(chips):
                b_idx = 2 * chip[0] + chip[1]
                pltpu.make_async_remote_copy(
                    src_ref=q_refs[t].at[b_idx], dst_ref=o_refs[t].at[b_idx],
                    send_sem=send_sems.at[t * 3 + j], recv_sem=recv_sems.at[t * 3 + j],
                    device_id=(x, y, c), device_id_type=MESH).wait_recv()
        for cp in sends:
            cp.wait_send()
        for cp in mine:
            cp.wait()

    return pl.pallas_call(
        body, name=name, in_specs=[ANY] * n, out_specs=[ANY] * n,
        out_shape=[jax.ShapeDtypeStruct(q.shape, q.dtype) for q in qs],
        scratch_shapes=[pltpu.SemaphoreType.DMA((3 * n,)), pltpu.SemaphoreType.DMA((3 * n,)),
                        pltpu.SemaphoreType.DMA((n,))],
    )(*qs)


def _pair_sum(p4s, rbs, c_idx, name, nst=4):
    n = len(p4s)
    trs = [p.shape[2] // nst for p in p4s]

    def body(c_ref, *refs):
        del c_ref
        p_refs, r_refs, o_refs = refs[:n], refs[n:2 * n], refs[2 * n:]
        for p_ref, r_ref, o_ref in zip(p_refs, r_refs, o_refs):
            o_ref[...] = (p_ref[...].astype(F32) + r_ref[...].astype(F32)).astype(o_ref.dtype)

    in_specs = [pl.BlockSpec((None, None, tr, p.shape[3]), lambda b, i, c_ref: (b, c_ref[0], i, 0))
                for p, tr in zip(p4s, trs)]
    in_specs += [pl.BlockSpec((None, tr, p.shape[3]), lambda b, i, c_ref: (b, i, 0)) for p, tr in zip(p4s, trs)]
    out_specs = [pl.BlockSpec((None, tr, p.shape[3]), lambda b, i, c_ref: (b, i, 0)) for p, tr in zip(p4s, trs)]
    return pl.pallas_call(
        body, name=name,
        grid_spec=pltpu.PrefetchScalarGridSpec(num_scalar_prefetch=1, grid=(NCHIP, nst), in_specs=in_specs,
                                               out_specs=out_specs),
        out_shape=[jax.ShapeDtypeStruct((NCHIP,) + p.shape[2:], p.dtype) for p in p4s],
        compiler_params=_cp("arbitrary", "arbitrary"),
    )(c_idx, *p4s, *rbs)


def _reduce_scatter(ps, c_idx, tag, nst=4):
    p4s = [p.reshape((NCHIP, 2) + p.shape[1:]) for p in ps]
    rbs = _rs_pair_exchange(p4s, name="rs_pair_" + tag)
    qs = _pair_sum(p4s, rbs, c_idx, name="rs_pairsum_" + tag, nst=nst)
    return _rs_chip_exchange(qs, name="rs_chip_" + tag)


def _adamw(w, g, m, v):
    m = ADAM_B1 * m + (1.0 - ADAM_B1) * g
    v = ADAM_B2 * v + (1.0 - ADAM_B2) * (g * g)
    m_hat = m / (1.0 - ADAM_B1 ** ADAM_STEP)
    v_hat = v / (1.0 - ADAM_B2 ** ADAM_STEP)
    delta = -ADAM_LR * (m_hat / (jnp.sqrt(v_hat) + ADAM_EPS) + ADAM_WD * w)
    return delta, m, v


def _adam_sharded(rcs, w, m, v, tr, name):
    _, r, c = w.shape
    nst = r // tr

    def body(rc0, rc1, rc2, rc3, w_ref, m_ref, v_ref, g_out, d_out, m_out, v_out):
        layer = pl.program_id(0)
        for k, rc in enumerate((rc0, rc1, rc2, rc3)):
            @pl.when(layer == k)
            def _():
                g = rc[0].astype(F32) + rc[1].astype(F32) + rc[2].astype(F32) + rc[3].astype(F32)
                delta, m_new, v_new = _adamw(w_ref[...], g, m_ref[...], v_ref[...])
                g_out[...] = g
                d_out[...] = delta
                m_out[...] = m_new
                v_out[...] = v_new

    rc_specs = [pl.BlockSpec((NCHIP, tr, c), lambda l, i, k=k: (0, jnp.where(l == k, i, 0), 0)) for k in range(DEPTH)]
    wspec = pl.BlockSpec((None, tr, c), lambda l, i: (l, i, 0))
    return pl.pallas_call(
        body, name=name, grid=(DEPTH, nst), in_specs=rc_specs + [wspec] * 3, out_specs=[wspec] * 4,
        out_shape=[jax.ShapeDtypeStruct(w.shape, F32)] * 4,
        compiler_params=_cp("arbitrary", "arbitrary"),
    )(*rcs, w, m, v)


def _adam_packed(g, w, m, v, tr=184):
    rows = g.shape[0]

    def body(g_ref, w_ref, m_ref, v_ref, d_out, m_out, v_out):
        delta, m_new, v_new = _adamw(w_ref[...], g_ref[...], m_ref[...], v_ref[...])
        d_out[...] = delta
        m_out[...] = m_new
        v_out[...] = v_new

    spec = pl.BlockSpec((tr, D), lambda i: (i, 0))
    return pl.pallas_call(
        body, name="adam_small", grid=(rows // tr,), in_specs=[spec] * 4, out_specs=[spec] * 3,
        out_shape=[jax.ShapeDtypeStruct(g.shape, F32)] * 3, compiler_params=_cp("arbitrary"),
    )(g, w, m, v)


def _sum4(rc):
    def body(rc_ref, o_ref):
        o_ref[...] = rc_ref[0] + rc_ref[1] + rc_ref[2] + rc_ref[3]

    return pl.pallas_call(
        body, name="small_sum", out_shape=jax.ShapeDtypeStruct(rc.shape[1:], F32),
    )(rc)


BIG = ("w_in", "w_branch", "w_out", "w_up", "w_down", "w_ple", "w_pleg")
BIG_SHARD = {"w_in": (D, D), "w_branch": (4 * W, GW), "w_out": (GW, D), "w_up": (D, W), "w_down": (W, D),
             "w_ple": (256, GW), "w_pleg": (GW, D)}
ADAM_ROWS = {"w_in": 256, "w_branch": 512, "w_out": 128, "w_up": 256, "w_down": 256, "w_ple": 256, "w_pleg": 128}
SMALL = (("norm_mix", (DEPTH, D)), ("conf_dw", (DEPTH, CONF_K, W)), ("conf_dw_b", (DEPTH, W)),
         ("conf_ln_g", (DEPTH, W)), ("conf_ln_b", (DEPTH, W)), ("pool_w", (DEPTH, 4, GW, GW)),
         ("pool_scale", (DEPTH, W)), ("sc_conv", (DEPTH, SC_K, W)), ("gmlp_ln_g", (DEPTH, W)),
         ("gmlp_ln_b", (DEPTH, W)), ("gmlp_ws", (DEPTH, 4, GW, GW)), ("gmlp_bs", (DEPTH, 4, GW)),
         ("norm_mlp", (DEPTH, D)), ("norm_ple", (DEPTH, D)), ("norm_final", (D,)))
CHANNEL_SHARDED = ("conf_dw", "sc_conv")
SMALL_ROWS = 80
PACK_ROWS = 552


def _pack(arrs, rows):
    flat = jnp.concatenate([a.reshape(-1) for a in arrs])
    return jnp.pad(flat, (0, rows * D - flat.shape[0])).reshape(rows, D)


def _unpack(packed, shapes):
    flat = packed.reshape(-1)
    out, off = [], 0
    for shp in shapes:
        size = 1
        for s in shp:
            size *= s
        out.append(flat[off:off + size].reshape(shp))
        off += size
    return out


def kernel(x, p, norm_mix, w_in, conf_dw, conf_dw_b, conf_ln_g, conf_ln_b, pool_w, pool_scale, sc_conv, gmlp_ln_g, gmlp_ln_b, gmlp_ws, gmlp_bs, w_branch, w_out, norm_mlp, w_up, w_down, norm_ple, w_ple, w_ple_gate, norm_final, loss_target, m_norm_mix, m_w_in, m_conf_dw, m_conf_dw_b, m_conf_ln_g, m_conf_ln_b, m_pool_w, m_pool_scale, m_sc_conv, m_gmlp_ln_g, m_gmlp_ln_b, m_gmlp_ws, m_gmlp_bs, m_w_branch, m_w_out, m_norm_mlp, m_w_up, m_w_down, m_norm_ple, m_w_ple, m_w_ple_gate, m_norm_final, v_norm_mix, v_w_in, v_conf_dw, v_conf_dw_b, v_conf_ln_g, v_conf_ln_b, v_pool_w, v_pool_scale, v_sc_conv, v_gmlp_ln_g, v_gmlp_ln_b, v_gmlp_ws, v_gmlp_bs, v_w_branch, v_w_out, v_norm_mlp, v_w_up, v_w_down, v_norm_ple, v_w_ple, v_w_ple_gate, v_norm_final):
    weights = dict(norm_mix=norm_mix, w_in=w_in, conf_dw=conf_dw, conf_dw_b=conf_dw_b, conf_ln_g=conf_ln_g,
                   conf_ln_b=conf_ln_b, pool_w=pool_w, pool_scale=pool_scale, sc_conv=sc_conv, gmlp_ln_g=gmlp_ln_g,
                   gmlp_ln_b=gmlp_ln_b, gmlp_ws=gmlp_ws, gmlp_bs=gmlp_bs, w_branch=w_branch, w_out=w_out,
                   norm_mlp=norm_mlp, w_up=w_up, w_down=w_down, norm_ple=norm_ple, w_ple=w_ple, w_pleg=w_ple_gate,
                   norm_final=norm_final)
    mom1 = dict(norm_mix=m_norm_mix, w_in=m_w_in, conf_dw=m_conf_dw, conf_dw_b=m_conf_dw_b, conf_ln_g=m_conf_ln_g,
                conf_ln_b=m_conf_ln_b, pool_w=m_pool_w, pool_scale=m_pool_scale, sc_conv=m_sc_conv,
                gmlp_ln_g=m_gmlp_ln_g, gmlp_ln_b=m_gmlp_ln_b, gmlp_ws=m_gmlp_ws, gmlp_bs=m_gmlp_bs,
                w_branch=m_w_branch, w_out=m_w_out, norm_mlp=m_norm_mlp, w_up=m_w_up, w_down=m_w_down,
                norm_ple=m_norm_ple, w_ple=m_w_ple, w_pleg=m_w_ple_gate, norm_final=m_norm_final)
    mom2 = dict(norm_mix=v_norm_mix, w_in=v_w_in, conf_dw=v_conf_dw, conf_dw_b=v_conf_dw_b, conf_ln_g=v_conf_ln_g,
                conf_ln_b=v_conf_ln_b, pool_w=v_pool_w, pool_scale=v_pool_scale, sc_conv=v_sc_conv,
                gmlp_ln_g=v_gmlp_ln_g, gmlp_ln_b=v_gmlp_ln_b, gmlp_ws=v_gmlp_ws, gmlp_bs=v_gmlp_bs,
                w_branch=v_w_branch, w_out=v_w_out, norm_mlp=v_norm_mlp, w_up=v_w_up, w_down=v_w_down,
                norm_ple=v_norm_ple, w_ple=v_w_ple, w_pleg=v_w_ple_gate, norm_final=v_norm_final)

    xi, yi, ci = _mesh_pos()
    me = 4 * xi + 2 * yi + ci
    c_idx = jnp.reshape(ci, (1,)).astype(jnp.int32)

    conv_g = _all_gather([conf_dw, sc_conv], name="ag_conv")
    conf_full = conv_g[0].transpose(1, 2, 0, 3).reshape(DEPTH, CONF_K, W)
    sc_full = conv_g[1].transpose(1, 2, 0, 3).reshape(DEPTH, SC_K, W)
    gathered = []
    for l in range(DEPTH):
        shards = [weights[n][l].astype(BF16).reshape(BIG_SHARD[n]) for n in BIG]
        got = _all_gather(shards, name="ag_weights")
        gw = dict(zip(BIG, got))
        gw["w_branch"] = gw["w_branch"].reshape(NDEV, 4, W, GW)
        gathered.append(gw)

    def small_params(l):
        return dict(cw=conf_full[l], cb=conf_dw_b[l][None], lg=conf_ln_g[l][None], lb=conf_ln_b[l][None],
                    pw=pool_w[l], ps=pool_scale[l][None], sc=sc_full[l], gg=gmlp_ln_g[l][None],
                    gb=gmlp_ln_b[l][None], ws=gmlp_ws[l], bst=gmlp_bs[l].T, g_mix=norm_mix[l][None],
                    g_mlp=norm_mlp[l][None], g_ple=norm_ple[l][None])

    xc = x.reshape(T, D)
    p_bf = p.reshape(DEPTH, T, 256).astype(BF16)
    h = _norm_first(xc, norm_mix[0][None])
    saved = []
    for l in range(DEPTH):
        g_next = norm_mix[l + 1][None] if l + 1 < DEPTH else norm_final[None]
        xc, h, sv = _layer_fwd(xc, h, p_bf[l], gathered[l], small_params(l), g_next)
        saved.append(sv)

    dxc, dg_final, loss_part = _loss_head(xc, loss_target.reshape(T, D), norm_final[None])
    loss = lax.psum(loss_part[0, 0], ("x", "y", "c"))
    small_grads = [None] * DEPTH
    rcs = {n: [None] * DEPTH for n in BIG}
    for l in reversed(range(DEPTH)):
        dxc, dw, small_grads[l] = _layer_bwd(dxc, saved[l], gathered[l], small_params(l))
        parts = [dw[n].reshape((NDEV,) + BIG_SHARD[n]) for n in BIG]
        for n, rc in zip(BIG, _reduce_scatter(parts, c_idx, "w")):
            rcs[n][l] = rc

    stacked = {n: jnp.stack([small_grads[l][n] for l in range(DEPTH)]) for n, _ in SMALL if n != "norm_final"}
    stacked["norm_final"] = dg_final[0]
    packed = _pack([stacked[n] for n, _ in SMALL], NDEV * SMALL_ROWS).reshape(NDEV, SMALL_ROWS, D)
    reduced_slot = _sum4(_reduce_scatter([packed], c_idx, "small", nst=1)[0])
    reduced = _all_gather([reduced_slot], name="ag_small")[0]
    small_full = dict(zip([n for n, _ in SMALL], _unpack(reduced, [s for _, s in SMALL])))

    grads, deltas, new_m, new_v = {}, {}, {}, {}
    for n in BIG:
        shp = (DEPTH,) + BIG_SHARD[n]
        g_, d_, m_, v_ = _adam_sharded(rcs[n], weights[n].reshape(shp), mom1[n].reshape(shp), mom2[n].reshape(shp),
                                       ADAM_ROWS[n], name="adam_" + n)
        full = weights[n].shape
        grads[n], deltas[n], new_m[n], new_v[n] = g_.reshape(full), d_.reshape(full), m_.reshape(full), v_.reshape(full)
    small_g = {}
    for n, _ in SMALL:
        if n in CHANNEL_SHARDED:
            small_g[n] = lax.dynamic_slice_in_dim(small_full[n], me * (W // NDEV), W // NDEV, axis=2)
        else:
            small_g[n] = small_full[n]
    names = [n for n, _ in SMALL]
    shapes = [weights[n].shape for n in names]
    d_p, m_p, v_p = _adam_packed(_pack([small_g[n] for n in names], PACK_ROWS),
                                 _pack([weights[n] for n in names], PACK_ROWS),
                                 _pack([mom1[n] for n in names], PACK_ROWS),
                                 _pack([mom2[n] for n in names], PACK_ROWS))
    for n, d_, m_, v_ in zip(names, _unpack(d_p, shapes), _unpack(m_p, shapes), _unpack(v_p, shapes)):
        grads[n], deltas[n], new_m[n], new_v[n] = small_g[n], d_, m_, v_

    order = ("norm_mix", "w_in", "conf_dw", "conf_dw_b", "conf_ln_g", "conf_ln_b", "pool_w", "pool_scale", "sc_conv",
             "gmlp_ln_g", "gmlp_ln_b", "gmlp_ws", "gmlp_bs", "w_branch", "w_out", "norm_mlp", "w_up", "w_down",
             "norm_ple", "w_ple", "w_pleg", "norm_final")
    return (loss, dxc.reshape(1, T, D), *[grads[n] for n in order], *[deltas[n] for n in order],
            *[new_m[n] for n in order], *[new_v[n] for n in order])
```

```python
import functools

import jax
import jax.numpy as jnp
from jax import lax
from jax.experimental import pallas as pl
from jax.experimental.pallas import tpu as pltpu
from jax.experimental.pallas import tpu_sc as plsc

F32 = jnp.float32
BF16 = jnp.bfloat16

DEPTH = 4
T = 2048
D = 1024
W = 512
NDEV = 8
NCHIP = 4
EPS = 1e-6
CONF_K = 31
SC_K = 3
POOL_WINDOWS = (2, 4, 8, 16)
GW = 128
HB = 32
HA = 32
COLS_IN = 8192
MIX_COLS = 4096

ADAM_LR = 0.001
ADAM_B1 = 0.9
ADAM_B2 = 0.999
ADAM_EPS = 1e-08
ADAM_WD = 0.01
ADAM_STEP = 10

VMEM_LIMIT_BYTES = 56 * 1024 * 1024
MESH = pl.DeviceIdType.MESH


def _cp(*sem):
    return pltpu.CompilerParams(dimension_semantics=tuple(sem), vmem_limit_bytes=VMEM_LIMIT_BYTES)


def _sig(x):
    return jax.nn.sigmoid(x)


def _rms(x, g):
    r = lax.rsqrt(jnp.mean(x * x, axis=-1, keepdims=True) + EPS)
    return x * r * g


def _rms_bwd(dh, x, g, dres):
    r = lax.rsqrt(jnp.mean(x * x, axis=-1, keepdims=True) + EPS)
    xh = x * r
    u = dh * g
    dx = r * (u - xh * jnp.mean(u * xh, axis=-1, keepdims=True)) + dres
    dg = jnp.sum(dh * xh, axis=0, keepdims=True)
    return dx, dg


def _ln_stats(x):
    mu = jnp.mean(x, axis=-1, keepdims=True)
    xc = x - mu
    rstd = lax.rsqrt(jnp.mean(xc * xc, axis=-1, keepdims=True) + EPS)
    return xc * rstd, rstd


def _ln_bwd(dxh, xh, rstd):
    return rstd * (dxh - jnp.mean(dxh, axis=-1, keepdims=True) - xh * jnp.mean(dxh * xh, axis=-1, keepdims=True))


def _rowsum(x):
    return jnp.sum(x, axis=0, keepdims=True)


def _mm(a, b3, *, mode, name, outs, trans_b=False, tm=512, tiles=(), params=(), epi=None, reds=()):
    t_, ka = a.shape
    nj, r, c = b3.shape
    kb, nb = (c, r) if trans_b else (r, c)
    nt = t_ // tm
    out_mode = mode == "out"
    if out_mode:
        assert ka == kb and not reds
        grid = (nj, nt)
        a_map = lambda g0, g1: (g1, 0)
        b_map = lambda g0, g1: (g0, 0, 0)
        t_map = lambda g0, g1: (g1, g0)
        width = nj * nb
    else:
        assert ka == nj * kb
        grid = (nt, nj)
        a_map = lambda g0, g1: (g0, g1)
        b_map = lambda g0, g1: (g1, 0, 0)
        t_map = lambda g0, g1: (g0, 0)
        width = nb
    n_t, n_p, n_o, n_r = len(tiles), len(params), len(outs), len(reds)
    use_acc = (not out_mode) and nj > 1
    dims = (((1,), (1,)), ((), ())) if trans_b else (((1,), (0,)), ((), ()))

    def body(a_ref, b_ref, *rest):
        t_refs = rest[:n_t]
        p_refs = rest[n_t:n_t + n_p]
        o_refs = rest[n_t + n_p:n_t + n_p + n_o]
        r_refs = rest[n_t + n_p + n_o:n_t + n_p + n_o + n_r]
        part = lax.dot_general(a_ref[...], b_ref[...], dims, preferred_element_type=F32)
        i = pl.program_id(1 if out_mode else 0)

        def finish(acc):
            if epi is None:
                res, rr = (acc,), ()
            else:
                res, rr = epi(acc, [t[...] for t in t_refs], [p[...] for p in p_refs])
            for o_ref, val in zip(o_refs, res):
                o_ref[...] = val.astype(o_ref.dtype)
            for r_ref, val in zip(r_refs, rr):
                @pl.when(i == 0)
                def _():
                    r_ref[...] = val

                @pl.when(i > 0)
                def _():
                    r_ref[...] += val

        if use_acc:
            acc_ref = rest[-1]
            j = pl.program_id(1)

            @pl.when(j == 0)
            def _():
                acc_ref[...] = part

            @pl.when(j > 0)
            def _():
                acc_ref[...] += part

            @pl.when(j == nj - 1)
            def _():
                finish(acc_ref[...])
        else:
            finish(part)

    const2 = lambda g0, g1: (0, 0)
    in_specs = [pl.BlockSpec((tm, kb), a_map), pl.BlockSpec((None, r, c), b_map)]
    in_specs += [pl.BlockSpec((tm, t.shape[1] // nj if out_mode else t.shape[1]), t_map) for t in tiles]
    in_specs += [pl.BlockSpec(p.shape, lambda g0, g1, nd=p.ndim: (0,) * nd) for p in params]
    out_specs = [pl.BlockSpec((tm, nb), t_map) for _ in outs] + [pl.BlockSpec((1, w), const2) for w in reds]
    out_shape = [jax.ShapeDtypeStruct((t_, width), dt) for dt in outs]
    out_shape += [jax.ShapeDtypeStruct((1, w), F32) for w in reds]
    res = pl.pallas_call(
        body, name=name, grid=grid, in_specs=in_specs, out_specs=out_specs, out_shape=out_shape,
        scratch_shapes=[pltpu.VMEM((tm, nb), F32)] if use_acc else [],
        compiler_params=_cp("arbitrary", "arbitrary"),
    )(a, b3, *tiles, *params)
    return res


def _mm_tn(a, g, *, nj, split, name, out_dtype=BF16):
    t_ = a.shape[0]
    if split == "col":
        r, c = a.shape[1], g.shape[1] // nj
        a_spec = pl.BlockSpec((t_, r), lambda j: (0, 0))
        g_spec = pl.BlockSpec((t_, c), lambda j: (0, j))
    else:
        r, c = a.shape[1] // nj, g.shape[1]
        a_spec = pl.BlockSpec((t_, r), lambda j: (0, j))
        g_spec = pl.BlockSpec((t_, c), lambda j: (0, 0))

    def body(a_ref, g_ref, o_ref):
        o_ref[...] = lax.dot_general(a_ref[...], g_ref[...], (((0,), (0,)), ((), ())),
                                     preferred_element_type=F32).astype(o_ref.dtype)

    return pl.pallas_call(
        body, name=name, grid=(nj,), in_specs=[a_spec, g_spec],
        out_specs=pl.BlockSpec((None, r, c), lambda j: (j, 0, 0)),
        out_shape=jax.ShapeDtypeStruct((nj, r, c), out_dtype),
        compiler_params=_cp("arbitrary"),
    )(a, g)


def _epi_res_norm(acc, tiles, params):
    x_new = tiles[0] + acc
    return (x_new, _rms(x_new, params[0])), ()


def _epi_relu2(acc, tiles, params):
    r = jnp.maximum(acc, 0.0)
    return (acc, r * r), ()


def _epi_ple(acc, tiles, params):
    x_old, p_tile = tiles
    g_next, w_ple8 = params
    pe = jnp.concatenate([jnp.dot(p_tile, w_ple8[j], preferred_element_type=F32) for j in range(NDEV)], axis=1)
    x_new = x_old + pe * _sig(acc)
    return (x_new, acc, _rms(x_new, g_next), pe), ()


def _epi_rms_bwd(acc, tiles, params):
    dx, dg = _rms_bwd(acc, tiles[0], params[0], tiles[1])
    return (dx, dx), (dg,)


def _epi_dup(acc, tiles, params):
    return (acc * (2.0 * jnp.maximum(tiles[0].astype(F32), 0.0)),), ()


def _tri_mask():
    row = lax.broadcasted_iota(jnp.int32, (GW, GW), 0)
    col = lax.broadcasted_iota(jnp.int32, (GW, GW), 1)
    return row >= col


def _small_specs(sp_list):
    return [pl.BlockSpec(p.shape, (lambda i: (0, 0)) if p.ndim == 2 else (lambda i: (0, 0, 0))) for p in sp_list]


def _mixer_params(sp):
    return [sp["cw"], sp["cb"], sp["lg"], sp["lb"], sp["pw"], sp["ps"], sp["sc"], sp["gg"], sp["gb"], sp["ws"], sp["bst"]]


def _mixer_fwd(proj, sp, tm=256):
    nt = T // tm
    per = tm // HB

    def body(main_ref, halo_ref, cw, cb, lg, lb, pw, ps, sc, gg, gb, ws, bst, y_ref, ext):
        i = pl.program_id(0)
        keep = (i > 0).astype(F32)

        def mcol(c0):
            return main_ref[:, c0:c0 + W].astype(F32)

        def hcol(c0):
            return halo_ref[:, c0:c0 + W].astype(F32)

        ext[0:HB, :] = hcol(0) * _sig(hcol(W)) * keep
        ext[HB:HB + tm, :] = mcol(0) * _sig(mcol(W))
        ca = jnp.broadcast_to(cb[...], (tm, W))
        for k in range(CONF_K):
            ca = ca + cw[k:k + 1, :] * ext[pl.ds(HB - (CONF_K - 1) + k, tm), :]
        xh, _ = _ln_stats(ca)
        n = xh * lg[...] + lb[...]
        y_ref[:, 0:W] = (n * _sig(n)).astype(BF16)

        pin = mcol(1024)
        ext[0:HB, :] = hcol(1024) * keep
        ext[HB:HB + tm, :] = pin
        pos = (i * tm + lax.broadcasted_iota(jnp.int32, (tm, 1), 0) + 1).astype(F32)
        for g, w in enumerate(POOL_WINDOWS):
            lo = g * GW
            s = ext[pl.ds(HB, tm), lo:lo + GW]
            for j in range(1, w):
                s = s + ext[pl.ds(HB - j, tm), lo:lo + GW]
            pooled = s / jnp.minimum(pos, float(w)) - pin[:, lo:lo + GW]
            mixed = jnp.dot(pooled.astype(BF16), pw[g].astype(BF16), preferred_element_type=F32)
            y_ref[:, W + lo:W + lo + GW] = (mixed * ps[:, lo:lo + GW]).astype(BF16)

        ext[0:HB, :] = hcol(2048) * hcol(2560) * keep
        ext[HB:HB + tm, :] = mcol(2048) * mcol(2560)
        cv = sc[0:1, :] * ext[pl.ds(HB - 2, tm), :]
        cv = cv + sc[1:2, :] * ext[pl.ds(HB - 1, tm), :]
        cv = cv + sc[2:3, :] * ext[pl.ds(HB, tm), :]
        y_ref[:, 2 * W:3 * W] = (mcol(1536) * cv).astype(BF16)

        vh, _ = _ln_stats(mcol(3584))
        vn = (vh * gg[...] + gb[...]).astype(BF16)
        u = mcol(3072)
        tri = _tri_mask()
        for g in range(4):
            lo = g * GW
            wm = jnp.where(tri, ws[g], 0.0).astype(BF16)
            for c in range(tm // GW):
                r0 = c * GW
                sg = jnp.dot(wm, vn[r0:r0 + GW, lo:lo + GW], preferred_element_type=F32) + bst[:, g:g + 1]
                y_ref[r0:r0 + GW, 3 * W + lo:3 * W + lo + GW] = (u[r0:r0 + GW, lo:lo + GW] * sg).astype(BF16)

    plist = _mixer_params(sp)
    in_specs = [pl.BlockSpec((tm, MIX_COLS), lambda i: (i, 0)),
                pl.BlockSpec((HB, MIX_COLS), lambda i: (jnp.maximum(i * per - 1, 0), 0))]
    in_specs += _small_specs(plist)
    return pl.pallas_call(
        body, name="f_mixers", grid=(nt,), in_specs=in_specs,
        out_specs=pl.BlockSpec((tm, 4 * W), lambda i: (i, 0)),
        out_shape=jax.ShapeDtypeStruct((T, 4 * W), BF16),
        scratch_shapes=[pltpu.VMEM((HB + tm, W), F32)],
        compiler_params=_cp("arbitrary"),
    )(proj, proj, *plist)


def _assemble_wb(wb8_ref, wbf_ref):
    for k in range(4):
        for j in range(NDEV):
            wbf_ref[k, :, j * GW:(j + 1) * GW] = wb8_ref[j, k]


def _merge_fwd(y, proj, wb8, tm=256):
    nt = T // tm

    def body(y_ref, gate_ref, wb8_ref, z_ref, m_ref, wbf):
        @pl.when(pl.program_id(0) == 0)
        def _():
            _assemble_wb(wb8_ref, wbf)

        m = jnp.zeros((tm, D), F32)
        for k in range(4):
            zk = jnp.dot(y_ref[:, k * W:(k + 1) * W], wbf[k], preferred_element_type=F32)
            z_ref[:, k * D:(k + 1) * D] = zk.astype(BF16)
            m = m + _sig(gate_ref[:, k * D:(k + 1) * D].astype(F32)) * zk
        m_ref[...] = m.astype(BF16)

    return pl.pallas_call(
        body, name="f_merge", grid=(nt,),
        in_specs=[pl.BlockSpec((tm, 4 * W), lambda i: (i, 0)),
                  pl.BlockSpec((tm, 4 * D), lambda i: (i, 1)),
                  pl.BlockSpec(wb8.shape, lambda i: (0, 0, 0, 0))],
        out_specs=[pl.BlockSpec((tm, 4 * D), lambda i: (i, 0)), pl.BlockSpec((tm, D), lambda i: (i, 0))],
        out_shape=[jax.ShapeDtypeStruct((T, 4 * D), BF16), jax.ShapeDtypeStruct((T, D), BF16)],
        scratch_shapes=[pltpu.VMEM((4, W, D), BF16)],
        compiler_params=_cp("arbitrary"),
    )(y, proj, wb8)


def _merge_bwd(dm, z, proj, wb8, tm=256):
    nt = T // tm

    def body(dm_ref, z_ref, gate_ref, wb8_ref, dz_ref, dp_ref, dy_ref, wbf):
        @pl.when(pl.program_id(0) == 0)
        def _():
            _assemble_wb(wb8_ref, wbf)

        dmv = dm_ref[...].astype(F32)
        for k in range(4):
            s = _sig(gate_ref[:, k * D:(k + 1) * D].astype(F32))
            dzk = (dmv * s).astype(BF16)
            dz_ref[:, k * D:(k + 1) * D] = dzk
            dp_ref[:, k * D:(k + 1) * D] = (dmv * z_ref[:, k * D:(k + 1) * D].astype(F32) * s * (1.0 - s)).astype(BF16)
            dyk = lax.dot_general(dzk, wbf[k], (((1,), (1,)), ((), ())), preferred_element_type=F32)
            dy_ref[:, k * W:(k + 1) * W] = dyk.astype(BF16)

    return pl.pallas_call(
        body, name="b_merge", grid=(nt,),
        in_specs=[pl.BlockSpec((tm, D), lambda i: (i, 0)),
                  pl.BlockSpec((tm, 4 * D), lambda i: (i, 0)),
                  pl.BlockSpec((tm, 4 * D), lambda i: (i, 1)),
                  pl.BlockSpec(wb8.shape, lambda i: (0, 0, 0, 0))],
        out_specs=[pl.BlockSpec((tm, 4 * D), lambda i: (i, 0)),
                   pl.BlockSpec((tm, 4 * D), lambda i: (i, 1)),
                   pl.BlockSpec((tm, 4 * W), lambda i: (i, 0))],
        out_shape=[jax.ShapeDtypeStruct((T, 4 * D), BF16),
                   jax.ShapeDtypeStruct((T, COLS_IN), BF16),
                   jax.ShapeDtypeStruct((T, 4 * W), BF16)],
        scratch_shapes=[pltpu.VMEM((4, W, D), BF16)],
        compiler_params=_cp("arbitrary"),
    )(dm, z, proj, wb8)


def _dw_branch(y, dz):
    def body(y_ref, dz_ref, o_ref):
        res = lax.dot_general(y_ref[...], dz_ref[...], (((0,), (0,)), ((), ())), preferred_element_type=F32)
        for j in range(NDEV):
            o_ref[j] = res[:, j * GW:(j + 1) * GW].astype(BF16)

    return pl.pallas_call(
        body, name="b_dw_branch", grid=(4,),
        in_specs=[pl.BlockSpec((T, W), lambda k: (0, k)), pl.BlockSpec((T, D), lambda k: (0, k))],
        out_specs=pl.BlockSpec((NDEV, None, W, GW), lambda k: (0, k, 0, 0)),
        out_shape=jax.ShapeDtypeStruct((NDEV, 4, W, GW), BF16),
        compiler_params=_cp("arbitrary"),
    )(y, dz)


def _mixer_bwd(proj, dy, dproj, sp, tm=256):
    nt = T // tm
    per = tm // HB
    ne = tm + HA
    last_blk = T // HA - 1

    def body(main_ref, hb_ref, ha_ref, dy_ref, dyh_ref, cw, cb, lg, lb, pw, ps, sc, gg, gb, ws, bst, dp_any,
             dp_ref, dcw_ref, dsc_ref, vec_ref, dpw_ref, dws_ref, dbs_ref, e1, e2, e3):
        del dp_any
        i = pl.program_id(0)
        keep_b = (i > 0).astype(F32)
        keep_a = (i < nt - 1).astype(F32)

        @pl.when(i == 0)
        def _():
            dcw_ref[...] = jnp.zeros_like(dcw_ref)
            dsc_ref[...] = jnp.zeros_like(dsc_ref)
            vec_ref[...] = jnp.zeros_like(vec_ref)
            dpw_ref[...] = jnp.zeros_like(dpw_ref)
            dws_ref[...] = jnp.zeros_like(dws_ref)
            dbs_ref[...] = jnp.zeros_like(dbs_ref)

        def mcol(c0):
            return main_ref[:, c0:c0 + W].astype(F32)

        def hbcol(c0):
            return hb_ref[:, c0:c0 + W].astype(F32)

        def hacol(c0):
            return ha_ref[:, c0:c0 + W].astype(F32)

        def load_dy(c0):
            e2[0:tm, :] = dy_ref[:, c0:c0 + W].astype(F32)
            e2[tm:ne, :] = dyh_ref[:, c0:c0 + W].astype(F32) * keep_a

        a = mcol(0)
        sa = _sig(mcol(W))
        e1[0:HB, :] = hbcol(0) * _sig(hbcol(W)) * keep_b
        e1[HB:HB + tm, :] = a * sa
        e1[HB + tm:HB + ne, :] = hacol(0) * _sig(hacol(W))
        ca = jnp.broadcast_to(cb[...], (ne, W))
        for k in range(CONF_K):
            ca = ca + cw[k:k + 1, :] * e1[pl.ds(HB - (CONF_K - 1) + k, ne), :]
        xh, rstd = _ln_stats(ca)
        nn = xh * lg[...] + lb[...]
        s = _sig(nn)
        load_dy(0)
        dn = e2[0:ne, :] * (s * (1.0 + nn * (1.0 - s)))
        vec_ref[1:2, :] += _rowsum(dn[0:tm] * xh[0:tm])
        vec_ref[2:3, :] += _rowsum(dn[0:tm])
        dca = _ln_bwd(dn * lg[...], xh, rstd)
        e3[0:ne, :] = dca
        dmain = dca[0:tm]
        vec_ref[0:1, :] += _rowsum(dmain)
        for k in range(CONF_K):
            dcw_ref[k:k + 1, :] += _rowsum(dmain * e1[pl.ds(HB - (CONF_K - 1) + k, tm), :])
        dglu = cw[0:1, :] * e3[pl.ds(CONF_K - 1, tm), :]
        for k in range(1, CONF_K):
            dglu = dglu + cw[k:k + 1, :] * e3[pl.ds(CONF_K - 1 - k, tm), :]
        dp_ref[:, 0:W] = (dglu * sa).astype(BF16)
        dp_ref[:, W:2 * W] = (dglu * a * sa * (1.0 - sa)).astype(BF16)

        pin = mcol(1024)
        e1[0:HB, :] = hbcol(1024) * keep_b
        e1[HB:HB + tm, :] = pin
        load_dy(W)
        dyb = e2[0:ne, :]
        pos_m = (i * tm + lax.broadcasted_iota(jnp.int32, (tm, 1), 0) + 1).astype(F32)
        pos_e = (i * tm + lax.broadcasted_iota(jnp.int32, (ne, 1), 0) + 1).astype(F32)
        for g, w in enumerate(POOL_WINDOWS):
            lo = g * GW
            acc = e1[pl.ds(HB, tm), lo:lo + GW]
            for j in range(1, w):
                acc = acc + e1[pl.ds(HB - j, tm), lo:lo + GW]
            pooled = (acc / jnp.minimum(pos_m, float(w)) - pin[:, lo:lo + GW]).astype(BF16)
            pwb = pw[g].astype(BF16)
            mixed = jnp.dot(pooled, pwb, preferred_element_type=F32)
            dyb_g = dyb[:, lo:lo + GW]
            vec_ref[3:4, lo:lo + GW] += _rowsum(dyb_g[0:tm] * mixed)
            dmb = (dyb_g * ps[:, lo:lo + GW]).astype(BF16)
            dpw_ref[g] += lax.dot_general(pooled, dmb[0:tm], (((0,), (0,)), ((), ())), preferred_element_type=F32)
            dpool = lax.dot_general(dmb, pwb, (((1,), (1,)), ((), ())), preferred_element_type=F32)
            e3[0:ne, lo:lo + GW] = dpool / jnp.minimum(pos_e, float(w))
            back = e3[pl.ds(0, tm), lo:lo + GW]
            for j in range(1, w):
                back = back + e3[pl.ds(j, tm), lo:lo + GW]
            dp_ref[:, 1024 + lo:1024 + lo + GW] = (back - dpool[0:tm]).astype(BF16)

        cg = mcol(2048)
        hx = mcol(2560)
        e1[0:HB, :] = hbcol(2048) * hbcol(2560) * keep_b
        e1[HB:HB + tm, :] = cg * hx
        load_dy(2 * W)
        dyc = e2[0:tm, :]
        dconv = dyc * mcol(1536)
        e3[0:tm, :] = dconv
        e3[tm:ne, :] = e2[tm:ne, :] * hacol(1536)
        cv = sc[0:1, :] * e1[pl.ds(HB - 2, tm), :]
        for k in range(1, SC_K):
            cv = cv + sc[k:k + 1, :] * e1[pl.ds(HB - 2 + k, tm), :]
        dp_ref[:, 1536:2048] = (dyc * cv).astype(BF16)
        for k in range(SC_K):
            dsc_ref[k:k + 1, :] += _rowsum(dconv * e1[pl.ds(HB - 2 + k, tm), :])
        dq = sc[0:1, :] * e3[pl.ds(2, tm), :]
        for k in range(1, SC_K):
            dq = dq + sc[k:k + 1, :] * e3[pl.ds(2 - k, tm), :]
        dp_ref[:, 2048:2560] = (dq * hx).astype(BF16)
        dp_ref[:, 2560:3072] = (dq * cg).astype(BF16)

        u = mcol(3072)
        vh, vr = _ln_stats(mcol(3584))
        vn = (vh * gg[...] + gb[...]).astype(BF16)
        dyd = dy_ref[:, 3 * W:4 * W].astype(F32)
        tri = _tri_mask()
        for g in range(4):
            lo = g * GW
            wm = jnp.where(tri, ws[g], 0.0).astype(BF16)
            dws_g = jnp.zeros((GW, GW), F32)
            dbs_g = jnp.zeros((GW, 1), F32)
            for c in range(tm // GW):
                r0 = c * GW
                blk = vn[r0:r0 + GW, lo:lo + GW]
                sg = jnp.dot(wm, blk, preferred_element_type=F32) + bst[:, g:g + 1]
                dyd_b = dyd[r0:r0 + GW, lo:lo + GW]
                dp_ref[r0:r0 + GW, 3072 + lo:3072 + lo + GW] = (dyd_b * sg).astype(BF16)
                dsg = dyd_b * u[r0:r0 + GW, lo:lo + GW]
                dsgb = dsg.astype(BF16)
                dbs_g = dbs_g + jnp.sum(dsg, axis=-1, keepdims=True)
                dws_g = dws_g + lax.dot_general(dsgb, blk, (((1,), (1,)), ((), ())), preferred_element_type=F32)
                e1[r0:r0 + GW, lo:lo + GW] = lax.dot_general(wm, dsgb, (((0,), (0,)), ((), ())),
                                                             preferred_element_type=F32)
            dws_ref[g] += jnp.where(tri, dws_g, 0.0)
            dbs_ref[g] += jnp.broadcast_to(dbs_g, (GW, GW))
        dvn = e1[0:tm, :]
        vec_ref[4:5, :] += _rowsum(dvn * vh)
        vec_ref[5:6, :] += _rowsum(dvn)
        dp_ref[:, 3584:4096] = _ln_bwd(dvn * gg[...], vh, vr).astype(BF16)

    plist = _mixer_params(sp)
    in_specs = [pl.BlockSpec((tm, MIX_COLS), lambda i: (i, 0)),
                pl.BlockSpec((HB, MIX_COLS), lambda i: (jnp.maximum(i * per - 1, 0), 0)),
                pl.BlockSpec((HA, MIX_COLS), lambda i: (jnp.minimum((i + 1) * per, last_blk), 0)),
                pl.BlockSpec((tm, 4 * W), lambda i: (i, 0)),
                pl.BlockSpec((HA, 4 * W), lambda i: (jnp.minimum((i + 1) * per, last_blk), 0))]
    in_specs += _small_specs(plist)
    in_specs += [pl.BlockSpec(memory_space=pl.ANY)]
    z2 = lambda i: (0, 0)
    z3 = lambda i: (0, 0, 0)
    out_specs = [pl.BlockSpec((tm, MIX_COLS), lambda i: (i, 0)),
                 pl.BlockSpec((32, W), z2), pl.BlockSpec((8, W), z2), pl.BlockSpec((8, W), z2),
                 pl.BlockSpec((4, GW, GW), z3), pl.BlockSpec((4, GW, GW), z3), pl.BlockSpec((4, GW, GW), z3)]
    out_shape = [jax.ShapeDtypeStruct((T, COLS_IN), BF16),
                 jax.ShapeDtypeStruct((32, W), F32), jax.ShapeDtypeStruct((8, W), F32),
                 jax.ShapeDtypeStruct((8, W), F32),
                 jax.ShapeDtypeStruct((4, GW, GW), F32), jax.ShapeDtypeStruct((4, GW, GW), F32),
                 jax.ShapeDtypeStruct((4, GW, GW), F32)]
    n_in = 5 + len(plist)
    return pl.pallas_call(
        body, name="b_mixers", grid=(nt,), in_specs=in_specs, out_specs=out_specs, out_shape=out_shape,
        scratch_shapes=[pltpu.VMEM((HB + ne, W), F32), pltpu.VMEM((ne, W), F32), pltpu.VMEM((ne, W), F32)],
        input_output_aliases={n_in: 0},
        compiler_params=_cp("arbitrary"),
    )(proj, proj, proj, dy, dy, *plist, dproj)


def _norm_first(x, g, tm=512):
    def body(x_ref, g_ref, o_ref):
        o_ref[...] = _rms(x_ref[...], g_ref[...]).astype(BF16)

    return pl.pallas_call(
        body, name="f_norm0", grid=(T // tm,),
        in_specs=[pl.BlockSpec((tm, D), lambda i: (i, 0)), pl.BlockSpec((1, D), lambda i: (0, 0))],
        out_specs=pl.BlockSpec((tm, D), lambda i: (i, 0)),
        out_shape=jax.ShapeDtypeStruct((T, D), BF16), compiler_params=_cp("arbitrary"),
    )(x, g)


def _loss_head(x, target, g, tm=256):
    def body(x_ref, t_ref, g_ref, dx_ref, dg_ref, loss_ref):
        i = pl.program_id(0)
        x = x_ref[...]
        r = lax.rsqrt(jnp.mean(x * x, axis=-1, keepdims=True) + EPS)
        xh = x * r
        gv = g_ref[...]
        e = xh * gv - t_ref[...]
        dyv = e * (1.0 / D)
        part = jnp.sum(_rowsum(e * e), axis=-1, keepdims=True) * (0.5 / D)
        u = dyv * gv
        dx_ref[...] = r * (u - xh * jnp.mean(u * xh, axis=-1, keepdims=True))
        dgp = _rowsum(dyv * xh)

        @pl.when(i == 0)
        def _():
            dg_ref[...] = dgp
            loss_ref[...] = jnp.broadcast_to(part, (1, GW))

        @pl.when(i > 0)
        def _():
            dg_ref[...] += dgp
            loss_ref[...] += jnp.broadcast_to(part, (1, GW))

    return pl.pallas_call(
        body, name="loss_head", grid=(T // tm,),
        in_specs=[pl.BlockSpec((tm, D), lambda i: (i, 0)), pl.BlockSpec((tm, D), lambda i: (i, 0)),
                  pl.BlockSpec((1, D), lambda i: (0, 0))],
        out_specs=[pl.BlockSpec((tm, D), lambda i: (i, 0)), pl.BlockSpec((1, D), lambda i: (0, 0)),
                   pl.BlockSpec((1, GW), lambda i: (0, 0))],
        out_shape=[jax.ShapeDtypeStruct((T, D), F32), jax.ShapeDtypeStruct((1, D), F32),
                   jax.ShapeDtypeStruct((1, GW), F32)],
        compiler_params=_cp("arbitrary"),
    )(x, target, g)


def _ple_bwd_elem(dx, gl, pe, tm=512):
    def body(dx_ref, gl_ref, pe_ref, dpe_ref, dgl_ref):
        d = dx_ref[...]
        s = _sig(gl_ref[...].astype(F32))
        dpe_ref[...] = (d * s).astype(BF16)
        dgl_ref[...] = (d * pe_ref[...].astype(F32) * s * (1.0 - s)).astype(BF16)

    spec = pl.BlockSpec((tm, D), lambda i: (i, 0))
    return pl.pallas_call(
        body, name="b_ple_elem", grid=(T // tm,), in_specs=[spec, spec, spec], out_specs=[spec, spec],
        out_shape=[jax.ShapeDtypeStruct((T, D), BF16), jax.ShapeDtypeStruct((T, D), BF16)],
        compiler_params=_cp("arbitrary"),
    )(dx, gl, pe)


def _layer_fwd(x, h1, p_bf, gw, sp, g_next):
    proj, = _mm(h1, gw["w_in"], mode="out", name="f_proj", outs=[BF16])
    y = _mixer_fwd(proj, sp)
    z, merged = _merge_fwd(y, proj, gw["w_branch"])
    x2, h2 = _mm(merged, gw["w_out"].reshape(1, D, D), mode="acc", name="f_out", outs=[F32, BF16],
                 tiles=[x], params=[sp["g_mlp"]], epi=_epi_res_norm)
    up, act = _mm(h2, gw["w_up"], mode="out", name="f_up", outs=[BF16, BF16], epi=_epi_relu2)
    x3, h3 = _mm(act, gw["w_down"].reshape(4, D, D), mode="acc", name="f_down", outs=[F32, BF16],
                 tiles=[x2], params=[sp["g_ple"]], epi=_epi_res_norm)
    x4, gl, hn, pe = _mm(h3, gw["w_pleg"].reshape(1, D, D), mode="acc", name="f_gate", outs=[F32, BF16, BF16, BF16],
                         tiles=[x3, p_bf], params=[g_next, gw["w_ple"]], epi=_epi_ple)
    saved = dict(x=x, h1=h1, proj=proj, y=y, z=z, merged=merged, x2=x2, h2=h2, up=up, act=act, x3=x3, h3=h3,
                 pe=pe, gl=gl, p=p_bf)
    return x4, hn, saved


def _layer_bwd(dx4, sv, gw, sp, submit):
    dpe, dgl = _ple_bwd_elem(dx4, sv["gl"], sv["pe"])
    dw = {}
    dw["w_ple"] = _mm_tn(sv["p"], dpe, nj=NDEV, split="col", name="b_dw_ple")
    dw["w_pleg"] = _mm_tn(sv["h3"], dgl, nj=NDEV, split="row", name="b_dw_pleg")
    dx3, dx3b, dg_ple = _mm(dgl, gw["w_pleg"].reshape(1, D, D), mode="acc", trans_b=True, name="b_dh3",
                            outs=[F32, BF16], tiles=[sv["x3"], dx4], params=[sp["g_ple"]], epi=_epi_rms_bwd, reds=[D])
    dup, = _mm(dx3b, gw["w_down"], mode="out", trans_b=True, name="b_dact", outs=[BF16],
               tiles=[sv["up"]], epi=_epi_dup)
    dw["w_down"] = _mm_tn(sv["act"], dx3b, nj=NDEV, split="row", name="b_dw_down")
    dw["w_up"] = _mm_tn(sv["h2"], dup, nj=NDEV, split="col", name="b_dw_up")
    dx2, dx2b, dg_mlp = _mm(dup, gw["w_up"], mode="acc", trans_b=True, name="b_dh2",
                            outs=[F32, BF16], tiles=[sv["x2"], dx3], params=[sp["g_mlp"]], epi=_epi_rms_bwd, reds=[D])
    dm, = _mm(dx2b, gw["w_out"].reshape(1, D, D), mode="acc", trans_b=True, name="b_dmerged", outs=[BF16])
    dw["w_out"] = _mm_tn(sv["merged"], dx2b, nj=NDEV, split="row", name="b_dw_out")
    dz, dproj, dy = _merge_bwd(dm, sv["z"], sv["proj"], gw["w_branch"])
    dw["w_branch"] = _dw_branch(sv["y"], dz)
    dy = submit(dw, BIG[1:], dy)
    dproj, dcw, dsc, vec, dpw, dws, dbs = _mixer_bwd(sv["proj"], dy, dproj, sp)
    dw["w_in"] = _mm_tn(sv["h1"], dproj, nj=NDEV, split="col", name="b_dw_in")
    dw["w_in"], dproj = lax.optimization_barrier((dw["w_in"], dproj))
    dproj = submit(dw, BIG[:1], dproj)
    dx, _, dg_mix = _mm(dproj, gw["w_in"], mode="acc", trans_b=True, name="b_dh1", outs=[F32, BF16],
                        tiles=[sv["x"], dx2], params=[sp["g_mix"]], epi=_epi_rms_bwd, reds=[D])
    small = dict(norm_mix=dg_mix[0], conf_dw=dcw[:CONF_K], conf_dw_b=vec[0], conf_ln_g=vec[1], conf_ln_b=vec[2],
                 pool_w=dpw, pool_scale=vec[3], sc_conv=dsc[:SC_K], gmlp_ln_g=vec[4], gmlp_ln_b=vec[5],
                 gmlp_ws=dws, gmlp_bs=dbs[:, :, 0], norm_mlp=dg_mlp[0], norm_ple=dg_ple[0])
    return dx, small


ANY = pl.BlockSpec(memory_space=pl.ANY)


def _mesh_pos():
    return lax.axis_index("x"), lax.axis_index("y"), lax.axis_index("c")


def _other_chips(x, y):
    return [(1 - x, y), (x, 1 - y), (1 - x, 1 - y)]


def _launch_comm(body, peers_of, operands, out_shapes, sems, name, seq_id):
    n_in, n_out = len(operands), len(out_shapes)
    if seq_id is None:
        return pl.pallas_call(body, name=name, in_specs=[ANY] * n_in, out_specs=[ANY] * n_out,
                              out_shape=out_shapes, scratch_shapes=sems)(*operands)

    def seq_body(*refs):
        peers = peers_of(*_mesh_pos())
        barrier = pltpu.get_barrier_semaphore()
        for peer in peers:
            pl.semaphore_signal(barrier, inc=1, device_id=peer, device_id_type=MESH)
        pl.semaphore_wait(barrier, len(peers))
        body(*refs)

    return pl.kernel(seq_body, name=name, out_type=out_shapes,
                     mesh=plsc.ScalarSubcoreMesh(axis_name="seq", num_cores=1), scratch_types=sems,
                     compiler_params=pltpu.CompilerParams(collective_id=seq_id))(*operands)


def _all_gather(shards, name, seq_id=None):
    n = len(shards)

    def body(*refs):
        s_refs, o_refs = refs[:n], refs[n:2 * n]
        send_sems, recv_sems, local_sems = refs[2 * n:]
        x, y, c = _mesh_pos()
        me = 4 * x + 2 * y + c
        here = (x, y, c)
        sibling = (x, y, 1 - c)
        chips = _other_chips(x, y)

        def slot(px, py, pc):
            return 4 * px + 2 * py + pc

        def copy(t, k, slot_idx, to, src=None):
            dst = o_refs[t].at[slot_idx]
            return pltpu.make_async_remote_copy(
                src_ref=dst if src is None else src, dst_ref=dst,
                send_sem=send_sems.at[t * 7 + k], recv_sem=recv_sems.at[t * 7 + k],
                device_id=to, device_id_type=MESH)

        mine = [pltpu.make_async_copy(s_refs[t], o_refs[t].at[me], local_sems.at[t]) for t in range(n)]
        for cp in mine:
            cp.start()
        first = []
        for t in range(n):
            for j, chip in enumerate(chips):
                first.append(copy(t, 1 + j, me, (*chip, c), src=s_refs[t]))
        for t in range(n):
            first.append(copy(t, 0, me, sibling, src=s_refs[t]))
        for cp in first:
            cp.start()
        passed = []
        for t in range(n):
            for j, chip in enumerate(chips):
                copy(t, 1 + j, slot(*chip, c), here).wait_recv()
                fwd = copy(t, 4 + j, slot(*chip, c), sibling)
                fwd.start()
                passed.append(fwd)
        for t in range(n):
            copy(t, 0, slot(x, y, 1 - c), here).wait_recv()
            for j, chip in enumerate(chips):
                copy(t, 4 + j, slot(*chip, 1 - c), here).wait_recv()
        for cp in first + passed:
            cp.wait_send()
        for cp in mine:
            cp.wait()

    def peers_of(x, y, c):
        return [(x, y, 1 - c)] + [(*chip, c) for chip in _other_chips(x, y)]

    return _launch_comm(
        body, peers_of, shards, [jax.ShapeDtypeStruct((NDEV,) + s.shape, s.dtype) for s in shards],
        [pltpu.SemaphoreType.DMA((7 * n,)), pltpu.SemaphoreType.DMA((7 * n,)), pltpu.SemaphoreType.DMA((n,))],
        name, seq_id)


def _rs_exchange(p4s, qs, name, seq_id=None):
    n_p, n_q = len(p4s), len(qs)

    def body(*refs):
        p_refs, q_refs = refs[:n_p], refs[n_p:n_p + n_q]
        rb_refs, rc_refs = refs[n_p + n_q:2 * n_p + n_q], refs[2 * n_p + n_q:2 * (n_p + n_q)]
        pair_send, pair_recv, chip_send, chip_recv, local_sems = refs[2 * (n_p + n_q):]
        x, y, c = _mesh_pos()
        a_idx = 2 * x + y
        chips = _other_chips(x, y)
        mine = [pltpu.make_async_copy(q_refs[t].at[a_idx], rc_refs[t].at[a_idx], local_sems.at[t])
                for t in range(n_q)]
        sends = []
        for t in range(n_q):
            for j, chip in enumerate(chips):
                sends.append(pltpu.make_async_remote_copy(
                    src_ref=q_refs[t].at[2 * chip[0] + chip[1]], dst_ref=rc_refs[t].at[a_idx],
                    send_sem=chip_send.at[t * 3 + j], recv_sem=chip_recv.at[t * 3 + j],
                    device_id=(*chip, c), device_id_type=MESH))
        pairs = [pltpu.make_async_remote_copy(
            src_ref=p_refs[t].at[:, 1 - c], dst_ref=rb_refs[t], send_sem=pair_send.at[t], recv_sem=pair_recv.at[t],
            device_id=(x, y, 1 - c), device_id_type=MESH) for t in range(n_p)]
        for cp in sends + mine + pairs:
            cp.start()
        for cp in pairs:
            cp.wait()
        for t in range(n_q):
            for j, chip in enumerate(chips):
                landed = rc_refs[t].at[2 * chip[0] + chip[1]]
                pltpu.make_async_remote_copy(
                    src_ref=landed, dst_ref=landed, send_sem=chip_send.at[t * 3 + j],
                    recv_sem=chip_recv.at[t * 3 + j], device_id=(x, y, c), device_id_type=MESH).wait_recv()
        for cp in sends:
            cp.wait_send()
        for cp in mine:
            cp.wait()

    def peers_of(x, y, c):
        peers = [(x, y, 1 - c)] if n_p else []
        return peers + ([(*chip, c) for chip in _other_chips(x, y)] if n_q else [])

    out_shapes = [jax.ShapeDtypeStruct((NCHIP,) + p.shape[2:], p.dtype) for p in p4s]
    out_shapes += [jax.ShapeDtypeStruct(q.shape, q.dtype) for q in qs]
    sems = [pltpu.SemaphoreType.DMA((max(n_p, 1),)), pltpu.SemaphoreType.DMA((max(n_p, 1),)),
            pltpu.SemaphoreType.DMA((max(3 * n_q, 1),)), pltpu.SemaphoreType.DMA((max(3 * n_q, 1),)),
            pltpu.SemaphoreType.DMA((max(n_q, 1),))]
    got = _launch_comm(body, peers_of, list(p4s) + list(qs), out_shapes, sems, name, seq_id)
    return got[:n_p], got[n_p:]


def _pair_sum(p4s, rbs, c_idx, name, nst=4):
    n = len(p4s)
    trs = [p.shape[2] // nst for p in p4s]

    def body(c_ref, *refs):
        del c_ref
        p_refs, r_refs, o_refs = refs[:n], refs[n:2 * n], refs[2 * n:]
        for p_ref, r_ref, o_ref in zip(p_refs, r_refs, o_refs):
            o_ref[...] = (p_ref[...].astype(F32) + r_ref[...].astype(F32)).astype(o_ref.dtype)

    in_specs = [pl.BlockSpec((None, None, tr, p.shape[3]), lambda b, i, c_ref: (b, c_ref[0], i, 0))
                for p, tr in zip(p4s, trs)]
    in_specs += [pl.BlockSpec((None, tr, p.shape[3]), lambda b, i, c_ref: (b, i, 0)) for p, tr in zip(p4s, trs)]
    out_specs = [pl.BlockSpec((None, tr, p.shape[3]), lambda b, i, c_ref: (b, i, 0)) for p, tr in zip(p4s, trs)]
    return pl.pallas_call(
        body, name=name,
        grid_spec=pltpu.PrefetchScalarGridSpec(num_scalar_prefetch=1, grid=(NCHIP, nst), in_specs=in_specs,
                                               out_specs=out_specs),
        out_shape=[jax.ShapeDtypeStruct((NCHIP,) + p.shape[2:], p.dtype) for p in p4s],
        compiler_params=_cp("arbitrary", "arbitrary"),
    )(c_idx, *p4s, *rbs)


def _reduce_scatter(ps, c_idx, tag, nst=4):
    p4s = [p.reshape((NCHIP, 2) + p.shape[1:]) for p in ps]
    rbs, _ = _rs_exchange(p4s, [], name="rs_pair_" + tag)
    qs = _pair_sum(p4s, rbs, c_idx, name="rs_pairsum_" + tag, nst=nst)
    return _rs_exchange([], qs, name="rs_chip_" + tag)[1]


class _GradientPipeline:
    def __init__(self, c_idx, results):
        self.c_idx, self.results, self.pending = c_idx, results, None

    def _sum_pending(self, chain):
        names, layer, p4s, rbs = self.pending
        qs = _pair_sum(p4s, rbs, self.c_idx, name="rs_pairsum_" + ("first" if len(names) == 1 else "rest"))
        return lax.optimization_barrier((chain, qs))

    def submit(self, dw, names, layer, chain):
        qs, tag, seq_id = [], "pair", 3
        if self.pending is not None:
            chain, qs = self._sum_pending(chain)
            tag, seq_id = "pair_chip", 4
        p4s = [dw[n].reshape((NCHIP, 2) + BIG_SHARD[n]) for n in names]
        rbs, rcs = _rs_exchange(p4s, qs, name="rs_%s_%d" % (tag, len(names)), seq_id=seq_id)
        self._record(rcs)
        self.pending = (names, layer, p4s, rbs)
        return chain

    def finish(self, chain):
        chain, qs = self._sum_pending(chain)
        self._record(_rs_exchange([], qs, name="rs_chip_last", seq_id=5)[1])
        self.pending = None
        return chain

    def _record(self, rcs):
        if rcs:
            names, layer = self.pending[:2]
            for n, rc in zip(names, rcs):
                self.results[n][layer] = rc


def _adamw(w, g, m, v):
    m = ADAM_B1 * m + (1.0 - ADAM_B1) * g
    v = ADAM_B2 * v + (1.0 - ADAM_B2) * (g * g)
    m_hat = m / (1.0 - ADAM_B1 ** ADAM_STEP)
    v_hat = v / (1.0 - ADAM_B2 ** ADAM_STEP)
    delta = -ADAM_LR * (m_hat / (jnp.sqrt(v_hat) + ADAM_EPS) + ADAM_WD * w)
    return delta, m, v


def _adam_sharded(rcs, w, m, v, tr, name):
    _, r, c = w.shape
    nst = r // tr

    def body(rc0, rc1, rc2, rc3, w_ref, m_ref, v_ref, g_out, d_out, m_out, v_out):
        layer = pl.program_id(0)
        for k, rc in enumerate((rc0, rc1, rc2, rc3)):
            @pl.when(layer == k)
            def _():
                g = rc[0].astype(F32) + rc[1].astype(F32) + rc[2].astype(F32) + rc[3].astype(F32)
                delta, m_new, v_new = _adamw(w_ref[...], g, m_ref[...], v_ref[...])
                g_out[...] = g
                d_out[...] = delta
                m_out[...] = m_new
                v_out[...] = v_new

    rc_specs = [pl.BlockSpec((NCHIP, tr, c), lambda l, i, k=k: (0, jnp.where(l == k, i, 0), 0)) for k in range(DEPTH)]
    wspec = pl.BlockSpec((None, tr, c), lambda l, i: (l, i, 0))
    return pl.pallas_call(
        body, name=name, grid=(DEPTH, nst), in_specs=rc_specs + [wspec] * 3, out_specs=[wspec] * 4,
        out_shape=[jax.ShapeDtypeStruct(w.shape, F32)] * 4,
        compiler_params=_cp("arbitrary", "arbitrary"),
    )(*rcs, w, m, v)


def _adam_packed(g, w, m, v, tr=184):
    rows = g.shape[0]

    def body(g_ref, w_ref, m_ref, v_ref, d_out, m_out, v_out):
        delta, m_new, v_new = _adamw(w_ref[...], g_ref[...], m_ref[...], v_ref[...])
        d_out[...] = delta
        m_out[...] = m_new
        v_out[...] = v_new

    spec = pl.BlockSpec((tr, D), lambda i: (i, 0))
    return pl.pallas_call(
        body, name="adam_small", grid=(rows // tr,), in_specs=[spec] * 4, out_specs=[spec] * 3,
        out_shape=[jax.ShapeDtypeStruct(g.shape, F32)] * 3, compiler_params=_cp("arbitrary"),
    )(g, w, m, v)


def _sum4(rc):
    def body(rc_ref, o_ref):
        o_ref[...] = rc_ref[0] + rc_ref[1] + rc_ref[2] + rc_ref[3]

    return pl.pallas_call(
        body, name="small_sum", out_shape=jax.ShapeDtypeStruct(rc.shape[1:], F32),
    )(rc)


BIG = ("w_in", "w_branch", "w_out", "w_up", "w_down", "w_ple", "w_pleg")
BIG_SHARD = {"w_in": (D, D), "w_branch": (4 * W, GW), "w_out": (GW, D), "w_up": (D, W), "w_down": (W, D),
             "w_ple": (256, GW), "w_pleg": (GW, D)}
ADAM_ROWS = {"w_in": 256, "w_branch": 512, "w_out": 128, "w_up": 256, "w_down": 256, "w_ple": 256, "w_pleg": 128}
SMALL = (("norm_mix", (DEPTH, D)), ("conf_dw", (DEPTH, CONF_K, W)), ("conf_dw_b", (DEPTH, W)),
         ("conf_ln_g", (DEPTH, W)), ("conf_ln_b", (DEPTH, W)), ("pool_w", (DEPTH, 4, GW, GW)),
         ("pool_scale", (DEPTH, W)), ("sc_conv", (DEPTH, SC_K, W)), ("gmlp_ln_g", (DEPTH, W)),
         ("gmlp_ln_b", (DEPTH, W)), ("gmlp_ws", (DEPTH, 4, GW, GW)), ("gmlp_bs", (DEPTH, 4, GW)),
         ("norm_mlp", (DEPTH, D)), ("norm_ple", (DEPTH, D)), ("norm_final", (D,)))
CHANNEL_SHARDED = ("conf_dw", "sc_conv")
SMALL_ROWS = 80
PACK_ROWS = 552


def _pack(arrs, rows):
    flat = jnp.concatenate([a.reshape(-1) for a in arrs])
    return jnp.pad(flat, (0, rows * D - flat.shape[0])).reshape(rows, D)


def _unpack(packed, shapes):
    flat = packed.reshape(-1)
    out, off = [], 0
    for shp in shapes:
        size = 1
        for s in shp:
            size *= s
        out.append(flat[off:off + size].reshape(shp))
        off += size
    return out


def kernel(x, p, norm_mix, w_in, conf_dw, conf_dw_b, conf_ln_g, conf_ln_b, pool_w, pool_scale, sc_conv, gmlp_ln_g, gmlp_ln_b, gmlp_ws, gmlp_bs, w_branch, w_out, norm_mlp, w_up, w_down, norm_ple, w_ple, w_ple_gate, norm_final, loss_target, m_norm_mix, m_w_in, m_conf_dw, m_conf_dw_b, m_conf_ln_g, m_conf_ln_b, m_pool_w, m_pool_scale, m_sc_conv, m_gmlp_ln_g, m_gmlp_ln_b, m_gmlp_ws, m_gmlp_bs, m_w_branch, m_w_out, m_norm_mlp, m_w_up, m_w_down, m_norm_ple, m_w_ple, m_w_ple_gate, m_norm_final, v_norm_mix, v_w_in, v_conf_dw, v_conf_dw_b, v_conf_ln_g, v_conf_ln_b, v_pool_w, v_pool_scale, v_sc_conv, v_gmlp_ln_g, v_gmlp_ln_b, v_gmlp_ws, v_gmlp_bs, v_w_branch, v_w_out, v_norm_mlp, v_w_up, v_w_down, v_norm_ple, v_w_ple, v_w_ple_gate, v_norm_final):
    weights = dict(norm_mix=norm_mix, w_in=w_in, conf_dw=conf_dw, conf_dw_b=conf_dw_b, conf_ln_g=conf_ln_g,
                   conf_ln_b=conf_ln_b, pool_w=pool_w, pool_scale=pool_scale, sc_conv=sc_conv, gmlp_ln_g=gmlp_ln_g,
                   gmlp_ln_b=gmlp_ln_b, gmlp_ws=gmlp_ws, gmlp_bs=gmlp_bs, w_branch=w_branch, w_out=w_out,
                   norm_mlp=norm_mlp, w_up=w_up, w_down=w_down, norm_ple=norm_ple, w_ple=w_ple, w_pleg=w_ple_gate,
                   norm_final=norm_final)
    mom1 = dict(norm_mix=m_norm_mix, w_in=m_w_in, conf_dw=m_conf_dw, conf_dw_b=m_conf_dw_b, conf_ln_g=m_conf_ln_g,
                conf_ln_b=m_conf_ln_b, pool_w=m_pool_w, pool_scale=m_pool_scale, sc_conv=m_sc_conv,
                gmlp_ln_g=m_gmlp_ln_g, gmlp_ln_b=m_gmlp_ln_b, gmlp_ws=m_gmlp_ws, gmlp_bs=m_gmlp_bs,
                w_branch=m_w_branch, w_out=m_w_out, norm_mlp=m_norm_mlp, w_up=m_w_up, w_down=m_w_down,
                norm_ple=m_norm_ple, w_ple=m_w_ple, w_pleg=m_w_ple_gate, norm_final=m_norm_final)
    mom2 = dict(norm_mix=v_norm_mix, w_in=v_w_in, conf_dw=v_conf_dw, conf_dw_b=v_conf_dw_b, conf_ln_g=v_conf_ln_g,
                conf_ln_b=v_conf_ln_b, pool_w=v_pool_w, pool_scale=v_pool_scale, sc_conv=v_sc_conv,
                gmlp_ln_g=v_gmlp_ln_g, gmlp_ln_b=v_gmlp_ln_b, gmlp_ws=v_gmlp_ws, gmlp_bs=v_gmlp_bs,
                w_branch=v_w_branch, w_out=v_w_out, norm_mlp=v_norm_mlp, w_up=v_w_up, w_down=v_w_down,
                norm_ple=v_norm_ple, w_ple=v_w_ple, w_pleg=v_w_ple_gate, norm_final=v_norm_final)

    xi, yi, ci = _mesh_pos()
    me = 4 * xi + 2 * yi + ci
    c_idx = jnp.reshape(ci, (1,)).astype(jnp.int32)

    gathered, conf_full, sc_full = [], [], []
    for l in range(DEPTH):
        shard = lambda n: weights[n][l].astype(BF16).reshape(BIG_SHARD[n])
        w_in_g, conf_g, sc_g = _all_gather([shard("w_in"), conf_dw[l], sc_conv[l]], name="ag_first", seq_id=1)
        rest = _all_gather([shard(n) for n in BIG[1:]], name="ag_rest", seq_id=2)
        gw = dict(zip(BIG[1:], rest), w_in=w_in_g)
        gw["w_branch"] = gw["w_branch"].reshape(NDEV, 4, W, GW)
        gathered.append(gw)
        conf_full.append(conf_g)
        sc_full.append(sc_g)

    def small_params(l):
        return dict(cw=conf_full[l], cb=conf_dw_b[l][None], lg=conf_ln_g[l][None], lb=conf_ln_b[l][None],
                    pw=pool_w[l], ps=pool_scale[l][None], sc=sc_full[l], gg=gmlp_ln_g[l][None],
                    gb=gmlp_ln_b[l][None], ws=gmlp_ws[l], bst=gmlp_bs[l].T, g_mix=norm_mix[l][None],
                    g_mlp=norm_mlp[l][None], g_ple=norm_ple[l][None])

    xc = x.reshape(T, D)
    p_bf = p.reshape(DEPTH, T, 256).astype(BF16)
    h = _norm_first(xc, norm_mix[0][None])
    saved = []
    for l in range(DEPTH):
        g_next = norm_mix[l + 1][None] if l + 1 < DEPTH else norm_final[None]
        h, conf_g, sc_g = lax.optimization_barrier((h, conf_full[l], sc_full[l]))
        conf_full[l] = conf_g.transpose(1, 0, 2).reshape(CONF_K, W)
        sc_full[l] = sc_g.transpose(1, 0, 2).reshape(SC_K, W)
        xc, h, sv = _layer_fwd(xc, h, p_bf[l], gathered[l], small_params(l), g_next)
        saved.append(sv)

    dxc, dg_final, loss_part = _loss_head(xc, loss_target.reshape(T, D), norm_final[None])
    loss = lax.psum(loss_part[0, 0], ("x", "y", "c"))
    small_grads = [None] * DEPTH
    rcs = {n: [None] * DEPTH for n in BIG}
    pipeline = _GradientPipeline(c_idx, rcs)
    for l in reversed(range(DEPTH)):
        dxc, small_grads[l] = _layer_bwd(dxc, saved[l], gathered[l], small_params(l),
                                         lambda dw, names, value, l=l: pipeline.submit(dw, names, l, value))
    dxc = pipeline.finish(dxc)

    stacked = {n: jnp.stack([small_grads[l][n] for l in range(DEPTH)]) for n, _ in SMALL if n != "norm_final"}
    stacked["norm_final"] = dg_final[0]
    packed = _pack([stacked[n] for n, _ in SMALL], NDEV * SMALL_ROWS).reshape(NDEV, SMALL_ROWS, D)
    reduced_slot = _sum4(_reduce_scatter([packed], c_idx, "small", nst=1)[0])
    reduced = _all_gather([reduced_slot], name="ag_small")[0]
    small_full = dict(zip([n for n, _ in SMALL], _unpack(reduced, [s for _, s in SMALL])))

    grads, deltas, new_m, new_v = {}, {}, {}, {}
    for n in BIG:
        shp = (DEPTH,) + BIG_SHARD[n]
        g_, d_, m_, v_ = _adam_sharded(rcs[n], weights[n].reshape(shp), mom1[n].reshape(shp), mom2[n].reshape(shp),
                                       ADAM_ROWS[n], name="adam_" + n)
        full = weights[n].shape
        grads[n], deltas[n], new_m[n], new_v[n] = g_.reshape(full), d_.reshape(full), m_.reshape(full), v_.reshape(full)
    small_g = {}
    for n, _ in SMALL:
        if n in CHANNEL_SHARDED:
            small_g[n] = lax.dynamic_slice_in_dim(small_full[n], me * (W // NDEV), W // NDEV, axis=2)
        else:
            small_g[n] = small_full[n]
    names = [n for n, _ in SMALL]
    shapes = [weights[n].shape for n in names]
    d_p, m_p, v_p = _adam_packed(_pack([small_g[n] for n in names], PACK_ROWS),
                                 _pack([weights[n] for n in names], PACK_ROWS),
                                 _pack([mom1[n] for n in names], PACK_ROWS),
                                 _pack([mom2[n] for n in names], PACK_ROWS))
    for n, d_, m_, v_ in zip(names, _unpack(d_p, shapes), _unpack(m_p, shapes), _unpack(v_p, shapes)):
        grads[n], deltas[n], new_m[n], new_v[n] = small_g[n], d_, m_, v_

    order = ("norm_mix", "w_in", "conf_dw", "conf_dw_b", "conf_ln_g", "conf_ln_b", "pool_w", "pool_scale", "sc_conv",
             "gmlp_ln_g", "gmlp_ln_b", "gmlp_ws", "gmlp_bs", "w_branch", "w_out", "norm_mlp", "w_up", "w_down",
             "norm_ple", "w_ple", "w_pleg", "norm_final")
    return (loss, dxc.reshape(1, T, D), *[grads[n] for n in order], *[deltas[n] for n in order],
            *[new_m[n] for n in order], *[new_v[n] for n in order])
```

```python
import functools

import jax
import jax.numpy as jnp
from jax import lax
from jax.experimental import pallas as pl
from jax.experimental.pallas import tpu as pltpu
from jax.experimental.pallas import tpu_sc as plsc

F32 = jnp.float32
BF16 = jnp.bfloat16

DEPTH = 4
T = 2048
D = 1024
W = 512
NDEV = 8
NCHIP = 4
EPS = 1e-6
CONF_K = 31
SC_K = 3
POOL_WINDOWS = (2, 4, 8, 16)
GW = 128
HB = 32
HA = 32
COLS_IN = 8192
MIX_COLS = 4096

ADAM_LR = 0.001
ADAM_B1 = 0.9
ADAM_B2 = 0.999
ADAM_EPS = 1e-08
ADAM_WD = 0.01
ADAM_STEP = 10

VMEM_LIMIT_BYTES = 56 * 1024 * 1024
MESH = pl.DeviceIdType.MESH


def _cp(*sem):
    return pltpu.CompilerParams(dimension_semantics=tuple(sem), vmem_limit_bytes=VMEM_LIMIT_BYTES)


def _sig(x):
    return jax.nn.sigmoid(x)


def _rms(x, g):
    r = lax.rsqrt(jnp.mean(x * x, axis=-1, keepdims=True) + EPS)
    return x * r * g


def _rms_bwd(dh, x, g, dres):
    r = lax.rsqrt(jnp.mean(x * x, axis=-1, keepdims=True) + EPS)
    xh = x * r
    u = dh * g
    dx = r * (u - xh * jnp.mean(u * xh, axis=-1, keepdims=True)) + dres
    dg = jnp.sum(dh * xh, axis=0, keepdims=True)
    return dx, dg


def _ln_stats(x):
    mu = jnp.mean(x, axis=-1, keepdims=True)
    xc = x - mu
    rstd = lax.rsqrt(jnp.mean(xc * xc, axis=-1, keepdims=True) + EPS)
    return xc * rstd, rstd


def _ln_bwd(dxh, xh, rstd):
    return rstd * (dxh - jnp.mean(dxh, axis=-1, keepdims=True) - xh * jnp.mean(dxh * xh, axis=-1, keepdims=True))


def _rowsum(x):
    return jnp.sum(x, axis=0, keepdims=True)


EPI_ROWS = 256


def _relu2_bf16(up):
    r = jnp.maximum(up.astype(F32), 0.0)
    return (r * r).astype(BF16)


def _mm(a, b3, *, mode, name, outs, trans_b=False, tm=512, tiles=(), params=(), epi=None, reds=(), a_pre=None):
    t_, ka = a.shape
    nj, r, c = b3.shape
    kb, nb = (c, r) if trans_b else (r, c)
    nt = t_ // tm
    out_mode = mode == "out"
    if out_mode:
        assert ka == kb and not reds
        grid = (nj, nt)
        a_map = lambda g0, g1: (g1, 0)
        b_map = lambda g0, g1: (g0, 0, 0)
        t_map = lambda g0, g1: (g1, g0)
        width = nj * nb
    else:
        assert ka == nj * kb
        grid = (nt, nj)
        a_map = lambda g0, g1: (g0, g1)
        b_map = lambda g0, g1: (g1, 0, 0)
        t_map = lambda g0, g1: (g0, 0)
        width = nb
    n_t, n_p, n_o, n_r = len(tiles), len(params), len(outs), len(reds)
    use_acc = (not out_mode) and nj > 1
    dims = (((1,), (1,)), ((), ())) if trans_b else (((1,), (0,)), ((), ()))

    def body(a_ref, b_ref, *rest):
        t_refs = rest[:n_t]
        p_refs = rest[n_t:n_t + n_p]
        o_refs = rest[n_t + n_p:n_t + n_p + n_o]
        r_refs = rest[n_t + n_p + n_o:n_t + n_p + n_o + n_r]
        a_val = a_ref[...] if a_pre is None else a_pre(a_ref[...])
        part = lax.dot_general(a_val, b_ref[...], dims, preferred_element_type=F32)
        i = pl.program_id(1 if out_mode else 0)

        def finish(acc_rows):
            totals = [None] * n_r
            for r0 in range(0, tm, min(tm, EPI_ROWS)):
                rows = slice(r0, r0 + min(tm, EPI_ROWS))
                if epi is None:
                    res, rr = (acc_rows(rows),), ()
                else:
                    res, rr = epi(acc_rows(rows), [t[rows, :] for t in t_refs], [p[...] for p in p_refs])
                for o_ref, val in zip(o_refs, res):
                    o_ref[rows, :] = val.astype(o_ref.dtype)
                totals = [val if tot is None else tot + val for tot, val in zip(totals, rr)]
            for r_ref, val in zip(r_refs, totals):
                @pl.when(i == 0)
                def _():
                    r_ref[...] = val

                @pl.when(i > 0)
                def _():
                    r_ref[...] += val

        if use_acc:
            acc_ref = rest[-1]
            j = pl.program_id(1)

            @pl.when(j == 0)
            def _():
                acc_ref[...] = part

            @pl.when(jnp.logical_and(j > 0, j < nj - 1))
            def _():
                acc_ref[...] += part

            @pl.when(j == nj - 1)
            def _():
                finish(lambda rows: acc_ref[rows, :] + part[rows])
        else:
            finish(lambda rows: part[rows])

    const2 = lambda g0, g1: (0, 0)
    in_specs = [pl.BlockSpec((tm, kb), a_map), pl.BlockSpec((None, r, c), b_map)]
    in_specs += [pl.BlockSpec((tm, t.shape[1] // nj if out_mode else t.shape[1]), t_map) for t in tiles]
    in_specs += [pl.BlockSpec(p.shape, lambda g0, g1, nd=p.ndim: (0,) * nd) for p in params]
    out_specs = [pl.BlockSpec((tm, nb), t_map) for _ in outs] + [pl.BlockSpec((1, w), const2) for w in reds]
    out_shape = [jax.ShapeDtypeStruct((t_, width), dt) for dt in outs]
    out_shape += [jax.ShapeDtypeStruct((1, w), F32) for w in reds]
    res = pl.pallas_call(
        body, name=name, grid=grid, in_specs=in_specs, out_specs=out_specs, out_shape=out_shape,
        scratch_shapes=[pltpu.VMEM((tm, nb), F32)] if use_acc else [],
        compiler_params=_cp("arbitrary", "arbitrary"),
    )(a, b3, *tiles, *params)
    return res


def _mm_tn(a, g, *, nj, split, name, out_dtype=BF16, a_pre=None):
    t_ = a.shape[0]
    if split == "col":
        r, c = a.shape[1], g.shape[1] // nj
        a_spec = pl.BlockSpec((t_, r), lambda j: (0, 0))
        g_spec = pl.BlockSpec((t_, c), lambda j: (0, j))
    else:
        r, c = a.shape[1] // nj, g.shape[1]
        a_spec = pl.BlockSpec((t_, r), lambda j: (0, j))
        g_spec = pl.BlockSpec((t_, c), lambda j: (0, 0))

    def body(a_ref, g_ref, o_ref):
        a_val = a_ref[...] if a_pre is None else a_pre(a_ref[...])
        o_ref[...] = lax.dot_general(a_val, g_ref[...], (((0,), (0,)), ((), ())),
                                     preferred_element_type=F32).astype(o_ref.dtype)

    return pl.pallas_call(
        body, name=name, grid=(nj,), in_specs=[a_spec, g_spec],
        out_specs=pl.BlockSpec((None, r, c), lambda j: (j, 0, 0)),
        out_shape=jax.ShapeDtypeStruct((nj, r, c), out_dtype),
        compiler_params=_cp("arbitrary"),
    )(a, g)


def _epi_res_norm(acc, tiles, params):
    x_new = tiles[0] + acc
    return (x_new, _rms(x_new, params[0])), ()


def _epi_ple(acc, tiles, params):
    x_old, p_tile = tiles
    g_next, w_ple8 = params
    pe = jnp.concatenate([jnp.dot(p_tile, w_ple8[j], preferred_element_type=F32) for j in range(NDEV)], axis=1)
    x_new = x_old + pe * _sig(acc)
    return (x_new, acc, _rms(x_new, g_next), pe), ()


def _epi_rms_bwd(acc, tiles, params):
    dx, dg = _rms_bwd(acc, tiles[0], params[0], tiles[1])
    return (dx, dx), (dg,)


def _epi_dup(acc, tiles, params):
    return (acc * (2.0 * jnp.maximum(tiles[0].astype(F32), 0.0)),), ()


def _tri_mask():
    row = lax.broadcasted_iota(jnp.int32, (GW, GW), 0)
    col = lax.broadcasted_iota(jnp.int32, (GW, GW), 1)
    return row >= col


def _small_specs(sp_list):
    return [pl.BlockSpec(p.shape, (lambda i: (0, 0)) if p.ndim == 2 else (lambda i: (0, 0, 0))) for p in sp_list]


SUBLANES = 8


def _tap_sum(src, w_ref, taps, rows, stage):
    groups = {}
    for off, k in taps:
        groups.setdefault(off % SUBLANES, []).append((off - off % SUBLANES, k))
    out = None
    for res, members in sorted(groups.items()):
        n = rows if res == 0 else rows + SUBLANES
        part = None
        for base, k in members:
            term = w_ref[k:k + 1, :] * src[pl.ds(base, n), :]
            part = term if part is None else part + term
        if res:
            stage[0:n, :] = part
            part = stage[pl.ds(res, rows), :]
        out = part if out is None else out + part
    return out


def _tap_grads(grad, src, offsets, rows, stage, out_ref):
    pad = SUBLANES
    stage[0:pad, :] = jnp.zeros((pad, grad.shape[1]), F32)
    stage[pad:pad + rows, :] = grad
    stage[pad + rows:2 * pad + rows, :] = jnp.zeros((pad, grad.shape[1]), F32)
    groups = {}
    for k, off in enumerate(offsets):
        groups.setdefault(off % SUBLANES, []).append((off - off % SUBLANES, k))
    for res, members in sorted(groups.items()):
        shifted = stage[pl.ds(pad - res, rows + pad), :]
        for base, k in members:
            out_ref[k:k + 1, :] += _rowsum(shifted * src[pl.ds(base, rows + pad), :])


def _mixer_params(sp):
    return [sp["cw"], sp["cb"], sp["lg"], sp["lb"], sp["pw"], sp["ps"], sp["sc"], sp["gg"], sp["gb"], sp["ws"], sp["bst"]]


def _mixer_fwd(proj, sp, tm=256):
    nt = T // tm
    per = tm // HB

    conv_taps = [(HB - (CONF_K - 1) + k, k) for k in range(CONF_K)]

    def body(main_ref, halo_ref, cw, cb, lg, lb, pw, ps, sc, gg, gb, ws, bst, y_ref, ext, stage):
        i = pl.program_id(0)
        keep = (i > 0).astype(F32)

        def mcol(c0):
            return main_ref[:, c0:c0 + W].astype(F32)

        def hcol(c0):
            return halo_ref[:, c0:c0 + W].astype(F32)

        ext[0:HB, :] = hcol(0) * _sig(hcol(W)) * keep
        ext[HB:HB + tm, :] = mcol(0) * _sig(mcol(W))
        ca = _tap_sum(ext, cw, conv_taps, tm, stage) + cb[...]
        xh, _ = _ln_stats(ca)
        n = xh * lg[...] + lb[...]
        y_ref[:, 0:W] = (n * _sig(n)).astype(BF16)

        pin = mcol(1024)
        ext[0:HB, :] = hcol(1024) * keep
        ext[HB:HB + tm, :] = pin
        pos = (i * tm + lax.broadcasted_iota(jnp.int32, (tm, 1), 0) + 1).astype(F32)
        for g, w in enumerate(POOL_WINDOWS):
            lo = g * GW
            s = ext[pl.ds(HB, tm), lo:lo + GW]
            for j in range(1, w):
                s = s + ext[pl.ds(HB - j, tm), lo:lo + GW]
            pooled = s / jnp.minimum(pos, float(w)) - pin[:, lo:lo + GW]
            mixed = jnp.dot(pooled.astype(BF16), pw[g].astype(BF16), preferred_element_type=F32)
            y_ref[:, W + lo:W + lo + GW] = (mixed * ps[:, lo:lo + GW]).astype(BF16)

        ext[0:HB, :] = hcol(2048) * hcol(2560) * keep
        ext[HB:HB + tm, :] = mcol(2048) * mcol(2560)
        cv = sc[0:1, :] * ext[pl.ds(HB - 2, tm), :]
        cv = cv + sc[1:2, :] * ext[pl.ds(HB - 1, tm), :]
        cv = cv + sc[2:3, :] * ext[pl.ds(HB, tm), :]
        y_ref[:, 2 * W:3 * W] = (mcol(1536) * cv).astype(BF16)

        vh, _ = _ln_stats(mcol(3584))
        vn = (vh * gg[...] + gb[...]).astype(BF16)
        u = mcol(3072)
        tri = _tri_mask()
        for g in range(4):
            lo = g * GW
            wm = jnp.where(tri, ws[g], 0.0).astype(BF16)
            for c in range(tm // GW):
                r0 = c * GW
                sg = jnp.dot(wm, vn[r0:r0 + GW, lo:lo + GW], preferred_element_type=F32) + bst[:, g:g + 1]
                y_ref[r0:r0 + GW, 3 * W + lo:3 * W + lo + GW] = (u[r0:r0 + GW, lo:lo + GW] * sg).astype(BF16)

    plist = _mixer_params(sp)
    in_specs = [pl.BlockSpec((tm, MIX_COLS), lambda i: (i, 0)),
                pl.BlockSpec((HB, MIX_COLS), lambda i: (jnp.maximum(i * per - 1, 0), 0))]
    in_specs += _small_specs(plist)
    return pl.pallas_call(
        body, name="f_mixers", grid=(nt,), in_specs=in_specs,
        out_specs=pl.BlockSpec((tm, 4 * W), lambda i: (i, 0)),
        out_shape=jax.ShapeDtypeStruct((T, 4 * W), BF16),
        scratch_shapes=[pltpu.VMEM((HB + tm, W), F32), pltpu.VMEM((tm + SUBLANES, W), F32)],
        compiler_params=_cp("arbitrary"),
    )(proj, proj, *plist)


def _assemble_wb(wb8_ref, wbf_ref):
    for k in range(4):
        for j in range(NDEV):
            wbf_ref[k, :, j * GW:(j + 1) * GW] = wb8_ref[j, k]


def _merge_fwd(y, proj, wb8, tm=256):
    nt = T // tm

    def body(y_ref, gate_ref, wb8_ref, z_ref, m_ref, wbf):
        @pl.when(pl.program_id(0) == 0)
        def _():
            _assemble_wb(wb8_ref, wbf)

        m = jnp.zeros((tm, D), F32)
        for k in range(4):
            zk = jnp.dot(y_ref[:, k * W:(k + 1) * W], wbf[k], preferred_element_type=F32)
            z_ref[:, k * D:(k + 1) * D] = zk.astype(BF16)
            m = m + _sig(gate_ref[:, k * D:(k + 1) * D].astype(F32)) * zk
        m_ref[...] = m.astype(BF16)

    return pl.pallas_call(
        body, name="f_merge", grid=(nt,),
        in_specs=[pl.BlockSpec((tm, 4 * W), lambda i: (i, 0)),
                  pl.BlockSpec((tm, 4 * D), lambda i: (i, 1)),
                  pl.BlockSpec(wb8.shape, lambda i: (0, 0, 0, 0))],
        out_specs=[pl.BlockSpec((tm, 4 * D), lambda i: (i, 0)), pl.BlockSpec((tm, D), lambda i: (i, 0))],
        out_shape=[jax.ShapeDtypeStruct((T, 4 * D), BF16), jax.ShapeDtypeStruct((T, D), BF16)],
        scratch_shapes=[pltpu.VMEM((4, W, D), BF16)],
        compiler_params=_cp("arbitrary"),
    )(y, proj, wb8)


def _merge_bwd(dm, z, proj, wb8, tm=256):
    nt = T // tm

    def body(dm_ref, z_ref, gate_ref, wb8_ref, dz_ref, dp_ref, dy_ref, wbf):
        @pl.when(pl.program_id(0) == 0)
        def _():
            _assemble_wb(wb8_ref, wbf)

        dmv = dm_ref[...].astype(F32)
        for k in range(4):
            s = _sig(gate_ref[:, k * D:(k + 1) * D].astype(F32))
            dzk = (dmv * s).astype(BF16)
            dz_ref[:, k * D:(k + 1) * D] = dzk
            dp_ref[:, k * D:(k + 1) * D] = (dmv * z_ref[:, k * D:(k + 1) * D].astype(F32) * s * (1.0 - s)).astype(BF16)
            dyk = lax.dot_general(dzk, wbf[k], (((1,), (1,)), ((), ())), preferred_element_type=F32)
            dy_ref[:, k * W:(k + 1) * W] = dyk.astype(BF16)

    return pl.pallas_call(
        body, name="b_merge", grid=(nt,),
        in_specs=[pl.BlockSpec((tm, D), lambda i: (i, 0)),
                  pl.BlockSpec((tm, 4 * D), lambda i: (i, 0)),
                  pl.BlockSpec((tm, 4 * D), lambda i: (i, 1)),
                  pl.BlockSpec(wb8.shape, lambda i: (0, 0, 0, 0))],
        out_specs=[pl.BlockSpec((tm, 4 * D), lambda i: (i, 0)),
                   pl.BlockSpec((tm, 4 * D), lambda i: (i, 1)),
                   pl.BlockSpec((tm, 4 * W), lambda i: (i, 0))],
        out_shape=[jax.ShapeDtypeStruct((T, 4 * D), BF16),
                   jax.ShapeDtypeStruct((T, COLS_IN), BF16),
                   jax.ShapeDtypeStruct((T, 4 * W), BF16)],
        scratch_shapes=[pltpu.VMEM((4, W, D), BF16)],
        compiler_params=_cp("arbitrary"),
    )(dm, z, proj, wb8)


def _dw_branch(y, dz):
    def body(y_ref, dz_ref, o_ref):
        res = lax.dot_general(y_ref[...], dz_ref[...], (((0,), (0,)), ((), ())), preferred_element_type=F32)
        for j in range(NDEV):
            o_ref[j] = res[:, j * GW:(j + 1) * GW].astype(BF16)

    return pl.pallas_call(
        body, name="b_dw_branch", grid=(4,),
        in_specs=[pl.BlockSpec((T, W), lambda k: (0, k)), pl.BlockSpec((T, D), lambda k: (0, k))],
        out_specs=pl.BlockSpec((NDEV, None, W, GW), lambda k: (0, k, 0, 0)),
        out_shape=jax.ShapeDtypeStruct((NDEV, 4, W, GW), BF16),
        compiler_params=_cp("arbitrary"),
    )(y, dz)


def _mixer_bwd(proj, dy, dproj, sp, tm=256):
    nt = T // tm
    per = tm // HB
    ne = tm + HA
    last_blk = T // HA - 1
    conv_taps = [(HB - (CONF_K - 1) + k, k) for k in range(CONF_K)]

    def body(main_ref, hb_ref, ha_ref, dy_ref, dyh_ref, cw, cb, lg, lb, pw, ps, sc, gg, gb, ws, bst, dp_any,
             dp_ref, dcw_ref, dsc_ref, vec_ref, dpw_ref, dws_ref, dbs_ref, e1, e2, e3, stage):
        del dp_any
        i = pl.program_id(0)
        keep_b = (i > 0).astype(F32)
        keep_a = (i < nt - 1).astype(F32)

        @pl.when(i == 0)
        def _():
            dcw_ref[...] = jnp.zeros_like(dcw_ref)
            dsc_ref[...] = jnp.zeros_like(dsc_ref)
            vec_ref[...] = jnp.zeros_like(vec_ref)
            dpw_ref[...] = jnp.zeros_like(dpw_ref)
            dws_ref[...] = jnp.zeros_like(dws_ref)
            dbs_ref[...] = jnp.zeros_like(dbs_ref)

        def mcol(c0):
            return main_ref[:, c0:c0 + W].astype(F32)

        def hbcol(c0):
            return hb_ref[:, c0:c0 + W].astype(F32)

        def hacol(c0):
            return ha_ref[:, c0:c0 + W].astype(F32)

        def load_dy(c0):
            e2[0:tm, :] = dy_ref[:, c0:c0 + W].astype(F32)
            e2[tm:ne, :] = dyh_ref[:, c0:c0 + W].astype(F32) * keep_a

        a = mcol(0)
        sa = _sig(mcol(W))
        e1[0:HB, :] = hbcol(0) * _sig(hbcol(W)) * keep_b
        e1[HB:HB + tm, :] = a * sa
        e1[HB + tm:HB + ne, :] = hacol(0) * _sig(hacol(W))
        ca = _tap_sum(e1, cw, conv_taps, ne, stage) + cb[...]
        xh, rstd = _ln_stats(ca)
        nn = xh * lg[...] + lb[...]
        s = _sig(nn)
        load_dy(0)
        dn = e2[0:ne, :] * (s * (1.0 + nn * (1.0 - s)))
        vec_ref[1:2, :] += _rowsum(dn[0:tm] * xh[0:tm])
        vec_ref[2:3, :] += _rowsum(dn[0:tm])
        dca = _ln_bwd(dn * lg[...], xh, rstd)
        e3[0:ne, :] = dca
        dmain = dca[0:tm]
        vec_ref[0:1, :] += _rowsum(dmain)
        _tap_grads(dmain, e1, [off for off, _ in conv_taps], tm, stage, dcw_ref)
        dglu = _tap_sum(e3, cw, [(CONF_K - 1 - k, k) for k in range(CONF_K)], tm, stage)
        dp_ref[:, 0:W] = (dglu * sa).astype(BF16)
        dp_ref[:, W:2 * W] = (dglu * a * sa * (1.0 - sa)).astype(BF16)

        pin = mcol(1024)
        e1[0:HB, :] = hbcol(1024) * keep_b
        e1[HB:HB + tm, :] = pin
        load_dy(W)
        dyb = e2[0:ne, :]
        pos_m = (i * tm + lax.broadcasted_iota(jnp.int32, (tm, 1), 0) + 1).astype(F32)
        pos_e = (i * tm + lax.broadcasted_iota(jnp.int32, (ne, 1), 0) + 1).astype(F32)
        for g, w in enumerate(POOL_WINDOWS):
            lo = g * GW
            acc = e1[pl.ds(HB, tm), lo:lo + GW]
            for j in range(1, w):
                acc = acc + e1[pl.ds(HB - j, tm), lo:lo + GW]
            pooled = (acc / jnp.minimum(pos_m, float(w)) - pin[:, lo:lo + GW]).astype(BF16)
            pwb = pw[g].astype(BF16)
            mixed = jnp.dot(pooled, pwb, preferred_element_type=F32)
            dyb_g = dyb[:, lo:lo + GW]
            vec_ref[3:4, lo:lo + GW] += _rowsum(dyb_g[0:tm] * mixed)
            dmb = (dyb_g * ps[:, lo:lo + GW]).astype(BF16)
            dpw_ref[g] += lax.dot_general(pooled, dmb[0:tm], (((0,), (0,)), ((), ())), preferred_element_type=F32)
            dpool = lax.dot_general(dmb, pwb, (((1,), (1,)), ((), ())), preferred_element_type=F32)
            e3[0:ne, lo:lo + GW] = dpool / jnp.minimum(pos_e, float(w))
            back = e3[pl.ds(0, tm), lo:lo + GW]
            for j in range(1, w):
                back = back + e3[pl.ds(j, tm), lo:lo + GW]
            dp_ref[:, 1024 + lo:1024 + lo + GW] = (back - dpool[0:tm]).astype(BF16)

        cg = mcol(2048)
        hx = mcol(2560)
        e1[0:HB, :] = hbcol(2048) * hbcol(2560) * keep_b
        e1[HB:HB + tm, :] = cg * hx
        load_dy(2 * W)
        dyc = e2[0:tm, :]
        dconv = dyc * mcol(1536)
        e3[0:tm, :] = dconv
        e3[tm:ne, :] = e2[tm:ne, :] * hacol(1536)
        cv = sc[0:1, :] * e1[pl.ds(HB - 2, tm), :]
        for k in range(1, SC_K):
            cv = cv + sc[k:k + 1, :] * e1[pl.ds(HB - 2 + k, tm), :]
        dp_ref[:, 1536:2048] = (dyc * cv).astype(BF16)
        for k in range(SC_K):
            dsc_ref[k:k + 1, :] += _rowsum(dconv * e1[pl.ds(HB - 2 + k, tm), :])
        dq = sc[0:1, :] * e3[pl.ds(2, tm), :]
        for k in range(1, SC_K):
            dq = dq + sc[k:k + 1, :] * e3[pl.ds(2 - k, tm), :]
        dp_ref[:, 2048:2560] = (dq * hx).astype(BF16)
        dp_ref[:, 2560:3072] = (dq * cg).astype(BF16)

        u = mcol(3072)
        vh, vr = _ln_stats(mcol(3584))
        vn = (vh * gg[...] + gb[...]).astype(BF16)
        dyd = dy_ref[:, 3 * W:4 * W].astype(F32)
        tri = _tri_mask()
        for g in range(4):
            lo = g * GW
            wm = jnp.where(tri, ws[g], 0.0).astype(BF16)
            dws_g = jnp.zeros((GW, GW), F32)
            dbs_g = jnp.zeros((GW, 1), F32)
            for c in range(tm // GW):
                r0 = c * GW
                blk = vn[r0:r0 + GW, lo:lo + GW]
                sg = jnp.dot(wm, blk, preferred_element_type=F32) + bst[:, g:g + 1]
                dyd_b = dyd[r0:r0 + GW, lo:lo + GW]
                dp_ref[r0:r0 + GW, 3072 + lo:3072 + lo + GW] = (dyd_b * sg).astype(BF16)
                dsg = dyd_b * u[r0:r0 + GW, lo:lo + GW]
                dsgb = dsg.astype(BF16)
                dbs_g = dbs_g + jnp.sum(dsg, axis=-1, keepdims=True)
                dws_g = dws_g + lax.dot_general(dsgb, blk, (((1,), (1,)), ((), ())), preferred_element_type=F32)
                e1[r0:r0 + GW, lo:lo + GW] = lax.dot_general(wm, dsgb, (((0,), (0,)), ((), ())),
                                                             preferred_element_type=F32)
            dws_ref[g] += jnp.where(tri, dws_g, 0.0)
            dbs_ref[g] += jnp.broadcast_to(dbs_g, (GW, GW))
        dvn = e1[0:tm, :]
        vec_ref[4:5, :] += _rowsum(dvn * vh)
        vec_ref[5:6, :] += _rowsum(dvn)
        dp_ref[:, 3584:4096] = _ln_bwd(dvn * gg[...], vh, vr).astype(BF16)

    plist = _mixer_params(sp)
    in_specs = [pl.BlockSpec((tm, MIX_COLS), lambda i: (i, 0)),
                pl.BlockSpec((HB, MIX_COLS), lambda i: (jnp.maximum(i * per - 1, 0), 0)),
                pl.BlockSpec((HA, MIX_COLS), lambda i: (jnp.minimum((i + 1) * per, last_blk), 0)),
                pl.BlockSpec((tm, 4 * W), lambda i: (i, 0)),
                pl.BlockSpec((HA, 4 * W), lambda i: (jnp.minimum((i + 1) * per, last_blk), 0))]
    in_specs += _small_specs(plist)
    in_specs += [pl.BlockSpec(memory_space=pl.ANY)]
    z2 = lambda i: (0, 0)
    z3 = lambda i: (0, 0, 0)
    out_specs = [pl.BlockSpec((tm, MIX_COLS), lambda i: (i, 0)),
                 pl.BlockSpec((32, W), z2), pl.BlockSpec((8, W), z2), pl.BlockSpec((8, W), z2),
                 pl.BlockSpec((4, GW, GW), z3), pl.BlockSpec((4, GW, GW), z3), pl.BlockSpec((4, GW, GW), z3)]
    out_shape = [jax.ShapeDtypeStruct((T, COLS_IN), BF16),
                 jax.ShapeDtypeStruct((32, W), F32), jax.ShapeDtypeStruct((8, W), F32),
                 jax.ShapeDtypeStruct((8, W), F32),
                 jax.ShapeDtypeStruct((4, GW, GW), F32), jax.ShapeDtypeStruct((4, GW, GW), F32),
                 jax.ShapeDtypeStruct((4, GW, GW), F32)]
    n_in = 5 + len(plist)
    return pl.pallas_call(
        body, name="b_mixers", grid=(nt,), in_specs=in_specs, out_specs=out_specs, out_shape=out_shape,
        scratch_shapes=[pltpu.VMEM((HB + ne, W), F32), pltpu.VMEM((ne, W), F32), pltpu.VMEM((ne, W), F32),
                        pltpu.VMEM((ne + SUBLANES, W), F32)],
        input_output_aliases={n_in: 0},
        compiler_params=_cp("arbitrary"),
    )(proj, proj, proj, dy, dy, *plist, dproj)


def _norm_first(x, g, tm=512):
    def body(x_ref, g_ref, o_ref):
        o_ref[...] = _rms(x_ref[...], g_ref[...]).astype(BF16)

    return pl.pallas_call(
        body, name="f_norm0", grid=(T // tm,),
        in_specs=[pl.BlockSpec((tm, D), lambda i: (i, 0)), pl.BlockSpec((1, D), lambda i: (0, 0))],
        out_specs=pl.BlockSpec((tm, D), lambda i: (i, 0)),
        out_shape=jax.ShapeDtypeStruct((T, D), BF16), compiler_params=_cp("arbitrary"),
    )(x, g)


def _loss_head(x, target, g, tm=256):
    def body(x_ref, t_ref, g_ref, dx_ref, dg_ref, loss_ref):
        i = pl.program_id(0)
        x = x_ref[...]
        r = lax.rsqrt(jnp.mean(x * x, axis=-1, keepdims=True) + EPS)
        xh = x * r
        gv = g_ref[...]
        e = xh * gv - t_ref[...]
        dyv = e * (1.0 / D)
        part = jnp.sum(_rowsum(e * e), axis=-1, keepdims=True) * (0.5 / D)
        u = dyv * gv
        dx_ref[...] = r * (u - xh * jnp.mean(u * xh, axis=-1, keepdims=True))
        dgp = _rowsum(dyv * xh)

        @pl.when(i == 0)
        def _():
            dg_ref[...] = dgp
            loss_ref[...] = jnp.broadcast_to(part, (1, GW))

        @pl.when(i > 0)
        def _():
            dg_ref[...] += dgp
            loss_ref[...] += jnp.broadcast_to(part, (1, GW))

    return pl.pallas_call(
        body, name="loss_head", grid=(T // tm,),
        in_specs=[pl.BlockSpec((tm, D), lambda i: (i, 0)), pl.BlockSpec((tm, D), lambda i: (i, 0)),
                  pl.BlockSpec((1, D), lambda i: (0, 0))],
        out_specs=[pl.BlockSpec((tm, D), lambda i: (i, 0)), pl.BlockSpec((1, D), lambda i: (0, 0)),
                   pl.BlockSpec((1, GW), lambda i: (0, 0))],
        out_shape=[jax.ShapeDtypeStruct((T, D), F32), jax.ShapeDtypeStruct((1, D), F32),
                   jax.ShapeDtypeStruct((1, GW), F32)],
        compiler_params=_cp("arbitrary"),
    )(x, target, g)


def _ple_bwd_elem(dx, gl, pe, tm=512):
    def body(dx_ref, gl_ref, pe_ref, dpe_ref, dgl_ref):
        d = dx_ref[...]
        s = _sig(gl_ref[...].astype(F32))
        dpe_ref[...] = (d * s).astype(BF16)
        dgl_ref[...] = (d * pe_ref[...].astype(F32) * s * (1.0 - s)).astype(BF16)

    spec = pl.BlockSpec((tm, D), lambda i: (i, 0))
    return pl.pallas_call(
        body, name="b_ple_elem", grid=(T // tm,), in_specs=[spec, spec, spec], out_specs=[spec, spec],
        out_shape=[jax.ShapeDtypeStruct((T, D), BF16), jax.ShapeDtypeStruct((T, D), BF16)],
        compiler_params=_cp("arbitrary"),
    )(dx, gl, pe)


def _layer_fwd(x, h1, p_bf, gw, sp, g_next):
    proj, = _mm(h1, gw["w_in"], mode="out", name="f_proj", outs=[BF16], tm=T)
    y = _mixer_fwd(proj, sp)
    z, merged = _merge_fwd(y, proj, gw["w_branch"])
    x2, h2 = _mm(merged, gw["w_out"].reshape(1, D, D), mode="acc", name="f_out", outs=[F32, BF16], tm=T // 2,
                 tiles=[x], params=[sp["g_mlp"]], epi=_epi_res_norm)
    up, = _mm(h2, gw["w_up"], mode="out", name="f_up", outs=[BF16], tm=T)
    x3, h3 = _mm(up, gw["w_down"].reshape(4, D, D), mode="acc", name="f_down", outs=[F32, BF16], tm=T // 2,
                 tiles=[x2], params=[sp["g_ple"]], epi=_epi_res_norm, a_pre=_relu2_bf16)
    x4, gl, hn, pe = _mm(h3, gw["w_pleg"].reshape(1, D, D), mode="acc", name="f_gate", tm=T // 2,
                         outs=[F32, BF16, BF16, BF16], tiles=[x3, p_bf], params=[g_next, gw["w_ple"]], epi=_epi_ple)
    saved = dict(x=x, h1=h1, proj=proj, y=y, z=z, merged=merged, x2=x2, h2=h2, up=up, x3=x3, h3=h3,
                 pe=pe, gl=gl, p=p_bf)
    return x4, hn, saved


def _layer_bwd(dx4, sv, gw, sp, submit):
    dpe, dgl = _ple_bwd_elem(dx4, sv["gl"], sv["pe"])
    dw = {}
    dw["w_ple"] = _mm_tn(sv["p"], dpe, nj=NDEV, split="col", name="b_dw_ple")
    dw["w_pleg"] = _mm_tn(sv["h3"], dgl, nj=NDEV, split="row", name="b_dw_pleg")
    dx3, dx3b, dg_ple = _mm(dgl, gw["w_pleg"].reshape(1, D, D), mode="acc", trans_b=True, name="b_dh3", tm=T // 2,
                            outs=[F32, BF16], tiles=[sv["x3"], dx4], params=[sp["g_ple"]], epi=_epi_rms_bwd, reds=[D])
    dup, = _mm(dx3b, gw["w_down"], mode="out", trans_b=True, name="b_dact", outs=[BF16], tm=T,
               tiles=[sv["up"]], epi=_epi_dup)
    dw["w_down"] = _mm_tn(sv["up"], dx3b, nj=NDEV, split="row", name="b_dw_down", a_pre=_relu2_bf16)
    dw["w_up"] = _mm_tn(sv["h2"], dup, nj=NDEV, split="col", name="b_dw_up")
    dx2, dx2b, dg_mlp = _mm(dup, gw["w_up"], mode="acc", trans_b=True, name="b_dh2", tm=T // 2,
                            outs=[F32, BF16], tiles=[sv["x2"], dx3], params=[sp["g_mlp"]], epi=_epi_rms_bwd, reds=[D])
    dm, = _mm(dx2b, gw["w_out"].reshape(1, D, D), mode="acc", trans_b=True, name="b_dmerged", outs=[BF16],
              tm=T // 2)
    dw["w_out"] = _mm_tn(sv["merged"], dx2b, nj=NDEV, split="row", name="b_dw_out")
    dz, dproj, dy = _merge_bwd(dm, sv["z"], sv["proj"], gw["w_branch"])
    dw["w_branch"] = _dw_branch(sv["y"], dz)
    dy = submit(dw, BIG[1:], dy)
    dproj, dcw, dsc, vec, dpw, dws, dbs = _mixer_bwd(sv["proj"], dy, dproj, sp)
    dw["w_in"] = _mm_tn(sv["h1"], dproj, nj=NDEV, split="col", name="b_dw_in")
    dw["w_in"], dproj = lax.optimization_barrier((dw["w_in"], dproj))
    dproj = submit(dw, BIG[:1], dproj)
    dx, dg_mix = _mm(dproj, gw["w_in"], mode="acc", trans_b=True, name="b_dh1", outs=[F32], tm=T // 2,
                     tiles=[sv["x"], dx2], params=[sp["g_mix"]], epi=_epi_rms_bwd, reds=[D])
    small = dict(norm_mix=dg_mix[0], conf_dw=dcw[:CONF_K], conf_dw_b=vec[0], conf_ln_g=vec[1], conf_ln_b=vec[2],
                 pool_w=dpw, pool_scale=vec[3], sc_conv=dsc[:SC_K], gmlp_ln_g=vec[4], gmlp_ln_b=vec[5],
                 gmlp_ws=dws, gmlp_bs=dbs[:, :, 0], norm_mlp=dg_mlp[0], norm_ple=dg_ple[0])
    return dx, small


ANY = pl.BlockSpec(memory_space=pl.ANY)


def _mesh_pos():
    return lax.axis_index("x"), lax.axis_index("y"), lax.axis_index("c")


def _other_chips(x, y):
    return [(1 - x, y), (x, 1 - y), (1 - x, 1 - y)]


def _launch_comm(body, peers_of, operands, out_shapes, sems, name, seq_id):
    n_in, n_out = len(operands), len(out_shapes)
    if seq_id is None:
        return pl.pallas_call(body, name=name, in_specs=[ANY] * n_in, out_specs=[ANY] * n_out,
                              out_shape=out_shapes, scratch_shapes=sems)(*operands)

    def seq_body(*refs):
        peers = peers_of(*_mesh_pos())
        barrier = pltpu.get_barrier_semaphore()
        for peer in peers:
            pl.semaphore_signal(barrier, inc=1, device_id=peer, device_id_type=MESH)
        pl.semaphore_wait(barrier, len(peers))
        body(*refs)

    return pl.kernel(seq_body, name=name, out_type=out_shapes,
                     mesh=plsc.ScalarSubcoreMesh(axis_name="seq", num_cores=1), scratch_types=sems,
                     compiler_params=pltpu.CompilerParams(collective_id=seq_id))(*operands)


def _all_gather(shards, name, seq_id=None):
    n = len(shards)

    def body(*refs):
        s_refs, o_refs = refs[:n], refs[n:2 * n]
        send_sems, recv_sems, local_sems = refs[2 * n:]
        x, y, c = _mesh_pos()
        me = 4 * x + 2 * y + c
        here = (x, y, c)
        sibling = (x, y, 1 - c)
        chips = _other_chips(x, y)

        def slot(px, py, pc):
            return 4 * px + 2 * py + pc

        def copy(t, k, slot_idx, to, src=None):
            dst = o_refs[t].at[slot_idx]
            return pltpu.make_async_remote_copy(
                src_ref=dst if src is None else src, dst_ref=dst,
                send_sem=send_sems.at[t * 7 + k], recv_sem=recv_sems.at[t * 7 + k],
                device_id=to, device_id_type=MESH)

        mine = [pltpu.make_async_copy(s_refs[t], o_refs[t].at[me], local_sems.at[t]) for t in range(n)]
        for cp in mine:
            cp.start()
        first = []
        for t in range(n):
            for j, chip in enumerate(chips):
                first.append(copy(t, 1 + j, me, (*chip, c), src=s_refs[t]))
        for t in range(n):
            first.append(copy(t, 0, me, sibling, src=s_refs[t]))
        for cp in first:
            cp.start()
        passed = []
        for t in range(n):
            for j, chip in enumerate(chips):
                copy(t, 1 + j, slot(*chip, c), here).wait_recv()
                fwd = copy(t, 4 + j, slot(*chip, c), sibling)
                fwd.start()
                passed.append(fwd)
        for t in range(n):
            copy(t, 0, slot(x, y, 1 - c), here).wait_recv()
            for j, chip in enumerate(chips):
                copy(t, 4 + j, slot(*chip, 1 - c), here).wait_recv()
        for cp in first + passed:
            cp.wait_send()
        for cp in mine:
            cp.wait()

    def peers_of(x, y, c):
        return [(x, y, 1 - c)] + [(*chip, c) for chip in _other_chips(x, y)]

    return _launch_comm(
        body, peers_of, shards, [jax.ShapeDtypeStruct((NDEV,) + s.shape, s.dtype) for s in shards],
        [pltpu.SemaphoreType.DMA((7 * n,)), pltpu.SemaphoreType.DMA((7 * n,)), pltpu.SemaphoreType.DMA((n,))],
        name, seq_id)


def _rs_exchange(p4s, qs, name, seq_id=None):
    n_p, n_q = len(p4s), len(qs)

    def body(*refs):
        p_refs, q_refs = refs[:n_p], refs[n_p:n_p + n_q]
        rb_refs, rc_refs = refs[n_p + n_q:2 * n_p + n_q], refs[2 * n_p + n_q:2 * (n_p + n_q)]
        pair_send, pair_recv, chip_send, chip_recv, local_sems = refs[2 * (n_p + n_q):]
        x, y, c = _mesh_pos()
        a_idx = 2 * x + y
        chips = _other_chips(x, y)
        mine = [pltpu.make_async_copy(q_refs[t].at[a_idx], rc_refs[t].at[a_idx], local_sems.at[t])
                for t in range(n_q)]
        sends = []
        for t in range(n_q):
            for j, chip in enumerate(chips):
                sends.append(pltpu.make_async_remote_copy(
                    src_ref=q_refs[t].at[2 * chip[0] + chip[1]], dst_ref=rc_refs[t].at[a_idx],
                    send_sem=chip_send.at[t * 3 + j], recv_sem=chip_recv.at[t * 3 + j],
                    device_id=(*chip, c), device_id_type=MESH))
        pairs = [pltpu.make_async_remote_copy(
            src_ref=p_refs[t].at[:, 1 - c], dst_ref=rb_refs[t], send_sem=pair_send.at[t], recv_sem=pair_recv.at[t],
            device_id=(x, y, 1 - c), device_id_type=MESH) for t in range(n_p)]
        for cp in sends + mine + pairs:
            cp.start()
        for cp in pairs:
            cp.wait()
        for t in range(n_q):
            for j, chip in enumerate(chips):
                landed = rc_refs[t].at[2 * chip[0] + chip[1]]
                pltpu.make_async_remote_copy(
                    src_ref=landed, dst_ref=landed, send_sem=chip_send.at[t * 3 + j],
                    recv_sem=chip_recv.at[t * 3 + j], device_id=(x, y, c), device_id_type=MESH).wait_recv()
        for cp in sends:
            cp.wait_send()
        for cp in mine:
            cp.wait()

    def peers_of(x, y, c):
        peers = [(x, y, 1 - c)] if n_p else []
        return peers + ([(*chip, c) for chip in _other_chips(x, y)] if n_q else [])

    out_shapes = [jax.ShapeDtypeStruct((NCHIP,) + p.shape[2:], p.dtype) for p in p4s]
    out_shapes += [jax.ShapeDtypeStruct(q.shape, q.dtype) for q in qs]
    sems = [pltpu.SemaphoreType.DMA((max(n_p, 1),)), pltpu.SemaphoreType.DMA((max(n_p, 1),)),
            pltpu.SemaphoreType.DMA((max(3 * n_q, 1),)), pltpu.SemaphoreType.DMA((max(3 * n_q, 1),)),
            pltpu.SemaphoreType.DMA((max(n_q, 1),))]
    got = _launch_comm(body, peers_of, list(p4s) + list(qs), out_shapes, sems, name, seq_id)
    return got[:n_p], got[n_p:]


def _pair_sum(p4s, rbs, c_idx, name, nst=4):
    n = len(p4s)
    trs = [p.shape[2] // nst for p in p4s]

    def body(c_ref, *refs):
        del c_ref
        p_refs, r_refs, o_refs = refs[:n], refs[n:2 * n], refs[2 * n:]
        for p_ref, r_ref, o_ref in zip(p_refs, r_refs, o_refs):
            o_ref[...] = (p_ref[...].astype(F32) + r_ref[...].astype(F32)).astype(o_ref.dtype)

    in_specs = [pl.BlockSpec((None, None, tr, p.shape[3]), lambda b, i, c_ref: (b, c_ref[0], i, 0))
                for p, tr in zip(p4s, trs)]
    in_specs += [pl.BlockSpec((None, tr, p.shape[3]), lambda b, i, c_ref: (b, i, 0)) for p, tr in zip(p4s, trs)]
    out_specs = [pl.BlockSpec((None, tr, p.shape[3]), lambda b, i, c_ref: (b, i, 0)) for p, tr in zip(p4s, trs)]
    return pl.pallas_call(
        body, name=name,
        grid_spec=pltpu.PrefetchScalarGridSpec(num_scalar_prefetch=1, grid=(NCHIP, nst), in_specs=in_specs,
                                               out_specs=out_specs),
        out_shape=[jax.ShapeDtypeStruct((NCHIP,) + p.shape[2:], p.dtype) for p in p4s],
        compiler_params=_cp("arbitrary", "arbitrary"),
    )(c_idx, *p4s, *rbs)


def _reduce_scatter(ps, c_idx, tag, nst=4):
    p4s = [p.reshape((NCHIP, 2) + p.shape[1:]) for p in ps]
    rbs, _ = _rs_exchange(p4s, [], name="rs_pair_" + tag)
    qs = _pair_sum(p4s, rbs, c_idx, name="rs_pairsum_" + tag, nst=nst)
    return _rs_exchange([], qs, name="rs_chip_" + tag)[1]


class _GradientPipeline:
    def __init__(self, c_idx, results):
        self.c_idx, self.results, self.pending = c_idx, results, None

    def _sum_pending(self, chain):
        names, layer, p4s, rbs = self.pending
        qs = _pair_sum(p4s, rbs, self.c_idx, name="rs_pairsum_" + ("first" if len(names) == 1 else "rest"))
        return lax.optimization_barrier((chain, qs))

    def submit(self, dw, names, layer, chain):
        qs, tag, seq_id = [], "pair", 3
        if self.pending is not None:
            chain, qs = self._sum_pending(chain)
            tag, seq_id = "pair_chip", 4
        p4s = [dw[n].reshape((NCHIP, 2) + BIG_SHARD[n]) for n in names]
        rbs, rcs = _rs_exchange(p4s, qs, name="rs_%s_%d" % (tag, len(names)), seq_id=seq_id)
        self._record(rcs)
        self.pending = (names, layer, p4s, rbs)
        return chain

    def finish(self, chain):
        chain, qs = self._sum_pending(chain)
        self._record(_rs_exchange([], qs, name="rs_chip_last", seq_id=5)[1])
        self.pending = None
        return chain

    def _record(self, rcs):
        if rcs:
            names, layer = self.pending[:2]
            for n, rc in zip(names, rcs):
                self.results[n][layer] = rc


def _adamw(w, g, m, v):
    m = ADAM_B1 * m + (1.0 - ADAM_B1) * g
    v = ADAM_B2 * v + (1.0 - ADAM_B2) * (g * g)
    m_hat = m / (1.0 - ADAM_B1 ** ADAM_STEP)
    v_hat = v / (1.0 - ADAM_B2 ** ADAM_STEP)
    delta = -ADAM_LR * (m_hat / (jnp.sqrt(v_hat) + ADAM_EPS) + ADAM_WD * w)
    return delta, m, v


def _adam_sharded(rcs, w, m, v, tr, name):
    _, r, c = w.shape
    nst = r // tr

    def body(rc0, rc1, rc2, rc3, w_ref, m_ref, v_ref, g_out, d_out, m_out, v_out):
        layer = pl.program_id(0)
        for k, rc in enumerate((rc0, rc1, rc2, rc3)):
            @pl.when(layer == k)
            def _():
                g = rc[0].astype(F32) + rc[1].astype(F32) + rc[2].astype(F32) + rc[3].astype(F32)
                delta, m_new, v_new = _adamw(w_ref[...], g, m_ref[...], v_ref[...])
                g_out[...] = g
                d_out[...] = delta
                m_out[...] = m_new
                v_out[...] = v_new

    rc_specs = [pl.BlockSpec((NCHIP, tr, c), lambda l, i, k=k: (0, jnp.where(l == k, i, 0), 0)) for k in range(DEPTH)]
    wspec = pl.BlockSpec((None, tr, c), lambda l, i: (l, i, 0))
    return pl.pallas_call(
        body, name=name, grid=(DEPTH, nst), in_specs=rc_specs + [wspec] * 3, out_specs=[wspec] * 4,
        out_shape=[jax.ShapeDtypeStruct(w.shape, F32)] * 4,
        compiler_params=_cp("arbitrary", "arbitrary"),
    )(*rcs, w, m, v)


def _adam_packed(g, w, m, v, tr=184):
    rows = g.shape[0]

    def body(g_ref, w_ref, m_ref, v_ref, d_out, m_out, v_out):
        delta, m_new, v_new = _adamw(w_ref[...], g_ref[...], m_ref[...], v_ref[...])
        d_out[...] = delta
        m_out[...] = m_new
        v_out[...] = v_new

    spec = pl.BlockSpec((tr, D), lambda i: (i, 0))
    return pl.pallas_call(
        body, name="adam_small", grid=(rows // tr,), in_specs=[spec] * 4, out_specs=[spec] * 3,
        out_shape=[jax.ShapeDtypeStruct(g.shape, F32)] * 3, compiler_params=_cp("arbitrary"),
    )(g, w, m, v)


def _sum4(rc):
    def body(rc_ref, o_ref):
        o_ref[...] = rc_ref[0] + rc_ref[1] + rc_ref[2] + rc_ref[3]

    return pl.pallas_call(
        body, name="small_sum", out_shape=jax.ShapeDtypeStruct(rc.shape[1:], F32),
    )(rc)


BIG = ("w_in", "w_branch", "w_out", "w_up", "w_down", "w_ple", "w_pleg")
BIG_SHARD = {"w_in": (D, D), "w_branch": (4 * W, GW), "w_out": (GW, D), "w_up": (D, W), "w_down": (W, D),
             "w_ple": (256, GW), "w_pleg": (GW, D)}
ADAM_ROWS = {"w_in": 256, "w_branch": 512, "w_out": 128, "w_up": 256, "w_down": 256, "w_ple": 256, "w_pleg": 128}
SMALL = (("norm_mix", (DEPTH, D)), ("conf_dw", (DEPTH, CONF_K, W)), ("conf_dw_b", (DEPTH, W)),
         ("conf_ln_g", (DEPTH, W)), ("conf_ln_b", (DEPTH, W)), ("pool_w", (DEPTH, 4, GW, GW)),
         ("pool_scale", (DEPTH, W)), ("sc_conv", (DEPTH, SC_K, W)), ("gmlp_ln_g", (DEPTH, W)),
         ("gmlp_ln_b", (DEPTH, W)), ("gmlp_ws", (DEPTH, 4, GW, GW)), ("gmlp_bs", (DEPTH, 4, GW)),
         ("norm_mlp", (DEPTH, D)), ("norm_ple", (DEPTH, D)), ("norm_final", (D,)))
CHANNEL_SHARDED = ("conf_dw", "sc_conv")
SMALL_ROWS = 80
PACK_ROWS = 552


def _pack(arrs, rows):
    flat = jnp.concatenate([a.reshape(-1) for a in arrs])
    return jnp.pad(flat, (0, rows * D - flat.shape[0])).reshape(rows, D)


def _unpack(packed, shapes):
    flat = packed.reshape(-1)
    out, off = [], 0
    for shp in shapes:
        size = 1
        for s in shp:
            size *= s
        out.append(flat[off:off + size].reshape(shp))
        off += size
    return out


def kernel(x, p, norm_mix, w_in, conf_dw, conf_dw_b, conf_ln_g, conf_ln_b, pool_w, pool_scale, sc_conv, gmlp_ln_g, gmlp_ln_b, gmlp_ws, gmlp_bs, w_branch, w_out, norm_mlp, w_up, w_down, norm_ple, w_ple, w_ple_gate, norm_final, loss_target, m_norm_mix, m_w_in, m_conf_dw, m_conf_dw_b, m_conf_ln_g, m_conf_ln_b, m_pool_w, m_pool_scale, m_sc_conv, m_gmlp_ln_g, m_gmlp_ln_b, m_gmlp_ws, m_gmlp_bs, m_w_branch, m_w_out, m_norm_mlp, m_w_up, m_w_down, m_norm_ple, m_w_ple, m_w_ple_gate, m_norm_final, v_norm_mix, v_w_in, v_conf_dw, v_conf_dw_b, v_conf_ln_g, v_conf_ln_b, v_pool_w, v_pool_scale, v_sc_conv, v_gmlp_ln_g, v_gmlp_ln_b, v_gmlp_ws, v_gmlp_bs, v_w_branch, v_w_out, v_norm_mlp, v_w_up, v_w_down, v_norm_ple, v_w_ple, v_w_ple_gate, v_norm_final):
    weights = dict(norm_mix=norm_mix, w_in=w_in, conf_dw=conf_dw, conf_dw_b=conf_dw_b, conf_ln_g=conf_ln_g,
                   conf_ln_b=conf_ln_b, pool_w=pool_w, pool_scale=pool_scale, sc_conv=sc_conv, gmlp_ln_g=gmlp_ln_g,
                   gmlp_ln_b=gmlp_ln_b, gmlp_ws=gmlp_ws, gmlp_bs=gmlp_bs, w_branch=w_branch, w_out=w_out,
                   norm_mlp=norm_mlp, w_up=w_up, w_down=w_down, norm_ple=norm_ple, w_ple=w_ple, w_pleg=w_ple_gate,
                   norm_final=norm_final)
    mom1 = dict(norm_mix=m_norm_mix, w_in=m_w_in, conf_dw=m_conf_dw, conf_dw_b=m_conf_dw_b, conf_ln_g=m_conf_ln_g,
                conf_ln_b=m_conf_ln_b, pool_w=m_pool_w, pool_scale=m_pool_scale, sc_conv=m_sc_conv,
                gmlp_ln_g=m_gmlp_ln_g, gmlp_ln_b=m_gmlp_ln_b, gmlp_ws=m_gmlp_ws, gmlp_bs=m_gmlp_bs,
                w_branch=m_w_branch, w_out=m_w_out, norm_mlp=m_norm_mlp, w_up=m_w_up, w_down=m_w_down,
                norm_ple=m_norm_ple, w_ple=m_w_ple, w_pleg=m_w_ple_gate, norm_final=m_norm_final)
    mom2 = dict(norm_mix=v_norm_mix, w_in=v_w_in, conf_dw=v_conf_dw, conf_dw_b=v_conf_dw_b, conf_ln_g=v_conf_ln_g,
                conf_ln_b=v_conf_ln_b, pool_w=v_pool_w, pool_scale=v_pool_scale, sc_conv=v_sc_conv,
                gmlp_ln_g=v_gmlp_ln_g, gmlp_ln_b=v_gmlp_ln_b, gmlp_ws=v_gmlp_ws, gmlp_bs=v_gmlp_bs,
                w_branch=v_w_branch, w_out=v_w_out, norm_mlp=v_norm_mlp, w_up=v_w_up, w_down=v_w_down,
                norm_ple=v_norm_ple, w_ple=v_w_ple, w_pleg=v_w_ple_gate, norm_final=v_norm_final)

    xi, yi, ci = _mesh_pos()
    me = 4 * xi + 2 * yi + ci
    c_idx = jnp.reshape(ci, (1,)).astype(jnp.int32)

    gathered, conf_full, sc_full = [], [], []
    for l in range(DEPTH):
        shard = lambda n: weights[n][l].astype(BF16).reshape(BIG_SHARD[n])
        w_in_g, conf_g, sc_g = _all_gather([shard("w_in"), conf_dw[l], sc_conv[l]], name="ag_first", seq_id=1)
        rest = _all_gather([shard(n) for n in BIG[1:]], name="ag_rest", seq_id=2)
        gw = dict(zip(BIG[1:], rest), w_in=w_in_g)
        gw["w_branch"] = gw["w_branch"].reshape(NDEV, 4, W, GW)
        gathered.append(gw)
        conf_full.append(conf_g)
        sc_full.append(sc_g)

    def small_params(l):
        return dict(cw=conf_full[l], cb=conf_dw_b[l][None], lg=conf_ln_g[l][None], lb=conf_ln_b[l][None],
                    pw=pool_w[l], ps=pool_scale[l][None], sc=sc_full[l], gg=gmlp_ln_g[l][None],
                    gb=gmlp_ln_b[l][None], ws=gmlp_ws[l], bst=gmlp_bs[l].T, g_mix=norm_mix[l][None],
                    g_mlp=norm_mlp[l][None], g_ple=norm_ple[l][None])

    xc = x.reshape(T, D)
    p_bf = p.reshape(DEPTH, T, 256).astype(BF16)
    h = _norm_first(xc, norm_mix[0][None])
    saved = []
    for l in range(DEPTH):
        g_next = norm_mix[l + 1][None] if l + 1 < DEPTH else norm_final[None]
        h, conf_g, sc_g = lax.optimization_barrier((h, conf_full[l], sc_full[l]))
        conf_full[l] = conf_g.transpose(1, 0, 2).reshape(CONF_K, W)
        sc_full[l] = sc_g.transpose(1, 0, 2).reshape(SC_K, W)
        xc, h, sv = _layer_fwd(xc, h, p_bf[l], gathered[l], small_params(l), g_next)
        saved.append(sv)

    dxc, dg_final, loss_part = _loss_head(xc, loss_target.reshape(T, D), norm_final[None])
    loss = lax.psum(loss_part[0, 0], ("x", "y", "c"))
    small_grads = [None] * DEPTH
    rcs = {n: [None] * DEPTH for n in BIG}
    pipeline = _GradientPipeline(c_idx, rcs)
    for l in reversed(range(DEPTH)):
        dxc, small_grads[l] = _layer_bwd(dxc, saved[l], gathered[l], small_params(l),
                                         lambda dw, names, value, l=l: pipeline.submit(dw, names, l, value))
    dxc = pipeline.finish(dxc)

    stacked = {n: jnp.stack([small_grads[l][n] for l in range(DEPTH)]) for n, _ in SMALL if n != "norm_final"}
    stacked["norm_final"] = dg_final[0]
    packed = _pack([stacked[n] for n, _ in SMALL], NDEV * SMALL_ROWS).reshape(NDEV, SMALL_ROWS, D)
    reduced_slot = _sum4(_reduce_scatter([packed], c_idx, "small", nst=1)[0])
    reduced = _all_gather([reduced_slot], name="ag_small")[0]
    small_full = dict(zip([n for n, _ in SMALL], _unpack(reduced, [s for _, s in SMALL])))

    grads, deltas, new_m, new_v = {}, {}, {}, {}
    for n in BIG:
        shp = (DEPTH,) + BIG_SHARD[n]
        g_, d_, m_, v_ = _adam_sharded(rcs[n], weights[n].reshape(shp), mom1[n].reshape(shp), mom2[n].reshape(shp),
                                       ADAM_ROWS[n], name="adam_" + n)
        full = weights[n].shape
        grads[n], deltas[n], new_m[n], new_v[n] = g_.reshape(full), d_.reshape(full), m_.reshape(full), v_.reshape(full)
    small_g = {}
    for n, _ in SMALL:
        if n in CHANNEL_SHARDED:
            small_g[n] = lax.dynamic_slice_in_dim(small_full[n], me * (W // NDEV), W // NDEV, axis=2)
        else:
            small_g[n] = small_full[n]
    names = [n for n, _ in SMALL]
    shapes = [weights[n].shape for n in names]
    d_p, m_p, v_p = _adam_packed(_pack([small_g[n] for n in names], PACK_ROWS),
                                 _pack([weights[n] for n in names], PACK_ROWS),
                                 _pack([mom1[n] for n in names], PACK_ROWS),
                                 _pack([mom2[n] for n in names], PACK_ROWS))
    for n, d_, m_, v_ in zip(names, _unpack(d_p, shapes), _unpack(m_p, shapes), _unpack(v_p, shapes)):
        grads[n], deltas[n], new_m[n], new_v[n] = small_g[n], d_, m_, v_

    order = ("norm_mix", "w_in", "conf_dw", "conf_dw_b", "conf_ln_g", "conf_ln_b", "pool_w", "pool_scale", "sc_conv",
             "gmlp_ln_g", "gmlp_ln_b", "gmlp_ws", "gmlp_bs", "w_branch", "w_out", "norm_mlp", "w_up", "w_down",
             "norm_ple", "w_ple", "w_pleg", "norm_final")
    return (loss, dxc.reshape(1, T, D), *[grads[n] for n in order], *[deltas[n] for n in order],
            *[new_m[n] for n in order], *[new_v[n] for n in order])
```

```python
import functools

import jax
import jax.numpy as jnp
from jax import lax
from jax.experimental import pallas as pl
from jax.experimental.pallas import tpu as pltpu
from jax.experimental.pallas import tpu_sc as plsc

F32 = jnp.float32
BF16 = jnp.bfloat16

DEPTH = 4
T = 2048
D = 1024
W = 512
NDEV = 8
NCHIP = 4
EPS = 1e-6
CONF_K = 31
SC_K = 3
POOL_WINDOWS = (2, 4, 8, 16)
GW = 128
HB = 32
HA = 32
COLS_IN = 8192
MIX_COLS = 4096

ADAM_LR = 0.001
ADAM_B1 = 0.9
ADAM_B2 = 0.999
ADAM_EPS = 1e-08
ADAM_WD = 0.01
ADAM_STEP = 10

VMEM_LIMIT_BYTES = 56 * 1024 * 1024
MESH = pl.DeviceIdType.MESH


def _cp(*sem):
    return pltpu.CompilerParams(dimension_semantics=tuple(sem), vmem_limit_bytes=VMEM_LIMIT_BYTES)


def _sig(x):
    return jax.nn.sigmoid(x)


def _rms(x, g):
    r = lax.rsqrt(jnp.mean(x * x, axis=-1, keepdims=True) + EPS)
    return x * r * g


def _rms_bwd(dh, x, g, dres):
    r = lax.rsqrt(jnp.mean(x * x, axis=-1, keepdims=True) + EPS)
    xh = x * r
    u = dh * g
    dx = r * (u - xh * jnp.mean(u * xh, axis=-1, keepdims=True)) + dres
    dg = jnp.sum(dh * xh, axis=0, keepdims=True)
    return dx, dg


def _ln_stats(x):
    mu = jnp.mean(x, axis=-1, keepdims=True)
    xc = x - mu
    rstd = lax.rsqrt(jnp.mean(xc * xc, axis=-1, keepdims=True) + EPS)
    return xc * rstd, rstd


def _ln_bwd(dxh, xh, rstd):
    return rstd * (dxh - jnp.mean(dxh, axis=-1, keepdims=True) - xh * jnp.mean(dxh * xh, axis=-1, keepdims=True))


def _rowsum(x):
    return jnp.sum(x, axis=0, keepdims=True)


EPI_ROWS = 256


def _relu2_bf16(up):
    r = jnp.maximum(up.astype(F32), 0.0)
    return (r * r).astype(BF16)


def _mm(a, b3, *, mode, name, outs, trans_b=False, tm=512, tiles=(), params=(), epi=None, reds=(), a_pre=None):
    t_, ka = a.shape
    nj, r, c = b3.shape
    kb, nb = (c, r) if trans_b else (r, c)
    nt = t_ // tm
    out_mode = mode == "out"
    full = mode == "full"
    assert trans_b or not full
    if out_mode:
        assert ka == kb and not reds
        grid = (nj, nt)
        a_map = lambda g0, g1: (g1, 0)
        b_map = lambda g0, g1: (g0, 0, 0)
        t_map = lambda g0, g1: (g1, g0)
        width = nj * nb
    else:
        assert ka == nj * kb
        grid = (nt, 1 if full else nj)
        a_map = lambda g0, g1: (g0, g1)
        b_map = lambda g0, g1: (g1, 0, 0)
        t_map = lambda g0, g1: (g0, 0)
        width = nb
    n_t, n_p, n_o, n_r = len(tiles), len(params), len(outs), len(reds)
    use_acc = (not out_mode) and nj > 1 and not full
    dims = (((1,), (1,)), ((), ())) if trans_b else (((1,), (0,)), ((), ()))

    def body(a_ref, b_ref, *rest):
        t_refs = rest[:n_t]
        p_refs = rest[n_t:n_t + n_p]
        o_refs = rest[n_t + n_p:n_t + n_p + n_o]
        r_refs = rest[n_t + n_p + n_o:n_t + n_p + n_o + n_r]
        i = pl.program_id(1 if out_mode else 0)
        a_val = a_ref[...] if a_pre is None else a_pre(a_ref[...])
        if full:
            b_all, b_sems = rest[-2], rest[-1]

            @pl.when(i == 0)
            def _():
                cps = [pltpu.make_async_copy(b_ref.at[j], b_all.at[:, j * c:(j + 1) * c], b_sems.at[j])
                       for j in range(nj)]
                for cp in cps:
                    cp.start()
                for cp in cps:
                    cp.wait()

            part = lax.dot_general(a_val, b_all[...], dims, preferred_element_type=F32)
        else:
            part = lax.dot_general(a_val, b_ref[...], dims, preferred_element_type=F32)

        def finish(acc_rows):
            totals = [None] * n_r
            for r0 in range(0, tm, min(tm, EPI_ROWS)):
                rows = slice(r0, r0 + min(tm, EPI_ROWS))
                if epi is None:
                    res, rr = (acc_rows(rows),), ()
                else:
                    res, rr = epi(acc_rows(rows), [t[rows, :] for t in t_refs], [p[...] for p in p_refs])
                for o_ref, val in zip(o_refs, res):
                    o_ref[rows, :] = val.astype(o_ref.dtype)
                totals = [val if tot is None else tot + val for tot, val in zip(totals, rr)]
            for r_ref, val in zip(r_refs, totals):
                @pl.when(i == 0)
                def _():
                    r_ref[...] = val

                @pl.when(i > 0)
                def _():
                    r_ref[...] += val

        if use_acc:
            acc_ref = rest[-1]
            j = pl.program_id(1)

            @pl.when(j == 0)
            def _():
                acc_ref[...] = part

            @pl.when(jnp.logical_and(j > 0, j < nj - 1))
            def _():
                acc_ref[...] += part

            @pl.when(j == nj - 1)
            def _():
                finish(lambda rows: acc_ref[rows, :] + part[rows])
        else:
            finish(lambda rows: part[rows])

    const2 = lambda g0, g1: (0, 0)
    if full:
        in_specs = [pl.BlockSpec((tm, ka), a_map), pl.BlockSpec(memory_space=pl.ANY)]
        scratch = [pltpu.VMEM((r, nj * c), b3.dtype), pltpu.SemaphoreType.DMA((nj,))]
    else:
        in_specs = [pl.BlockSpec((tm, kb), a_map), pl.BlockSpec((None, r, c), b_map)]
        scratch = [pltpu.VMEM((tm, nb), F32)] if use_acc else []
    in_specs += [pl.BlockSpec((tm, t.shape[1] // nj if out_mode else t.shape[1]), t_map) for t in tiles]
    in_specs += [pl.BlockSpec(p.shape, lambda g0, g1, nd=p.ndim: (0,) * nd) for p in params]
    out_specs = [pl.BlockSpec((tm, nb), t_map) for _ in outs] + [pl.BlockSpec((1, w), const2) for w in reds]
    out_shape = [jax.ShapeDtypeStruct((t_, width), dt) for dt in outs]
    out_shape += [jax.ShapeDtypeStruct((1, w), F32) for w in reds]
    res = pl.pallas_call(
        body, name=name, grid=grid, in_specs=in_specs, out_specs=out_specs, out_shape=out_shape,
        scratch_shapes=scratch, compiler_params=_cp("arbitrary", "arbitrary"),
    )(a, b3, *tiles, *params)
    return res


def _mm_tn(a, g, *, nj, split, name, out_dtype=BF16, a_pre=None):
    t_ = a.shape[0]
    if split == "col":
        r, c = a.shape[1], g.shape[1] // nj
        a_spec = pl.BlockSpec((t_, r), lambda j: (0, 0))
        g_spec = pl.BlockSpec((t_, c), lambda j: (0, j))
    else:
        r, c = a.shape[1] // nj, g.shape[1]
        a_spec = pl.BlockSpec((t_, r), lambda j: (0, j))
        g_spec = pl.BlockSpec((t_, c), lambda j: (0, 0))

    def body(a_ref, g_ref, o_ref):
        a_val = a_ref[...] if a_pre is None else a_pre(a_ref[...])
        o_ref[...] = lax.dot_general(a_val, g_ref[...], (((0,), (0,)), ((), ())),
                                     preferred_element_type=F32).astype(o_ref.dtype)

    return pl.pallas_call(
        body, name=name, grid=(nj,), in_specs=[a_spec, g_spec],
        out_specs=pl.BlockSpec((None, r, c), lambda j: (j, 0, 0)),
        out_shape=jax.ShapeDtypeStruct((nj, r, c), out_dtype),
        compiler_params=_cp("arbitrary"),
    )(a, g)


def _epi_res_norm(acc, tiles, params):
    x_new = tiles[0] + acc
    return (x_new, _rms(x_new, params[0])), ()


def _epi_ple(acc, tiles, params):
    x_old, p_tile = tiles
    g_next, w_ple8 = params
    pe = jnp.concatenate([jnp.dot(p_tile, w_ple8[j], preferred_element_type=F32) for j in range(NDEV)], axis=1)
    x_new = x_old + pe * _sig(acc)
    return (x_new, acc, _rms(x_new, g_next), pe), ()


def _epi_rms_bwd(acc, tiles, params):
    dx, dg = _rms_bwd(acc, tiles[0], params[0], tiles[1])
    return (dx, dx), (dg,)


def _epi_dup(acc, tiles, params):
    return (acc * (2.0 * jnp.maximum(tiles[0].astype(F32), 0.0)),), ()


def _tri_mask():
    row = lax.broadcasted_iota(jnp.int32, (GW, GW), 0)
    col = lax.broadcasted_iota(jnp.int32, (GW, GW), 1)
    return row >= col


def _small_specs(sp_list):
    return [pl.BlockSpec(p.shape, (lambda i: (0, 0)) if p.ndim == 2 else (lambda i: (0, 0, 0))) for p in sp_list]


SUBLANES = 8


def _tap_sum(src, w_ref, taps, rows, stage):
    groups = {}
    for off, k in taps:
        groups.setdefault(off % SUBLANES, []).append((off - off % SUBLANES, k))
    out = None
    for res, members in sorted(groups.items()):
        n = rows if res == 0 else rows + SUBLANES
        part = None
        for base, k in members:
            term = w_ref[k:k + 1, :] * src[pl.ds(base, n), :]
            part = term if part is None else part + term
        if res:
            stage[0:n, :] = part
            part = stage[pl.ds(res, rows), :]
        out = part if out is None else out + part
    return out


def _tap_grads(grad, src, offsets, rows, stage, out_ref):
    pad = SUBLANES
    stage[0:pad, :] = jnp.zeros((pad, grad.shape[1]), F32)
    stage[pad:pad + rows, :] = grad
    stage[pad + rows:2 * pad + rows, :] = jnp.zeros((pad, grad.shape[1]), F32)
    groups = {}
    for k, off in enumerate(offsets):
        groups.setdefault(off % SUBLANES, []).append((off - off % SUBLANES, k))
    for res, members in sorted(groups.items()):
        shifted = stage[pl.ds(pad - res, rows + pad), :]
        for base, k in members:
            out_ref[k:k + 1, :] += _rowsum(shifted * src[pl.ds(base, rows + pad), :])


def _mixer_params(sp):
    return [sp["cw"], sp["cb"], sp["lg"], sp["lb"], sp["pw"], sp["ps"], sp["sc"], sp["gg"], sp["gb"], sp["ws"], sp["bst"]]


def _mixer_fwd(proj, sp, tm=256):
    nt = T // tm
    per = tm // HB

    conv_taps = [(HB - (CONF_K - 1) + k, k) for k in range(CONF_K)]

    def body(main_ref, halo_ref, cw, cb, lg, lb, pw, ps, sc, gg, gb, ws, bst, y_ref, ext, stage):
        i = pl.program_id(0)
        keep = (i > 0).astype(F32)

        def mcol(c0):
            return main_ref[:, c0:c0 + W].astype(F32)

        def hcol(c0):
            return halo_ref[:, c0:c0 + W].astype(F32)

        ext[0:HB, :] = hcol(0) * _sig(hcol(W)) * keep
        ext[HB:HB + tm, :] = mcol(0) * _sig(mcol(W))
        ca = _tap_sum(ext, cw, conv_taps, tm, stage) + cb[...]
        xh, _ = _ln_stats(ca)
        n = xh * lg[...] + lb[...]
        y_ref[:, 0:W] = (n * _sig(n)).astype(BF16)

        pin = mcol(1024)
        ext[0:HB, :] = hcol(1024) * keep
        ext[HB:HB + tm, :] = pin
        pos = (i * tm + lax.broadcasted_iota(jnp.int32, (tm, 1), 0) + 1).astype(F32)
        for g, w in enumerate(POOL_WINDOWS):
            lo = g * GW
            s = ext[pl.ds(HB, tm), lo:lo + GW]
            for j in range(1, w):
                s = s + ext[pl.ds(HB - j, tm), lo:lo + GW]
            pooled = s / jnp.minimum(pos, float(w)) - pin[:, lo:lo + GW]
            mixed = jnp.dot(pooled.astype(BF16), pw[g].astype(BF16), preferred_element_type=F32)
            y_ref[:, W + lo:W + lo + GW] = (mixed * ps[:, lo:lo + GW]).astype(BF16)

        ext[0:HB, :] = hcol(2048) * hcol(2560) * keep
        ext[HB:HB + tm, :] = mcol(2048) * mcol(2560)
        cv = sc[0:1, :] * ext[pl.ds(HB - 2, tm), :]
        cv = cv + sc[1:2, :] * ext[pl.ds(HB - 1, tm), :]
        cv = cv + sc[2:3, :] * ext[pl.ds(HB, tm), :]
        y_ref[:, 2 * W:3 * W] = (mcol(1536) * cv).astype(BF16)

        vh, _ = _ln_stats(mcol(3584))
        vn = (vh * gg[...] + gb[...]).astype(BF16)
        u = mcol(3072)
        tri = _tri_mask()
        for g in range(4):
            lo = g * GW
            wm = jnp.where(tri, ws[g], 0.0).astype(BF16)
            for c in range(tm // GW):
                r0 = c * GW
                sg = jnp.dot(wm, vn[r0:r0 + GW, lo:lo + GW], preferred_element_type=F32) + bst[:, g:g + 1]
                y_ref[r0:r0 + GW, 3 * W + lo:3 * W + lo + GW] = (u[r0:r0 + GW, lo:lo + GW] * sg).astype(BF16)

    plist = _mixer_params(sp)
    in_specs = [pl.BlockSpec((tm, MIX_COLS), lambda i: (i, 0)),
                pl.BlockSpec((HB, MIX_COLS), lambda i: (jnp.maximum(i * per - 1, 0), 0))]
    in_specs += _small_specs(plist)
    return pl.pallas_call(
        body, name="f_mixers", grid=(nt,), in_specs=in_specs,
        out_specs=pl.BlockSpec((tm, 4 * W), lambda i: (i, 0)),
        out_shape=jax.ShapeDtypeStruct((T, 4 * W), BF16),
        scratch_shapes=[pltpu.VMEM((HB + tm, W), F32), pltpu.VMEM((tm + SUBLANES, W), F32)],
        compiler_params=_cp("arbitrary"),
    )(proj, proj, *plist)


def _assemble_wb(wb8_ref, wbf_ref):
    for k in range(4):
        for j in range(NDEV):
            wbf_ref[k, :, j * GW:(j + 1) * GW] = wb8_ref[j, k]


def _merge_fwd(y, proj, wb8, tm=256):
    nt = T // tm

    def body(y_ref, gate_ref, wb8_ref, z_ref, m_ref, wbf):
        @pl.when(pl.program_id(0) == 0)
        def _():
            _assemble_wb(wb8_ref, wbf)

        m = jnp.zeros((tm, D), F32)
        for k in range(4):
            zk = jnp.dot(y_ref[:, k * W:(k + 1) * W], wbf[k], preferred_element_type=F32)
            z_ref[:, k * D:(k + 1) * D] = zk.astype(BF16)
            m = m + _sig(gate_ref[:, k * D:(k + 1) * D].astype(F32)) * zk
        m_ref[...] = m.astype(BF16)

    return pl.pallas_call(
        body, name="f_merge", grid=(nt,),
        in_specs=[pl.BlockSpec((tm, 4 * W), lambda i: (i, 0)),
                  pl.BlockSpec((tm, 4 * D), lambda i: (i, 1)),
                  pl.BlockSpec(wb8.shape, lambda i: (0, 0, 0, 0))],
        out_specs=[pl.BlockSpec((tm, 4 * D), lambda i: (i, 0)), pl.BlockSpec((tm, D), lambda i: (i, 0))],
        out_shape=[jax.ShapeDtypeStruct((T, 4 * D), BF16), jax.ShapeDtypeStruct((T, D), BF16)],
        scratch_shapes=[pltpu.VMEM((4, W, D), BF16)],
        compiler_params=_cp("arbitrary"),
    )(y, proj, wb8)


def _merge_bwd(dm, z, proj, y, wb8, tm=256):
    nt = T // tm

    def body(dm_ref, z_ref, gate_ref, y_ref, wb8_ref, dp_ref, dy_ref, dwb_ref, wbf, acc):
        i = pl.program_id(0)

        @pl.when(i == 0)
        def _():
            _assemble_wb(wb8_ref, wbf)

        dmv = dm_ref[...].astype(F32)
        for k in range(4):
            s = _sig(gate_ref[:, k * D:(k + 1) * D].astype(F32))
            dzk = (dmv * s).astype(BF16)
            dp_ref[:, k * D:(k + 1) * D] = (dmv * z_ref[:, k * D:(k + 1) * D].astype(F32) * s * (1.0 - s)).astype(BF16)
            dyk = lax.dot_general(dzk, wbf[k], (((1,), (1,)), ((), ())), preferred_element_type=F32)
            dy_ref[:, k * W:(k + 1) * W] = dyk.astype(BF16)
            part = lax.dot_general(y_ref[:, k * W:(k + 1) * W], dzk, (((0,), (0,)), ((), ())),
                                   preferred_element_type=F32)

            @pl.when(i == 0)
            def _():
                acc[k] = part

            @pl.when(i > 0)
            def _():
                acc[k] += part

        @pl.when(i == nt - 1)
        def _():
            for k in range(4):
                for j in range(NDEV):
                    dwb_ref[j, k] = acc[k, :, j * GW:(j + 1) * GW].astype(BF16)

    return pl.pallas_call(
        body, name="b_merge", grid=(nt,),
        in_specs=[pl.BlockSpec((tm, D), lambda i: (i, 0)),
                  pl.BlockSpec((tm, 4 * D), lambda i: (i, 0)),
                  pl.BlockSpec((tm, 4 * D), lambda i: (i, 1)),
                  pl.BlockSpec((tm, 4 * W), lambda i: (i, 0)),
                  pl.BlockSpec(wb8.shape, lambda i: (0, 0, 0, 0))],
        out_specs=[pl.BlockSpec((tm, 4 * D), lambda i: (i, 1)),
                   pl.BlockSpec((tm, 4 * W), lambda i: (i, 0)),
                   pl.BlockSpec(wb8.shape, lambda i: (0, 0, 0, 0))],
        out_shape=[jax.ShapeDtypeStruct((T, COLS_IN), BF16),
                   jax.ShapeDtypeStruct((T, 4 * W), BF16),
                   jax.ShapeDtypeStruct(wb8.shape, BF16)],
        scratch_shapes=[pltpu.VMEM((4, W, D), BF16), pltpu.VMEM((4, W, D), F32)],
        compiler_params=_cp("arbitrary"),
    )(dm, z, proj, y, wb8)


def _mixer_bwd(proj, dy, dproj, sp, tm=256):
    nt = T // tm
    per = tm // HB
    ne = tm + HA
    last_blk = T // HA - 1
    conv_taps = [(HB - (CONF_K - 1) + k, k) for k in range(CONF_K)]

    def body(main_ref, hb_ref, ha_ref, dy_ref, dyh_ref, cw, cb, lg, lb, pw, ps, sc, gg, gb, ws, bst, dp_any,
             dp_ref, dcw_ref, dsc_ref, vec_ref, dpw_ref, dws_ref, dbs_ref, e1, e2, e3, stage):
        del dp_any
        i = pl.program_id(0)
        keep_b = (i > 0).astype(F32)
        keep_a = (i < nt - 1).astype(F32)

        @pl.when(i == 0)
        def _():
            dcw_ref[...] = jnp.zeros_like(dcw_ref)
            dsc_ref[...] = jnp.zeros_like(dsc_ref)
            vec_ref[...] = jnp.zeros_like(vec_ref)
            dpw_ref[...] = jnp.zeros_like(dpw_ref)
            dws_ref[...] = jnp.zeros_like(dws_ref)
            dbs_ref[...] = jnp.zeros_like(dbs_ref)

        def mcol(c0):
            return main_ref[:, c0:c0 + W].astype(F32)

        def hbcol(c0):
            return hb_ref[:, c0:c0 + W].astype(F32)

        def hacol(c0):
            return ha_ref[:, c0:c0 + W].astype(F32)

        def load_dy(c0):
            e2[0:tm, :] = dy_ref[:, c0:c0 + W].astype(F32)
            e2[tm:ne, :] = dyh_ref[:, c0:c0 + W].astype(F32) * keep_a

        a = mcol(0)
        sa = _sig(mcol(W))
        e1[0:HB, :] = hbcol(0) * _sig(hbcol(W)) * keep_b
        e1[HB:HB + tm, :] = a * sa
        e1[HB + tm:HB + ne, :] = hacol(0) * _sig(hacol(W))
        ca = _tap_sum(e1, cw, conv_taps, ne, stage) + cb[...]
        xh, rstd = _ln_stats(ca)
        nn = xh * lg[...] + lb[...]
        s = _sig(nn)
        load_dy(0)
        dn = e2[0:ne, :] * (s * (1.0 + nn * (1.0 - s)))
        vec_ref[1:2, :] += _rowsum(dn[0:tm] * xh[0:tm])
        vec_ref[2:3, :] += _rowsum(dn[0:tm])
        dca = _ln_bwd(dn * lg[...], xh, rstd)
        e3[0:ne, :] = dca
        dmain = dca[0:tm]
        vec_ref[0:1, :] += _rowsum(dmain)
        _tap_grads(dmain, e1, [off for off, _ in conv_taps], tm, stage, dcw_ref)
        dglu = _tap_sum(e3, cw, [(CONF_K - 1 - k, k) for k in range(CONF_K)], tm, stage)
        dp_ref[:, 0:W] = (dglu * sa).astype(BF16)
        dp_ref[:, W:2 * W] = (dglu * a * sa * (1.0 - sa)).astype(BF16)

        pin = mcol(1024)
        e1[0:HB, :] = hbcol(1024) * keep_b
        e1[HB:HB + tm, :] = pin
        load_dy(W)
        dyb = e2[0:ne, :]
        pos_m = (i * tm + lax.broadcasted_iota(jnp.int32, (tm, 1), 0) + 1).astype(F32)
        pos_e = (i * tm + lax.broadcasted_iota(jnp.int32, (ne, 1), 0) + 1).astype(F32)
        for g, w in enumerate(POOL_WINDOWS):
            lo = g * GW
            acc = e1[pl.ds(HB, tm), lo:lo + GW]
            for j in range(1, w):
                acc = acc + e1[pl.ds(HB - j, tm), lo:lo + GW]
            pooled = (acc / jnp.minimum(pos_m, float(w)) - pin[:, lo:lo + GW]).astype(BF16)
            pwb = pw[g].astype(BF16)
            mixed = jnp.dot(pooled, pwb, preferred_element_type=F32)
            dyb_g = dyb[:, lo:lo + GW]
            vec_ref[3:4, lo:lo + GW] += _rowsum(dyb_g[0:tm] * mixed)
            dmb = (dyb_g * ps[:, lo:lo + GW]).astype(BF16)
            dpw_ref[g] += lax.dot_general(pooled, dmb[0:tm], (((0,), (0,)), ((), ())), preferred_element_type=F32)
            dpool = lax.dot_general(dmb, pwb, (((1,), (1,)), ((), ())), preferred_element_type=F32)
            e3[0:ne, lo:lo + GW] = dpool / jnp.minimum(pos_e, float(w))
            back = e3[pl.ds(0, tm), lo:lo + GW]
            for j in range(1, w):
                back = back + e3[pl.ds(j, tm), lo:lo + GW]
            dp_ref[:, 1024 + lo:1024 + lo + GW] = (back - dpool[0:tm]).astype(BF16)

        cg = mcol(2048)
        hx = mcol(2560)
        e1[0:HB, :] = hbcol(2048) * hbcol(2560) * keep_b
        e1[HB:HB + tm, :] = cg * hx
        load_dy(2 * W)
        dyc = e2[0:tm, :]
        dconv = dyc * mcol(1536)
        e3[0:tm, :] = dconv
        e3[tm:ne, :] = e2[tm:ne, :] * hacol(1536)
        cv = sc[0:1, :] * e1[pl.ds(HB - 2, tm), :]
        for k in range(1, SC_K):
            cv = cv + sc[k:k + 1, :] * e1[pl.ds(HB - 2 + k, tm), :]
        dp_ref[:, 1536:2048] = (dyc * cv).astype(BF16)
        for k in range(SC_K):
            dsc_ref[k:k + 1, :] += _rowsum(dconv * e1[pl.ds(HB - 2 + k, tm), :])
        dq = sc[0:1, :] * e3[pl.ds(2, tm), :]
        for k in range(1, SC_K):
            dq = dq + sc[k:k + 1, :] * e3[pl.ds(2 - k, tm), :]
        dp_ref[:, 2048:2560] = (dq * hx).astype(BF16)
        dp_ref[:, 2560:3072] = (dq * cg).astype(BF16)

        u = mcol(3072)
        vh, vr = _ln_stats(mcol(3584))
        vn = (vh * gg[...] + gb[...]).astype(BF16)
        dyd = dy_ref[:, 3 * W:4 * W].astype(F32)
        tri = _tri_mask()
        for g in range(4):
            lo = g * GW
            wm = jnp.where(tri, ws[g], 0.0).astype(BF16)
            dws_g = jnp.zeros((GW, GW), F32)
            dbs_g = jnp.zeros((GW, 1), F32)
            for c in range(tm // GW):
                r0 = c * GW
                blk = vn[r0:r0 + GW, lo:lo + GW]
                sg = jnp.dot(wm, blk, preferred_element_type=F32) + bst[:, g:g + 1]
                dyd_b = dyd[r0:r0 + GW, lo:lo + GW]
                dp_ref[r0:r0 + GW, 3072 + lo:3072 + lo + GW] = (dyd_b * sg).astype(BF16)
                dsg = dyd_b * u[r0:r0 + GW, lo:lo + GW]
                dsgb = dsg.astype(BF16)
                dbs_g = dbs_g + jnp.sum(dsg, axis=-1, keepdims=True)
                dws_g = dws_g + lax.dot_general(dsgb, blk, (((1,), (1,)), ((), ())), preferred_element_type=F32)
                e1[r0:r0 + GW, lo:lo + GW] = lax.dot_general(wm, dsgb, (((0,), (0,)), ((), ())),
                                                             preferred_element_type=F32)
            dws_ref[g] += jnp.where(tri, dws_g, 0.0)
            dbs_ref[g] += jnp.broadcast_to(dbs_g, (GW, GW))
        dvn = e1[0:tm, :]
        vec_ref[4:5, :] += _rowsum(dvn * vh)
        vec_ref[5:6, :] += _rowsum(dvn)
        dp_ref[:, 3584:4096] = _ln_bwd(dvn * gg[...], vh, vr).astype(BF16)

    plist = _mixer_params(sp)
    in_specs = [pl.BlockSpec((tm, MIX_COLS), lambda i: (i, 0)),
                pl.BlockSpec((HB, MIX_COLS), lambda i: (jnp.maximum(i * per - 1, 0), 0)),
                pl.BlockSpec((HA, MIX_COLS), lambda i: (jnp.minimum((i + 1) * per, last_blk), 0)),
                pl.BlockSpec((tm, 4 * W), lambda i: (i, 0)),
                pl.BlockSpec((HA, 4 * W), lambda i: (jnp.minimum((i + 1) * per, last_blk), 0))]
    in_specs += _small_specs(plist)
    in_specs += [pl.BlockSpec(memory_space=pl.ANY)]
    z2 = lambda i: (0, 0)
    z3 = lambda i: (0, 0, 0)
    out_specs = [pl.BlockSpec((tm, MIX_COLS), lambda i: (i, 0)),
                 pl.BlockSpec((32, W), z2), pl.BlockSpec((8, W), z2), pl.BlockSpec((8, W), z2),
                 pl.BlockSpec((4, GW, GW), z3), pl.BlockSpec((4, GW, GW), z3), pl.BlockSpec((4, GW, GW), z3)]
    out_shape = [jax.ShapeDtypeStruct((T, COLS_IN), BF16),
                 jax.ShapeDtypeStruct((32, W), F32), jax.ShapeDtypeStruct((8, W), F32),
                 jax.ShapeDtypeStruct((8, W), F32),
                 jax.ShapeDtypeStruct((4, GW, GW), F32), jax.ShapeDtypeStruct((4, GW, GW), F32),
                 jax.ShapeDtypeStruct((4, GW, GW), F32)]
    n_in = 5 + len(plist)
    return pl.pallas_call(
        body, name="b_mixers", grid=(nt,), in_specs=in_specs, out_specs=out_specs, out_shape=out_shape,
        scratch_shapes=[pltpu.VMEM((HB + ne, W), F32), pltpu.VMEM((ne, W), F32), pltpu.VMEM((ne, W), F32),
                        pltpu.VMEM((ne + SUBLANES, W), F32)],
        input_output_aliases={n_in: 0},
        compiler_params=_cp("arbitrary"),
    )(proj, proj, proj, dy, dy, *plist, dproj)


def _norm_first(x, g, tm=512):
    def body(x_ref, g_ref, o_ref):
        o_ref[...] = _rms(x_ref[...], g_ref[...]).astype(BF16)

    return pl.pallas_call(
        body, name="f_norm0", grid=(T // tm,),
        in_specs=[pl.BlockSpec((tm, D), lambda i: (i, 0)), pl.BlockSpec((1, D), lambda i: (0, 0))],
        out_specs=pl.BlockSpec((tm, D), lambda i: (i, 0)),
        out_shape=jax.ShapeDtypeStruct((T, D), BF16), compiler_params=_cp("arbitrary"),
    )(x, g)


def _loss_head(x, target, g, tm=256):
    def body(x_ref, t_ref, g_ref, dx_ref, dg_ref, loss_ref):
        i = pl.program_id(0)
        x = x_ref[...]
        r = lax.rsqrt(jnp.mean(x * x, axis=-1, keepdims=True) + EPS)
        xh = x * r
        gv = g_ref[...]
        e = xh * gv - t_ref[...]
        dyv = e * (1.0 / D)
        part = jnp.sum(_rowsum(e * e), axis=-1, keepdims=True) * (0.5 / D)
        u = dyv * gv
        dx_ref[...] = r * (u - xh * jnp.mean(u * xh, axis=-1, keepdims=True))
        dgp = _rowsum(dyv * xh)

        @pl.when(i == 0)
        def _():
            dg_ref[...] = dgp
            loss_ref[...] = jnp.broadcast_to(part, (1, GW))

        @pl.when(i > 0)
        def _():
            dg_ref[...] += dgp
            loss_ref[...] += jnp.broadcast_to(part, (1, GW))

    return pl.pallas_call(
        body, name="loss_head", grid=(T // tm,),
        in_specs=[pl.BlockSpec((tm, D), lambda i: (i, 0)), pl.BlockSpec((tm, D), lambda i: (i, 0)),
                  pl.BlockSpec((1, D), lambda i: (0, 0))],
        out_specs=[pl.BlockSpec((tm, D), lambda i: (i, 0)), pl.BlockSpec((1, D), lambda i: (0, 0)),
                   pl.BlockSpec((1, GW), lambda i: (0, 0))],
        out_shape=[jax.ShapeDtypeStruct((T, D), F32), jax.ShapeDtypeStruct((1, D), F32),
                   jax.ShapeDtypeStruct((1, GW), F32)],
        compiler_params=_cp("arbitrary"),
    )(x, target, g)


def _ple_bwd_elem(dx, gl, pe, tm=512):
    def body(dx_ref, gl_ref, pe_ref, dpe_ref, dgl_ref):
        d = dx_ref[...]
        s = _sig(gl_ref[...].astype(F32))
        dpe_ref[...] = (d * s).astype(BF16)
        dgl_ref[...] = (d * pe_ref[...].astype(F32) * s * (1.0 - s)).astype(BF16)

    spec = pl.BlockSpec((tm, D), lambda i: (i, 0))
    return pl.pallas_call(
        body, name="b_ple_elem", grid=(T // tm,), in_specs=[spec, spec, spec], out_specs=[spec, spec],
        out_shape=[jax.ShapeDtypeStruct((T, D), BF16), jax.ShapeDtypeStruct((T, D), BF16)],
        compiler_params=_cp("arbitrary"),
    )(dx, gl, pe)


def _layer_fwd(x, h1, p_bf, gw, sp, g_next):
    proj, = _mm(h1, gw["w_in"], mode="out", name="f_proj", outs=[BF16], tm=T)
    y = _mixer_fwd(proj, sp)
    z, merged = _merge_fwd(y, proj, gw["w_branch"])
    x2, h2 = _mm(merged, gw["w_out"].reshape(1, D, D), mode="acc", name="f_out", outs=[F32, BF16], tm=T // 2,
                 tiles=[x], params=[sp["g_mlp"]], epi=_epi_res_norm)
    up, = _mm(h2, gw["w_up"], mode="out", name="f_up", outs=[BF16], tm=T)
    x3, h3 = _mm(up, gw["w_down"].reshape(1, 4 * D, D), mode="acc", name="f_down", outs=[F32, BF16], tm=T // 4,
                 tiles=[x2], params=[sp["g_ple"]], epi=_epi_res_norm, a_pre=_relu2_bf16)
    x4, gl, hn, pe = _mm(h3, gw["w_pleg"].reshape(1, D, D), mode="acc", name="f_gate", tm=T // 2,
                         outs=[F32, BF16, BF16, BF16], tiles=[x3, p_bf], params=[g_next, gw["w_ple"]], epi=_epi_ple)
    saved = dict(x=x, h1=h1, proj=proj, y=y, z=z, merged=merged, x2=x2, h2=h2, up=up, x3=x3, h3=h3,
                 pe=pe, gl=gl, p=p_bf)
    return x4, hn, saved


def _layer_bwd(dx4, sv, gw, sp, submit):
    dpe, dgl = _ple_bwd_elem(dx4, sv["gl"], sv["pe"])
    dw = {}
    dw["w_ple"] = _mm_tn(sv["p"], dpe, nj=NDEV, split="col", name="b_dw_ple")
    dw["w_pleg"] = _mm_tn(sv["h3"], dgl, nj=NDEV, split="row", name="b_dw_pleg")
    dx3, dx3b, dg_ple = _mm(dgl, gw["w_pleg"].reshape(1, D, D), mode="acc", trans_b=True, name="b_dh3", tm=T // 2,
                            outs=[F32, BF16], tiles=[sv["x3"], dx4], params=[sp["g_ple"]], epi=_epi_rms_bwd, reds=[D])
    dup, = _mm(dx3b, gw["w_down"], mode="out", trans_b=True, name="b_dact", outs=[BF16], tm=T,
               tiles=[sv["up"]], epi=_epi_dup)
    dw["w_down"] = _mm_tn(sv["up"], dx3b, nj=NDEV, split="row", name="b_dw_down", a_pre=_relu2_bf16)
    dw["w_up"] = _mm_tn(sv["h2"], dup, nj=NDEV, split="col", name="b_dw_up")
    dx2, dx2b, dg_mlp = _mm(dup, gw["w_up"], mode="full", trans_b=True, name="b_dh2", tm=T // 4,
                            outs=[F32, BF16], tiles=[sv["x2"], dx3], params=[sp["g_mlp"]], epi=_epi_rms_bwd, reds=[D])
    dm, = _mm(dx2b, gw["w_out"].reshape(1, D, D), mode="acc", trans_b=True, name="b_dmerged", outs=[BF16],
              tm=T // 2)
    dw["w_out"] = _mm_tn(sv["merged"], dx2b, nj=NDEV, split="row", name="b_dw_out")
    dproj, dy, dw["w_branch"] = _merge_bwd(dm, sv["z"], sv["proj"], sv["y"], gw["w_branch"])
    dy = submit(dw, BIG[1:], dy)
    dproj, dcw, dsc, vec, dpw, dws, dbs = _mixer_bwd(sv["proj"], dy, dproj, sp)
    dw["w_in"] = _mm_tn(sv["h1"], dproj, nj=NDEV, split="col", name="b_dw_in")
    dw["w_in"], dproj = lax.optimization_barrier((dw["w_in"], dproj))
    dproj = submit(dw, BIG[:1], dproj)
    dx, dg_mix = _mm(dproj, gw["w_in"], mode="full", trans_b=True, name="b_dh1", outs=[F32], tm=T // 4,
                     tiles=[sv["x"], dx2], params=[sp["g_mix"]], epi=_epi_rms_bwd, reds=[D])
    small = dict(norm_mix=dg_mix[0], conf_dw=dcw[:CONF_K], conf_dw_b=vec[0], conf_ln_g=vec[1], conf_ln_b=vec[2],
                 pool_w=dpw, pool_scale=vec[3], sc_conv=dsc[:SC_K], gmlp_ln_g=vec[4], gmlp_ln_b=vec[5],
                 gmlp_ws=dws, gmlp_bs=dbs[:, :, 0], norm_mlp=dg_mlp[0], norm_ple=dg_ple[0])
    return dx, small


ANY = pl.BlockSpec(memory_space=pl.ANY)


def _mesh_pos():
    return lax.axis_index("x"), lax.axis_index("y"), lax.axis_index("c")


def _other_chips(x, y):
    return [(1 - x, y), (x, 1 - y), (1 - x, 1 - y)]


def _launch_comm(body, peers_of, operands, out_shapes, sems, name, seq_id):
    n_in, n_out = len(operands), len(out_shapes)
    if seq_id is None:
        return pl.pallas_call(body, name=name, in_specs=[ANY] * n_in, out_specs=[ANY] * n_out,
                              out_shape=out_shapes, scratch_shapes=sems)(*operands)

    def seq_body(*refs):
        peers = peers_of(*_mesh_pos())
        barrier = pltpu.get_barrier_semaphore()
        for peer in peers:
            pl.semaphore_signal(barrier, inc=1, device_id=peer, device_id_type=MESH)
        pl.semaphore_wait(barrier, len(peers))
        body(*refs)

    return pl.kernel(seq_body, name=name, out_type=out_shapes,
                     mesh=plsc.ScalarSubcoreMesh(axis_name="seq", num_cores=1), scratch_types=sems,
                     compiler_params=pltpu.CompilerParams(collective_id=seq_id))(*operands)


def _all_gather(shards, name, seq_id=None):
    n = len(shards)

    def body(*refs):
        s_refs, o_refs = refs[:n], refs[n:2 * n]
        send_sems, recv_sems, local_sems = refs[2 * n:]
        x, y, c = _mesh_pos()
        me = 4 * x + 2 * y + c
        here = (x, y, c)
        sibling = (x, y, 1 - c)
        chips = _other_chips(x, y)

        def slot(px, py, pc):
            return 4 * px + 2 * py + pc

        def copy(t, k, slot_idx, to, src=None):
            dst = o_refs[t].at[slot_idx]
            return pltpu.make_async_remote_copy(
                src_ref=dst if src is None else src, dst_ref=dst,
                send_sem=send_sems.at[t * 7 + k], recv_sem=recv_sems.at[t * 7 + k],
                device_id=to, device_id_type=MESH)

        mine = [pltpu.make_async_copy(s_refs[t], o_refs[t].at[me], local_sems.at[t]) for t in range(n)]
        for cp in mine:
            cp.start()
        first = []
        for t in range(n):
            for j, chip in enumerate(chips):
                first.append(copy(t, 1 + j, me, (*chip, c), src=s_refs[t]))
        for t in range(n):
            first.append(copy(t, 0, me, sibling, src=s_refs[t]))
        for cp in first:
            cp.start()
        passed = []
        for t in range(n):
            for j, chip in enumerate(chips):
                copy(t, 1 + j, slot(*chip, c), here).wait_recv()
                fwd = copy(t, 4 + j, slot(*chip, c), sibling)
                fwd.start()
                passed.append(fwd)
        for t in range(n):
            copy(t, 0, slot(x, y, 1 - c), here).wait_recv()
            for j, chip in enumerate(chips):
                copy(t, 4 + j, slot(*chip, 1 - c), here).wait_recv()
        for cp in first + passed:
            cp.wait_send()
        for cp in mine:
            cp.wait()

    def peers_of(x, y, c):
        return [(x, y, 1 - c)] + [(*chip, c) for chip in _other_chips(x, y)]

    return _launch_comm(
        body, peers_of, shards, [jax.ShapeDtypeStruct((NDEV,) + s.shape, s.dtype) for s in shards],
        [pltpu.SemaphoreType.DMA((7 * n,)), pltpu.SemaphoreType.DMA((7 * n,)), pltpu.SemaphoreType.DMA((n,))],
        name, seq_id)


def _rs_exchange(p4s, qs, name, seq_id=None):
    n_p, n_q = len(p4s), len(qs)

    def body(*refs):
        p_refs, q_refs = refs[:n_p], refs[n_p:n_p + n_q]
        rb_refs, rc_refs = refs[n_p + n_q:2 * n_p + n_q], refs[2 * n_p + n_q:2 * (n_p + n_q)]
        pair_send, pair_recv, chip_send, chip_recv, local_sems = refs[2 * (n_p + n_q):]
        x, y, c = _mesh_pos()
        a_idx = 2 * x + y
        chips = _other_chips(x, y)
        mine = [pltpu.make_async_copy(q_refs[t].at[a_idx], rc_refs[t].at[a_idx], local_sems.at[t])
                for t in range(n_q)]
        sends = []
        for t in range(n_q):
            for j, chip in enumerate(chips):
                sends.append(pltpu.make_async_remote_copy(
                    src_ref=q_refs[t].at[2 * chip[0] + chip[1]], dst_ref=rc_refs[t].at[a_idx],
                    send_sem=chip_send.at[t * 3 + j], recv_sem=chip_recv.at[t * 3 + j],
                    device_id=(*chip, c), device_id_type=MESH))
        pairs = [pltpu.make_async_remote_copy(
            src_ref=p_refs[t].at[:, 1 - c], dst_ref=rb_refs[t], send_sem=pair_send.at[t], recv_sem=pair_recv.at[t],
            device_id=(x, y, 1 - c), device_id_type=MESH) for t in range(n_p)]
        for cp in sends + mine + pairs:
            cp.start()
        for cp in pairs:
            cp.wait()
        for t in range(n_q):
            for j, chip in enumerate(chips):
                landed = rc_refs[t].at[2 * chip[0] + chip[1]]
                pltpu.make_async_remote_copy(
                    src_ref=landed, dst_ref=landed, send_sem=chip_send.at[t * 3 + j],
                    recv_sem=chip_recv.at[t * 3 + j], device_id=(x, y, c), device_id_type=MESH).wait_recv()
        for cp in sends:
            cp.wait_send()
        for cp in mine:
            cp.wait()

    def peers_of(x, y, c):
        peers = [(x, y, 1 - c)] if n_p else []
        return peers + ([(*chip, c) for chip in _other_chips(x, y)] if n_q else [])

    out_shapes = [jax.ShapeDtypeStruct((NCHIP,) + p.shape[2:], p.dtype) for p in p4s]
    out_shapes += [jax.ShapeDtypeStruct(q.shape, q.dtype) for q in qs]
    sems = [pltpu.SemaphoreType.DMA((max(n_p, 1),)), pltpu.SemaphoreType.DMA((max(n_p, 1),)),
            pltpu.SemaphoreType.DMA((max(3 * n_q, 1),)), pltpu.SemaphoreType.DMA((max(3 * n_q, 1),)),
            pltpu.SemaphoreType.DMA((max(n_q, 1),))]
    got = _launch_comm(body, peers_of, list(p4s) + list(qs), out_shapes, sems, name, seq_id)
    return got[:n_p], got[n_p:]


def _pair_sum(p4s, rbs, c_idx, name, nst=4):
    n = len(p4s)
    trs = [p.shape[2] // nst for p in p4s]

    def body(c_ref, *refs):
        del c_ref
        p_refs, r_refs, o_refs = refs[:n], refs[n:2 * n], refs[2 * n:]
        for p_ref, r_ref, o_ref in zip(p_refs, r_refs, o_refs):
            o_ref[...] = (p_ref[...].astype(F32) + r_ref[...].astype(F32)).astype(o_ref.dtype)

    in_specs = [pl.BlockSpec((None, None, tr, p.shape[3]), lambda b, i, c_ref: (b, c_ref[0], i, 0))
                for p, tr in zip(p4s, trs)]
    in_specs += [pl.BlockSpec((None, tr, p.shape[3]), lambda b, i, c_ref: (b, i, 0)) for p, tr in zip(p4s, trs)]
    out_specs = [pl.BlockSpec((None, tr, p.shape[3]), lambda b, i, c_ref: (b, i, 0)) for p, tr in zip(p4s, trs)]
    return pl.pallas_call(
        body, name=name,
        grid_spec=pltpu.PrefetchScalarGridSpec(num_scalar_prefetch=1, grid=(NCHIP, nst), in_specs=in_specs,
                                               out_specs=out_specs),
        out_shape=[jax.ShapeDtypeStruct((NCHIP,) + p.shape[2:], p.dtype) for p in p4s],
        compiler_params=_cp("arbitrary", "arbitrary"),
    )(c_idx, *p4s, *rbs)


def _reduce_scatter(ps, c_idx, tag, nst=4):
    p4s = [p.reshape((NCHIP, 2) + p.shape[1:]) for p in ps]
    rbs, _ = _rs_exchange(p4s, [], name="rs_pair_" + tag)
    qs = _pair_sum(p4s, rbs, c_idx, name="rs_pairsum_" + tag, nst=nst)
    return _rs_exchange([], qs, name="rs_chip_" + tag)[1]


class _GradientPipeline:
    def __init__(self, c_idx, results):
        self.c_idx, self.results, self.pending = c_idx, results, None

    def _sum_pending(self, chain):
        names, layer, p4s, rbs = self.pending
        qs = _pair_sum(p4s, rbs, self.c_idx, name="rs_pairsum_" + ("first" if len(names) == 1 else "rest"))
        return lax.optimization_barrier((chain, qs))

    def submit(self, dw, names, layer, chain):
        qs, tag, seq_id = [], "pair", 3
        if self.pending is not None:
            chain, qs = self._sum_pending(chain)
            tag, seq_id = "pair_chip", 4
        p4s = [dw[n].reshape((NCHIP, 2) + BIG_SHARD[n]) for n in names]
        rbs, rcs = _rs_exchange(p4s, qs, name="rs_%s_%d" % (tag, len(names)), seq_id=seq_id)
        self._record(rcs)
        self.pending = (names, layer, p4s, rbs)
        return chain

    def finish(self, chain):
        chain, qs = self._sum_pending(chain)
        self._record(_rs_exchange([], qs, name="rs_chip_last", seq_id=5)[1])
        self.pending = None
        return chain

    def _record(self, rcs):
        if rcs:
            names, layer = self.pending[:2]
            for n, rc in zip(names, rcs):
                self.results[n][layer] = rc


def _adamw(w, g, m, v):
    m = ADAM_B1 * m + (1.0 - ADAM_B1) * g
    v = ADAM_B2 * v + (1.0 - ADAM_B2) * (g * g)
    m_hat = m / (1.0 - ADAM_B1 ** ADAM_STEP)
    v_hat = v / (1.0 - ADAM_B2 ** ADAM_STEP)
    delta = -ADAM_LR * (m_hat / (jnp.sqrt(v_hat) + ADAM_EPS) + ADAM_WD * w)
    return delta, m, v


def _adam_sharded(rcs, w, m, v, tr, name):
    _, r, c = w.shape
    nst = r // tr

    def body(rc0, rc1, rc2, rc3, w_ref, m_ref, v_ref, g_out, d_out, m_out, v_out):
        layer = pl.program_id(0)
        for k, rc in enumerate((rc0, rc1, rc2, rc3)):
            @pl.when(layer == k)
            def _():
                g = rc[0].astype(F32) + rc[1].astype(F32) + rc[2].astype(F32) + rc[3].astype(F32)
                delta, m_new, v_new = _adamw(w_ref[...], g, m_ref[...], v_ref[...])
                g_out[...] = g
                d_out[...] = delta
                m_out[...] = m_new
                v_out[...] = v_new

    rc_specs = [pl.BlockSpec((NCHIP, tr, c), lambda l, i, k=k: (0, jnp.where(l == k, i, 0), 0)) for k in range(DEPTH)]
    wspec = pl.BlockSpec((None, tr, c), lambda l, i: (l, i, 0))
    return pl.pallas_call(
        body, name=name, grid=(DEPTH, nst), in_specs=rc_specs + [wspec] * 3, out_specs=[wspec] * 4,
        out_shape=[jax.ShapeDtypeStruct(w.shape, F32)] * 4,
        compiler_params=_cp("arbitrary", "arbitrary"),
    )(*rcs, w, m, v)


def _adam_packed(g, w, m, v, tr=184):
    rows = g.shape[0]

    def body(g_ref, w_ref, m_ref, v_ref, d_out, m_out, v_out):
        delta, m_new, v_new = _adamw(w_ref[...], g_ref[...], m_ref[...], v_ref[...])
        d_out[...] = delta
        m_out[...] = m_new
        v_out[...] = v_new

    spec = pl.BlockSpec((tr, D), lambda i: (i, 0))
    return pl.pallas_call(
        body, name="adam_small", grid=(rows // tr,), in_specs=[spec] * 4, out_specs=[spec] * 3,
        out_shape=[jax.ShapeDtypeStruct(g.shape, F32)] * 3, compiler_params=_cp("arbitrary"),
    )(g, w, m, v)


def _sum4(rc):
    def body(rc_ref, o_ref):
        o_ref[...] = rc_ref[0] + rc_ref[1] + rc_ref[2] + rc_ref[3]

    return pl.pallas_call(
        body, name="small_sum", out_shape=jax.ShapeDtypeStruct(rc.shape[1:], F32),
    )(rc)


BIG = ("w_in", "w_branch", "w_out", "w_up", "w_down", "w_ple", "w_pleg")
BIG_SHARD = {"w_in": (D, D), "w_branch": (4 * W, GW), "w_out": (GW, D), "w_up": (D, W), "w_down": (W, D),
             "w_ple": (256, GW), "w_pleg": (GW, D)}
ADAM_ROWS = {"w_in": 256, "w_branch": 512, "w_out": 128, "w_up": 256, "w_down": 256, "w_ple": 256, "w_pleg": 128}
SMALL = (("norm_mix", (DEPTH, D)), ("conf_dw", (DEPTH, CONF_K, W)), ("conf_dw_b", (DEPTH, W)),
         ("conf_ln_g", (DEPTH, W)), ("conf_ln_b", (DEPTH, W)), ("pool_w", (DEPTH, 4, GW, GW)),
         ("pool_scale", (DEPTH, W)), ("sc_conv", (DEPTH, SC_K, W)), ("gmlp_ln_g", (DEPTH, W)),
         ("gmlp_ln_b", (DEPTH, W)), ("gmlp_ws", (DEPTH, 4, GW, GW)), ("gmlp_bs", (DEPTH, 4, GW)),
         ("norm_mlp", (DEPTH, D)), ("norm_ple", (DEPTH, D)), ("norm_final", (D,)))
CHANNEL_SHARDED = ("conf_dw", "sc_conv")
SMALL_ROWS = 80
PACK_ROWS = 552


def _pack(arrs, rows):
    flat = jnp.concatenate([a.reshape(-1) for a in arrs])
    return jnp.pad(flat, (0, rows * D - flat.shape[0])).reshape(rows, D)


def _unpack(packed, shapes):
    flat = packed.reshape(-1)
    out, off = [], 0
    for shp in shapes:
        size = 1
        for s in shp:
            size *= s
        out.append(flat[off:off + size].reshape(shp))
        off += size
    return out


def kernel(x, p, norm_mix, w_in, conf_dw, conf_dw_b, conf_ln_g, conf_ln_b, pool_w, pool_scale, sc_conv, gmlp_ln_g, gmlp_ln_b, gmlp_ws, gmlp_bs, w_branch, w_out, norm_mlp, w_up, w_down, norm_ple, w_ple, w_ple_gate, norm_final, loss_target, m_norm_mix, m_w_in, m_conf_dw, m_conf_dw_b, m_conf_ln_g, m_conf_ln_b, m_pool_w, m_pool_scale, m_sc_conv, m_gmlp_ln_g, m_gmlp_ln_b, m_gmlp_ws, m_gmlp_bs, m_w_branch, m_w_out, m_norm_mlp, m_w_up, m_w_down, m_norm_ple, m_w_ple, m_w_ple_gate, m_norm_final, v_norm_mix, v_w_in, v_conf_dw, v_conf_dw_b, v_conf_ln_g, v_conf_ln_b, v_pool_w, v_pool_scale, v_sc_conv, v_gmlp_ln_g, v_gmlp_ln_b, v_gmlp_ws, v_gmlp_bs, v_w_branch, v_w_out, v_norm_mlp, v_w_up, v_w_down, v_norm_ple, v_w_ple, v_w_ple_gate, v_norm_final):
    weights = dict(norm_mix=norm_mix, w_in=w_in, conf_dw=conf_dw, conf_dw_b=conf_dw_b, conf_ln_g=conf_ln_g,
                   conf_ln_b=conf_ln_b, pool_w=pool_w, pool_scale=pool_scale, sc_conv=sc_conv, gmlp_ln_g=gmlp_ln_g,
                   gmlp_ln_b=gmlp_ln_b, gmlp_ws=gmlp_ws, gmlp_bs=gmlp_bs, w_branch=w_branch, w_out=w_out,
                   norm_mlp=norm_mlp, w_up=w_up, w_down=w_down, norm_ple=norm_ple, w_ple=w_ple, w_pleg=w_ple_gate,
                   norm_final=norm_final)
    mom1 = dict(norm_mix=m_norm_mix, w_in=m_w_in, conf_dw=m_conf_dw, conf_dw_b=m_conf_dw_b, conf_ln_g=m_conf_ln_g,
                conf_ln_b=m_conf_ln_b, pool_w=m_pool_w, pool_scale=m_pool_scale, sc_conv=m_sc_conv,
                gmlp_ln_g=m_gmlp_ln_g, gmlp_ln_b=m_gmlp_ln_b, gmlp_ws=m_gmlp_ws, gmlp_bs=m_gmlp_bs,
                w_branch=m_w_branch, w_out=m_w_out, norm_mlp=m_norm_mlp, w_up=m_w_up, w_down=m_w_down,
                norm_ple=m_norm_ple, w_ple=m_w_ple, w_pleg=m_w_ple_gate, norm_final=m_norm_final)
    mom2 = dict(norm_mix=v_norm_mix, w_in=v_w_in, conf_dw=v_conf_dw, conf_dw_b=v_conf_dw_b, conf_ln_g=v_conf_ln_g,
                conf_ln_b=v_conf_ln_b, pool_w=v_pool_w, pool_scale=v_pool_scale, sc_conv=v_sc_conv,
                gmlp_ln_g=v_gmlp_ln_g, gmlp_ln_b=v_gmlp_ln_b, gmlp_ws=v_gmlp_ws, gmlp_bs=v_gmlp_bs,
                w_branch=v_w_branch, w_out=v_w_out, norm_mlp=v_norm_mlp, w_up=v_w_up, w_down=v_w_down,
                norm_ple=v_norm_ple, w_ple=v_w_ple, w_pleg=v_w_ple_gate, norm_final=v_norm_final)

    xi, yi, ci = _mesh_pos()
    me = 4 * xi + 2 * yi + ci
    c_idx = jnp.reshape(ci, (1,)).astype(jnp.int32)

    gathered, conf_full, sc_full = [], [], []
    for l in range(DEPTH):
        shard = lambda n: weights[n][l].astype(BF16).reshape(BIG_SHARD[n])
        w_in_g, conf_g, sc_g = _all_gather([shard("w_in"), conf_dw[l], sc_conv[l]], name="ag_first", seq_id=1)
        rest = _all_gather([shard(n) for n in BIG[1:]], name="ag_rest", seq_id=2)
        gw = dict(zip(BIG[1:], rest), w_in=w_in_g)
        gw["w_branch"] = gw["w_branch"].reshape(NDEV, 4, W, GW)
        gathered.append(gw)
        conf_full.append(conf_g)
        sc_full.append(sc_g)

    def small_params(l):
        return dict(cw=conf_full[l], cb=conf_dw_b[l][None], lg=conf_ln_g[l][None], lb=conf_ln_b[l][None],
                    pw=pool_w[l], ps=pool_scale[l][None], sc=sc_full[l], gg=gmlp_ln_g[l][None],
                    gb=gmlp_ln_b[l][None], ws=gmlp_ws[l], bst=gmlp_bs[l].T, g_mix=norm_mix[l][None],
                    g_mlp=norm_mlp[l][None], g_ple=norm_ple[l][None])

    xc = x.reshape(T, D)
    small_names = [n for n, _ in SMALL]
    small_state = [_pack([src[n] for n in small_names], PACK_ROWS) for src in (weights, mom1, mom2)]
    xc, small_state = lax.optimization_barrier((xc, small_state))
    p_bf = p.reshape(DEPTH, T, 256).astype(BF16)
    h = _norm_first(xc, norm_mix[0][None])
    saved = []
    for l in range(DEPTH):
        g_next = norm_mix[l + 1][None] if l + 1 < DEPTH else norm_final[None]
        h, conf_g, sc_g = lax.optimization_barrier((h, conf_full[l], sc_full[l]))
        conf_full[l] = conf_g.transpose(1, 0, 2).reshape(CONF_K, W)
        sc_full[l] = sc_g.transpose(1, 0, 2).reshape(SC_K, W)
        xc, h, sv = _layer_fwd(xc, h, p_bf[l], gathered[l], small_params(l), g_next)
        saved.append(sv)

    dxc, dg_final, loss_part = _loss_head(xc, loss_target.reshape(T, D), norm_final[None])
    loss = lax.psum(loss_part[0, 0], ("x", "y", "c"))
    small_grads = [None] * DEPTH
    rcs = {n: [None] * DEPTH for n in BIG}
    pipeline = _GradientPipeline(c_idx, rcs)
    for l in reversed(range(DEPTH)):
        dxc, small_grads[l] = _layer_bwd(dxc, saved[l], gathered[l], small_params(l),
                                         lambda dw, names, value, l=l: pipeline.submit(dw, names, l, value))

    stacked = {n: jnp.stack([small_grads[l][n] for l in range(DEPTH)]) for n, _ in SMALL if n != "norm_final"}
    stacked["norm_final"] = dg_final[0]
    packed = _pack([stacked[n] for n, _ in SMALL], NDEV * SMALL_ROWS).reshape(NDEV, SMALL_ROWS, D)
    reduced_slot = _sum4(_reduce_scatter([packed], c_idx, "small", nst=1)[0])
    reduced = _all_gather([reduced_slot], name="ag_small")[0]
    small_full = dict(zip([n for n, _ in SMALL], _unpack(reduced, [s for _, s in SMALL])))
    grads, deltas, new_m, new_v = {}, {}, {}, {}
    small_g = {}
    for n, _ in SMALL:
        if n in CHANNEL_SHARDED:
            small_g[n] = lax.dynamic_slice_in_dim(small_full[n], me * (W // NDEV), W // NDEV, axis=2)
        else:
            small_g[n] = small_full[n]
    d_p, m_p, v_p = _adam_packed(_pack([small_g[n] for n in small_names], PACK_ROWS), *small_state)
    small_shapes = [weights[n].shape for n in small_names]
    for n, d_, m_, v_ in zip(small_names, _unpack(d_p, small_shapes), _unpack(m_p, small_shapes),
                             _unpack(v_p, small_shapes)):
        grads[n], deltas[n], new_m[n], new_v[n] = small_g[n], d_, m_, v_

    dxc, _ = pipeline.finish((dxc, d_p))
    for n in BIG:
        shp = (DEPTH,) + BIG_SHARD[n]
        g_, d_, m_, v_ = _adam_sharded(rcs[n], weights[n].reshape(shp), mom1[n].reshape(shp), mom2[n].reshape(shp),
                                       ADAM_ROWS[n], name="adam_" + n)
        full = weights[n].shape
        grads[n], deltas[n], new_m[n], new_v[n] = g_.reshape(full), d_.reshape(full), m_.reshape(full), v_.reshape(full)

    order = ("norm_mix", "w_in", "conf_dw", "conf_dw_b", "conf_ln_g", "conf_ln_b", "pool_w", "pool_scale", "sc_conv",
             "gmlp_ln_g", "gmlp_ln_b", "gmlp_ws", "gmlp_bs", "w_branch", "w_out", "norm_mlp", "w_up", "w_down",
             "norm_ple", "w_ple", "w_pleg", "norm_final")
    return (loss, dxc.reshape(1, T, D), *[grads[n] for n in order], *[deltas[n] for n in order],
            *[new_m[n] for n in order], *[new_v[n] for n in order])
```

```python
import functools

import jax
import jax.numpy as jnp
from jax import lax
from jax.experimental import pallas as pl
from jax.experimental.pallas import tpu as pltpu
from jax.experimental.pallas import tpu_sc as plsc

F32 = jnp.float32
BF16 = jnp.bfloat16

DEPTH = 4
T = 2048
D = 1024
W = 512
NDEV = 8
NCHIP = 4
EPS = 1e-6
CONF_K = 31
SC_K = 3
POOL_WINDOWS = (2, 4, 8, 16)
GW = 128
HB = 32
HA = 32
COLS_IN = 8192
MIX_COLS = 4096

ADAM_LR = 0.001
ADAM_B1 = 0.9
ADAM_B2 = 0.999
ADAM_EPS = 1e-08
ADAM_WD = 0.01
ADAM_STEP = 10

VMEM_LIMIT_BYTES = 56 * 1024 * 1024
MESH = pl.DeviceIdType.MESH


def _cp(*sem):
    return pltpu.CompilerParams(dimension_semantics=tuple(sem), vmem_limit_bytes=VMEM_LIMIT_BYTES)


def _sig(x):
    return jax.nn.sigmoid(x)


def _rms(x, g):
    r = lax.rsqrt(jnp.mean(x * x, axis=-1, keepdims=True) + EPS)
    return x * r * g


def _rms_bwd(dh, x, g, dres):
    r = lax.rsqrt(jnp.mean(x * x, axis=-1, keepdims=True) + EPS)
    xh = x * r
    u = dh * g
    dx = r * (u - xh * jnp.mean(u * xh, axis=-1, keepdims=True)) + dres
    dg = jnp.sum(dh * xh, axis=0, keepdims=True)
    return dx, dg


def _ln_stats(x):
    mu = jnp.mean(x, axis=-1, keepdims=True)
    xc = x - mu
    rstd = lax.rsqrt(jnp.mean(xc * xc, axis=-1, keepdims=True) + EPS)
    return xc * rstd, rstd


def _ln_bwd(dxh, xh, rstd):
    return rstd * (dxh - jnp.mean(dxh, axis=-1, keepdims=True) - xh * jnp.mean(dxh * xh, axis=-1, keepdims=True))


def _rowsum(x):
    return jnp.sum(x, axis=0, keepdims=True)


EPI_ROWS = 256


def _relu2_bf16(up):
    r = jnp.maximum(up.astype(F32), 0.0)
    return (r * r).astype(BF16)


def _mm(a, b3, *, mode, name, outs, trans_b=False, tm=512, tiles=(), params=(), epi=None, reds=(), a_pre=None):
    t_, ka = a.shape
    nj, r, c = b3.shape
    kb, nb = (c, r) if trans_b else (r, c)
    nt = t_ // tm
    out_mode = mode == "out"
    full = mode == "full"
    assert trans_b or not full
    if out_mode:
        assert ka == kb and not reds
        grid = (nj, nt)
        a_map = lambda g0, g1: (g1, 0)
        b_map = lambda g0, g1: (g0, 0, 0)
        t_map = lambda g0, g1: (g1, g0)
        width = nj * nb
    else:
        assert ka == nj * kb
        grid = (nt, 1 if full else nj)
        a_map = lambda g0, g1: (g0, g1)
        b_map = lambda g0, g1: (g1, 0, 0)
        t_map = lambda g0, g1: (g0, 0)
        width = nb
    n_t, n_p, n_o, n_r = len(tiles), len(params), len(outs), len(reds)
    use_acc = (not out_mode) and nj > 1 and not full
    dims = (((1,), (1,)), ((), ())) if trans_b else (((1,), (0,)), ((), ()))

    def body(a_ref, b_ref, *rest):
        t_refs = rest[:n_t]
        p_refs = rest[n_t:n_t + n_p]
        o_refs = rest[n_t + n_p:n_t + n_p + n_o]
        r_refs = rest[n_t + n_p + n_o:n_t + n_p + n_o + n_r]
        i = pl.program_id(1 if out_mode else 0)
        a_val = a_ref[...] if a_pre is None else a_pre(a_ref[...])
        if full:
            b_all, b_sems = rest[-2], rest[-1]

            @pl.when(i == 0)
            def _():
                cps = [pltpu.make_async_copy(b_ref.at[j], b_all.at[:, j * c:(j + 1) * c], b_sems.at[j])
                       for j in range(nj)]
                for cp in cps:
                    cp.start()
                for cp in cps:
                    cp.wait()

            part = lax.dot_general(a_val, b_all[...], dims, preferred_element_type=F32)
        else:
            part = lax.dot_general(a_val, b_ref[...], dims, preferred_element_type=F32)

        def finish(acc_rows):
            totals = [None] * n_r
            for r0 in range(0, tm, min(tm, EPI_ROWS)):
                rows = slice(r0, r0 + min(tm, EPI_ROWS))
                if epi is None:
                    res, rr = (acc_rows(rows),), ()
                else:
                    res, rr = epi(acc_rows(rows), [t[rows, :] for t in t_refs], [p[...] for p in p_refs])
                for o_ref, val in zip(o_refs, res):
                    o_ref[rows, :] = val.astype(o_ref.dtype)
                totals = [val if tot is None else tot + val for tot, val in zip(totals, rr)]
            for r_ref, val in zip(r_refs, totals):
                @pl.when(i == 0)
                def _():
                    r_ref[...] = val

                @pl.when(i > 0)
                def _():
                    r_ref[...] += val

        if use_acc:
            acc_ref = rest[-1]
            j = pl.program_id(1)

            @pl.when(j == 0)
            def _():
                acc_ref[...] = part

            @pl.when(jnp.logical_and(j > 0, j < nj - 1))
            def _():
                acc_ref[...] += part

            @pl.when(j == nj - 1)
            def _():
                finish(lambda rows: acc_ref[rows, :] + part[rows])
        else:
            finish(lambda rows: part[rows])

    const2 = lambda g0, g1: (0, 0)
    if full:
        in_specs = [pl.BlockSpec((tm, ka), a_map), pl.BlockSpec(memory_space=pl.ANY)]
        scratch = [pltpu.VMEM((r, nj * c), b3.dtype), pltpu.SemaphoreType.DMA((nj,))]
    else:
        in_specs = [pl.BlockSpec((tm, kb), a_map), pl.BlockSpec((None, r, c), b_map)]
        scratch = [pltpu.VMEM((tm, nb), F32)] if use_acc else []
    in_specs += [pl.BlockSpec((tm, t.shape[1] // nj if out_mode else t.shape[1]), t_map) for t in tiles]
    in_specs += [pl.BlockSpec(p.shape, lambda g0, g1, nd=p.ndim: (0,) * nd) for p in params]
    out_specs = [pl.BlockSpec((tm, nb), t_map) for _ in outs] + [pl.BlockSpec((1, w), const2) for w in reds]
    out_shape = [jax.ShapeDtypeStruct((t_, width), dt) for dt in outs]
    out_shape += [jax.ShapeDtypeStruct((1, w), F32) for w in reds]
    res = pl.pallas_call(
        body, name=name, grid=grid, in_specs=in_specs, out_specs=out_specs, out_shape=out_shape,
        scratch_shapes=scratch, compiler_params=_cp("arbitrary", "arbitrary"),
    )(a, b3, *tiles, *params)
    return res


def _mm_tn(a, g, *, nj, split, name, out_dtype=BF16, a_pre=None):
    t_ = a.shape[0]
    if split == "col":
        r, c = a.shape[1], g.shape[1] // nj
        a_spec = pl.BlockSpec((t_, r), lambda j: (0, 0))
        g_spec = pl.BlockSpec((t_, c), lambda j: (0, j))
    else:
        r, c = a.shape[1] // nj, g.shape[1]
        a_spec = pl.BlockSpec((t_, r), lambda j: (0, j))
        g_spec = pl.BlockSpec((t_, c), lambda j: (0, 0))

    def body(a_ref, g_ref, o_ref):
        a_val = a_ref[...] if a_pre is None else a_pre(a_ref[...])
        o_ref[...] = lax.dot_general(a_val, g_ref[...], (((0,), (0,)), ((), ())),
                                     preferred_element_type=F32).astype(o_ref.dtype)

    return pl.pallas_call(
        body, name=name, grid=(nj,), in_specs=[a_spec, g_spec],
        out_specs=pl.BlockSpec((None, r, c), lambda j: (j, 0, 0)),
        out_shape=jax.ShapeDtypeStruct((nj, r, c), out_dtype),
        compiler_params=_cp("arbitrary"),
    )(a, g)


def _epi_res_norm(acc, tiles, params):
    x_new = tiles[0] + acc
    return (x_new, _rms(x_new, params[0])), ()


def _epi_ple(acc, tiles, params):
    x_old, p_tile = tiles
    g_next, w_ple8 = params
    pe = jnp.concatenate([jnp.dot(p_tile, w_ple8[j], preferred_element_type=F32) for j in range(NDEV)], axis=1)
    x_new = x_old + pe * _sig(acc)
    return (x_new, acc, _rms(x_new, g_next), pe), ()


def _epi_rms_bwd(acc, tiles, params):
    dx, dg = _rms_bwd(acc, tiles[0], params[0], tiles[1])
    return (dx, dx), (dg,)


def _epi_dup(acc, tiles, params):
    return (acc * (2.0 * jnp.maximum(tiles[0].astype(F32), 0.0)),), ()


def _tri_mask():
    row = lax.broadcasted_iota(jnp.int32, (GW, GW), 0)
    col = lax.broadcasted_iota(jnp.int32, (GW, GW), 1)
    return row >= col


def _small_specs(sp_list):
    return [pl.BlockSpec(p.shape, (lambda i: (0, 0)) if p.ndim == 2 else (lambda i: (0, 0, 0))) for p in sp_list]


SUBLANES = 8


def _tap_sum(src, w_ref, taps, rows, stage):
    groups = {}
    for off, k in taps:
        groups.setdefault(off % SUBLANES, []).append((off - off % SUBLANES, k))
    out = None
    for res, members in sorted(groups.items()):
        n = rows if res == 0 else rows + SUBLANES
        part = None
        for base, k in members:
            term = w_ref[k:k + 1, :] * src[pl.ds(base, n), :]
            part = term if part is None else part + term
        if res:
            stage[0:n, :] = part
            part = stage[pl.ds(res, rows), :]
        out = part if out is None else out + part
    return out


def _tap_grads(grad, src, offsets, rows, stage, out_ref):
    pad = SUBLANES
    stage[0:pad, :] = jnp.zeros((pad, grad.shape[1]), F32)
    stage[pad:pad + rows, :] = grad
    stage[pad + rows:2 * pad + rows, :] = jnp.zeros((pad, grad.shape[1]), F32)
    groups = {}
    for k, off in enumerate(offsets):
        groups.setdefault(off % SUBLANES, []).append((off - off % SUBLANES, k))
    for res, members in sorted(groups.items()):
        shifted = stage[pl.ds(pad - res, rows + pad), :]
        for base, k in members:
            out_ref[k:k + 1, :] += _rowsum(shifted * src[pl.ds(base, rows + pad), :])


def _mixer_params(sp):
    return [sp["cw"], sp["cb"], sp["lg"], sp["lb"], sp["pw"], sp["ps"], sp["sc"], sp["gg"], sp["gb"], sp["ws"], sp["bst"]]


def _mixer_fwd(proj, sp, tm=256):
    nt = T // tm
    per = tm // HB

    conv_taps = [(HB - (CONF_K - 1) + k, k) for k in range(CONF_K)]

    def body(main_ref, halo_ref, cw, cb, lg, lb, pw, ps, sc, gg, gb, ws, bst, y_ref, ca_ref, ext, stage):
        i = pl.program_id(0)
        keep = (i > 0).astype(F32)

        def mcol(c0):
            return main_ref[:, c0:c0 + W].astype(F32)

        def hcol(c0):
            return halo_ref[:, c0:c0 + W].astype(F32)

        ext[0:HB, :] = hcol(0) * _sig(hcol(W)) * keep
        ext[HB:HB + tm, :] = mcol(0) * _sig(mcol(W))
        ca = (_tap_sum(ext, cw, conv_taps, tm, stage) + cb[...]).astype(BF16)
        ca_ref[...] = ca
        xh, _ = _ln_stats(ca.astype(F32))
        n = xh * lg[...] + lb[...]
        y_ref[:, 0:W] = (n * _sig(n)).astype(BF16)

        pin = mcol(1024)
        ext[0:HB, :] = hcol(1024) * keep
        ext[HB:HB + tm, :] = pin
        pos = (i * tm + lax.broadcasted_iota(jnp.int32, (tm, 1), 0) + 1).astype(F32)
        for g, w in enumerate(POOL_WINDOWS):
            lo = g * GW
            s = ext[pl.ds(HB, tm), lo:lo + GW]
            for j in range(1, w):
                s = s + ext[pl.ds(HB - j, tm), lo:lo + GW]
            pooled = s / jnp.minimum(pos, float(w)) - pin[:, lo:lo + GW]
            mixed = jnp.dot(pooled.astype(BF16), pw[g].astype(BF16), preferred_element_type=F32)
            y_ref[:, W + lo:W + lo + GW] = (mixed * ps[:, lo:lo + GW]).astype(BF16)

        ext[0:HB, :] = hcol(2048) * hcol(2560) * keep
        ext[HB:HB + tm, :] = mcol(2048) * mcol(2560)
        cv = sc[0:1, :] * ext[pl.ds(HB - 2, tm), :]
        cv = cv + sc[1:2, :] * ext[pl.ds(HB - 1, tm), :]
        cv = cv + sc[2:3, :] * ext[pl.ds(HB, tm), :]
        y_ref[:, 2 * W:3 * W] = (mcol(1536) * cv).astype(BF16)

        vh, _ = _ln_stats(mcol(3584))
        vn = (vh * gg[...] + gb[...]).astype(BF16)
        u = mcol(3072)
        tri = _tri_mask()
        for g in range(4):
            lo = g * GW
            wm = jnp.where(tri, ws[g], 0.0).astype(BF16)
            for c in range(tm // GW):
                r0 = c * GW
                sg = jnp.dot(wm, vn[r0:r0 + GW, lo:lo + GW], preferred_element_type=F32) + bst[:, g:g + 1]
                y_ref[r0:r0 + GW, 3 * W + lo:3 * W + lo + GW] = (u[r0:r0 + GW, lo:lo + GW] * sg).astype(BF16)

    plist = _mixer_params(sp)
    in_specs = [pl.BlockSpec((tm, MIX_COLS), lambda i: (i, 0)),
                pl.BlockSpec((HB, MIX_COLS), lambda i: (jnp.maximum(i * per - 1, 0), 0))]
    in_specs += _small_specs(plist)
    return pl.pallas_call(
        body, name="f_mixers", grid=(nt,), in_specs=in_specs,
        out_specs=[pl.BlockSpec((tm, 4 * W), lambda i: (i, 0)), pl.BlockSpec((tm, W), lambda i: (i, 0))],
        out_shape=[jax.ShapeDtypeStruct((T, 4 * W), BF16), jax.ShapeDtypeStruct((T, W), BF16)],
        scratch_shapes=[pltpu.VMEM((HB + tm, W), F32), pltpu.VMEM((tm + SUBLANES, W), F32)],
        compiler_params=_cp("arbitrary"),
    )(proj, proj, *plist)


def _assemble_wb(wb8_ref, wbf_ref):
    for k in range(4):
        for j in range(NDEV):
            wbf_ref[k, :, j * GW:(j + 1) * GW] = wb8_ref[j, k]


def _merge_fwd(y, proj, wb8, tm=256):
    nt = T // tm

    def body(y_ref, gate_ref, wb8_ref, z_ref, m_ref, wbf):
        @pl.when(pl.program_id(0) == 0)
        def _():
            _assemble_wb(wb8_ref, wbf)

        m = jnp.zeros((tm, D), F32)
        for k in range(4):
            zk = jnp.dot(y_ref[:, k * W:(k + 1) * W], wbf[k], preferred_element_type=F32)
            z_ref[:, k * D:(k + 1) * D] = zk.astype(BF16)
            m = m + _sig(gate_ref[:, k * D:(k + 1) * D].astype(F32)) * zk
        m_ref[...] = m.astype(BF16)

    return pl.pallas_call(
        body, name="f_merge", grid=(nt,),
        in_specs=[pl.BlockSpec((tm, 4 * W), lambda i: (i, 0)),
                  pl.BlockSpec((tm, 4 * D), lambda i: (i, 1)),
                  pl.BlockSpec(wb8.shape, lambda i: (0, 0, 0, 0))],
        out_specs=[pl.BlockSpec((tm, 4 * D), lambda i: (i, 0)), pl.BlockSpec((tm, D), lambda i: (i, 0))],
        out_shape=[jax.ShapeDtypeStruct((T, 4 * D), BF16), jax.ShapeDtypeStruct((T, D), BF16)],
        scratch_shapes=[pltpu.VMEM((4, W, D), BF16)],
        compiler_params=_cp("arbitrary"),
    )(y, proj, wb8)


def _merge_bwd(dm, z, proj, y, wb8, tm=256):
    nt = T // tm

    def body(dm_ref, z_ref, gate_ref, y_ref, wb8_ref, dp_ref, dy_ref, dwb_ref, wbf, acc):
        i = pl.program_id(0)

        @pl.when(i == 0)
        def _():
            _assemble_wb(wb8_ref, wbf)

        dmv = dm_ref[...].astype(F32)
        for k in range(4):
            s = _sig(gate_ref[:, k * D:(k + 1) * D].astype(F32))
            dzk = (dmv * s).astype(BF16)
            dp_ref[:, k * D:(k + 1) * D] = (dmv * z_ref[:, k * D:(k + 1) * D].astype(F32) * s * (1.0 - s)).astype(BF16)
            dyk = lax.dot_general(dzk, wbf[k], (((1,), (1,)), ((), ())), preferred_element_type=F32)
            dy_ref[:, k * W:(k + 1) * W] = dyk.astype(BF16)
            part = lax.dot_general(y_ref[:, k * W:(k + 1) * W], dzk, (((0,), (0,)), ((), ())),
                                   preferred_element_type=F32)

            @pl.when(i == 0)
            def _():
                acc[k] = part

            @pl.when(i > 0)
            def _():
                acc[k] += part

        @pl.when(i == nt - 1)
        def _():
            for k in range(4):
                for j in range(NDEV):
                    dwb_ref[j, k] = acc[k, :, j * GW:(j + 1) * GW].astype(BF16)

    return pl.pallas_call(
        body, name="b_merge", grid=(nt,),
        in_specs=[pl.BlockSpec((tm, D), lambda i: (i, 0)),
                  pl.BlockSpec((tm, 4 * D), lambda i: (i, 0)),
                  pl.BlockSpec((tm, 4 * D), lambda i: (i, 1)),
                  pl.BlockSpec((tm, 4 * W), lambda i: (i, 0)),
                  pl.BlockSpec(wb8.shape, lambda i: (0, 0, 0, 0))],
        out_specs=[pl.BlockSpec((tm, 4 * D), lambda i: (i, 1)),
                   pl.BlockSpec((tm, 4 * W), lambda i: (i, 0)),
                   pl.BlockSpec(wb8.shape, lambda i: (0, 0, 0, 0))],
        out_shape=[jax.ShapeDtypeStruct((T, COLS_IN), BF16),
                   jax.ShapeDtypeStruct((T, 4 * W), BF16),
                   jax.ShapeDtypeStruct(wb8.shape, BF16)],
        scratch_shapes=[pltpu.VMEM((4, W, D), BF16), pltpu.VMEM((4, W, D), F32)],
        compiler_params=_cp("arbitrary"),
    )(dm, z, proj, y, wb8)


def _mixer_bwd(proj, ca_saved, dy, dproj, sp, tm=256):
    nt = T // tm
    per = tm // HB
    ne = tm + HA
    last_blk = T // HA - 1
    conv_taps = [(HB - (CONF_K - 1) + k, k) for k in range(CONF_K)]

    def body(main_ref, hb_ref, ha_ref, ca_ref, cah_ref, dy_ref, dyh_ref, cw, cb, lg, lb, pw, ps, sc, gg, gb, ws, bst,
             dp_any, dp_ref, dcw_ref, dsc_ref, vec_ref, dpw_ref, dws_ref, dbs_ref, e1, e2, e3, stage):
        del dp_any, cb
        i = pl.program_id(0)
        keep_b = (i > 0).astype(F32)
        keep_a = (i < nt - 1).astype(F32)

        @pl.when(i == 0)
        def _():
            dcw_ref[...] = jnp.zeros_like(dcw_ref)
            dsc_ref[...] = jnp.zeros_like(dsc_ref)
            vec_ref[...] = jnp.zeros_like(vec_ref)
            dpw_ref[...] = jnp.zeros_like(dpw_ref)
            dws_ref[...] = jnp.zeros_like(dws_ref)
            dbs_ref[...] = jnp.zeros_like(dbs_ref)

        def mcol(c0):
            return main_ref[:, c0:c0 + W].astype(F32)

        def hbcol(c0):
            return hb_ref[:, c0:c0 + W].astype(F32)

        def hacol(c0):
            return ha_ref[:, c0:c0 + W].astype(F32)

        def load_dy(c0):
            e2[0:tm, :] = dy_ref[:, c0:c0 + W].astype(F32)
            e2[tm:ne, :] = dyh_ref[:, c0:c0 + W].astype(F32) * keep_a

        a = mcol(0)
        sa = _sig(mcol(W))
        e1[0:HB, :] = hbcol(0) * _sig(hbcol(W)) * keep_b
        e1[HB:HB + tm, :] = a * sa
        e1[HB + tm:HB + tm + SUBLANES, :] = jnp.zeros((SUBLANES, W), F32)
        e2[0:tm, :] = ca_ref[...].astype(F32)
        e2[tm:ne, :] = cah_ref[...].astype(F32)
        xh, rstd = _ln_stats(e2[0:ne, :])
        nn = xh * lg[...] + lb[...]
        s = _sig(nn)
        load_dy(0)
        dn = e2[0:ne, :] * (s * (1.0 + nn * (1.0 - s)))
        vec_ref[1:2, :] += _rowsum(dn[0:tm] * xh[0:tm])
        vec_ref[2:3, :] += _rowsum(dn[0:tm])
        dca = _ln_bwd(dn * lg[...], xh, rstd)
        e3[0:ne, :] = dca
        dmain = dca[0:tm]
        vec_ref[0:1, :] += _rowsum(dmain)
        _tap_grads(dmain, e1, [off for off, _ in conv_taps], tm, stage, dcw_ref)
        dglu = _tap_sum(e3, cw, [(CONF_K - 1 - k, k) for k in range(CONF_K)], tm, stage)
        dp_ref[:, 0:W] = (dglu * sa).astype(BF16)
        dp_ref[:, W:2 * W] = (dglu * a * sa * (1.0 - sa)).astype(BF16)

        pin = mcol(1024)
        e1[0:HB, :] = hbcol(1024) * keep_b
        e1[HB:HB + tm, :] = pin
        load_dy(W)
        dyb = e2[0:ne, :]
        pos_m = (i * tm + lax.broadcasted_iota(jnp.int32, (tm, 1), 0) + 1).astype(F32)
        pos_e = (i * tm + lax.broadcasted_iota(jnp.int32, (ne, 1), 0) + 1).astype(F32)
        for g, w in enumerate(POOL_WINDOWS):
            lo = g * GW
            acc = e1[pl.ds(HB, tm), lo:lo + GW]
            for j in range(1, w):
                acc = acc + e1[pl.ds(HB - j, tm), lo:lo + GW]
            pooled = (acc / jnp.minimum(pos_m, float(w)) - pin[:, lo:lo + GW]).astype(BF16)
            pwb = pw[g].astype(BF16)
            mixed = jnp.dot(pooled, pwb, preferred_element_type=F32)
            dyb_g = dyb[:, lo:lo + GW]
            vec_ref[3:4, lo:lo + GW] += _rowsum(dyb_g[0:tm] * mixed)
            dmb = (dyb_g * ps[:, lo:lo + GW]).astype(BF16)
            dpw_ref[g] += lax.dot_general(pooled, dmb[0:tm], (((0,), (0,)), ((), ())), preferred_element_type=F32)
            dpool = lax.dot_general(dmb, pwb, (((1,), (1,)), ((), ())), preferred_element_type=F32)
            e3[0:ne, lo:lo + GW] = dpool / jnp.minimum(pos_e, float(w))
            back = e3[pl.ds(0, tm), lo:lo + GW]
            for j in range(1, w):
                back = back + e3[pl.ds(j, tm), lo:lo + GW]
            dp_ref[:, 1024 + lo:1024 + lo + GW] = (back - dpool[0:tm]).astype(BF16)

        cg = mcol(2048)
        hx = mcol(2560)
        e1[0:HB, :] = hbcol(2048) * hbcol(2560) * keep_b
        e1[HB:HB + tm, :] = cg * hx
        load_dy(2 * W)
        dyc = e2[0:tm, :]
        dconv = dyc * mcol(1536)
        e3[0:tm, :] = dconv
        e3[tm:ne, :] = e2[tm:ne, :] * hacol(1536)
        cv = sc[0:1, :] * e1[pl.ds(HB - 2, tm), :]
        for k in range(1, SC_K):
            cv = cv + sc[k:k + 1, :] * e1[pl.ds(HB - 2 + k, tm), :]
        dp_ref[:, 1536:2048] = (dyc * cv).astype(BF16)
        for k in range(SC_K):
            dsc_ref[k:k + 1, :] += _rowsum(dconv * e1[pl.ds(HB - 2 + k, tm), :])
        dq = sc[0:1, :] * e3[pl.ds(2, tm), :]
        for k in range(1, SC_K):
            dq = dq + sc[k:k + 1, :] * e3[pl.ds(2 - k, tm), :]
        dp_ref[:, 2048:2560] = (dq * hx).astype(BF16)
        dp_ref[:, 2560:3072] = (dq * cg).astype(BF16)

        u = mcol(3072)
        vh, vr = _ln_stats(mcol(3584))
        vn = (vh * gg[...] + gb[...]).astype(BF16)
        dyd = dy_ref[:, 3 * W:4 * W].astype(F32)
        tri = _tri_mask()
        for g in range(4):
            lo = g * GW
            wm = jnp.where(tri, ws[g], 0.0).astype(BF16)
            dws_g = jnp.zeros((GW, GW), F32)
            dbs_g = jnp.zeros((GW, 1), F32)
            for c in range(tm // GW):
                r0 = c * GW
                blk = vn[r0:r0 + GW, lo:lo + GW]
                sg = jnp.dot(wm, blk, preferred_element_type=F32) + bst[:, g:g + 1]
                dyd_b = dyd[r0:r0 + GW, lo:lo + GW]
                dp_ref[r0:r0 + GW, 3072 + lo:3072 + lo + GW] = (dyd_b * sg).astype(BF16)
                dsg = dyd_b * u[r0:r0 + GW, lo:lo + GW]
                dsgb = dsg.astype(BF16)
                dbs_g = dbs_g + jnp.sum(dsg, axis=-1, keepdims=True)
                dws_g = dws_g + lax.dot_general(dsgb, blk, (((1,), (1,)), ((), ())), preferred_element_type=F32)
                e1[r0:r0 + GW, lo:lo + GW] = lax.dot_general(wm, dsgb, (((0,), (0,)), ((), ())),
                                                             preferred_element_type=F32)
            dws_ref[g] += jnp.where(tri, dws_g, 0.0)
            dbs_ref[g] += jnp.broadcast_to(dbs_g, (GW, GW))
        dvn = e1[0:tm, :]
        vec_ref[4:5, :] += _rowsum(dvn * vh)
        vec_ref[5:6, :] += _rowsum(dvn)
        dp_ref[:, 3584:4096] = _ln_bwd(dvn * gg[...], vh, vr).astype(BF16)

    plist = _mixer_params(sp)
    in_specs = [pl.BlockSpec((tm, MIX_COLS), lambda i: (i, 0)),
                pl.BlockSpec((HB, MIX_COLS), lambda i: (jnp.maximum(i * per - 1, 0), 0)),
                pl.BlockSpec((HA, MIX_COLS), lambda i: (jnp.minimum((i + 1) * per, last_blk), 0)),
                pl.BlockSpec((tm, W), lambda i: (i, 0)),
                pl.BlockSpec((HA, W), lambda i: (jnp.minimum((i + 1) * per, last_blk), 0)),
                pl.BlockSpec((tm, 4 * W), lambda i: (i, 0)),
                pl.BlockSpec((HA, 4 * W), lambda i: (jnp.minimum((i + 1) * per, last_blk), 0))]
    in_specs += _small_specs(plist)
    in_specs += [pl.BlockSpec(memory_space=pl.ANY)]
    z2 = lambda i: (0, 0)
    z3 = lambda i: (0, 0, 0)
    out_specs = [pl.BlockSpec((tm, MIX_COLS), lambda i: (i, 0)),
                 pl.BlockSpec((32, W), z2), pl.BlockSpec((8, W), z2), pl.BlockSpec((8, W), z2),
                 pl.BlockSpec((4, GW, GW), z3), pl.BlockSpec((4, GW, GW), z3), pl.BlockSpec((4, GW, GW), z3)]
    out_shape = [jax.ShapeDtypeStruct((T, COLS_IN), BF16),
                 jax.ShapeDtypeStruct((32, W), F32), jax.ShapeDtypeStruct((8, W), F32),
                 jax.ShapeDtypeStruct((8, W), F32),
                 jax.ShapeDtypeStruct((4, GW, GW), F32), jax.ShapeDtypeStruct((4, GW, GW), F32),
                 jax.ShapeDtypeStruct((4, GW, GW), F32)]
    n_in = 7 + len(plist)
    return pl.pallas_call(
        body, name="b_mixers", grid=(nt,), in_specs=in_specs, out_specs=out_specs, out_shape=out_shape,
        scratch_shapes=[pltpu.VMEM((HB + ne, W), F32), pltpu.VMEM((ne, W), F32), pltpu.VMEM((ne, W), F32),
                        pltpu.VMEM((ne + SUBLANES, W), F32)],
        input_output_aliases={n_in: 0},
        compiler_params=_cp("arbitrary"),
    )(proj, proj, proj, ca_saved, ca_saved, dy, dy, *plist, dproj)


def _norm_first(x, g, tm=512):
    def body(x_ref, g_ref, o_ref):
        o_ref[...] = _rms(x_ref[...], g_ref[...]).astype(BF16)

    return pl.pallas_call(
        body, name="f_norm0", grid=(T // tm,),
        in_specs=[pl.BlockSpec((tm, D), lambda i: (i, 0)), pl.BlockSpec((1, D), lambda i: (0, 0))],
        out_specs=pl.BlockSpec((tm, D), lambda i: (i, 0)),
        out_shape=jax.ShapeDtypeStruct((T, D), BF16), compiler_params=_cp("arbitrary"),
    )(x, g)


def _loss_head(x, target, g, tm=256):
    def body(x_ref, t_ref, g_ref, dx_ref, dg_ref, loss_ref):
        i = pl.program_id(0)
        x = x_ref[...]
        r = lax.rsqrt(jnp.mean(x * x, axis=-1, keepdims=True) + EPS)
        xh = x * r
        gv = g_ref[...]
        e = xh * gv - t_ref[...]
        dyv = e * (1.0 / D)
        part = jnp.sum(_rowsum(e * e), axis=-1, keepdims=True) * (0.5 / D)
        u = dyv * gv
        dx_ref[...] = r * (u - xh * jnp.mean(u * xh, axis=-1, keepdims=True))
        dgp = _rowsum(dyv * xh)

        @pl.when(i == 0)
        def _():
            dg_ref[...] = dgp
            loss_ref[...] = jnp.broadcast_to(part, (1, GW))

        @pl.when(i > 0)
        def _():
            dg_ref[...] += dgp
            loss_ref[...] += jnp.broadcast_to(part, (1, GW))

    return pl.pallas_call(
        body, name="loss_head", grid=(T // tm,),
        in_specs=[pl.BlockSpec((tm, D), lambda i: (i, 0)), pl.BlockSpec((tm, D), lambda i: (i, 0)),
                  pl.BlockSpec((1, D), lambda i: (0, 0))],
        out_specs=[pl.BlockSpec((tm, D), lambda i: (i, 0)), pl.BlockSpec((1, D), lambda i: (0, 0)),
                   pl.BlockSpec((1, GW), lambda i: (0, 0))],
        out_shape=[jax.ShapeDtypeStruct((T, D), F32), jax.ShapeDtypeStruct((1, D), F32),
                   jax.ShapeDtypeStruct((1, GW), F32)],
        compiler_params=_cp("arbitrary"),
    )(x, target, g)


def _ple_bwd_elem(dx, gl, pe, tm=512):
    def body(dx_ref, gl_ref, pe_ref, dpe_ref, dgl_ref):
        d = dx_ref[...]
        s = _sig(gl_ref[...].astype(F32))
        dpe_ref[...] = (d * s).astype(BF16)
        dgl_ref[...] = (d * pe_ref[...].astype(F32) * s * (1.0 - s)).astype(BF16)

    spec = pl.BlockSpec((tm, D), lambda i: (i, 0))
    return pl.pallas_call(
        body, name="b_ple_elem", grid=(T // tm,), in_specs=[spec, spec, spec], out_specs=[spec, spec],
        out_shape=[jax.ShapeDtypeStruct((T, D), BF16), jax.ShapeDtypeStruct((T, D), BF16)],
        compiler_params=_cp("arbitrary"),
    )(dx, gl, pe)


def _layer_fwd(x, h1, p_bf, gw, sp, g_next):
    proj, = _mm(h1, gw["w_in"], mode="out", name="f_proj", outs=[BF16], tm=T)
    y, ca = _mixer_fwd(proj, sp)
    z, merged = _merge_fwd(y, proj, gw["w_branch"])
    x2, h2 = _mm(merged, gw["w_out"].reshape(1, D, D), mode="acc", name="f_out", outs=[F32, BF16], tm=T // 2,
                 tiles=[x], params=[sp["g_mlp"]], epi=_epi_res_norm)
    up, = _mm(h2, gw["w_up"], mode="out", name="f_up", outs=[BF16], tm=T)
    x3, h3 = _mm(up, gw["w_down"].reshape(1, 4 * D, D), mode="acc", name="f_down", outs=[F32, BF16], tm=T // 4,
                 tiles=[x2], params=[sp["g_ple"]], epi=_epi_res_norm, a_pre=_relu2_bf16)
    x4, gl, hn, pe = _mm(h3, gw["w_pleg"].reshape(1, D, D), mode="acc", name="f_gate", tm=T // 2,
                         outs=[F32, BF16, BF16, BF16], tiles=[x3, p_bf], params=[g_next, gw["w_ple"]], epi=_epi_ple)
    saved = dict(x=x, h1=h1, proj=proj, y=y, ca=ca, z=z, merged=merged, x2=x2, h2=h2, up=up, x3=x3, h3=h3,
                 pe=pe, gl=gl, p=p_bf)
    return x4, hn, saved


def _layer_bwd(dx4, sv, gw, sp, submit):
    dpe, dgl = _ple_bwd_elem(dx4, sv["gl"], sv["pe"])
    dw = {}
    dw["w_ple"] = _mm_tn(sv["p"], dpe, nj=NDEV, split="col", name="b_dw_ple")
    dw["w_pleg"] = _mm_tn(sv["h3"], dgl, nj=NDEV, split="row", name="b_dw_pleg")
    dx3, dx3b, dg_ple = _mm(dgl, gw["w_pleg"].reshape(1, D, D), mode="acc", trans_b=True, name="b_dh3", tm=T // 2,
                            outs=[F32, BF16], tiles=[sv["x3"], dx4], params=[sp["g_ple"]], epi=_epi_rms_bwd, reds=[D])
    dup, = _mm(dx3b, gw["w_down"], mode="out", trans_b=True, name="b_dact", outs=[BF16], tm=T,
               tiles=[sv["up"]], epi=_epi_dup)
    dw["w_down"] = _mm_tn(sv["up"], dx3b, nj=NDEV, split="row", name="b_dw_down", a_pre=_relu2_bf16)
    dw["w_up"] = _mm_tn(sv["h2"], dup, nj=NDEV, split="col", name="b_dw_up")
    dx2, dx2b, dg_mlp = _mm(dup, gw["w_up"], mode="full", trans_b=True, name="b_dh2", tm=T // 4,
                            outs=[F32, BF16], tiles=[sv["x2"], dx3], params=[sp["g_mlp"]], epi=_epi_rms_bwd, reds=[D])
    dm, = _mm(dx2b, gw["w_out"].reshape(1, D, D), mode="acc", trans_b=True, name="b_dmerged", outs=[BF16],
              tm=T // 2)
    dw["w_out"] = _mm_tn(sv["merged"], dx2b, nj=NDEV, split="row", name="b_dw_out")
    dproj, dy, dw["w_branch"] = _merge_bwd(dm, sv["z"], sv["proj"], sv["y"], gw["w_branch"])
    dy = submit(dw, BIG[1:], dy)
    dproj, dcw, dsc, vec, dpw, dws, dbs = _mixer_bwd(sv["proj"], sv["ca"], dy, dproj, sp)
    dw["w_in"] = _mm_tn(sv["h1"], dproj, nj=NDEV, split="col", name="b_dw_in")
    dw["w_in"], dproj = lax.optimization_barrier((dw["w_in"], dproj))
    dproj = submit(dw, BIG[:1], dproj)
    dx, dg_mix = _mm(dproj, gw["w_in"], mode="full", trans_b=True, name="b_dh1", outs=[F32], tm=T // 4,
                     tiles=[sv["x"], dx2], params=[sp["g_mix"]], epi=_epi_rms_bwd, reds=[D])
    small = dict(norm_mix=dg_mix[0], conf_dw=dcw[:CONF_K], conf_dw_b=vec[0], conf_ln_g=vec[1], conf_ln_b=vec[2],
                 pool_w=dpw, pool_scale=vec[3], sc_conv=dsc[:SC_K], gmlp_ln_g=vec[4], gmlp_ln_b=vec[5],
                 gmlp_ws=dws, gmlp_bs=dbs[:, :, 0], norm_mlp=dg_mlp[0], norm_ple=dg_ple[0])
    return dx, small


ANY = pl.BlockSpec(memory_space=pl.ANY)


def _mesh_pos():
    return lax.axis_index("x"), lax.axis_index("y"), lax.axis_index("c")


def _other_chips(x, y):
    return [(1 - x, y), (x, 1 - y), (1 - x, 1 - y)]


def _launch_comm(body, peers_of, operands, out_shapes, sems, name, seq_id):
    n_in, n_out = len(operands), len(out_shapes)
    if seq_id is None:
        return pl.pallas_call(body, name=name, in_specs=[ANY] * n_in, out_specs=[ANY] * n_out,
                              out_shape=out_shapes, scratch_shapes=sems)(*operands)

    def seq_body(*refs):
        peers = peers_of(*_mesh_pos())
        barrier = pltpu.get_barrier_semaphore()
        for peer in peers:
            pl.semaphore_signal(barrier, inc=1, device_id=peer, device_id_type=MESH)
        pl.semaphore_wait(barrier, len(peers))
        body(*refs)

    return pl.kernel(seq_body, name=name, out_type=out_shapes,
                     mesh=plsc.ScalarSubcoreMesh(axis_name="seq", num_cores=1), scratch_types=sems,
                     compiler_params=pltpu.CompilerParams(collective_id=seq_id))(*operands)


def _all_gather(shards, name, seq_id=None):
    n = len(shards)

    def body(*refs):
        s_refs, o_refs = refs[:n], refs[n:2 * n]
        send_sems, recv_sems, local_sems = refs[2 * n:]
        x, y, c = _mesh_pos()
        me = 4 * x + 2 * y + c
        here = (x, y, c)
        sibling = (x, y, 1 - c)
        chips = _other_chips(x, y)

        def slot(px, py, pc):
            return 4 * px + 2 * py + pc

        def copy(t, k, slot_idx, to, src=None):
            dst = o_refs[t].at[slot_idx]
            return pltpu.make_async_remote_copy(
                src_ref=dst if src is None else src, dst_ref=dst,
                send_sem=send_sems.at[t * 7 + k], recv_sem=recv_sems.at[t * 7 + k],
                device_id=to, device_id_type=MESH)

        mine = [pltpu.make_async_copy(s_refs[t], o_refs[t].at[me], local_sems.at[t]) for t in range(n)]
        for cp in mine:
            cp.start()
        first = []
        for t in range(n):
            for j, chip in enumerate(chips):
                first.append(copy(t, 1 + j, me, (*chip, c), src=s_refs[t]))
        for t in range(n):
            first.append(copy(t, 0, me, sibling, src=s_refs[t]))
        for cp in first:
            cp.start()
        passed = []
        for t in range(n):
            for j, chip in enumerate(chips):
                copy(t, 1 + j, slot(*chip, c), here).wait_recv()
                fwd = copy(t, 4 + j, slot(*chip, c), sibling)
                fwd.start()
                passed.append(fwd)
        for t in range(n):
            copy(t, 0, slot(x, y, 1 - c), here).wait_recv()
            for j, chip in enumerate(chips):
                copy(t, 4 + j, slot(*chip, 1 - c), here).wait_recv()
        for cp in first + passed:
            cp.wait_send()
        for cp in mine:
            cp.wait()

    def peers_of(x, y, c):
        return [(x, y, 1 - c)] + [(*chip, c) for chip in _other_chips(x, y)]

    return _launch_comm(
        body, peers_of, shards, [jax.ShapeDtypeStruct((NDEV,) + s.shape, s.dtype) for s in shards],
        [pltpu.SemaphoreType.DMA((7 * n,)), pltpu.SemaphoreType.DMA((7 * n,)), pltpu.SemaphoreType.DMA((n,))],
        name, seq_id)


def _rs_exchange(p4s, qs, name, seq_id=None):
    n_p, n_q = len(p4s), len(qs)

    def body(*refs):
        p_refs, q_refs = refs[:n_p], refs[n_p:n_p + n_q]
        rb_refs, rc_refs = refs[n_p + n_q:2 * n_p + n_q], refs[2 * n_p + n_q:2 * (n_p + n_q)]
        pair_send, pair_recv, chip_send, chip_recv, local_sems = refs[2 * (n_p + n_q):]
        x, y, c = _mesh_pos()
        a_idx = 2 * x + y
        chips = _other_chips(x, y)
        mine = [pltpu.make_async_copy(q_refs[t].at[a_idx], rc_refs[t].at[a_idx], local_sems.at[t])
                for t in range(n_q)]
        sends = []
        for t in range(n_q):
            for j, chip in enumerate(chips):
                sends.append(pltpu.make_async_remote_copy(
                    src_ref=q_refs[t].at[2 * chip[0] + chip[1]], dst_ref=rc_refs[t].at[a_idx],
                    send_sem=chip_send.at[t * 3 + j], recv_sem=chip_recv.at[t * 3 + j],
                    device_id=(*chip, c), device_id_type=MESH))
        pairs = [pltpu.make_async_remote_copy(
            src_ref=p_refs[t].at[:, 1 - c], dst_ref=rb_refs[t], send_sem=pair_send.at[t], recv_sem=pair_recv.at[t],
            device_id=(x, y, 1 - c), device_id_type=MESH) for t in range(n_p)]
        for cp in sends + mine + pairs:
            cp.start()
        for cp in pairs:
            cp.wait()
        for t in range(n_q):
            for j, chip in enumerate(chips):
                landed = rc_refs[t].at[2 * chip[0] + chip[1]]
                pltpu.make_async_remote_copy(
                    src_ref=landed, dst_ref=landed, send_sem=chip_send.at[t * 3 + j],
                    recv_sem=chip_recv.at[t * 3 + j], device_id=(x, y, c), device_id_type=MESH).wait_recv()
        for cp in sends:
            cp.wait_send()
        for cp in mine:
            cp.wait()

    def peers_of(x, y, c):
        peers = [(x, y, 1 - c)] if n_p else []
        return peers + ([(*chip, c) for chip in _other_chips(x, y)] if n_q else [])

    out_shapes = [jax.ShapeDtypeStruct((NCHIP,) + p.shape[2:], p.dtype) for p in p4s]
    out_shapes += [jax.ShapeDtypeStruct(q.shape, q.dtype) for q in qs]
    sems = [pltpu.SemaphoreType.DMA((max(n_p, 1),)), pltpu.SemaphoreType.DMA((max(n_p, 1),)),
            pltpu.SemaphoreType.DMA((max(3 * n_q, 1),)), pltpu.SemaphoreType.DMA((max(3 * n_q, 1),)),
            pltpu.SemaphoreType.DMA((max(n_q, 1),))]
    got = _launch_comm(body, peers_of, list(p4s) + list(qs), out_shapes, sems, name, seq_id)
    return got[:n_p], got[n_p:]


def _pair_sum(p4s, rbs, c_idx, name, nst=4):
    n = len(p4s)
    trs = [p.shape[2] // nst for p in p4s]

    def body(c_ref, *refs):
        del c_ref
        p_refs, r_refs, o_refs = refs[:n], refs[n:2 * n], refs[2 * n:]
        for p_ref, r_ref, o_ref in zip(p_refs, r_refs, o_refs):
            o_ref[...] = (p_ref[...].astype(F32) + r_ref[...].astype(F32)).astype(o_ref.dtype)

    in_specs = [pl.BlockSpec((None, None, tr, p.shape[3]), lambda b, i, c_ref: (b, c_ref[0], i, 0))
                for p, tr in zip(p4s, trs)]
    in_specs += [pl.BlockSpec((None, tr, p.shape[3]), lambda b, i, c_ref: (b, i, 0)) for p, tr in zip(p4s, trs)]
    out_specs = [pl.BlockSpec((None, tr, p.shape[3]), lambda b, i, c_ref: (b, i, 0)) for p, tr in zip(p4s, trs)]
    return pl.pallas_call(
        body, name=name,
        grid_spec=pltpu.PrefetchScalarGridSpec(num_scalar_prefetch=1, grid=(NCHIP, nst), in_specs=in_specs,
                                               out_specs=out_specs),
        out_shape=[jax.ShapeDtypeStruct((NCHIP,) + p.shape[2:], p.dtype) for p in p4s],
        compiler_params=_cp("arbitrary", "arbitrary"),
    )(c_idx, *p4s, *rbs)


def _reduce_scatter(ps, c_idx, tag, nst=4):
    p4s = [p.reshape((NCHIP, 2) + p.shape[1:]) for p in ps]
    rbs, _ = _rs_exchange(p4s, [], name="rs_pair_" + tag)
    qs = _pair_sum(p4s, rbs, c_idx, name="rs_pairsum_" + tag, nst=nst)
    return _rs_exchange([], qs, name="rs_chip_" + tag)[1]


class _GradientPipeline:
    def __init__(self, c_idx, results):
        self.c_idx, self.results, self.pending = c_idx, results, None

    def _sum_pending(self, chain):
        names, layer, p4s, rbs = self.pending
        qs = _pair_sum(p4s, rbs, self.c_idx, name="rs_pairsum_" + ("first" if len(names) == 1 else "rest"))
        return lax.optimization_barrier((chain, qs))

    def submit(self, dw, names, layer, chain):
        qs, tag, seq_id = [], "pair", 3
        if self.pending is not None:
            chain, qs = self._sum_pending(chain)
            tag, seq_id = "pair_chip", 4
        p4s = [dw[n].reshape((NCHIP, 2) + BIG_SHARD[n]) for n in names]
        rbs, rcs = _rs_exchange(p4s, qs, name="rs_%s_%d" % (tag, len(names)), seq_id=seq_id)
        self._record(rcs)
        self.pending = (names, layer, p4s, rbs)
        return chain

    def finish(self, chain):
        chain, qs = self._sum_pending(chain)
        self._record(_rs_exchange([], qs, name="rs_chip_last", seq_id=5)[1])
        self.pending = None
        return chain

    def _record(self, rcs):
        if rcs:
            names, layer = self.pending[:2]
            for n, rc in zip(names, rcs):
                self.results[n][layer] = rc


def _adamw(w, g, m, v):
    m = ADAM_B1 * m + (1.0 - ADAM_B1) * g
    v = ADAM_B2 * v + (1.0 - ADAM_B2) * (g * g)
    m_hat = m / (1.0 - ADAM_B1 ** ADAM_STEP)
    v_hat = v / (1.0 - ADAM_B2 ** ADAM_STEP)
    delta = -ADAM_LR * (m_hat / (jnp.sqrt(v_hat) + ADAM_EPS) + ADAM_WD * w)
    return delta, m, v


def _adam_sharded(rcs, w, m, v, tr, name, first_layer, partial=None):
    _, r, c = w.shape
    nst = r // tr
    n_l = len(rcs)

    def body(*refs):
        rc_refs = refs[:n_l]
        w_ref, m_ref, v_ref = refs[n_l:n_l + 3]
        g_out, d_out, m_out, v_out = refs[-4:]
        layer = pl.program_id(0)
        for k, rc in enumerate(rc_refs):
            @pl.when(layer == k)
            def _():
                g = rc[0].astype(F32) + rc[1].astype(F32) + rc[2].astype(F32) + rc[3].astype(F32)
                delta, m_new, v_new = _adamw(w_ref[...], g, m_ref[...], v_ref[...])
                g_out[...] = g
                d_out[...] = delta
                m_out[...] = m_new
                v_out[...] = v_new

    rc_specs = [pl.BlockSpec((NCHIP, tr, c), lambda l, i, k=k: (0, jnp.where(l == k, i, 0), 0)) for k in range(n_l)]
    wspec = pl.BlockSpec((None, tr, c), lambda l, i: (first_layer + l, i, 0))
    carried = [] if partial is None else list(partial)
    return pl.pallas_call(
        body, name=name, grid=(n_l, nst),
        in_specs=rc_specs + [wspec] * 3 + [pl.BlockSpec(memory_space=pl.ANY)] * len(carried),
        out_specs=[wspec] * 4, out_shape=[jax.ShapeDtypeStruct(w.shape, F32)] * 4,
        input_output_aliases={n_l + 3 + k: k for k in range(len(carried))},
        compiler_params=_cp("arbitrary", "arbitrary"),
    )(*rcs, w, m, v, *carried)


def _adam_packed(g, w, m, v, tr=184):
    rows = g.shape[0]

    def body(g_ref, w_ref, m_ref, v_ref, d_out, m_out, v_out):
        delta, m_new, v_new = _adamw(w_ref[...], g_ref[...], m_ref[...], v_ref[...])
        d_out[...] = delta
        m_out[...] = m_new
        v_out[...] = v_new

    spec = pl.BlockSpec((tr, D), lambda i: (i, 0))
    return pl.pallas_call(
        body, name="adam_small", grid=(rows // tr,), in_specs=[spec] * 4, out_specs=[spec] * 3,
        out_shape=[jax.ShapeDtypeStruct(g.shape, F32)] * 3, compiler_params=_cp("arbitrary"),
    )(g, w, m, v)


def _sum4(rc):
    def body(rc_ref, o_ref):
        o_ref[...] = rc_ref[0] + rc_ref[1] + rc_ref[2] + rc_ref[3]

    return pl.pallas_call(
        body, name="small_sum", out_shape=jax.ShapeDtypeStruct(rc.shape[1:], F32),
    )(rc)


BIG = ("w_in", "w_branch", "w_out", "w_up", "w_down", "w_ple", "w_pleg")
BIG_SHARD = {"w_in": (D, D), "w_branch": (4 * W, GW), "w_out": (GW, D), "w_up": (D, W), "w_down": (W, D),
             "w_ple": (256, GW), "w_pleg": (GW, D)}
ADAM_ROWS = {"w_in": 256, "w_branch": 512, "w_out": 128, "w_up": 256, "w_down": 256, "w_ple": 256, "w_pleg": 128}
SMALL = (("norm_mix", (DEPTH, D)), ("conf_dw", (DEPTH, CONF_K, W)), ("conf_dw_b", (DEPTH, W)),
         ("conf_ln_g", (DEPTH, W)), ("conf_ln_b", (DEPTH, W)), ("pool_w", (DEPTH, 4, GW, GW)),
         ("pool_scale", (DEPTH, W)), ("sc_conv", (DEPTH, SC_K, W)), ("gmlp_ln_g", (DEPTH, W)),
         ("gmlp_ln_b", (DEPTH, W)), ("gmlp_ws", (DEPTH, 4, GW, GW)), ("gmlp_bs", (DEPTH, 4, GW)),
         ("norm_mlp", (DEPTH, D)), ("norm_ple", (DEPTH, D)), ("norm_final", (D,)))
CHANNEL_SHARDED = ("conf_dw", "sc_conv")
SMALL_ROWS = 80
PACK_ROWS = 552


def _pack(arrs, rows):
    flat = jnp.concatenate([a.reshape(-1) for a in arrs])
    return jnp.pad(flat, (0, rows * D - flat.shape[0])).reshape(rows, D)


def _unpack(packed, shapes):
    flat = packed.reshape(-1)
    out, off = [], 0
    for shp in shapes:
        size = 1
        for s in shp:
            size *= s
        out.append(flat[off:off + size].reshape(shp))
        off += size
    return out


def kernel(x, p, norm_mix, w_in, conf_dw, conf_dw_b, conf_ln_g, conf_ln_b, pool_w, pool_scale, sc_conv, gmlp_ln_g, gmlp_ln_b, gmlp_ws, gmlp_bs, w_branch, w_out, norm_mlp, w_up, w_down, norm_ple, w_ple, w_ple_gate, norm_final, loss_target, m_norm_mix, m_w_in, m_conf_dw, m_conf_dw_b, m_conf_ln_g, m_conf_ln_b, m_pool_w, m_pool_scale, m_sc_conv, m_gmlp_ln_g, m_gmlp_ln_b, m_gmlp_ws, m_gmlp_bs, m_w_branch, m_w_out, m_norm_mlp, m_w_up, m_w_down, m_norm_ple, m_w_ple, m_w_ple_gate, m_norm_final, v_norm_mix, v_w_in, v_conf_dw, v_conf_dw_b, v_conf_ln_g, v_conf_ln_b, v_pool_w, v_pool_scale, v_sc_conv, v_gmlp_ln_g, v_gmlp_ln_b, v_gmlp_ws, v_gmlp_bs, v_w_branch, v_w_out, v_norm_mlp, v_w_up, v_w_down, v_norm_ple, v_w_ple, v_w_ple_gate, v_norm_final):
    weights = dict(norm_mix=norm_mix, w_in=w_in, conf_dw=conf_dw, conf_dw_b=conf_dw_b, conf_ln_g=conf_ln_g,
                   conf_ln_b=conf_ln_b, pool_w=pool_w, pool_scale=pool_scale, sc_conv=sc_conv, gmlp_ln_g=gmlp_ln_g,
                   gmlp_ln_b=gmlp_ln_b, gmlp_ws=gmlp_ws, gmlp_bs=gmlp_bs, w_branch=w_branch, w_out=w_out,
                   norm_mlp=norm_mlp, w_up=w_up, w_down=w_down, norm_ple=norm_ple, w_ple=w_ple, w_pleg=w_ple_gate,
                   norm_final=norm_final)
    mom1 = dict(norm_mix=m_norm_mix, w_in=m_w_in, conf_dw=m_conf_dw, conf_dw_b=m_conf_dw_b, conf_ln_g=m_conf_ln_g,
                conf_ln_b=m_conf_ln_b, pool_w=m_pool_w, pool_scale=m_pool_scale, sc_conv=m_sc_conv,
                gmlp_ln_g=m_gmlp_ln_g, gmlp_ln_b=m_gmlp_ln_b, gmlp_ws=m_gmlp_ws, gmlp_bs=m_gmlp_bs,
                w_branch=m_w_branch, w_out=m_w_out, norm_mlp=m_norm_mlp, w_up=m_w_up, w_down=m_w_down,
                norm_ple=m_norm_ple, w_ple=m_w_ple, w_pleg=m_w_ple_gate, norm_final=m_norm_final)
    mom2 = dict(norm_mix=v_norm_mix, w_in=v_w_in, conf_dw=v_conf_dw, conf_dw_b=v_conf_dw_b, conf_ln_g=v_conf_ln_g,
                conf_ln_b=v_conf_ln_b, pool_w=v_pool_w, pool_scale=v_pool_scale, sc_conv=v_sc_conv,
                gmlp_ln_g=v_gmlp_ln_g, gmlp_ln_b=v_gmlp_ln_b, gmlp_ws=v_gmlp_ws, gmlp_bs=v_gmlp_bs,
                w_branch=v_w_branch, w_out=v_w_out, norm_mlp=v_norm_mlp, w_up=v_w_up, w_down=v_w_down,
                norm_ple=v_norm_ple, w_ple=v_w_ple, w_pleg=v_w_ple_gate, norm_final=v_norm_final)

    xi, yi, ci = _mesh_pos()
    me = 4 * xi + 2 * yi + ci
    c_idx = jnp.reshape(ci, (1,)).astype(jnp.int32)

    gathered, conf_full, sc_full = [], [], []
    for l in range(DEPTH):
        shard = lambda n: weights[n][l].astype(BF16).reshape(BIG_SHARD[n])
        w_in_g, conf_g, sc_g = _all_gather([shard("w_in"), conf_dw[l], sc_conv[l]], name="ag_first", seq_id=1)
        rest = _all_gather([shard(n) for n in BIG[1:]], name="ag_rest", seq_id=2)
        gw = dict(zip(BIG[1:], rest), w_in=w_in_g)
        gw["w_branch"] = gw["w_branch"].reshape(NDEV, 4, W, GW)
        gathered.append(gw)
        conf_full.append(conf_g)
        sc_full.append(sc_g)

    def small_params(l):
        return dict(cw=conf_full[l], cb=conf_dw_b[l][None], lg=conf_ln_g[l][None], lb=conf_ln_b[l][None],
                    pw=pool_w[l], ps=pool_scale[l][None], sc=sc_full[l], gg=gmlp_ln_g[l][None],
                    gb=gmlp_ln_b[l][None], ws=gmlp_ws[l], bst=gmlp_bs[l].T, g_mix=norm_mix[l][None],
                    g_mlp=norm_mlp[l][None], g_ple=norm_ple[l][None])

    xc = x.reshape(T, D)
    small_names = [n for n, _ in SMALL]
    small_state = [_pack([src[n] for n in small_names], PACK_ROWS) for src in (weights, mom1, mom2)]
    xc, small_state = lax.optimization_barrier((xc, small_state))
    p_bf = p.reshape(DEPTH, T, 256).astype(BF16)
    h = _norm_first(xc, norm_mix[0][None])
    saved = []
    for l in range(DEPTH):
        g_next = norm_mix[l + 1][None] if l + 1 < DEPTH else norm_final[None]
        h, conf_g, sc_g = lax.optimization_barrier((h, conf_full[l], sc_full[l]))
        conf_full[l] = conf_g.transpose(1, 0, 2).reshape(CONF_K, W)
        sc_full[l] = sc_g.transpose(1, 0, 2).reshape(SC_K, W)
        xc, h, sv = _layer_fwd(xc, h, p_bf[l], gathered[l], small_params(l), g_next)
        saved.append(sv)

    dxc, dg_final, loss_part = _loss_head(xc, loss_target.reshape(T, D), norm_final[None])
    loss = lax.psum(loss_part[0, 0], ("x", "y", "c"))
    small_grads = [None] * DEPTH
    rcs = {n: [None] * DEPTH for n in BIG}
    pipeline = _GradientPipeline(c_idx, rcs)
    for l in reversed(range(DEPTH)):
        dxc, small_grads[l] = _layer_bwd(dxc, saved[l], gathered[l], small_params(l),
                                         lambda dw, names, value, l=l: pipeline.submit(dw, names, l, value))

    def adam_sharded(first_layer, n_layers, partial, tag):
        outs = {}
        for n in BIG:
            shp = (DEPTH,) + BIG_SHARD[n]
            outs[n] = _adam_sharded(rcs[n][first_layer:first_layer + n_layers], weights[n].reshape(shp),
                                    mom1[n].reshape(shp), mom2[n].reshape(shp), ADAM_ROWS[n],
                                    "adam_%s_%s" % (n, tag), first_layer, None if partial is None else partial[n])
        return outs

    dxc, upper = lax.optimization_barrier((dxc, {n: rcs[n][1:] for n in BIG}))
    for n in BIG:
        rcs[n][1:] = upper[n]
    partial = adam_sharded(1, DEPTH - 1, None, "upper")
    dxc, partial = pipeline.finish((dxc, partial))

    stacked = {n: jnp.stack([small_grads[l][n] for l in range(DEPTH)]) for n, _ in SMALL if n != "norm_final"}
    stacked["norm_final"] = dg_final[0]
    packed = _pack([stacked[n] for n, _ in SMALL], NDEV * SMALL_ROWS).reshape(NDEV, SMALL_ROWS, D)
    packed, dxc = lax.optimization_barrier((packed, dxc))
    reduced_slot = _sum4(_reduce_scatter([packed], c_idx, "small", nst=1)[0])
    reduced = _all_gather([reduced_slot], name="ag_small")[0]
    small_full = dict(zip([n for n, _ in SMALL], _unpack(reduced, [s for _, s in SMALL])))
    grads, deltas, new_m, new_v = {}, {}, {}, {}
    small_g = {}
    for n, _ in SMALL:
        if n in CHANNEL_SHARDED:
            small_g[n] = lax.dynamic_slice_in_dim(small_full[n], me * (W // NDEV), W // NDEV, axis=2)
        else:
            small_g[n] = small_full[n]
    d_p, m_p, v_p = _adam_packed(_pack([small_g[n] for n in small_names], PACK_ROWS), *small_state)
    small_shapes = [weights[n].shape for n in small_names]
    for n, d_, m_, v_ in zip(small_names, _unpack(d_p, small_shapes), _unpack(m_p, small_shapes),
                             _unpack(v_p, small_shapes)):
        grads[n], deltas[n], new_m[n], new_v[n] = small_g[n], d_, m_, v_

    for n, (g_, d_, m_, v_) in adam_sharded(0, 1, partial, "last").items():
        full = weights[n].shape
        grads[n], deltas[n], new_m[n], new_v[n] = g_.reshape(full), d_.reshape(full), m_.reshape(full), v_.reshape(full)

    order = ("norm_mix", "w_in", "conf_dw", "conf_dw_b", "conf_ln_g", "conf_ln_b", "pool_w", "pool_scale", "sc_conv",
             "gmlp_ln_g", "gmlp_ln_b", "gmlp_ws", "gmlp_bs", "w_branch", "w_out", "norm_mlp", "w_up", "w_down",
             "norm_ple", "w_ple", "w_pleg", "norm_final")
    return (loss, dxc.reshape(1, T, D), *[grads[n] for n in order], *[deltas[n] for n in order],
            *[new_m[n] for n in order], *[new_v[n] for n in order])
```

```python
import functools

import jax
import jax.numpy as jnp
from jax import lax
from jax.experimental import pallas as pl
from jax.experimental.pallas import tpu as pltpu
from jax.experimental.pallas import tpu_sc as plsc

F32 = jnp.float32
BF16 = jnp.bfloat16

DEPTH = 4
T = 2048
D = 1024
W = 512
NDEV = 8
NCHIP = 4
EPS = 1e-6
CONF_K = 31
SC_K = 3
POOL_WINDOWS = (2, 4, 8, 16)
GW = 128
HB = 32
HA = 32
COLS_IN = 8192
MIX_COLS = 4096

ADAM_LR = 0.001
ADAM_B1 = 0.9
ADAM_B2 = 0.999
ADAM_EPS = 1e-08
ADAM_WD = 0.01
ADAM_STEP = 10

VMEM_LIMIT_BYTES = 56 * 1024 * 1024
MESH = pl.DeviceIdType.MESH


def _cp(*sem):
    return pltpu.CompilerParams(dimension_semantics=tuple(sem), vmem_limit_bytes=VMEM_LIMIT_BYTES)


def _sig(x):
    return jax.nn.sigmoid(x)


def _rms(x, g):
    r = lax.rsqrt(jnp.mean(x * x, axis=-1, keepdims=True) + EPS)
    return x * r * g


def _rms_bwd(dh, x, g, dres):
    r = lax.rsqrt(jnp.mean(x * x, axis=-1, keepdims=True) + EPS)
    xh = x * r
    u = dh * g
    dx = r * (u - xh * jnp.mean(u * xh, axis=-1, keepdims=True)) + dres
    dg = jnp.sum(dh * xh, axis=0, keepdims=True)
    return dx, dg


def _ln_stats(x):
    mu = jnp.mean(x, axis=-1, keepdims=True)
    xc = x - mu
    rstd = lax.rsqrt(jnp.mean(xc * xc, axis=-1, keepdims=True) + EPS)
    return xc * rstd, rstd


def _ln_bwd(dxh, xh, rstd):
    return rstd * (dxh - jnp.mean(dxh, axis=-1, keepdims=True) - xh * jnp.mean(dxh * xh, axis=-1, keepdims=True))


def _rowsum(x):
    return jnp.sum(x, axis=0, keepdims=True)


EPI_ROWS = 256


def _relu2_bf16(up):
    r = jnp.maximum(up.astype(F32), 0.0)
    return (r * r).astype(BF16)


def _mm(a, b3, *, mode, name, outs, trans_b=False, tm=512, tiles=(), params=(), epi=None, reds=(), a_pre=None):
    t_, ka = a.shape
    nj, r, c = b3.shape
    kb, nb = (c, r) if trans_b else (r, c)
    nt = t_ // tm
    out_mode = mode == "out"
    full = mode == "full"
    assert trans_b or not full
    if out_mode:
        assert ka == kb and not reds
        grid = (nj, nt)
        a_map = lambda g0, g1: (g1, 0)
        b_map = lambda g0, g1: (g0, 0, 0)
        t_map = lambda g0, g1: (g1, g0)
        width = nj * nb
    else:
        assert ka == nj * kb
        grid = (nt, 1 if full else nj)
        a_map = lambda g0, g1: (g0, g1)
        b_map = lambda g0, g1: (g1, 0, 0)
        t_map = lambda g0, g1: (g0, 0)
        width = nb
    n_t, n_p, n_o, n_r = len(tiles), len(params), len(outs), len(reds)
    use_acc = (not out_mode) and nj > 1 and not full
    dims = (((1,), (1,)), ((), ())) if trans_b else (((1,), (0,)), ((), ()))

    def body(a_ref, b_ref, *rest):
        t_refs = rest[:n_t]
        p_refs = rest[n_t:n_t + n_p]
        o_refs = rest[n_t + n_p:n_t + n_p + n_o]
        r_refs = rest[n_t + n_p + n_o:n_t + n_p + n_o + n_r]
        i = pl.program_id(1 if out_mode else 0)
        a_val = a_ref[...] if a_pre is None else a_pre(a_ref[...])
        if full:
            b_all, b_sems = rest[-2], rest[-1]

            @pl.when(i == 0)
            def _():
                cps = [pltpu.make_async_copy(b_ref.at[j], b_all.at[:, j * c:(j + 1) * c], b_sems.at[j])
                       for j in range(nj)]
                for cp in cps:
                    cp.start()
                for cp in cps:
                    cp.wait()

            part = lax.dot_general(a_val, b_all[...], dims, preferred_element_type=F32)
        else:
            part = lax.dot_general(a_val, b_ref[...], dims, preferred_element_type=F32)

        def finish(acc_rows):
            totals = [None] * n_r
            for r0 in range(0, tm, min(tm, EPI_ROWS)):
                rows = slice(r0, r0 + min(tm, EPI_ROWS))
                if epi is None:
                    res, rr = (acc_rows(rows),), ()
                else:
                    res, rr = epi(acc_rows(rows), [t[rows, :] for t in t_refs], [p[...] for p in p_refs])
                for o_ref, val in zip(o_refs, res):
                    o_ref[rows, :] = val.astype(o_ref.dtype)
                totals = [val if tot is None else tot + val for tot, val in zip(totals, rr)]
            for r_ref, val in zip(r_refs, totals):
                @pl.when(i == 0)
                def _():
                    r_ref[...] = val

                @pl.when(i > 0)
                def _():
                    r_ref[...] += val

        if use_acc:
            acc_ref = rest[-1]
            j = pl.program_id(1)

            @pl.when(j == 0)
            def _():
                acc_ref[...] = part

            @pl.when(jnp.logical_and(j > 0, j < nj - 1))
            def _():
                acc_ref[...] += part

            @pl.when(j == nj - 1)
            def _():
                finish(lambda rows: acc_ref[rows, :] + part[rows])
        else:
            finish(lambda rows: part[rows])

    const2 = lambda g0, g1: (0, 0)
    if full:
        in_specs = [pl.BlockSpec((tm, ka), a_map), pl.BlockSpec(memory_space=pl.ANY)]
        scratch = [pltpu.VMEM((r, nj * c), b3.dtype), pltpu.SemaphoreType.DMA((nj,))]
    else:
        in_specs = [pl.BlockSpec((tm, kb), a_map), pl.BlockSpec((None, r, c), b_map)]
        scratch = [pltpu.VMEM((tm, nb), F32)] if use_acc else []
    in_specs += [pl.BlockSpec((tm, t.shape[1] // nj if out_mode else t.shape[1]), t_map) for t in tiles]
    in_specs += [pl.BlockSpec(p.shape, lambda g0, g1, nd=p.ndim: (0,) * nd) for p in params]
    out_specs = [pl.BlockSpec((tm, nb), t_map) for _ in outs] + [pl.BlockSpec((1, w), const2) for w in reds]
    out_shape = [jax.ShapeDtypeStruct((t_, width), dt) for dt in outs]
    out_shape += [jax.ShapeDtypeStruct((1, w), F32) for w in reds]
    res = pl.pallas_call(
        body, name=name, grid=grid, in_specs=in_specs, out_specs=out_specs, out_shape=out_shape,
        scratch_shapes=scratch, compiler_params=_cp("arbitrary", "arbitrary"),
    )(a, b3, *tiles, *params)
    return res


def _mm_tn(a, g, *, nj, split, name, out_dtype=BF16, a_pre=None):
    t_ = a.shape[0]
    if split == "col":
        r, c = a.shape[1], g.shape[1] // nj
        a_spec = pl.BlockSpec((t_, r), lambda j: (0, 0))
        g_spec = pl.BlockSpec((t_, c), lambda j: (0, j))
    else:
        r, c = a.shape[1] // nj, g.shape[1]
        a_spec = pl.BlockSpec((t_, r), lambda j: (0, j))
        g_spec = pl.BlockSpec((t_, c), lambda j: (0, 0))

    def body(a_ref, g_ref, o_ref):
        a_val = a_ref[...] if a_pre is None else a_pre(a_ref[...])
        o_ref[...] = lax.dot_general(a_val, g_ref[...], (((0,), (0,)), ((), ())),
                                     preferred_element_type=F32).astype(o_ref.dtype)

    return pl.pallas_call(
        body, name=name, grid=(nj,), in_specs=[a_spec, g_spec],
        out_specs=pl.BlockSpec((None, r, c), lambda j: (j, 0, 0)),
        out_shape=jax.ShapeDtypeStruct((nj, r, c), out_dtype),
        compiler_params=_cp("arbitrary"),
    )(a, g)


def _epi_res_norm(acc, tiles, params):
    x_new = tiles[0] + acc
    return (x_new, _rms(x_new, params[0])), ()


def _epi_ple(acc, tiles, params):
    x_old, p_tile = tiles
    g_next, w_ple8 = params
    pe = jnp.concatenate([jnp.dot(p_tile, w_ple8[j], preferred_element_type=F32) for j in range(NDEV)], axis=1)
    x_new = x_old + pe * _sig(acc)
    return (x_new, acc, _rms(x_new, g_next), pe), ()


def _epi_rms_bwd(acc, tiles, params):
    dx, dg = _rms_bwd(acc, tiles[0], params[0], tiles[1])
    return (dx, dx), (dg,)


def _epi_dup(acc, tiles, params):
    return (acc * (2.0 * jnp.maximum(tiles[0].astype(F32), 0.0)),), ()


def _tri_mask():
    row = lax.broadcasted_iota(jnp.int32, (GW, GW), 0)
    col = lax.broadcasted_iota(jnp.int32, (GW, GW), 1)
    return row >= col


def _small_specs(sp_list):
    return [pl.BlockSpec(p.shape, (lambda i: (0, 0)) if p.ndim == 2 else (lambda i: (0, 0, 0))) for p in sp_list]


SUBLANES = 8


def _tap_sum(src, w_ref, taps, rows, stage):
    groups = {}
    for off, k in taps:
        groups.setdefault(off % SUBLANES, []).append((off - off % SUBLANES, k))
    out = None
    for res, members in sorted(groups.items()):
        n = rows if res == 0 else rows + SUBLANES
        part = None
        for base, k in members:
            term = w_ref[k:k + 1, :] * src[pl.ds(base, n), :]
            part = term if part is None else part + term
        if res:
            stage[0:n, :] = part
            part = stage[pl.ds(res, rows), :]
        out = part if out is None else out + part
    return out


def _tap_grads(grad, src, offsets, rows, stage, out_ref):
    pad = SUBLANES
    stage[0:pad, :] = jnp.zeros((pad, grad.shape[1]), F32)
    stage[pad:pad + rows, :] = grad
    stage[pad + rows:2 * pad + rows, :] = jnp.zeros((pad, grad.shape[1]), F32)
    groups = {}
    for k, off in enumerate(offsets):
        groups.setdefault(off % SUBLANES, []).append((off - off % SUBLANES, k))
    for res, members in sorted(groups.items()):
        shifted = stage[pl.ds(pad - res, rows + pad), :]
        for base, k in members:
            out_ref[k:k + 1, :] += _rowsum(shifted * src[pl.ds(base, rows + pad), :])


def _mixer_params(sp):
    return [sp["cw"], sp["cb"], sp["lg"], sp["lb"], sp["pw"], sp["ps"], sp["sc"], sp["gg"], sp["gb"], sp["ws"], sp["bst"]]


def _mixer_fwd(proj, sp, tm=256):
    nt = T // tm
    per = tm // HB

    conv_taps = [(HB - (CONF_K - 1) + k, k) for k in range(CONF_K)]

    def body(main_ref, halo_ref, cw, cb, lg, lb, pw, ps, sc, gg, gb, ws, bst, y_ref, ca_ref, ext, stage):
        i = pl.program_id(0)
        keep = (i > 0).astype(F32)

        def mcol(c0):
            return main_ref[:, c0:c0 + W].astype(F32)

        def hcol(c0):
            return halo_ref[:, c0:c0 + W].astype(F32)

        ext[0:HB, :] = hcol(0) * _sig(hcol(W)) * keep
        ext[HB:HB + tm, :] = mcol(0) * _sig(mcol(W))
        ca = (_tap_sum(ext, cw, conv_taps, tm, stage) + cb[...]).astype(BF16)
        ca_ref[...] = ca
        xh, _ = _ln_stats(ca.astype(F32))
        n = xh * lg[...] + lb[...]
        y_ref[:, 0:W] = (n * _sig(n)).astype(BF16)

        pin = mcol(1024)
        ext[0:HB, :] = hcol(1024) * keep
        ext[HB:HB + tm, :] = pin
        pos = (i * tm + lax.broadcasted_iota(jnp.int32, (tm, 1), 0) + 1).astype(F32)
        for g, w in enumerate(POOL_WINDOWS):
            lo = g * GW
            s = ext[pl.ds(HB, tm), lo:lo + GW]
            for j in range(1, w):
                s = s + ext[pl.ds(HB - j, tm), lo:lo + GW]
            pooled = s / jnp.minimum(pos, float(w)) - pin[:, lo:lo + GW]
            mixed = jnp.dot(pooled.astype(BF16), pw[g].astype(BF16), preferred_element_type=F32)
            y_ref[:, W + lo:W + lo + GW] = (mixed * ps[:, lo:lo + GW]).astype(BF16)

        ext[0:HB, :] = hcol(2048) * hcol(2560) * keep
        ext[HB:HB + tm, :] = mcol(2048) * mcol(2560)
        cv = sc[0:1, :] * ext[pl.ds(HB - 2, tm), :]
        cv = cv + sc[1:2, :] * ext[pl.ds(HB - 1, tm), :]
        cv = cv + sc[2:3, :] * ext[pl.ds(HB, tm), :]
        y_ref[:, 2 * W:3 * W] = (mcol(1536) * cv).astype(BF16)

        vh, _ = _ln_stats(mcol(3584))
        vn = (vh * gg[...] + gb[...]).astype(BF16)
        u = mcol(3072)
        tri = _tri_mask()
        for g in range(4):
            lo = g * GW
            wm = jnp.where(tri, ws[g], 0.0).astype(BF16)
            for c in range(tm // GW):
                r0 = c * GW
                sg = jnp.dot(wm, vn[r0:r0 + GW, lo:lo + GW], preferred_element_type=F32) + bst[:, g:g + 1]
                y_ref[r0:r0 + GW, 3 * W + lo:3 * W + lo + GW] = (u[r0:r0 + GW, lo:lo + GW] * sg).astype(BF16)

    plist = _mixer_params(sp)
    in_specs = [pl.BlockSpec((tm, MIX_COLS), lambda i: (i, 0)),
                pl.BlockSpec((HB, MIX_COLS), lambda i: (jnp.maximum(i * per - 1, 0), 0))]
    in_specs += _small_specs(plist)
    return pl.pallas_call(
        body, name="f_mixers", grid=(nt,), in_specs=in_specs,
        out_specs=[pl.BlockSpec((tm, 4 * W), lambda i: (i, 0)), pl.BlockSpec((tm, W), lambda i: (i, 0))],
        out_shape=[jax.ShapeDtypeStruct((T, 4 * W), BF16), jax.ShapeDtypeStruct((T, W), BF16)],
        scratch_shapes=[pltpu.VMEM((HB + tm, W), F32), pltpu.VMEM((tm + SUBLANES, W), F32)],
        compiler_params=_cp("arbitrary"),
    )(proj, proj, *plist)


def _assemble_wb(wb8_ref, wbf_ref):
    for k in range(4):
        for j in range(NDEV):
            wbf_ref[k, :, j * GW:(j + 1) * GW] = wb8_ref[j, k]


def _merge_fwd(y, proj, wb8, tm=256):
    nt = T // tm

    def body(y_ref, gate_ref, wb8_ref, z_ref, m_ref, wbf):
        @pl.when(pl.program_id(0) == 0)
        def _():
            _assemble_wb(wb8_ref, wbf)

        m = jnp.zeros((tm, D), F32)
        for k in range(4):
            zk = jnp.dot(y_ref[:, k * W:(k + 1) * W], wbf[k], preferred_element_type=F32)
            z_ref[:, k * D:(k + 1) * D] = zk.astype(BF16)
            m = m + _sig(gate_ref[:, k * D:(k + 1) * D].astype(F32)) * zk
        m_ref[...] = m.astype(BF16)

    return pl.pallas_call(
        body, name="f_merge", grid=(nt,),
        in_specs=[pl.BlockSpec((tm, 4 * W), lambda i: (i, 0)),
                  pl.BlockSpec((tm, 4 * D), lambda i: (i, 1)),
                  pl.BlockSpec(wb8.shape, lambda i: (0, 0, 0, 0))],
        out_specs=[pl.BlockSpec((tm, 4 * D), lambda i: (i, 0)), pl.BlockSpec((tm, D), lambda i: (i, 0))],
        out_shape=[jax.ShapeDtypeStruct((T, 4 * D), BF16), jax.ShapeDtypeStruct((T, D), BF16)],
        scratch_shapes=[pltpu.VMEM((4, W, D), BF16)],
        compiler_params=_cp("arbitrary"),
    )(y, proj, wb8)


def _merge_bwd(dm, z, proj, y, wb8, tm=256):
    nt = T // tm

    def body(dm_ref, z_ref, gate_ref, y_ref, wb8_ref, dp_ref, dy_ref, dwb_ref, wbf, acc):
        i = pl.program_id(0)

        @pl.when(i == 0)
        def _():
            _assemble_wb(wb8_ref, wbf)

        dmv = dm_ref[...].astype(F32)
        for k in range(4):
            s = _sig(gate_ref[:, k * D:(k + 1) * D].astype(F32))
            dzk = (dmv * s).astype(BF16)
            dp_ref[:, k * D:(k + 1) * D] = (dmv * z_ref[:, k * D:(k + 1) * D].astype(F32) * s * (1.0 - s)).astype(BF16)
            dyk = lax.dot_general(dzk, wbf[k], (((1,), (1,)), ((), ())), preferred_element_type=F32)
            dy_ref[:, k * W:(k + 1) * W] = dyk.astype(BF16)
            part = lax.dot_general(y_ref[:, k * W:(k + 1) * W], dzk, (((0,), (0,)), ((), ())),
                                   preferred_element_type=F32)

            @pl.when(i == 0)
            def _():
                acc[k] = part

            @pl.when(i > 0)
            def _():
                acc[k] += part

        @pl.when(i == nt - 1)
        def _():
            for k in range(4):
                for j in range(NDEV):
                    dwb_ref[j, k] = acc[k, :, j * GW:(j + 1) * GW].astype(BF16)

    return pl.pallas_call(
        body, name="b_merge", grid=(nt,),
        in_specs=[pl.BlockSpec((tm, D), lambda i: (i, 0)),
                  pl.BlockSpec((tm, 4 * D), lambda i: (i, 0)),
                  pl.BlockSpec((tm, 4 * D), lambda i: (i, 1)),
                  pl.BlockSpec((tm, 4 * W), lambda i: (i, 0)),
                  pl.BlockSpec(wb8.shape, lambda i: (0, 0, 0, 0))],
        out_specs=[pl.BlockSpec((tm, 4 * D), lambda i: (i, 1)),
                   pl.BlockSpec((tm, 4 * W), lambda i: (i, 0)),
                   pl.BlockSpec(wb8.shape, lambda i: (0, 0, 0, 0))],
        out_shape=[jax.ShapeDtypeStruct((T, COLS_IN), BF16),
                   jax.ShapeDtypeStruct((T, 4 * W), BF16),
                   jax.ShapeDtypeStruct(wb8.shape, BF16)],
        scratch_shapes=[pltpu.VMEM((4, W, D), BF16), pltpu.VMEM((4, W, D), F32)],
        compiler_params=_cp("arbitrary"),
    )(dm, z, proj, y, wb8)


def _mixer_bwd(proj, ca_saved, dy, dproj, sp, tm=256):
    nt = T // tm
    per = tm // HB
    ne = tm + HA
    last_blk = T // HA - 1
    conv_taps = [(HB - (CONF_K - 1) + k, k) for k in range(CONF_K)]

    def body(main_ref, hb_ref, ha_ref, ca_ref, cah_ref, dy_ref, dyh_ref, cw, cb, lg, lb, pw, ps, sc, gg, gb, ws, bst,
             dp_any, dp_ref, dcw_ref, dsc_ref, vec_ref, dpw_ref, dws_ref, dbs_ref, e1, e2, e3, stage):
        del dp_any, cb
        i = pl.program_id(0)
        keep_b = (i > 0).astype(F32)
        keep_a = (i < nt - 1).astype(F32)

        @pl.when(i == 0)
        def _():
            dcw_ref[...] = jnp.zeros_like(dcw_ref)
            dsc_ref[...] = jnp.zeros_like(dsc_ref)
            vec_ref[...] = jnp.zeros_like(vec_ref)
            dpw_ref[...] = jnp.zeros_like(dpw_ref)
            dws_ref[...] = jnp.zeros_like(dws_ref)
            dbs_ref[...] = jnp.zeros_like(dbs_ref)

        def mcol(c0):
            return main_ref[:, c0:c0 + W].astype(F32)

        def hbcol(c0):
            return hb_ref[:, c0:c0 + W].astype(F32)

        def hacol(c0):
            return ha_ref[:, c0:c0 + W].astype(F32)

        def load_dy(c0):
            e2[0:tm, :] = dy_ref[:, c0:c0 + W].astype(F32)
            e2[tm:ne, :] = dyh_ref[:, c0:c0 + W].astype(F32) * keep_a

        a = mcol(0)
        sa = _sig(mcol(W))
        e1[0:HB, :] = hbcol(0) * _sig(hbcol(W)) * keep_b
        e1[HB:HB + tm, :] = a * sa
        e1[HB + tm:HB + tm + SUBLANES, :] = jnp.zeros((SUBLANES, W), F32)
        e2[0:tm, :] = ca_ref[...].astype(F32)
        e2[tm:ne, :] = cah_ref[...].astype(F32)
        xh, rstd = _ln_stats(e2[0:ne, :])
        nn = xh * lg[...] + lb[...]
        s = _sig(nn)
        load_dy(0)
        dn = e2[0:ne, :] * (s * (1.0 + nn * (1.0 - s)))
        vec_ref[1:2, :] += _rowsum(dn[0:tm] * xh[0:tm])
        vec_ref[2:3, :] += _rowsum(dn[0:tm])
        dca = _ln_bwd(dn * lg[...], xh, rstd)
        e3[0:ne, :] = dca
        dmain = dca[0:tm]
        vec_ref[0:1, :] += _rowsum(dmain)
        _tap_grads(dmain, e1, [off for off, _ in conv_taps], tm, stage, dcw_ref)
        dglu = _tap_sum(e3, cw, [(CONF_K - 1 - k, k) for k in range(CONF_K)], tm, stage)
        dp_ref[:, 0:W] = (dglu * sa).astype(BF16)
        dp_ref[:, W:2 * W] = (dglu * a * sa * (1.0 - sa)).astype(BF16)

        pin = mcol(1024)
        e1[0:HB, :] = hbcol(1024) * keep_b
        e1[HB:HB + tm, :] = pin
        load_dy(W)
        dyb = e2[0:ne, :]
        pos_m = (i * tm + lax.broadcasted_iota(jnp.int32, (tm, 1), 0) + 1).astype(F32)
        pos_e = (i * tm + lax.broadcasted_iota(jnp.int32, (ne, 1), 0) + 1).astype(F32)
        for g, w in enumerate(POOL_WINDOWS):
            lo = g * GW
            acc = e1[pl.ds(HB, tm), lo:lo + GW]
            for j in range(1, w):
                acc = acc + e1[pl.ds(HB - j, tm), lo:lo + GW]
            pooled = (acc / jnp.minimum(pos_m, float(w)) - pin[:, lo:lo + GW]).astype(BF16)
            pwb = pw[g].astype(BF16)
            mixed = jnp.dot(pooled, pwb, preferred_element_type=F32)
            dyb_g = dyb[:, lo:lo + GW]
            vec_ref[3:4, lo:lo + GW] += _rowsum(dyb_g[0:tm] * mixed)
            dmb = (dyb_g * ps[:, lo:lo + GW]).astype(BF16)
            dpw_ref[g] += lax.dot_general(pooled, dmb[0:tm], (((0,), (0,)), ((), ())), preferred_element_type=F32)
            dpool = lax.dot_general(dmb, pwb, (((1,), (1,)), ((), ())), preferred_element_type=F32)
            e3[0:ne, lo:lo + GW] = dpool / jnp.minimum(pos_e, float(w))
            back = e3[pl.ds(0, tm), lo:lo + GW]
            for j in range(1, w):
                back = back + e3[pl.ds(j, tm), lo:lo + GW]
            dp_ref[:, 1024 + lo:1024 + lo + GW] = (back - dpool[0:tm]).astype(BF16)

        cg = mcol(2048)
        hx = mcol(2560)
        e1[0:HB, :] = hbcol(2048) * hbcol(2560) * keep_b
        e1[HB:HB + tm, :] = cg * hx
        load_dy(2 * W)
        dyc = e2[0:tm, :]
        dconv = dyc * mcol(1536)
        e3[0:tm, :] = dconv
        e3[tm:ne, :] = e2[tm:ne, :] * hacol(1536)
        cv = sc[0:1, :] * e1[pl.ds(HB - 2, tm), :]
        for k in range(1, SC_K):
            cv = cv + sc[k:k + 1, :] * e1[pl.ds(HB - 2 + k, tm), :]
        dp_ref[:, 1536:2048] = (dyc * cv).astype(BF16)
        for k in range(SC_K):
            dsc_ref[k:k + 1, :] += _rowsum(dconv * e1[pl.ds(HB - 2 + k, tm), :])
        dq = sc[0:1, :] * e3[pl.ds(2, tm), :]
        for k in range(1, SC_K):
            dq = dq + sc[k:k + 1, :] * e3[pl.ds(2 - k, tm), :]
        dp_ref[:, 2048:2560] = (dq * hx).astype(BF16)
        dp_ref[:, 2560:3072] = (dq * cg).astype(BF16)

        u = mcol(3072)
        vh, vr = _ln_stats(mcol(3584))
        vn = (vh * gg[...] + gb[...]).astype(BF16)
        dyd = dy_ref[:, 3 * W:4 * W].astype(F32)
        tri = _tri_mask()
        for g in range(4):
            lo = g * GW
            wm = jnp.where(tri, ws[g], 0.0).astype(BF16)
            dws_g = jnp.zeros((GW, GW), F32)
            dbs_g = jnp.zeros((GW, 1), F32)
            for c in range(tm // GW):
                r0 = c * GW
                blk = vn[r0:r0 + GW, lo:lo + GW]
                sg = jnp.dot(wm, blk, preferred_element_type=F32) + bst[:, g:g + 1]
                dyd_b = dyd[r0:r0 + GW, lo:lo + GW]
                dp_ref[r0:r0 + GW, 3072 + lo:3072 + lo + GW] = (dyd_b * sg).astype(BF16)
                dsg = dyd_b * u[r0:r0 + GW, lo:lo + GW]
                dsgb = dsg.astype(BF16)
                dbs_g = dbs_g + jnp.sum(dsg, axis=-1, keepdims=True)
                dws_g = dws_g + lax.dot_general(dsgb, blk, (((1,), (1,)), ((), ())), preferred_element_type=F32)
                e1[r0:r0 + GW, lo:lo + GW] = lax.dot_general(wm, dsgb, (((0,), (0,)), ((), ())),
                                                             preferred_element_type=F32)
            dws_ref[g] += jnp.where(tri, dws_g, 0.0)
            dbs_ref[g] += jnp.broadcast_to(dbs_g, (GW, GW))
        dvn = e1[0:tm, :]
        vec_ref[4:5, :] += _rowsum(dvn * vh)
        vec_ref[5:6, :] += _rowsum(dvn)
        dp_ref[:, 3584:4096] = _ln_bwd(dvn * gg[...], vh, vr).astype(BF16)

    plist = _mixer_params(sp)
    in_specs = [pl.BlockSpec((tm, MIX_COLS), lambda i: (i, 0)),
                pl.BlockSpec((HB, MIX_COLS), lambda i: (jnp.maximum(i * per - 1, 0), 0)),
                pl.BlockSpec((HA, MIX_COLS), lambda i: (jnp.minimum((i + 1) * per, last_blk), 0)),
                pl.BlockSpec((tm, W), lambda i: (i, 0)),
                pl.BlockSpec((HA, W), lambda i: (jnp.minimum((i + 1) * per, last_blk), 0)),
                pl.BlockSpec((tm, 4 * W), lambda i: (i, 0)),
                pl.BlockSpec((HA, 4 * W), lambda i: (jnp.minimum((i + 1) * per, last_blk), 0))]
    in_specs += _small_specs(plist)
    in_specs += [pl.BlockSpec(memory_space=pl.ANY)]
    z2 = lambda i: (0, 0)
    z3 = lambda i: (0, 0, 0)
    out_specs = [pl.BlockSpec((tm, MIX_COLS), lambda i: (i, 0)),
                 pl.BlockSpec((32, W), z2), pl.BlockSpec((8, W), z2), pl.BlockSpec((8, W), z2),
                 pl.BlockSpec((4, GW, GW), z3), pl.BlockSpec((4, GW, GW), z3), pl.BlockSpec((4, GW, GW), z3)]
    out_shape = [jax.ShapeDtypeStruct((T, COLS_IN), BF16),
                 jax.ShapeDtypeStruct((32, W), F32), jax.ShapeDtypeStruct((8, W), F32),
                 jax.ShapeDtypeStruct((8, W), F32),
                 jax.ShapeDtypeStruct((4, GW, GW), F32), jax.ShapeDtypeStruct((4, GW, GW), F32),
                 jax.ShapeDtypeStruct((4, GW, GW), F32)]
    n_in = 7 + len(plist)
    return pl.pallas_call(
        body, name="b_mixers", grid=(nt,), in_specs=in_specs, out_specs=out_specs, out_shape=out_shape,
        scratch_shapes=[pltpu.VMEM((HB + ne, W), F32), pltpu.VMEM((ne, W), F32), pltpu.VMEM((ne, W), F32),
                        pltpu.VMEM((ne + SUBLANES, W), F32)],
        input_output_aliases={n_in: 0},
        compiler_params=_cp("arbitrary"),
    )(proj, proj, proj, ca_saved, ca_saved, dy, dy, *plist, dproj)


def _norm_first(x, g, tm=512):
    def body(x_ref, g_ref, o_ref):
        o_ref[...] = _rms(x_ref[...], g_ref[...]).astype(BF16)

    return pl.pallas_call(
        body, name="f_norm0", grid=(T // tm,),
        in_specs=[pl.BlockSpec((tm, D), lambda i: (i, 0)), pl.BlockSpec((1, D), lambda i: (0, 0))],
        out_specs=pl.BlockSpec((tm, D), lambda i: (i, 0)),
        out_shape=jax.ShapeDtypeStruct((T, D), BF16), compiler_params=_cp("arbitrary"),
    )(x, g)


def _loss_head(x, target, g, tm=256):
    def body(x_ref, t_ref, g_ref, dx_ref, dg_ref, loss_ref):
        i = pl.program_id(0)
        x = x_ref[...]
        r = lax.rsqrt(jnp.mean(x * x, axis=-1, keepdims=True) + EPS)
        xh = x * r
        gv = g_ref[...]
        e = xh * gv - t_ref[...]
        dyv = e * (1.0 / D)
        part = jnp.sum(_rowsum(e * e), axis=-1, keepdims=True) * (0.5 / D)
        u = dyv * gv
        dx_ref[...] = r * (u - xh * jnp.mean(u * xh, axis=-1, keepdims=True))
        dgp = _rowsum(dyv * xh)

        @pl.when(i == 0)
        def _():
            dg_ref[...] = dgp
            loss_ref[...] = jnp.broadcast_to(part, (1, GW))

        @pl.when(i > 0)
        def _():
            dg_ref[...] += dgp
            loss_ref[...] += jnp.broadcast_to(part, (1, GW))

    return pl.pallas_call(
        body, name="loss_head", grid=(T // tm,),
        in_specs=[pl.BlockSpec((tm, D), lambda i: (i, 0)), pl.BlockSpec((tm, D), lambda i: (i, 0)),
                  pl.BlockSpec((1, D), lambda i: (0, 0))],
        out_specs=[pl.BlockSpec((tm, D), lambda i: (i, 0)), pl.BlockSpec((1, D), lambda i: (0, 0)),
                   pl.BlockSpec((1, GW), lambda i: (0, 0))],
        out_shape=[jax.ShapeDtypeStruct((T, D), F32), jax.ShapeDtypeStruct((1, D), F32),
                   jax.ShapeDtypeStruct((1, GW), F32)],
        compiler_params=_cp("arbitrary"),
    )(x, target, g)


def _ple_bwd(dx4, sv, w_pleg8, g_ple, tm=256):
    nt = T // tm
    ple_dim = sv["p"].shape[1]

    def body(dx_ref, gl_ref, pe_ref, x_ref, h_ref, p_ref, g_ref, wg_ref,
             dx3_ref, dx3b_ref, dg_ref, dwg_ref, dwp_ref, acc_g, acc_p):
        i = pl.program_id(0)
        d = dx_ref[...]
        s = _sig(gl_ref[...].astype(F32))
        dpe = (d * s).astype(BF16)
        dgl = (d * pe_ref[...].astype(F32) * s * (1.0 - s)).astype(BF16)
        dh = lax.dot_general(dgl, wg_ref[...], (((1,), (1,)), ((), ())), preferred_element_type=F32)
        dx, dgp = _rms_bwd(dh, x_ref[...], g_ref[...], d)
        dx3_ref[...] = dx
        dx3b_ref[...] = dx.astype(BF16)
        part_g = lax.dot_general(h_ref[...], dgl, (((0,), (0,)), ((), ())), preferred_element_type=F32)
        part_p = lax.dot_general(p_ref[...], dpe, (((0,), (0,)), ((), ())), preferred_element_type=F32)

        @pl.when(i == 0)
        def _():
            dg_ref[...] = dgp
            acc_g[...] = part_g
            acc_p[...] = part_p

        @pl.when(i > 0)
        def _():
            dg_ref[...] += dgp
            acc_g[...] += part_g
            acc_p[...] += part_p

        @pl.when(i == nt - 1)
        def _():
            for j in range(NDEV):
                dwg_ref[j] = acc_g[j * GW:(j + 1) * GW, :].astype(BF16)
                dwp_ref[j] = acc_p[:, j * GW:(j + 1) * GW].astype(BF16)

    tile = lambda w: pl.BlockSpec((tm, w), lambda i: (i, 0))
    const = lambda shp: pl.BlockSpec(shp, lambda i: (0,) * len(shp))
    return pl.pallas_call(
        body, name="b_ple", grid=(nt,),
        in_specs=[tile(D), tile(D), tile(D), tile(D), tile(D), tile(ple_dim), const((1, D)), const((D, D))],
        out_specs=[tile(D), tile(D), const((1, D)), const((NDEV, GW, D)), const((NDEV, ple_dim, GW))],
        out_shape=[jax.ShapeDtypeStruct((T, D), F32), jax.ShapeDtypeStruct((T, D), BF16),
                   jax.ShapeDtypeStruct((1, D), F32), jax.ShapeDtypeStruct((NDEV, GW, D), BF16),
                   jax.ShapeDtypeStruct((NDEV, ple_dim, GW), BF16)],
        scratch_shapes=[pltpu.VMEM((D, D), F32), pltpu.VMEM((ple_dim, D), F32)],
        compiler_params=_cp("arbitrary"),
    )(dx4, sv["gl"], sv["pe"], sv["x3"], sv["h3"], sv["p"], g_ple, w_pleg8.reshape(D, D))


def _layer_fwd(x, h1, p_bf, gw, sp, g_next):
    proj, = _mm(h1, gw["w_in"], mode="out", name="f_proj", outs=[BF16], tm=T)
    y, ca = _mixer_fwd(proj, sp)
    z, merged = _merge_fwd(y, proj, gw["w_branch"])
    x2, h2 = _mm(merged, gw["w_out"].reshape(1, D, D), mode="acc", name="f_out", outs=[F32, BF16], tm=T // 2,
                 tiles=[x], params=[sp["g_mlp"]], epi=_epi_res_norm)
    up, = _mm(h2, gw["w_up"], mode="out", name="f_up", outs=[BF16], tm=T)
    x3, h3 = _mm(up, gw["w_down"].reshape(1, 4 * D, D), mode="acc", name="f_down", outs=[F32, BF16], tm=T // 4,
                 tiles=[x2], params=[sp["g_ple"]], epi=_epi_res_norm, a_pre=_relu2_bf16)
    x4, gl, hn, pe = _mm(h3, gw["w_pleg"].reshape(1, D, D), mode="acc", name="f_gate", tm=T // 2,
                         outs=[F32, BF16, BF16, BF16], tiles=[x3, p_bf], params=[g_next, gw["w_ple"]], epi=_epi_ple)
    saved = dict(x=x, h1=h1, proj=proj, y=y, ca=ca, z=z, merged=merged, x2=x2, h2=h2, up=up, x3=x3, h3=h3,
                 pe=pe, gl=gl, p=p_bf)
    return x4, hn, saved


def _layer_bwd(dx4, sv, gw, sp, submit):
    dw = {}
    dx3, dx3b, dg_ple, dw["w_pleg"], dw["w_ple"] = _ple_bwd(dx4, sv, gw["w_pleg"], sp["g_ple"])
    dup, = _mm(dx3b, gw["w_down"], mode="out", trans_b=True, name="b_dact", outs=[BF16], tm=T,
               tiles=[sv["up"]], epi=_epi_dup)
    dw["w_down"] = _mm_tn(sv["up"], dx3b, nj=NDEV, split="row", name="b_dw_down", a_pre=_relu2_bf16)
    dw["w_up"] = _mm_tn(sv["h2"], dup, nj=NDEV, split="col", name="b_dw_up")
    dx2, dx2b, dg_mlp = _mm(dup, gw["w_up"], mode="full", trans_b=True, name="b_dh2", tm=T // 4,
                            outs=[F32, BF16], tiles=[sv["x2"], dx3], params=[sp["g_mlp"]], epi=_epi_rms_bwd, reds=[D])
    dm, = _mm(dx2b, gw["w_out"].reshape(1, D, D), mode="acc", trans_b=True, name="b_dmerged", outs=[BF16],
              tm=T // 2)
    dw["w_out"] = _mm_tn(sv["merged"], dx2b, nj=NDEV, split="row", name="b_dw_out")
    dproj, dy, dw["w_branch"] = _merge_bwd(dm, sv["z"], sv["proj"], sv["y"], gw["w_branch"])
    dy = submit(dw, BIG[1:], dy)
    dproj, dcw, dsc, vec, dpw, dws, dbs = _mixer_bwd(sv["proj"], sv["ca"], dy, dproj, sp)
    dw["w_in"] = _mm_tn(sv["h1"], dproj, nj=NDEV, split="col", name="b_dw_in")
    dw["w_in"], dproj = lax.optimization_barrier((dw["w_in"], dproj))
    dproj = submit(dw, BIG[:1], dproj)
    dx, dg_mix = _mm(dproj, gw["w_in"], mode="full", trans_b=True, name="b_dh1", outs=[F32], tm=T // 4,
                     tiles=[sv["x"], dx2], params=[sp["g_mix"]], epi=_epi_rms_bwd, reds=[D])
    small = dict(norm_mix=dg_mix[0], conf_dw=dcw[:CONF_K], conf_dw_b=vec[0], conf_ln_g=vec[1], conf_ln_b=vec[2],
                 pool_w=dpw, pool_scale=vec[3], sc_conv=dsc[:SC_K], gmlp_ln_g=vec[4], gmlp_ln_b=vec[5],
                 gmlp_ws=dws, gmlp_bs=dbs[:, :, 0], norm_mlp=dg_mlp[0], norm_ple=dg_ple[0])
    return dx, small


ANY = pl.BlockSpec(memory_space=pl.ANY)


def _mesh_pos():
    return lax.axis_index("x"), lax.axis_index("y"), lax.axis_index("c")


def _other_chips(x, y):
    return [(1 - x, y), (x, 1 - y), (1 - x, 1 - y)]


def _launch_comm(body, peers_of, operands, out_shapes, sems, name, seq_id):
    n_in, n_out = len(operands), len(out_shapes)
    if seq_id is None:
        return pl.pallas_call(body, name=name, in_specs=[ANY] * n_in, out_specs=[ANY] * n_out,
                              out_shape=out_shapes, scratch_shapes=sems)(*operands)

    def seq_body(*refs):
        peers = peers_of(*_mesh_pos())
        barrier = pltpu.get_barrier_semaphore()
        for peer in peers:
            pl.semaphore_signal(barrier, inc=1, device_id=peer, device_id_type=MESH)
        pl.semaphore_wait(barrier, len(peers))
        body(*refs)

    return pl.kernel(seq_body, name=name, out_type=out_shapes,
                     mesh=plsc.ScalarSubcoreMesh(axis_name="seq", num_cores=1), scratch_types=sems,
                     compiler_params=pltpu.CompilerParams(collective_id=seq_id))(*operands)


def _all_gather(shards, name, seq_id=None):
    n = len(shards)

    def body(*refs):
        s_refs, o_refs = refs[:n], refs[n:2 * n]
        send_sems, recv_sems, local_sems = refs[2 * n:]
        x, y, c = _mesh_pos()
        me = 4 * x + 2 * y + c
        here = (x, y, c)
        sibling = (x, y, 1 - c)
        chips = _other_chips(x, y)

        def slot(px, py, pc):
            return 4 * px + 2 * py + pc

        def copy(t, k, slot_idx, to, src=None):
            dst = o_refs[t].at[slot_idx]
            return pltpu.make_async_remote_copy(
                src_ref=dst if src is None else src, dst_ref=dst,
                send_sem=send_sems.at[t * 7 + k], recv_sem=recv_sems.at[t * 7 + k],
                device_id=to, device_id_type=MESH)

        mine = [pltpu.make_async_copy(s_refs[t], o_refs[t].at[me], local_sems.at[t]) for t in range(n)]
        for cp in mine:
            cp.start()
        first = []
        for t in range(n):
            for j, chip in enumerate(chips):
                first.append(copy(t, 1 + j, me, (*chip, c), src=s_refs[t]))
        for t in range(n):
            first.append(copy(t, 0, me, sibling, src=s_refs[t]))
        for cp in first:
            cp.start()
        passed = []
        for t in range(n):
            for j, chip in enumerate(chips):
                copy(t, 1 + j, slot(*chip, c), here).wait_recv()
                fwd = copy(t, 4 + j, slot(*chip, c), sibling)
                fwd.start()
                passed.append(fwd)
        for t in range(n):
            copy(t, 0, slot(x, y, 1 - c), here).wait_recv()
            for j, chip in enumerate(chips):
                copy(t, 4 + j, slot(*chip, 1 - c), here).wait_recv()
        for cp in first + passed:
            cp.wait_send()
        for cp in mine:
            cp.wait()

    def peers_of(x, y, c):
        return [(x, y, 1 - c)] + [(*chip, c) for chip in _other_chips(x, y)]

    return _launch_comm(
        body, peers_of, shards, [jax.ShapeDtypeStruct((NDEV,) + s.shape, s.dtype) for s in shards],
        [pltpu.SemaphoreType.DMA((7 * n,)), pltpu.SemaphoreType.DMA((7 * n,)), pltpu.SemaphoreType.DMA((n,))],
        name, seq_id)


def _rs_exchange(p4s, qs, name, seq_id=None):
    n_p, n_q = len(p4s), len(qs)

    def body(*refs):
        p_refs, q_refs = refs[:n_p], refs[n_p:n_p + n_q]
        rb_refs, rc_refs = refs[n_p + n_q:2 * n_p + n_q], refs[2 * n_p + n_q:2 * (n_p + n_q)]
        pair_send, pair_recv, chip_send, chip_recv, local_sems = refs[2 * (n_p + n_q):]
        x, y, c = _mesh_pos()
        a_idx = 2 * x + y
        chips = _other_chips(x, y)
        mine = [pltpu.make_async_copy(q_refs[t].at[a_idx], rc_refs[t].at[a_idx], local_sems.at[t])
                for t in range(n_q)]
        sends = []
        for t in range(n_q):
            for j, chip in enumerate(chips):
                sends.append(pltpu.make_async_remote_copy(
                    src_ref=q_refs[t].at[2 * chip[0] + chip[1]], dst_ref=rc_refs[t].at[a_idx],
                    send_sem=chip_send.at[t * 3 + j], recv_sem=chip_recv.at[t * 3 + j],
                    device_id=(*chip, c), device_id_type=MESH))
        pairs = [pltpu.make_async_remote_copy(
            src_ref=p_refs[t].at[:, 1 - c], dst_ref=rb_refs[t], send_sem=pair_send.at[t], recv_sem=pair_recv.at[t],
            device_id=(x, y, 1 - c), device_id_type=MESH) for t in range(n_p)]
        for cp in sends + mine + pairs:
            cp.start()
        for cp in pairs:
            cp.wait()
        for t in range(n_q):
            for j, chip in enumerate(chips):
                landed = rc_refs[t].at[2 * chip[0] + chip[1]]
                pltpu.make_async_remote_copy(
                    src_ref=landed, dst_ref=landed, send_sem=chip_send.at[t * 3 + j],
                    recv_sem=chip_recv.at[t * 3 + j], device_id=(x, y, c), device_id_type=MESH).wait_recv()
        for cp in sends:
            cp.wait_send()
        for cp in mine:
            cp.wait()

    def peers_of(x, y, c):
        peers = [(x, y, 1 - c)] if n_p else []
        return peers + ([(*chip, c) for chip in _other_chips(x, y)] if n_q else [])

    out_shapes = [jax.ShapeDtypeStruct((NCHIP,) + p.shape[2:], p.dtype) for p in p4s]
    out_shapes += [jax.ShapeDtypeStruct(q.shape, q.dtype) for q in qs]
    sems = [pltpu.SemaphoreType.DMA((max(n_p, 1),)), pltpu.SemaphoreType.DMA((max(n_p, 1),)),
            pltpu.SemaphoreType.DMA((max(3 * n_q, 1),)), pltpu.SemaphoreType.DMA((max(3 * n_q, 1),)),
            pltpu.SemaphoreType.DMA((max(n_q, 1),))]
    got = _launch_comm(body, peers_of, list(p4s) + list(qs), out_shapes, sems, name, seq_id)
    return got[:n_p], got[n_p:]


def _pair_sum(p4s, rbs, c_idx, name, nst=4):
    n = len(p4s)
    trs = [p.shape[2] // nst for p in p4s]

    def body(c_ref, *refs):
        del c_ref
        p_refs, r_refs, o_refs = refs[:n], refs[n:2 * n], refs[2 * n:]
        for p_ref, r_ref, o_ref in zip(p_refs, r_refs, o_refs):
            o_ref[...] = (p_ref[...].astype(F32) + r_ref[...].astype(F32)).astype(o_ref.dtype)

    in_specs = [pl.BlockSpec((None, None, tr, p.shape[3]), lambda b, i, c_ref: (b, c_ref[0], i, 0))
                for p, tr in zip(p4s, trs)]
    in_specs += [pl.BlockSpec((None, tr, p.shape[3]), lambda b, i, c_ref: (b, i, 0)) for p, tr in zip(p4s, trs)]
    out_specs = [pl.BlockSpec((None, tr, p.shape[3]), lambda b, i, c_ref: (b, i, 0)) for p, tr in zip(p4s, trs)]
    return pl.pallas_call(
        body, name=name,
        grid_spec=pltpu.PrefetchScalarGridSpec(num_scalar_prefetch=1, grid=(NCHIP, nst), in_specs=in_specs,
                                               out_specs=out_specs),
        out_shape=[jax.ShapeDtypeStruct((NCHIP,) + p.shape[2:], p.dtype) for p in p4s],
        compiler_params=_cp("arbitrary", "arbitrary"),
    )(c_idx, *p4s, *rbs)


def _reduce_scatter(ps, c_idx, tag, nst=4):
    p4s = [p.reshape((NCHIP, 2) + p.shape[1:]) for p in ps]
    rbs, _ = _rs_exchange(p4s, [], name="rs_pair_" + tag)
    qs = _pair_sum(p4s, rbs, c_idx, name="rs_pairsum_" + tag, nst=nst)
    return _rs_exchange([], qs, name="rs_chip_" + tag)[1]


class _GradientPipeline:
    def __init__(self, c_idx, results):
        self.c_idx, self.results, self.pending = c_idx, results, None

    def _sum_pending(self, chain):
        names, layer, p4s, rbs = self.pending
        qs = _pair_sum(p4s, rbs, self.c_idx, name="rs_pairsum_" + ("first" if len(names) == 1 else "rest"))
        return lax.optimization_barrier((chain, qs))

    def submit(self, dw, names, layer, chain):
        qs, tag, seq_id = [], "pair", 3
        if self.pending is not None:
            chain, qs = self._sum_pending(chain)
            tag, seq_id = "pair_chip", 4
        p4s = [dw[n].reshape((NCHIP, 2) + BIG_SHARD[n]) for n in names]
        rbs, rcs = _rs_exchange(p4s, qs, name="rs_%s_%d" % (tag, len(names)), seq_id=seq_id)
        self._record(rcs)
        self.pending = (names, layer, p4s, rbs)
        return chain

    def finish(self, chain):
        chain, qs = self._sum_pending(chain)
        self._record(_rs_exchange([], qs, name="rs_chip_last", seq_id=5)[1])
        self.pending = None
        return chain

    def _record(self, rcs):
        if rcs:
            names, layer = self.pending[:2]
            for n, rc in zip(names, rcs):
                self.results[n][layer] = rc


def _adamw(w, g, m, v):
    m = ADAM_B1 * m + (1.0 - ADAM_B1) * g
    v = ADAM_B2 * v + (1.0 - ADAM_B2) * (g * g)
    m_hat = m / (1.0 - ADAM_B1 ** ADAM_STEP)
    v_hat = v / (1.0 - ADAM_B2 ** ADAM_STEP)
    delta = -ADAM_LR * (m_hat / (jnp.sqrt(v_hat) + ADAM_EPS) + ADAM_WD * w)
    return delta, m, v


def _adam_sharded(rcs, w, m, v, tr, name, first_layer, partial=None):
    _, r, c = w.shape
    nst = r // tr
    n_l = len(rcs)

    def body(*refs):
        rc_refs = refs[:n_l]
        w_ref, m_ref, v_ref = refs[n_l:n_l + 3]
        g_out, d_out, m_out, v_out = refs[-4:]
        layer = pl.program_id(0)
        for k, rc in enumerate(rc_refs):
            @pl.when(layer == k)
            def _():
                g = rc[0].astype(F32) + rc[1].astype(F32) + rc[2].astype(F32) + rc[3].astype(F32)
                delta, m_new, v_new = _adamw(w_ref[...], g, m_ref[...], v_ref[...])
                g_out[...] = g
                d_out[...] = delta
                m_out[...] = m_new
                v_out[...] = v_new

    rc_specs = [pl.BlockSpec((NCHIP, tr, c), lambda l, i, k=k: (0, jnp.where(l == k, i, 0), 0)) for k in range(n_l)]
    wspec = pl.BlockSpec((None, tr, c), lambda l, i: (first_layer + l, i, 0))
    carried = [] if partial is None else list(partial)
    return pl.pallas_call(
        body, name=name, grid=(n_l, nst),
        in_specs=rc_specs + [wspec] * 3 + [pl.BlockSpec(memory_space=pl.ANY)] * len(carried),
        out_specs=[wspec] * 4, out_shape=[jax.ShapeDtypeStruct(w.shape, F32)] * 4,
        input_output_aliases={n_l + 3 + k: k for k in range(len(carried))},
        compiler_params=_cp("arbitrary", "arbitrary"),
    )(*rcs, w, m, v, *carried)


def _adam_packed(g, w, m, v, tr=128):
    rows = g.shape[0]

    def body(g_ref, w_ref, m_ref, v_ref, d_out, m_out, v_out):
        delta, m_new, v_new = _adamw(w_ref[...], g_ref[...], m_ref[...], v_ref[...])
        d_out[...] = delta
        m_out[...] = m_new
        v_out[...] = v_new

    spec = pl.BlockSpec((tr, D), lambda i: (i, 0))
    return pl.pallas_call(
        body, name="adam_small", grid=(rows // tr,), in_specs=[spec] * 4, out_specs=[spec] * 3,
        out_shape=[jax.ShapeDtypeStruct(g.shape, F32)] * 3, compiler_params=_cp("arbitrary"),
    )(g, w, m, v)


def _sum4(rc):
    def body(rc_ref, o_ref):
        o_ref[...] = rc_ref[0] + rc_ref[1] + rc_ref[2] + rc_ref[3]

    return pl.pallas_call(
        body, name="small_sum", out_shape=jax.ShapeDtypeStruct(rc.shape[1:], F32),
    )(rc)


BIG = ("w_in", "w_branch", "w_out", "w_up", "w_down", "w_ple", "w_pleg")
BIG_SHARD = {"w_in": (D, D), "w_branch": (4 * W, GW), "w_out": (GW, D), "w_up": (D, W), "w_down": (W, D),
             "w_ple": (256, GW), "w_pleg": (GW, D)}
ADAM_ROWS = {"w_in": 256, "w_branch": 512, "w_out": 128, "w_up": 256, "w_down": 256, "w_ple": 256, "w_pleg": 128}
SMALL = (("norm_mix", (DEPTH, D)), ("conf_dw", (DEPTH, CONF_K, W)), ("conf_dw_b", (DEPTH, W)),
         ("conf_ln_g", (DEPTH, W)), ("conf_ln_b", (DEPTH, W)), ("pool_w", (DEPTH, 4, GW, GW)),
         ("pool_scale", (DEPTH, W)), ("sc_conv", (DEPTH, SC_K, W)), ("gmlp_ln_g", (DEPTH, W)),
         ("gmlp_ln_b", (DEPTH, W)), ("gmlp_ws", (DEPTH, 4, GW, GW)), ("gmlp_bs", (DEPTH, 4, GW)),
         ("norm_mlp", (DEPTH, D)), ("norm_ple", (DEPTH, D)), ("norm_final", (D,)))
CHANNEL_SHARDED = ("conf_dw", "sc_conv")
SMALL_ROWS = 80


def _pack(arrs, rows):
    flat = jnp.concatenate([a.reshape(-1) for a in arrs])
    return jnp.pad(flat, (0, rows * D - flat.shape[0])).reshape(rows, D)


def _unpack(packed, shapes):
    flat = packed.reshape(-1)
    out, off = [], 0
    for shp in shapes:
        size = 1
        for s in shp:
            size *= s
        out.append(flat[off:off + size].reshape(shp))
        off += size
    return out


def kernel(x, p, norm_mix, w_in, conf_dw, conf_dw_b, conf_ln_g, conf_ln_b, pool_w, pool_scale, sc_conv, gmlp_ln_g, gmlp_ln_b, gmlp_ws, gmlp_bs, w_branch, w_out, norm_mlp, w_up, w_down, norm_ple, w_ple, w_ple_gate, norm_final, loss_target, m_norm_mix, m_w_in, m_conf_dw, m_conf_dw_b, m_conf_ln_g, m_conf_ln_b, m_pool_w, m_pool_scale, m_sc_conv, m_gmlp_ln_g, m_gmlp_ln_b, m_gmlp_ws, m_gmlp_bs, m_w_branch, m_w_out, m_norm_mlp, m_w_up, m_w_down, m_norm_ple, m_w_ple, m_w_ple_gate, m_norm_final, v_norm_mix, v_w_in, v_conf_dw, v_conf_dw_b, v_conf_ln_g, v_conf_ln_b, v_pool_w, v_pool_scale, v_sc_conv, v_gmlp_ln_g, v_gmlp_ln_b, v_gmlp_ws, v_gmlp_bs, v_w_branch, v_w_out, v_norm_mlp, v_w_up, v_w_down, v_norm_ple, v_w_ple, v_w_ple_gate, v_norm_final):
    weights = dict(norm_mix=norm_mix, w_in=w_in, conf_dw=conf_dw, conf_dw_b=conf_dw_b, conf_ln_g=conf_ln_g,
                   conf_ln_b=conf_ln_b, pool_w=pool_w, pool_scale=pool_scale, sc_conv=sc_conv, gmlp_ln_g=gmlp_ln_g,
                   gmlp_ln_b=gmlp_ln_b, gmlp_ws=gmlp_ws, gmlp_bs=gmlp_bs, w_branch=w_branch, w_out=w_out,
                   norm_mlp=norm_mlp, w_up=w_up, w_down=w_down, norm_ple=norm_ple, w_ple=w_ple, w_pleg=w_ple_gate,
                   norm_final=norm_final)
    mom1 = dict(norm_mix=m_norm_mix, w_in=m_w_in, conf_dw=m_conf_dw, conf_dw_b=m_conf_dw_b, conf_ln_g=m_conf_ln_g,
                conf_ln_b=m_conf_ln_b, pool_w=m_pool_w, pool_scale=m_pool_scale, sc_conv=m_sc_conv,
                gmlp_ln_g=m_gmlp_ln_g, gmlp_ln_b=m_gmlp_ln_b, gmlp_ws=m_gmlp_ws, gmlp_bs=m_gmlp_bs,
                w_branch=m_w_branch, w_out=m_w_out, norm_mlp=m_norm_mlp, w_up=m_w_up, w_down=m_w_down,
                norm_ple=m_norm_ple, w_ple=m_w_ple, w_pleg=m_w_ple_gate, norm_final=m_norm_final)
    mom2 = dict(norm_mix=v_norm_mix, w_in=v_w_in, conf_dw=v_conf_dw, conf_dw_b=v_conf_dw_b, conf_ln_g=v_conf_ln_g,
                conf_ln_b=v_conf_ln_b, pool_w=v_pool_w, pool_scale=v_pool_scale, sc_conv=v_sc_conv,
                gmlp_ln_g=v_gmlp_ln_g, gmlp_ln_b=v_gmlp_ln_b, gmlp_ws=v_gmlp_ws, gmlp_bs=v_gmlp_bs,
                w_branch=v_w_branch, w_out=v_w_out, norm_mlp=v_norm_mlp, w_up=v_w_up, w_down=v_w_down,
                norm_ple=v_norm_ple, w_ple=v_w_ple, w_pleg=v_w_ple_gate, norm_final=v_norm_final)

    xi, yi, ci = _mesh_pos()
    me = 4 * xi + 2 * yi + ci
    c_idx = jnp.reshape(ci, (1,)).astype(jnp.int32)

    gathered, conf_full, sc_full = [], [], []
    for l in range(DEPTH):
        shard = lambda n: weights[n][l].astype(BF16).reshape(BIG_SHARD[n])
        w_in_g, conf_g, sc_g = _all_gather([shard("w_in"), conf_dw[l], sc_conv[l]], name="ag_first", seq_id=1)
        rest = _all_gather([shard(n) for n in BIG[1:]], name="ag_rest", seq_id=2)
        gw = dict(zip(BIG[1:], rest), w_in=w_in_g)
        gw["w_branch"] = gw["w_branch"].reshape(NDEV, 4, W, GW)
        gathered.append(gw)
        conf_full.append(conf_g)
        sc_full.append(sc_g)

    def small_params(l):
        return dict(cw=conf_full[l], cb=conf_dw_b[l][None], lg=conf_ln_g[l][None], lb=conf_ln_b[l][None],
                    pw=pool_w[l], ps=pool_scale[l][None], sc=sc_full[l], gg=gmlp_ln_g[l][None],
                    gb=gmlp_ln_b[l][None], ws=gmlp_ws[l], bst=gmlp_bs[l].T, g_mix=norm_mix[l][None],
                    g_mlp=norm_mlp[l][None], g_ple=norm_ple[l][None])

    xc = x.reshape(T, D)
    small_names = [n for n, _ in SMALL]

    def in_gradient_layout(n, shard, shape):
        if n not in CHANNEL_SHARDED:
            return shard
        return lax.dynamic_update_slice(jnp.zeros(shape, F32), shard, (0, 0, me * (W // NDEV)))

    small_state = [_pack([in_gradient_layout(n, src[n], shape) for n, shape in SMALL], NDEV * SMALL_ROWS)
                   for src in (weights, mom1, mom2)]
    xc, small_state = lax.optimization_barrier((xc, small_state))
    p_bf = p.reshape(DEPTH, T, 256).astype(BF16)
    h = _norm_first(xc, norm_mix[0][None])
    saved = []
    for l in range(DEPTH):
        g_next = norm_mix[l + 1][None] if l + 1 < DEPTH else norm_final[None]
        h, conf_g, sc_g = lax.optimization_barrier((h, conf_full[l], sc_full[l]))
        conf_full[l] = conf_g.transpose(1, 0, 2).reshape(CONF_K, W)
        sc_full[l] = sc_g.transpose(1, 0, 2).reshape(SC_K, W)
        xc, h, sv = _layer_fwd(xc, h, p_bf[l], gathered[l], small_params(l), g_next)
        saved.append(sv)

    dxc, dg_final, loss_part = _loss_head(xc, loss_target.reshape(T, D), norm_final[None])
    loss = lax.psum(loss_part[0, 0], ("x", "y", "c"))
    small_grads = [None] * DEPTH
    rcs = {n: [None] * DEPTH for n in BIG}
    pipeline = _GradientPipeline(c_idx, rcs)
    for l in reversed(range(DEPTH)):
        dxc, small_grads[l] = _layer_bwd(dxc, saved[l], gathered[l], small_params(l),
                                         lambda dw, names, value, l=l: pipeline.submit(dw, names, l, value))

    def adam_sharded(first_layer, n_layers, partial, tag):
        outs = {}
        for n in BIG:
            shp = (DEPTH,) + BIG_SHARD[n]
            outs[n] = _adam_sharded(rcs[n][first_layer:first_layer + n_layers], weights[n].reshape(shp),
                                    mom1[n].reshape(shp), mom2[n].reshape(shp), ADAM_ROWS[n],
                                    "adam_%s_%s" % (n, tag), first_layer, None if partial is None else partial[n])
        return outs

    dxc, upper = lax.optimization_barrier((dxc, {n: rcs[n][1:] for n in BIG}))
    for n in BIG:
        rcs[n][1:] = upper[n]
    partial = adam_sharded(1, DEPTH - 1, None, "upper")
    dxc, partial = pipeline.finish((dxc, partial))

    stacked = {n: jnp.stack([small_grads[l][n] for l in range(DEPTH)]) for n, _ in SMALL if n != "norm_final"}
    stacked["norm_final"] = dg_final[0]
    packed = _pack([stacked[n] for n, _ in SMALL], NDEV * SMALL_ROWS).reshape(NDEV, SMALL_ROWS, D)
    packed, dxc = lax.optimization_barrier((packed, dxc))
    reduced_slot = _sum4(_reduce_scatter([packed], c_idx, "small", nst=1)[0])
    reduced = _all_gather([reduced_slot], name="ag_small")[0]
    small_full = dict(zip([n for n, _ in SMALL], _unpack(reduced, [s for _, s in SMALL])))
    grads, deltas, new_m, new_v = {}, {}, {}, {}
    d_p, m_p, v_p = _adam_packed(reduced.reshape(NDEV * SMALL_ROWS, D), *small_state)
    small_shapes = [s for _, s in SMALL]

    def own_channels(n, full):
        return lax.dynamic_slice_in_dim(full, me * (W // NDEV), W // NDEV, axis=2) if n in CHANNEL_SHARDED else full

    for n, d_, m_, v_ in zip(small_names, _unpack(d_p, small_shapes), _unpack(m_p, small_shapes),
                             _unpack(v_p, small_shapes)):
        grads[n], deltas[n], new_m[n], new_v[n] = (own_channels(n, small_full[n]), own_channels(n, d_),
                                                   own_channels(n, m_), own_channels(n, v_))

    for n, (g_, d_, m_, v_) in adam_sharded(0, 1, partial, "last").items():
        full = weights[n].shape
        grads[n], deltas[n], new_m[n], new_v[n] = g_.reshape(full), d_.reshape(full), m_.reshape(full), v_.reshape(full)

    order = ("norm_mix", "w_in", "conf_dw", "conf_dw_b", "conf_ln_g", "conf_ln_b", "pool_w", "pool_scale", "sc_conv",
             "gmlp_ln_g", "gmlp_ln_b", "gmlp_ws", "gmlp_bs", "w_branch", "w_out", "norm_mlp", "w_up", "w_down",
             "norm_ple", "w_ple", "w_pleg", "norm_final")
    return (loss, dxc.reshape(1, T, D), *[grads[n] for n in order], *[deltas[n] for n in order],
            *[new_m[n] for n in order], *[new_v[n] for n in order])
```

```python
import functools

import jax
import jax.numpy as jnp
from jax import lax
from jax.experimental import pallas as pl
from jax.experimental.pallas import tpu as pltpu
from jax.experimental.pallas import tpu_sc as plsc

F32 = jnp.float32
BF16 = jnp.bfloat16

DEPTH = 4
T = 2048
D = 1024
W = 512
NDEV = 8
NCHIP = 4
EPS = 1e-6
CONF_K = 31
SC_K = 3
POOL_WINDOWS = (2, 4, 8, 16)
GW = 128
HB = 32
HA = 32
COLS_IN = 8192
MIX_COLS = 4096

ADAM_LR = 0.001
ADAM_B1 = 0.9
ADAM_B2 = 0.999
ADAM_EPS = 1e-08
ADAM_WD = 0.01
ADAM_STEP = 10

VMEM_LIMIT_BYTES = 56 * 1024 * 1024
MESH = pl.DeviceIdType.MESH


def _cp(*sem):
    return pltpu.CompilerParams(dimension_semantics=tuple(sem), vmem_limit_bytes=VMEM_LIMIT_BYTES)


def _sig(x):
    return jax.nn.sigmoid(x)


def _rms(x, g):
    r = lax.rsqrt(jnp.mean(x * x, axis=-1, keepdims=True) + EPS)
    return x * r * g


def _rms_bwd(dh, x, g, dres):
    r = lax.rsqrt(jnp.mean(x * x, axis=-1, keepdims=True) + EPS)
    xh = x * r
    u = dh * g
    dx = r * (u - xh * jnp.mean(u * xh, axis=-1, keepdims=True)) + dres
    dg = jnp.sum(dh * xh, axis=0, keepdims=True)
    return dx, dg


def _ln_stats(x):
    mu = jnp.mean(x, axis=-1, keepdims=True)
    xc = x - mu
    rstd = lax.rsqrt(jnp.mean(xc * xc, axis=-1, keepdims=True) + EPS)
    return xc * rstd, rstd


def _ln_bwd(dxh, xh, rstd):
    return rstd * (dxh - jnp.mean(dxh, axis=-1, keepdims=True) - xh * jnp.mean(dxh * xh, axis=-1, keepdims=True))


def _rowsum(x):
    return jnp.sum(x, axis=0, keepdims=True)


EPI_ROWS = 256


def _relu2_bf16(up):
    r = jnp.maximum(up.astype(F32), 0.0)
    return (r * r).astype(BF16)


def _mm(a, b3, *, mode, name, outs, trans_b=False, tm=512, tiles=(), params=(), epi=None, reds=(), a_pre=None):
    t_, ka = a.shape
    nj, r, c = b3.shape
    kb, nb = (c, r) if trans_b else (r, c)
    nt = t_ // tm
    out_mode = mode == "out"
    full = mode == "full"
    assert trans_b or not full
    if out_mode:
        assert ka == kb and not reds
        grid = (nj, nt)
        a_map = lambda g0, g1: (g1, 0)
        b_map = lambda g0, g1: (g0, 0, 0)
        t_map = lambda g0, g1: (g1, g0)
        width = nj * nb
    else:
        assert ka == nj * kb
        grid = (nt, 1 if full else nj)
        a_map = lambda g0, g1: (g0, g1)
        b_map = lambda g0, g1: (g1, 0, 0)
        t_map = lambda g0, g1: (g0, 0)
        width = nb
    n_t, n_p, n_o, n_r = len(tiles), len(params), len(outs), len(reds)
    use_acc = (not out_mode) and nj > 1 and not full
    dims = (((1,), (1,)), ((), ())) if trans_b else (((1,), (0,)), ((), ()))

    def body(a_ref, b_ref, *rest):
        t_refs = rest[:n_t]
        p_refs = rest[n_t:n_t + n_p]
        o_refs = rest[n_t + n_p:n_t + n_p + n_o]
        r_refs = rest[n_t + n_p + n_o:n_t + n_p + n_o + n_r]
        i = pl.program_id(1 if out_mode else 0)
        a_val = a_ref[...] if a_pre is None else a_pre(a_ref[...])
        if full:
            b_all, b_sems = rest[-2], rest[-1]

            @pl.when(i == 0)
            def _():
                cps = [pltpu.make_async_copy(b_ref.at[j], b_all.at[:, j * c:(j + 1) * c], b_sems.at[j])
                       for j in range(nj)]
                for cp in cps:
                    cp.start()
                for cp in cps:
                    cp.wait()

            part = lax.dot_general(a_val, b_all[...], dims, preferred_element_type=F32)
        else:
            part = lax.dot_general(a_val, b_ref[...], dims, preferred_element_type=F32)

        def finish(acc_rows):
            totals = [None] * n_r
            for r0 in range(0, tm, min(tm, EPI_ROWS)):
                rows = slice(r0, r0 + min(tm, EPI_ROWS))
                if epi is None:
                    res, rr = (acc_rows(rows),), ()
                else:
                    res, rr = epi(acc_rows(rows), [t[rows, :] for t in t_refs], [p[...] for p in p_refs])
                for o_ref, val in zip(o_refs, res):
                    o_ref[rows, :] = val.astype(o_ref.dtype)
                totals = [val if tot is None else tot + val for tot, val in zip(totals, rr)]
            for r_ref, val in zip(r_refs, totals):
                @pl.when(i == 0)
                def _():
                    r_ref[...] = val

                @pl.when(i > 0)
                def _():
                    r_ref[...] += val

        if use_acc:
            acc_ref = rest[-1]
            j = pl.program_id(1)

            @pl.when(j == 0)
            def _():
                acc_ref[...] = part

            @pl.when(jnp.logical_and(j > 0, j < nj - 1))
            def _():
                acc_ref[...] += part

            @pl.when(j == nj - 1)
            def _():
                finish(lambda rows: acc_ref[rows, :] + part[rows])
        else:
            finish(lambda rows: part[rows])

    const2 = lambda g0, g1: (0, 0)
    if full:
        in_specs = [pl.BlockSpec((tm, ka), a_map), pl.BlockSpec(memory_space=pl.ANY)]
        scratch = [pltpu.VMEM((r, nj * c), b3.dtype), pltpu.SemaphoreType.DMA((nj,))]
    else:
        in_specs = [pl.BlockSpec((tm, kb), a_map), pl.BlockSpec((None, r, c), b_map)]
        scratch = [pltpu.VMEM((tm, nb), F32)] if use_acc else []
    in_specs += [pl.BlockSpec((tm, t.shape[1] // nj if out_mode else t.shape[1]), t_map) for t in tiles]
    in_specs += [pl.BlockSpec(p.shape, lambda g0, g1, nd=p.ndim: (0,) * nd) for p in params]
    out_specs = [pl.BlockSpec((tm, nb), t_map) for _ in outs] + [pl.BlockSpec((1, w), const2) for w in reds]
    out_shape = [jax.ShapeDtypeStruct((t_, width), dt) for dt in outs]
    out_shape += [jax.ShapeDtypeStruct((1, w), F32) for w in reds]
    res = pl.pallas_call(
        body, name=name, grid=grid, in_specs=in_specs, out_specs=out_specs, out_shape=out_shape,
        scratch_shapes=scratch, compiler_params=_cp("arbitrary", "arbitrary"),
    )(a, b3, *tiles, *params)
    return res


def _mm_tn(a, g, *, nj, split, name, out_dtype=BF16, a_pre=None):
    t_ = a.shape[0]
    if split == "col":
        r, c = a.shape[1], g.shape[1] // nj
        a_spec = pl.BlockSpec((t_, r), lambda j: (0, 0))
        g_spec = pl.BlockSpec((t_, c), lambda j: (0, j))
    else:
        r, c = a.shape[1] // nj, g.shape[1]
        a_spec = pl.BlockSpec((t_, r), lambda j: (0, j))
        g_spec = pl.BlockSpec((t_, c), lambda j: (0, 0))

    def body(a_ref, g_ref, o_ref):
        a_val = a_ref[...] if a_pre is None else a_pre(a_ref[...])
        o_ref[...] = lax.dot_general(a_val, g_ref[...], (((0,), (0,)), ((), ())),
                                     preferred_element_type=F32).astype(o_ref.dtype)

    return pl.pallas_call(
        body, name=name, grid=(nj,), in_specs=[a_spec, g_spec],
        out_specs=pl.BlockSpec((None, r, c), lambda j: (j, 0, 0)),
        out_shape=jax.ShapeDtypeStruct((nj, r, c), out_dtype),
        compiler_params=_cp("arbitrary"),
    )(a, g)


def _epi_res_norm(acc, tiles, params):
    x_new = tiles[0] + acc
    return (x_new, _rms(x_new, params[0])), ()


def _epi_ple(acc, tiles, params):
    x_old, p_tile = tiles
    g_next, w_ple8 = params
    pe = jnp.concatenate([jnp.dot(p_tile, w_ple8[j], preferred_element_type=F32) for j in range(NDEV)], axis=1)
    x_new = x_old + pe * _sig(acc)
    return (x_new, acc, _rms(x_new, g_next), pe), ()


def _epi_rms_bwd(acc, tiles, params):
    dx, dg = _rms_bwd(acc, tiles[0], params[0], tiles[1])
    return (dx, dx), (dg,)


def _epi_dup(acc, tiles, params):
    return (acc * (2.0 * jnp.maximum(tiles[0].astype(F32), 0.0)),), ()


def _tri_mask():
    row = lax.broadcasted_iota(jnp.int32, (GW, GW), 0)
    col = lax.broadcasted_iota(jnp.int32, (GW, GW), 1)
    return row >= col


def _small_specs(sp_list):
    return [pl.BlockSpec(p.shape, (lambda i: (0, 0)) if p.ndim == 2 else (lambda i: (0, 0, 0))) for p in sp_list]


SUBLANES = 8


def _tap_sum(src, w_ref, taps, rows, stage):
    groups = {}
    for off, k in taps:
        groups.setdefault(off % SUBLANES, []).append((off - off % SUBLANES, k))
    out = None
    for res, members in sorted(groups.items()):
        n = rows if res == 0 else rows + SUBLANES
        part = None
        for base, k in members:
            term = w_ref[k:k + 1, :] * src[pl.ds(base, n), :]
            part = term if part is None else part + term
        if res:
            stage[0:n, :] = part
            part = stage[pl.ds(res, rows), :]
        out = part if out is None else out + part
    return out


def _tap_grads(grad, src, offsets, rows, stage, out_ref):
    pad = SUBLANES
    stage[0:pad, :] = jnp.zeros((pad, grad.shape[1]), F32)
    stage[pad:pad + rows, :] = grad
    stage[pad + rows:2 * pad + rows, :] = jnp.zeros((pad, grad.shape[1]), F32)
    groups = {}
    for k, off in enumerate(offsets):
        groups.setdefault(off % SUBLANES, []).append((off - off % SUBLANES, k))
    for res, members in sorted(groups.items()):
        shifted = stage[pl.ds(pad - res, rows + pad), :]
        for base, k in members:
            out_ref[k:k + 1, :] += _rowsum(shifted * src[pl.ds(base, rows + pad), :])


def _mixer_params(sp):
    return [sp["cw"], sp["cb"], sp["lg"], sp["lb"], sp["pw"], sp["ps"], sp["sc"], sp["gg"], sp["gb"], sp["ws"], sp["bst"]]


def _mixer_fwd(proj, sp, tm=256):
    nt = T // tm
    per = tm // HB

    conv_taps = [(HB - (CONF_K - 1) + k, k) for k in range(CONF_K)]

    def body(main_ref, halo_ref, cw, cb, lg, lb, pw, ps, sc, gg, gb, ws, bst, y_ref, ca_ref, ext, stage):
        i = pl.program_id(0)
        keep = (i > 0).astype(F32)

        def mcol(c0):
            return main_ref[:, c0:c0 + W].astype(F32)

        def hcol(c0):
            return halo_ref[:, c0:c0 + W].astype(F32)

        ext[0:HB, :] = hcol(0) * _sig(hcol(W)) * keep
        ext[HB:HB + tm, :] = mcol(0) * _sig(mcol(W))
        ca = (_tap_sum(ext, cw, conv_taps, tm, stage) + cb[...]).astype(BF16)
        ca_ref[...] = ca
        xh, _ = _ln_stats(ca.astype(F32))
        n = xh * lg[...] + lb[...]
        y_ref[:, 0:W] = (n * _sig(n)).astype(BF16)

        pin = mcol(1024)
        ext[0:HB, :] = hcol(1024) * keep
        ext[HB:HB + tm, :] = pin
        pos = (i * tm + lax.broadcasted_iota(jnp.int32, (tm, 1), 0) + 1).astype(F32)
        for g, w in enumerate(POOL_WINDOWS):
            lo = g * GW
            s = ext[pl.ds(HB, tm), lo:lo + GW]
            for j in range(1, w):
                s = s + ext[pl.ds(HB - j, tm), lo:lo + GW]
            pooled = s / jnp.minimum(pos, float(w)) - pin[:, lo:lo + GW]
            mixed = jnp.dot(pooled.astype(BF16), pw[g].astype(BF16), preferred_element_type=F32)
            y_ref[:, W + lo:W + lo + GW] = (mixed * ps[:, lo:lo + GW]).astype(BF16)

        ext[0:HB, :] = hcol(2048) * hcol(2560) * keep
        ext[HB:HB + tm, :] = mcol(2048) * mcol(2560)
        cv = sc[0:1, :] * ext[pl.ds(HB - 2, tm), :]
        cv = cv + sc[1:2, :] * ext[pl.ds(HB - 1, tm), :]
        cv = cv + sc[2:3, :] * ext[pl.ds(HB, tm), :]
        y_ref[:, 2 * W:3 * W] = (mcol(1536) * cv).astype(BF16)

        vh, _ = _ln_stats(mcol(3584))
        vn = (vh * gg[...] + gb[...]).astype(BF16)
        u = mcol(3072)
        tri = _tri_mask()
        for g in range(4):
            lo = g * GW
            wm = jnp.where(tri, ws[g], 0.0).astype(BF16)
            for c in range(tm // GW):
                r0 = c * GW
                sg = jnp.dot(wm, vn[r0:r0 + GW, lo:lo + GW], preferred_element_type=F32) + bst[:, g:g + 1]
                y_ref[r0:r0 + GW, 3 * W + lo:3 * W + lo + GW] = (u[r0:r0 + GW, lo:lo + GW] * sg).astype(BF16)

    plist = _mixer_params(sp)
    in_specs = [pl.BlockSpec((tm, MIX_COLS), lambda i: (i, 0)),
                pl.BlockSpec((HB, MIX_COLS), lambda i: (jnp.maximum(i * per - 1, 0), 0))]
    in_specs += _small_specs(plist)
    return pl.pallas_call(
        body, name="f_mixers", grid=(nt,), in_specs=in_specs,
        out_specs=[pl.BlockSpec((tm, 4 * W), lambda i: (i, 0)), pl.BlockSpec((tm, W), lambda i: (i, 0))],
        out_shape=[jax.ShapeDtypeStruct((T, 4 * W), BF16), jax.ShapeDtypeStruct((T, W), BF16)],
        scratch_shapes=[pltpu.VMEM((HB + tm, W), F32), pltpu.VMEM((tm + SUBLANES, W), F32)],
        compiler_params=_cp("arbitrary"),
    )(proj, proj, *plist)


def _assemble_wb(wb8_ref, wbf_ref):
    for k in range(4):
        for j in range(NDEV):
            wbf_ref[k, :, j * GW:(j + 1) * GW] = wb8_ref[j, k]


def _merge_fwd(y, proj, wb8, tm=256):
    nt = T // tm

    def body(y_ref, gate_ref, wb8_ref, z_ref, m_ref, wbf):
        @pl.when(pl.program_id(0) == 0)
        def _():
            _assemble_wb(wb8_ref, wbf)

        m = jnp.zeros((tm, D), F32)
        for k in range(4):
            zk = jnp.dot(y_ref[:, k * W:(k + 1) * W], wbf[k], preferred_element_type=F32)
            z_ref[:, k * D:(k + 1) * D] = zk.astype(BF16)
            m = m + _sig(gate_ref[:, k * D:(k + 1) * D].astype(F32)) * zk
        m_ref[...] = m.astype(BF16)

    return pl.pallas_call(
        body, name="f_merge", grid=(nt,),
        in_specs=[pl.BlockSpec((tm, 4 * W), lambda i: (i, 0)),
                  pl.BlockSpec((tm, 4 * D), lambda i: (i, 1)),
                  pl.BlockSpec(wb8.shape, lambda i: (0, 0, 0, 0))],
        out_specs=[pl.BlockSpec((tm, 4 * D), lambda i: (i, 0)), pl.BlockSpec((tm, D), lambda i: (i, 0))],
        out_shape=[jax.ShapeDtypeStruct((T, 4 * D), BF16), jax.ShapeDtypeStruct((T, D), BF16)],
        scratch_shapes=[pltpu.VMEM((4, W, D), BF16)],
        compiler_params=_cp("arbitrary"),
    )(y, proj, wb8)


def _merge_bwd(dm, z, proj, y, wb8, tm=256):
    nt = T // tm

    def body(dm_ref, z_ref, gate_ref, y_ref, wb8_ref, dp_ref, dy_ref, dwb_ref, wbf, acc):
        i = pl.program_id(0)

        @pl.when(i == 0)
        def _():
            _assemble_wb(wb8_ref, wbf)

        dmv = dm_ref[...].astype(F32)
        for k in range(4):
            s = _sig(gate_ref[:, k * D:(k + 1) * D].astype(F32))
            dzk = (dmv * s).astype(BF16)
            dp_ref[:, k * D:(k + 1) * D] = (dmv * z_ref[:, k * D:(k + 1) * D].astype(F32) * s * (1.0 - s)).astype(BF16)
            dyk = lax.dot_general(dzk, wbf[k], (((1,), (1,)), ((), ())), preferred_element_type=F32)
            dy_ref[:, k * W:(k + 1) * W] = dyk.astype(BF16)
            part = lax.dot_general(y_ref[:, k * W:(k + 1) * W], dzk, (((0,), (0,)), ((), ())),
                                   preferred_element_type=F32)

            @pl.when(i == 0)
            def _():
                acc[k] = part

            @pl.when(i > 0)
            def _():
                acc[k] += part

        @pl.when(i == nt - 1)
        def _():
            for k in range(4):
                for j in range(NDEV):
                    dwb_ref[j, k] = acc[k, :, j * GW:(j + 1) * GW].astype(BF16)

    return pl.pallas_call(
        body, name="b_merge", grid=(nt,),
        in_specs=[pl.BlockSpec((tm, D), lambda i: (i, 0)),
                  pl.BlockSpec((tm, 4 * D), lambda i: (i, 0)),
                  pl.BlockSpec((tm, 4 * D), lambda i: (i, 1)),
                  pl.BlockSpec((tm, 4 * W), lambda i: (i, 0)),
                  pl.BlockSpec(wb8.shape, lambda i: (0, 0, 0, 0))],
        out_specs=[pl.BlockSpec((tm, 4 * D), lambda i: (i, 1)),
                   pl.BlockSpec((tm, 4 * W), lambda i: (i, 0)),
                   pl.BlockSpec(wb8.shape, lambda i: (0, 0, 0, 0))],
        out_shape=[jax.ShapeDtypeStruct((T, COLS_IN), BF16),
                   jax.ShapeDtypeStruct((T, 4 * W), BF16),
                   jax.ShapeDtypeStruct(wb8.shape, BF16)],
        scratch_shapes=[pltpu.VMEM((4, W, D), BF16), pltpu.VMEM((4, W, D), F32)],
        compiler_params=_cp("arbitrary"),
    )(dm, z, proj, y, wb8)


def _mixer_bwd(proj, ca_saved, dy, dproj, sp, tm=256):
    nt = T // tm
    per = tm // HB
    ne = tm + HA
    last_blk = T // HA - 1
    conv_taps = [(HB - (CONF_K - 1) + k, k) for k in range(CONF_K)]

    def body(main_ref, hb_ref, ha_ref, ca_ref, cah_ref, dy_ref, dyh_ref, cw, cb, lg, lb, pw, ps, sc, gg, gb, ws, bst,
             dp_any, dp_ref, dcw_ref, dsc_ref, vec_ref, dpw_ref, dws_ref, dbs_ref, e1, e2, e3, stage):
        del dp_any, cb
        i = pl.program_id(0)
        keep_b = (i > 0).astype(F32)
        keep_a = (i < nt - 1).astype(F32)

        @pl.when(i == 0)
        def _():
            dcw_ref[...] = jnp.zeros_like(dcw_ref)
            dsc_ref[...] = jnp.zeros_like(dsc_ref)
            vec_ref[...] = jnp.zeros_like(vec_ref)
            dpw_ref[...] = jnp.zeros_like(dpw_ref)
            dws_ref[...] = jnp.zeros_like(dws_ref)
            dbs_ref[...] = jnp.zeros_like(dbs_ref)

        def mcol(c0):
            return main_ref[:, c0:c0 + W].astype(F32)

        def hbcol(c0):
            return hb_ref[:, c0:c0 + W].astype(F32)

        def hacol(c0):
            return ha_ref[:, c0:c0 + W].astype(F32)

        def load_dy(c0):
            e2[0:tm, :] = dy_ref[:, c0:c0 + W].astype(F32)
            e2[tm:ne, :] = dyh_ref[:, c0:c0 + W].astype(F32) * keep_a

        a = mcol(0)
        sa = _sig(mcol(W))
        e1[0:HB, :] = hbcol(0) * _sig(hbcol(W)) * keep_b
        e1[HB:HB + tm, :] = a * sa
        e1[HB + tm:HB + tm + SUBLANES, :] = jnp.zeros((SUBLANES, W), F32)
        e2[0:tm, :] = ca_ref[...].astype(F32)
        e2[tm:ne, :] = cah_ref[...].astype(F32)
        xh, rstd = _ln_stats(e2[0:ne, :])
        nn = xh * lg[...] + lb[...]
        s = _sig(nn)
        load_dy(0)
        dn = e2[0:ne, :] * (s * (1.0 + nn * (1.0 - s)))
        vec_ref[1:2, :] += _rowsum(dn[0:tm] * xh[0:tm])
        vec_ref[2:3, :] += _rowsum(dn[0:tm])
        dca = _ln_bwd(dn * lg[...], xh, rstd)
        e3[0:ne, :] = dca
        dmain = dca[0:tm]
        vec_ref[0:1, :] += _rowsum(dmain)
        _tap_grads(dmain, e1, [off for off, _ in conv_taps], tm, stage, dcw_ref)
        dglu = _tap_sum(e3, cw, [(CONF_K - 1 - k, k) for k in range(CONF_K)], tm, stage)
        dp_ref[:, 0:W] = (dglu * sa).astype(BF16)
        dp_ref[:, W:2 * W] = (dglu * a * sa * (1.0 - sa)).astype(BF16)

        pin = mcol(1024)
        e1[0:HB, :] = hbcol(1024) * keep_b
        e1[HB:HB + tm, :] = pin
        load_dy(W)
        dyb = e2[0:ne, :]
        pos_m = (i * tm + lax.broadcasted_iota(jnp.int32, (tm, 1), 0) + 1).astype(F32)
        pos_e = (i * tm + lax.broadcasted_iota(jnp.int32, (ne, 1), 0) + 1).astype(F32)
        for g, w in enumerate(POOL_WINDOWS):
            lo = g * GW
            acc = e1[pl.ds(HB, tm), lo:lo + GW]
            for j in range(1, w):
                acc = acc + e1[pl.ds(HB - j, tm), lo:lo + GW]
            pooled = (acc / jnp.minimum(pos_m, float(w)) - pin[:, lo:lo + GW]).astype(BF16)
            pwb = pw[g].astype(BF16)
            mixed = jnp.dot(pooled, pwb, preferred_element_type=F32)
            dyb_g = dyb[:, lo:lo + GW]
            vec_ref[3:4, lo:lo + GW] += _rowsum(dyb_g[0:tm] * mixed)
            dmb = (dyb_g * ps[:, lo:lo + GW]).astype(BF16)
            dpw_ref[g] += lax.dot_general(pooled, dmb[0:tm], (((0,), (0,)), ((), ())), preferred_element_type=F32)
            dpool = lax.dot_general(dmb, pwb, (((1,), (1,)), ((), ())), preferred_element_type=F32)
            e3[0:ne, lo:lo + GW] = dpool / jnp.minimum(pos_e, float(w))
            back = e3[pl.ds(0, tm), lo:lo + GW]
            for j in range(1, w):
                back = back + e3[pl.ds(j, tm), lo:lo + GW]
            dp_ref[:, 1024 + lo:1024 + lo + GW] = (back - dpool[0:tm]).astype(BF16)

        cg = mcol(2048)
        hx = mcol(2560)
        e1[0:HB, :] = hbcol(2048) * hbcol(2560) * keep_b
        e1[HB:HB + tm, :] = cg * hx
        load_dy(2 * W)
        dyc = e2[0:tm, :]
        dconv = dyc * mcol(1536)
        e3[0:tm, :] = dconv
        e3[tm:ne, :] = e2[tm:ne, :] * hacol(1536)
        cv = sc[0:1, :] * e1[pl.ds(HB - 2, tm), :]
        for k in range(1, SC_K):
            cv = cv + sc[k:k + 1, :] * e1[pl.ds(HB - 2 + k, tm), :]
        dp_ref[:, 1536:2048] = (dyc * cv).astype(BF16)
        for k in range(SC_K):
            dsc_ref[k:k + 1, :] += _rowsum(dconv * e1[pl.ds(HB - 2 + k, tm), :])
        dq = sc[0:1, :] * e3[pl.ds(2, tm), :]
        for k in range(1, SC_K):
            dq = dq + sc[k:k + 1, :] * e3[pl.ds(2 - k, tm), :]
        dp_ref[:, 2048:2560] = (dq * hx).astype(BF16)
        dp_ref[:, 2560:3072] = (dq * cg).astype(BF16)

        u = mcol(3072)
        vh, vr = _ln_stats(mcol(3584))
        vn = (vh * gg[...] + gb[...]).astype(BF16)
        dyd = dy_ref[:, 3 * W:4 * W].astype(F32)
        tri = _tri_mask()
        for g in range(4):
            lo = g * GW
            wm = jnp.where(tri, ws[g], 0.0).astype(BF16)
            dws_g = jnp.zeros((GW, GW), F32)
            dbs_g = jnp.zeros((GW, 1), F32)
            for c in range(tm // GW):
                r0 = c * GW
                blk = vn[r0:r0 + GW, lo:lo + GW]
                sg = jnp.dot(wm, blk, preferred_element_type=F32) + bst[:, g:g + 1]
                dyd_b = dyd[r0:r0 + GW, lo:lo + GW]
                dp_ref[r0:r0 + GW, 3072 + lo:3072 + lo + GW] = (dyd_b * sg).astype(BF16)
                dsg = dyd_b * u[r0:r0 + GW, lo:lo + GW]
                dsgb = dsg.astype(BF16)
                dbs_g = dbs_g + jnp.sum(dsg, axis=-1, keepdims=True)
                dws_g = dws_g + lax.dot_general(dsgb, blk, (((1,), (1,)), ((), ())), preferred_element_type=F32)
                e1[r0:r0 + GW, lo:lo + GW] = lax.dot_general(wm, dsgb, (((0,), (0,)), ((), ())),
                                                             preferred_element_type=F32)
            dws_ref[g] += jnp.where(tri, dws_g, 0.0)
            dbs_ref[g] += jnp.broadcast_to(dbs_g, (GW, GW))
        dvn = e1[0:tm, :]
        vec_ref[4:5, :] += _rowsum(dvn * vh)
        vec_ref[5:6, :] += _rowsum(dvn)
        dp_ref[:, 3584:4096] = _ln_bwd(dvn * gg[...], vh, vr).astype(BF16)

    plist = _mixer_params(sp)
    in_specs = [pl.BlockSpec((tm, MIX_COLS), lambda i: (i, 0)),
                pl.BlockSpec((HB, MIX_COLS), lambda i: (jnp.maximum(i * per - 1, 0), 0)),
                pl.BlockSpec((HA, MIX_COLS), lambda i: (jnp.minimum((i + 1) * per, last_blk), 0)),
                pl.BlockSpec((tm, W), lambda i: (i, 0)),
                pl.BlockSpec((HA, W), lambda i: (jnp.minimum((i + 1) * per, last_blk), 0)),
                pl.BlockSpec((tm, 4 * W), lambda i: (i, 0)),
                pl.BlockSpec((HA, 4 * W), lambda i: (jnp.minimum((i + 1) * per, last_blk), 0))]
    in_specs += _small_specs(plist)
    in_specs += [pl.BlockSpec(memory_space=pl.ANY)]
    z2 = lambda i: (0, 0)
    z3 = lambda i: (0, 0, 0)
    out_specs = [pl.BlockSpec((tm, MIX_COLS), lambda i: (i, 0)),
                 pl.BlockSpec((32, W), z2), pl.BlockSpec((8, W), z2), pl.BlockSpec((8, W), z2),
                 pl.BlockSpec((4, GW, GW), z3), pl.BlockSpec((4, GW, GW), z3), pl.BlockSpec((4, GW, GW), z3)]
    out_shape = [jax.ShapeDtypeStruct((T, COLS_IN), BF16),
                 jax.ShapeDtypeStruct((32, W), F32), jax.ShapeDtypeStruct((8, W), F32),
                 jax.ShapeDtypeStruct((8, W), F32),
                 jax.ShapeDtypeStruct((4, GW, GW), F32), jax.ShapeDtypeStruct((4, GW, GW), F32),
                 jax.ShapeDtypeStruct((4, GW, GW), F32)]
    n_in = 7 + len(plist)
    return pl.pallas_call(
        body, name="b_mixers", grid=(nt,), in_specs=in_specs, out_specs=out_specs, out_shape=out_shape,
        scratch_shapes=[pltpu.VMEM((HB + ne, W), F32), pltpu.VMEM((ne, W), F32), pltpu.VMEM((ne, W), F32),
                        pltpu.VMEM((ne + SUBLANES, W), F32)],
        input_output_aliases={n_in: 0},
        compiler_params=_cp("arbitrary"),
    )(proj, proj, proj, ca_saved, ca_saved, dy, dy, *plist, dproj)


def _norm_first(x, g, tm=512):
    def body(x_ref, g_ref, o_ref):
        o_ref[...] = _rms(x_ref[...], g_ref[...]).astype(BF16)

    return pl.pallas_call(
        body, name="f_norm0", grid=(T // tm,),
        in_specs=[pl.BlockSpec((tm, D), lambda i: (i, 0)), pl.BlockSpec((1, D), lambda i: (0, 0))],
        out_specs=pl.BlockSpec((tm, D), lambda i: (i, 0)),
        out_shape=jax.ShapeDtypeStruct((T, D), BF16), compiler_params=_cp("arbitrary"),
    )(x, g)


def _loss_head(x, target, g, tm=256):
    def body(x_ref, t_ref, g_ref, dx_ref, dg_ref, loss_ref):
        i = pl.program_id(0)
        x = x_ref[...]
        r = lax.rsqrt(jnp.mean(x * x, axis=-1, keepdims=True) + EPS)
        xh = x * r
        gv = g_ref[...]
        e = xh * gv - t_ref[...]
        dyv = e * (1.0 / D)
        part = jnp.sum(_rowsum(e * e), axis=-1, keepdims=True) * (0.5 / D)
        u = dyv * gv
        dx_ref[...] = r * (u - xh * jnp.mean(u * xh, axis=-1, keepdims=True))
        dgp = _rowsum(dyv * xh)

        @pl.when(i == 0)
        def _():
            dg_ref[...] = dgp
            loss_ref[...] = jnp.broadcast_to(part, (1, GW))

        @pl.when(i > 0)
        def _():
            dg_ref[...] += dgp
            loss_ref[...] += jnp.broadcast_to(part, (1, GW))

    return pl.pallas_call(
        body, name="loss_head", grid=(T // tm,),
        in_specs=[pl.BlockSpec((tm, D), lambda i: (i, 0)), pl.BlockSpec((tm, D), lambda i: (i, 0)),
                  pl.BlockSpec((1, D), lambda i: (0, 0))],
        out_specs=[pl.BlockSpec((tm, D), lambda i: (i, 0)), pl.BlockSpec((1, D), lambda i: (0, 0)),
                   pl.BlockSpec((1, GW), lambda i: (0, 0))],
        out_shape=[jax.ShapeDtypeStruct((T, D), F32), jax.ShapeDtypeStruct((1, D), F32),
                   jax.ShapeDtypeStruct((1, GW), F32)],
        compiler_params=_cp("arbitrary"),
    )(x, target, g)


def _ple_bwd(dx4, sv, w_pleg8, g_ple, tm=256):
    nt = T // tm
    ple_dim = sv["p"].shape[1]

    def body(dx_ref, gl_ref, pe_ref, x_ref, h_ref, p_ref, g_ref, wg_ref,
             dx3_ref, dx3b_ref, dg_ref, dwg_ref, dwp_ref, acc_g, acc_p):
        i = pl.program_id(0)
        d = dx_ref[...]
        s = _sig(gl_ref[...].astype(F32))
        dpe = (d * s).astype(BF16)
        dgl = (d * pe_ref[...].astype(F32) * s * (1.0 - s)).astype(BF16)
        dh = lax.dot_general(dgl, wg_ref[...], (((1,), (1,)), ((), ())), preferred_element_type=F32)
        dx, dgp = _rms_bwd(dh, x_ref[...], g_ref[...], d)
        dx3_ref[...] = dx
        dx3b_ref[...] = dx.astype(BF16)
        part_g = lax.dot_general(h_ref[...], dgl, (((0,), (0,)), ((), ())), preferred_element_type=F32)
        part_p = lax.dot_general(p_ref[...], dpe, (((0,), (0,)), ((), ())), preferred_element_type=F32)

        @pl.when(i == 0)
        def _():
            dg_ref[...] = dgp
            acc_g[...] = part_g
            acc_p[...] = part_p

        @pl.when(i > 0)
        def _():
            dg_ref[...] += dgp
            acc_g[...] += part_g
            acc_p[...] += part_p

        @pl.when(i == nt - 1)
        def _():
            for j in range(NDEV):
                dwg_ref[j] = acc_g[j * GW:(j + 1) * GW, :].astype(BF16)
                dwp_ref[j] = acc_p[:, j * GW:(j + 1) * GW].astype(BF16)

    tile = lambda w: pl.BlockSpec((tm, w), lambda i: (i, 0))
    const = lambda shp: pl.BlockSpec(shp, lambda i: (0,) * len(shp))
    return pl.pallas_call(
        body, name="b_ple", grid=(nt,),
        in_specs=[tile(D), tile(D), tile(D), tile(D), tile(D), tile(ple_dim), const((1, D)), const((D, D))],
        out_specs=[tile(D), tile(D), const((1, D)), const((NDEV, GW, D)), const((NDEV, ple_dim, GW))],
        out_shape=[jax.ShapeDtypeStruct((T, D), F32), jax.ShapeDtypeStruct((T, D), BF16),
                   jax.ShapeDtypeStruct((1, D), F32), jax.ShapeDtypeStruct((NDEV, GW, D), BF16),
                   jax.ShapeDtypeStruct((NDEV, ple_dim, GW), BF16)],
        scratch_shapes=[pltpu.VMEM((D, D), F32), pltpu.VMEM((ple_dim, D), F32)],
        compiler_params=_cp("arbitrary"),
    )(dx4, sv["gl"], sv["pe"], sv["x3"], sv["h3"], sv["p"], g_ple, w_pleg8.reshape(D, D))


def _layer_fwd(x, h1, p_bf, gw, sp, g_next):
    proj, = _mm(h1, gw["w_in"], mode="out", name="f_proj", outs=[BF16], tm=T)
    y, ca = _mixer_fwd(proj, sp)
    z, merged = _merge_fwd(y, proj, gw["w_branch"])
    x2, h2 = _mm(merged, gw["w_out"].reshape(1, D, D), mode="acc", name="f_out", outs=[F32, BF16], tm=T // 2,
                 tiles=[x], params=[sp["g_mlp"]], epi=_epi_res_norm)
    up, = _mm(h2, gw["w_up"], mode="out", name="f_up", outs=[BF16], tm=T)
    x3, h3 = _mm(up, gw["w_down"].reshape(1, 4 * D, D), mode="acc", name="f_down", outs=[F32, BF16], tm=T // 4,
                 tiles=[x2], params=[sp["g_ple"]], epi=_epi_res_norm, a_pre=_relu2_bf16)
    x4, gl, hn, pe = _mm(h3, gw["w_pleg"].reshape(1, D, D), mode="acc", name="f_gate", tm=T // 2,
                         outs=[F32, BF16, BF16, BF16], tiles=[x3, p_bf], params=[g_next, gw["w_ple"]], epi=_epi_ple)
    saved = dict(x=x, h1=h1, proj=proj, y=y, ca=ca, z=z, merged=merged, x2=x2, h2=h2, up=up, x3=x3, h3=h3,
                 pe=pe, gl=gl, p=p_bf)
    return x4, hn, saved


def _layer_bwd(dx4, sv, gw, sp, submit):
    dw = {}
    dx3, dx3b, dg_ple, dw["w_pleg"], dw["w_ple"] = _ple_bwd(dx4, sv, gw["w_pleg"], sp["g_ple"])
    dup, = _mm(dx3b, gw["w_down"], mode="out", trans_b=True, name="b_dact", outs=[BF16], tm=T,
               tiles=[sv["up"]], epi=_epi_dup)
    dw["w_down"] = _mm_tn(sv["up"], dx3b, nj=NDEV, split="row", name="b_dw_down", a_pre=_relu2_bf16)
    dw["w_up"] = _mm_tn(sv["h2"], dup, nj=NDEV, split="col", name="b_dw_up")
    dx2, dx2b, dg_mlp = _mm(dup, gw["w_up"], mode="full", trans_b=True, name="b_dh2", tm=T // 4,
                            outs=[F32, BF16], tiles=[sv["x2"], dx3], params=[sp["g_mlp"]], epi=_epi_rms_bwd, reds=[D])
    dm, = _mm(dx2b, gw["w_out"].reshape(1, D, D), mode="acc", trans_b=True, name="b_dmerged", outs=[BF16],
              tm=T // 2)
    dw["w_out"] = _mm_tn(sv["merged"], dx2b, nj=NDEV, split="row", name="b_dw_out")
    dproj, dy, dw["w_branch"] = _merge_bwd(dm, sv["z"], sv["proj"], sv["y"], gw["w_branch"])
    dy = submit(dw, BIG[1:], dy)
    dproj, dcw, dsc, vec, dpw, dws, dbs = _mixer_bwd(sv["proj"], sv["ca"], dy, dproj, sp)
    dw["w_in"] = _mm_tn(sv["h1"], dproj, nj=NDEV, split="col", name="b_dw_in")
    dw["w_in"], dproj = lax.optimization_barrier((dw["w_in"], dproj))
    dproj = submit(dw, BIG[:1], dproj)
    dx, dg_mix = _mm(dproj, gw["w_in"], mode="full", trans_b=True, name="b_dh1", outs=[F32], tm=T // 4,
                     tiles=[sv["x"], dx2], params=[sp["g_mix"]], epi=_epi_rms_bwd, reds=[D])
    small = dict(norm_mix=dg_mix[0], conf_dw=dcw[:CONF_K], conf_dw_b=vec[0], conf_ln_g=vec[1], conf_ln_b=vec[2],
                 pool_w=dpw, pool_scale=vec[3], sc_conv=dsc[:SC_K], gmlp_ln_g=vec[4], gmlp_ln_b=vec[5],
                 gmlp_ws=dws, gmlp_bs=dbs[:, :, 0], norm_mlp=dg_mlp[0], norm_ple=dg_ple[0])
    return dx, small


ANY = pl.BlockSpec(memory_space=pl.ANY)


def _mesh_pos():
    return lax.axis_index("x"), lax.axis_index("y"), lax.axis_index("c")


def _other_chips(x, y):
    return [(1 - x, y), (x, 1 - y), (1 - x, 1 - y)]


def _launch_comm(body, peers_of, operands, out_shapes, sems, name, seq_id):
    n_in, n_out = len(operands), len(out_shapes)
    if seq_id is None:
        return pl.pallas_call(body, name=name, in_specs=[ANY] * n_in, out_specs=[ANY] * n_out,
                              out_shape=out_shapes, scratch_shapes=sems)(*operands)

    def seq_body(*refs):
        peers = peers_of(*_mesh_pos())
        barrier = pltpu.get_barrier_semaphore()
        for peer in peers:
            pl.semaphore_signal(barrier, inc=1, device_id=peer, device_id_type=MESH)
        pl.semaphore_wait(barrier, len(peers))
        body(*refs)

    return pl.kernel(seq_body, name=name, out_type=out_shapes,
                     mesh=plsc.ScalarSubcoreMesh(axis_name="seq", num_cores=1), scratch_types=sems,
                     compiler_params=pltpu.CompilerParams(collective_id=seq_id))(*operands)


def _all_gather(shards, name, seq_id=None):
    n = len(shards)

    def body(*refs):
        s_refs, o_refs = refs[:n], refs[n:2 * n]
        send_sems, recv_sems, local_sems = refs[2 * n:]
        x, y, c = _mesh_pos()
        me = 4 * x + 2 * y + c
        here = (x, y, c)
        sibling = (x, y, 1 - c)
        chips = _other_chips(x, y)

        def slot(px, py, pc):
            return 4 * px + 2 * py + pc

        def copy(t, k, slot_idx, to, src=None):
            dst = o_refs[t].at[slot_idx]
            return pltpu.make_async_remote_copy(
                src_ref=dst if src is None else src, dst_ref=dst,
                send_sem=send_sems.at[t * 7 + k], recv_sem=recv_sems.at[t * 7 + k],
                device_id=to, device_id_type=MESH)

        mine = [pltpu.make_async_copy(s_refs[t], o_refs[t].at[me], local_sems.at[t]) for t in range(n)]
        for cp in mine:
            cp.start()
        first = []
        for t in range(n):
            for j, chip in enumerate(chips):
                first.append(copy(t, 1 + j, me, (*chip, c), src=s_refs[t]))
        for t in range(n):
            first.append(copy(t, 0, me, sibling, src=s_refs[t]))
        for cp in first:
            cp.start()
        passed = []
        for t in range(n):
            for j, chip in enumerate(chips):
                copy(t, 1 + j, slot(*chip, c), here).wait_recv()
                fwd = copy(t, 4 + j, slot(*chip, c), sibling)
                fwd.start()
                passed.append(fwd)
        for t in range(n):
            copy(t, 0, slot(x, y, 1 - c), here).wait_recv()
            for j, chip in enumerate(chips):
                copy(t, 4 + j, slot(*chip, 1 - c), here).wait_recv()
        for cp in first + passed:
            cp.wait_send()
        for cp in mine:
            cp.wait()

    def peers_of(x, y, c):
        return [(x, y, 1 - c)] + [(*chip, c) for chip in _other_chips(x, y)]

    return _launch_comm(
        body, peers_of, shards, [jax.ShapeDtypeStruct((NDEV,) + s.shape, s.dtype) for s in shards],
        [pltpu.SemaphoreType.DMA((7 * n,)), pltpu.SemaphoreType.DMA((7 * n,)), pltpu.SemaphoreType.DMA((n,))],
        name, seq_id)


def _rs_exchange(p4s, qs, name, seq_id=None):
    n_p, n_q = len(p4s), len(qs)

    def body(*refs):
        p_refs, q_refs = refs[:n_p], refs[n_p:n_p + n_q]
        rb_refs, rc_refs = refs[n_p + n_q:2 * n_p + n_q], refs[2 * n_p + n_q:2 * (n_p + n_q)]
        pair_send, pair_recv, chip_send, chip_recv, local_sems = refs[2 * (n_p + n_q):]
        x, y, c = _mesh_pos()
        a_idx = 2 * x + y
        chips = _other_chips(x, y)
        mine = [pltpu.make_async_copy(q_refs[t].at[a_idx], rc_refs[t].at[a_idx], local_sems.at[t])
                for t in range(n_q)]
        sends = []
        for t in range(n_q):
            for j, chip in enumerate(chips):
                sends.append(pltpu.make_async_remote_copy(
                    src_ref=q_refs[t].at[2 * chip[0] + chip[1]], dst_ref=rc_refs[t].at[a_idx],
                    send_sem=chip_send.at[t * 3 + j], recv_sem=chip_recv.at[t * 3 + j],
                    device_id=(*chip, c), device_id_type=MESH))
        pairs = [pltpu.make_async_remote_copy(
            src_ref=p_refs[t].at[:, 1 - c], dst_ref=rb_refs[t], send_sem=pair_send.at[t], recv_sem=pair_recv.at[t],
            device_id=(x, y, 1 - c), device_id_type=MESH) for t in range(n_p)]
        for cp in sends + mine + pairs:
            cp.start()
        for cp in pairs:
            cp.wait()
        for t in range(n_q):
            for j, chip in enumerate(chips):
                landed = rc_refs[t].at[2 * chip[0] + chip[1]]
                pltpu.make_async_remote_copy(
                    src_ref=landed, dst_ref=landed, send_sem=chip_send.at[t * 3 + j],
                    recv_sem=chip_recv.at[t * 3 + j], device_id=(x, y, c), device_id_type=MESH).wait_recv()
        for cp in sends:
            cp.wait_send()
        for cp in mine:
            cp.wait()

    def peers_of(x, y, c):
        peers = [(x, y, 1 - c)] if n_p else []
        return peers + ([(*chip, c) for chip in _other_chips(x, y)] if n_q else [])

    out_shapes = [jax.ShapeDtypeStruct((NCHIP,) + p.shape[2:], p.dtype) for p in p4s]
    out_shapes += [jax.ShapeDtypeStruct(q.shape, q.dtype) for q in qs]
    sems = [pltpu.SemaphoreType.DMA((max(n_p, 1),)), pltpu.SemaphoreType.DMA((max(n_p, 1),)),
            pltpu.SemaphoreType.DMA((max(3 * n_q, 1),)), pltpu.SemaphoreType.DMA((max(3 * n_q, 1),)),
            pltpu.SemaphoreType.DMA((max(n_q, 1),))]
    got = _launch_comm(body, peers_of, list(p4s) + list(qs), out_shapes, sems, name, seq_id)
    return got[:n_p], got[n_p:]


def _pair_sum(p4s, rbs, c_idx, name, nst=4):
    n = len(p4s)
    trs = [p.shape[2] // nst for p in p4s]

    def body(c_ref, *refs):
        del c_ref
        p_refs, r_refs, o_refs = refs[:n], refs[n:2 * n], refs[2 * n:]
        for p_ref, r_ref, o_ref in zip(p_refs, r_refs, o_refs):
            o_ref[...] = (p_ref[...].astype(F32) + r_ref[...].astype(F32)).astype(o_ref.dtype)

    in_specs = [pl.BlockSpec((None, None, tr, p.shape[3]), lambda b, i, c_ref: (b, c_ref[0], i, 0))
                for p, tr in zip(p4s, trs)]
    in_specs += [pl.BlockSpec((None, tr, p.shape[3]), lambda b, i, c_ref: (b, i, 0)) for p, tr in zip(p4s, trs)]
    out_specs = [pl.BlockSpec((None, tr, p.shape[3]), lambda b, i, c_ref: (b, i, 0)) for p, tr in zip(p4s, trs)]
    return pl.pallas_call(
        body, name=name,
        grid_spec=pltpu.PrefetchScalarGridSpec(num_scalar_prefetch=1, grid=(NCHIP, nst), in_specs=in_specs,
                                               out_specs=out_specs),
        out_shape=[jax.ShapeDtypeStruct((NCHIP,) + p.shape[2:], p.dtype) for p in p4s],
        compiler_params=_cp("arbitrary", "arbitrary"),
    )(c_idx, *p4s, *rbs)


class _GradientPipeline:
    def __init__(self, c_idx, results):
        self.c_idx, self.results, self.pending = c_idx, results, None

    def _sum_pending(self, chain):
        names, layer, p4s, rbs = self.pending
        qs = _pair_sum(p4s, rbs, self.c_idx, name="rs_pairsum_" + ("first" if len(names) == 1 else "rest"))
        return lax.optimization_barrier((chain, qs))

    def submit(self, dw, names, layer, chain):
        qs, tag, seq_id = [], "pair", 3
        if self.pending is not None:
            chain, qs = self._sum_pending(chain)
            tag, seq_id = "pair_chip", 4
        p4s = [dw[n].reshape((NCHIP, 2) + BIG_SHARD[n]) for n in names]
        rbs, rcs = _rs_exchange(p4s, qs, name="rs_%s_%d" % (tag, len(names)), seq_id=seq_id)
        self._record(rcs)
        self.pending = (names, layer, p4s, rbs)
        return chain

    def finish(self, chain):
        chain, qs = self._sum_pending(chain)
        self._record(_rs_exchange([], qs, name="rs_chip_last", seq_id=5)[1])
        self.pending = None
        return chain

    def _record(self, rcs):
        if rcs:
            names, layer = self.pending[:2]
            for n, rc in zip(names, rcs):
                self.results[n][layer] = rc


def _adamw(w, g, m, v):
    m = ADAM_B1 * m + (1.0 - ADAM_B1) * g
    v = ADAM_B2 * v + (1.0 - ADAM_B2) * (g * g)
    m_hat = m / (1.0 - ADAM_B1 ** ADAM_STEP)
    v_hat = v / (1.0 - ADAM_B2 ** ADAM_STEP)
    delta = -ADAM_LR * (m_hat / (jnp.sqrt(v_hat) + ADAM_EPS) + ADAM_WD * w)
    return delta, m, v


def _adam_sharded(rcs, w, m, v, tr, name, first_layer, partial=None):
    _, r, c = w.shape
    nst = r // tr
    n_l = len(rcs)

    def body(*refs):
        rc_refs = refs[:n_l]
        w_ref, m_ref, v_ref = refs[n_l:n_l + 3]
        g_out, d_out, m_out, v_out = refs[-4:]
        layer = pl.program_id(0)
        for k, rc in enumerate(rc_refs):
            @pl.when(layer == k)
            def _():
                g = rc[0].astype(F32) + rc[1].astype(F32) + rc[2].astype(F32) + rc[3].astype(F32)
                delta, m_new, v_new = _adamw(w_ref[...], g, m_ref[...], v_ref[...])
                g_out[...] = g
                d_out[...] = delta
                m_out[...] = m_new
                v_out[...] = v_new

    rc_specs = [pl.BlockSpec((NCHIP, tr, c), lambda l, i, k=k: (0, jnp.where(l == k, i, 0), 0)) for k in range(n_l)]
    wspec = pl.BlockSpec((None, tr, c), lambda l, i: (first_layer + l, i, 0))
    carried = [] if partial is None else list(partial)
    return pl.pallas_call(
        body, name=name, grid=(n_l, nst),
        in_specs=rc_specs + [wspec] * 3 + [pl.BlockSpec(memory_space=pl.ANY)] * len(carried),
        out_specs=[wspec] * 4, out_shape=[jax.ShapeDtypeStruct(w.shape, F32)] * 4,
        input_output_aliases={n_l + 3 + k: k for k in range(len(carried))},
        compiler_params=_cp("arbitrary", "arbitrary"),
    )(*rcs, w, m, v, *carried)


def _adam_packed(g, w, m, v, direct):
    n_d = len(direct)

    def pieces(shape):
        width = shape[-1]
        count = 1
        for s in shape[:-1]:
            count *= s
        per_row = D // width
        out = []
        for k in range(count):
            idx = (k,) if len(shape) == 2 else (k // shape[1], k % shape[1])
            out.append((idx, k // per_row, (k % per_row) * width, width))
        return out

    def body(g_ref, w_ref, m_ref, v_ref, d_out, m_out, v_out, *outs):
        delta, m_new, v_new = _adamw(w_ref[...], g_ref[...], m_ref[...], v_ref[...])
        d_out[...] = delta
        m_out[...] = m_new
        v_out[...] = v_new
        for a, (_, row0, shape) in enumerate(direct):
            for src, dst in zip((g_ref, d_out, m_out, v_out), outs[4 * a:4 * a + 4]):
                for idx, row, lane0, width in pieces(shape):
                    piece = src[pl.ds(row0 + row, 1), lane0:lane0 + width]
                    if len(idx) == 1:
                        dst[pl.ds(idx[0], 1), :] = piece
                    else:
                        dst[idx[0], pl.ds(idx[1], 1), :] = piece

    out_shape = [jax.ShapeDtypeStruct(g.shape, F32)] * 3
    for _, _, shape in direct:
        out_shape += [jax.ShapeDtypeStruct(shape, F32)] * 4
    res = pl.pallas_call(body, name="adam_small", out_shape=out_shape,
                         compiler_params=pltpu.CompilerParams(vmem_limit_bytes=VMEM_LIMIT_BYTES))(g, w, m, v)
    return res[:3], {name: res[3 + 4 * a:7 + 4 * a] for a, (name, _, _) in enumerate(direct)}


def _sum4(rc):
    def body(rc_ref, o_ref):
        o_ref[...] = rc_ref[0] + rc_ref[1] + rc_ref[2] + rc_ref[3]

    return pl.pallas_call(
        body, name="small_sum", out_shape=jax.ShapeDtypeStruct(rc.shape[1:], F32),
    )(rc)


BIG = ("w_in", "w_branch", "w_out", "w_up", "w_down", "w_ple", "w_pleg")
BIG_SHARD = {"w_in": (D, D), "w_branch": (4 * W, GW), "w_out": (GW, D), "w_up": (D, W), "w_down": (W, D),
             "w_ple": (256, GW), "w_pleg": (GW, D)}
ADAM_ROWS = {"w_in": 256, "w_branch": 512, "w_out": 128, "w_up": 256, "w_down": 256, "w_ple": 256, "w_pleg": 128}
SMALL = (("norm_mix", (DEPTH, D)), ("conf_dw", (DEPTH, CONF_K, W)), ("conf_dw_b", (DEPTH, W)),
         ("conf_ln_g", (DEPTH, W)), ("conf_ln_b", (DEPTH, W)), ("pool_w", (DEPTH, 4, GW, GW)),
         ("pool_scale", (DEPTH, W)), ("sc_conv", (DEPTH, SC_K, W)), ("gmlp_ln_g", (DEPTH, W)),
         ("gmlp_ln_b", (DEPTH, W)), ("gmlp_ws", (DEPTH, 4, GW, GW)), ("gmlp_bs", (DEPTH, 4, GW)),
         ("norm_mlp", (DEPTH, D)), ("norm_ple", (DEPTH, D)), ("norm_final", (D,)))
CHANNEL_SHARDED = ("conf_dw", "sc_conv")
SMALL_ROWS = 80


def _pack(arrs, rows):
    flat = jnp.concatenate([a.reshape(-1) for a in arrs])
    return jnp.pad(flat, (0, rows * D - flat.shape[0])).reshape(rows, D)


def _unpack(packed, shapes):
    flat = packed.reshape(-1)
    out, off = [], 0
    for shp in shapes:
        size = 1
        for s in shp:
            size *= s
        out.append(flat[off:off + size].reshape(shp))
        off += size
    return out


def kernel(x, p, norm_mix, w_in, conf_dw, conf_dw_b, conf_ln_g, conf_ln_b, pool_w, pool_scale, sc_conv, gmlp_ln_g, gmlp_ln_b, gmlp_ws, gmlp_bs, w_branch, w_out, norm_mlp, w_up, w_down, norm_ple, w_ple, w_ple_gate, norm_final, loss_target, m_norm_mix, m_w_in, m_conf_dw, m_conf_dw_b, m_conf_ln_g, m_conf_ln_b, m_pool_w, m_pool_scale, m_sc_conv, m_gmlp_ln_g, m_gmlp_ln_b, m_gmlp_ws, m_gmlp_bs, m_w_branch, m_w_out, m_norm_mlp, m_w_up, m_w_down, m_norm_ple, m_w_ple, m_w_ple_gate, m_norm_final, v_norm_mix, v_w_in, v_conf_dw, v_conf_dw_b, v_conf_ln_g, v_conf_ln_b, v_pool_w, v_pool_scale, v_sc_conv, v_gmlp_ln_g, v_gmlp_ln_b, v_gmlp_ws, v_gmlp_bs, v_w_branch, v_w_out, v_norm_mlp, v_w_up, v_w_down, v_norm_ple, v_w_ple, v_w_ple_gate, v_norm_final):
    weights = dict(norm_mix=norm_mix, w_in=w_in, conf_dw=conf_dw, conf_dw_b=conf_dw_b, conf_ln_g=conf_ln_g,
                   conf_ln_b=conf_ln_b, pool_w=pool_w, pool_scale=pool_scale, sc_conv=sc_conv, gmlp_ln_g=gmlp_ln_g,
                   gmlp_ln_b=gmlp_ln_b, gmlp_ws=gmlp_ws, gmlp_bs=gmlp_bs, w_branch=w_branch, w_out=w_out,
                   norm_mlp=norm_mlp, w_up=w_up, w_down=w_down, norm_ple=norm_ple, w_ple=w_ple, w_pleg=w_ple_gate,
                   norm_final=norm_final)
    mom1 = dict(norm_mix=m_norm_mix, w_in=m_w_in, conf_dw=m_conf_dw, conf_dw_b=m_conf_dw_b, conf_ln_g=m_conf_ln_g,
                conf_ln_b=m_conf_ln_b, pool_w=m_pool_w, pool_scale=m_pool_scale, sc_conv=m_sc_conv,
                gmlp_ln_g=m_gmlp_ln_g, gmlp_ln_b=m_gmlp_ln_b, gmlp_ws=m_gmlp_ws, gmlp_bs=m_gmlp_bs,
                w_branch=m_w_branch, w_out=m_w_out, norm_mlp=m_norm_mlp, w_up=m_w_up, w_down=m_w_down,
                norm_ple=m_norm_ple, w_ple=m_w_ple, w_pleg=m_w_ple_gate, norm_final=m_norm_final)
    mom2 = dict(norm_mix=v_norm_mix, w_in=v_w_in, conf_dw=v_conf_dw, conf_dw_b=v_conf_dw_b, conf_ln_g=v_conf_ln_g,
                conf_ln_b=v_conf_ln_b, pool_w=v_pool_w, pool_scale=v_pool_scale, sc_conv=v_sc_conv,
                gmlp_ln_g=v_gmlp_ln_g, gmlp_ln_b=v_gmlp_ln_b, gmlp_ws=v_gmlp_ws, gmlp_bs=v_gmlp_bs,
                w_branch=v_w_branch, w_out=v_w_out, norm_mlp=v_norm_mlp, w_up=v_w_up, w_down=v_w_down,
                norm_ple=v_norm_ple, w_ple=v_w_ple, w_pleg=v_w_ple_gate, norm_final=v_norm_final)

    xi, yi, ci = _mesh_pos()
    me = 4 * xi + 2 * yi + ci
    c_idx = jnp.reshape(ci, (1,)).astype(jnp.int32)

    gathered, conf_full, sc_full = [], [], []
    for l in range(DEPTH):
        shard = lambda n: weights[n][l].astype(BF16).reshape(BIG_SHARD[n])
        w_in_g, conf_g, sc_g = _all_gather([shard("w_in"), conf_dw[l], sc_conv[l]], name="ag_first", seq_id=1)
        rest = _all_gather([shard(n) for n in BIG[1:]], name="ag_rest", seq_id=2)
        gw = dict(zip(BIG[1:], rest), w_in=w_in_g)
        gw["w_branch"] = gw["w_branch"].reshape(NDEV, 4, W, GW)
        gathered.append(gw)
        conf_full.append(conf_g)
        sc_full.append(sc_g)

    def small_params(l):
        return dict(cw=conf_full[l], cb=conf_dw_b[l][None], lg=conf_ln_g[l][None], lb=conf_ln_b[l][None],
                    pw=pool_w[l], ps=pool_scale[l][None], sc=sc_full[l], gg=gmlp_ln_g[l][None],
                    gb=gmlp_ln_b[l][None], ws=gmlp_ws[l], bst=gmlp_bs[l].T, g_mix=norm_mix[l][None],
                    g_mlp=norm_mlp[l][None], g_ple=norm_ple[l][None])

    xc = x.reshape(T, D)
    small_names = [n for n, _ in SMALL]

    def in_gradient_layout(n, shard, shape):
        if n not in CHANNEL_SHARDED:
            return shard
        return lax.dynamic_update_slice(jnp.zeros(shape, F32), shard, (0, 0, me * (W // NDEV)))

    small_state = [_pack([in_gradient_layout(n, src[n], shape) for n, shape in SMALL], NDEV * SMALL_ROWS)
                   for src in (weights, mom1, mom2)]
    xc, small_state = lax.optimization_barrier((xc, small_state))
    p_bf = p.reshape(DEPTH, T, 256).astype(BF16)
    h = _norm_first(xc, norm_mix[0][None])
    saved = []
    for l in range(DEPTH):
        g_next = norm_mix[l + 1][None] if l + 1 < DEPTH else norm_final[None]
        h, conf_g, sc_g = lax.optimization_barrier((h, conf_full[l], sc_full[l]))
        conf_full[l] = conf_g.transpose(1, 0, 2).reshape(CONF_K, W)
        sc_full[l] = sc_g.transpose(1, 0, 2).reshape(SC_K, W)
        xc, h, sv = _layer_fwd(xc, h, p_bf[l], gathered[l], small_params(l), g_next)
        saved.append(sv)

    dxc, dg_final, loss_part = _loss_head(xc, loss_target.reshape(T, D), norm_final[None])
    loss = lax.psum(loss_part[0, 0], ("x", "y", "c"))
    small_grads = [None] * DEPTH
    rcs = {n: [None] * DEPTH for n in BIG}
    pipeline = _GradientPipeline(c_idx, rcs)
    for l in reversed(range(DEPTH)):
        dxc, small_grads[l] = _layer_bwd(dxc, saved[l], gathered[l], small_params(l),
                                         lambda dw, names, value, l=l: pipeline.submit(dw, names, l, value))

    def adam_sharded(first_layer, n_layers, partial, tag):
        outs = {}
        for n in BIG:
            shp = (DEPTH,) + BIG_SHARD[n]
            outs[n] = _adam_sharded(rcs[n][first_layer:first_layer + n_layers], weights[n].reshape(shp),
                                    mom1[n].reshape(shp), mom2[n].reshape(shp), ADAM_ROWS[n],
                                    "adam_%s_%s" % (n, tag), first_layer, None if partial is None else partial[n])
        return outs

    stacked = {n: jnp.stack([small_grads[l][n] for l in range(DEPTH)]) for n, _ in SMALL if n != "norm_final"}
    stacked["norm_final"] = dg_final[0]
    packed = _pack([stacked[n] for n, _ in SMALL], NDEV * SMALL_ROWS).reshape(NCHIP, 2, SMALL_ROWS, D)
    (pair_small,), _ = _rs_exchange([packed], [], name="rs_pair_small")
    q_small = _pair_sum([packed], [pair_small], c_idx, name="rs_pairsum_small", nst=1)
    dxc, upper, q_small = lax.optimization_barrier((dxc, {n: rcs[n][1:] for n in BIG}, q_small))
    _, (chips_small,) = _rs_exchange([], q_small, name="rs_chip_small", seq_id=6)
    for n in BIG:
        rcs[n][1:] = upper[n]
    partial = adam_sharded(1, DEPTH - 1, None, "upper")
    partial, chips_small = lax.optimization_barrier((partial, chips_small))
    reduced_slot = _sum4(chips_small)
    reduced = _all_gather([reduced_slot], name="ag_small", seq_id=7)[0]
    dxc, partial, reduced_slot = pipeline.finish((dxc, partial, reduced_slot))
    small_full = dict(zip([n for n, _ in SMALL], _unpack(reduced, [s for _, s in SMALL])))
    grads, deltas, new_m, new_v = {}, {}, {}, {}
    direct, row = [], 0
    for n, shape in SMALL:
        if len(shape) == 1 or shape[-2] == DEPTH:
            direct.append((n, row, (1,) * (2 - len(shape)) + tuple(shape)))
        size = 1
        for s in shape:
            size *= s
        row += size // D
    (d_p, m_p, v_p), own_shape = _adam_packed(reduced.reshape(NDEV * SMALL_ROWS, D), *small_state, direct)
    small_shapes = [s for _, s in SMALL]

    def own_channels(n, full):
        return lax.dynamic_slice_in_dim(full, me * (W // NDEV), W // NDEV, axis=2) if n in CHANNEL_SHARDED else full

    for n, d_, m_, v_ in zip(small_names, _unpack(d_p, small_shapes), _unpack(m_p, small_shapes),
                             _unpack(v_p, small_shapes)):
        if n in own_shape:
            grads[n], deltas[n], new_m[n], new_v[n] = [a.reshape(weights[n].shape) for a in own_shape[n]]
        else:
            grads[n], deltas[n], new_m[n], new_v[n] = (own_channels(n, small_full[n]), own_channels(n, d_),
                                                       own_channels(n, m_), own_channels(n, v_))

    for n, (g_, d_, m_, v_) in adam_sharded(0, 1, partial, "last").items():
        full = weights[n].shape
        grads[n], deltas[n], new_m[n], new_v[n] = g_.reshape(full), d_.reshape(full), m_.reshape(full), v_.reshape(full)

    order = ("norm_mix", "w_in", "conf_dw", "conf_dw_b", "conf_ln_g", "conf_ln_b", "pool_w", "pool_scale", "sc_conv",
             "gmlp_ln_g", "gmlp_ln_b", "gmlp_ws", "gmlp_bs", "w_branch", "w_out", "norm_mlp", "w_up", "w_down",
             "norm_ple", "w_ple", "w_pleg", "norm_final")
    return (loss, dxc.reshape(1, T, D), *[grads[n] for n in order], *[deltas[n] for n in order],
            *[new_m[n] for n in order], *[new_v[n] for n in order])
```

```python
import functools

import jax
import jax.numpy as jnp
from jax import lax
from jax.experimental import pallas as pl
from jax.experimental.pallas import tpu as pltpu
from jax.experimental.pallas import tpu_sc as plsc

F32 = jnp.float32
BF16 = jnp.bfloat16

DEPTH = 4
T = 2048
D = 1024
W = 512
NDEV = 8
NCHIP = 4
EPS = 1e-6
CONF_K = 31
SC_K = 3
POOL_WINDOWS = (2, 4, 8, 16)
GW = 128
HB = 32
HA = 32
COLS_IN = 8192
MIX_COLS = 4096

ADAM_LR = 0.001
ADAM_B1 = 0.9
ADAM_B2 = 0.999
ADAM_EPS = 1e-08
ADAM_WD = 0.01
ADAM_STEP = 10

VMEM_LIMIT_BYTES = 56 * 1024 * 1024
MESH = pl.DeviceIdType.MESH


def _cp(*sem):
    return pltpu.CompilerParams(dimension_semantics=tuple(sem), vmem_limit_bytes=VMEM_LIMIT_BYTES)


def _sig(x):
    return jax.nn.sigmoid(x)


def _rms(x, g):
    r = lax.rsqrt(jnp.mean(x * x, axis=-1, keepdims=True) + EPS)
    return x * r * g


def _rms_bwd(dh, x, g, dres):
    r = lax.rsqrt(jnp.mean(x * x, axis=-1, keepdims=True) + EPS)
    xh = x * r
    u = dh * g
    dx = r * (u - xh * jnp.mean(u * xh, axis=-1, keepdims=True)) + dres
    dg = jnp.sum(dh * xh, axis=0, keepdims=True)
    return dx, dg


def _ln_stats(x):
    mu = jnp.mean(x, axis=-1, keepdims=True)
    xc = x - mu
    rstd = lax.rsqrt(jnp.mean(xc * xc, axis=-1, keepdims=True) + EPS)
    return xc * rstd, rstd


def _ln_bwd(dxh, xh, rstd):
    return rstd * (dxh - jnp.mean(dxh, axis=-1, keepdims=True) - xh * jnp.mean(dxh * xh, axis=-1, keepdims=True))


def _rowsum(x):
    return jnp.sum(x, axis=0, keepdims=True)


EPI_ROWS = 256


def _relu2_bf16(up):
    r = jnp.maximum(up.astype(F32), 0.0)
    return (r * r).astype(BF16)


def _mm(a, b3, *, mode, name, outs, trans_b=False, tm=512, tiles=(), params=(), epi=None, reds=(), a_pre=None):
    t_, ka = a.shape
    nj, r, c = b3.shape
    kb, nb = (c, r) if trans_b else (r, c)
    nt = t_ // tm
    out_mode = mode == "out"
    full = mode == "full"
    assert trans_b or not full
    if out_mode:
        assert ka == kb and not reds
        grid = (nj, nt)
        a_map = lambda g0, g1: (g1, 0)
        b_map = lambda g0, g1: (g0, 0, 0)
        t_map = lambda g0, g1: (g1, g0)
        width = nj * nb
    else:
        assert ka == nj * kb
        grid = (nt, 1 if full else nj)
        a_map = lambda g0, g1: (g0, g1)
        b_map = lambda g0, g1: (g1, 0, 0)
        t_map = lambda g0, g1: (g0, 0)
        width = nb
    n_t, n_p, n_o, n_r = len(tiles), len(params), len(outs), len(reds)
    use_acc = (not out_mode) and nj > 1 and not full
    dims = (((1,), (1,)), ((), ())) if trans_b else (((1,), (0,)), ((), ()))

    def body(a_ref, b_ref, *rest):
        t_refs = rest[:n_t]
        p_refs = rest[n_t:n_t + n_p]
        o_refs = rest[n_t + n_p:n_t + n_p + n_o]
        r_refs = rest[n_t + n_p + n_o:n_t + n_p + n_o + n_r]
        i = pl.program_id(1 if out_mode else 0)
        a_val = a_ref[...] if a_pre is None else a_pre(a_ref[...])
        if full:
            b_all, b_sems = rest[-2], rest[-1]

            @pl.when(i == 0)
            def _():
                cps = [pltpu.make_async_copy(b_ref.at[j], b_all.at[:, j * c:(j + 1) * c], b_sems.at[j])
                       for j in range(nj)]
                for cp in cps:
                    cp.start()
                for cp in cps:
                    cp.wait()

            part = lax.dot_general(a_val, b_all[...], dims, preferred_element_type=F32)
        else:
            part = lax.dot_general(a_val, b_ref[...], dims, preferred_element_type=F32)

        def finish(acc_rows):
            totals = [None] * n_r
            for r0 in range(0, tm, min(tm, EPI_ROWS)):
                rows = slice(r0, r0 + min(tm, EPI_ROWS))
                if epi is None:
                    res, rr = (acc_rows(rows),), ()
                else:
                    res, rr = epi(acc_rows(rows), [t[rows, :] for t in t_refs], [p[...] for p in p_refs])
                for o_ref, val in zip(o_refs, res):
                    o_ref[rows, :] = val.astype(o_ref.dtype)
                totals = [val if tot is None else tot + val for tot, val in zip(totals, rr)]
            for r_ref, val in zip(r_refs, totals):
                @pl.when(i == 0)
                def _():
                    r_ref[...] = val

                @pl.when(i > 0)
                def _():
                    r_ref[...] += val

        if use_acc:
            acc_ref = rest[-1]
            j = pl.program_id(1)

            @pl.when(j == 0)
            def _():
                acc_ref[...] = part

            @pl.when(jnp.logical_and(j > 0, j < nj - 1))
            def _():
                acc_ref[...] += part

            @pl.when(j == nj - 1)
            def _():
                finish(lambda rows: acc_ref[rows, :] + part[rows])
        else:
            finish(lambda rows: part[rows, :])

    const2 = lambda g0, g1: (0, 0)
    if full:
        in_specs = [pl.BlockSpec((tm, ka), a_map), pl.BlockSpec(memory_space=pl.ANY)]
        scratch = [pltpu.VMEM((r, nj * c), b3.dtype), pltpu.SemaphoreType.DMA((nj,))]
    else:
        in_specs = [pl.BlockSpec((tm, kb), a_map), pl.BlockSpec((None, r, c), b_map)]
        scratch = [pltpu.VMEM((tm, nb), F32)] if use_acc else []
    in_specs += [pl.BlockSpec((tm, t.shape[1] // nj if out_mode else t.shape[1]), t_map) for t in tiles]
    in_specs += [pl.BlockSpec(p.shape, lambda g0, g1, nd=p.ndim: (0,) * nd) for p in params]
    out_specs = [pl.BlockSpec((tm, nb), t_map) for _ in outs] + [pl.BlockSpec((1, w), const2) for w in reds]
    out_shape = [jax.ShapeDtypeStruct((t_, width), dt) for dt in outs]
    out_shape += [jax.ShapeDtypeStruct((1, w), F32) for w in reds]
    res = pl.pallas_call(
        body, name=name, grid=grid, in_specs=in_specs, out_specs=out_specs, out_shape=out_shape,
        scratch_shapes=scratch, compiler_params=_cp("arbitrary", "arbitrary"),
    )(a, b3, *tiles, *params)
    return res


def _mm_tn(a, g, *, nj, split, name, out_dtype=BF16, a_pre=None):
    t_ = a.shape[0]
    if split == "col":
        r, c = a.shape[1], g.shape[1] // nj
        a_spec = pl.BlockSpec((t_, r), lambda j: (0, 0))
        g_spec = pl.BlockSpec((t_, c), lambda j: (0, j))
    else:
        r, c = a.shape[1] // nj, g.shape[1]
        a_spec = pl.BlockSpec((t_, r), lambda j: (0, j))
        g_spec = pl.BlockSpec((t_, c), lambda j: (0, 0))

    def body(a_ref, g_ref, o_ref):
        a_val = a_ref[...] if a_pre is None else a_pre(a_ref[...])
        o_ref[...] = lax.dot_general(a_val, g_ref[...], (((0,), (0,)), ((), ())),
                                     preferred_element_type=F32).astype(o_ref.dtype)

    return pl.pallas_call(
        body, name=name, grid=(nj,), in_specs=[a_spec, g_spec],
        out_specs=pl.BlockSpec((None, r, c), lambda j: (j, 0, 0)),
        out_shape=jax.ShapeDtypeStruct((nj, r, c), out_dtype),
        compiler_params=_cp("arbitrary"),
    )(a, g)


def _epi_res_norm(acc, tiles, params):
    x_new = tiles[0] + acc
    return (x_new, _rms(x_new, params[0])), ()


def _epi_ple(acc, tiles, params):
    x_old, p_tile = tiles
    g_next, w_ple8 = params
    pe = jnp.concatenate([jnp.dot(p_tile, w_ple8[j], preferred_element_type=F32) for j in range(NDEV)], axis=1)
    x_new = x_old + pe * _sig(acc)
    return (x_new, acc, _rms(x_new, g_next), pe), ()


def _epi_rms_bwd(acc, tiles, params):
    dx, dg = _rms_bwd(acc, tiles[0], params[0], tiles[1])
    return (dx, dx), (dg,)


def _epi_dup(acc, tiles, params):
    return (acc * (2.0 * jnp.maximum(tiles[0].astype(F32), 0.0)),), ()


def _tri_mask():
    row = lax.broadcasted_iota(jnp.int32, (GW, GW), 0)
    col = lax.broadcasted_iota(jnp.int32, (GW, GW), 1)
    return row >= col


def _small_specs(sp_list):
    return [pl.BlockSpec(p.shape, (lambda i: (0, 0)) if p.ndim == 2 else (lambda i: (0, 0, 0))) for p in sp_list]


SUBLANES = 8


def _tap_sum(src, w_ref, taps, rows, stage):
    groups = {}
    for off, k in taps:
        groups.setdefault(off % SUBLANES, []).append((off - off % SUBLANES, k))
    out = None
    for res, members in sorted(groups.items()):
        n = rows if res == 0 else rows + SUBLANES
        part = None
        for base, k in members:
            term = w_ref[k:k + 1, :] * src[pl.ds(base, n), :]
            part = term if part is None else part + term
        if res:
            stage[0:n, :] = part
            part = stage[pl.ds(res, rows), :]
        out = part if out is None else out + part
    return out


def _tap_grads(grad, src, offsets, rows, stage, out_ref):
    pad = SUBLANES
    stage[0:pad, :] = jnp.zeros((pad, grad.shape[1]), F32)
    stage[pad:pad + rows, :] = grad
    stage[pad + rows:2 * pad + rows, :] = jnp.zeros((pad, grad.shape[1]), F32)
    groups = {}
    for k, off in enumerate(offsets):
        groups.setdefault(off % SUBLANES, []).append((off - off % SUBLANES, k))
    for res, members in sorted(groups.items()):
        shifted = stage[pl.ds(pad - res, rows + pad), :]
        for base, k in members:
            out_ref[k:k + 1, :] += _rowsum(shifted * src[pl.ds(base, rows + pad), :])


def _mixer_params(sp):
    return [sp["cw"], sp["cb"], sp["lg"], sp["lb"], sp["pw"], sp["ps"], sp["sc"], sp["gg"], sp["gb"], sp["ws"], sp["bst"]]


def _mixer_fwd(proj, sp, tm=256):
    nt = T // tm
    per = tm // HB

    conv_taps = [(HB - (CONF_K - 1) + k, k) for k in range(CONF_K)]

    def body(main_ref, halo_ref, cw, cb, lg, lb, pw, ps, sc, gg, gb, ws, bst, y_ref, ca_ref, ext, stage):
        i = pl.program_id(0)
        keep = (i > 0).astype(F32)

        def mcol(c0):
            return main_ref[:, c0:c0 + W].astype(F32)

        def hcol(c0):
            return halo_ref[:, c0:c0 + W].astype(F32)

        ext[0:HB, :] = hcol(0) * _sig(hcol(W)) * keep
        ext[HB:HB + tm, :] = mcol(0) * _sig(mcol(W))
        ca = (_tap_sum(ext, cw, conv_taps, tm, stage) + cb[...]).astype(BF16)
        ca_ref[...] = ca
        xh, _ = _ln_stats(ca.astype(F32))
        n = xh * lg[...] + lb[...]
        y_ref[:, 0:W] = (n * _sig(n)).astype(BF16)

        pin = mcol(1024)
        ext[0:HB, :] = hcol(1024) * keep
        ext[HB:HB + tm, :] = pin
        pos = (i * tm + lax.broadcasted_iota(jnp.int32, (tm, 1), 0) + 1).astype(F32)
        for g, w in enumerate(POOL_WINDOWS):
            lo = g * GW
            s = ext[pl.ds(HB, tm), lo:lo + GW]
            for j in range(1, w):
                s = s + ext[pl.ds(HB - j, tm), lo:lo + GW]
            pooled = s / jnp.minimum(pos, float(w)) - pin[:, lo:lo + GW]
            mixed = jnp.dot(pooled.astype(BF16), pw[g].astype(BF16), preferred_element_type=F32)
            y_ref[:, W + lo:W + lo + GW] = (mixed * ps[:, lo:lo + GW]).astype(BF16)

        ext[0:HB, :] = hcol(2048) * hcol(2560) * keep
        ext[HB:HB + tm, :] = mcol(2048) * mcol(2560)
        cv = sc[0:1, :] * ext[pl.ds(HB - 2, tm), :]
        cv = cv + sc[1:2, :] * ext[pl.ds(HB - 1, tm), :]
        cv = cv + sc[2:3, :] * ext[pl.ds(HB, tm), :]
        y_ref[:, 2 * W:3 * W] = (mcol(1536) * cv).astype(BF16)

        vh, _ = _ln_stats(mcol(3584))
        vn = (vh * gg[...] + gb[...]).astype(BF16)
        u = mcol(3072)
        tri = _tri_mask()
        for g in range(4):
            lo = g * GW
            wm = jnp.where(tri, ws[g], 0.0).astype(BF16)
            for c in range(tm // GW):
                r0 = c * GW
                sg = jnp.dot(wm, vn[r0:r0 + GW, lo:lo + GW], preferred_element_type=F32) + bst[:, g:g + 1]
                y_ref[r0:r0 + GW, 3 * W + lo:3 * W + lo + GW] = (u[r0:r0 + GW, lo:lo + GW] * sg).astype(BF16)

    plist = _mixer_params(sp)
    in_specs = [pl.BlockSpec((tm, MIX_COLS), lambda i: (i, 0)),
                pl.BlockSpec((HB, MIX_COLS), lambda i: (jnp.maximum(i * per - 1, 0), 0))]
    in_specs += _small_specs(plist)
    return pl.pallas_call(
        body, name="f_mixers", grid=(nt,), in_specs=in_specs,
        out_specs=[pl.BlockSpec((tm, 4 * W), lambda i: (i, 0)), pl.BlockSpec((tm, W), lambda i: (i, 0))],
        out_shape=[jax.ShapeDtypeStruct((T, 4 * W), BF16), jax.ShapeDtypeStruct((T, W), BF16)],
        scratch_shapes=[pltpu.VMEM((HB + tm, W), F32), pltpu.VMEM((tm + SUBLANES, W), F32)],
        compiler_params=_cp("arbitrary"),
    )(proj, proj, *plist)


def _assemble_wb(wb8_ref, wbf_ref):
    for k in range(4):
        for j in range(NDEV):
            wbf_ref[k, :, j * GW:(j + 1) * GW] = wb8_ref[j, k]


def _merge_fwd(y, proj, wb8, tm=256):
    nt = T // tm

    def body(y_ref, gate_ref, wb8_ref, z_ref, m_ref, wbf):
        @pl.when(pl.program_id(0) == 0)
        def _():
            _assemble_wb(wb8_ref, wbf)

        m = jnp.zeros((tm, D), F32)
        for k in range(4):
            zk = jnp.dot(y_ref[:, k * W:(k + 1) * W], wbf[k], preferred_element_type=F32)
            z_ref[:, k * D:(k + 1) * D] = zk.astype(BF16)
            m = m + _sig(gate_ref[:, k * D:(k + 1) * D].astype(F32)) * zk
        m_ref[...] = m.astype(BF16)

    return pl.pallas_call(
        body, name="f_merge", grid=(nt,),
        in_specs=[pl.BlockSpec((tm, 4 * W), lambda i: (i, 0)),
                  pl.BlockSpec((tm, 4 * D), lambda i: (i, 1)),
                  pl.BlockSpec(wb8.shape, lambda i: (0, 0, 0, 0))],
        out_specs=[pl.BlockSpec((tm, 4 * D), lambda i: (i, 0)), pl.BlockSpec((tm, D), lambda i: (i, 0))],
        out_shape=[jax.ShapeDtypeStruct((T, 4 * D), BF16), jax.ShapeDtypeStruct((T, D), BF16)],
        scratch_shapes=[pltpu.VMEM((4, W, D), BF16)],
        compiler_params=_cp("arbitrary"),
    )(y, proj, wb8)


def _merge_bwd(dm, z, proj, y, wb8, tm=256):
    nt = T // tm

    def body(dm_ref, z_ref, gate_ref, y_ref, wb8_ref, dp_ref, dy_ref, dwb_ref, wbf, acc):
        i = pl.program_id(0)

        @pl.when(i == 0)
        def _():
            _assemble_wb(wb8_ref, wbf)

        dmv = dm_ref[...].astype(F32)
        for k in range(4):
            s = _sig(gate_ref[:, k * D:(k + 1) * D].astype(F32))
            dzk = (dmv * s).astype(BF16)
            dp_ref[:, k * D:(k + 1) * D] = (dmv * z_ref[:, k * D:(k + 1) * D].astype(F32) * s * (1.0 - s)).astype(BF16)
            dyk = lax.dot_general(dzk, wbf[k], (((1,), (1,)), ((), ())), preferred_element_type=F32)
            dy_ref[:, k * W:(k + 1) * W] = dyk.astype(BF16)
            part = lax.dot_general(y_ref[:, k * W:(k + 1) * W], dzk, (((0,), (0,)), ((), ())),
                                   preferred_element_type=F32)

            @pl.when(i == 0)
            def _():
                acc[k] = part

            @pl.when(i > 0)
            def _():
                acc[k] += part

        @pl.when(i == nt - 1)
        def _():
            for k in range(4):
                for j in range(NDEV):
                    dwb_ref[j, k] = acc[k, :, j * GW:(j + 1) * GW].astype(BF16)

    return pl.pallas_call(
        body, name="b_merge", grid=(nt,),
        in_specs=[pl.BlockSpec((tm, D), lambda i: (i, 0)),
                  pl.BlockSpec((tm, 4 * D), lambda i: (i, 0)),
                  pl.BlockSpec((tm, 4 * D), lambda i: (i, 1)),
                  pl.BlockSpec((tm, 4 * W), lambda i: (i, 0)),
                  pl.BlockSpec(wb8.shape, lambda i: (0, 0, 0, 0))],
        out_specs=[pl.BlockSpec((tm, 4 * D), lambda i: (i, 1)),
                   pl.BlockSpec((tm, 4 * W), lambda i: (i, 0)),
                   pl.BlockSpec(wb8.shape, lambda i: (0, 0, 0, 0))],
        out_shape=[jax.ShapeDtypeStruct((T, COLS_IN), BF16),
                   jax.ShapeDtypeStruct((T, 4 * W), BF16),
                   jax.ShapeDtypeStruct(wb8.shape, BF16)],
        scratch_shapes=[pltpu.VMEM((4, W, D), BF16), pltpu.VMEM((4, W, D), F32)],
        compiler_params=_cp("arbitrary"),
    )(dm, z, proj, y, wb8)


def _mixer_bwd(proj, ca_saved, dy, dproj, sp, tm=256):
    nt = T // tm
    per = tm // HB
    ne = tm + HA
    last_blk = T // HA - 1
    conv_taps = [(HB - (CONF_K - 1) + k, k) for k in range(CONF_K)]

    def body(main_ref, hb_ref, ha_ref, ca_ref, cah_ref, dy_ref, dyh_ref, cw, cb, lg, lb, pw, ps, sc, gg, gb, ws, bst,
             dp_any, dp_ref, dcw_ref, dsc_ref, vec_ref, dpw_ref, dws_ref, dbs_ref, e1, e2, e3, stage):
        del dp_any, cb
        i = pl.program_id(0)
        keep_b = (i > 0).astype(F32)
        keep_a = (i < nt - 1).astype(F32)

        @pl.when(i == 0)
        def _():
            dcw_ref[...] = jnp.zeros_like(dcw_ref)
            dsc_ref[...] = jnp.zeros_like(dsc_ref)
            vec_ref[...] = jnp.zeros_like(vec_ref)
            dpw_ref[...] = jnp.zeros_like(dpw_ref)
            dws_ref[...] = jnp.zeros_like(dws_ref)
            dbs_ref[...] = jnp.zeros_like(dbs_ref)

        def mcol(c0):
            return main_ref[:, c0:c0 + W].astype(F32)

        def hbcol(c0):
            return hb_ref[:, c0:c0 + W].astype(F32)

        def hacol(c0):
            return ha_ref[:, c0:c0 + W].astype(F32)

        def load_dy(c0):
            e2[0:tm, :] = dy_ref[:, c0:c0 + W].astype(F32)
            e2[tm:ne, :] = dyh_ref[:, c0:c0 + W].astype(F32) * keep_a

        a = mcol(0)
        sa = _sig(mcol(W))
        e1[0:HB, :] = hbcol(0) * _sig(hbcol(W)) * keep_b
        e1[HB:HB + tm, :] = a * sa
        e1[HB + tm:HB + tm + SUBLANES, :] = jnp.zeros((SUBLANES, W), F32)
        e2[0:tm, :] = ca_ref[...].astype(F32)
        e2[tm:ne, :] = cah_ref[...].astype(F32)
        xh, rstd = _ln_stats(e2[0:ne, :])
        nn = xh * lg[...] + lb[...]
        s = _sig(nn)
        load_dy(0)
        dn = e2[0:ne, :] * (s * (1.0 + nn * (1.0 - s)))
        vec_ref[1:2, :] += _rowsum(dn[0:tm] * xh[0:tm])
        vec_ref[2:3, :] += _rowsum(dn[0:tm])
        dca = _ln_bwd(dn * lg[...], xh, rstd)
        e3[0:ne, :] = dca
        dmain = dca[0:tm]
        vec_ref[0:1, :] += _rowsum(dmain)
        _tap_grads(dmain, e1, [off for off, _ in conv_taps], tm, stage, dcw_ref)
        dglu = _tap_sum(e3, cw, [(CONF_K - 1 - k, k) for k in range(CONF_K)], tm, stage)
        dp_ref[:, 0:W] = (dglu * sa).astype(BF16)
        dp_ref[:, W:2 * W] = (dglu * a * sa * (1.0 - sa)).astype(BF16)

        pin = mcol(1024)
        e1[0:HB, :] = hbcol(1024) * keep_b
        e1[HB:HB + tm, :] = pin
        load_dy(W)
        dyb = e2[0:ne, :]
        pos_m = (i * tm + lax.broadcasted_iota(jnp.int32, (tm, 1), 0) + 1).astype(F32)
        pos_e = (i * tm + lax.broadcasted_iota(jnp.int32, (ne, 1), 0) + 1).astype(F32)
        for g, w in enumerate(POOL_WINDOWS):
            lo = g * GW
            acc = e1[pl.ds(HB, tm), lo:lo + GW]
            for j in range(1, w):
                acc = acc + e1[pl.ds(HB - j, tm), lo:lo + GW]
            pooled = (acc / jnp.minimum(pos_m, float(w)) - pin[:, lo:lo + GW]).astype(BF16)
            pwb = pw[g].astype(BF16)
            mixed = jnp.dot(pooled, pwb, preferred_element_type=F32)
            dyb_g = dyb[:, lo:lo + GW]
            vec_ref[3:4, lo:lo + GW] += _rowsum(dyb_g[0:tm] * mixed)
            dmb = (dyb_g * ps[:, lo:lo + GW]).astype(BF16)
            dpw_ref[g] += lax.dot_general(pooled, dmb[0:tm], (((0,), (0,)), ((), ())), preferred_element_type=F32)
            dpool = lax.dot_general(dmb, pwb, (((1,), (1,)), ((), ())), preferred_element_type=F32)
            e3[0:ne, lo:lo + GW] = dpool / jnp.minimum(pos_e, float(w))
            back = e3[pl.ds(0, tm), lo:lo + GW]
            for j in range(1, w):
                back = back + e3[pl.ds(j, tm), lo:lo + GW]
            dp_ref[:, 1024 + lo:1024 + lo + GW] = (back - dpool[0:tm]).astype(BF16)

        cg = mcol(2048)
        hx = mcol(2560)
        e1[0:HB, :] = hbcol(2048) * hbcol(2560) * keep_b
        e1[HB:HB + tm, :] = cg * hx
        load_dy(2 * W)
        dyc = e2[0:tm, :]
        dconv = dyc * mcol(1536)
        e3[0:tm, :] = dconv
        e3[tm:ne, :] = e2[tm:ne, :] * hacol(1536)
        cv = sc[0:1, :] * e1[pl.ds(HB - 2, tm), :]
        for k in range(1, SC_K):
            cv = cv + sc[k:k + 1, :] * e1[pl.ds(HB - 2 + k, tm), :]
        dp_ref[:, 1536:2048] = (dyc * cv).astype(BF16)
        for k in range(SC_K):
            dsc_ref[k:k + 1, :] += _rowsum(dconv * e1[pl.ds(HB - 2 + k, tm), :])
        dq = sc[0:1, :] * e3[pl.ds(2, tm), :]
        for k in range(1, SC_K):
            dq = dq + sc[k:k + 1, :] * e3[pl.ds(2 - k, tm), :]
        dp_ref[:, 2048:2560] = (dq * hx).astype(BF16)
        dp_ref[:, 2560:3072] = (dq * cg).astype(BF16)

        u = mcol(3072)
        vh, vr = _ln_stats(mcol(3584))
        vn = (vh * gg[...] + gb[...]).astype(BF16)
        dyd = dy_ref[:, 3 * W:4 * W].astype(F32)
        tri = _tri_mask()
        for g in range(4):
            lo = g * GW
            wm = jnp.where(tri, ws[g], 0.0).astype(BF16)
            dws_g = jnp.zeros((GW, GW), F32)
            dbs_g = jnp.zeros((GW, 1), F32)
            for c in range(tm // GW):
                r0 = c * GW
                blk = vn[r0:r0 + GW, lo:lo + GW]
                sg = jnp.dot(wm, blk, preferred_element_type=F32) + bst[:, g:g + 1]
                dyd_b = dyd[r0:r0 + GW, lo:lo + GW]
                dp_ref[r0:r0 + GW, 3072 + lo:3072 + lo + GW] = (dyd_b * sg).astype(BF16)
                dsg = dyd_b * u[r0:r0 + GW, lo:lo + GW]
                dsgb = dsg.astype(BF16)
                dbs_g = dbs_g + jnp.sum(dsg, axis=-1, keepdims=True)
                dws_g = dws_g + lax.dot_general(dsgb, blk, (((1,), (1,)), ((), ())), preferred_element_type=F32)
                e1[r0:r0 + GW, lo:lo + GW] = lax.dot_general(wm, dsgb, (((0,), (0,)), ((), ())),
                                                             preferred_element_type=F32)
            dws_ref[g] += jnp.where(tri, dws_g, 0.0)
            dbs_ref[g] += jnp.broadcast_to(dbs_g, (GW, GW))
        dvn = e1[0:tm, :]
        vec_ref[4:5, :] += _rowsum(dvn * vh)
        vec_ref[5:6, :] += _rowsum(dvn)
        dp_ref[:, 3584:4096] = _ln_bwd(dvn * gg[...], vh, vr).astype(BF16)

    plist = _mixer_params(sp)
    in_specs = [pl.BlockSpec((tm, MIX_COLS), lambda i: (i, 0)),
                pl.BlockSpec((HB, MIX_COLS), lambda i: (jnp.maximum(i * per - 1, 0), 0)),
                pl.BlockSpec((HA, MIX_COLS), lambda i: (jnp.minimum((i + 1) * per, last_blk), 0)),
                pl.BlockSpec((tm, W), lambda i: (i, 0)),
                pl.BlockSpec((HA, W), lambda i: (jnp.minimum((i + 1) * per, last_blk), 0)),
                pl.BlockSpec((tm, 4 * W), lambda i: (i, 0)),
                pl.BlockSpec((HA, 4 * W), lambda i: (jnp.minimum((i + 1) * per, last_blk), 0))]
    in_specs += _small_specs(plist)
    in_specs += [pl.BlockSpec(memory_space=pl.ANY)]
    z2 = lambda i: (0, 0)
    z3 = lambda i: (0, 0, 0)
    out_specs = [pl.BlockSpec((tm, MIX_COLS), lambda i: (i, 0)),
                 pl.BlockSpec((32, W), z2), pl.BlockSpec((8, W), z2), pl.BlockSpec((8, W), z2),
                 pl.BlockSpec((4, GW, GW), z3), pl.BlockSpec((4, GW, GW), z3), pl.BlockSpec((4, GW, GW), z3)]
    out_shape = [jax.ShapeDtypeStruct((T, COLS_IN), BF16),
                 jax.ShapeDtypeStruct((32, W), F32), jax.ShapeDtypeStruct((8, W), F32),
                 jax.ShapeDtypeStruct((8, W), F32),
                 jax.ShapeDtypeStruct((4, GW, GW), F32), jax.ShapeDtypeStruct((4, GW, GW), F32),
                 jax.ShapeDtypeStruct((4, GW, GW), F32)]
    n_in = 7 + len(plist)
    return pl.pallas_call(
        body, name="b_mixers", grid=(nt,), in_specs=in_specs, out_specs=out_specs, out_shape=out_shape,
        scratch_shapes=[pltpu.VMEM((HB + ne, W), F32), pltpu.VMEM((ne, W), F32), pltpu.VMEM((ne, W), F32),
                        pltpu.VMEM((ne + SUBLANES, W), F32)],
        input_output_aliases={n_in: 0},
        compiler_params=_cp("arbitrary"),
    )(proj, proj, proj, ca_saved, ca_saved, dy, dy, *plist, dproj)


def _norm_first(x, g, tm=512):
    def body(x_ref, g_ref, o_ref):
        o_ref[...] = _rms(x_ref[...], g_ref[...]).astype(BF16)

    return pl.pallas_call(
        body, name="f_norm0", grid=(T // tm,),
        in_specs=[pl.BlockSpec((tm, D), lambda i: (i, 0)), pl.BlockSpec((1, D), lambda i: (0, 0))],
        out_specs=pl.BlockSpec((tm, D), lambda i: (i, 0)),
        out_shape=jax.ShapeDtypeStruct((T, D), BF16), compiler_params=_cp("arbitrary"),
    )(x, g)


def _loss_head(x, target, g, tm=256):
    def body(x_ref, t_ref, g_ref, dx_ref, dg_ref, loss_ref):
        i = pl.program_id(0)
        x = x_ref[...]
        r = lax.rsqrt(jnp.mean(x * x, axis=-1, keepdims=True) + EPS)
        xh = x * r
        gv = g_ref[...]
        e = xh * gv - t_ref[...]
        dyv = e * (1.0 / D)
        part = jnp.sum(_rowsum(e * e), axis=-1, keepdims=True) * (0.5 / D)
        u = dyv * gv
        dx_ref[...] = r * (u - xh * jnp.mean(u * xh, axis=-1, keepdims=True))
        dgp = _rowsum(dyv * xh)

        @pl.when(i == 0)
        def _():
            dg_ref[...] = dgp
            loss_ref[...] = jnp.broadcast_to(part, (1, GW))

        @pl.when(i > 0)
        def _():
            dg_ref[...] += dgp
            loss_ref[...] += jnp.broadcast_to(part, (1, GW))

    return pl.pallas_call(
        body, name="loss_head", grid=(T // tm,),
        in_specs=[pl.BlockSpec((tm, D), lambda i: (i, 0)), pl.BlockSpec((tm, D), lambda i: (i, 0)),
                  pl.BlockSpec((1, D), lambda i: (0, 0))],
        out_specs=[pl.BlockSpec((tm, D), lambda i: (i, 0)), pl.BlockSpec((1, D), lambda i: (0, 0)),
                   pl.BlockSpec((1, GW), lambda i: (0, 0))],
        out_shape=[jax.ShapeDtypeStruct((T, D), F32), jax.ShapeDtypeStruct((1, D), F32),
                   jax.ShapeDtypeStruct((1, GW), F32)],
        compiler_params=_cp("arbitrary"),
    )(x, target, g)


def _ple_bwd(dx4, sv, w_pleg8, g_ple, tm=256):
    nt = T // tm
    ple_dim = sv["p"].shape[1]

    def body(dx_ref, gl_ref, pe_ref, x_ref, h_ref, p_ref, g_ref, wg_ref,
             dx3_ref, dx3b_ref, dg_ref, dwg_ref, dwp_ref, acc_g, acc_p):
        i = pl.program_id(0)
        d = dx_ref[...]
        s = _sig(gl_ref[...].astype(F32))
        dpe = (d * s).astype(BF16)
        dgl = (d * pe_ref[...].astype(F32) * s * (1.0 - s)).astype(BF16)
        dh = lax.dot_general(dgl, wg_ref[...], (((1,), (1,)), ((), ())), preferred_element_type=F32)
        dx, dgp = _rms_bwd(dh, x_ref[...], g_ref[...], d)
        dx3_ref[...] = dx
        dx3b_ref[...] = dx.astype(BF16)
        part_g = lax.dot_general(h_ref[...], dgl, (((0,), (0,)), ((), ())), preferred_element_type=F32)
        part_p = lax.dot_general(p_ref[...], dpe, (((0,), (0,)), ((), ())), preferred_element_type=F32)

        @pl.when(i == 0)
        def _():
            dg_ref[...] = dgp
            acc_g[...] = part_g
            acc_p[...] = part_p

        @pl.when(i > 0)
        def _():
            dg_ref[...] += dgp
            acc_g[...] += part_g
            acc_p[...] += part_p

        @pl.when(i == nt - 1)
        def _():
            for j in range(NDEV):
                dwg_ref[j] = acc_g[j * GW:(j + 1) * GW, :].astype(BF16)
                dwp_ref[j] = acc_p[:, j * GW:(j + 1) * GW].astype(BF16)

    tile = lambda w: pl.BlockSpec((tm, w), lambda i: (i, 0))
    const = lambda shp: pl.BlockSpec(shp, lambda i: (0,) * len(shp))
    return pl.pallas_call(
        body, name="b_ple", grid=(nt,),
        in_specs=[tile(D), tile(D), tile(D), tile(D), tile(D), tile(ple_dim), const((1, D)), const((D, D))],
        out_specs=[tile(D), tile(D), const((1, D)), const((NDEV, GW, D)), const((NDEV, ple_dim, GW))],
        out_shape=[jax.ShapeDtypeStruct((T, D), F32), jax.ShapeDtypeStruct((T, D), BF16),
                   jax.ShapeDtypeStruct((1, D), F32), jax.ShapeDtypeStruct((NDEV, GW, D), BF16),
                   jax.ShapeDtypeStruct((NDEV, ple_dim, GW), BF16)],
        scratch_shapes=[pltpu.VMEM((D, D), F32), pltpu.VMEM((ple_dim, D), F32)],
        compiler_params=_cp("arbitrary"),
    )(dx4, sv["gl"], sv["pe"], sv["x3"], sv["h3"], sv["p"], g_ple, w_pleg8.reshape(D, D))


def _layer_fwd(x, h1, p_bf, gw, sp, g_next):
    proj, = _mm(h1, gw["w_in"], mode="out", name="f_proj", outs=[BF16], tm=T)
    y, ca = _mixer_fwd(proj, sp)
    z, merged = _merge_fwd(y, proj, gw["w_branch"])
    x2, h2 = _mm(merged, gw["w_out"].reshape(1, D, D), mode="acc", name="f_out", outs=[F32, BF16], tm=T // 2,
                 tiles=[x], params=[sp["g_mlp"]], epi=_epi_res_norm)
    up, = _mm(h2, gw["w_up"], mode="out", name="f_up", outs=[BF16], tm=T)
    x3, h3 = _mm(up, gw["w_down"].reshape(1, 4 * D, D), mode="acc", name="f_down", outs=[F32, BF16], tm=T // 4,
                 tiles=[x2], params=[sp["g_ple"]], epi=_epi_res_norm, a_pre=_relu2_bf16)
    x4, gl, hn, pe = _mm(h3, gw["w_pleg"].reshape(1, D, D), mode="acc", name="f_gate", tm=T // 2,
                         outs=[F32, BF16, BF16, BF16], tiles=[x3, p_bf], params=[g_next, gw["w_ple"]], epi=_epi_ple)
    saved = dict(x=x, h1=h1, proj=proj, y=y, ca=ca, z=z, merged=merged, x2=x2, h2=h2, up=up, x3=x3, h3=h3,
                 pe=pe, gl=gl, p=p_bf)
    return x4, hn, saved


def _layer_bwd(dx4, sv, gw, sp, submit, early_group=False):
    dw = {}
    dx3, dx3b, dg_ple, dw["w_pleg"], dw["w_ple"] = _ple_bwd(dx4, sv, gw["w_pleg"], sp["g_ple"])
    dup, = _mm(dx3b, gw["w_down"], mode="out", trans_b=True, name="b_dact", outs=[BF16], tm=T,
               tiles=[sv["up"]], epi=_epi_dup)
    dw["w_down"] = _mm_tn(sv["up"], dx3b, nj=NDEV, split="row", name="b_dw_down", a_pre=_relu2_bf16)
    dw["w_up"] = _mm_tn(sv["h2"], dup, nj=NDEV, split="col", name="b_dw_up")
    if early_group:
        dw["w_up"], dup = lax.optimization_barrier((dw["w_up"], dup))
        dup = submit(dw, ("w_up", "w_down", "w_ple", "w_pleg"), dup)
    dx2, dx2b, dg_mlp = _mm(dup, gw["w_up"], mode="full", trans_b=True, name="b_dh2", tm=T // 4,
                            outs=[F32, BF16], tiles=[sv["x2"], dx3], params=[sp["g_mlp"]], epi=_epi_rms_bwd, reds=[D])
    dm, = _mm(dx2b, gw["w_out"].reshape(1, D, D), mode="acc", trans_b=True, name="b_dmerged", outs=[BF16],
              tm=T // 2)
    dw["w_out"] = _mm_tn(sv["merged"], dx2b, nj=NDEV, split="row", name="b_dw_out")
    dproj, dy, dw["w_branch"] = _merge_bwd(dm, sv["z"], sv["proj"], sv["y"], gw["w_branch"])
    dy = submit(dw, ("w_branch", "w_out") if early_group else BIG[1:], dy)
    dproj, dcw, dsc, vec, dpw, dws, dbs = _mixer_bwd(sv["proj"], sv["ca"], dy, dproj, sp)
    dw["w_in"] = _mm_tn(sv["h1"], dproj, nj=NDEV, split="col", name="b_dw_in")
    dw["w_in"], dproj = lax.optimization_barrier((dw["w_in"], dproj))
    dproj = submit(dw, BIG[:1], dproj)
    dx, dg_mix = _mm(dproj, gw["w_in"], mode="full", trans_b=True, name="b_dh1", outs=[F32], tm=T // 4,
                     tiles=[sv["x"], dx2], params=[sp["g_mix"]], epi=_epi_rms_bwd, reds=[D])
    small = dict(norm_mix=dg_mix[0], conf_dw=dcw[:CONF_K], conf_dw_b=vec[0], conf_ln_g=vec[1], conf_ln_b=vec[2],
                 pool_w=dpw, pool_scale=vec[3], sc_conv=dsc[:SC_K], gmlp_ln_g=vec[4], gmlp_ln_b=vec[5],
                 gmlp_ws=dws, gmlp_bs=dbs[:, :, 0], norm_mlp=dg_mlp[0], norm_ple=dg_ple[0])
    return dx, small


ANY = pl.BlockSpec(memory_space=pl.ANY)


def _mesh_pos():
    return lax.axis_index("x"), lax.axis_index("y"), lax.axis_index("c")


def _other_chips(x, y):
    return [(1 - x, y), (x, 1 - y), (1 - x, 1 - y)]


def _launch_comm(body, peers_of, operands, out_shapes, sems, name, seq_id):
    n_in, n_out = len(operands), len(out_shapes)
    if seq_id is None:
        return pl.pallas_call(body, name=name, in_specs=[ANY] * n_in, out_specs=[ANY] * n_out,
                              out_shape=out_shapes, scratch_shapes=sems)(*operands)

    def seq_body(*refs):
        peers = peers_of(*_mesh_pos())
        barrier = pltpu.get_barrier_semaphore()
        for peer in peers:
            pl.semaphore_signal(barrier, inc=1, device_id=peer, device_id_type=MESH)
        pl.semaphore_wait(barrier, len(peers))
        body(*refs)

    return pl.kernel(seq_body, name=name, out_type=out_shapes,
                     mesh=plsc.ScalarSubcoreMesh(axis_name="seq", num_cores=1), scratch_types=sems,
                     compiler_params=pltpu.CompilerParams(collective_id=seq_id))(*operands)


def _all_gather(shards, name, seq_id=None):
    n = len(shards)

    def body(*refs):
        s_refs, o_refs = refs[:n], refs[n:2 * n]
        send_sems, recv_sems, local_sems = refs[2 * n:]
        x, y, c = _mesh_pos()
        me = 4 * x + 2 * y + c
        here = (x, y, c)
        sibling = (x, y, 1 - c)
        chips = _other_chips(x, y)

        def slot(px, py, pc):
            return 4 * px + 2 * py + pc

        def copy(t, k, slot_idx, to, src=None):
            dst = o_refs[t].at[slot_idx]
            return pltpu.make_async_remote_copy(
                src_ref=dst if src is None else src, dst_ref=dst,
                send_sem=send_sems.at[t * 7 + k], recv_sem=recv_sems.at[t * 7 + k],
                device_id=to, device_id_type=MESH)

        mine = [pltpu.make_async_copy(s_refs[t], o_refs[t].at[me], local_sems.at[t]) for t in range(n)]
        for cp in mine:
            cp.start()
        first = []
        for t in range(n):
            for j, chip in enumerate(chips):
                first.append(copy(t, 1 + j, me, (*chip, c), src=s_refs[t]))
        for t in range(n):
            first.append(copy(t, 0, me, sibling, src=s_refs[t]))
        for cp in first:
            cp.start()
        passed = []
        for t in range(n):
            for j, chip in enumerate(chips):
                copy(t, 1 + j, slot(*chip, c), here).wait_recv()
                fwd = copy(t, 4 + j, slot(*chip, c), sibling)
                fwd.start()
                passed.append(fwd)
        for t in range(n):
            copy(t, 0, slot(x, y, 1 - c), here).wait_recv()
            for j, chip in enumerate(chips):
                copy(t, 4 + j, slot(*chip, 1 - c), here).wait_recv()
        for cp in first + passed:
            cp.wait_send()
        for cp in mine:
            cp.wait()

    def peers_of(x, y, c):
        return [(x, y, 1 - c)] + [(*chip, c) for chip in _other_chips(x, y)]

    return _launch_comm(
        body, peers_of, shards, [jax.ShapeDtypeStruct((NDEV,) + s.shape, s.dtype) for s in shards],
        [pltpu.SemaphoreType.DMA((7 * n,)), pltpu.SemaphoreType.DMA((7 * n,)), pltpu.SemaphoreType.DMA((n,))],
        name, seq_id)


def _rs_exchange(p4s, qs, name, seq_id=None):
    n_p, n_q = len(p4s), len(qs)

    def body(*refs):
        p_refs, q_refs = refs[:n_p], refs[n_p:n_p + n_q]
        rb_refs, rc_refs = refs[n_p + n_q:2 * n_p + n_q], refs[2 * n_p + n_q:2 * (n_p + n_q)]
        pair_send, pair_recv, chip_send, chip_recv, local_sems = refs[2 * (n_p + n_q):]
        x, y, c = _mesh_pos()
        a_idx = 2 * x + y
        chips = _other_chips(x, y)
        mine = [pltpu.make_async_copy(q_refs[t].at[a_idx], rc_refs[t].at[a_idx], local_sems.at[t])
                for t in range(n_q)]
        sends = []
        for t in range(n_q):
            for j, chip in enumerate(chips):
                sends.append(pltpu.make_async_remote_copy(
                    src_ref=q_refs[t].at[2 * chip[0] + chip[1]], dst_ref=rc_refs[t].at[a_idx],
                    send_sem=chip_send.at[t * 3 + j], recv_sem=chip_recv.at[t * 3 + j],
                    device_id=(*chip, c), device_id_type=MESH))
        pairs = [pltpu.make_async_remote_copy(
            src_ref=p_refs[t].at[:, 1 - c], dst_ref=rb_refs[t], send_sem=pair_send.at[t], recv_sem=pair_recv.at[t],
            device_id=(x, y, 1 - c), device_id_type=MESH) for t in range(n_p)]
        for cp in sends + mine + pairs:
            cp.start()
        for cp in pairs:
            cp.wait()
        for t in range(n_q):
            for j, chip in enumerate(chips):
                landed = rc_refs[t].at[2 * chip[0] + chip[1]]
                pltpu.make_async_remote_copy(
                    src_ref=landed, dst_ref=landed, send_sem=chip_send.at[t * 3 + j],
                    recv_sem=chip_recv.at[t * 3 + j], device_id=(x, y, c), device_id_type=MESH).wait_recv()
        for cp in sends:
            cp.wait_send()
        for cp in mine:
            cp.wait()

    def peers_of(x, y, c):
        peers = [(x, y, 1 - c)] if n_p else []
        return peers + ([(*chip, c) for chip in _other_chips(x, y)] if n_q else [])

    out_shapes = [jax.ShapeDtypeStruct((NCHIP,) + p.shape[2:], p.dtype) for p in p4s]
    out_shapes += [jax.ShapeDtypeStruct(q.shape, q.dtype) for q in qs]
    sems = [pltpu.SemaphoreType.DMA((max(n_p, 1),)), pltpu.SemaphoreType.DMA((max(n_p, 1),)),
            pltpu.SemaphoreType.DMA((max(3 * n_q, 1),)), pltpu.SemaphoreType.DMA((max(3 * n_q, 1),)),
            pltpu.SemaphoreType.DMA((max(n_q, 1),))]
    got = _launch_comm(body, peers_of, list(p4s) + list(qs), out_shapes, sems, name, seq_id)
    return got[:n_p], got[n_p:]


def _pair_sum(p4s, rbs, c_idx, name, nst=4):
    n = len(p4s)
    trs = [p.shape[2] // nst for p in p4s]

    def body(c_ref, *refs):
        del c_ref
        p_refs, r_refs, o_refs = refs[:n], refs[n:2 * n], refs[2 * n:]
        for p_ref, r_ref, o_ref in zip(p_refs, r_refs, o_refs):
            o_ref[...] = (p_ref[...].astype(F32) + r_ref[...].astype(F32)).astype(o_ref.dtype)

    in_specs = [pl.BlockSpec((None, None, tr, p.shape[3]), lambda b, i, c_ref: (b, c_ref[0], i, 0))
                for p, tr in zip(p4s, trs)]
    in_specs += [pl.BlockSpec((None, tr, p.shape[3]), lambda b, i, c_ref: (b, i, 0)) for p, tr in zip(p4s, trs)]
    out_specs = [pl.BlockSpec((None, tr, p.shape[3]), lambda b, i, c_ref: (b, i, 0)) for p, tr in zip(p4s, trs)]
    return pl.pallas_call(
        body, name=name,
        grid_spec=pltpu.PrefetchScalarGridSpec(num_scalar_prefetch=1, grid=(NCHIP, nst), in_specs=in_specs,
                                               out_specs=out_specs),
        out_shape=[jax.ShapeDtypeStruct((NCHIP,) + p.shape[2:], p.dtype) for p in p4s],
        compiler_params=_cp("arbitrary", "arbitrary"),
    )(c_idx, *p4s, *rbs)


class _GradientPipeline:
    def __init__(self, c_idx, results):
        self.c_idx, self.results, self.pending = c_idx, results, None

    def _sum_pending(self, chain):
        names, layer, p4s, rbs = self.pending
        qs = _pair_sum(p4s, rbs, self.c_idx, name="rs_pairsum_%d" % len(names))
        return lax.optimization_barrier((chain, qs))

    def submit(self, dw, names, layer, chain):
        qs, tag, seq_id = [], "pair", 3
        if self.pending is not None:
            chain, qs = self._sum_pending(chain)
            tag, seq_id = "pair_chip", 4
        p4s = [dw[n].reshape((NCHIP, 2) + BIG_SHARD[n]) for n in names]
        rbs, rcs = _rs_exchange(p4s, qs, name="rs_%s_%d" % (tag, len(names)), seq_id=seq_id)
        self._record(rcs)
        self.pending = (names, layer, p4s, rbs)
        return chain

    def finish(self, chain):
        chain, qs = self._sum_pending(chain)
        self._record(_rs_exchange([], qs, name="rs_chip_last", seq_id=5)[1])
        self.pending = None
        return chain

    def _record(self, rcs):
        if rcs:
            names, layer = self.pending[:2]
            for n, rc in zip(names, rcs):
                self.results[n][layer] = rc


def _adamw(w, g, m, v):
    m = ADAM_B1 * m + (1.0 - ADAM_B1) * g
    v = ADAM_B2 * v + (1.0 - ADAM_B2) * (g * g)
    m_hat = m / (1.0 - ADAM_B1 ** ADAM_STEP)
    v_hat = v / (1.0 - ADAM_B2 ** ADAM_STEP)
    delta = -ADAM_LR * (m_hat / (jnp.sqrt(v_hat) + ADAM_EPS) + ADAM_WD * w)
    return delta, m, v


def _adam_sharded(rcs, w, m, v, tr, name, first_layer, partial=None):
    _, r, c = w.shape
    nst = r // tr
    n_l = len(rcs)

    def body(*refs):
        rc_refs = refs[:n_l]
        w_ref, m_ref, v_ref = refs[n_l:n_l + 3]
        g_out, d_out, m_out, v_out = refs[-4:]
        layer = pl.program_id(0)
        for k, rc in enumerate(rc_refs):
            @pl.when(layer == k)
            def _():
                g = rc[0].astype(F32) + rc[1].astype(F32) + rc[2].astype(F32) + rc[3].astype(F32)
                delta, m_new, v_new = _adamw(w_ref[...], g, m_ref[...], v_ref[...])
                g_out[...] = g
                d_out[...] = delta
                m_out[...] = m_new
                v_out[...] = v_new

    rc_specs = [pl.BlockSpec((NCHIP, tr, c), lambda l, i, k=k: (0, jnp.where(l == k, i, 0), 0)) for k in range(n_l)]
    wspec = pl.BlockSpec((None, tr, c), lambda l, i: (first_layer + l, i, 0))
    carried = [] if partial is None else list(partial)
    return pl.pallas_call(
        body, name=name, grid=(n_l, nst),
        in_specs=rc_specs + [wspec] * 3 + [pl.BlockSpec(memory_space=pl.ANY)] * len(carried),
        out_specs=[wspec] * 4, out_shape=[jax.ShapeDtypeStruct(w.shape, F32)] * 4,
        input_output_aliases={n_l + 3 + k: k for k in range(len(carried))},
        compiler_params=_cp("arbitrary", "arbitrary"),
    )(*rcs, w, m, v, *carried)


def _adam_packed(g, w, m, v, direct):
    n_d = len(direct)

    def pieces(shape):
        width = shape[-1]
        count = 1
        for s in shape[:-1]:
            count *= s
        per_row = D // width
        out = []
        for k in range(count):
            idx = (k,) if len(shape) == 2 else (k // shape[1], k % shape[1])
            out.append((idx, k // per_row, (k % per_row) * width, width))
        return out

    def body(g_ref, w_ref, m_ref, v_ref, d_out, m_out, v_out, *outs):
        delta, m_new, v_new = _adamw(w_ref[...], g_ref[...], m_ref[...], v_ref[...])
        d_out[...] = delta
        m_out[...] = m_new
        v_out[...] = v_new
        for a, (_, row0, shape) in enumerate(direct):
            for src, dst in zip((g_ref, d_out, m_out, v_out), outs[4 * a:4 * a + 4]):
                for idx, row, lane0, width in pieces(shape):
                    piece = src[pl.ds(row0 + row, 1), lane0:lane0 + width]
                    if len(idx) == 1:
                        dst[pl.ds(idx[0], 1), :] = piece
                    else:
                        dst[idx[0], pl.ds(idx[1], 1), :] = piece

    out_shape = [jax.ShapeDtypeStruct(g.shape, F32)] * 3
    for _, _, shape in direct:
        out_shape += [jax.ShapeDtypeStruct(shape, F32)] * 4
    res = pl.pallas_call(body, name="adam_small", out_shape=out_shape,
                         compiler_params=pltpu.CompilerParams(vmem_limit_bytes=VMEM_LIMIT_BYTES))(g, w, m, v)
    return res[:3], {name: res[3 + 4 * a:7 + 4 * a] for a, (name, _, _) in enumerate(direct)}


def _sum4(rc):
    def body(rc_ref, o_ref):
        o_ref[...] = rc_ref[0] + rc_ref[1] + rc_ref[2] + rc_ref[3]

    return pl.pallas_call(
        body, name="small_sum", out_shape=jax.ShapeDtypeStruct(rc.shape[1:], F32),
    )(rc)


BIG = ("w_in", "w_branch", "w_out", "w_up", "w_down", "w_ple", "w_pleg")
BIG_SHARD = {"w_in": (D, D), "w_branch": (4 * W, GW), "w_out": (GW, D), "w_up": (D, W), "w_down": (W, D),
             "w_ple": (256, GW), "w_pleg": (GW, D)}
ADAM_ROWS = {"w_in": 256, "w_branch": 512, "w_out": 128, "w_up": 256, "w_down": 256, "w_ple": 256, "w_pleg": 128}
SMALL = (("norm_mix", (DEPTH, D)), ("conf_dw", (DEPTH, CONF_K, W)), ("conf_dw_b", (DEPTH, W)),
         ("conf_ln_g", (DEPTH, W)), ("conf_ln_b", (DEPTH, W)), ("pool_w", (DEPTH, 4, GW, GW)),
         ("pool_scale", (DEPTH, W)), ("sc_conv", (DEPTH, SC_K, W)), ("gmlp_ln_g", (DEPTH, W)),
         ("gmlp_ln_b", (DEPTH, W)), ("gmlp_ws", (DEPTH, 4, GW, GW)), ("gmlp_bs", (DEPTH, 4, GW)),
         ("norm_mlp", (DEPTH, D)), ("norm_ple", (DEPTH, D)), ("norm_final", (D,)))
CHANNEL_SHARDED = ("conf_dw", "sc_conv")
SMALL_ROWS = 80


def _pack(arrs, rows):
    flat = jnp.concatenate([a.reshape(-1) for a in arrs])
    return jnp.pad(flat, (0, rows * D - flat.shape[0])).reshape(rows, D)


def _unpack(packed, shapes):
    flat = packed.reshape(-1)
    out, off = [], 0
    for shp in shapes:
        size = 1
        for s in shp:
            size *= s
        out.append(flat[off:off + size].reshape(shp))
        off += size
    return out


def kernel(x, p, norm_mix, w_in, conf_dw, conf_dw_b, conf_ln_g, conf_ln_b, pool_w, pool_scale, sc_conv, gmlp_ln_g, gmlp_ln_b, gmlp_ws, gmlp_bs, w_branch, w_out, norm_mlp, w_up, w_down, norm_ple, w_ple, w_ple_gate, norm_final, loss_target, m_norm_mix, m_w_in, m_conf_dw, m_conf_dw_b, m_conf_ln_g, m_conf_ln_b, m_pool_w, m_pool_scale, m_sc_conv, m_gmlp_ln_g, m_gmlp_ln_b, m_gmlp_ws, m_gmlp_bs, m_w_branch, m_w_out, m_norm_mlp, m_w_up, m_w_down, m_norm_ple, m_w_ple, m_w_ple_gate, m_norm_final, v_norm_mix, v_w_in, v_conf_dw, v_conf_dw_b, v_conf_ln_g, v_conf_ln_b, v_pool_w, v_pool_scale, v_sc_conv, v_gmlp_ln_g, v_gmlp_ln_b, v_gmlp_ws, v_gmlp_bs, v_w_branch, v_w_out, v_norm_mlp, v_w_up, v_w_down, v_norm_ple, v_w_ple, v_w_ple_gate, v_norm_final):
    weights = dict(norm_mix=norm_mix, w_in=w_in, conf_dw=conf_dw, conf_dw_b=conf_dw_b, conf_ln_g=conf_ln_g,
                   conf_ln_b=conf_ln_b, pool_w=pool_w, pool_scale=pool_scale, sc_conv=sc_conv, gmlp_ln_g=gmlp_ln_g,
                   gmlp_ln_b=gmlp_ln_b, gmlp_ws=gmlp_ws, gmlp_bs=gmlp_bs, w_branch=w_branch, w_out=w_out,
                   norm_mlp=norm_mlp, w_up=w_up, w_down=w_down, norm_ple=norm_ple, w_ple=w_ple, w_pleg=w_ple_gate,
                   norm_final=norm_final)
    mom1 = dict(norm_mix=m_norm_mix, w_in=m_w_in, conf_dw=m_conf_dw, conf_dw_b=m_conf_dw_b, conf_ln_g=m_conf_ln_g,
                conf_ln_b=m_conf_ln_b, pool_w=m_pool_w, pool_scale=m_pool_scale, sc_conv=m_sc_conv,
                gmlp_ln_g=m_gmlp_ln_g, gmlp_ln_b=m_gmlp_ln_b, gmlp_ws=m_gmlp_ws, gmlp_bs=m_gmlp_bs,
                w_branch=m_w_branch, w_out=m_w_out, norm_mlp=m_norm_mlp, w_up=m_w_up, w_down=m_w_down,
                norm_ple=m_norm_ple, w_ple=m_w_ple, w_pleg=m_w_ple_gate, norm_final=m_norm_final)
    mom2 = dict(norm_mix=v_norm_mix, w_in=v_w_in, conf_dw=v_conf_dw, conf_dw_b=v_conf_dw_b, conf_ln_g=v_conf_ln_g,
                conf_ln_b=v_conf_ln_b, pool_w=v_pool_w, pool_scale=v_pool_scale, sc_conv=v_sc_conv,
                gmlp_ln_g=v_gmlp_ln_g, gmlp_ln_b=v_gmlp_ln_b, gmlp_ws=v_gmlp_ws, gmlp_bs=v_gmlp_bs,
                w_branch=v_w_branch, w_out=v_w_out, norm_mlp=v_norm_mlp, w_up=v_w_up, w_down=v_w_down,
                norm_ple=v_norm_ple, w_ple=v_w_ple, w_pleg=v_w_ple_gate, norm_final=v_norm_final)

    xi, yi, ci = _mesh_pos()
    me = 4 * xi + 2 * yi + ci
    c_idx = jnp.reshape(ci, (1,)).astype(jnp.int32)

    gathered, conf_full, sc_full = [], [], []
    for l in range(DEPTH):
        shard = lambda n: weights[n][l].astype(BF16).reshape(BIG_SHARD[n])
        w_in_g, conf_g, sc_g = _all_gather([shard("w_in"), conf_dw[l], sc_conv[l]], name="ag_first", seq_id=1)
        if l + 1 < DEPTH:
            rest = _all_gather([shard(n) for n in BIG[1:]], name="ag_rest", seq_id=2)
        else:
            rest = (list(_all_gather([shard(n) for n in BIG[1:4]], name="ag_rest_a", seq_id=2))
                    + list(_all_gather([shard(n) for n in BIG[4:]], name="ag_rest_b", seq_id=2)))
        gw = dict(zip(BIG[1:], rest), w_in=w_in_g)
        gw["w_branch"] = gw["w_branch"].reshape(NDEV, 4, W, GW)
        gathered.append(gw)
        conf_full.append(conf_g)
        sc_full.append(sc_g)

    def small_params(l):
        return dict(cw=conf_full[l], cb=conf_dw_b[l][None], lg=conf_ln_g[l][None], lb=conf_ln_b[l][None],
                    pw=pool_w[l], ps=pool_scale[l][None], sc=sc_full[l], gg=gmlp_ln_g[l][None],
                    gb=gmlp_ln_b[l][None], ws=gmlp_ws[l], bst=gmlp_bs[l].T, g_mix=norm_mix[l][None],
                    g_mlp=norm_mlp[l][None], g_ple=norm_ple[l][None])

    xc = x.reshape(T, D)
    small_names = [n for n, _ in SMALL]

    def in_gradient_layout(n, shard, shape):
        if n not in CHANNEL_SHARDED:
            return shard
        return lax.dynamic_update_slice(jnp.zeros(shape, F32), shard, (0, 0, me * (W // NDEV)))

    small_state = [_pack([in_gradient_layout(n, src[n], shape) for n, shape in SMALL], NDEV * SMALL_ROWS)
                   for src in (weights, mom1, mom2)]
    xc, small_state = lax.optimization_barrier((xc, small_state))
    p_bf = p.reshape(DEPTH, T, 256).astype(BF16)
    h = _norm_first(xc, norm_mix[0][None])
    saved = []
    for l in range(DEPTH):
        g_next = norm_mix[l + 1][None] if l + 1 < DEPTH else norm_final[None]
        h, conf_g, sc_g = lax.optimization_barrier((h, conf_full[l], sc_full[l]))
        conf_full[l] = conf_g.transpose(1, 0, 2).reshape(CONF_K, W)
        sc_full[l] = sc_g.transpose(1, 0, 2).reshape(SC_K, W)
        xc, h, sv = _layer_fwd(xc, h, p_bf[l], gathered[l], small_params(l), g_next)
        saved.append(sv)

    dxc, dg_final, loss_part = _loss_head(xc, loss_target.reshape(T, D), norm_final[None])
    loss = lax.psum(loss_part[0, 0], ("x", "y", "c"))
    small_grads = [None] * DEPTH
    rcs = {n: [None] * DEPTH for n in BIG}
    pipeline = _GradientPipeline(c_idx, rcs)
    for l in reversed(range(DEPTH)):
        dxc, small_grads[l] = _layer_bwd(dxc, saved[l], gathered[l], small_params(l),
                                         lambda dw, names, value, l=l: pipeline.submit(dw, names, l, value),
                                         early_group=(l == 0))

    def adam_sharded(first_layer, n_layers, partial, tag):
        outs = {}
        for n in BIG:
            shp = (DEPTH,) + BIG_SHARD[n]
            outs[n] = _adam_sharded(rcs[n][first_layer:first_layer + n_layers], weights[n].reshape(shp),
                                    mom1[n].reshape(shp), mom2[n].reshape(shp), ADAM_ROWS[n],
                                    "adam_%s_%s" % (n, tag), first_layer, None if partial is None else partial[n])
        return outs

    stacked = {n: jnp.stack([small_grads[l][n] for l in range(DEPTH)]) for n, _ in SMALL if n != "norm_final"}
    stacked["norm_final"] = dg_final[0]
    packed = _pack([stacked[n] for n, _ in SMALL], NDEV * SMALL_ROWS).reshape(NCHIP, 2, SMALL_ROWS, D)
    (pair_small,), _ = _rs_exchange([packed], [], name="rs_pair_small")
    q_small = _pair_sum([packed], [pair_small], c_idx, name="rs_pairsum_small", nst=1)
    dxc, upper, q_small = lax.optimization_barrier((dxc, {n: rcs[n][1:] for n in BIG}, q_small))
    _, (chips_small,) = _rs_exchange([], q_small, name="rs_chip_small", seq_id=6)
    dxc, upper = pipeline.finish((dxc, upper))
    for n in BIG:
        rcs[n][1:] = upper[n]
    partial = adam_sharded(1, DEPTH - 1, None, "upper")
    partial, chips_small = lax.optimization_barrier((partial, chips_small))
    reduced_slot = _sum4(chips_small)
    reduced = _all_gather([reduced_slot], name="ag_small", seq_id=7)[0]
    small_full = dict(zip([n for n, _ in SMALL], _unpack(reduced, [s for _, s in SMALL])))
    grads, deltas, new_m, new_v = {}, {}, {}, {}
    direct, row = [], 0
    for n, shape in SMALL:
        if len(shape) == 1 or shape[-2] == DEPTH:
            direct.append((n, row, (1,) * (2 - len(shape)) + tuple(shape)))
        size = 1
        for s in shape:
            size *= s
        row += size // D
    (d_p, m_p, v_p), own_shape = _adam_packed(reduced.reshape(NDEV * SMALL_ROWS, D), *small_state, direct)
    small_shapes = [s for _, s in SMALL]

    def own_channels(n, full):
        return lax.dynamic_slice_in_dim(full, me * (W // NDEV), W // NDEV, axis=2) if n in CHANNEL_SHARDED else full

    for n, d_, m_, v_ in zip(small_names, _unpack(d_p, small_shapes), _unpack(m_p, small_shapes),
                             _unpack(v_p, small_shapes)):
        if n in own_shape:
            grads[n], deltas[n], new_m[n], new_v[n] = [a.reshape(weights[n].shape) for a in own_shape[n]]
        else:
            grads[n], deltas[n], new_m[n], new_v[n] = (own_channels(n, small_full[n]), own_channels(n, d_),
                                                       own_channels(n, m_), own_channels(n, v_))

    for n, (g_, d_, m_, v_) in adam_sharded(0, 1, partial, "last").items():
        full = weights[n].shape
        grads[n], deltas[n], new_m[n], new_v[n] = g_.reshape(full), d_.reshape(full), m_.reshape(full), v_.reshape(full)

    order = ("norm_mix", "w_in", "conf_dw", "conf_dw_b", "conf_ln_g", "conf_ln_b", "pool_w", "pool_scale", "sc_conv",
             "gmlp_ln_g", "gmlp_ln_b", "gmlp_ws", "gmlp_bs", "w_branch", "w_out", "norm_mlp", "w_up", "w_down",
             "norm_ple", "w_ple", "w_pleg", "norm_final")
    return (loss, dxc.reshape(1, T, D), *[grads[n] for n in order], *[deltas[n] for n in order],
            *[new_m[n] for n in order], *[new_v[n] for n in order])
```

```python
import functools

import jax
import jax.numpy as jnp
from jax import lax
from jax.experimental import pallas as pl
from jax.experimental.pallas import tpu as pltpu
from jax.experimental.pallas import tpu_sc as plsc

F32 = jnp.float32
BF16 = jnp.bfloat16

DEPTH = 4
T = 2048
D = 1024
W = 512
NDEV = 8
NCHIP = 4
EPS = 1e-6
CONF_K = 31
SC_K = 3
POOL_WINDOWS = (2, 4, 8, 16)
GW = 128
HB = 32
HA = 32
COLS_IN = 8192
MIX_COLS = 4096

ADAM_LR = 0.001
ADAM_B1 = 0.9
ADAM_B2 = 0.999
ADAM_EPS = 1e-08
ADAM_WD = 0.01
ADAM_STEP = 10

VMEM_LIMIT_BYTES = 56 * 1024 * 1024
MESH = pl.DeviceIdType.MESH


def _cp(*sem):
    return pltpu.CompilerParams(dimension_semantics=tuple(sem), vmem_limit_bytes=VMEM_LIMIT_BYTES)


def _sig(x):
    return jax.nn.sigmoid(x)


def _rms(x, g):
    r = lax.rsqrt(jnp.mean(x * x, axis=-1, keepdims=True) + EPS)
    return x * r * g


def _rms_bwd(dh, x, g, dres):
    r = lax.rsqrt(jnp.mean(x * x, axis=-1, keepdims=True) + EPS)
    xh = x * r
    u = dh * g
    dx = r * (u - xh * jnp.mean(u * xh, axis=-1, keepdims=True)) + dres
    dg = jnp.sum(dh * xh, axis=0, keepdims=True)
    return dx, dg


def _ln_stats(x):
    mu = jnp.mean(x, axis=-1, keepdims=True)
    xc = x - mu
    rstd = lax.rsqrt(jnp.mean(xc * xc, axis=-1, keepdims=True) + EPS)
    return xc * rstd, rstd


def _ln_bwd(dxh, xh, rstd):
    return rstd * (dxh - jnp.mean(dxh, axis=-1, keepdims=True) - xh * jnp.mean(dxh * xh, axis=-1, keepdims=True))


def _rowsum(x):
    return jnp.sum(x, axis=0, keepdims=True)


EPI_ROWS = 256


def _relu2_bf16(up):
    r = jnp.maximum(up.astype(F32), 0.0)
    return (r * r).astype(BF16)


def _mm(a, b3, *, mode, name, outs, trans_b=False, tm=512, tiles=(), params=(), epi=None, reds=(), a_pre=None):
    t_, ka = a.shape
    nj, r, c = b3.shape
    kb, nb = (c, r) if trans_b else (r, c)
    nt = t_ // tm
    out_mode = mode == "out"
    full = mode == "full"
    assert trans_b or not full
    if out_mode:
        assert ka == kb and not reds
        grid = (nj, nt)
        a_map = lambda g0, g1: (g1, 0)
        b_map = lambda g0, g1: (g0, 0, 0)
        t_map = lambda g0, g1: (g1, g0)
        width = nj * nb
    else:
        assert ka == nj * kb
        grid = (nt, 1 if full else nj)
        a_map = lambda g0, g1: (g0, g1)
        b_map = lambda g0, g1: (g1, 0, 0)
        t_map = lambda g0, g1: (g0, 0)
        width = nb
    n_t, n_p, n_o, n_r = len(tiles), len(params), len(outs), len(reds)
    use_acc = (not out_mode) and nj > 1 and not full
    dims = (((1,), (1,)), ((), ())) if trans_b else (((1,), (0,)), ((), ()))

    def body(a_ref, b_ref, *rest):
        t_refs = rest[:n_t]
        p_refs = rest[n_t:n_t + n_p]
        o_refs = rest[n_t + n_p:n_t + n_p + n_o]
        r_refs = rest[n_t + n_p + n_o:n_t + n_p + n_o + n_r]
        i = pl.program_id(1 if out_mode else 0)
        a_val = a_ref[...] if a_pre is None else a_pre(a_ref[...])
        if full:
            b_all, b_sems = rest[-2], rest[-1]

            @pl.when(i == 0)
            def _():
                cps = [pltpu.make_async_copy(b_ref.at[j], b_all.at[:, j * c:(j + 1) * c], b_sems.at[j])
                       for j in range(nj)]
                for cp in cps:
                    cp.start()
                for cp in cps:
                    cp.wait()

            part = lax.dot_general(a_val, b_all[...], dims, preferred_element_type=F32)
        else:
            part = lax.dot_general(a_val, b_ref[...], dims, preferred_element_type=F32)

        def finish(acc_rows):
            totals = [None] * n_r
            for r0 in range(0, tm, min(tm, EPI_ROWS)):
                rows = slice(r0, r0 + min(tm, EPI_ROWS))
                if epi is None:
                    res, rr = (acc_rows(rows),), ()
                else:
                    res, rr = epi(acc_rows(rows), [t[rows, :] for t in t_refs], [p[...] for p in p_refs])
                for o_ref, val in zip(o_refs, res):
                    o_ref[rows, :] = val.astype(o_ref.dtype)
                totals = [val if tot is None else tot + val for tot, val in zip(totals, rr)]
            for r_ref, val in zip(r_refs, totals):
                @pl.when(i == 0)
                def _():
                    r_ref[...] = val

                @pl.when(i > 0)
                def _():
                    r_ref[...] += val

        if use_acc:
            acc_ref = rest[-1]
            j = pl.program_id(1)

            @pl.when(j == 0)
            def _():
                acc_ref[...] = part

            @pl.when(jnp.logical_and(j > 0, j < nj - 1))
            def _():
                acc_ref[...] += part

            @pl.when(j == nj - 1)
            def _():
                finish(lambda rows: acc_ref[rows, :] + part[rows])
        else:
            finish(lambda rows: part[rows, :])

    const2 = lambda g0, g1: (0, 0)
    if full:
        in_specs = [pl.BlockSpec((tm, ka), a_map), pl.BlockSpec(memory_space=pl.ANY)]
        scratch = [pltpu.VMEM((r, nj * c), b3.dtype), pltpu.SemaphoreType.DMA((nj,))]
    else:
        in_specs = [pl.BlockSpec((tm, kb), a_map), pl.BlockSpec((None, r, c), b_map)]
        scratch = [pltpu.VMEM((tm, nb), F32)] if use_acc else []
    in_specs += [pl.BlockSpec((tm, t.shape[1] // nj if out_mode else t.shape[1]), t_map) for t in tiles]
    in_specs += [pl.BlockSpec(p.shape, lambda g0, g1, nd=p.ndim: (0,) * nd) for p in params]
    out_specs = [pl.BlockSpec((tm, nb), t_map) for _ in outs] + [pl.BlockSpec((1, w), const2) for w in reds]
    out_shape = [jax.ShapeDtypeStruct((t_, width), dt) for dt in outs]
    out_shape += [jax.ShapeDtypeStruct((1, w), F32) for w in reds]
    res = pl.pallas_call(
        body, name=name, grid=grid, in_specs=in_specs, out_specs=out_specs, out_shape=out_shape,
        scratch_shapes=scratch, compiler_params=_cp("arbitrary", "arbitrary"),
    )(a, b3, *tiles, *params)
    return res


def _mm_tn(a, g, *, nj, split, name, out_dtype=BF16, a_pre=None):
    t_ = a.shape[0]
    if split == "col":
        r, c = a.shape[1], g.shape[1] // nj
        a_spec = pl.BlockSpec((t_, r), lambda j: (0, 0))
        g_spec = pl.BlockSpec((t_, c), lambda j: (0, j))
    else:
        r, c = a.shape[1] // nj, g.shape[1]
        a_spec = pl.BlockSpec((t_, r), lambda j: (0, j))
        g_spec = pl.BlockSpec((t_, c), lambda j: (0, 0))

    def body(a_ref, g_ref, o_ref):
        a_val = a_ref[...] if a_pre is None else a_pre(a_ref[...])
        o_ref[...] = lax.dot_general(a_val, g_ref[...], (((0,), (0,)), ((), ())),
                                     preferred_element_type=F32).astype(o_ref.dtype)

    return pl.pallas_call(
        body, name=name, grid=(nj,), in_specs=[a_spec, g_spec],
        out_specs=pl.BlockSpec((None, r, c), lambda j: (j, 0, 0)),
        out_shape=jax.ShapeDtypeStruct((nj, r, c), out_dtype),
        compiler_params=_cp("arbitrary"),
    )(a, g)


def _epi_res_norm(acc, tiles, params):
    x_new = tiles[0] + acc
    return (x_new, _rms(x_new, params[0])), ()


def _epi_ple(acc, tiles, params):
    x_old, p_tile = tiles
    g_next, w_ple8 = params
    pe = jnp.concatenate([jnp.dot(p_tile, w_ple8[j], preferred_element_type=F32) for j in range(NDEV)], axis=1)
    x_new = x_old + pe * _sig(acc)
    return (x_new, acc, _rms(x_new, g_next), pe), ()


def _epi_rms_bwd(acc, tiles, params):
    dx, dg = _rms_bwd(acc, tiles[0], params[0], tiles[1])
    return (dx, dx), (dg,)


def _epi_dup(acc, tiles, params):
    return (acc * (2.0 * jnp.maximum(tiles[0].astype(F32), 0.0)),), ()


def _tri_mask():
    row = lax.broadcasted_iota(jnp.int32, (GW, GW), 0)
    col = lax.broadcasted_iota(jnp.int32, (GW, GW), 1)
    return row >= col


def _small_specs(sp_list):
    return [pl.BlockSpec(p.shape, (lambda i: (0, 0)) if p.ndim == 2 else (lambda i: (0, 0, 0))) for p in sp_list]


SUBLANES = 8


def _tap_sum(src, w_ref, taps, rows, stage):
    groups = {}
    for off, k in taps:
        groups.setdefault(off % SUBLANES, []).append((off - off % SUBLANES, k))
    out = None
    for res, members in sorted(groups.items()):
        n = rows if res == 0 else rows + SUBLANES
        part = None
        for base, k in members:
            term = w_ref[k:k + 1, :] * src[pl.ds(base, n), :]
            part = term if part is None else part + term
        if res:
            stage[0:n, :] = part
            part = stage[pl.ds(res, rows), :]
        out = part if out is None else out + part
    return out


def _tap_grads(grad, src, offsets, rows, stage, out_ref):
    pad = SUBLANES
    stage[0:pad, :] = jnp.zeros((pad, grad.shape[1]), F32)
    stage[pad:pad + rows, :] = grad
    stage[pad + rows:2 * pad + rows, :] = jnp.zeros((pad, grad.shape[1]), F32)
    groups = {}
    for k, off in enumerate(offsets):
        groups.setdefault(off % SUBLANES, []).append((off - off % SUBLANES, k))
    for res, members in sorted(groups.items()):
        shifted = stage[pl.ds(pad - res, rows + pad), :]
        for base, k in members:
            out_ref[k:k + 1, :] += _rowsum(shifted * src[pl.ds(base, rows + pad), :])


def _mixer_params(sp):
    return [sp["cw"], sp["cb"], sp["lg"], sp["lb"], sp["pw"], sp["ps"], sp["sc"], sp["gg"], sp["gb"], sp["ws"], sp["bst"]]


def _mixer_fwd(proj, sp, tm=256):
    nt = T // tm
    per = tm // HB

    conv_taps = [(HB - (CONF_K - 1) + k, k) for k in range(CONF_K)]

    def body(main_ref, halo_ref, cw, cb, lg, lb, pw, ps, sc, gg, gb, ws, bst, y_ref, ca_ref, ext, stage):
        i = pl.program_id(0)
        keep = (i > 0).astype(F32)

        def mcol(c0):
            return main_ref[:, c0:c0 + W].astype(F32)

        def hcol(c0):
            return halo_ref[:, c0:c0 + W].astype(F32)

        ext[0:HB, :] = hcol(0) * _sig(hcol(W)) * keep
        ext[HB:HB + tm, :] = mcol(0) * _sig(mcol(W))
        ca = (_tap_sum(ext, cw, conv_taps, tm, stage) + cb[...]).astype(BF16)
        ca_ref[...] = ca
        xh, _ = _ln_stats(ca.astype(F32))
        n = xh * lg[...] + lb[...]
        y_ref[:, 0:W] = (n * _sig(n)).astype(BF16)

        pin = mcol(1024)
        ext[0:HB, :] = hcol(1024) * keep
        ext[HB:HB + tm, :] = pin
        pos = (i * tm + lax.broadcasted_iota(jnp.int32, (tm, 1), 0) + 1).astype(F32)
        for g, w in enumerate(POOL_WINDOWS):
            lo = g * GW
            s = ext[pl.ds(HB, tm), lo:lo + GW]
            for j in range(1, w):
                s = s + ext[pl.ds(HB - j, tm), lo:lo + GW]
            pooled = s / jnp.minimum(pos, float(w)) - pin[:, lo:lo + GW]
            mixed = jnp.dot(pooled.astype(BF16), pw[g].astype(BF16), preferred_element_type=F32)
            y_ref[:, W + lo:W + lo + GW] = (mixed * ps[:, lo:lo + GW]).astype(BF16)

        ext[0:HB, :] = hcol(2048) * hcol(2560) * keep
        ext[HB:HB + tm, :] = mcol(2048) * mcol(2560)
        cv = sc[0:1, :] * ext[pl.ds(HB - 2, tm), :]
        cv = cv + sc[1:2, :] * ext[pl.ds(HB - 1, tm), :]
        cv = cv + sc[2:3, :] * ext[pl.ds(HB, tm), :]
        y_ref[:, 2 * W:3 * W] = (mcol(1536) * cv).astype(BF16)

        vh, _ = _ln_stats(mcol(3584))
        vn = (vh * gg[...] + gb[...]).astype(BF16)
        u = mcol(3072)
        tri = _tri_mask()
        for g in range(4):
            lo = g * GW
            wm = jnp.where(tri, ws[g], 0.0).astype(BF16)
            for c in range(tm // GW):
                r0 = c * GW
                sg = jnp.dot(wm, vn[r0:r0 + GW, lo:lo + GW], preferred_element_type=F32) + bst[:, g:g + 1]
                y_ref[r0:r0 + GW, 3 * W + lo:3 * W + lo + GW] = (u[r0:r0 + GW, lo:lo + GW] * sg).astype(BF16)

    plist = _mixer_params(sp)
    in_specs = [pl.BlockSpec((tm, MIX_COLS), lambda i: (i, 0)),
                pl.BlockSpec((HB, MIX_COLS), lambda i: (jnp.maximum(i * per - 1, 0), 0))]
    in_specs += _small_specs(plist)
    return pl.pallas_call(
        body, name="f_mixers", grid=(nt,), in_specs=in_specs,
        out_specs=[pl.BlockSpec((tm, 4 * W), lambda i: (i, 0)), pl.BlockSpec((tm, W), lambda i: (i, 0))],
        out_shape=[jax.ShapeDtypeStruct((T, 4 * W), BF16), jax.ShapeDtypeStruct((T, W), BF16)],
        scratch_shapes=[pltpu.VMEM((HB + tm, W), F32), pltpu.VMEM((tm + SUBLANES, W), F32)],
        compiler_params=_cp("arbitrary"),
    )(proj, proj, *plist)


def _assemble_wb(wb8_ref, wbf_ref):
    for k in range(4):
        for j in range(NDEV):
            wbf_ref[k, :, j * GW:(j + 1) * GW] = wb8_ref[j, k]


def _merge_fwd(y, proj, wb8, tm=256):
    nt = T // tm

    def body(y_ref, gate_ref, wb8_ref, z_ref, m_ref, wbf):
        @pl.when(pl.program_id(0) == 0)
        def _():
            _assemble_wb(wb8_ref, wbf)

        m = jnp.zeros((tm, D), F32)
        for k in range(4):
            zk = jnp.dot(y_ref[:, k * W:(k + 1) * W], wbf[k], preferred_element_type=F32)
            z_ref[:, k * D:(k + 1) * D] = zk.astype(BF16)
            m = m + _sig(gate_ref[:, k * D:(k + 1) * D].astype(F32)) * zk
        m_ref[...] = m.astype(BF16)

    return pl.pallas_call(
        body, name="f_merge", grid=(nt,),
        in_specs=[pl.BlockSpec((tm, 4 * W), lambda i: (i, 0)),
                  pl.BlockSpec((tm, 4 * D), lambda i: (i, 1)),
                  pl.BlockSpec(wb8.shape, lambda i: (0, 0, 0, 0))],
        out_specs=[pl.BlockSpec((tm, 4 * D), lambda i: (i, 0)), pl.BlockSpec((tm, D), lambda i: (i, 0))],
        out_shape=[jax.ShapeDtypeStruct((T, 4 * D), BF16), jax.ShapeDtypeStruct((T, D), BF16)],
        scratch_shapes=[pltpu.VMEM((4, W, D), BF16)],
        compiler_params=_cp("arbitrary"),
    )(y, proj, wb8)


def _merge_bwd(dm, z, proj, y, wb8, tm=256):
    nt = T // tm

    def body(dm_ref, z_ref, gate_ref, y_ref, wb8_ref, dp_ref, dy_ref, dwb_ref, wbf, acc):
        i = pl.program_id(0)

        @pl.when(i == 0)
        def _():
            _assemble_wb(wb8_ref, wbf)

        dmv = dm_ref[...].astype(F32)
        for k in range(4):
            s = _sig(gate_ref[:, k * D:(k + 1) * D].astype(F32))
            dzk = (dmv * s).astype(BF16)
            dp_ref[:, k * D:(k + 1) * D] = (dmv * z_ref[:, k * D:(k + 1) * D].astype(F32) * s * (1.0 - s)).astype(BF16)
            dyk = lax.dot_general(dzk, wbf[k], (((1,), (1,)), ((), ())), preferred_element_type=F32)
            dy_ref[:, k * W:(k + 1) * W] = dyk.astype(BF16)
            part = lax.dot_general(y_ref[:, k * W:(k + 1) * W], dzk, (((0,), (0,)), ((), ())),
                                   preferred_element_type=F32)

            @pl.when(i == 0)
            def _():
                acc[k] = part

            @pl.when(i > 0)
            def _():
                acc[k] += part

        @pl.when(i == nt - 1)
        def _():
            for k in range(4):
                for j in range(NDEV):
                    dwb_ref[j, k] = acc[k, :, j * GW:(j + 1) * GW].astype(BF16)

    return pl.pallas_call(
        body, name="b_merge", grid=(nt,),
        in_specs=[pl.BlockSpec((tm, D), lambda i: (i, 0)),
                  pl.BlockSpec((tm, 4 * D), lambda i: (i, 0)),
                  pl.BlockSpec((tm, 4 * D), lambda i: (i, 1)),
                  pl.BlockSpec((tm, 4 * W), lambda i: (i, 0)),
                  pl.BlockSpec(wb8.shape, lambda i: (0, 0, 0, 0))],
        out_specs=[pl.BlockSpec((tm, 4 * D), lambda i: (i, 1)),
                   pl.BlockSpec((tm, 4 * W), lambda i: (i, 0)),
                   pl.BlockSpec(wb8.shape, lambda i: (0, 0, 0, 0))],
        out_shape=[jax.ShapeDtypeStruct((T, COLS_IN), BF16),
                   jax.ShapeDtypeStruct((T, 4 * W), BF16),
                   jax.ShapeDtypeStruct(wb8.shape, BF16)],
        scratch_shapes=[pltpu.VMEM((4, W, D), BF16), pltpu.VMEM((4, W, D), F32)],
        compiler_params=_cp("arbitrary"),
    )(dm, z, proj, y, wb8)


def _mixer_bwd(proj, ca_saved, dy, dproj, sp, tm=256):
    nt = T // tm
    per = tm // HB
    ne = tm + HA
    last_blk = T // HA - 1
    conv_taps = [(HB - (CONF_K - 1) + k, k) for k in range(CONF_K)]

    def body(main_ref, hb_ref, ha_ref, ca_ref, cah_ref, dy_ref, dyh_ref, cw, cb, lg, lb, pw, ps, sc, gg, gb, ws, bst,
             dp_any, dp_ref, dcw_ref, dsc_ref, vec_ref, dpw_ref, dws_ref, dbs_ref, e1, e2, e3, stage):
        del dp_any, cb
        i = pl.program_id(0)
        keep_b = (i > 0).astype(F32)
        keep_a = (i < nt - 1).astype(F32)

        @pl.when(i == 0)
        def _():
            dcw_ref[...] = jnp.zeros_like(dcw_ref)
            dsc_ref[...] = jnp.zeros_like(dsc_ref)
            vec_ref[...] = jnp.zeros_like(vec_ref)
            dpw_ref[...] = jnp.zeros_like(dpw_ref)
            dws_ref[...] = jnp.zeros_like(dws_ref)
            dbs_ref[...] = jnp.zeros_like(dbs_ref)

        def mcol(c0):
            return main_ref[:, c0:c0 + W].astype(F32)

        def hbcol(c0):
            return hb_ref[:, c0:c0 + W].astype(F32)

        def hacol(c0):
            return ha_ref[:, c0:c0 + W].astype(F32)

        def load_dy(c0):
            e2[0:tm, :] = dy_ref[:, c0:c0 + W].astype(F32)
            e2[tm:ne, :] = dyh_ref[:, c0:c0 + W].astype(F32) * keep_a

        a = mcol(0)
        sa = _sig(mcol(W))
        e1[0:HB, :] = hbcol(0) * _sig(hbcol(W)) * keep_b
        e1[HB:HB + tm, :] = a * sa
        e1[HB + tm:HB + tm + SUBLANES, :] = jnp.zeros((SUBLANES, W), F32)
        e2[0:tm, :] = ca_ref[...].astype(F32)
        e2[tm:ne, :] = cah_ref[...].astype(F32)
        xh, rstd = _ln_stats(e2[0:ne, :])
        nn = xh * lg[...] + lb[...]
        s = _sig(nn)
        load_dy(0)
        dn = e2[0:ne, :] * (s * (1.0 + nn * (1.0 - s)))
        vec_ref[1:2, :] += _rowsum(dn[0:tm] * xh[0:tm])
        vec_ref[2:3, :] += _rowsum(dn[0:tm])
        dca = _ln_bwd(dn * lg[...], xh, rstd)
        e3[0:ne, :] = dca
        dmain = dca[0:tm]
        vec_ref[0:1, :] += _rowsum(dmain)
        _tap_grads(dmain, e1, [off for off, _ in conv_taps], tm, stage, dcw_ref)
        dglu = _tap_sum(e3, cw, [(CONF_K - 1 - k, k) for k in range(CONF_K)], tm, stage)
        dp_ref[:, 0:W] = (dglu * sa).astype(BF16)
        dp_ref[:, W:2 * W] = (dglu * a * sa * (1.0 - sa)).astype(BF16)

        pin = mcol(1024)
        e1[0:HB, :] = hbcol(1024) * keep_b
        e1[HB:HB + tm, :] = pin
        load_dy(W)
        dyb = e2[0:ne, :]
        pos_m = (i * tm + lax.broadcasted_iota(jnp.int32, (tm, 1), 0) + 1).astype(F32)
        pos_e = (i * tm + lax.broadcasted_iota(jnp.int32, (ne, 1), 0) + 1).astype(F32)
        for g, w in enumerate(POOL_WINDOWS):
            lo = g * GW
            acc = e1[pl.ds(HB, tm), lo:lo + GW]
            for j in range(1, w):
                acc = acc + e1[pl.ds(HB - j, tm), lo:lo + GW]
            pooled = (acc / jnp.minimum(pos_m, float(w)) - pin[:, lo:lo + GW]).astype(BF16)
            pwb = pw[g].astype(BF16)
            mixed = jnp.dot(pooled, pwb, preferred_element_type=F32)
            dyb_g = dyb[:, lo:lo + GW]
            vec_ref[3:4, lo:lo + GW] += _rowsum(dyb_g[0:tm] * mixed)
            dmb = (dyb_g * ps[:, lo:lo + GW]).astype(BF16)
            dpw_ref[g] += lax.dot_general(pooled, dmb[0:tm], (((0,), (0,)), ((), ())), preferred_element_type=F32)
            dpool = lax.dot_general(dmb, pwb, (((1,), (1,)), ((), ())), preferred_element_type=F32)
            e3[0:ne, lo:lo + GW] = dpool / jnp.minimum(pos_e, float(w))
            back = e3[pl.ds(0, tm), lo:lo + GW]
            for j in range(1, w):
                back = back + e3[pl.ds(j, tm), lo:lo + GW]
            dp_ref[:, 1024 + lo:1024 + lo + GW] = (back - dpool[0:tm]).astype(BF16)

        cg = mcol(2048)
        hx = mcol(2560)
        e1[0:HB, :] = hbcol(2048) * hbcol(2560) * keep_b
        e1[HB:HB + tm, :] = cg * hx
        load_dy(2 * W)
        dyc = e2[0:tm, :]
        dconv = dyc * mcol(1536)
        e3[0:tm, :] = dconv
        e3[tm:ne, :] = e2[tm:ne, :] * hacol(1536)
        cv = sc[0:1, :] * e1[pl.ds(HB - 2, tm), :]
        for k in range(1, SC_K):
            cv = cv + sc[k:k + 1, :] * e1[pl.ds(HB - 2 + k, tm), :]
        dp_ref[:, 1536:2048] = (dyc * cv).astype(BF16)
        for k in range(SC_K):
            dsc_ref[k:k + 1, :] += _rowsum(dconv * e1[pl.ds(HB - 2 + k, tm), :])
        dq = sc[0:1, :] * e3[pl.ds(2, tm), :]
        for k in range(1, SC_K):
            dq = dq + sc[k:k + 1, :] * e3[pl.ds(2 - k, tm), :]
        dp_ref[:, 2048:2560] = (dq * hx).astype(BF16)
        dp_ref[:, 2560:3072] = (dq * cg).astype(BF16)

        u = mcol(3072)
        vh, vr = _ln_stats(mcol(3584))
        vn = (vh * gg[...] + gb[...]).astype(BF16)
        dyd = dy_ref[:, 3 * W:4 * W].astype(F32)
        tri = _tri_mask()
        for g in range(4):
            lo = g * GW
            wm = jnp.where(tri, ws[g], 0.0).astype(BF16)
            dws_g = jnp.zeros((GW, GW), F32)
            dbs_g = jnp.zeros((GW, 1), F32)
            for c in range(tm // GW):
                r0 = c * GW
                blk = vn[r0:r0 + GW, lo:lo + GW]
                sg = jnp.dot(wm, blk, preferred_element_type=F32) + bst[:, g:g + 1]
                dyd_b = dyd[r0:r0 + GW, lo:lo + GW]
                dp_ref[r0:r0 + GW, 3072 + lo:3072 + lo + GW] = (dyd_b * sg).astype(BF16)
                dsg = dyd_b * u[r0:r0 + GW, lo:lo + GW]
                dsgb = dsg.astype(BF16)
                dbs_g = dbs_g + jnp.sum(dsg, axis=-1, keepdims=True)
                dws_g = dws_g + lax.dot_general(dsgb, blk, (((1,), (1,)), ((), ())), preferred_element_type=F32)
                e1[r0:r0 + GW, lo:lo + GW] = lax.dot_general(wm, dsgb, (((0,), (0,)), ((), ())),
                                                             preferred_element_type=F32)
            dws_ref[g] += jnp.where(tri, dws_g, 0.0)
            dbs_ref[g] += jnp.broadcast_to(dbs_g, (GW, GW))
        dvn = e1[0:tm, :]
        vec_ref[4:5, :] += _rowsum(dvn * vh)
        vec_ref[5:6, :] += _rowsum(dvn)
        dp_ref[:, 3584:4096] = _ln_bwd(dvn * gg[...], vh, vr).astype(BF16)

    plist = _mixer_params(sp)
    in_specs = [pl.BlockSpec((tm, MIX_COLS), lambda i: (i, 0)),
                pl.BlockSpec((HB, MIX_COLS), lambda i: (jnp.maximum(i * per - 1, 0), 0)),
                pl.BlockSpec((HA, MIX_COLS), lambda i: (jnp.minimum((i + 1) * per, last_blk), 0)),
                pl.BlockSpec((tm, W), lambda i: (i, 0)),
                pl.BlockSpec((HA, W), lambda i: (jnp.minimum((i + 1) * per, last_blk), 0)),
                pl.BlockSpec((tm, 4 * W), lambda i: (i, 0)),
                pl.BlockSpec((HA, 4 * W), lambda i: (jnp.minimum((i + 1) * per, last_blk), 0))]
    in_specs += _small_specs(plist)
    in_specs += [pl.BlockSpec(memory_space=pl.ANY)]
    z2 = lambda i: (0, 0)
    z3 = lambda i: (0, 0, 0)
    out_specs = [pl.BlockSpec((tm, MIX_COLS), lambda i: (i, 0)),
                 pl.BlockSpec((32, W), z2), pl.BlockSpec((8, W), z2), pl.BlockSpec((8, W), z2),
                 pl.BlockSpec((4, GW, GW), z3), pl.BlockSpec((4, GW, GW), z3), pl.BlockSpec((4, GW, GW), z3)]
    out_shape = [jax.ShapeDtypeStruct((T, COLS_IN), BF16),
                 jax.ShapeDtypeStruct((32, W), F32), jax.ShapeDtypeStruct((8, W), F32),
                 jax.ShapeDtypeStruct((8, W), F32),
                 jax.ShapeDtypeStruct((4, GW, GW), F32), jax.ShapeDtypeStruct((4, GW, GW), F32),
                 jax.ShapeDtypeStruct((4, GW, GW), F32)]
    n_in = 7 + len(plist)
    return pl.pallas_call(
        body, name="b_mixers", grid=(nt,), in_specs=in_specs, out_specs=out_specs, out_shape=out_shape,
        scratch_shapes=[pltpu.VMEM((HB + ne, W), F32), pltpu.VMEM((ne, W), F32), pltpu.VMEM((ne, W), F32),
                        pltpu.VMEM((ne + SUBLANES, W), F32)],
        input_output_aliases={n_in: 0},
        compiler_params=_cp("arbitrary"),
    )(proj, proj, proj, ca_saved, ca_saved, dy, dy, *plist, dproj)


def _norm_first(x, g, tm=512):
    def body(x_ref, g_ref, o_ref):
        o_ref[...] = _rms(x_ref[...], g_ref[...]).astype(BF16)

    return pl.pallas_call(
        body, name="f_norm0", grid=(T // tm,),
        in_specs=[pl.BlockSpec((tm, D), lambda i: (i, 0)), pl.BlockSpec((1, D), lambda i: (0, 0))],
        out_specs=pl.BlockSpec((tm, D), lambda i: (i, 0)),
        out_shape=jax.ShapeDtypeStruct((T, D), BF16), compiler_params=_cp("arbitrary"),
    )(x, g)


def _loss_head(x, target, g, tm=256):
    def body(x_ref, t_ref, g_ref, dx_ref, dg_ref, loss_ref):
        i = pl.program_id(0)
        x = x_ref[...]
        r = lax.rsqrt(jnp.mean(x * x, axis=-1, keepdims=True) + EPS)
        xh = x * r
        gv = g_ref[...]
        e = xh * gv - t_ref[...]
        dyv = e * (1.0 / D)
        part = jnp.sum(_rowsum(e * e), axis=-1, keepdims=True) * (0.5 / D)
        u = dyv * gv
        dx_ref[...] = r * (u - xh * jnp.mean(u * xh, axis=-1, keepdims=True))
        dgp = _rowsum(dyv * xh)

        @pl.when(i == 0)
        def _():
            dg_ref[...] = dgp
            loss_ref[...] = jnp.broadcast_to(part, (1, GW))

        @pl.when(i > 0)
        def _():
            dg_ref[...] += dgp
            loss_ref[...] += jnp.broadcast_to(part, (1, GW))

    return pl.pallas_call(
        body, name="loss_head", grid=(T // tm,),
        in_specs=[pl.BlockSpec((tm, D), lambda i: (i, 0)), pl.BlockSpec((tm, D), lambda i: (i, 0)),
                  pl.BlockSpec((1, D), lambda i: (0, 0))],
        out_specs=[pl.BlockSpec((tm, D), lambda i: (i, 0)), pl.BlockSpec((1, D), lambda i: (0, 0)),
                   pl.BlockSpec((1, GW), lambda i: (0, 0))],
        out_shape=[jax.ShapeDtypeStruct((T, D), F32), jax.ShapeDtypeStruct((1, D), F32),
                   jax.ShapeDtypeStruct((1, GW), F32)],
        compiler_params=_cp("arbitrary"),
    )(x, target, g)


def _ple_bwd(dx4, sv, w_pleg8, g_ple, tm=256):
    nt = T // tm
    ple_dim = sv["p"].shape[1]

    def body(dx_ref, gl_ref, pe_ref, x_ref, h_ref, p_ref, g_ref, wg_ref,
             dx3_ref, dx3b_ref, dg_ref, dwg_ref, dwp_ref, acc_g, acc_p):
        i = pl.program_id(0)
        d = dx_ref[...]
        s = _sig(gl_ref[...].astype(F32))
        dpe = (d * s).astype(BF16)
        dgl = (d * pe_ref[...].astype(F32) * s * (1.0 - s)).astype(BF16)
        dh = lax.dot_general(dgl, wg_ref[...], (((1,), (1,)), ((), ())), preferred_element_type=F32)
        dx, dgp = _rms_bwd(dh, x_ref[...], g_ref[...], d)
        dx3_ref[...] = dx
        dx3b_ref[...] = dx.astype(BF16)
        part_g = lax.dot_general(h_ref[...], dgl, (((0,), (0,)), ((), ())), preferred_element_type=F32)
        part_p = lax.dot_general(p_ref[...], dpe, (((0,), (0,)), ((), ())), preferred_element_type=F32)

        @pl.when(i == 0)
        def _():
            dg_ref[...] = dgp
            acc_g[...] = part_g
            acc_p[...] = part_p

        @pl.when(i > 0)
        def _():
            dg_ref[...] += dgp
            acc_g[...] += part_g
            acc_p[...] += part_p

        @pl.when(i == nt - 1)
        def _():
            for j in range(NDEV):
                dwg_ref[j] = acc_g[j * GW:(j + 1) * GW, :].astype(BF16)
                dwp_ref[j] = acc_p[:, j * GW:(j + 1) * GW].astype(BF16)

    tile = lambda w: pl.BlockSpec((tm, w), lambda i: (i, 0))
    const = lambda shp: pl.BlockSpec(shp, lambda i: (0,) * len(shp))
    return pl.pallas_call(
        body, name="b_ple", grid=(nt,),
        in_specs=[tile(D), tile(D), tile(D), tile(D), tile(D), tile(ple_dim), const((1, D)), const((D, D))],
        out_specs=[tile(D), tile(D), const((1, D)), const((NDEV, GW, D)), const((NDEV, ple_dim, GW))],
        out_shape=[jax.ShapeDtypeStruct((T, D), F32), jax.ShapeDtypeStruct((T, D), BF16),
                   jax.ShapeDtypeStruct((1, D), F32), jax.ShapeDtypeStruct((NDEV, GW, D), BF16),
                   jax.ShapeDtypeStruct((NDEV, ple_dim, GW), BF16)],
        scratch_shapes=[pltpu.VMEM((D, D), F32), pltpu.VMEM((ple_dim, D), F32)],
        compiler_params=_cp("arbitrary"),
    )(dx4, sv["gl"], sv["pe"], sv["x3"], sv["h3"], sv["p"], g_ple, w_pleg8.reshape(D, D))


def _layer_fwd(x, h1, p_bf, gw, sp, g_next):
    proj, = _mm(h1, gw["w_in"], mode="out", name="f_proj", outs=[BF16], tm=T)
    y, ca = _mixer_fwd(proj, sp)
    z, merged = _merge_fwd(y, proj, gw["w_branch"])
    x2, h2 = _mm(merged, gw["w_out"].reshape(1, D, D), mode="acc", name="f_out", outs=[F32, BF16], tm=T // 2,
                 tiles=[x], params=[sp["g_mlp"]], epi=_epi_res_norm)
    up, = _mm(h2, gw["w_up"], mode="out", name="f_up", outs=[BF16], tm=T)
    x3, h3 = _mm(up, gw["w_down"].reshape(1, 4 * D, D), mode="acc", name="f_down", outs=[F32, BF16], tm=T // 4,
                 tiles=[x2], params=[sp["g_ple"]], epi=_epi_res_norm, a_pre=_relu2_bf16)
    x4, gl, hn, pe = _mm(h3, gw["w_pleg"].reshape(1, D, D), mode="acc", name="f_gate", tm=T // 2,
                         outs=[F32, BF16, BF16, BF16], tiles=[x3, p_bf], params=[g_next, gw["w_ple"]], epi=_epi_ple)
    saved = dict(x=x, h1=h1, proj=proj, y=y, ca=ca, z=z, merged=merged, x2=x2, h2=h2, up=up, x3=x3, h3=h3,
                 pe=pe, gl=gl, p=p_bf)
    return x4, hn, saved


def _layer_bwd(dx4, sv, gw, sp, submit, early_group=False):
    dw = {}
    dx3, dx3b, dg_ple, dw["w_pleg"], dw["w_ple"] = _ple_bwd(dx4, sv, gw["w_pleg"], sp["g_ple"])
    dup, = _mm(dx3b, gw["w_down"], mode="out", trans_b=True, name="b_dact", outs=[BF16], tm=T,
               tiles=[sv["up"]], epi=_epi_dup)
    dw["w_down"] = _mm_tn(sv["up"], dx3b, nj=NDEV, split="row", name="b_dw_down", a_pre=_relu2_bf16)
    dw["w_up"] = _mm_tn(sv["h2"], dup, nj=NDEV, split="col", name="b_dw_up")
    if early_group:
        dw["w_up"], dup = lax.optimization_barrier((dw["w_up"], dup))
        dup = submit(dw, ("w_up", "w_down", "w_ple", "w_pleg"), dup)
    dx2, dx2b, dg_mlp = _mm(dup, gw["w_up"], mode="full", trans_b=True, name="b_dh2", tm=T // 4,
                            outs=[F32, BF16], tiles=[sv["x2"], dx3], params=[sp["g_mlp"]], epi=_epi_rms_bwd, reds=[D])
    dm, = _mm(dx2b, gw["w_out"].reshape(1, D, D), mode="acc", trans_b=True, name="b_dmerged", outs=[BF16],
              tm=T // 2)
    dw["w_out"] = _mm_tn(sv["merged"], dx2b, nj=NDEV, split="row", name="b_dw_out")
    dproj, dy, dw["w_branch"] = _merge_bwd(dm, sv["z"], sv["proj"], sv["y"], gw["w_branch"])
    dy = submit(dw, ("w_branch", "w_out") if early_group else BIG[1:], dy)
    dproj, dcw, dsc, vec, dpw, dws, dbs = _mixer_bwd(sv["proj"], sv["ca"], dy, dproj, sp)
    dw["w_in"] = _mm_tn(sv["h1"], dproj, nj=NDEV, split="col", name="b_dw_in")
    dw["w_in"], dproj = lax.optimization_barrier((dw["w_in"], dproj))
    dproj = submit(dw, BIG[:1], dproj)
    dx, dg_mix = _mm(dproj, gw["w_in"], mode="full", trans_b=True, name="b_dh1", outs=[F32], tm=T // 4,
                     tiles=[sv["x"], dx2], params=[sp["g_mix"]], epi=_epi_rms_bwd, reds=[D])
    small = dict(norm_mix=dg_mix[0], conf_dw=dcw[:CONF_K], conf_dw_b=vec[0], conf_ln_g=vec[1], conf_ln_b=vec[2],
                 pool_w=dpw, pool_scale=vec[3], sc_conv=dsc[:SC_K], gmlp_ln_g=vec[4], gmlp_ln_b=vec[5],
                 gmlp_ws=dws, gmlp_bs=dbs[:, :, 0], norm_mlp=dg_mlp[0], norm_ple=dg_ple[0])
    return dx, small


ANY = pl.BlockSpec(memory_space=pl.ANY)


def _mesh_pos():
    return lax.axis_index("x"), lax.axis_index("y"), lax.axis_index("c")


def _other_chips(x, y):
    return [(1 - x, y), (x, 1 - y), (1 - x, 1 - y)]


def _launch_comm(body, peers_of, operands, out_shapes, sems, name, seq_id):
    n_in, n_out = len(operands), len(out_shapes)
    if seq_id is None:
        return pl.pallas_call(body, name=name, in_specs=[ANY] * n_in, out_specs=[ANY] * n_out,
                              out_shape=out_shapes, scratch_shapes=sems)(*operands)

    def seq_body(*refs):
        peers = peers_of(*_mesh_pos())
        barrier = pltpu.get_barrier_semaphore()
        for peer in peers:
            pl.semaphore_signal(barrier, inc=1, device_id=peer, device_id_type=MESH)
        pl.semaphore_wait(barrier, len(peers))
        body(*refs)

    return pl.kernel(seq_body, name=name, out_type=out_shapes,
                     mesh=plsc.ScalarSubcoreMesh(axis_name="seq", num_cores=1), scratch_types=sems,
                     compiler_params=pltpu.CompilerParams(collective_id=seq_id))(*operands)


def _all_gather(shards, name, seq_id=None):
    n = len(shards)

    def body(*refs):
        s_refs, o_refs = refs[:n], refs[n:2 * n]
        send_sems, recv_sems, local_sems = refs[2 * n:]
        x, y, c = _mesh_pos()
        me = 4 * x + 2 * y + c
        here = (x, y, c)
        sibling = (x, y, 1 - c)
        chips = _other_chips(x, y)

        def slot(px, py, pc):
            return 4 * px + 2 * py + pc

        def copy(t, k, slot_idx, to, src=None):
            dst = o_refs[t].at[slot_idx]
            return pltpu.make_async_remote_copy(
                src_ref=dst if src is None else src, dst_ref=dst,
                send_sem=send_sems.at[t * 7 + k], recv_sem=recv_sems.at[t * 7 + k],
                device_id=to, device_id_type=MESH)

        mine = [pltpu.make_async_copy(s_refs[t], o_refs[t].at[me], local_sems.at[t]) for t in range(n)]
        for cp in mine:
            cp.start()
        first = []
        for t in range(n):
            for j, chip in enumerate(chips):
                first.append(copy(t, 1 + j, me, (*chip, c), src=s_refs[t]))
        for t in range(n):
            first.append(copy(t, 0, me, sibling, src=s_refs[t]))
        for cp in first:
            cp.start()
        passed = []
        for t in range(n):
            for j, chip in enumerate(chips):
                copy(t, 1 + j, slot(*chip, c), here).wait_recv()
                fwd = copy(t, 4 + j, slot(*chip, c), sibling)
                fwd.start()
                passed.append(fwd)
        for t in range(n):
            copy(t, 0, slot(x, y, 1 - c), here).wait_recv()
            for j, chip in enumerate(chips):
                copy(t, 4 + j, slot(*chip, 1 - c), here).wait_recv()
        for cp in first + passed:
            cp.wait_send()
        for cp in mine:
            cp.wait()

    def peers_of(x, y, c):
        return [(x, y, 1 - c)] + [(*chip, c) for chip in _other_chips(x, y)]

    return _launch_comm(
        body, peers_of, shards, [jax.ShapeDtypeStruct((NDEV,) + s.shape, s.dtype) for s in shards],
        [pltpu.SemaphoreType.DMA((7 * n,)), pltpu.SemaphoreType.DMA((7 * n,)), pltpu.SemaphoreType.DMA((n,))],
        name, seq_id)


def _rs_exchange(p4s, qs, name, seq_id=None):
    n_p, n_q = len(p4s), len(qs)

    def body(*refs):
        p_refs, q_refs = refs[:n_p], refs[n_p:n_p + n_q]
        rb_refs, rc_refs = refs[n_p + n_q:2 * n_p + n_q], refs[2 * n_p + n_q:2 * (n_p + n_q)]
        pair_send, pair_recv, chip_send, chip_recv, local_sems = refs[2 * (n_p + n_q):]
        x, y, c = _mesh_pos()
        a_idx = 2 * x + y
        chips = _other_chips(x, y)
        mine = [pltpu.make_async_copy(q_refs[t].at[a_idx], rc_refs[t].at[a_idx], local_sems.at[t])
                for t in range(n_q)]
        sends = []
        for t in range(n_q):
            for j, chip in enumerate(chips):
                sends.append(pltpu.make_async_remote_copy(
                    src_ref=q_refs[t].at[2 * chip[0] + chip[1]], dst_ref=rc_refs[t].at[a_idx],
                    send_sem=chip_send.at[t * 3 + j], recv_sem=chip_recv.at[t * 3 + j],
                    device_id=(*chip, c), device_id_type=MESH))
        pairs = [pltpu.make_async_remote_copy(
            src_ref=p_refs[t].at[:, 1 - c], dst_ref=rb_refs[t], send_sem=pair_send.at[t], recv_sem=pair_recv.at[t],
            device_id=(x, y, 1 - c), device_id_type=MESH) for t in range(n_p)]
        for cp in sends + mine + pairs:
            cp.start()
        for cp in pairs:
            cp.wait()
        for t in range(n_q):
            for j, chip in enumerate(chips):
                landed = rc_refs[t].at[2 * chip[0] + chip[1]]
                pltpu.make_async_remote_copy(
                    src_ref=landed, dst_ref=landed, send_sem=chip_send.at[t * 3 + j],
                    recv_sem=chip_recv.at[t * 3 + j], device_id=(x, y, c), device_id_type=MESH).wait_recv()
        for cp in sends:
            cp.wait_send()
        for cp in mine:
            cp.wait()

    def peers_of(x, y, c):
        peers = [(x, y, 1 - c)] if n_p else []
        return peers + ([(*chip, c) for chip in _other_chips(x, y)] if n_q else [])

    out_shapes = [jax.ShapeDtypeStruct((NCHIP,) + p.shape[2:], p.dtype) for p in p4s]
    out_shapes += [jax.ShapeDtypeStruct(q.shape, q.dtype) for q in qs]
    sems = [pltpu.SemaphoreType.DMA((max(n_p, 1),)), pltpu.SemaphoreType.DMA((max(n_p, 1),)),
            pltpu.SemaphoreType.DMA((max(3 * n_q, 1),)), pltpu.SemaphoreType.DMA((max(3 * n_q, 1),)),
            pltpu.SemaphoreType.DMA((max(n_q, 1),))]
    got = _launch_comm(body, peers_of, list(p4s) + list(qs), out_shapes, sems, name, seq_id)
    return got[:n_p], got[n_p:]


def _pair_sum(p4s, rbs, c_idx, name, nst=4):
    n = len(p4s)
    trs = [p.shape[2] // nst for p in p4s]

    def body(c_ref, *refs):
        del c_ref
        p_refs, r_refs, o_refs = refs[:n], refs[n:2 * n], refs[2 * n:]
        for p_ref, r_ref, o_ref in zip(p_refs, r_refs, o_refs):
            o_ref[...] = (p_ref[...].astype(F32) + r_ref[...].astype(F32)).astype(o_ref.dtype)

    in_specs = [pl.BlockSpec((None, None, tr, p.shape[3]), lambda b, i, c_ref: (b, c_ref[0], i, 0))
                for p, tr in zip(p4s, trs)]
    in_specs += [pl.BlockSpec((None, tr, p.shape[3]), lambda b, i, c_ref: (b, i, 0)) for p, tr in zip(p4s, trs)]
    out_specs = [pl.BlockSpec((None, tr, p.shape[3]), lambda b, i, c_ref: (b, i, 0)) for p, tr in zip(p4s, trs)]
    return pl.pallas_call(
        body, name=name,
        grid_spec=pltpu.PrefetchScalarGridSpec(num_scalar_prefetch=1, grid=(NCHIP, nst), in_specs=in_specs,
                                               out_specs=out_specs),
        out_shape=[jax.ShapeDtypeStruct((NCHIP,) + p.shape[2:], p.dtype) for p in p4s],
        compiler_params=_cp("arbitrary", "arbitrary"),
    )(c_idx, *p4s, *rbs)


class _GradientPipeline:
    def __init__(self, c_idx, results):
        self.c_idx, self.results, self.pending = c_idx, results, None

    def _sum_pending(self, chain):
        names, layer, p4s, rbs = self.pending
        qs = _pair_sum(p4s, rbs, self.c_idx, name="rs_pairsum_%d" % len(names))
        return lax.optimization_barrier((chain, qs))

    def submit(self, dw, names, layer, chain):
        qs, tag, seq_id = [], "pair", 3
        if self.pending is not None:
            chain, qs = self._sum_pending(chain)
            tag, seq_id = "pair_chip", 4
        p4s = [dw[n].reshape((NCHIP, 2) + BIG_SHARD[n]) for n in names]
        rbs, rcs = _rs_exchange(p4s, qs, name="rs_%s_%d" % (tag, len(names)), seq_id=seq_id)
        self._record(rcs)
        self.pending = (names, layer, p4s, rbs)
        return chain

    def finish(self, chain):
        chain, qs = self._sum_pending(chain)
        self._record(_rs_exchange([], qs, name="rs_chip_last", seq_id=5)[1])
        self.pending = None
        return chain

    def _record(self, rcs):
        if rcs:
            names, layer = self.pending[:2]
            for n, rc in zip(names, rcs):
                self.results[n][layer] = rc


def _adamw(w, g, m, v):
    m = ADAM_B1 * m + (1.0 - ADAM_B1) * g
    v = ADAM_B2 * v + (1.0 - ADAM_B2) * (g * g)
    m_hat = m / (1.0 - ADAM_B1 ** ADAM_STEP)
    v_hat = v / (1.0 - ADAM_B2 ** ADAM_STEP)
    delta = -ADAM_LR * (m_hat / (jnp.sqrt(v_hat) + ADAM_EPS) + ADAM_WD * w)
    return delta, m, v


def _adam_sharded(rcs, w, m, v, tr, name, first_layer, partial=None):
    _, r, c = w.shape
    nst = r // tr
    n_l = len(rcs)

    def body(*refs):
        rc_refs = refs[:n_l]
        w_ref, m_ref, v_ref = refs[n_l:n_l + 3]
        g_out, d_out, m_out, v_out = refs[-4:]
        layer = pl.program_id(0)
        for k, rc in enumerate(rc_refs):
            @pl.when(layer == k)
            def _():
                g = rc[0].astype(F32) + rc[1].astype(F32) + rc[2].astype(F32) + rc[3].astype(F32)
                delta, m_new, v_new = _adamw(w_ref[...], g, m_ref[...], v_ref[...])
                g_out[...] = g
                d_out[...] = delta
                m_out[...] = m_new
                v_out[...] = v_new

    rc_specs = [pl.BlockSpec((NCHIP, tr, c), lambda l, i, k=k: (0, jnp.where(l == k, i, 0), 0)) for k in range(n_l)]
    wspec = pl.BlockSpec((None, tr, c), lambda l, i: (first_layer + l, i, 0))
    carried = [] if partial is None else list(partial)
    return pl.pallas_call(
        body, name=name, grid=(n_l, nst),
        in_specs=rc_specs + [wspec] * 3 + [pl.BlockSpec(memory_space=pl.ANY)] * len(carried),
        out_specs=[wspec] * 4, out_shape=[jax.ShapeDtypeStruct(w.shape, F32)] * 4,
        input_output_aliases={n_l + 3 + k: k for k in range(len(carried))},
        compiler_params=_cp("arbitrary", "arbitrary"),
    )(*rcs, w, m, v, *carried)


def _adam_packed(g, w, m, v, direct):
    n_d = len(direct)

    def pieces(shape):
        width = shape[-1]
        count = 1
        for s in shape[:-1]:
            count *= s
        per_row = D // width
        out = []
        for k in range(count):
            idx = (k,) if len(shape) == 2 else (k // shape[1], k % shape[1])
            out.append((idx, k // per_row, (k % per_row) * width, width))
        return out

    def body(g_ref, w_ref, m_ref, v_ref, d_out, m_out, v_out, *outs):
        delta, m_new, v_new = _adamw(w_ref[...], g_ref[...], m_ref[...], v_ref[...])
        d_out[...] = delta
        m_out[...] = m_new
        v_out[...] = v_new
        for a, (_, row0, shape) in enumerate(direct):
            for src, dst in zip((g_ref, d_out, m_out, v_out), outs[4 * a:4 * a + 4]):
                for idx, row, lane0, width in pieces(shape):
                    piece = src[pl.ds(row0 + row, 1), lane0:lane0 + width]
                    if len(idx) == 1:
                        dst[pl.ds(idx[0], 1), :] = piece
                    else:
                        dst[idx[0], pl.ds(idx[1], 1), :] = piece

    out_shape = [jax.ShapeDtypeStruct(g.shape, F32)] * 3
    for _, _, shape in direct:
        out_shape += [jax.ShapeDtypeStruct(shape, F32)] * 4
    res = pl.pallas_call(body, name="adam_small", out_shape=out_shape,
                         compiler_params=pltpu.CompilerParams(vmem_limit_bytes=VMEM_LIMIT_BYTES))(g, w, m, v)
    return res[:3], {name: res[3 + 4 * a:7 + 4 * a] for a, (name, _, _) in enumerate(direct)}


def _sum4(rc):
    def body(rc_ref, o_ref):
        o_ref[...] = rc_ref[0] + rc_ref[1] + rc_ref[2] + rc_ref[3]

    return pl.pallas_call(
        body, name="small_sum", out_shape=jax.ShapeDtypeStruct(rc.shape[1:], F32),
    )(rc)


BIG = ("w_in", "w_branch", "w_out", "w_up", "w_down", "w_ple", "w_pleg")
BIG_SHARD = {"w_in": (D, D), "w_branch": (4 * W, GW), "w_out": (GW, D), "w_up": (D, W), "w_down": (W, D),
             "w_ple": (256, GW), "w_pleg": (GW, D)}
ADAM_ROWS = {"w_in": 256, "w_branch": 512, "w_out": 128, "w_up": 256, "w_down": 256, "w_ple": 256, "w_pleg": 128}
SMALL = (("norm_mix", (DEPTH, D)), ("conf_dw", (DEPTH, CONF_K, W)), ("conf_dw_b", (DEPTH, W)),
         ("conf_ln_g", (DEPTH, W)), ("conf_ln_b", (DEPTH, W)), ("pool_w", (DEPTH, 4, GW, GW)),
         ("pool_scale", (DEPTH, W)), ("sc_conv", (DEPTH, SC_K, W)), ("gmlp_ln_g", (DEPTH, W)),
         ("gmlp_ln_b", (DEPTH, W)), ("gmlp_ws", (DEPTH, 4, GW, GW)), ("gmlp_bs", (DEPTH, 4, GW)),
         ("norm_mlp", (DEPTH, D)), ("norm_ple", (DEPTH, D)), ("norm_final", (D,)))
CHANNEL_SHARDED = ("conf_dw", "sc_conv")
SMALL_ROWS = 80


def _pack(arrs, rows):
    flat = jnp.concatenate([a.reshape(-1) for a in arrs])
    return jnp.pad(flat, (0, rows * D - flat.shape[0])).reshape(rows, D)


def _unpack(packed, shapes):
    flat = packed.reshape(-1)
    out, off = [], 0
    for shp in shapes:
        size = 1
        for s in shp:
            size *= s
        out.append(flat[off:off + size].reshape(shp))
        off += size
    return out


def kernel(x, p, norm_mix, w_in, conf_dw, conf_dw_b, conf_ln_g, conf_ln_b, pool_w, pool_scale, sc_conv, gmlp_ln_g, gmlp_ln_b, gmlp_ws, gmlp_bs, w_branch, w_out, norm_mlp, w_up, w_down, norm_ple, w_ple, w_ple_gate, norm_final, loss_target, m_norm_mix, m_w_in, m_conf_dw, m_conf_dw_b, m_conf_ln_g, m_conf_ln_b, m_pool_w, m_pool_scale, m_sc_conv, m_gmlp_ln_g, m_gmlp_ln_b, m_gmlp_ws, m_gmlp_bs, m_w_branch, m_w_out, m_norm_mlp, m_w_up, m_w_down, m_norm_ple, m_w_ple, m_w_ple_gate, m_norm_final, v_norm_mix, v_w_in, v_conf_dw, v_conf_dw_b, v_conf_ln_g, v_conf_ln_b, v_pool_w, v_pool_scale, v_sc_conv, v_gmlp_ln_g, v_gmlp_ln_b, v_gmlp_ws, v_gmlp_bs, v_w_branch, v_w_out, v_norm_mlp, v_w_up, v_w_down, v_norm_ple, v_w_ple, v_w_ple_gate, v_norm_final):
    weights = dict(norm_mix=norm_mix, w_in=w_in, conf_dw=conf_dw, conf_dw_b=conf_dw_b, conf_ln_g=conf_ln_g,
                   conf_ln_b=conf_ln_b, pool_w=pool_w, pool_scale=pool_scale, sc_conv=sc_conv, gmlp_ln_g=gmlp_ln_g,
                   gmlp_ln_b=gmlp_ln_b, gmlp_ws=gmlp_ws, gmlp_bs=gmlp_bs, w_branch=w_branch, w_out=w_out,
                   norm_mlp=norm_mlp, w_up=w_up, w_down=w_down, norm_ple=norm_ple, w_ple=w_ple, w_pleg=w_ple_gate,
                   norm_final=norm_final)
    mom1 = dict(norm_mix=m_norm_mix, w_in=m_w_in, conf_dw=m_conf_dw, conf_dw_b=m_conf_dw_b, conf_ln_g=m_conf_ln_g,
                conf_ln_b=m_conf_ln_b, pool_w=m_pool_w, pool_scale=m_pool_scale, sc_conv=m_sc_conv,
                gmlp_ln_g=m_gmlp_ln_g, gmlp_ln_b=m_gmlp_ln_b, gmlp_ws=m_gmlp_ws, gmlp_bs=m_gmlp_bs,
                w_branch=m_w_branch, w_out=m_w_out, norm_mlp=m_norm_mlp, w_up=m_w_up, w_down=m_w_down,
                norm_ple=m_norm_ple, w_ple=m_w_ple, w_pleg=m_w_ple_gate, norm_final=m_norm_final)
    mom2 = dict(norm_mix=v_norm_mix, w_in=v_w_in, conf_dw=v_conf_dw, conf_dw_b=v_conf_dw_b, conf_ln_g=v_conf_ln_g,
                conf_ln_b=v_conf_ln_b, pool_w=v_pool_w, pool_scale=v_pool_scale, sc_conv=v_sc_conv,
                gmlp_ln_g=v_gmlp_ln_g, gmlp_ln_b=v_gmlp_ln_b, gmlp_ws=v_gmlp_ws, gmlp_bs=v_gmlp_bs,
                w_branch=v_w_branch, w_out=v_w_out, norm_mlp=v_norm_mlp, w_up=v_w_up, w_down=v_w_down,
                norm_ple=v_norm_ple, w_ple=v_w_ple, w_pleg=v_w_ple_gate, norm_final=v_norm_final)

    xi, yi, ci = _mesh_pos()
    me = 4 * xi + 2 * yi + ci
    c_idx = jnp.reshape(ci, (1,)).astype(jnp.int32)

    gathered, conf_full, sc_full = [], [], []
    for l in range(DEPTH):
        shard = lambda n: weights[n][l].astype(BF16).reshape(BIG_SHARD[n])
        w_in_g, conf_g, sc_g = _all_gather([shard("w_in"), conf_dw[l], sc_conv[l]], name="ag_first", seq_id=1)
        if l + 1 < DEPTH:
            rest = _all_gather([shard(n) for n in BIG[1:]], name="ag_rest", seq_id=2)
        else:
            rest = (list(_all_gather([shard(n) for n in BIG[1:4]], name="ag_rest_a", seq_id=2))
                    + list(_all_gather([shard(n) for n in BIG[4:]], name="ag_rest_b", seq_id=2)))
        gw = dict(zip(BIG[1:], rest), w_in=w_in_g)
        gw["w_branch"] = gw["w_branch"].reshape(NDEV, 4, W, GW)
        gathered.append(gw)
        conf_full.append(conf_g)
        sc_full.append(sc_g)

    def small_params(l):
        return dict(cw=conf_full[l], cb=conf_dw_b[l][None], lg=conf_ln_g[l][None], lb=conf_ln_b[l][None],
                    pw=pool_w[l], ps=pool_scale[l][None], sc=sc_full[l], gg=gmlp_ln_g[l][None],
                    gb=gmlp_ln_b[l][None], ws=gmlp_ws[l], bst=gmlp_bs[l].T, g_mix=norm_mix[l][None],
                    g_mlp=norm_mlp[l][None], g_ple=norm_ple[l][None])

    xc = x.reshape(T, D)
    small_names = [n for n, _ in SMALL]

    def in_gradient_layout(n, shard, shape):
        if n not in CHANNEL_SHARDED:
            return shard
        return lax.dynamic_update_slice(jnp.zeros(shape, F32), shard, (0, 0, me * (W // NDEV)))

    small_state = [_pack([in_gradient_layout(n, src[n], shape) for n, shape in SMALL], NDEV * SMALL_ROWS)
                   for src in (weights, mom1, mom2)]
    xc, small_state = lax.optimization_barrier((xc, small_state))
    p_bf = p.reshape(DEPTH, T, 256).astype(BF16)
    h = _norm_first(xc, norm_mix[0][None])
    saved = []
    for l in range(DEPTH):
        g_next = norm_mix[l + 1][None] if l + 1 < DEPTH else norm_final[None]
        h, conf_g, sc_g = lax.optimization_barrier((h, conf_full[l], sc_full[l]))
        conf_full[l] = conf_g.transpose(1, 0, 2).reshape(CONF_K, W)
        sc_full[l] = sc_g.transpose(1, 0, 2).reshape(SC_K, W)
        xc, h, sv = _layer_fwd(xc, h, p_bf[l], gathered[l], small_params(l), g_next)
        saved.append(sv)

    dxc, dg_final, loss_part = _loss_head(xc, loss_target.reshape(T, D), norm_final[None])
    loss = lax.psum(loss_part[0, 0], ("x", "y", "c"))
    small_grads = [None] * DEPTH
    rcs = {n: [None] * DEPTH for n in BIG}
    pipeline = _GradientPipeline(c_idx, rcs)
    for l in reversed(range(DEPTH)):
        dxc, small_grads[l] = _layer_bwd(dxc, saved[l], gathered[l], small_params(l),
                                         lambda dw, names, value, l=l: pipeline.submit(dw, names, l, value),
                                         early_group=(l == 0))

    def adam_sharded(first_layer, n_layers, partial, tag, names=BIG):
        outs = {}
        for n in names:
            shp = (DEPTH,) + BIG_SHARD[n]
            outs[n] = _adam_sharded(rcs[n][first_layer:first_layer + n_layers], weights[n].reshape(shp),
                                    mom1[n].reshape(shp), mom2[n].reshape(shp), ADAM_ROWS[n],
                                    "adam_%s_%s" % (n, tag), first_layer, None if partial is None else partial[n])
        return outs

    stacked = {n: jnp.stack([small_grads[l][n] for l in range(DEPTH)]) for n, _ in SMALL if n != "norm_final"}
    stacked["norm_final"] = dg_final[0]
    packed = _pack([stacked[n] for n, _ in SMALL], NDEV * SMALL_ROWS).reshape(NCHIP, 2, SMALL_ROWS, D)
    (pair_small,), _ = _rs_exchange([packed], [], name="rs_pair_small")
    q_small = _pair_sum([packed], [pair_small], c_idx, name="rs_pairsum_small", nst=1)
    dxc, upper, q_small = lax.optimization_barrier((dxc, {n: rcs[n][1:] for n in BIG}, q_small))
    _, (chips_small,) = _rs_exchange([], q_small, name="rs_chip_small", seq_id=6)
    dxc, upper = pipeline.finish((dxc, upper))
    for n in BIG:
        rcs[n][1:] = upper[n]
    partial = adam_sharded(1, DEPTH - 1, None, "upper")
    last = adam_sharded(0, 1, partial, "last", names=BIG[:1])
    last, partial, chips_small = lax.optimization_barrier((last, partial, chips_small))
    reduced_slot = _sum4(chips_small)
    reduced = _all_gather([reduced_slot], name="ag_small", seq_id=7)[0]
    small_full = dict(zip([n for n, _ in SMALL], _unpack(reduced, [s for _, s in SMALL])))
    grads, deltas, new_m, new_v = {}, {}, {}, {}
    direct, row = [], 0
    for n, shape in SMALL:
        if len(shape) == 1 or shape[-2] == DEPTH:
            direct.append((n, row, (1,) * (2 - len(shape)) + tuple(shape)))
        size = 1
        for s in shape:
            size *= s
        row += size // D
    (d_p, m_p, v_p), own_shape = _adam_packed(reduced.reshape(NDEV * SMALL_ROWS, D), *small_state, direct)
    small_shapes = [s for _, s in SMALL]

    def own_channels(n, full):
        return lax.dynamic_slice_in_dim(full, me * (W // NDEV), W // NDEV, axis=2) if n in CHANNEL_SHARDED else full

    for n, d_, m_, v_ in zip(small_names, _unpack(d_p, small_shapes), _unpack(m_p, small_shapes),
                             _unpack(v_p, small_shapes)):
        if n in own_shape:
            grads[n], deltas[n], new_m[n], new_v[n] = [a.reshape(weights[n].shape) for a in own_shape[n]]
        else:
            grads[n], deltas[n], new_m[n], new_v[n] = (own_channels(n, small_full[n]), own_channels(n, d_),
                                                       own_channels(n, m_), own_channels(n, v_))

    last.update(adam_sharded(0, 1, partial, "last", names=BIG[1:]))
    for n, (g_, d_, m_, v_) in last.items():
        full = weights[n].shape
        grads[n], deltas[n], new_m[n], new_v[n] = g_.reshape(full), d_.reshape(full), m_.reshape(full), v_.reshape(full)

    order = ("norm_mix", "w_in", "conf_dw", "conf_dw_b", "conf_ln_g", "conf_ln_b", "pool_w", "pool_scale", "sc_conv",
             "gmlp_ln_g", "gmlp_ln_b", "gmlp_ws", "gmlp_bs", "w_branch", "w_out", "norm_mlp", "w_up", "w_down",
             "norm_ple", "w_ple", "w_pleg", "norm_final")
    return (loss, dxc.reshape(1, T, D), *[grads[n] for n in order], *[deltas[n] for n in order],
            *[new_m[n] for n in order], *[new_v[n] for n in order])
```

```python
import functools

import jax
import jax.numpy as jnp
from jax import lax
from jax.experimental import pallas as pl
from jax.experimental.pallas import tpu as pltpu
from jax.experimental.pallas import tpu_sc as plsc

F32 = jnp.float32
BF16 = jnp.bfloat16

DEPTH = 4
T = 2048
D = 1024
W = 512
NDEV = 8
NCHIP = 4
EPS = 1e-6
CONF_K = 31
SC_K = 3
POOL_WINDOWS = (2, 4, 8, 16)
GW = 128
HB = 32
HA = 32
COLS_IN = 8192
MIX_COLS = 4096

ADAM_LR = 0.001
ADAM_B1 = 0.9
ADAM_B2 = 0.999
ADAM_EPS = 1e-08
ADAM_WD = 0.01
ADAM_STEP = 10

VMEM_LIMIT_BYTES = 56 * 1024 * 1024
MESH = pl.DeviceIdType.MESH


def _cp(*sem):
    return pltpu.CompilerParams(dimension_semantics=tuple(sem), vmem_limit_bytes=VMEM_LIMIT_BYTES)


def _sig(x):
    return jax.nn.sigmoid(x)


def _rms(x, g):
    r = lax.rsqrt(jnp.mean(x * x, axis=-1, keepdims=True) + EPS)
    return x * r * g


def _rms_bwd(dh, x, g, dres):
    r = lax.rsqrt(jnp.mean(x * x, axis=-1, keepdims=True) + EPS)
    xh = x * r
    u = dh * g
    dx = r * (u - xh * jnp.mean(u * xh, axis=-1, keepdims=True)) + dres
    dg = jnp.sum(dh * xh, axis=0, keepdims=True)
    return dx, dg


def _ln_stats(x):
    mu = jnp.mean(x, axis=-1, keepdims=True)
    xc = x - mu
    rstd = lax.rsqrt(jnp.mean(xc * xc, axis=-1, keepdims=True) + EPS)
    return xc * rstd, rstd


def _ln_bwd(dxh, xh, rstd):
    return rstd * (dxh - jnp.mean(dxh, axis=-1, keepdims=True) - xh * jnp.mean(dxh * xh, axis=-1, keepdims=True))


def _rowsum(x):
    return jnp.sum(x, axis=0, keepdims=True)


EPI_ROWS = 256


def _relu2_bf16(up):
    r = jnp.maximum(up.astype(F32), 0.0)
    return (r * r).astype(BF16)


def _mm(a, b3, *, mode, name, outs, trans_b=False, tm=512, tiles=(), params=(), epi=None, reds=(), a_pre=None):
    t_, ka = a.shape
    nj, r, c = b3.shape
    kb, nb = (c, r) if trans_b else (r, c)
    nt = t_ // tm
    out_mode = mode == "out"
    full = mode == "full"
    assert trans_b or not full
    if out_mode:
        assert ka == kb and not reds
        grid = (nj, nt)
        a_map = lambda g0, g1: (g1, 0)
        b_map = lambda g0, g1: (g0, 0, 0)
        t_map = lambda g0, g1: (g1, g0)
        width = nj * nb
    else:
        assert ka == nj * kb
        grid = (nt, 1 if full else nj)
        a_map = lambda g0, g1: (g0, g1)
        b_map = lambda g0, g1: (g1, 0, 0)
        t_map = lambda g0, g1: (g0, 0)
        width = nb
    n_t, n_p, n_o, n_r = len(tiles), len(params), len(outs), len(reds)
    use_acc = (not out_mode) and nj > 1 and not full
    dims = (((1,), (1,)), ((), ())) if trans_b else (((1,), (0,)), ((), ()))

    def body(a_ref, b_ref, *rest):
        t_refs = rest[:n_t]
        p_refs = rest[n_t:n_t + n_p]
        o_refs = rest[n_t + n_p:n_t + n_p + n_o]
        r_refs = rest[n_t + n_p + n_o:n_t + n_p + n_o + n_r]
        i = pl.program_id(1 if out_mode else 0)
        a_val = a_ref[...] if a_pre is None else a_pre(a_ref[...])
        if full:
            b_all, b_sems = rest[-2], rest[-1]

            @pl.when(i == 0)
            def _():
                cps = [pltpu.make_async_copy(b_ref.at[j], b_all.at[:, j * c:(j + 1) * c], b_sems.at[j])
                       for j in range(nj)]
                for cp in cps:
                    cp.start()
                for cp in cps:
                    cp.wait()

            part = lax.dot_general(a_val, b_all[...], dims, preferred_element_type=F32)
        else:
            part = lax.dot_general(a_val, b_ref[...], dims, preferred_element_type=F32)

        def finish(acc_rows):
            totals = [None] * n_r
            for r0 in range(0, tm, min(tm, EPI_ROWS)):
                rows = slice(r0, r0 + min(tm, EPI_ROWS))
                if epi is None:
                    res, rr = (acc_rows(rows),), ()
                else:
                    res, rr = epi(acc_rows(rows), [t[rows, :] for t in t_refs], [p[...] for p in p_refs])
                for o_ref, val in zip(o_refs, res):
                    o_ref[rows, :] = val.astype(o_ref.dtype)
                totals = [val if tot is None else tot + val for tot, val in zip(totals, rr)]
            for r_ref, val in zip(r_refs, totals):
                @pl.when(i == 0)
                def _():
                    r_ref[...] = val

                @pl.when(i > 0)
                def _():
                    r_ref[...] += val

        if use_acc:
            acc_ref = rest[-1]
            j = pl.program_id(1)

            @pl.when(j == 0)
            def _():
                acc_ref[...] = part

            @pl.when(jnp.logical_and(j > 0, j < nj - 1))
            def _():
                acc_ref[...] += part

            @pl.when(j == nj - 1)
            def _():
                finish(lambda rows: acc_ref[rows, :] + part[rows])
        else:
            finish(lambda rows: part[rows, :])

    const2 = lambda g0, g1: (0, 0)
    if full:
        in_specs = [pl.BlockSpec((tm, ka), a_map), pl.BlockSpec(memory_space=pl.ANY)]
        scratch = [pltpu.VMEM((r, nj * c), b3.dtype), pltpu.SemaphoreType.DMA((nj,))]
    else:
        in_specs = [pl.BlockSpec((tm, kb), a_map), pl.BlockSpec((None, r, c), b_map)]
        scratch = [pltpu.VMEM((tm, nb), F32)] if use_acc else []
    in_specs += [pl.BlockSpec((tm, t.shape[1] // nj if out_mode else t.shape[1]), t_map) for t in tiles]
    in_specs += [pl.BlockSpec(p.shape, lambda g0, g1, nd=p.ndim: (0,) * nd) for p in params]
    out_specs = [pl.BlockSpec((tm, nb), t_map) for _ in outs] + [pl.BlockSpec((1, w), const2) for w in reds]
    out_shape = [jax.ShapeDtypeStruct((t_, width), dt) for dt in outs]
    out_shape += [jax.ShapeDtypeStruct((1, w), F32) for w in reds]
    res = pl.pallas_call(
        body, name=name, grid=grid, in_specs=in_specs, out_specs=out_specs, out_shape=out_shape,
        scratch_shapes=scratch, compiler_params=_cp("arbitrary", "arbitrary"),
    )(a, b3, *tiles, *params)
    return res


def _mm_tn(a, g, *, nj, split, name, out_dtype=BF16, a_pre=None):
    t_ = a.shape[0]
    if split == "col":
        r, c = a.shape[1], g.shape[1] // nj
        a_spec = pl.BlockSpec((t_, r), lambda j: (0, 0))
        g_spec = pl.BlockSpec((t_, c), lambda j: (0, j))
    else:
        r, c = a.shape[1] // nj, g.shape[1]
        a_spec = pl.BlockSpec((t_, r), lambda j: (0, j))
        g_spec = pl.BlockSpec((t_, c), lambda j: (0, 0))

    def body(a_ref, g_ref, o_ref):
        a_val = a_ref[...] if a_pre is None else a_pre(a_ref[...])
        o_ref[...] = lax.dot_general(a_val, g_ref[...], (((0,), (0,)), ((), ())),
                                     preferred_element_type=F32).astype(o_ref.dtype)

    return pl.pallas_call(
        body, name=name, grid=(nj,), in_specs=[a_spec, g_spec],
        out_specs=pl.BlockSpec((None, r, c), lambda j: (j, 0, 0)),
        out_shape=jax.ShapeDtypeStruct((nj, r, c), out_dtype),
        compiler_params=_cp("arbitrary"),
    )(a, g)


def _epi_res_norm(acc, tiles, params):
    x_new = tiles[0] + acc
    return (x_new, _rms(x_new, params[0])), ()


def _epi_ple(acc, tiles, params):
    x_old, p_tile = tiles
    g_next, w_ple8 = params
    pe = jnp.concatenate([jnp.dot(p_tile, w_ple8[j], preferred_element_type=F32) for j in range(NDEV)], axis=1)
    x_new = x_old + pe * _sig(acc)
    return (x_new, acc, _rms(x_new, g_next), pe), ()


def _epi_rms_bwd(acc, tiles, params):
    dx, dg = _rms_bwd(acc, tiles[0], params[0], tiles[1])
    return (dx, dx), (dg,)


def _epi_dup(acc, tiles, params):
    return (acc * (2.0 * jnp.maximum(tiles[0].astype(F32), 0.0)),), ()


def _tri_mask():
    row = lax.broadcasted_iota(jnp.int32, (GW, GW), 0)
    col = lax.broadcasted_iota(jnp.int32, (GW, GW), 1)
    return row >= col


def _small_specs(sp_list):
    return [pl.BlockSpec(p.shape, (lambda i: (0, 0)) if p.ndim == 2 else (lambda i: (0, 0, 0))) for p in sp_list]


SUBLANES = 8


def _tap_sum(src, w_ref, taps, rows, stage):
    groups = {}
    for off, k in taps:
        groups.setdefault(off % SUBLANES, []).append((off - off % SUBLANES, k))
    out = None
    for res, members in sorted(groups.items()):
        n = rows if res == 0 else rows + SUBLANES
        part = None
        for base, k in members:
            term = w_ref[k:k + 1, :] * src[pl.ds(base, n), :]
            part = term if part is None else part + term
        if res:
            stage[0:n, :] = part
            part = stage[pl.ds(res, rows), :]
        out = part if out is None else out + part
    return out


def _tap_grads(grad, src, offsets, rows, stage, out_ref):
    pad = SUBLANES
    stage[0:pad, :] = jnp.zeros((pad, grad.shape[1]), F32)
    stage[pad:pad + rows, :] = grad
    stage[pad + rows:2 * pad + rows, :] = jnp.zeros((pad, grad.shape[1]), F32)
    groups = {}
    for k, off in enumerate(offsets):
        groups.setdefault(off % SUBLANES, []).append((off - off % SUBLANES, k))
    for res, members in sorted(groups.items()):
        shifted = stage[pl.ds(pad - res, rows + pad), :]
        for base, k in members:
            out_ref[k:k + 1, :] += _rowsum(shifted * src[pl.ds(base, rows + pad), :])


def _mixer_params(sp):
    return [sp["cw"], sp["cb"], sp["lg"], sp["lb"], sp["pw"], sp["ps"], sp["sc"], sp["gg"], sp["gb"], sp["ws"], sp["bst"]]


def _mixer_fwd(proj, sp, tm=256):
    nt = T // tm
    per = tm // HB

    conv_taps = [(HB - (CONF_K - 1) + k, k) for k in range(CONF_K)]

    def body(main_ref, halo_ref, cw, cb, lg, lb, pw, ps, sc, gg, gb, ws, bst, y_ref, ca_ref, ext, stage):
        i = pl.program_id(0)
        keep = (i > 0).astype(F32)

        def mcol(c0):
            return main_ref[:, c0:c0 + W].astype(F32)

        def hcol(c0):
            return halo_ref[:, c0:c0 + W].astype(F32)

        ext[0:HB, :] = hcol(0) * _sig(hcol(W)) * keep
        ext[HB:HB + tm, :] = mcol(0) * _sig(mcol(W))
        ca = (_tap_sum(ext, cw, conv_taps, tm, stage) + cb[...]).astype(BF16)
        ca_ref[...] = ca
        xh, _ = _ln_stats(ca.astype(F32))
        n = xh * lg[...] + lb[...]
        y_ref[:, 0:W] = (n * _sig(n)).astype(BF16)

        pin = mcol(1024)
        ext[0:HB, :] = hcol(1024) * keep
        ext[HB:HB + tm, :] = pin
        pos = (i * tm + lax.broadcasted_iota(jnp.int32, (tm, 1), 0) + 1).astype(F32)
        for g, w in enumerate(POOL_WINDOWS):
            lo = g * GW
            s = ext[pl.ds(HB, tm), lo:lo + GW]
            for j in range(1, w):
                s = s + ext[pl.ds(HB - j, tm), lo:lo + GW]
            pooled = s / jnp.minimum(pos, float(w)) - pin[:, lo:lo + GW]
            mixed = jnp.dot(pooled.astype(BF16), pw[g].astype(BF16), preferred_element_type=F32)
            y_ref[:, W + lo:W + lo + GW] = (mixed * ps[:, lo:lo + GW]).astype(BF16)

        ext[0:HB, :] = hcol(2048) * hcol(2560) * keep
        ext[HB:HB + tm, :] = mcol(2048) * mcol(2560)
        cv = sc[0:1, :] * ext[pl.ds(HB - 2, tm), :]
        cv = cv + sc[1:2, :] * ext[pl.ds(HB - 1, tm), :]
        cv = cv + sc[2:3, :] * ext[pl.ds(HB, tm), :]
        y_ref[:, 2 * W:3 * W] = (mcol(1536) * cv).astype(BF16)

        vh, _ = _ln_stats(mcol(3584))
        vn = (vh * gg[...] + gb[...]).astype(BF16)
        u = mcol(3072)
        tri = _tri_mask()
        for g in range(4):
            lo = g * GW
            wm = jnp.where(tri, ws[g], 0.0).astype(BF16)
            for c in range(tm // GW):
                r0 = c * GW
                sg = jnp.dot(wm, vn[r0:r0 + GW, lo:lo + GW], preferred_element_type=F32) + bst[:, g:g + 1]
                y_ref[r0:r0 + GW, 3 * W + lo:3 * W + lo + GW] = (u[r0:r0 + GW, lo:lo + GW] * sg).astype(BF16)

    plist = _mixer_params(sp)
    in_specs = [pl.BlockSpec((tm, MIX_COLS), lambda i: (i, 0)),
                pl.BlockSpec((HB, MIX_COLS), lambda i: (jnp.maximum(i * per - 1, 0), 0))]
    in_specs += _small_specs(plist)
    return pl.pallas_call(
        body, name="f_mixers", grid=(nt,), in_specs=in_specs,
        out_specs=[pl.BlockSpec((tm, 4 * W), lambda i: (i, 0)), pl.BlockSpec((tm, W), lambda i: (i, 0))],
        out_shape=[jax.ShapeDtypeStruct((T, 4 * W), BF16), jax.ShapeDtypeStruct((T, W), BF16)],
        scratch_shapes=[pltpu.VMEM((HB + tm, W), F32), pltpu.VMEM((tm + SUBLANES, W), F32)],
        compiler_params=_cp("arbitrary"),
    )(proj, proj, *plist)


def _assemble_wb(wb8_ref, wbf_ref):
    for k in range(4):
        for j in range(NDEV):
            wbf_ref[k, :, j * GW:(j + 1) * GW] = wb8_ref[j, k]


def _merge_fwd(y, proj, wb8, tm=256):
    nt = T // tm

    def body(y_ref, gate_ref, wb8_ref, z_ref, m_ref, wbf):
        @pl.when(pl.program_id(0) == 0)
        def _():
            _assemble_wb(wb8_ref, wbf)

        m = jnp.zeros((tm, D), F32)
        for k in range(4):
            zk = jnp.dot(y_ref[:, k * W:(k + 1) * W], wbf[k], preferred_element_type=F32)
            z_ref[:, k * D:(k + 1) * D] = zk.astype(BF16)
            m = m + _sig(gate_ref[:, k * D:(k + 1) * D].astype(F32)) * zk
        m_ref[...] = m.astype(BF16)

    return pl.pallas_call(
        body, name="f_merge", grid=(nt,),
        in_specs=[pl.BlockSpec((tm, 4 * W), lambda i: (i, 0)),
                  pl.BlockSpec((tm, 4 * D), lambda i: (i, 1)),
                  pl.BlockSpec(wb8.shape, lambda i: (0, 0, 0, 0))],
        out_specs=[pl.BlockSpec((tm, 4 * D), lambda i: (i, 0)), pl.BlockSpec((tm, D), lambda i: (i, 0))],
        out_shape=[jax.ShapeDtypeStruct((T, 4 * D), BF16), jax.ShapeDtypeStruct((T, D), BF16)],
        scratch_shapes=[pltpu.VMEM((4, W, D), BF16)],
        compiler_params=_cp("arbitrary"),
    )(y, proj, wb8)


def _merge_bwd(dm, z, proj, y, wb8, tm=256):
    nt = T // tm

    def body(dm_ref, z_ref, gate_ref, y_ref, wb8_ref, dp_ref, dy_ref, dwb_ref, wbf, acc):
        i = pl.program_id(0)

        @pl.when(i == 0)
        def _():
            _assemble_wb(wb8_ref, wbf)

        dmv = dm_ref[...].astype(F32)
        for k in range(4):
            s = _sig(gate_ref[:, k * D:(k + 1) * D].astype(F32))
            dzk = (dmv * s).astype(BF16)
            dp_ref[:, k * D:(k + 1) * D] = (dmv * z_ref[:, k * D:(k + 1) * D].astype(F32) * s * (1.0 - s)).astype(BF16)
            dyk = lax.dot_general(dzk, wbf[k], (((1,), (1,)), ((), ())), preferred_element_type=F32)
            dy_ref[:, k * W:(k + 1) * W] = dyk.astype(BF16)
            part = lax.dot_general(y_ref[:, k * W:(k + 1) * W], dzk, (((0,), (0,)), ((), ())),
                                   preferred_element_type=F32)

            @pl.when(i == 0)
            def _():
                acc[k] = part

            @pl.when(i > 0)
            def _():
                acc[k] += part

        @pl.when(i == nt - 1)
        def _():
            for k in range(4):
                for j in range(NDEV):
                    dwb_ref[j, k] = acc[k, :, j * GW:(j + 1) * GW].astype(BF16)

    return pl.pallas_call(
        body, name="b_merge", grid=(nt,),
        in_specs=[pl.BlockSpec((tm, D), lambda i: (i, 0)),
                  pl.BlockSpec((tm, 4 * D), lambda i: (i, 0)),
                  pl.BlockSpec((tm, 4 * D), lambda i: (i, 1)),
                  pl.BlockSpec((tm, 4 * W), lambda i: (i, 0)),
                  pl.BlockSpec(wb8.shape, lambda i: (0, 0, 0, 0))],
        out_specs=[pl.BlockSpec((tm, 4 * D), lambda i: (i, 1)),
                   pl.BlockSpec((tm, 4 * W), lambda i: (i, 0)),
                   pl.BlockSpec(wb8.shape, lambda i: (0, 0, 0, 0))],
        out_shape=[jax.ShapeDtypeStruct((T, COLS_IN), BF16),
                   jax.ShapeDtypeStruct((T, 4 * W), BF16),
                   jax.ShapeDtypeStruct(wb8.shape, BF16)],
        scratch_shapes=[pltpu.VMEM((4, W, D), BF16), pltpu.VMEM((4, W, D), F32)],
        compiler_params=_cp("arbitrary"),
    )(dm, z, proj, y, wb8)


def _mixer_bwd(proj, ca_saved, dy, dproj, sp, tm=256):
    nt = T // tm
    per = tm // HB
    ne = tm + HA
    last_blk = T // HA - 1
    conv_taps = [(HB - (CONF_K - 1) + k, k) for k in range(CONF_K)]

    def body(main_ref, hb_ref, ha_ref, ca_ref, cah_ref, dy_ref, dyh_ref, cw, cb, lg, lb, pw, ps, sc, gg, gb, ws, bst,
             dp_any, dp_ref, dcw_ref, dsc_ref, vec_ref, dpw_ref, dws_ref, dbs_ref, e1, e2, e3, stage):
        del dp_any, cb
        i = pl.program_id(0)
        keep_b = (i > 0).astype(F32)
        keep_a = (i < nt - 1).astype(F32)

        @pl.when(i == 0)
        def _():
            dcw_ref[...] = jnp.zeros_like(dcw_ref)
            dsc_ref[...] = jnp.zeros_like(dsc_ref)
            vec_ref[...] = jnp.zeros_like(vec_ref)
            dpw_ref[...] = jnp.zeros_like(dpw_ref)
            dws_ref[...] = jnp.zeros_like(dws_ref)
            dbs_ref[...] = jnp.zeros_like(dbs_ref)

        def mcol(c0):
            return main_ref[:, c0:c0 + W].astype(F32)

        def hbcol(c0):
            return hb_ref[:, c0:c0 + W].astype(F32)

        def hacol(c0):
            return ha_ref[:, c0:c0 + W].astype(F32)

        def load_dy(c0):
            e2[0:tm, :] = dy_ref[:, c0:c0 + W].astype(F32)
            e2[tm:ne, :] = dyh_ref[:, c0:c0 + W].astype(F32) * keep_a

        a = mcol(0)
        sa = _sig(mcol(W))
        e1[0:HB, :] = hbcol(0) * _sig(hbcol(W)) * keep_b
        e1[HB:HB + tm, :] = a * sa
        e1[HB + tm:HB + tm + SUBLANES, :] = jnp.zeros((SUBLANES, W), F32)
        e2[0:tm, :] = ca_ref[...].astype(F32)
        e2[tm:ne, :] = cah_ref[...].astype(F32)
        xh, rstd = _ln_stats(e2[0:ne, :])
        nn = xh * lg[...] + lb[...]
        s = _sig(nn)
        load_dy(0)
        dn = e2[0:ne, :] * (s * (1.0 + nn * (1.0 - s)))
        vec_ref[1:2, :] += _rowsum(dn[0:tm] * xh[0:tm])
        vec_ref[2:3, :] += _rowsum(dn[0:tm])
        dca = _ln_bwd(dn * lg[...], xh, rstd)
        e3[0:ne, :] = dca
        dmain = dca[0:tm]
        vec_ref[0:1, :] += _rowsum(dmain)
        _tap_grads(dmain, e1, [off for off, _ in conv_taps], tm, stage, dcw_ref)
        dglu = _tap_sum(e3, cw, [(CONF_K - 1 - k, k) for k in range(CONF_K)], tm, stage)
        dp_ref[:, 0:W] = (dglu * sa).astype(BF16)
        dp_ref[:, W:2 * W] = (dglu * a * sa * (1.0 - sa)).astype(BF16)

        pin = mcol(1024)
        e1[0:HB, :] = hbcol(1024) * keep_b
        e1[HB:HB + tm, :] = pin
        load_dy(W)
        dyb = e2[0:ne, :]
        pos_m = (i * tm + lax.broadcasted_iota(jnp.int32, (tm, 1), 0) + 1).astype(F32)
        pos_e = (i * tm + lax.broadcasted_iota(jnp.int32, (ne, 1), 0) + 1).astype(F32)
        for g, w in enumerate(POOL_WINDOWS):
            lo = g * GW
            acc = e1[pl.ds(HB, tm), lo:lo + GW]
            for j in range(1, w):
                acc = acc + e1[pl.ds(HB - j, tm), lo:lo + GW]
            pooled = (acc / jnp.minimum(pos_m, float(w)) - pin[:, lo:lo + GW]).astype(BF16)
            pwb = pw[g].astype(BF16)
            mixed = jnp.dot(pooled, pwb, preferred_element_type=F32)
            dyb_g = dyb[:, lo:lo + GW]
            vec_ref[3:4, lo:lo + GW] += _rowsum(dyb_g[0:tm] * mixed)
            dmb = (dyb_g * ps[:, lo:lo + GW]).astype(BF16)
            dpw_ref[g] += lax.dot_general(pooled, dmb[0:tm], (((0,), (0,)), ((), ())), preferred_element_type=F32)
            dpool = lax.dot_general(dmb, pwb, (((1,), (1,)), ((), ())), preferred_element_type=F32)
            e3[0:ne, lo:lo + GW] = dpool / jnp.minimum(pos_e, float(w))
            back = e3[pl.ds(0, tm), lo:lo + GW]
            for j in range(1, w):
                back = back + e3[pl.ds(j, tm), lo:lo + GW]
            dp_ref[:, 1024 + lo:1024 + lo + GW] = (back - dpool[0:tm]).astype(BF16)

        cg = mcol(2048)
        hx = mcol(2560)
        e1[0:HB, :] = hbcol(2048) * hbcol(2560) * keep_b
        e1[HB:HB + tm, :] = cg * hx
        load_dy(2 * W)
        dyc = e2[0:tm, :]
        dconv = dyc * mcol(1536)
        e3[0:tm, :] = dconv
        e3[tm:ne, :] = e2[tm:ne, :] * hacol(1536)
        cv = sc[0:1, :] * e1[pl.ds(HB - 2, tm), :]
        for k in range(1, SC_K):
            cv = cv + sc[k:k + 1, :] * e1[pl.ds(HB - 2 + k, tm), :]
        dp_ref[:, 1536:2048] = (dyc * cv).astype(BF16)
        for k in range(SC_K):
            dsc_ref[k:k + 1, :] += _rowsum(dconv * e1[pl.ds(HB - 2 + k, tm), :])
        dq = sc[0:1, :] * e3[pl.ds(2, tm), :]
        for k in range(1, SC_K):
            dq = dq + sc[k:k + 1, :] * e3[pl.ds(2 - k, tm), :]
        dp_ref[:, 2048:2560] = (dq * hx).astype(BF16)
        dp_ref[:, 2560:3072] = (dq * cg).astype(BF16)

        u = mcol(3072)
        vh, vr = _ln_stats(mcol(3584))
        vn = (vh * gg[...] + gb[...]).astype(BF16)
        dyd = dy_ref[:, 3 * W:4 * W].astype(F32)
        tri = _tri_mask()
        for g in range(4):
            lo = g * GW
            wm = jnp.where(tri, ws[g], 0.0).astype(BF16)
            dws_g = jnp.zeros((GW, GW), F32)
            dbs_g = jnp.zeros((GW, 1), F32)
            for c in range(tm // GW):
                r0 = c * GW
                blk = vn[r0:r0 + GW, lo:lo + GW]
                sg = jnp.dot(wm, blk, preferred_element_type=F32) + bst[:, g:g + 1]
                dyd_b = dyd[r0:r0 + GW, lo:lo + GW]
                dp_ref[r0:r0 + GW, 3072 + lo:3072 + lo + GW] = (dyd_b * sg).astype(BF16)
                dsg = dyd_b * u[r0:r0 + GW, lo:lo + GW]
                dsgb = dsg.astype(BF16)
                dbs_g = dbs_g + jnp.sum(dsg, axis=-1, keepdims=True)
                dws_g = dws_g + lax.dot_general(dsgb, blk, (((1,), (1,)), ((), ())), preferred_element_type=F32)
                e1[r0:r0 + GW, lo:lo + GW] = lax.dot_general(wm, dsgb, (((0,), (0,)), ((), ())),
                                                             preferred_element_type=F32)
            dws_ref[g] += jnp.where(tri, dws_g, 0.0)
            dbs_ref[g] += jnp.broadcast_to(dbs_g, (GW, GW))
        dvn = e1[0:tm, :]
        vec_ref[4:5, :] += _rowsum(dvn * vh)
        vec_ref[5:6, :] += _rowsum(dvn)
        dp_ref[:, 3584:4096] = _ln_bwd(dvn * gg[...], vh, vr).astype(BF16)

    plist = _mixer_params(sp)
    in_specs = [pl.BlockSpec((tm, MIX_COLS), lambda i: (i, 0)),
                pl.BlockSpec((HB, MIX_COLS), lambda i: (jnp.maximum(i * per - 1, 0), 0)),
                pl.BlockSpec((HA, MIX_COLS), lambda i: (jnp.minimum((i + 1) * per, last_blk), 0)),
                pl.BlockSpec((tm, W), lambda i: (i, 0)),
                pl.BlockSpec((HA, W), lambda i: (jnp.minimum((i + 1) * per, last_blk), 0)),
                pl.BlockSpec((tm, 4 * W), lambda i: (i, 0)),
                pl.BlockSpec((HA, 4 * W), lambda i: (jnp.minimum((i + 1) * per, last_blk), 0))]
    in_specs += _small_specs(plist)
    in_specs += [pl.BlockSpec(memory_space=pl.ANY)]
    z2 = lambda i: (0, 0)
    z3 = lambda i: (0, 0, 0)
    out_specs = [pl.BlockSpec((tm, MIX_COLS), lambda i: (i, 0)),
                 pl.BlockSpec((32, W), z2), pl.BlockSpec((8, W), z2), pl.BlockSpec((8, W), z2),
                 pl.BlockSpec((4, GW, GW), z3), pl.BlockSpec((4, GW, GW), z3), pl.BlockSpec((4, GW, GW), z3)]
    out_shape = [jax.ShapeDtypeStruct((T, COLS_IN), BF16),
                 jax.ShapeDtypeStruct((32, W), F32), jax.ShapeDtypeStruct((8, W), F32),
                 jax.ShapeDtypeStruct((8, W), F32),
                 jax.ShapeDtypeStruct((4, GW, GW), F32), jax.ShapeDtypeStruct((4, GW, GW), F32),
                 jax.ShapeDtypeStruct((4, GW, GW), F32)]
    n_in = 7 + len(plist)
    return pl.pallas_call(
        body, name="b_mixers", grid=(nt,), in_specs=in_specs, out_specs=out_specs, out_shape=out_shape,
        scratch_shapes=[pltpu.VMEM((HB + ne, W), F32), pltpu.VMEM((ne, W), F32), pltpu.VMEM((ne, W), F32),
                        pltpu.VMEM((ne + SUBLANES, W), F32)],
        input_output_aliases={n_in: 0},
        compiler_params=_cp("arbitrary"),
    )(proj, proj, proj, ca_saved, ca_saved, dy, dy, *plist, dproj)


def _norm_first(x, g, tm=512):
    def body(x_ref, g_ref, o_ref):
        o_ref[...] = _rms(x_ref[...], g_ref[...]).astype(BF16)

    return pl.pallas_call(
        body, name="f_norm0", grid=(T // tm,),
        in_specs=[pl.BlockSpec((tm, D), lambda i: (i, 0)), pl.BlockSpec((1, D), lambda i: (0, 0))],
        out_specs=pl.BlockSpec((tm, D), lambda i: (i, 0)),
        out_shape=jax.ShapeDtypeStruct((T, D), BF16), compiler_params=_cp("arbitrary"),
    )(x, g)


def _loss_head(x, target, g, tm=256):
    def body(x_ref, t_ref, g_ref, dx_ref, dg_ref, loss_ref):
        i = pl.program_id(0)
        x = x_ref[...]
        r = lax.rsqrt(jnp.mean(x * x, axis=-1, keepdims=True) + EPS)
        xh = x * r
        gv = g_ref[...]
        e = xh * gv - t_ref[...]
        dyv = e * (1.0 / D)
        part = jnp.sum(_rowsum(e * e), axis=-1, keepdims=True) * (0.5 / D)
        u = dyv * gv
        dx_ref[...] = r * (u - xh * jnp.mean(u * xh, axis=-1, keepdims=True))
        dgp = _rowsum(dyv * xh)

        @pl.when(i == 0)
        def _():
            dg_ref[...] = dgp
            loss_ref[...] = jnp.broadcast_to(part, (1, GW))

        @pl.when(i > 0)
        def _():
            dg_ref[...] += dgp
            loss_ref[...] += jnp.broadcast_to(part, (1, GW))

    return pl.pallas_call(
        body, name="loss_head", grid=(T // tm,),
        in_specs=[pl.BlockSpec((tm, D), lambda i: (i, 0)), pl.BlockSpec((tm, D), lambda i: (i, 0)),
                  pl.BlockSpec((1, D), lambda i: (0, 0))],
        out_specs=[pl.BlockSpec((tm, D), lambda i: (i, 0)), pl.BlockSpec((1, D), lambda i: (0, 0)),
                   pl.BlockSpec((1, GW), lambda i: (0, 0))],
        out_shape=[jax.ShapeDtypeStruct((T, D), F32), jax.ShapeDtypeStruct((1, D), F32),
                   jax.ShapeDtypeStruct((1, GW), F32)],
        compiler_params=_cp("arbitrary"),
    )(x, target, g)


def _ple_bwd(dx4, sv, w_pleg8, g_ple, tm=256):
    nt = T // tm
    ple_dim = sv["p"].shape[1]

    def body(dx_ref, gl_ref, pe_ref, x_ref, h_ref, p_ref, g_ref, wg_ref,
             dx3_ref, dx3b_ref, dg_ref, dwg_ref, dwp_ref, acc_g, acc_p):
        i = pl.program_id(0)
        d = dx_ref[...]
        s = _sig(gl_ref[...].astype(F32))
        dpe = (d * s).astype(BF16)
        dgl = (d * pe_ref[...].astype(F32) * s * (1.0 - s)).astype(BF16)
        dh = lax.dot_general(dgl, wg_ref[...], (((1,), (1,)), ((), ())), preferred_element_type=F32)
        dx, dgp = _rms_bwd(dh, x_ref[...], g_ref[...], d)
        dx3_ref[...] = dx
        dx3b_ref[...] = dx.astype(BF16)
        part_g = lax.dot_general(h_ref[...], dgl, (((0,), (0,)), ((), ())), preferred_element_type=F32)
        part_p = lax.dot_general(p_ref[...], dpe, (((0,), (0,)), ((), ())), preferred_element_type=F32)

        @pl.when(i == 0)
        def _():
            dg_ref[...] = dgp
            acc_g[...] = part_g
            acc_p[...] = part_p

        @pl.when(i > 0)
        def _():
            dg_ref[...] += dgp
            acc_g[...] += part_g
            acc_p[...] += part_p

        @pl.when(i == nt - 1)
        def _():
            for j in range(NDEV):
                dwg_ref[j] = acc_g[j * GW:(j + 1) * GW, :].astype(BF16)
                dwp_ref[j] = acc_p[:, j * GW:(j + 1) * GW].astype(BF16)

    tile = lambda w: pl.BlockSpec((tm, w), lambda i: (i, 0))
    const = lambda shp: pl.BlockSpec(shp, lambda i: (0,) * len(shp))
    return pl.pallas_call(
        body, name="b_ple", grid=(nt,),
        in_specs=[tile(D), tile(D), tile(D), tile(D), tile(D), tile(ple_dim), const((1, D)), const((D, D))],
        out_specs=[tile(D), tile(D), const((1, D)), const((NDEV, GW, D)), const((NDEV, ple_dim, GW))],
        out_shape=[jax.ShapeDtypeStruct((T, D), F32), jax.ShapeDtypeStruct((T, D), BF16),
                   jax.ShapeDtypeStruct((1, D), F32), jax.ShapeDtypeStruct((NDEV, GW, D), BF16),
                   jax.ShapeDtypeStruct((NDEV, ple_dim, GW), BF16)],
        scratch_shapes=[pltpu.VMEM((D, D), F32), pltpu.VMEM((ple_dim, D), F32)],
        compiler_params=_cp("arbitrary"),
    )(dx4, sv["gl"], sv["pe"], sv["x3"], sv["h3"], sv["p"], g_ple, w_pleg8.reshape(D, D))


def _layer_fwd(x, h1, p_bf, gw, sp, g_next):
    proj, = _mm(h1, gw["w_in"], mode="out", name="f_proj", outs=[BF16], tm=T)
    y, ca = _mixer_fwd(proj, sp)
    z, merged = _merge_fwd(y, proj, gw["w_branch"])
    x2, h2 = _mm(merged, gw["w_out"].reshape(1, D, D), mode="acc", name="f_out", outs=[F32, BF16], tm=T // 2,
                 tiles=[x], params=[sp["g_mlp"]], epi=_epi_res_norm)
    up, = _mm(h2, gw["w_up"], mode="out", name="f_up", outs=[BF16], tm=T)
    x3, h3 = _mm(up, gw["w_down"].reshape(1, 4 * D, D), mode="acc", name="f_down", outs=[F32, BF16], tm=T // 4,
                 tiles=[x2], params=[sp["g_ple"]], epi=_epi_res_norm, a_pre=_relu2_bf16)
    x4, gl, hn, pe = _mm(h3, gw["w_pleg"].reshape(1, D, D), mode="acc", name="f_gate", tm=T // 2,
                         outs=[F32, BF16, BF16, BF16], tiles=[x3, p_bf], params=[g_next, gw["w_ple"]], epi=_epi_ple)
    saved = dict(x=x, h1=h1, proj=proj, y=y, ca=ca, z=z, merged=merged, x2=x2, h2=h2, up=up, x3=x3, h3=h3,
                 pe=pe, gl=gl, p=p_bf)
    return x4, hn, saved


def _layer_bwd(dx4, sv, gw, sp, submit, early_group=False):
    dw = {}
    dx3, dx3b, dg_ple, dw["w_pleg"], dw["w_ple"] = _ple_bwd(dx4, sv, gw["w_pleg"], sp["g_ple"])
    dup, = _mm(dx3b, gw["w_down"], mode="out", trans_b=True, name="b_dact", outs=[BF16], tm=T,
               tiles=[sv["up"]], epi=_epi_dup)
    dw["w_down"] = _mm_tn(sv["up"], dx3b, nj=NDEV, split="row", name="b_dw_down", a_pre=_relu2_bf16)
    dw["w_up"] = _mm_tn(sv["h2"], dup, nj=NDEV, split="col", name="b_dw_up")
    if early_group:
        dw["w_up"], dup = lax.optimization_barrier((dw["w_up"], dup))
        dup = submit(dw, ("w_up", "w_down", "w_ple", "w_pleg"), dup)
    dx2, dx2b, dg_mlp = _mm(dup, gw["w_up"], mode="full", trans_b=True, name="b_dh2", tm=T // 4,
                            outs=[F32, BF16], tiles=[sv["x2"], dx3], params=[sp["g_mlp"]], epi=_epi_rms_bwd, reds=[D])
    dm, = _mm(dx2b, gw["w_out"].reshape(1, D, D), mode="acc", trans_b=True, name="b_dmerged", outs=[BF16],
              tm=T // 2)
    dw["w_out"] = _mm_tn(sv["merged"], dx2b, nj=NDEV, split="row", name="b_dw_out")
    dproj, dy, dw["w_branch"] = _merge_bwd(dm, sv["z"], sv["proj"], sv["y"], gw["w_branch"])
    dy = submit(dw, ("w_branch", "w_out") if early_group else BIG[1:], dy)
    dproj, dcw, dsc, vec, dpw, dws, dbs = _mixer_bwd(sv["proj"], sv["ca"], dy, dproj, sp)
    dw["w_in"] = _mm_tn(sv["h1"], dproj, nj=NDEV, split="col", name="b_dw_in")
    dw["w_in"], dproj = lax.optimization_barrier((dw["w_in"], dproj))
    dproj = submit(dw, BIG[:1], dproj)
    dx, dg_mix = _mm(dproj, gw["w_in"], mode="full", trans_b=True, name="b_dh1", outs=[F32], tm=T // 4,
                     tiles=[sv["x"], dx2], params=[sp["g_mix"]], epi=_epi_rms_bwd, reds=[D])
    small = dict(norm_mix=dg_mix[0], conf_dw=dcw[:CONF_K], conf_dw_b=vec[0], conf_ln_g=vec[1], conf_ln_b=vec[2],
                 pool_w=dpw, pool_scale=vec[3], sc_conv=dsc[:SC_K], gmlp_ln_g=vec[4], gmlp_ln_b=vec[5],
                 gmlp_ws=dws, gmlp_bs=dbs[:, :, 0], norm_mlp=dg_mlp[0], norm_ple=dg_ple[0])
    return dx, small


ANY = pl.BlockSpec(memory_space=pl.ANY)


def _mesh_pos():
    return lax.axis_index("x"), lax.axis_index("y"), lax.axis_index("c")


def _other_chips(x, y):
    return [(1 - x, y), (x, 1 - y), (1 - x, 1 - y)]


def _launch_comm(body, peers_of, operands, out_shapes, sems, name, seq_id):
    n_in, n_out = len(operands), len(out_shapes)
    if seq_id is None:
        return pl.pallas_call(body, name=name, in_specs=[ANY] * n_in, out_specs=[ANY] * n_out,
                              out_shape=out_shapes, scratch_shapes=sems)(*operands)

    def seq_body(*refs):
        peers = peers_of(*_mesh_pos())
        barrier = pltpu.get_barrier_semaphore()
        for peer in peers:
            pl.semaphore_signal(barrier, inc=1, device_id=peer, device_id_type=MESH)
        pl.semaphore_wait(barrier, len(peers))
        body(*refs)

    return pl.kernel(seq_body, name=name, out_type=out_shapes,
                     mesh=plsc.ScalarSubcoreMesh(axis_name="seq", num_cores=1), scratch_types=sems,
                     compiler_params=pltpu.CompilerParams(collective_id=seq_id))(*operands)


def _all_gather(shards, name, seq_id=None):
    n = len(shards)

    def body(*refs):
        s_refs, o_refs = refs[:n], refs[n:2 * n]
        send_sems, recv_sems, local_sems = refs[2 * n:]
        x, y, c = _mesh_pos()
        me = 4 * x + 2 * y + c
        here = (x, y, c)
        sibling = (x, y, 1 - c)
        chips = _other_chips(x, y)

        def slot(px, py, pc):
            return 4 * px + 2 * py + pc

        def copy(t, k, slot_idx, to, src=None):
            dst = o_refs[t].at[slot_idx]
            return pltpu.make_async_remote_copy(
                src_ref=dst if src is None else src, dst_ref=dst,
                send_sem=send_sems.at[t * 7 + k], recv_sem=recv_sems.at[t * 7 + k],
                device_id=to, device_id_type=MESH)

        mine = [pltpu.make_async_copy(s_refs[t], o_refs[t].at[me], local_sems.at[t]) for t in range(n)]
        for cp in mine:
            cp.start()
        first = []
        for t in range(n):
            for j, chip in enumerate(chips):
                first.append(copy(t, 1 + j, me, (*chip, c), src=s_refs[t]))
        for t in range(n):
            first.append(copy(t, 0, me, sibling, src=s_refs[t]))
        for cp in first:
            cp.start()
        passed = []
        for t in range(n):
            for j, chip in enumerate(chips):
                copy(t, 1 + j, slot(*chip, c), here).wait_recv()
                fwd = copy(t, 4 + j, slot(*chip, c), sibling)
                fwd.start()
                passed.append(fwd)
        for t in range(n):
            copy(t, 0, slot(x, y, 1 - c), here).wait_recv()
            for j, chip in enumerate(chips):
                copy(t, 4 + j, slot(*chip, 1 - c), here).wait_recv()
        for cp in first + passed:
            cp.wait_send()
        for cp in mine:
            cp.wait()

    def peers_of(x, y, c):
        return [(x, y, 1 - c)] + [(*chip, c) for chip in _other_chips(x, y)]

    return _launch_comm(
        body, peers_of, shards, [jax.ShapeDtypeStruct((NDEV,) + s.shape, s.dtype) for s in shards],
        [pltpu.SemaphoreType.DMA((7 * n,)), pltpu.SemaphoreType.DMA((7 * n,)), pltpu.SemaphoreType.DMA((n,))],
        name, seq_id)


def _rs_exchange(p4s, qs, name, seq_id=None):
    n_p, n_q = len(p4s), len(qs)

    def body(*refs):
        p_refs, q_refs = refs[:n_p], refs[n_p:n_p + n_q]
        rb_refs, rc_refs = refs[n_p + n_q:2 * n_p + n_q], refs[2 * n_p + n_q:2 * (n_p + n_q)]
        pair_send, pair_recv, chip_send, chip_recv, local_sems = refs[2 * (n_p + n_q):]
        x, y, c = _mesh_pos()
        a_idx = 2 * x + y
        chips = _other_chips(x, y)
        mine = [pltpu.make_async_copy(q_refs[t].at[a_idx], rc_refs[t].at[a_idx], local_sems.at[t])
                for t in range(n_q)]
        sends = []
        for t in range(n_q):
            for j, chip in enumerate(chips):
                sends.append(pltpu.make_async_remote_copy(
                    src_ref=q_refs[t].at[2 * chip[0] + chip[1]], dst_ref=rc_refs[t].at[a_idx],
                    send_sem=chip_send.at[t * 3 + j], recv_sem=chip_recv.at[t * 3 + j],
                    device_id=(*chip, c), device_id_type=MESH))
        pairs = [pltpu.make_async_remote_copy(
            src_ref=p_refs[t].at[:, 1 - c], dst_ref=rb_refs[t], send_sem=pair_send.at[t], recv_sem=pair_recv.at[t],
            device_id=(x, y, 1 - c), device_id_type=MESH) for t in range(n_p)]
        for cp in sends + mine + pairs:
            cp.start()
        for cp in pairs:
            cp.wait()
        for t in range(n_q):
            for j, chip in enumerate(chips):
                landed = rc_refs[t].at[2 * chip[0] + chip[1]]
                pltpu.make_async_remote_copy(
                    src_ref=landed, dst_ref=landed, send_sem=chip_send.at[t * 3 + j],
                    recv_sem=chip_recv.at[t * 3 + j], device_id=(x, y, c), device_id_type=MESH).wait_recv()
        for cp in sends:
            cp.wait_send()
        for cp in mine:
            cp.wait()

    def peers_of(x, y, c):
        peers = [(x, y, 1 - c)] if n_p else []
        return peers + ([(*chip, c) for chip in _other_chips(x, y)] if n_q else [])

    out_shapes = [jax.ShapeDtypeStruct((NCHIP,) + p.shape[2:], p.dtype) for p in p4s]
    out_shapes += [jax.ShapeDtypeStruct(q.shape, q.dtype) for q in qs]
    sems = [pltpu.SemaphoreType.DMA((max(n_p, 1),)), pltpu.SemaphoreType.DMA((max(n_p, 1),)),
            pltpu.SemaphoreType.DMA((max(3 * n_q, 1),)), pltpu.SemaphoreType.DMA((max(3 * n_q, 1),)),
            pltpu.SemaphoreType.DMA((max(n_q, 1),))]
    got = _launch_comm(body, peers_of, list(p4s) + list(qs), out_shapes, sems, name, seq_id)
    return got[:n_p], got[n_p:]


def _pair_sum(p4s, rbs, c_idx, name, nst=1):
    n = len(p4s)
    trs = [p.shape[2] // nst for p in p4s]

    def body(c_ref, *refs):
        del c_ref
        p_refs, r_refs, o_refs = refs[:n], refs[n:2 * n], refs[2 * n:]
        for p_ref, r_ref, o_ref in zip(p_refs, r_refs, o_refs):
            o_ref[...] = (p_ref[...].astype(F32) + r_ref[...].astype(F32)).astype(o_ref.dtype)

    in_specs = [pl.BlockSpec((None, None, tr, p.shape[3]), lambda b, i, c_ref: (b, c_ref[0], i, 0))
                for p, tr in zip(p4s, trs)]
    in_specs += [pl.BlockSpec((None, tr, p.shape[3]), lambda b, i, c_ref: (b, i, 0)) for p, tr in zip(p4s, trs)]
    out_specs = [pl.BlockSpec((None, tr, p.shape[3]), lambda b, i, c_ref: (b, i, 0)) for p, tr in zip(p4s, trs)]
    return pl.pallas_call(
        body, name=name,
        grid_spec=pltpu.PrefetchScalarGridSpec(num_scalar_prefetch=1, grid=(NCHIP, nst), in_specs=in_specs,
                                               out_specs=out_specs),
        out_shape=[jax.ShapeDtypeStruct((NCHIP,) + p.shape[2:], p.dtype) for p in p4s],
        compiler_params=_cp("arbitrary", "arbitrary"),
    )(c_idx, *p4s, *rbs)


class _GradientPipeline:
    def __init__(self, c_idx, results):
        self.c_idx, self.results, self.pending = c_idx, results, None

    def _sum_pending(self, chain):
        names, layer, p4s, rbs = self.pending
        qs = _pair_sum(p4s, rbs, self.c_idx, name="rs_pairsum_%d" % len(names))
        return lax.optimization_barrier((chain, qs))

    def submit(self, dw, names, layer, chain):
        qs, tag, seq_id = [], "pair", 3
        if self.pending is not None:
            chain, qs = self._sum_pending(chain)
            tag, seq_id = "pair_chip", 4
        p4s = [dw[n].reshape((NCHIP, 2) + BIG_SHARD[n]) for n in names]
        rbs, rcs = _rs_exchange(p4s, qs, name="rs_%s_%d" % (tag, len(names)), seq_id=seq_id)
        self._record(rcs)
        self.pending = (names, layer, p4s, rbs)
        return chain

    def finish(self, chain):
        chain, qs = self._sum_pending(chain)
        self._record(_rs_exchange([], qs, name="rs_chip_last", seq_id=5)[1])
        self.pending = None
        return chain

    def _record(self, rcs):
        if rcs:
            names, layer = self.pending[:2]
            for n, rc in zip(names, rcs):
                self.results[n][layer] = rc


def _adamw(w, g, m, v):
    m = ADAM_B1 * m + (1.0 - ADAM_B1) * g
    v = ADAM_B2 * v + (1.0 - ADAM_B2) * (g * g)
    m_hat = m / (1.0 - ADAM_B1 ** ADAM_STEP)
    v_hat = v / (1.0 - ADAM_B2 ** ADAM_STEP)
    delta = -ADAM_LR * (m_hat / (jnp.sqrt(v_hat) + ADAM_EPS) + ADAM_WD * w)
    return delta, m, v


def _adam_sharded(rcs, w, m, v, tr, name, first_layer, partial=None):
    _, r, c = w.shape
    nst = r // tr
    n_l = len(rcs)

    def body(*refs):
        rc_refs = refs[:n_l]
        w_ref, m_ref, v_ref = refs[n_l:n_l + 3]
        g_out, d_out, m_out, v_out = refs[-4:]
        layer = pl.program_id(0)
        for k, rc in enumerate(rc_refs):
            @pl.when(layer == k)
            def _():
                g = rc[0].astype(F32) + rc[1].astype(F32) + rc[2].astype(F32) + rc[3].astype(F32)
                delta, m_new, v_new = _adamw(w_ref[...], g, m_ref[...], v_ref[...])
                g_out[...] = g
                d_out[...] = delta
                m_out[...] = m_new
                v_out[...] = v_new

    rc_specs = [pl.BlockSpec((NCHIP, tr, c), lambda l, i, k=k: (0, jnp.where(l == k, i, 0), 0)) for k in range(n_l)]
    wspec = pl.BlockSpec((None, tr, c), lambda l, i: (first_layer + l, i, 0))
    carried = [] if partial is None else list(partial)
    return pl.pallas_call(
        body, name=name, grid=(n_l, nst),
        in_specs=rc_specs + [wspec] * 3 + [pl.BlockSpec(memory_space=pl.ANY)] * len(carried),
        out_specs=[wspec] * 4, out_shape=[jax.ShapeDtypeStruct(w.shape, F32)] * 4,
        input_output_aliases={n_l + 3 + k: k for k in range(len(carried))},
        compiler_params=_cp("arbitrary", "arbitrary"),
    )(*rcs, w, m, v, *carried)


def _adam_packed(g, w, m, v, direct):
    n_d = len(direct)

    def pieces(shape):
        width = shape[-1]
        count = 1
        for s in shape[:-1]:
            count *= s
        per_row = D // width
        out = []
        for k in range(count):
            idx = (k,) if len(shape) == 2 else (k // shape[1], k % shape[1])
            out.append((idx, k // per_row, (k % per_row) * width, width))
        return out

    def body(g_ref, w_ref, m_ref, v_ref, d_out, m_out, v_out, *outs):
        delta, m_new, v_new = _adamw(w_ref[...], g_ref[...], m_ref[...], v_ref[...])
        d_out[...] = delta
        m_out[...] = m_new
        v_out[...] = v_new
        for a, (_, row0, shape) in enumerate(direct):
            for src, dst in zip((g_ref, d_out, m_out, v_out), outs[4 * a:4 * a + 4]):
                for idx, row, lane0, width in pieces(shape):
                    piece = src[pl.ds(row0 + row, 1), lane0:lane0 + width]
                    if len(idx) == 1:
                        dst[pl.ds(idx[0], 1), :] = piece
                    else:
                        dst[idx[0], pl.ds(idx[1], 1), :] = piece

    out_shape = [jax.ShapeDtypeStruct(g.shape, F32)] * 3
    for _, _, shape in direct:
        out_shape += [jax.ShapeDtypeStruct(shape, F32)] * 4
    res = pl.pallas_call(body, name="adam_small", out_shape=out_shape,
                         compiler_params=pltpu.CompilerParams(vmem_limit_bytes=VMEM_LIMIT_BYTES))(g, w, m, v)
    return res[:3], {name: res[3 + 4 * a:7 + 4 * a] for a, (name, _, _) in enumerate(direct)}


def _sum4(rc):
    def body(rc_ref, o_ref):
        o_ref[...] = rc_ref[0] + rc_ref[1] + rc_ref[2] + rc_ref[3]

    return pl.pallas_call(
        body, name="small_sum", out_shape=jax.ShapeDtypeStruct(rc.shape[1:], F32),
    )(rc)


BIG = ("w_in", "w_branch", "w_out", "w_up", "w_down", "w_ple", "w_pleg")
BIG_SHARD = {"w_in": (D, D), "w_branch": (4 * W, GW), "w_out": (GW, D), "w_up": (D, W), "w_down": (W, D),
             "w_ple": (256, GW), "w_pleg": (GW, D)}
ADAM_ROWS = {"w_in": 256, "w_branch": 512, "w_out": 128, "w_up": 256, "w_down": 256, "w_ple": 256, "w_pleg": 128}
SMALL = (("norm_mix", (DEPTH, D)), ("conf_dw", (DEPTH, CONF_K, W)), ("conf_dw_b", (DEPTH, W)),
         ("conf_ln_g", (DEPTH, W)), ("conf_ln_b", (DEPTH, W)), ("pool_w", (DEPTH, 4, GW, GW)),
         ("pool_scale", (DEPTH, W)), ("sc_conv", (DEPTH, SC_K, W)), ("gmlp_ln_g", (DEPTH, W)),
         ("gmlp_ln_b", (DEPTH, W)), ("gmlp_ws", (DEPTH, 4, GW, GW)), ("gmlp_bs", (DEPTH, 4, GW)),
         ("norm_mlp", (DEPTH, D)), ("norm_ple", (DEPTH, D)), ("norm_final", (D,)))
CHANNEL_SHARDED = ("conf_dw", "sc_conv")
SMALL_ROWS = 80


def _pack(arrs, rows):
    flat = jnp.concatenate([a.reshape(-1) for a in arrs])
    return jnp.pad(flat, (0, rows * D - flat.shape[0])).reshape(rows, D)


def _unpack(packed, shapes):
    flat = packed.reshape(-1)
    out, off = [], 0
    for shp in shapes:
        size = 1
        for s in shp:
            size *= s
        out.append(flat[off:off + size].reshape(shp))
        off += size
    return out


def kernel(x, p, norm_mix, w_in, conf_dw, conf_dw_b, conf_ln_g, conf_ln_b, pool_w, pool_scale, sc_conv, gmlp_ln_g, gmlp_ln_b, gmlp_ws, gmlp_bs, w_branch, w_out, norm_mlp, w_up, w_down, norm_ple, w_ple, w_ple_gate, norm_final, loss_target, m_norm_mix, m_w_in, m_conf_dw, m_conf_dw_b, m_conf_ln_g, m_conf_ln_b, m_pool_w, m_pool_scale, m_sc_conv, m_gmlp_ln_g, m_gmlp_ln_b, m_gmlp_ws, m_gmlp_bs, m_w_branch, m_w_out, m_norm_mlp, m_w_up, m_w_down, m_norm_ple, m_w_ple, m_w_ple_gate, m_norm_final, v_norm_mix, v_w_in, v_conf_dw, v_conf_dw_b, v_conf_ln_g, v_conf_ln_b, v_pool_w, v_pool_scale, v_sc_conv, v_gmlp_ln_g, v_gmlp_ln_b, v_gmlp_ws, v_gmlp_bs, v_w_branch, v_w_out, v_norm_mlp, v_w_up, v_w_down, v_norm_ple, v_w_ple, v_w_ple_gate, v_norm_final):
    weights = dict(norm_mix=norm_mix, w_in=w_in, conf_dw=conf_dw, conf_dw_b=conf_dw_b, conf_ln_g=conf_ln_g,
                   conf_ln_b=conf_ln_b, pool_w=pool_w, pool_scale=pool_scale, sc_conv=sc_conv, gmlp_ln_g=gmlp_ln_g,
                   gmlp_ln_b=gmlp_ln_b, gmlp_ws=gmlp_ws, gmlp_bs=gmlp_bs, w_branch=w_branch, w_out=w_out,
                   norm_mlp=norm_mlp, w_up=w_up, w_down=w_down, norm_ple=norm_ple, w_ple=w_ple, w_pleg=w_ple_gate,
                   norm_final=norm_final)
    mom1 = dict(norm_mix=m_norm_mix, w_in=m_w_in, conf_dw=m_conf_dw, conf_dw_b=m_conf_dw_b, conf_ln_g=m_conf_ln_g,
                conf_ln_b=m_conf_ln_b, pool_w=m_pool_w, pool_scale=m_pool_scale, sc_conv=m_sc_conv,
                gmlp_ln_g=m_gmlp_ln_g, gmlp_ln_b=m_gmlp_ln_b, gmlp_ws=m_gmlp_ws, gmlp_bs=m_gmlp_bs,
                w_branch=m_w_branch, w_out=m_w_out, norm_mlp=m_norm_mlp, w_up=m_w_up, w_down=m_w_down,
                norm_ple=m_norm_ple, w_ple=m_w_ple, w_pleg=m_w_ple_gate, norm_final=m_norm_final)
    mom2 = dict(norm_mix=v_norm_mix, w_in=v_w_in, conf_dw=v_conf_dw, conf_dw_b=v_conf_dw_b, conf_ln_g=v_conf_ln_g,
                conf_ln_b=v_conf_ln_b, pool_w=v_pool_w, pool_scale=v_pool_scale, sc_conv=v_sc_conv,
                gmlp_ln_g=v_gmlp_ln_g, gmlp_ln_b=v_gmlp_ln_b, gmlp_ws=v_gmlp_ws, gmlp_bs=v_gmlp_bs,
                w_branch=v_w_branch, w_out=v_w_out, norm_mlp=v_norm_mlp, w_up=v_w_up, w_down=v_w_down,
                norm_ple=v_norm_ple, w_ple=v_w_ple, w_pleg=v_w_ple_gate, norm_final=v_norm_final)

    xi, yi, ci = _mesh_pos()
    me = 4 * xi + 2 * yi + ci
    c_idx = jnp.reshape(ci, (1,)).astype(jnp.int32)

    gathered, conf_full, sc_full = [], [], []
    for l in range(DEPTH):
        shard = lambda n: weights[n][l].astype(BF16).reshape(BIG_SHARD[n])
        w_in_g, conf_g, sc_g = _all_gather([shard("w_in"), conf_dw[l], sc_conv[l]], name="ag_first", seq_id=1)
        if l + 1 < DEPTH:
            rest = _all_gather([shard(n) for n in BIG[1:]], name="ag_rest", seq_id=2)
        else:
            rest = (list(_all_gather([shard(n) for n in BIG[1:4]], name="ag_rest_a", seq_id=2))
                    + list(_all_gather([shard(n) for n in BIG[4:]], name="ag_rest_b", seq_id=2)))
        gw = dict(zip(BIG[1:], rest), w_in=w_in_g)
        gw["w_branch"] = gw["w_branch"].reshape(NDEV, 4, W, GW)
        gathered.append(gw)
        conf_full.append(conf_g)
        sc_full.append(sc_g)

    def small_params(l):
        return dict(cw=conf_full[l], cb=conf_dw_b[l][None], lg=conf_ln_g[l][None], lb=conf_ln_b[l][None],
                    pw=pool_w[l], ps=pool_scale[l][None], sc=sc_full[l], gg=gmlp_ln_g[l][None],
                    gb=gmlp_ln_b[l][None], ws=gmlp_ws[l], bst=gmlp_bs[l].T, g_mix=norm_mix[l][None],
                    g_mlp=norm_mlp[l][None], g_ple=norm_ple[l][None])

    xc = x.reshape(T, D)
    small_names = [n for n, _ in SMALL]

    def in_gradient_layout(n, shard, shape):
        if n not in CHANNEL_SHARDED:
            return shard
        return lax.dynamic_update_slice(jnp.zeros(shape, F32), shard, (0, 0, me * (W // NDEV)))

    small_state = [_pack([in_gradient_layout(n, src[n], shape) for n, shape in SMALL], NDEV * SMALL_ROWS)
                   for src in (weights, mom1, mom2)]
    xc, small_state = lax.optimization_barrier((xc, small_state))
    p_bf = p.reshape(DEPTH, T, 256).astype(BF16)
    h = _norm_first(xc, norm_mix[0][None])
    saved = []
    for l in range(DEPTH):
        g_next = norm_mix[l + 1][None] if l + 1 < DEPTH else norm_final[None]
        h, conf_g, sc_g = lax.optimization_barrier((h, conf_full[l], sc_full[l]))
        conf_full[l] = conf_g.transpose(1, 0, 2).reshape(CONF_K, W)
        sc_full[l] = sc_g.transpose(1, 0, 2).reshape(SC_K, W)
        xc, h, sv = _layer_fwd(xc, h, p_bf[l], gathered[l], small_params(l), g_next)
        saved.append(sv)

    dxc, dg_final, loss_part = _loss_head(xc, loss_target.reshape(T, D), norm_final[None])
    loss = lax.psum(loss_part[0, 0], ("x", "y", "c"))
    small_grads = [None] * DEPTH
    rcs = {n: [None] * DEPTH for n in BIG}
    pipeline = _GradientPipeline(c_idx, rcs)
    for l in reversed(range(DEPTH)):
        dxc, small_grads[l] = _layer_bwd(dxc, saved[l], gathered[l], small_params(l),
                                         lambda dw, names, value, l=l: pipeline.submit(dw, names, l, value),
                                         early_group=(l == 0))

    def adam_sharded(first_layer, n_layers, partial, tag, names=BIG):
        outs = {}
        for n in names:
            shp = (DEPTH,) + BIG_SHARD[n]
            outs[n] = _adam_sharded(rcs[n][first_layer:first_layer + n_layers], weights[n].reshape(shp),
                                    mom1[n].reshape(shp), mom2[n].reshape(shp), ADAM_ROWS[n],
                                    "adam_%s_%s" % (n, tag), first_layer, None if partial is None else partial[n])
        return outs

    stacked = {n: jnp.stack([small_grads[l][n] for l in range(DEPTH)]) for n, _ in SMALL if n != "norm_final"}
    stacked["norm_final"] = dg_final[0]
    packed = _pack([stacked[n] for n, _ in SMALL], NDEV * SMALL_ROWS).reshape(NCHIP, 2, SMALL_ROWS, D)
    (pair_small,), _ = _rs_exchange([packed], [], name="rs_pair_small")
    q_small = _pair_sum([packed], [pair_small], c_idx, name="rs_pairsum_small")
    dxc, upper, q_small = lax.optimization_barrier((dxc, {n: rcs[n][1:] for n in BIG}, q_small))
    _, (chips_small,) = _rs_exchange([], q_small, name="rs_chip_small", seq_id=6)
    dxc, upper = pipeline.finish((dxc, upper))
    for n in BIG:
        rcs[n][1:] = upper[n]
    partial = adam_sharded(1, DEPTH - 1, None, "upper")
    last = adam_sharded(0, 1, partial, "last", names=BIG[:1])
    partial = {n: partial[n] for n in BIG[1:]}
    last, partial, chips_small = lax.optimization_barrier((last, partial, chips_small))
    reduced_slot = _sum4(chips_small)
    reduced = _all_gather([reduced_slot], name="ag_small", seq_id=7)[0]
    small_full = dict(zip([n for n, _ in SMALL], _unpack(reduced, [s for _, s in SMALL])))
    grads, deltas, new_m, new_v = {}, {}, {}, {}
    direct, row = [], 0
    for n, shape in SMALL:
        if len(shape) == 1 or shape[-2] == DEPTH:
            direct.append((n, row, (1,) * (2 - len(shape)) + tuple(shape)))
        size = 1
        for s in shape:
            size *= s
        row += size // D
    (d_p, m_p, v_p), own_shape = _adam_packed(reduced.reshape(NDEV * SMALL_ROWS, D), *small_state, direct)
    small_shapes = [s for _, s in SMALL]

    def own_channels(n, full):
        return lax.dynamic_slice_in_dim(full, me * (W // NDEV), W // NDEV, axis=2) if n in CHANNEL_SHARDED else full

    for n, d_, m_, v_ in zip(small_names, _unpack(d_p, small_shapes), _unpack(m_p, small_shapes),
                             _unpack(v_p, small_shapes)):
        if n in own_shape:
            grads[n], deltas[n], new_m[n], new_v[n] = [a.reshape(weights[n].shape) for a in own_shape[n]]
        else:
            grads[n], deltas[n], new_m[n], new_v[n] = (own_channels(n, small_full[n]), own_channels(n, d_),
                                                       own_channels(n, m_), own_channels(n, v_))

    last.update(adam_sharded(0, 1, partial, "last", names=BIG[1:]))
    for n, (g_, d_, m_, v_) in last.items():
        full = weights[n].shape
        grads[n], deltas[n], new_m[n], new_v[n] = g_.reshape(full), d_.reshape(full), m_.reshape(full), v_.reshape(full)

    order = ("norm_mix", "w_in", "conf_dw", "conf_dw_b", "conf_ln_g", "conf_ln_b", "pool_w", "pool_scale", "sc_conv",
             "gmlp_ln_g", "gmlp_ln_b", "gmlp_ws", "gmlp_bs", "w_branch", "w_out", "norm_mlp", "w_up", "w_down",
             "norm_ple", "w_ple", "w_pleg", "norm_final")
    return (loss, dxc.reshape(1, T, D), *[grads[n] for n in order], *[deltas[n] for n in order],
            *[new_m[n] for n in order], *[new_v[n] for n in order])
```

```python
import functools

import jax
import jax.numpy as jnp
from jax import lax
from jax.experimental import pallas as pl
from jax.experimental.pallas import tpu as pltpu
from jax.experimental.pallas import tpu_sc as plsc

F32 = jnp.float32
BF16 = jnp.bfloat16

DEPTH = 4
T = 2048
D = 1024
W = 512
NDEV = 8
NCHIP = 4
EPS = 1e-6
CONF_K = 31
SC_K = 3
POOL_WINDOWS = (2, 4, 8, 16)
GW = 128
HB = 32
HA = 32
COLS_IN = 8192
MIX_COLS = 4096

ADAM_LR = 0.001
ADAM_B1 = 0.9
ADAM_B2 = 0.999
ADAM_EPS = 1e-08
ADAM_WD = 0.01
ADAM_STEP = 10

VMEM_LIMIT_BYTES = 56 * 1024 * 1024
MESH = pl.DeviceIdType.MESH


def _cp(*sem):
    return pltpu.CompilerParams(dimension_semantics=tuple(sem), vmem_limit_bytes=VMEM_LIMIT_BYTES)


def _sig(x):
    return jax.nn.sigmoid(x)


def _rms(x, g):
    r = lax.rsqrt(jnp.mean(x * x, axis=-1, keepdims=True) + EPS)
    return x * r * g


def _rms_bwd(dh, x, g, dres):
    r = lax.rsqrt(jnp.mean(x * x, axis=-1, keepdims=True) + EPS)
    xh = x * r
    u = dh * g
    dx = r * (u - xh * jnp.mean(u * xh, axis=-1, keepdims=True)) + dres
    dg = jnp.sum(dh * xh, axis=0, keepdims=True)
    return dx, dg


def _ln_stats(x):
    mu = jnp.mean(x, axis=-1, keepdims=True)
    xc = x - mu
    rstd = lax.rsqrt(jnp.mean(xc * xc, axis=-1, keepdims=True) + EPS)
    return xc * rstd, rstd


def _ln_bwd(dxh, xh, rstd):
    return rstd * (dxh - jnp.mean(dxh, axis=-1, keepdims=True) - xh * jnp.mean(dxh * xh, axis=-1, keepdims=True))


def _rowsum(x):
    return jnp.sum(x, axis=0, keepdims=True)


EPI_ROWS = 256


def _relu2_bf16(up):
    r = jnp.maximum(up.astype(F32), 0.0)
    return (r * r).astype(BF16)


def _mm(a, b3, *, mode, name, outs, trans_b=False, tm=512, tiles=(), params=(), epi=None, reds=(), a_pre=None,
        stream_first=False):
    t_, ka = a.shape
    nj, r, c = b3.shape
    kb, nb = (c, r) if trans_b else (r, c)
    nt = t_ // tm
    out_mode = mode == "out"
    full = mode == "full"
    assert trans_b or not full
    if out_mode:
        assert ka == kb and not reds
        grid = (nj, nt)
        a_map = lambda g0, g1: (g1, 0)
        b_map = lambda g0, g1: (g0, 0, 0)
        t_map = lambda g0, g1: (g1, g0)
        width = nj * nb
    else:
        assert ka == nj * kb
        grid = (nt, 1 if full else nj)
        a_map = lambda g0, g1: (g0, g1)
        b_map = lambda g0, g1: (g1, 0, 0)
        t_map = lambda g0, g1: (g0, 0)
        width = nb
    n_t, n_p, n_o, n_r = len(tiles), len(params), len(outs), len(reds)
    use_acc = (not out_mode) and nj > 1 and not full
    dims = (((1,), (1,)), ((), ())) if trans_b else (((1,), (0,)), ((), ()))

    def body(a_ref, b_ref, *rest):
        t_refs = rest[:n_t]
        p_refs = rest[n_t:n_t + n_p]
        o_refs = rest[n_t + n_p:n_t + n_p + n_o]
        r_refs = rest[n_t + n_p + n_o:n_t + n_p + n_o + n_r]
        i = pl.program_id(1 if out_mode else 0)
        a_val = a_ref[...] if a_pre is None else a_pre(a_ref[...])
        if full:
            b_all, b_sems = rest[-2 - stream_first], rest[-1 - stream_first]

            def weight_copies():
                return [pltpu.make_async_copy(b_ref.at[j], b_all.at[:, j * c:(j + 1) * c], b_sems.at[j])
                        for j in range(nj)]

            if stream_first:
                part = rest[-1]

                @pl.when(i == 0)
                def _():
                    cps = weight_copies()
                    for cp in cps:
                        cp.start()
                    acc = None
                    for j, cp in enumerate(cps):
                        cp.wait()
                        term = lax.dot_general(a_val[:, j * c:(j + 1) * c], b_all[:, j * c:(j + 1) * c], dims,
                                               preferred_element_type=F32)
                        acc = term if acc is None else acc + term
                    part[...] = acc

                @pl.when(i > 0)
                def _():
                    part[...] = lax.dot_general(a_val, b_all[...], dims, preferred_element_type=F32)
            else:
                @pl.when(i == 0)
                def _():
                    cps = weight_copies()
                    for cp in cps:
                        cp.start()
                    for cp in cps:
                        cp.wait()

                part = lax.dot_general(a_val, b_all[...], dims, preferred_element_type=F32)
        else:
            part = lax.dot_general(a_val, b_ref[...], dims, preferred_element_type=F32)

        def finish(acc_rows):
            totals = [None] * n_r
            for r0 in range(0, tm, min(tm, EPI_ROWS)):
                rows = slice(r0, r0 + min(tm, EPI_ROWS))
                if epi is None:
                    res, rr = (acc_rows(rows),), ()
                else:
                    res, rr = epi(acc_rows(rows), [t[rows, :] for t in t_refs], [p[...] for p in p_refs])
                for o_ref, val in zip(o_refs, res):
                    o_ref[rows, :] = val.astype(o_ref.dtype)
                totals = [val if tot is None else tot + val for tot, val in zip(totals, rr)]
            for r_ref, val in zip(r_refs, totals):
                @pl.when(i == 0)
                def _():
                    r_ref[...] = val

                @pl.when(i > 0)
                def _():
                    r_ref[...] += val

        if use_acc:
            acc_ref = rest[-1]
            j = pl.program_id(1)

            @pl.when(j == 0)
            def _():
                acc_ref[...] = part

            @pl.when(jnp.logical_and(j > 0, j < nj - 1))
            def _():
                acc_ref[...] += part

            @pl.when(j == nj - 1)
            def _():
                finish(lambda rows: acc_ref[rows, :] + part[rows])
        else:
            finish(lambda rows: part[rows, :])

    const2 = lambda g0, g1: (0, 0)
    if full:
        in_specs = [pl.BlockSpec((tm, ka), a_map), pl.BlockSpec(memory_space=pl.ANY)]
        scratch = [pltpu.VMEM((r, nj * c), b3.dtype), pltpu.SemaphoreType.DMA((nj,))]
        scratch += [pltpu.VMEM((tm, nb), F32)] if stream_first else []
    else:
        in_specs = [pl.BlockSpec((tm, kb), a_map), pl.BlockSpec((None, r, c), b_map)]
        scratch = [pltpu.VMEM((tm, nb), F32)] if use_acc else []
    in_specs += [pl.BlockSpec((tm, t.shape[1] // nj if out_mode else t.shape[1]), t_map) for t in tiles]
    in_specs += [pl.BlockSpec(p.shape, lambda g0, g1, nd=p.ndim: (0,) * nd) for p in params]
    out_specs = [pl.BlockSpec((tm, nb), t_map) for _ in outs] + [pl.BlockSpec((1, w), const2) for w in reds]
    out_shape = [jax.ShapeDtypeStruct((t_, width), dt) for dt in outs]
    out_shape += [jax.ShapeDtypeStruct((1, w), F32) for w in reds]
    res = pl.pallas_call(
        body, name=name, grid=grid, in_specs=in_specs, out_specs=out_specs, out_shape=out_shape,
        scratch_shapes=scratch, compiler_params=_cp("arbitrary", "arbitrary"),
    )(a, b3, *tiles, *params)
    return res


def _mm_tn(a, g, *, nj, split, name, out_dtype=BF16, a_pre=None):
    t_ = a.shape[0]
    if split == "col":
        r, c = a.shape[1], g.shape[1] // nj
        a_spec = pl.BlockSpec((t_, r), lambda j: (0, 0))
        g_spec = pl.BlockSpec((t_, c), lambda j: (0, j))
    else:
        r, c = a.shape[1] // nj, g.shape[1]
        a_spec = pl.BlockSpec((t_, r), lambda j: (0, j))
        g_spec = pl.BlockSpec((t_, c), lambda j: (0, 0))

    def body(a_ref, g_ref, o_ref):
        a_val = a_ref[...] if a_pre is None else a_pre(a_ref[...])
        o_ref[...] = lax.dot_general(a_val, g_ref[...], (((0,), (0,)), ((), ())),
                                     preferred_element_type=F32).astype(o_ref.dtype)

    return pl.pallas_call(
        body, name=name, grid=(nj,), in_specs=[a_spec, g_spec],
        out_specs=pl.BlockSpec((None, r, c), lambda j: (j, 0, 0)),
        out_shape=jax.ShapeDtypeStruct((nj, r, c), out_dtype),
        compiler_params=_cp("arbitrary"),
    )(a, g)


def _epi_res_norm(acc, tiles, params):
    x_new = tiles[0] + acc
    return (x_new, _rms(x_new, params[0])), ()


def _epi_ple(acc, tiles, params):
    x_old, p_tile = tiles
    g_next, w_ple8 = params
    pe = jnp.concatenate([jnp.dot(p_tile, w_ple8[j], preferred_element_type=F32) for j in range(NDEV)], axis=1)
    x_new = x_old + pe * _sig(acc)
    return (x_new, acc, _rms(x_new, g_next), pe), ()


def _epi_rms_bwd(acc, tiles, params):
    dx, dg = _rms_bwd(acc, tiles[0], params[0], tiles[1])
    return (dx, dx), (dg,)


def _epi_dup(acc, tiles, params):
    return (acc * (2.0 * jnp.maximum(tiles[0].astype(F32), 0.0)),), ()


def _tri_mask():
    row = lax.broadcasted_iota(jnp.int32, (GW, GW), 0)
    col = lax.broadcasted_iota(jnp.int32, (GW, GW), 1)
    return row >= col


def _small_specs(sp_list):
    return [pl.BlockSpec(p.shape, (lambda i: (0, 0)) if p.ndim == 2 else (lambda i: (0, 0, 0))) for p in sp_list]


SUBLANES = 8


def _tap_sum(src, w_ref, taps, rows, stage):
    groups = {}
    for off, k in taps:
        groups.setdefault(off % SUBLANES, []).append((off - off % SUBLANES, k))
    out = None
    for res, members in sorted(groups.items()):
        n = rows if res == 0 else rows + SUBLANES
        part = None
        for base, k in members:
            term = w_ref[k:k + 1, :] * src[pl.ds(base, n), :]
            part = term if part is None else part + term
        if res:
            stage[0:n, :] = part
            part = stage[pl.ds(res, rows), :]
        out = part if out is None else out + part
    return out


def _tap_grads(grad, src, offsets, rows, stage, out_ref):
    pad = SUBLANES
    stage[0:pad, :] = jnp.zeros((pad, grad.shape[1]), F32)
    stage[pad:pad + rows, :] = grad
    stage[pad + rows:2 * pad + rows, :] = jnp.zeros((pad, grad.shape[1]), F32)
    groups = {}
    for k, off in enumerate(offsets):
        groups.setdefault(off % SUBLANES, []).append((off - off % SUBLANES, k))
    for res, members in sorted(groups.items()):
        shifted = stage[pl.ds(pad - res, rows + pad), :]
        for base, k in members:
            out_ref[k:k + 1, :] += _rowsum(shifted * src[pl.ds(base, rows + pad), :])


def _mixer_params(sp):
    return [sp["cw"], sp["cb"], sp["lg"], sp["lb"], sp["pw"], sp["ps"], sp["sc"], sp["gg"], sp["gb"], sp["ws"], sp["bst"]]


def _mixer_fwd(proj, sp, tm=256):
    nt = T // tm
    per = tm // HB

    conv_taps = [(HB - (CONF_K - 1) + k, k) for k in range(CONF_K)]

    def body(main_ref, halo_ref, cw, cb, lg, lb, pw, ps, sc, gg, gb, ws, bst, y_ref, ca_ref, ext, stage):
        i = pl.program_id(0)
        keep = (i > 0).astype(F32)

        def mcol(c0):
            return main_ref[:, c0:c0 + W].astype(F32)

        def hcol(c0):
            return halo_ref[:, c0:c0 + W].astype(F32)

        ext[0:HB, :] = hcol(0) * _sig(hcol(W)) * keep
        ext[HB:HB + tm, :] = mcol(0) * _sig(mcol(W))
        ca = (_tap_sum(ext, cw, conv_taps, tm, stage) + cb[...]).astype(BF16)
        ca_ref[...] = ca
        xh, _ = _ln_stats(ca.astype(F32))
        n = xh * lg[...] + lb[...]
        y_ref[:, 0:W] = (n * _sig(n)).astype(BF16)

        pin = mcol(1024)
        ext[0:HB, :] = hcol(1024) * keep
        ext[HB:HB + tm, :] = pin
        pos = (i * tm + lax.broadcasted_iota(jnp.int32, (tm, 1), 0) + 1).astype(F32)
        for g, w in enumerate(POOL_WINDOWS):
            lo = g * GW
            s = ext[pl.ds(HB, tm), lo:lo + GW]
            for j in range(1, w):
                s = s + ext[pl.ds(HB - j, tm), lo:lo + GW]
            pooled = s / jnp.minimum(pos, float(w)) - pin[:, lo:lo + GW]
            mixed = jnp.dot(pooled.astype(BF16), pw[g].astype(BF16), preferred_element_type=F32)
            y_ref[:, W + lo:W + lo + GW] = (mixed * ps[:, lo:lo + GW]).astype(BF16)

        ext[0:HB, :] = hcol(2048) * hcol(2560) * keep
        ext[HB:HB + tm, :] = mcol(2048) * mcol(2560)
        cv = sc[0:1, :] * ext[pl.ds(HB - 2, tm), :]
        cv = cv + sc[1:2, :] * ext[pl.ds(HB - 1, tm), :]
        cv = cv + sc[2:3, :] * ext[pl.ds(HB, tm), :]
        y_ref[:, 2 * W:3 * W] = (mcol(1536) * cv).astype(BF16)

        vh, _ = _ln_stats(mcol(3584))
        vn = (vh * gg[...] + gb[...]).astype(BF16)
        u = mcol(3072)
        tri = _tri_mask()
        for g in range(4):
            lo = g * GW
            wm = jnp.where(tri, ws[g], 0.0).astype(BF16)
            for c in range(tm // GW):
                r0 = c * GW
                sg = jnp.dot(wm, vn[r0:r0 + GW, lo:lo + GW], preferred_element_type=F32) + bst[:, g:g + 1]
                y_ref[r0:r0 + GW, 3 * W + lo:3 * W + lo + GW] = (u[r0:r0 + GW, lo:lo + GW] * sg).astype(BF16)

    plist = _mixer_params(sp)
    in_specs = [pl.BlockSpec((tm, MIX_COLS), lambda i: (i, 0)),
                pl.BlockSpec((HB, MIX_COLS), lambda i: (jnp.maximum(i * per - 1, 0), 0))]
    in_specs += _small_specs(plist)
    return pl.pallas_call(
        body, name="f_mixers", grid=(nt,), in_specs=in_specs,
        out_specs=[pl.BlockSpec((tm, 4 * W), lambda i: (i, 0)), pl.BlockSpec((tm, W), lambda i: (i, 0))],
        out_shape=[jax.ShapeDtypeStruct((T, 4 * W), BF16), jax.ShapeDtypeStruct((T, W), BF16)],
        scratch_shapes=[pltpu.VMEM((HB + tm, W), F32), pltpu.VMEM((tm + SUBLANES, W), F32)],
        compiler_params=_cp("arbitrary"),
    )(proj, proj, *plist)


def _assemble_wb(wb8_ref, wbf_ref):
    for k in range(4):
        for j in range(NDEV):
            wbf_ref[k, :, j * GW:(j + 1) * GW] = wb8_ref[j, k]


def _merge_fwd(y, proj, wb8, tm=256):
    nt = T // tm

    def body(y_ref, gate_ref, wb8_ref, z_ref, m_ref, wbf):
        @pl.when(pl.program_id(0) == 0)
        def _():
            _assemble_wb(wb8_ref, wbf)

        m = jnp.zeros((tm, D), F32)
        for k in range(4):
            zk = jnp.dot(y_ref[:, k * W:(k + 1) * W], wbf[k], preferred_element_type=F32)
            z_ref[:, k * D:(k + 1) * D] = zk.astype(BF16)
            m = m + _sig(gate_ref[:, k * D:(k + 1) * D].astype(F32)) * zk
        m_ref[...] = m.astype(BF16)

    return pl.pallas_call(
        body, name="f_merge", grid=(nt,),
        in_specs=[pl.BlockSpec((tm, 4 * W), lambda i: (i, 0)),
                  pl.BlockSpec((tm, 4 * D), lambda i: (i, 1)),
                  pl.BlockSpec(wb8.shape, lambda i: (0, 0, 0, 0))],
        out_specs=[pl.BlockSpec((tm, 4 * D), lambda i: (i, 0)), pl.BlockSpec((tm, D), lambda i: (i, 0))],
        out_shape=[jax.ShapeDtypeStruct((T, 4 * D), BF16), jax.ShapeDtypeStruct((T, D), BF16)],
        scratch_shapes=[pltpu.VMEM((4, W, D), BF16)],
        compiler_params=_cp("arbitrary"),
    )(y, proj, wb8)


def _merge_bwd(dm, z, proj, y, wb8, tm=256):
    nt = T // tm

    def body(dm_ref, z_ref, gate_ref, y_ref, wb8_ref, dp_ref, dy_ref, dwb_ref, wbf, acc):
        i = pl.program_id(0)

        @pl.when(i == 0)
        def _():
            _assemble_wb(wb8_ref, wbf)

        dmv = dm_ref[...].astype(F32)
        for k in range(4):
            s = _sig(gate_ref[:, k * D:(k + 1) * D].astype(F32))
            dzk = (dmv * s).astype(BF16)
            dp_ref[:, k * D:(k + 1) * D] = (dmv * z_ref[:, k * D:(k + 1) * D].astype(F32) * s * (1.0 - s)).astype(BF16)
            dyk = lax.dot_general(dzk, wbf[k], (((1,), (1,)), ((), ())), preferred_element_type=F32)
            dy_ref[:, k * W:(k + 1) * W] = dyk.astype(BF16)
            part = lax.dot_general(y_ref[:, k * W:(k + 1) * W], dzk, (((0,), (0,)), ((), ())),
                                   preferred_element_type=F32)

            @pl.when(i == 0)
            def _():
                acc[k] = part

            @pl.when(i > 0)
            def _():
                acc[k] += part

        @pl.when(i == nt - 1)
        def _():
            for k in range(4):
                for j in range(NDEV):
                    dwb_ref[j, k] = acc[k, :, j * GW:(j + 1) * GW].astype(BF16)

    return pl.pallas_call(
        body, name="b_merge", grid=(nt,),
        in_specs=[pl.BlockSpec((tm, D), lambda i: (i, 0)),
                  pl.BlockSpec((tm, 4 * D), lambda i: (i, 0)),
                  pl.BlockSpec((tm, 4 * D), lambda i: (i, 1)),
                  pl.BlockSpec((tm, 4 * W), lambda i: (i, 0)),
                  pl.BlockSpec(wb8.shape, lambda i: (0, 0, 0, 0))],
        out_specs=[pl.BlockSpec((tm, 4 * D), lambda i: (i, 1)),
                   pl.BlockSpec((tm, 4 * W), lambda i: (i, 0)),
                   pl.BlockSpec(wb8.shape, lambda i: (0, 0, 0, 0))],
        out_shape=[jax.ShapeDtypeStruct((T, COLS_IN), BF16),
                   jax.ShapeDtypeStruct((T, 4 * W), BF16),
                   jax.ShapeDtypeStruct(wb8.shape, BF16)],
        scratch_shapes=[pltpu.VMEM((4, W, D), BF16), pltpu.VMEM((4, W, D), F32)],
        compiler_params=_cp("arbitrary"),
    )(dm, z, proj, y, wb8)


def _mixer_bwd(proj, ca_saved, dy, dproj, sp, tm=256):
    nt = T // tm
    per = tm // HB
    ne = tm + HA
    last_blk = T // HA - 1
    conv_taps = [(HB - (CONF_K - 1) + k, k) for k in range(CONF_K)]

    def body(main_ref, hb_ref, ha_ref, ca_ref, cah_ref, dy_ref, dyh_ref, cw, cb, lg, lb, pw, ps, sc, gg, gb, ws, bst,
             dp_any, dp_ref, dcw_ref, dsc_ref, vec_ref, dpw_ref, dws_ref, dbs_ref, e1, e2, e3, stage):
        del dp_any, cb
        i = pl.program_id(0)
        keep_b = (i > 0).astype(F32)
        keep_a = (i < nt - 1).astype(F32)

        @pl.when(i == 0)
        def _():
            dcw_ref[...] = jnp.zeros_like(dcw_ref)
            dsc_ref[...] = jnp.zeros_like(dsc_ref)
            vec_ref[...] = jnp.zeros_like(vec_ref)
            dpw_ref[...] = jnp.zeros_like(dpw_ref)
            dws_ref[...] = jnp.zeros_like(dws_ref)
            dbs_ref[...] = jnp.zeros_like(dbs_ref)

        def mcol(c0):
            return main_ref[:, c0:c0 + W].astype(F32)

        def hbcol(c0):
            return hb_ref[:, c0:c0 + W].astype(F32)

        def hacol(c0):
            return ha_ref[:, c0:c0 + W].astype(F32)

        def load_dy(c0):
            e2[0:tm, :] = dy_ref[:, c0:c0 + W].astype(F32)
            e2[tm:ne, :] = dyh_ref[:, c0:c0 + W].astype(F32) * keep_a

        a = mcol(0)
        sa = _sig(mcol(W))
        e1[0:HB, :] = hbcol(0) * _sig(hbcol(W)) * keep_b
        e1[HB:HB + tm, :] = a * sa
        e1[HB + tm:HB + tm + SUBLANES, :] = jnp.zeros((SUBLANES, W), F32)
        e2[0:tm, :] = ca_ref[...].astype(F32)
        e2[tm:ne, :] = cah_ref[...].astype(F32)
        xh, rstd = _ln_stats(e2[0:ne, :])
        nn = xh * lg[...] + lb[...]
        s = _sig(nn)
        load_dy(0)
        dn = e2[0:ne, :] * (s * (1.0 + nn * (1.0 - s)))
        vec_ref[1:2, :] += _rowsum(dn[0:tm] * xh[0:tm])
        vec_ref[2:3, :] += _rowsum(dn[0:tm])
        dca = _ln_bwd(dn * lg[...], xh, rstd)
        e3[0:ne, :] = dca
        dmain = dca[0:tm]
        vec_ref[0:1, :] += _rowsum(dmain)
        _tap_grads(dmain, e1, [off for off, _ in conv_taps], tm, stage, dcw_ref)
        dglu = _tap_sum(e3, cw, [(CONF_K - 1 - k, k) for k in range(CONF_K)], tm, stage)
        dp_ref[:, 0:W] = (dglu * sa).astype(BF16)
        dp_ref[:, W:2 * W] = (dglu * a * sa * (1.0 - sa)).astype(BF16)

        pin = mcol(1024)
        e1[0:HB, :] = hbcol(1024) * keep_b
        e1[HB:HB + tm, :] = pin
        load_dy(W)
        dyb = e2[0:ne, :]
        pos_m = (i * tm + lax.broadcasted_iota(jnp.int32, (tm, 1), 0) + 1).astype(F32)
        pos_e = (i * tm + lax.broadcasted_iota(jnp.int32, (ne, 1), 0) + 1).astype(F32)
        for g, w in enumerate(POOL_WINDOWS):
            lo = g * GW
            acc = e1[pl.ds(HB, tm), lo:lo + GW]
            for j in range(1, w):
                acc = acc + e1[pl.ds(HB - j, tm), lo:lo + GW]
            pooled = (acc / jnp.minimum(pos_m, float(w)) - pin[:, lo:lo + GW]).astype(BF16)
            pwb = pw[g].astype(BF16)
            mixed = jnp.dot(pooled, pwb, preferred_element_type=F32)
            dyb_g = dyb[:, lo:lo + GW]
            vec_ref[3:4, lo:lo + GW] += _rowsum(dyb_g[0:tm] * mixed)
            dmb = (dyb_g * ps[:, lo:lo + GW]).astype(BF16)
            dpw_ref[g] += lax.dot_general(pooled, dmb[0:tm], (((0,), (0,)), ((), ())), preferred_element_type=F32)
            dpool = lax.dot_general(dmb, pwb, (((1,), (1,)), ((), ())), preferred_element_type=F32)
            e3[0:ne, lo:lo + GW] = dpool / jnp.minimum(pos_e, float(w))
            back = e3[pl.ds(0, tm), lo:lo + GW]
            for j in range(1, w):
                back = back + e3[pl.ds(j, tm), lo:lo + GW]
            dp_ref[:, 1024 + lo:1024 + lo + GW] = (back - dpool[0:tm]).astype(BF16)

        cg = mcol(2048)
        hx = mcol(2560)
        e1[0:HB, :] = hbcol(2048) * hbcol(2560) * keep_b
        e1[HB:HB + tm, :] = cg * hx
        load_dy(2 * W)
        dyc = e2[0:tm, :]
        dconv = dyc * mcol(1536)
        e3[0:tm, :] = dconv
        e3[tm:ne, :] = e2[tm:ne, :] * hacol(1536)
        cv = sc[0:1, :] * e1[pl.ds(HB - 2, tm), :]
        for k in range(1, SC_K):
            cv = cv + sc[k:k + 1, :] * e1[pl.ds(HB - 2 + k, tm), :]
        dp_ref[:, 1536:2048] = (dyc * cv).astype(BF16)
        for k in range(SC_K):
            dsc_ref[k:k + 1, :] += _rowsum(dconv * e1[pl.ds(HB - 2 + k, tm), :])
        dq = sc[0:1, :] * e3[pl.ds(2, tm), :]
        for k in range(1, SC_K):
            dq = dq + sc[k:k + 1, :] * e3[pl.ds(2 - k, tm), :]
        dp_ref[:, 2048:2560] = (dq * hx).astype(BF16)
        dp_ref[:, 2560:3072] = (dq * cg).astype(BF16)

        u = mcol(3072)
        vh, vr = _ln_stats(mcol(3584))
        vn = (vh * gg[...] + gb[...]).astype(BF16)
        dyd = dy_ref[:, 3 * W:4 * W].astype(F32)
        tri = _tri_mask()
        for g in range(4):
            lo = g * GW
            wm = jnp.where(tri, ws[g], 0.0).astype(BF16)
            dws_g = jnp.zeros((GW, GW), F32)
            dbs_g = jnp.zeros((GW, 1), F32)
            for c in range(tm // GW):
                r0 = c * GW
                blk = vn[r0:r0 + GW, lo:lo + GW]
                sg = jnp.dot(wm, blk, preferred_element_type=F32) + bst[:, g:g + 1]
                dyd_b = dyd[r0:r0 + GW, lo:lo + GW]
                dp_ref[r0:r0 + GW, 3072 + lo:3072 + lo + GW] = (dyd_b * sg).astype(BF16)
                dsg = dyd_b * u[r0:r0 + GW, lo:lo + GW]
                dsgb = dsg.astype(BF16)
                dbs_g = dbs_g + jnp.sum(dsg, axis=-1, keepdims=True)
                dws_g = dws_g + lax.dot_general(dsgb, blk, (((1,), (1,)), ((), ())), preferred_element_type=F32)
                e1[r0:r0 + GW, lo:lo + GW] = lax.dot_general(wm, dsgb, (((0,), (0,)), ((), ())),
                                                             preferred_element_type=F32)
            dws_ref[g] += jnp.where(tri, dws_g, 0.0)
            dbs_ref[g] += jnp.broadcast_to(dbs_g, (GW, GW))
        dvn = e1[0:tm, :]
        vec_ref[4:5, :] += _rowsum(dvn * vh)
        vec_ref[5:6, :] += _rowsum(dvn)
        dp_ref[:, 3584:4096] = _ln_bwd(dvn * gg[...], vh, vr).astype(BF16)

    plist = _mixer_params(sp)
    in_specs = [pl.BlockSpec((tm, MIX_COLS), lambda i: (i, 0)),
                pl.BlockSpec((HB, MIX_COLS), lambda i: (jnp.maximum(i * per - 1, 0), 0)),
                pl.BlockSpec((HA, MIX_COLS), lambda i: (jnp.minimum((i + 1) * per, last_blk), 0)),
                pl.BlockSpec((tm, W), lambda i: (i, 0)),
                pl.BlockSpec((HA, W), lambda i: (jnp.minimum((i + 1) * per, last_blk), 0)),
                pl.BlockSpec((tm, 4 * W), lambda i: (i, 0)),
                pl.BlockSpec((HA, 4 * W), lambda i: (jnp.minimum((i + 1) * per, last_blk), 0))]
    in_specs += _small_specs(plist)
    in_specs += [pl.BlockSpec(memory_space=pl.ANY)]
    z2 = lambda i: (0, 0)
    z3 = lambda i: (0, 0, 0)
    out_specs = [pl.BlockSpec((tm, MIX_COLS), lambda i: (i, 0)),
                 pl.BlockSpec((32, W), z2), pl.BlockSpec((8, W), z2), pl.BlockSpec((8, W), z2),
                 pl.BlockSpec((4, GW, GW), z3), pl.BlockSpec((4, GW, GW), z3), pl.BlockSpec((4, GW, GW), z3)]
    out_shape = [jax.ShapeDtypeStruct((T, COLS_IN), BF16),
                 jax.ShapeDtypeStruct((32, W), F32), jax.ShapeDtypeStruct((8, W), F32),
                 jax.ShapeDtypeStruct((8, W), F32),
                 jax.ShapeDtypeStruct((4, GW, GW), F32), jax.ShapeDtypeStruct((4, GW, GW), F32),
                 jax.ShapeDtypeStruct((4, GW, GW), F32)]
    n_in = 7 + len(plist)
    return pl.pallas_call(
        body, name="b_mixers", grid=(nt,), in_specs=in_specs, out_specs=out_specs, out_shape=out_shape,
        scratch_shapes=[pltpu.VMEM((HB + ne, W), F32), pltpu.VMEM((ne, W), F32), pltpu.VMEM((ne, W), F32),
                        pltpu.VMEM((ne + SUBLANES, W), F32)],
        input_output_aliases={n_in: 0},
        compiler_params=_cp("arbitrary"),
    )(proj, proj, proj, ca_saved, ca_saved, dy, dy, *plist, dproj)


def _norm_first(x, g, tm=512):
    def body(x_ref, g_ref, o_ref):
        o_ref[...] = _rms(x_ref[...], g_ref[...]).astype(BF16)

    return pl.pallas_call(
        body, name="f_norm0", grid=(T // tm,),
        in_specs=[pl.BlockSpec((tm, D), lambda i: (i, 0)), pl.BlockSpec((1, D), lambda i: (0, 0))],
        out_specs=pl.BlockSpec((tm, D), lambda i: (i, 0)),
        out_shape=jax.ShapeDtypeStruct((T, D), BF16), compiler_params=_cp("arbitrary"),
    )(x, g)


def _loss_head(x, target, g, tm=256):
    def body(x_ref, t_ref, g_ref, dx_ref, dg_ref, loss_ref):
        i = pl.program_id(0)
        x = x_ref[...]
        r = lax.rsqrt(jnp.mean(x * x, axis=-1, keepdims=True) + EPS)
        xh = x * r
        gv = g_ref[...]
        e = xh * gv - t_ref[...]
        dyv = e * (1.0 / D)
        part = jnp.sum(_rowsum(e * e), axis=-1, keepdims=True) * (0.5 / D)
        u = dyv * gv
        dx_ref[...] = r * (u - xh * jnp.mean(u * xh, axis=-1, keepdims=True))
        dgp = _rowsum(dyv * xh)

        @pl.when(i == 0)
        def _():
            dg_ref[...] = dgp
            loss_ref[...] = jnp.broadcast_to(part, (1, GW))

        @pl.when(i > 0)
        def _():
            dg_ref[...] += dgp
            loss_ref[...] += jnp.broadcast_to(part, (1, GW))

    return pl.pallas_call(
        body, name="loss_head", grid=(T // tm,),
        in_specs=[pl.BlockSpec((tm, D), lambda i: (i, 0)), pl.BlockSpec((tm, D), lambda i: (i, 0)),
                  pl.BlockSpec((1, D), lambda i: (0, 0))],
        out_specs=[pl.BlockSpec((tm, D), lambda i: (i, 0)), pl.BlockSpec((1, D), lambda i: (0, 0)),
                   pl.BlockSpec((1, GW), lambda i: (0, 0))],
        out_shape=[jax.ShapeDtypeStruct((T, D), F32), jax.ShapeDtypeStruct((1, D), F32),
                   jax.ShapeDtypeStruct((1, GW), F32)],
        compiler_params=_cp("arbitrary"),
    )(x, target, g)


def _out_bwd(dx2b, merged, w_out8, tm=512):
    nt = T // tm

    def body(dx_ref, mg_ref, w_ref, dm_ref, dw_ref, acc):
        i = pl.program_id(0)
        dx = dx_ref[...]
        dm_ref[...] = lax.dot_general(dx, w_ref[...], (((1,), (1,)), ((), ())),
                                      preferred_element_type=F32).astype(BF16)
        part = lax.dot_general(mg_ref[...], dx, (((0,), (0,)), ((), ())), preferred_element_type=F32)

        @pl.when(i == 0)
        def _():
            acc[...] = part

        @pl.when(i > 0)
        def _():
            acc[...] += part

        @pl.when(i == nt - 1)
        def _():
            for j in range(NDEV):
                dw_ref[j] = acc[j * GW:(j + 1) * GW, :].astype(BF16)

    tile = pl.BlockSpec((tm, D), lambda i: (i, 0))
    return pl.pallas_call(
        body, name="b_out", grid=(nt,),
        in_specs=[tile, tile, pl.BlockSpec((D, D), lambda i: (0, 0))],
        out_specs=[tile, pl.BlockSpec((NDEV, GW, D), lambda i: (0, 0, 0))],
        out_shape=[jax.ShapeDtypeStruct((T, D), BF16), jax.ShapeDtypeStruct((NDEV, GW, D), BF16)],
        scratch_shapes=[pltpu.VMEM((D, D), F32)],
        compiler_params=_cp("arbitrary"),
    )(dx2b, merged, w_out8.reshape(D, D))


def _ple_bwd(dx4, sv, w_pleg8, g_ple, tm=256):
    nt = T // tm
    ple_dim = sv["p"].shape[1]

    def body(dx_ref, gl_ref, pe_ref, x_ref, h_ref, p_ref, g_ref, wg_ref,
             dx3_ref, dx3b_ref, dg_ref, dwg_ref, dwp_ref, acc_g, acc_p):
        i = pl.program_id(0)
        d = dx_ref[...]
        s = _sig(gl_ref[...].astype(F32))
        dpe = (d * s).astype(BF16)
        dgl = (d * pe_ref[...].astype(F32) * s * (1.0 - s)).astype(BF16)
        dh = lax.dot_general(dgl, wg_ref[...], (((1,), (1,)), ((), ())), preferred_element_type=F32)
        dx, dgp = _rms_bwd(dh, x_ref[...], g_ref[...], d)
        dx3_ref[...] = dx
        dx3b_ref[...] = dx.astype(BF16)
        part_g = lax.dot_general(h_ref[...], dgl, (((0,), (0,)), ((), ())), preferred_element_type=F32)
        part_p = lax.dot_general(p_ref[...], dpe, (((0,), (0,)), ((), ())), preferred_element_type=F32)

        @pl.when(i == 0)
        def _():
            dg_ref[...] = dgp
            acc_g[...] = part_g
            acc_p[...] = part_p

        @pl.when(i > 0)
        def _():
            dg_ref[...] += dgp
            acc_g[...] += part_g
            acc_p[...] += part_p

        @pl.when(i == nt - 1)
        def _():
            for j in range(NDEV):
                dwg_ref[j] = acc_g[j * GW:(j + 1) * GW, :].astype(BF16)
                dwp_ref[j] = acc_p[:, j * GW:(j + 1) * GW].astype(BF16)

    tile = lambda w: pl.BlockSpec((tm, w), lambda i: (i, 0))
    const = lambda shp: pl.BlockSpec(shp, lambda i: (0,) * len(shp))
    return pl.pallas_call(
        body, name="b_ple", grid=(nt,),
        in_specs=[tile(D), tile(D), tile(D), tile(D), tile(D), tile(ple_dim), const((1, D)), const((D, D))],
        out_specs=[tile(D), tile(D), const((1, D)), const((NDEV, GW, D)), const((NDEV, ple_dim, GW))],
        out_shape=[jax.ShapeDtypeStruct((T, D), F32), jax.ShapeDtypeStruct((T, D), BF16),
                   jax.ShapeDtypeStruct((1, D), F32), jax.ShapeDtypeStruct((NDEV, GW, D), BF16),
                   jax.ShapeDtypeStruct((NDEV, ple_dim, GW), BF16)],
        scratch_shapes=[pltpu.VMEM((D, D), F32), pltpu.VMEM((ple_dim, D), F32)],
        compiler_params=_cp("arbitrary"),
    )(dx4, sv["gl"], sv["pe"], sv["x3"], sv["h3"], sv["p"], g_ple, w_pleg8.reshape(D, D))


def _layer_fwd(x, h1, p_bf, gw, sp, g_next):
    proj, = _mm(h1, gw["w_in"], mode="out", name="f_proj", outs=[BF16], tm=T)
    y, ca = _mixer_fwd(proj, sp)
    z, merged = _merge_fwd(y, proj, gw["w_branch"])
    x2, h2 = _mm(merged, gw["w_out"].reshape(1, D, D), mode="acc", name="f_out", outs=[F32, BF16], tm=T // 2,
                 tiles=[x], params=[sp["g_mlp"]], epi=_epi_res_norm)
    up, = _mm(h2, gw["w_up"], mode="out", name="f_up", outs=[BF16], tm=T)
    x3, h3 = _mm(up, gw["w_down"].reshape(1, 4 * D, D), mode="acc", name="f_down", outs=[F32, BF16], tm=T // 4,
                 tiles=[x2], params=[sp["g_ple"]], epi=_epi_res_norm, a_pre=_relu2_bf16)
    x4, gl, hn, pe = _mm(h3, gw["w_pleg"].reshape(1, D, D), mode="acc", name="f_gate", tm=T // 2,
                         outs=[F32, BF16, BF16, BF16], tiles=[x3, p_bf], params=[g_next, gw["w_ple"]], epi=_epi_ple)
    saved = dict(x=x, h1=h1, proj=proj, y=y, ca=ca, z=z, merged=merged, x2=x2, h2=h2, up=up, x3=x3, h3=h3,
                 pe=pe, gl=gl, p=p_bf)
    return x4, hn, saved


def _layer_bwd(dx4, sv, gw, sp, submit, early_group=False):
    dw = {}
    dx3, dx3b, dg_ple, dw["w_pleg"], dw["w_ple"] = _ple_bwd(dx4, sv, gw["w_pleg"], sp["g_ple"])
    dup, = _mm(dx3b, gw["w_down"], mode="out", trans_b=True, name="b_dact", outs=[BF16], tm=T,
               tiles=[sv["up"]], epi=_epi_dup)
    dw["w_down"] = _mm_tn(sv["up"], dx3b, nj=NDEV, split="row", name="b_dw_down", a_pre=_relu2_bf16)
    dw["w_up"] = _mm_tn(sv["h2"], dup, nj=NDEV, split="col", name="b_dw_up")
    if early_group:
        dw["w_up"], dup = lax.optimization_barrier((dw["w_up"], dup))
        dup = submit(dw, ("w_up", "w_down", "w_ple", "w_pleg"), dup)
    dx2, dx2b, dg_mlp = _mm(dup, gw["w_up"], mode="full", trans_b=True, name="b_dh2", tm=T // 4, stream_first=True,
                            outs=[F32, BF16], tiles=[sv["x2"], dx3], params=[sp["g_mlp"]], epi=_epi_rms_bwd, reds=[D])
    dm, dw["w_out"] = _out_bwd(dx2b, sv["merged"], gw["w_out"])
    dproj, dy, dw["w_branch"] = _merge_bwd(dm, sv["z"], sv["proj"], sv["y"], gw["w_branch"])
    dy = submit(dw, ("w_branch", "w_out") if early_group else BIG[1:], dy)
    dproj, dcw, dsc, vec, dpw, dws, dbs = _mixer_bwd(sv["proj"], sv["ca"], dy, dproj, sp)
    dw["w_in"] = _mm_tn(sv["h1"], dproj, nj=NDEV, split="col", name="b_dw_in")
    dw["w_in"], dproj = lax.optimization_barrier((dw["w_in"], dproj))
    dproj = submit(dw, BIG[:1], dproj)
    dx, dg_mix = _mm(dproj, gw["w_in"], mode="full", trans_b=True, name="b_dh1", outs=[F32], tm=T // 4,
                     tiles=[sv["x"], dx2], params=[sp["g_mix"]], epi=_epi_rms_bwd, reds=[D])
    small = dict(norm_mix=dg_mix[0], conf_dw=dcw[:CONF_K], conf_dw_b=vec[0], conf_ln_g=vec[1], conf_ln_b=vec[2],
                 pool_w=dpw, pool_scale=vec[3], sc_conv=dsc[:SC_K], gmlp_ln_g=vec[4], gmlp_ln_b=vec[5],
                 gmlp_ws=dws, gmlp_bs=dbs[:, :, 0], norm_mlp=dg_mlp[0], norm_ple=dg_ple[0])
    return dx, small


ANY = pl.BlockSpec(memory_space=pl.ANY)


def _mesh_pos():
    return lax.axis_index("x"), lax.axis_index("y"), lax.axis_index("c")


def _other_chips(x, y):
    return [(1 - x, y), (x, 1 - y), (1 - x, 1 - y)]


def _launch_comm(body, peers_of, operands, out_shapes, sems, name, seq_id):
    n_in, n_out = len(operands), len(out_shapes)
    if seq_id is None:
        return pl.pallas_call(body, name=name, in_specs=[ANY] * n_in, out_specs=[ANY] * n_out,
                              out_shape=out_shapes, scratch_shapes=sems)(*operands)

    def seq_body(*refs):
        peers = peers_of(*_mesh_pos())
        barrier = pltpu.get_barrier_semaphore()
        for peer in peers:
            pl.semaphore_signal(barrier, inc=1, device_id=peer, device_id_type=MESH)
        pl.semaphore_wait(barrier, len(peers))
        body(*refs)

    return pl.kernel(seq_body, name=name, out_type=out_shapes,
                     mesh=plsc.ScalarSubcoreMesh(axis_name="seq", num_cores=1), scratch_types=sems,
                     compiler_params=pltpu.CompilerParams(collective_id=seq_id))(*operands)


def _all_gather(shards, name, seq_id=None):
    n = len(shards)

    def body(*refs):
        s_refs, o_refs = refs[:n], refs[n:2 * n]
        send_sems, recv_sems, local_sems = refs[2 * n:]
        x, y, c = _mesh_pos()
        me = 4 * x + 2 * y + c
        here = (x, y, c)
        sibling = (x, y, 1 - c)
        chips = _other_chips(x, y)

        def slot(px, py, pc):
            return 4 * px + 2 * py + pc

        def copy(t, k, slot_idx, to, src=None):
            dst = o_refs[t].at[slot_idx]
            return pltpu.make_async_remote_copy(
                src_ref=dst if src is None else src, dst_ref=dst,
                send_sem=send_sems.at[t * 7 + k], recv_sem=recv_sems.at[t * 7 + k],
                device_id=to, device_id_type=MESH)

        mine = [pltpu.make_async_copy(s_refs[t], o_refs[t].at[me], local_sems.at[t]) for t in range(n)]
        for cp in mine:
            cp.start()
        first = []
        for t in range(n):
            for j, chip in enumerate(chips):
                first.append(copy(t, 1 + j, me, (*chip, c), src=s_refs[t]))
        for t in range(n):
            first.append(copy(t, 0, me, sibling, src=s_refs[t]))
        for cp in first:
            cp.start()
        passed = []
        for t in range(n):
            for j, chip in enumerate(chips):
                copy(t, 1 + j, slot(*chip, c), here).wait_recv()
                fwd = copy(t, 4 + j, slot(*chip, c), sibling)
                fwd.start()
                passed.append(fwd)
        for t in range(n):
            copy(t, 0, slot(x, y, 1 - c), here).wait_recv()
            for j, chip in enumerate(chips):
                copy(t, 4 + j, slot(*chip, 1 - c), here).wait_recv()
        for cp in first + passed:
            cp.wait_send()
        for cp in mine:
            cp.wait()

    def peers_of(x, y, c):
        return [(x, y, 1 - c)] + [(*chip, c) for chip in _other_chips(x, y)]

    return _launch_comm(
        body, peers_of, shards, [jax.ShapeDtypeStruct((NDEV,) + s.shape, s.dtype) for s in shards],
        [pltpu.SemaphoreType.DMA((7 * n,)), pltpu.SemaphoreType.DMA((7 * n,)), pltpu.SemaphoreType.DMA((n,))],
        name, seq_id)


def _rs_exchange(p4s, qs, name, seq_id=None):
    n_p, n_q = len(p4s), len(qs)

    def body(*refs):
        p_refs, q_refs = refs[:n_p], refs[n_p:n_p + n_q]
        rb_refs, rc_refs = refs[n_p + n_q:2 * n_p + n_q], refs[2 * n_p + n_q:2 * (n_p + n_q)]
        pair_send, pair_recv, chip_send, chip_recv, local_sems = refs[2 * (n_p + n_q):]
        x, y, c = _mesh_pos()
        a_idx = 2 * x + y
        chips = _other_chips(x, y)
        mine = [pltpu.make_async_copy(q_refs[t].at[a_idx], rc_refs[t].at[a_idx], local_sems.at[t])
                for t in range(n_q)]
        sends = []
        for t in range(n_q):
            for j, chip in enumerate(chips):
                sends.append(pltpu.make_async_remote_copy(
                    src_ref=q_refs[t].at[2 * chip[0] + chip[1]], dst_ref=rc_refs[t].at[a_idx],
                    send_sem=chip_send.at[t * 3 + j], recv_sem=chip_recv.at[t * 3 + j],
                    device_id=(*chip, c), device_id_type=MESH))
        pairs = [pltpu.make_async_remote_copy(
            src_ref=p_refs[t].at[:, 1 - c], dst_ref=rb_refs[t], send_sem=pair_send.at[t], recv_sem=pair_recv.at[t],
            device_id=(x, y, 1 - c), device_id_type=MESH) for t in range(n_p)]
        for cp in sends + mine + pairs:
            cp.start()
        for cp in pairs:
            cp.wait()
        for t in range(n_q):
            for j, chip in enumerate(chips):
                landed = rc_refs[t].at[2 * chip[0] + chip[1]]
                pltpu.make_async_remote_copy(
                    src_ref=landed, dst_ref=landed, send_sem=chip_send.at[t * 3 + j],
                    recv_sem=chip_recv.at[t * 3 + j], device_id=(x, y, c), device_id_type=MESH).wait_recv()
        for cp in sends:
            cp.wait_send()
        for cp in mine:
            cp.wait()

    def peers_of(x, y, c):
        peers = [(x, y, 1 - c)] if n_p else []
        return peers + ([(*chip, c) for chip in _other_chips(x, y)] if n_q else [])

    out_shapes = [jax.ShapeDtypeStruct((NCHIP,) + p.shape[2:], p.dtype) for p in p4s]
    out_shapes += [jax.ShapeDtypeStruct(q.shape, q.dtype) for q in qs]
    sems = [pltpu.SemaphoreType.DMA((max(n_p, 1),)), pltpu.SemaphoreType.DMA((max(n_p, 1),)),
            pltpu.SemaphoreType.DMA((max(3 * n_q, 1),)), pltpu.SemaphoreType.DMA((max(3 * n_q, 1),)),
            pltpu.SemaphoreType.DMA((max(n_q, 1),))]
    got = _launch_comm(body, peers_of, list(p4s) + list(qs), out_shapes, sems, name, seq_id)
    return got[:n_p], got[n_p:]


def _pair_sum(p4s, rbs, c_idx, name, nst=1):
    n = len(p4s)
    trs = [p.shape[2] // nst for p in p4s]

    def body(c_ref, *refs):
        del c_ref
        p_refs, r_refs, o_refs = refs[:n], refs[n:2 * n], refs[2 * n:]
        for p_ref, r_ref, o_ref in zip(p_refs, r_refs, o_refs):
            o_ref[...] = (p_ref[...].astype(F32) + r_ref[...].astype(F32)).astype(o_ref.dtype)

    in_specs = [pl.BlockSpec((None, None, tr, p.shape[3]), lambda b, i, c_ref: (b, c_ref[0], i, 0))
                for p, tr in zip(p4s, trs)]
    in_specs += [pl.BlockSpec((None, tr, p.shape[3]), lambda b, i, c_ref: (b, i, 0)) for p, tr in zip(p4s, trs)]
    out_specs = [pl.BlockSpec((None, tr, p.shape[3]), lambda b, i, c_ref: (b, i, 0)) for p, tr in zip(p4s, trs)]
    return pl.pallas_call(
        body, name=name,
        grid_spec=pltpu.PrefetchScalarGridSpec(num_scalar_prefetch=1, grid=(NCHIP, nst), in_specs=in_specs,
                                               out_specs=out_specs),
        out_shape=[jax.ShapeDtypeStruct((NCHIP,) + p.shape[2:], p.dtype) for p in p4s],
        compiler_params=_cp("arbitrary", "arbitrary"),
    )(c_idx, *p4s, *rbs)


class _GradientPipeline:
    def __init__(self, c_idx, results):
        self.c_idx, self.results, self.pending = c_idx, results, None

    def _sum_pending(self, chain):
        names, layer, p4s, rbs = self.pending
        qs = _pair_sum(p4s, rbs, self.c_idx, name="rs_pairsum_%d" % len(names))
        return lax.optimization_barrier((chain, qs))

    def submit(self, dw, names, layer, chain):
        qs, tag, seq_id = [], "pair", 3
        if self.pending is not None:
            chain, qs = self._sum_pending(chain)
            tag, seq_id = "pair_chip", 4
        p4s = [dw[n].reshape((NCHIP, 2) + BIG_SHARD[n]) for n in names]
        rbs, rcs = _rs_exchange(p4s, qs, name="rs_%s_%d" % (tag, len(names)), seq_id=seq_id)
        self._record(rcs)
        self.pending = (names, layer, p4s, rbs)
        return chain

    def finish(self, chain):
        chain, qs = self._sum_pending(chain)
        self._record(_rs_exchange([], qs, name="rs_chip_last", seq_id=5)[1])
        self.pending = None
        return chain

    def _record(self, rcs):
        if rcs:
            names, layer = self.pending[:2]
            for n, rc in zip(names, rcs):
                self.results[n][layer] = rc


def _adamw(w, g, m, v):
    m = ADAM_B1 * m + (1.0 - ADAM_B1) * g
    v = ADAM_B2 * v + (1.0 - ADAM_B2) * (g * g)
    m_hat = m / (1.0 - ADAM_B1 ** ADAM_STEP)
    v_hat = v / (1.0 - ADAM_B2 ** ADAM_STEP)
    delta = -ADAM_LR * (m_hat / (jnp.sqrt(v_hat) + ADAM_EPS) + ADAM_WD * w)
    return delta, m, v


def _adam_sharded(rcs, w, m, v, tr, name, first_layer, partial=None):
    _, r, c = w.shape
    nst = r // tr
    n_l = len(rcs)

    def body(*refs):
        rc_refs = refs[:n_l]
        w_ref, m_ref, v_ref = refs[n_l:n_l + 3]
        g_out, d_out, m_out, v_out = refs[-4:]
        layer = pl.program_id(0)
        for k, rc in enumerate(rc_refs):
            @pl.when(layer == k)
            def _():
                g = rc[0].astype(F32) + rc[1].astype(F32) + rc[2].astype(F32) + rc[3].astype(F32)
                delta, m_new, v_new = _adamw(w_ref[...], g, m_ref[...], v_ref[...])
                g_out[...] = g
                d_out[...] = delta
                m_out[...] = m_new
                v_out[...] = v_new

    rc_specs = [pl.BlockSpec((NCHIP, tr, c), lambda l, i, k=k: (0, jnp.where(l == k, i, 0), 0)) for k in range(n_l)]
    wspec = pl.BlockSpec((None, tr, c), lambda l, i: (first_layer + l, i, 0))
    carried = [] if partial is None else list(partial)
    return pl.pallas_call(
        body, name=name, grid=(n_l, nst),
        in_specs=rc_specs + [wspec] * 3 + [pl.BlockSpec(memory_space=pl.ANY)] * len(carried),
        out_specs=[wspec] * 4, out_shape=[jax.ShapeDtypeStruct(w.shape, F32)] * 4,
        input_output_aliases={n_l + 3 + k: k for k in range(len(carried))},
        compiler_params=_cp("arbitrary", "arbitrary"),
    )(*rcs, w, m, v, *carried)


def _adam_packed(g, w, m, v, direct):
    n_d = len(direct)

    def pieces(shape):
        width = shape[-1]
        count = 1
        for s in shape[:-1]:
            count *= s
        per_row = D // width
        out = []
        for k in range(count):
            idx = (k,) if len(shape) == 2 else (k // shape[1], k % shape[1])
            out.append((idx, k // per_row, (k % per_row) * width, width))
        return out

    def body(g_ref, w_ref, m_ref, v_ref, d_out, m_out, v_out, *outs):
        delta, m_new, v_new = _adamw(w_ref[...], g_ref[...], m_ref[...], v_ref[...])
        d_out[...] = delta
        m_out[...] = m_new
        v_out[...] = v_new
        for a, (_, row0, shape) in enumerate(direct):
            for src, dst in zip((g_ref, d_out, m_out, v_out), outs[4 * a:4 * a + 4]):
                for idx, row, lane0, width in pieces(shape):
                    piece = src[pl.ds(row0 + row, 1), lane0:lane0 + width]
                    if len(idx) == 1:
                        dst[pl.ds(idx[0], 1), :] = piece
                    else:
                        dst[idx[0], pl.ds(idx[1], 1), :] = piece

    out_shape = [jax.ShapeDtypeStruct(g.shape, F32)] * 3
    for _, _, shape in direct:
        out_shape += [jax.ShapeDtypeStruct(shape, F32)] * 4
    res = pl.pallas_call(body, name="adam_small", out_shape=out_shape,
                         compiler_params=pltpu.CompilerParams(vmem_limit_bytes=VMEM_LIMIT_BYTES))(g, w, m, v)
    return res[:3], {name: res[3 + 4 * a:7 + 4 * a] for a, (name, _, _) in enumerate(direct)}


def _sum4(rc):
    def body(rc_ref, o_ref):
        o_ref[...] = rc_ref[0] + rc_ref[1] + rc_ref[2] + rc_ref[3]

    return pl.pallas_call(
        body, name="small_sum", out_shape=jax.ShapeDtypeStruct(rc.shape[1:], F32),
    )(rc)


BIG = ("w_in", "w_branch", "w_out", "w_up", "w_down", "w_ple", "w_pleg")
BIG_SHARD = {"w_in": (D, D), "w_branch": (4 * W, GW), "w_out": (GW, D), "w_up": (D, W), "w_down": (W, D),
             "w_ple": (256, GW), "w_pleg": (GW, D)}
ADAM_ROWS = {"w_in": 256, "w_branch": 512, "w_out": 128, "w_up": 256, "w_down": 256, "w_ple": 256, "w_pleg": 128}
SMALL = (("norm_mix", (DEPTH, D)), ("conf_dw", (DEPTH, CONF_K, W)), ("conf_dw_b", (DEPTH, W)),
         ("conf_ln_g", (DEPTH, W)), ("conf_ln_b", (DEPTH, W)), ("pool_w", (DEPTH, 4, GW, GW)),
         ("pool_scale", (DEPTH, W)), ("sc_conv", (DEPTH, SC_K, W)), ("gmlp_ln_g", (DEPTH, W)),
         ("gmlp_ln_b", (DEPTH, W)), ("gmlp_ws", (DEPTH, 4, GW, GW)), ("gmlp_bs", (DEPTH, 4, GW)),
         ("norm_mlp", (DEPTH, D)), ("norm_ple", (DEPTH, D)), ("norm_final", (D,)))
CHANNEL_SHARDED = ("conf_dw", "sc_conv")
SMALL_ROWS = 80


def _pack(arrs, rows):
    flat = jnp.concatenate([a.reshape(-1) for a in arrs])
    return jnp.pad(flat, (0, rows * D - flat.shape[0])).reshape(rows, D)


def _unpack(packed, shapes):
    flat = packed.reshape(-1)
    out, off = [], 0
    for shp in shapes:
        size = 1
        for s in shp:
            size *= s
        out.append(flat[off:off + size].reshape(shp))
        off += size
    return out


def kernel(x, p, norm_mix, w_in, conf_dw, conf_dw_b, conf_ln_g, conf_ln_b, pool_w, pool_scale, sc_conv, gmlp_ln_g, gmlp_ln_b, gmlp_ws, gmlp_bs, w_branch, w_out, norm_mlp, w_up, w_down, norm_ple, w_ple, w_ple_gate, norm_final, loss_target, m_norm_mix, m_w_in, m_conf_dw, m_conf_dw_b, m_conf_ln_g, m_conf_ln_b, m_pool_w, m_pool_scale, m_sc_conv, m_gmlp_ln_g, m_gmlp_ln_b, m_gmlp_ws, m_gmlp_bs, m_w_branch, m_w_out, m_norm_mlp, m_w_up, m_w_down, m_norm_ple, m_w_ple, m_w_ple_gate, m_norm_final, v_norm_mix, v_w_in, v_conf_dw, v_conf_dw_b, v_conf_ln_g, v_conf_ln_b, v_pool_w, v_pool_scale, v_sc_conv, v_gmlp_ln_g, v_gmlp_ln_b, v_gmlp_ws, v_gmlp_bs, v_w_branch, v_w_out, v_norm_mlp, v_w_up, v_w_down, v_norm_ple, v_w_ple, v_w_ple_gate, v_norm_final):
    weights = dict(norm_mix=norm_mix, w_in=w_in, conf_dw=conf_dw, conf_dw_b=conf_dw_b, conf_ln_g=conf_ln_g,
                   conf_ln_b=conf_ln_b, pool_w=pool_w, pool_scale=pool_scale, sc_conv=sc_conv, gmlp_ln_g=gmlp_ln_g,
                   gmlp_ln_b=gmlp_ln_b, gmlp_ws=gmlp_ws, gmlp_bs=gmlp_bs, w_branch=w_branch, w_out=w_out,
                   norm_mlp=norm_mlp, w_up=w_up, w_down=w_down, norm_ple=norm_ple, w_ple=w_ple, w_pleg=w_ple_gate,
                   norm_final=norm_final)
    mom1 = dict(norm_mix=m_norm_mix, w_in=m_w_in, conf_dw=m_conf_dw, conf_dw_b=m_conf_dw_b, conf_ln_g=m_conf_ln_g,
                conf_ln_b=m_conf_ln_b, pool_w=m_pool_w, pool_scale=m_pool_scale, sc_conv=m_sc_conv,
                gmlp_ln_g=m_gmlp_ln_g, gmlp_ln_b=m_gmlp_ln_b, gmlp_ws=m_gmlp_ws, gmlp_bs=m_gmlp_bs,
                w_branch=m_w_branch, w_out=m_w_out, norm_mlp=m_norm_mlp, w_up=m_w_up, w_down=m_w_down,
                norm_ple=m_norm_ple, w_ple=m_w_ple, w_pleg=m_w_ple_gate, norm_final=m_norm_final)
    mom2 = dict(norm_mix=v_norm_mix, w_in=v_w_in, conf_dw=v_conf_dw, conf_dw_b=v_conf_dw_b, conf_ln_g=v_conf_ln_g,
                conf_ln_b=v_conf_ln_b, pool_w=v_pool_w, pool_scale=v_pool_scale, sc_conv=v_sc_conv,
                gmlp_ln_g=v_gmlp_ln_g, gmlp_ln_b=v_gmlp_ln_b, gmlp_ws=v_gmlp_ws, gmlp_bs=v_gmlp_bs,
                w_branch=v_w_branch, w_out=v_w_out, norm_mlp=v_norm_mlp, w_up=v_w_up, w_down=v_w_down,
                norm_ple=v_norm_ple, w_ple=v_w_ple, w_pleg=v_w_ple_gate, norm_final=v_norm_final)

    xi, yi, ci = _mesh_pos()
    me = 4 * xi + 2 * yi + ci
    c_idx = jnp.reshape(ci, (1,)).astype(jnp.int32)

    gathered, conf_full, sc_full = [], [], []
    for l in range(DEPTH):
        shard = lambda n: weights[n][l].astype(BF16).reshape(BIG_SHARD[n])
        w_in_g, conf_g, sc_g = _all_gather([shard("w_in"), conf_dw[l], sc_conv[l]], name="ag_first", seq_id=1)
        if l + 1 < DEPTH:
            rest = _all_gather([shard(n) for n in BIG[1:]], name="ag_rest", seq_id=2)
        else:
            rest = (list(_all_gather([shard(n) for n in BIG[1:4]], name="ag_rest_a", seq_id=2))
                    + list(_all_gather([shard(n) for n in BIG[4:]], name="ag_rest_b", seq_id=2)))
        gw = dict(zip(BIG[1:], rest), w_in=w_in_g)
        gw["w_branch"] = gw["w_branch"].reshape(NDEV, 4, W, GW)
        gathered.append(gw)
        conf_full.append(conf_g)
        sc_full.append(sc_g)

    def small_params(l):
        return dict(cw=conf_full[l], cb=conf_dw_b[l][None], lg=conf_ln_g[l][None], lb=conf_ln_b[l][None],
                    pw=pool_w[l], ps=pool_scale[l][None], sc=sc_full[l], gg=gmlp_ln_g[l][None],
                    gb=gmlp_ln_b[l][None], ws=gmlp_ws[l], bst=gmlp_bs[l].T, g_mix=norm_mix[l][None],
                    g_mlp=norm_mlp[l][None], g_ple=norm_ple[l][None])

    xc = x.reshape(T, D)
    small_names = [n for n, _ in SMALL]

    def in_gradient_layout(n, shard, shape):
        if n not in CHANNEL_SHARDED:
            return shard
        return lax.dynamic_update_slice(jnp.zeros(shape, F32), shard, (0, 0, me * (W // NDEV)))

    small_state = [_pack([in_gradient_layout(n, src[n], shape) for n, shape in SMALL], NDEV * SMALL_ROWS)
                   for src in (weights, mom1, mom2)]
    xc, small_state = lax.optimization_barrier((xc, small_state))
    p_bf = p.reshape(DEPTH, T, 256).astype(BF16)
    h = _norm_first(xc, norm_mix[0][None])
    saved = []
    for l in range(DEPTH):
        g_next = norm_mix[l + 1][None] if l + 1 < DEPTH else norm_final[None]
        h, conf_g, sc_g = lax.optimization_barrier((h, conf_full[l], sc_full[l]))
        conf_full[l] = conf_g.transpose(1, 0, 2).reshape(CONF_K, W)
        sc_full[l] = sc_g.transpose(1, 0, 2).reshape(SC_K, W)
        xc, h, sv = _layer_fwd(xc, h, p_bf[l], gathered[l], small_params(l), g_next)
        saved.append(sv)

    dxc, dg_final, loss_part = _loss_head(xc, loss_target.reshape(T, D), norm_final[None])
    loss = lax.psum(loss_part[0, 0], ("x", "y", "c"))
    small_grads = [None] * DEPTH
    rcs = {n: [None] * DEPTH for n in BIG}
    pipeline = _GradientPipeline(c_idx, rcs)
    for l in reversed(range(DEPTH)):
        dxc, small_grads[l] = _layer_bwd(dxc, saved[l], gathered[l], small_params(l),
                                         lambda dw, names, value, l=l: pipeline.submit(dw, names, l, value),
                                         early_group=(l == 0))

    def adam_sharded(first_layer, n_layers, partial, tag, names=BIG):
        outs = {}
        for n in names:
            shp = (DEPTH,) + BIG_SHARD[n]
            outs[n] = _adam_sharded(rcs[n][first_layer:first_layer + n_layers], weights[n].reshape(shp),
                                    mom1[n].reshape(shp), mom2[n].reshape(shp), ADAM_ROWS[n],
                                    "adam_%s_%s" % (n, tag), first_layer, None if partial is None else partial[n])
        return outs

    stacked = {n: jnp.stack([small_grads[l][n] for l in range(DEPTH)]) for n, _ in SMALL if n != "norm_final"}
    stacked["norm_final"] = dg_final[0]
    packed = _pack([stacked[n] for n, _ in SMALL], NDEV * SMALL_ROWS).reshape(NCHIP, 2, SMALL_ROWS, D)
    (pair_small,), _ = _rs_exchange([packed], [], name="rs_pair_small")
    q_small = _pair_sum([packed], [pair_small], c_idx, name="rs_pairsum_small")
    dxc, upper, q_small = lax.optimization_barrier((dxc, {n: rcs[n][1:] for n in BIG}, q_small))
    _, (chips_small,) = _rs_exchange([], q_small, name="rs_chip_small", seq_id=6)
    dxc, upper = pipeline.finish((dxc, upper))
    for n in BIG:
        rcs[n][1:] = upper[n]
    partial = adam_sharded(1, DEPTH - 1, None, "upper")
    last = adam_sharded(0, 1, partial, "last", names=BIG[:1])
    partial = {n: partial[n] for n in BIG[1:]}
    last, partial, chips_small = lax.optimization_barrier((last, partial, chips_small))
    reduced_slot = _sum4(chips_small)
    reduced = _all_gather([reduced_slot], name="ag_small", seq_id=7)[0]
    small_full = dict(zip([n for n, _ in SMALL], _unpack(reduced, [s for _, s in SMALL])))
    grads, deltas, new_m, new_v = {}, {}, {}, {}
    direct, row = [], 0
    for n, shape in SMALL:
        if len(shape) == 1 or shape[-2] == DEPTH:
            direct.append((n, row, (1,) * (2 - len(shape)) + tuple(shape)))
        size = 1
        for s in shape:
            size *= s
        row += size // D
    (d_p, m_p, v_p), own_shape = _adam_packed(reduced.reshape(NDEV * SMALL_ROWS, D), *small_state, direct)
    small_shapes = [s for _, s in SMALL]

    def own_channels(n, full):
        return lax.dynamic_slice_in_dim(full, me * (W // NDEV), W // NDEV, axis=2) if n in CHANNEL_SHARDED else full

    for n, d_, m_, v_ in zip(small_names, _unpack(d_p, small_shapes), _unpack(m_p, small_shapes),
                             _unpack(v_p, small_shapes)):
        if n in own_shape:
            grads[n], deltas[n], new_m[n], new_v[n] = [a.reshape(weights[n].shape) for a in own_shape[n]]
        else:
            grads[n], deltas[n], new_m[n], new_v[n] = (own_channels(n, small_full[n]), own_channels(n, d_),
                                                       own_channels(n, m_), own_channels(n, v_))

    last.update(adam_sharded(0, 1, partial, "last", names=BIG[1:]))
    for n, (g_, d_, m_, v_) in last.items():
        full = weights[n].shape
        grads[n], deltas[n], new_m[n], new_v[n] = g_.reshape(full), d_.reshape(full), m_.reshape(full), v_.reshape(full)

    order = ("norm_mix", "w_in", "conf_dw", "conf_dw_b", "conf_ln_g", "conf_ln_b", "pool_w", "pool_scale", "sc_conv",
             "gmlp_ln_g", "gmlp_ln_b", "gmlp_ws", "gmlp_bs", "w_branch", "w_out", "norm_mlp", "w_up", "w_down",
             "norm_ple", "w_ple", "w_pleg", "norm_final")
    return (loss, dxc.reshape(1, T, D), *[grads[n] for n in order], *[deltas[n] for n in order],
            *[new_m[n] for n in order], *[new_v[n] for n in order])
```

```python
import functools

import jax
import jax.numpy as jnp
from jax import lax
from jax.experimental import pallas as pl
from jax.experimental.pallas import tpu as pltpu
from jax.experimental.pallas import tpu_sc as plsc

F32 = jnp.float32
BF16 = jnp.bfloat16

DEPTH = 4
T = 2048
D = 1024
W = 512
NDEV = 8
NCHIP = 4
EPS = 1e-6
CONF_K = 31
SC_K = 3
POOL_WINDOWS = (2, 4, 8, 16)
GW = 128
HB = 32
HA = 32
COLS_IN = 8192
MIX_COLS = 4096

ADAM_LR = 0.001
ADAM_B1 = 0.9
ADAM_B2 = 0.999
ADAM_EPS = 1e-08
ADAM_WD = 0.01
ADAM_STEP = 10

VMEM_LIMIT_BYTES = 56 * 1024 * 1024
MESH = pl.DeviceIdType.MESH


def _cp(*sem):
    return pltpu.CompilerParams(dimension_semantics=tuple(sem), vmem_limit_bytes=VMEM_LIMIT_BYTES)


def _sig(x):
    return jax.nn.sigmoid(x)


def _rms(x, g):
    r = lax.rsqrt(jnp.mean(x * x, axis=-1, keepdims=True) + EPS)
    return x * r * g


def _rms_bwd(dh, x, g, dres):
    r = lax.rsqrt(jnp.mean(x * x, axis=-1, keepdims=True) + EPS)
    xh = x * r
    u = dh * g
    dx = r * (u - xh * jnp.mean(u * xh, axis=-1, keepdims=True)) + dres
    dg = jnp.sum(dh * xh, axis=0, keepdims=True)
    return dx, dg


def _ln_stats(x):
    mu = jnp.mean(x, axis=-1, keepdims=True)
    xc = x - mu
    rstd = lax.rsqrt(jnp.mean(xc * xc, axis=-1, keepdims=True) + EPS)
    return xc * rstd, rstd


def _ln_bwd(dxh, xh, rstd):
    return rstd * (dxh - jnp.mean(dxh, axis=-1, keepdims=True) - xh * jnp.mean(dxh * xh, axis=-1, keepdims=True))


def _rowsum(x):
    return jnp.sum(x, axis=0, keepdims=True)


EPI_ROWS = 256


def _relu2_bf16(up):
    r = jnp.maximum(up.astype(F32), 0.0)
    return (r * r).astype(BF16)


def _mm(a, b3, *, mode, name, outs, trans_b=False, tm=512, tiles=(), params=(), epi=None, reds=(), a_pre=None,
        stream_first=False):
    t_, ka = a.shape
    nj, r, c = b3.shape
    kb, nb = (c, r) if trans_b else (r, c)
    nt = t_ // tm
    out_mode = mode == "out"
    full = mode == "full"
    assert trans_b or not full
    if out_mode:
        assert ka == kb and not reds
        grid = (nj, nt)
        a_map = lambda g0, g1: (g1, 0)
        b_map = lambda g0, g1: (g0, 0, 0)
        t_map = lambda g0, g1: (g1, g0)
        width = nj * nb
    else:
        assert ka == nj * kb
        grid = (nt, 1 if full else nj)
        a_map = lambda g0, g1: (g0, g1)
        b_map = lambda g0, g1: (g1, 0, 0)
        t_map = lambda g0, g1: (g0, 0)
        width = nb
    n_t, n_p, n_o, n_r = len(tiles), len(params), len(outs), len(reds)
    use_acc = (not out_mode) and nj > 1 and not full
    dims = (((1,), (1,)), ((), ())) if trans_b else (((1,), (0,)), ((), ()))

    def body(a_ref, b_ref, *rest):
        t_refs = rest[:n_t]
        p_refs = rest[n_t:n_t + n_p]
        o_refs = rest[n_t + n_p:n_t + n_p + n_o]
        r_refs = rest[n_t + n_p + n_o:n_t + n_p + n_o + n_r]
        i = pl.program_id(1 if out_mode else 0)
        a_val = a_ref[...] if a_pre is None else a_pre(a_ref[...])
        if full:
            b_all, b_sems = rest[-2 - stream_first], rest[-1 - stream_first]

            def weight_copies():
                return [pltpu.make_async_copy(b_ref.at[j], b_all.at[:, j * c:(j + 1) * c], b_sems.at[j])
                        for j in range(nj)]

            if stream_first:
                part = rest[-1]

                @pl.when(i == 0)
                def _():
                    cps = weight_copies()
                    for cp in cps:
                        cp.start()
                    acc = None
                    for j, cp in enumerate(cps):
                        cp.wait()
                        term = lax.dot_general(a_val[:, j * c:(j + 1) * c], b_all[:, j * c:(j + 1) * c], dims,
                                               preferred_element_type=F32)
                        acc = term if acc is None else acc + term
                    part[...] = acc

                @pl.when(i > 0)
                def _():
                    part[...] = lax.dot_general(a_val, b_all[...], dims, preferred_element_type=F32)
            else:
                @pl.when(i == 0)
                def _():
                    cps = weight_copies()
                    for cp in cps:
                        cp.start()
                    for cp in cps:
                        cp.wait()

                part = lax.dot_general(a_val, b_all[...], dims, preferred_element_type=F32)
        else:
            part = lax.dot_general(a_val, b_ref[...], dims, preferred_element_type=F32)

        def finish(acc_rows):
            totals = [None] * n_r
            for r0 in range(0, tm, min(tm, EPI_ROWS)):
                rows = slice(r0, r0 + min(tm, EPI_ROWS))
                if epi is None:
                    res, rr = (acc_rows(rows),), ()
                else:
                    res, rr = epi(acc_rows(rows), [t[rows, :] for t in t_refs], [p[...] for p in p_refs])
                for o_ref, val in zip(o_refs, res):
                    o_ref[rows, :] = val.astype(o_ref.dtype)
                totals = [val if tot is None else tot + val for tot, val in zip(totals, rr)]
            for r_ref, val in zip(r_refs, totals):
                @pl.when(i == 0)
                def _():
                    r_ref[...] = val

                @pl.when(i > 0)
                def _():
                    r_ref[...] += val

        if use_acc:
            acc_ref = rest[-1]
            j = pl.program_id(1)

            @pl.when(j == 0)
            def _():
                acc_ref[...] = part

            @pl.when(jnp.logical_and(j > 0, j < nj - 1))
            def _():
                acc_ref[...] += part

            @pl.when(j == nj - 1)
            def _():
                finish(lambda rows: acc_ref[rows, :] + part[rows])
        else:
            finish(lambda rows: part[rows, :])

    const2 = lambda g0, g1: (0, 0)
    if full:
        in_specs = [pl.BlockSpec((tm, ka), a_map), pl.BlockSpec(memory_space=pl.ANY)]
        scratch = [pltpu.VMEM((r, nj * c), b3.dtype), pltpu.SemaphoreType.DMA((nj,))]
        scratch += [pltpu.VMEM((tm, nb), F32)] if stream_first else []
    else:
        in_specs = [pl.BlockSpec((tm, kb), a_map), pl.BlockSpec((None, r, c), b_map)]
        scratch = [pltpu.VMEM((tm, nb), F32)] if use_acc else []
    in_specs += [pl.BlockSpec((tm, t.shape[1] // nj if out_mode else t.shape[1]), t_map) for t in tiles]
    in_specs += [pl.BlockSpec(p.shape, lambda g0, g1, nd=p.ndim: (0,) * nd) for p in params]
    out_specs = [pl.BlockSpec((tm, nb), t_map) for _ in outs] + [pl.BlockSpec((1, w), const2) for w in reds]
    out_shape = [jax.ShapeDtypeStruct((t_, width), dt) for dt in outs]
    out_shape += [jax.ShapeDtypeStruct((1, w), F32) for w in reds]
    res = pl.pallas_call(
        body, name=name, grid=grid, in_specs=in_specs, out_specs=out_specs, out_shape=out_shape,
        scratch_shapes=scratch, compiler_params=_cp("arbitrary", "arbitrary"),
    )(a, b3, *tiles, *params)
    return res


def _mm_tn(a, g, *, nj, split, name, out_dtype=BF16, a_pre=None):
    t_ = a.shape[0]
    if split == "col":
        r, c = a.shape[1], g.shape[1] // nj
        a_spec = pl.BlockSpec((t_, r), lambda j: (0, 0))
        g_spec = pl.BlockSpec((t_, c), lambda j: (0, j))
    else:
        r, c = a.shape[1] // nj, g.shape[1]
        a_spec = pl.BlockSpec((t_, r), lambda j: (0, j))
        g_spec = pl.BlockSpec((t_, c), lambda j: (0, 0))

    def body(a_ref, g_ref, o_ref):
        a_val = a_ref[...] if a_pre is None else a_pre(a_ref[...])
        o_ref[...] = lax.dot_general(a_val, g_ref[...], (((0,), (0,)), ((), ())),
                                     preferred_element_type=F32).astype(o_ref.dtype)

    return pl.pallas_call(
        body, name=name, grid=(nj,), in_specs=[a_spec, g_spec],
        out_specs=pl.BlockSpec((None, r, c), lambda j: (j, 0, 0)),
        out_shape=jax.ShapeDtypeStruct((nj, r, c), out_dtype),
        compiler_params=_cp("arbitrary"),
    )(a, g)


def _epi_res_norm(acc, tiles, params):
    x_new = tiles[0] + acc
    return (x_new, _rms(x_new, params[0])), ()


def _epi_ple(acc, tiles, params):
    x_old, p_tile = tiles
    g_next, w_ple8 = params
    pe = jnp.concatenate([jnp.dot(p_tile, w_ple8[j], preferred_element_type=F32) for j in range(NDEV)], axis=1)
    x_new = x_old + pe * _sig(acc)
    return (x_new, acc, _rms(x_new, g_next), pe), ()


def _epi_rms_bwd(acc, tiles, params):
    dx, dg = _rms_bwd(acc, tiles[0], params[0], tiles[1])
    return (dx, dx), (dg,)


def _epi_dup(acc, tiles, params):
    return (acc * (2.0 * jnp.maximum(tiles[0].astype(F32), 0.0)),), ()


def _tri_mask():
    row = lax.broadcasted_iota(jnp.int32, (GW, GW), 0)
    col = lax.broadcasted_iota(jnp.int32, (GW, GW), 1)
    return row >= col


def _small_specs(sp_list):
    return [pl.BlockSpec(p.shape, (lambda i: (0, 0)) if p.ndim == 2 else (lambda i: (0, 0, 0))) for p in sp_list]


SUBLANES = 8


def _tap_sum(src, w_ref, taps, rows, stage):
    groups = {}
    for off, k in taps:
        groups.setdefault(off % SUBLANES, []).append((off - off % SUBLANES, k))
    out = None
    for res, members in sorted(groups.items()):
        n = rows if res == 0 else rows + SUBLANES
        part = None
        for base, k in members:
            term = w_ref[k:k + 1, :] * src[pl.ds(base, n), :]
            part = term if part is None else part + term
        if res:
            stage[0:n, :] = part
            part = stage[pl.ds(res, rows), :]
        out = part if out is None else out + part
    return out


def _tap_grads(grad, src, offsets, rows, stage, out_ref):
    pad = SUBLANES
    stage[0:pad, :] = jnp.zeros((pad, grad.shape[1]), F32)
    stage[pad:pad + rows, :] = grad
    stage[pad + rows:2 * pad + rows, :] = jnp.zeros((pad, grad.shape[1]), F32)
    groups = {}
    for k, off in enumerate(offsets):
        groups.setdefault(off % SUBLANES, []).append((off - off % SUBLANES, k))
    for res, members in sorted(groups.items()):
        shifted = stage[pl.ds(pad - res, rows + pad), :]
        for base, k in members:
            out_ref[k:k + 1, :] += _rowsum(shifted * src[pl.ds(base, rows + pad), :])


def _mixer_params(sp):
    return [sp["cw"], sp["cb"], sp["lg"], sp["lb"], sp["pw"], sp["ps"], sp["sc"], sp["gg"], sp["gb"], sp["ws"], sp["bst"]]


def _mixer_fwd(proj, sp, tm=256):
    nt = T // tm
    per = tm // HB

    conv_taps = [(HB - (CONF_K - 1) + k, k) for k in range(CONF_K)]

    def body(main_ref, halo_ref, cw, cb, lg, lb, pw, ps, sc, gg, gb, ws, bst, y_ref, ca_ref, ext, stage):
        i = pl.program_id(0)
        keep = (i > 0).astype(F32)

        def mcol(c0):
            return main_ref[:, c0:c0 + W].astype(F32)

        def hcol(c0):
            return halo_ref[:, c0:c0 + W].astype(F32)

        ext[0:HB, :] = hcol(0) * _sig(hcol(W)) * keep
        ext[HB:HB + tm, :] = mcol(0) * _sig(mcol(W))
        ca = (_tap_sum(ext, cw, conv_taps, tm, stage) + cb[...]).astype(BF16)
        ca_ref[...] = ca
        xh, _ = _ln_stats(ca.astype(F32))
        n = xh * lg[...] + lb[...]
        y_ref[:, 0:W] = (n * _sig(n)).astype(BF16)

        pin = mcol(1024)
        ext[0:HB, :] = hcol(1024) * keep
        ext[HB:HB + tm, :] = pin
        pos = (i * tm + lax.broadcasted_iota(jnp.int32, (tm, 1), 0) + 1).astype(F32)
        for g, w in enumerate(POOL_WINDOWS):
            lo = g * GW
            s = ext[pl.ds(HB, tm), lo:lo + GW]
            for j in range(1, w):
                s = s + ext[pl.ds(HB - j, tm), lo:lo + GW]
            pooled = s / jnp.minimum(pos, float(w)) - pin[:, lo:lo + GW]
            mixed = jnp.dot(pooled.astype(BF16), pw[g].astype(BF16), preferred_element_type=F32)
            y_ref[:, W + lo:W + lo + GW] = (mixed * ps[:, lo:lo + GW]).astype(BF16)

        ext[0:HB, :] = hcol(2048) * hcol(2560) * keep
        ext[HB:HB + tm, :] = mcol(2048) * mcol(2560)
        cv = sc[0:1, :] * ext[pl.ds(HB - 2, tm), :]
        cv = cv + sc[1:2, :] * ext[pl.ds(HB - 1, tm), :]
        cv = cv + sc[2:3, :] * ext[pl.ds(HB, tm), :]
        y_ref[:, 2 * W:3 * W] = (mcol(1536) * cv).astype(BF16)

        vh, _ = _ln_stats(mcol(3584))
        vn = (vh * gg[...] + gb[...]).astype(BF16)
        u = mcol(3072)
        tri = _tri_mask()
        for g in range(4):
            lo = g * GW
            wm = jnp.where(tri, ws[g], 0.0).astype(BF16)
            for c in range(tm // GW):
                r0 = c * GW
                sg = jnp.dot(wm, vn[r0:r0 + GW, lo:lo + GW], preferred_element_type=F32) + bst[:, g:g + 1]
                y_ref[r0:r0 + GW, 3 * W + lo:3 * W + lo + GW] = (u[r0:r0 + GW, lo:lo + GW] * sg).astype(BF16)

    plist = _mixer_params(sp)
    in_specs = [pl.BlockSpec((tm, MIX_COLS), lambda i: (i, 0)),
                pl.BlockSpec((HB, MIX_COLS), lambda i: (jnp.maximum(i * per - 1, 0), 0))]
    in_specs += _small_specs(plist)
    return pl.pallas_call(
        body, name="f_mixers", grid=(nt,), in_specs=in_specs,
        out_specs=[pl.BlockSpec((tm, 4 * W), lambda i: (i, 0)), pl.BlockSpec((tm, W), lambda i: (i, 0))],
        out_shape=[jax.ShapeDtypeStruct((T, 4 * W), BF16), jax.ShapeDtypeStruct((T, W), BF16)],
        scratch_shapes=[pltpu.VMEM((HB + tm, W), F32), pltpu.VMEM((tm + SUBLANES, W), F32)],
        compiler_params=_cp("arbitrary"),
    )(proj, proj, *plist)


def _assemble_wb(wb8_ref, wbf_ref):
    for k in range(4):
        for j in range(NDEV):
            wbf_ref[k, :, j * GW:(j + 1) * GW] = wb8_ref[j, k]


def _merge_fwd(y, proj, wb8, tm=256):
    nt = T // tm

    def body(y_ref, gate_ref, wb8_ref, z_ref, m_ref, wbf):
        @pl.when(pl.program_id(0) == 0)
        def _():
            _assemble_wb(wb8_ref, wbf)

        m = jnp.zeros((tm, D), F32)
        for k in range(4):
            zk = jnp.dot(y_ref[:, k * W:(k + 1) * W], wbf[k], preferred_element_type=F32)
            z_ref[:, k * D:(k + 1) * D] = zk.astype(BF16)
            m = m + _sig(gate_ref[:, k * D:(k + 1) * D].astype(F32)) * zk
        m_ref[...] = m.astype(BF16)

    return pl.pallas_call(
        body, name="f_merge", grid=(nt,),
        in_specs=[pl.BlockSpec((tm, 4 * W), lambda i: (i, 0)),
                  pl.BlockSpec((tm, 4 * D), lambda i: (i, 1)),
                  pl.BlockSpec(wb8.shape, lambda i: (0, 0, 0, 0))],
        out_specs=[pl.BlockSpec((tm, 4 * D), lambda i: (i, 0)), pl.BlockSpec((tm, D), lambda i: (i, 0))],
        out_shape=[jax.ShapeDtypeStruct((T, 4 * D), BF16), jax.ShapeDtypeStruct((T, D), BF16)],
        scratch_shapes=[pltpu.VMEM((4, W, D), BF16)],
        compiler_params=_cp("arbitrary"),
    )(y, proj, wb8)


def _merge_bwd(dm, z, proj, y, wb8, tm=256):
    nt = T // tm

    def body(dm_ref, z_ref, gate_ref, y_ref, wb8_ref, dp_ref, dy_ref, dwb_ref, wbf, acc):
        i = pl.program_id(0)

        @pl.when(i == 0)
        def _():
            _assemble_wb(wb8_ref, wbf)

        dmv = dm_ref[...].astype(F32)
        for k in range(4):
            s = _sig(gate_ref[:, k * D:(k + 1) * D].astype(F32))
            dzk = (dmv * s).astype(BF16)
            dp_ref[:, k * D:(k + 1) * D] = (dmv * z_ref[:, k * D:(k + 1) * D].astype(F32) * s * (1.0 - s)).astype(BF16)
            dyk = lax.dot_general(dzk, wbf[k], (((1,), (1,)), ((), ())), preferred_element_type=F32)
            dy_ref[:, k * W:(k + 1) * W] = dyk.astype(BF16)
            part = lax.dot_general(y_ref[:, k * W:(k + 1) * W], dzk, (((0,), (0,)), ((), ())),
                                   preferred_element_type=F32)

            @pl.when(i == 0)
            def _():
                acc[k] = part

            @pl.when(i > 0)
            def _():
                acc[k] += part

        @pl.when(i == nt - 1)
        def _():
            for k in range(4):
                for j in range(NDEV):
                    dwb_ref[j, k] = acc[k, :, j * GW:(j + 1) * GW].astype(BF16)

    return pl.pallas_call(
        body, name="b_merge", grid=(nt,),
        in_specs=[pl.BlockSpec((tm, D), lambda i: (i, 0)),
                  pl.BlockSpec((tm, 4 * D), lambda i: (i, 0)),
                  pl.BlockSpec((tm, 4 * D), lambda i: (i, 1)),
                  pl.BlockSpec((tm, 4 * W), lambda i: (i, 0)),
                  pl.BlockSpec(wb8.shape, lambda i: (0, 0, 0, 0))],
        out_specs=[pl.BlockSpec((tm, 4 * D), lambda i: (i, 1)),
                   pl.BlockSpec((tm, 4 * W), lambda i: (i, 0)),
                   pl.BlockSpec(wb8.shape, lambda i: (0, 0, 0, 0))],
        out_shape=[jax.ShapeDtypeStruct((T, COLS_IN), BF16),
                   jax.ShapeDtypeStruct((T, 4 * W), BF16),
                   jax.ShapeDtypeStruct(wb8.shape, BF16)],
        scratch_shapes=[pltpu.VMEM((4, W, D), BF16), pltpu.VMEM((4, W, D), F32)],
        compiler_params=_cp("arbitrary"),
    )(dm, z, proj, y, wb8)


def _mixer_bwd(proj, ca_saved, dy, dproj, sp, tm=256):
    nt = T // tm
    per = tm // HB
    ne = tm + HA
    last_blk = T // HA - 1
    conv_taps = [(HB - (CONF_K - 1) + k, k) for k in range(CONF_K)]

    def body(main_ref, hb_ref, ha_ref, ca_ref, cah_ref, dy_ref, dyh_ref, cw, cb, lg, lb, pw, ps, sc, gg, gb, ws, bst,
             dp_any, dp_ref, dcw_ref, dsc_ref, vec_ref, dpw_ref, dws_ref, dbs_ref, e1, e2, e3, stage):
        del dp_any, cb
        i = pl.program_id(0)
        keep_b = (i > 0).astype(F32)
        keep_a = (i < nt - 1).astype(F32)

        @pl.when(i == 0)
        def _():
            dcw_ref[...] = jnp.zeros_like(dcw_ref)
            dsc_ref[...] = jnp.zeros_like(dsc_ref)
            vec_ref[...] = jnp.zeros_like(vec_ref)
            dpw_ref[...] = jnp.zeros_like(dpw_ref)
            dws_ref[...] = jnp.zeros_like(dws_ref)
            dbs_ref[...] = jnp.zeros_like(dbs_ref)

        def mcol(c0):
            return main_ref[:, c0:c0 + W].astype(F32)

        def hbcol(c0):
            return hb_ref[:, c0:c0 + W].astype(F32)

        def hacol(c0):
            return ha_ref[:, c0:c0 + W].astype(F32)

        def load_dy(c0):
            e2[0:tm, :] = dy_ref[:, c0:c0 + W].astype(F32)
            e2[tm:ne, :] = dyh_ref[:, c0:c0 + W].astype(F32) * keep_a

        a = mcol(0)
        sa = _sig(mcol(W))
        e1[0:HB, :] = hbcol(0) * _sig(hbcol(W)) * keep_b
        e1[HB:HB + tm, :] = a * sa
        e1[HB + tm:HB + tm + SUBLANES, :] = jnp.zeros((SUBLANES, W), F32)
        e2[0:tm, :] = ca_ref[...].astype(F32)
        e2[tm:ne, :] = cah_ref[...].astype(F32)
        xh, rstd = _ln_stats(e2[0:ne, :])
        nn = xh * lg[...] + lb[...]
        s = _sig(nn)
        load_dy(0)
        dn = e2[0:ne, :] * (s * (1.0 + nn * (1.0 - s)))
        vec_ref[1:2, :] += _rowsum(dn[0:tm] * xh[0:tm])
        vec_ref[2:3, :] += _rowsum(dn[0:tm])
        dca = _ln_bwd(dn * lg[...], xh, rstd)
        e3[0:ne, :] = dca
        dmain = dca[0:tm]
        vec_ref[0:1, :] += _rowsum(dmain)
        _tap_grads(dmain, e1, [off for off, _ in conv_taps], tm, stage, dcw_ref)
        dglu = _tap_sum(e3, cw, [(CONF_K - 1 - k, k) for k in range(CONF_K)], tm, stage)
        dp_ref[:, 0:W] = (dglu * sa).astype(BF16)
        dp_ref[:, W:2 * W] = (dglu * a * sa * (1.0 - sa)).astype(BF16)

        pin = mcol(1024)
        e1[0:HB, :] = hbcol(1024) * keep_b
        e1[HB:HB + tm, :] = pin
        load_dy(W)
        dyb = e2[0:ne, :]
        pos_m = (i * tm + lax.broadcasted_iota(jnp.int32, (tm, 1), 0) + 1).astype(F32)
        pos_e = (i * tm + lax.broadcasted_iota(jnp.int32, (ne, 1), 0) + 1).astype(F32)
        for g, w in enumerate(POOL_WINDOWS):
            lo = g * GW
            acc = e1[pl.ds(HB, tm), lo:lo + GW]
            for j in range(1, w):
                acc = acc + e1[pl.ds(HB - j, tm), lo:lo + GW]
            pooled = (acc / jnp.minimum(pos_m, float(w)) - pin[:, lo:lo + GW]).astype(BF16)
            pwb = pw[g].astype(BF16)
            mixed = jnp.dot(pooled, pwb, preferred_element_type=F32)
            dyb_g = dyb[:, lo:lo + GW]
            vec_ref[3:4, lo:lo + GW] += _rowsum(dyb_g[0:tm] * mixed)
            dmb = (dyb_g * ps[:, lo:lo + GW]).astype(BF16)
            dpw_ref[g] += lax.dot_general(pooled, dmb[0:tm], (((0,), (0,)), ((), ())), preferred_element_type=F32)
            dpool = lax.dot_general(dmb, pwb, (((1,), (1,)), ((), ())), preferred_element_type=F32)
            e3[0:ne, lo:lo + GW] = dpool / jnp.minimum(pos_e, float(w))
            back = e3[pl.ds(0, tm), lo:lo + GW]
            for j in range(1, w):
                back = back + e3[pl.ds(j, tm), lo:lo + GW]
            dp_ref[:, 1024 + lo:1024 + lo + GW] = (back - dpool[0:tm]).astype(BF16)

        cg = mcol(2048)
        hx = mcol(2560)
        e1[0:HB, :] = hbcol(2048) * hbcol(2560) * keep_b
        e1[HB:HB + tm, :] = cg * hx
        load_dy(2 * W)
        dyc = e2[0:tm, :]
        dconv = dyc * mcol(1536)
        e3[0:tm, :] = dconv
        e3[tm:ne, :] = e2[tm:ne, :] * hacol(1536)
        cv = sc[0:1, :] * e1[pl.ds(HB - 2, tm), :]
        for k in range(1, SC_K):
            cv = cv + sc[k:k + 1, :] * e1[pl.ds(HB - 2 + k, tm), :]
        dp_ref[:, 1536:2048] = (dyc * cv).astype(BF16)
        for k in range(SC_K):
            dsc_ref[k:k + 1, :] += _rowsum(dconv * e1[pl.ds(HB - 2 + k, tm), :])
        dq = sc[0:1, :] * e3[pl.ds(2, tm), :]
        for k in range(1, SC_K):
            dq = dq + sc[k:k + 1, :] * e3[pl.ds(2 - k, tm), :]
        dp_ref[:, 2048:2560] = (dq * hx).astype(BF16)
        dp_ref[:, 2560:3072] = (dq * cg).astype(BF16)

        u = mcol(3072)
        vh, vr = _ln_stats(mcol(3584))
        vn = (vh * gg[...] + gb[...]).astype(BF16)
        dyd = dy_ref[:, 3 * W:4 * W].astype(F32)
        tri = _tri_mask()
        for g in range(4):
            lo = g * GW
            wm = jnp.where(tri, ws[g], 0.0).astype(BF16)
            dws_g = jnp.zeros((GW, GW), F32)
            dbs_g = jnp.zeros((GW, 1), F32)
            for c in range(tm // GW):
                r0 = c * GW
                blk = vn[r0:r0 + GW, lo:lo + GW]
                sg = jnp.dot(wm, blk, preferred_element_type=F32) + bst[:, g:g + 1]
                dyd_b = dyd[r0:r0 + GW, lo:lo + GW]
                dp_ref[r0:r0 + GW, 3072 + lo:3072 + lo + GW] = (dyd_b * sg).astype(BF16)
                dsg = dyd_b * u[r0:r0 + GW, lo:lo + GW]
                dsgb = dsg.astype(BF16)
                dbs_g = dbs_g + jnp.sum(dsg, axis=-1, keepdims=True)
                dws_g = dws_g + lax.dot_general(dsgb, blk, (((1,), (1,)), ((), ())), preferred_element_type=F32)
                e1[r0:r0 + GW, lo:lo + GW] = lax.dot_general(wm, dsgb, (((0,), (0,)), ((), ())),
                                                             preferred_element_type=F32)
            dws_ref[g] += jnp.where(tri, dws_g, 0.0)
            dbs_ref[g] += jnp.broadcast_to(dbs_g, (GW, GW))
        dvn = e1[0:tm, :]
        vec_ref[4:5, :] += _rowsum(dvn * vh)
        vec_ref[5:6, :] += _rowsum(dvn)
        dp_ref[:, 3584:4096] = _ln_bwd(dvn * gg[...], vh, vr).astype(BF16)

    plist = _mixer_params(sp)
    in_specs = [pl.BlockSpec((tm, MIX_COLS), lambda i: (i, 0)),
                pl.BlockSpec((HB, MIX_COLS), lambda i: (jnp.maximum(i * per - 1, 0), 0)),
                pl.BlockSpec((HA, MIX_COLS), lambda i: (jnp.minimum((i + 1) * per, last_blk), 0)),
                pl.BlockSpec((tm, W), lambda i: (i, 0)),
                pl.BlockSpec((HA, W), lambda i: (jnp.minimum((i + 1) * per, last_blk), 0)),
                pl.BlockSpec((tm, 4 * W), lambda i: (i, 0)),
                pl.BlockSpec((HA, 4 * W), lambda i: (jnp.minimum((i + 1) * per, last_blk), 0))]
    in_specs += _small_specs(plist)
    in_specs += [pl.BlockSpec(memory_space=pl.ANY)]
    z2 = lambda i: (0, 0)
    z3 = lambda i: (0, 0, 0)
    out_specs = [pl.BlockSpec((tm, MIX_COLS), lambda i: (i, 0)),
                 pl.BlockSpec((32, W), z2), pl.BlockSpec((8, W), z2), pl.BlockSpec((8, W), z2),
                 pl.BlockSpec((4, GW, GW), z3), pl.BlockSpec((4, GW, GW), z3), pl.BlockSpec((4, GW, GW), z3)]
    out_shape = [jax.ShapeDtypeStruct((T, COLS_IN), BF16),
                 jax.ShapeDtypeStruct((32, W), F32), jax.ShapeDtypeStruct((8, W), F32),
                 jax.ShapeDtypeStruct((8, W), F32),
                 jax.ShapeDtypeStruct((4, GW, GW), F32), jax.ShapeDtypeStruct((4, GW, GW), F32),
                 jax.ShapeDtypeStruct((4, GW, GW), F32)]
    n_in = 7 + len(plist)
    return pl.pallas_call(
        body, name="b_mixers", grid=(nt,), in_specs=in_specs, out_specs=out_specs, out_shape=out_shape,
        scratch_shapes=[pltpu.VMEM((HB + ne, W), F32), pltpu.VMEM((ne, W), F32), pltpu.VMEM((ne, W), F32),
                        pltpu.VMEM((ne + SUBLANES, W), F32)],
        input_output_aliases={n_in: 0},
        compiler_params=_cp("arbitrary"),
    )(proj, proj, proj, ca_saved, ca_saved, dy, dy, *plist, dproj)


def _norm_first(x, g, tm=512):
    def body(x_ref, g_ref, o_ref):
        o_ref[...] = _rms(x_ref[...], g_ref[...]).astype(BF16)

    return pl.pallas_call(
        body, name="f_norm0", grid=(T // tm,),
        in_specs=[pl.BlockSpec((tm, D), lambda i: (i, 0)), pl.BlockSpec((1, D), lambda i: (0, 0))],
        out_specs=pl.BlockSpec((tm, D), lambda i: (i, 0)),
        out_shape=jax.ShapeDtypeStruct((T, D), BF16), compiler_params=_cp("arbitrary"),
    )(x, g)


def _loss_head(x, target, g, tm=256):
    def body(x_ref, t_ref, g_ref, dx_ref, dg_ref, loss_ref):
        i = pl.program_id(0)
        x = x_ref[...]
        r = lax.rsqrt(jnp.mean(x * x, axis=-1, keepdims=True) + EPS)
        xh = x * r
        gv = g_ref[...]
        e = xh * gv - t_ref[...]
        dyv = e * (1.0 / D)
        part = jnp.sum(_rowsum(e * e), axis=-1, keepdims=True) * (0.5 / D)
        u = dyv * gv
        dx_ref[...] = r * (u - xh * jnp.mean(u * xh, axis=-1, keepdims=True))
        dgp = _rowsum(dyv * xh)

        @pl.when(i == 0)
        def _():
            dg_ref[...] = dgp
            loss_ref[...] = jnp.broadcast_to(part, (1, GW))

        @pl.when(i > 0)
        def _():
            dg_ref[...] += dgp
            loss_ref[...] += jnp.broadcast_to(part, (1, GW))

    return pl.pallas_call(
        body, name="loss_head", grid=(T // tm,),
        in_specs=[pl.BlockSpec((tm, D), lambda i: (i, 0)), pl.BlockSpec((tm, D), lambda i: (i, 0)),
                  pl.BlockSpec((1, D), lambda i: (0, 0))],
        out_specs=[pl.BlockSpec((tm, D), lambda i: (i, 0)), pl.BlockSpec((1, D), lambda i: (0, 0)),
                   pl.BlockSpec((1, GW), lambda i: (0, 0))],
        out_shape=[jax.ShapeDtypeStruct((T, D), F32), jax.ShapeDtypeStruct((1, D), F32),
                   jax.ShapeDtypeStruct((1, GW), F32)],
        compiler_params=_cp("arbitrary"),
    )(x, target, g)


def _out_bwd(dx2b, merged, w_out8, tm=512):
    nt = T // tm

    def body(dx_ref, mg_ref, w_ref, dm_ref, dw_ref, acc):
        i = pl.program_id(0)
        dx = dx_ref[...]
        dm_ref[...] = lax.dot_general(dx, w_ref[...], (((1,), (1,)), ((), ())),
                                      preferred_element_type=F32).astype(BF16)
        part = lax.dot_general(mg_ref[...], dx, (((0,), (0,)), ((), ())), preferred_element_type=F32)

        @pl.when(i == 0)
        def _():
            acc[...] = part

        @pl.when(i > 0)
        def _():
            acc[...] += part

        @pl.when(i == nt - 1)
        def _():
            for j in range(NDEV):
                dw_ref[j] = acc[j * GW:(j + 1) * GW, :].astype(BF16)

    tile = pl.BlockSpec((tm, D), lambda i: (i, 0))
    return pl.pallas_call(
        body, name="b_out", grid=(nt,),
        in_specs=[tile, tile, pl.BlockSpec((D, D), lambda i: (0, 0))],
        out_specs=[tile, pl.BlockSpec((NDEV, GW, D), lambda i: (0, 0, 0))],
        out_shape=[jax.ShapeDtypeStruct((T, D), BF16), jax.ShapeDtypeStruct((NDEV, GW, D), BF16)],
        scratch_shapes=[pltpu.VMEM((D, D), F32)],
        compiler_params=_cp("arbitrary"),
    )(dx2b, merged, w_out8.reshape(D, D))


def _ple_bwd(dx4, sv, w_pleg8, g_ple, tm=256):
    nt = T // tm
    ple_dim = sv["p"].shape[1]

    def body(dx_ref, gl_ref, pe_ref, x_ref, h_ref, p_ref, g_ref, wg_ref,
             dx3_ref, dx3b_ref, dg_ref, dwg_ref, dwp_ref, acc_g, acc_p):
        i = pl.program_id(0)
        d = dx_ref[...]
        s = _sig(gl_ref[...].astype(F32))
        dpe = (d * s).astype(BF16)
        dgl = (d * pe_ref[...].astype(F32) * s * (1.0 - s)).astype(BF16)
        dh = lax.dot_general(dgl, wg_ref[...], (((1,), (1,)), ((), ())), preferred_element_type=F32)
        dx, dgp = _rms_bwd(dh, x_ref[...], g_ref[...], d)
        dx3_ref[...] = dx
        dx3b_ref[...] = dx.astype(BF16)
        part_g = lax.dot_general(h_ref[...], dgl, (((0,), (0,)), ((), ())), preferred_element_type=F32)
        part_p = lax.dot_general(p_ref[...], dpe, (((0,), (0,)), ((), ())), preferred_element_type=F32)

        @pl.when(i == 0)
        def _():
            dg_ref[...] = dgp
            acc_g[...] = part_g
            acc_p[...] = part_p

        @pl.when(i > 0)
        def _():
            dg_ref[...] += dgp
            acc_g[...] += part_g
            acc_p[...] += part_p

        @pl.when(i == nt - 1)
        def _():
            for j in range(NDEV):
                dwg_ref[j] = acc_g[j * GW:(j + 1) * GW, :].astype(BF16)
                dwp_ref[j] = acc_p[:, j * GW:(j + 1) * GW].astype(BF16)

    tile = lambda w: pl.BlockSpec((tm, w), lambda i: (i, 0))
    const = lambda shp: pl.BlockSpec(shp, lambda i: (0,) * len(shp))
    return pl.pallas_call(
        body, name="b_ple", grid=(nt,),
        in_specs=[tile(D), tile(D), tile(D), tile(D), tile(D), tile(ple_dim), const((1, D)), const((D, D))],
        out_specs=[tile(D), tile(D), const((1, D)), const((NDEV, GW, D)), const((NDEV, ple_dim, GW))],
        out_shape=[jax.ShapeDtypeStruct((T, D), F32), jax.ShapeDtypeStruct((T, D), BF16),
                   jax.ShapeDtypeStruct((1, D), F32), jax.ShapeDtypeStruct((NDEV, GW, D), BF16),
                   jax.ShapeDtypeStruct((NDEV, ple_dim, GW), BF16)],
        scratch_shapes=[pltpu.VMEM((D, D), F32), pltpu.VMEM((ple_dim, D), F32)],
        compiler_params=_cp("arbitrary"),
    )(dx4, sv["gl"], sv["pe"], sv["x3"], sv["h3"], sv["p"], g_ple, w_pleg8.reshape(D, D))


def _layer_fwd(x, h1, p_bf, gw, sp, g_next):
    proj, = _mm(h1, gw["w_in"], mode="out", name="f_proj", outs=[BF16], tm=T)
    y, ca = _mixer_fwd(proj, sp)
    z, merged = _merge_fwd(y, proj, gw["w_branch"])
    x2, h2 = _mm(merged, gw["w_out"].reshape(1, D, D), mode="acc", name="f_out", outs=[F32, BF16], tm=T // 2,
                 tiles=[x], params=[sp["g_mlp"]], epi=_epi_res_norm)
    up, = _mm(h2, gw["w_up"], mode="out", name="f_up", outs=[BF16], tm=T)
    x3, h3 = _mm(up, gw["w_down"].reshape(1, 4 * D, D), mode="acc", name="f_down", outs=[F32, BF16], tm=T // 4,
                 tiles=[x2], params=[sp["g_ple"]], epi=_epi_res_norm, a_pre=_relu2_bf16)
    x4, gl, hn, pe = _mm(h3, gw["w_pleg"].reshape(1, D, D), mode="acc", name="f_gate", tm=T // 2,
                         outs=[F32, BF16, BF16, BF16], tiles=[x3, p_bf], params=[g_next, gw["w_ple"]], epi=_epi_ple)
    saved = dict(x=x, h1=h1, proj=proj, y=y, ca=ca, z=z, merged=merged, x2=x2, h2=h2, up=up, x3=x3, h3=h3,
                 pe=pe, gl=gl, p=p_bf)
    return x4, hn, saved


def _layer_bwd(dx4, sv, gw, sp, submit, early_group=False):
    dw = {}
    dx3, dx3b, dg_ple, dw["w_pleg"], dw["w_ple"] = _ple_bwd(dx4, sv, gw["w_pleg"], sp["g_ple"])
    dup, = _mm(dx3b, gw["w_down"], mode="out", trans_b=True, name="b_dact", outs=[BF16], tm=T,
               tiles=[sv["up"]], epi=_epi_dup)
    dw["w_down"] = _mm_tn(sv["up"], dx3b, nj=NDEV, split="row", name="b_dw_down", a_pre=_relu2_bf16)
    dw["w_up"] = _mm_tn(sv["h2"], dup, nj=NDEV, split="col", name="b_dw_up")
    if early_group:
        dw["w_up"], dup = lax.optimization_barrier((dw["w_up"], dup))
        dup = submit(dw, ("w_up", "w_down", "w_ple", "w_pleg"), dup)
    dx2, dx2b, dg_mlp = _mm(dup, gw["w_up"], mode="full", trans_b=True, name="b_dh2", tm=T // 4, stream_first=True,
                            outs=[F32, BF16], tiles=[sv["x2"], dx3], params=[sp["g_mlp"]], epi=_epi_rms_bwd, reds=[D])
    dm, dw["w_out"] = _out_bwd(dx2b, sv["merged"], gw["w_out"])
    dproj, dy, dw["w_branch"] = _merge_bwd(dm, sv["z"], sv["proj"], sv["y"], gw["w_branch"])
    dy = submit(dw, ("w_branch", "w_out") if early_group else BIG[1:], dy)
    dproj, dcw, dsc, vec, dpw, dws, dbs = _mixer_bwd(sv["proj"], sv["ca"], dy, dproj, sp)
    dw["w_in"] = _mm_tn(sv["h1"], dproj, nj=NDEV, split="col", name="b_dw_in")
    dw["w_in"], dproj = lax.optimization_barrier((dw["w_in"], dproj))
    dproj = submit(dw, BIG[:1], dproj)
    dx, dg_mix = _mm(dproj, gw["w_in"], mode="full", trans_b=True, name="b_dh1", outs=[F32], tm=T // 8,
                     stream_first=True,
                     tiles=[sv["x"], dx2], params=[sp["g_mix"]], epi=_epi_rms_bwd, reds=[D])
    small = dict(norm_mix=dg_mix[0], conf_dw=dcw[:CONF_K], conf_dw_b=vec[0], conf_ln_g=vec[1], conf_ln_b=vec[2],
                 pool_w=dpw, pool_scale=vec[3], sc_conv=dsc[:SC_K], gmlp_ln_g=vec[4], gmlp_ln_b=vec[5],
                 gmlp_ws=dws, gmlp_bs=dbs[:, :, 0], norm_mlp=dg_mlp[0], norm_ple=dg_ple[0])
    return dx, small


ANY = pl.BlockSpec(memory_space=pl.ANY)


def _mesh_pos():
    return lax.axis_index("x"), lax.axis_index("y"), lax.axis_index("c")


def _other_chips(x, y):
    return [(1 - x, y), (x, 1 - y), (1 - x, 1 - y)]


def _launch_comm(body, peers_of, operands, out_shapes, sems, name, seq_id):
    n_in, n_out = len(operands), len(out_shapes)
    if seq_id is None:
        return pl.pallas_call(body, name=name, in_specs=[ANY] * n_in, out_specs=[ANY] * n_out,
                              out_shape=out_shapes, scratch_shapes=sems)(*operands)

    def seq_body(*refs):
        peers = peers_of(*_mesh_pos())
        barrier = pltpu.get_barrier_semaphore()
        for peer in peers:
            pl.semaphore_signal(barrier, inc=1, device_id=peer, device_id_type=MESH)
        pl.semaphore_wait(barrier, len(peers))
        body(*refs)

    return pl.kernel(seq_body, name=name, out_type=out_shapes,
                     mesh=plsc.ScalarSubcoreMesh(axis_name="seq", num_cores=1), scratch_types=sems,
                     compiler_params=pltpu.CompilerParams(collective_id=seq_id))(*operands)


def _all_gather(shards, name, seq_id=None):
    n = len(shards)

    def body(*refs):
        s_refs, o_refs = refs[:n], refs[n:2 * n]
        send_sems, recv_sems, local_sems = refs[2 * n:]
        x, y, c = _mesh_pos()
        me = 4 * x + 2 * y + c
        here = (x, y, c)
        sibling = (x, y, 1 - c)
        chips = _other_chips(x, y)

        def slot(px, py, pc):
            return 4 * px + 2 * py + pc

        def copy(t, k, slot_idx, to, src=None):
            dst = o_refs[t].at[slot_idx]
            return pltpu.make_async_remote_copy(
                src_ref=dst if src is None else src, dst_ref=dst,
                send_sem=send_sems.at[t * 7 + k], recv_sem=recv_sems.at[t * 7 + k],
                device_id=to, device_id_type=MESH)

        mine = [pltpu.make_async_copy(s_refs[t], o_refs[t].at[me], local_sems.at[t]) for t in range(n)]
        for cp in mine:
            cp.start()
        first = []
        for t in range(n):
            for j, chip in enumerate(chips):
                first.append(copy(t, 1 + j, me, (*chip, c), src=s_refs[t]))
        for t in range(n):
            first.append(copy(t, 0, me, sibling, src=s_refs[t]))
        for cp in first:
            cp.start()
        passed = []
        for t in range(n):
            for j, chip in enumerate(chips):
                copy(t, 1 + j, slot(*chip, c), here).wait_recv()
                fwd = copy(t, 4 + j, slot(*chip, c), sibling)
                fwd.start()
                passed.append(fwd)
        for t in range(n):
            copy(t, 0, slot(x, y, 1 - c), here).wait_recv()
            for j, chip in enumerate(chips):
                copy(t, 4 + j, slot(*chip, 1 - c), here).wait_recv()
        for cp in first + passed:
            cp.wait_send()
        for cp in mine:
            cp.wait()

    def peers_of(x, y, c):
        return [(x, y, 1 - c)] + [(*chip, c) for chip in _other_chips(x, y)]

    return _launch_comm(
        body, peers_of, shards, [jax.ShapeDtypeStruct((NDEV,) + s.shape, s.dtype) for s in shards],
        [pltpu.SemaphoreType.DMA((7 * n,)), pltpu.SemaphoreType.DMA((7 * n,)), pltpu.SemaphoreType.DMA((n,))],
        name, seq_id)


def _rs_exchange(p4s, qs, name, seq_id=None):
    n_p, n_q = len(p4s), len(qs)

    def body(*refs):
        p_refs, q_refs = refs[:n_p], refs[n_p:n_p + n_q]
        rb_refs, rc_refs = refs[n_p + n_q:2 * n_p + n_q], refs[2 * n_p + n_q:2 * (n_p + n_q)]
        pair_send, pair_recv, chip_send, chip_recv, local_sems = refs[2 * (n_p + n_q):]
        x, y, c = _mesh_pos()
        a_idx = 2 * x + y
        chips = _other_chips(x, y)
        mine = [pltpu.make_async_copy(q_refs[t].at[a_idx], rc_refs[t].at[a_idx], local_sems.at[t])
                for t in range(n_q)]
        sends = []
        for t in range(n_q):
            for j, chip in enumerate(chips):
                sends.append(pltpu.make_async_remote_copy(
                    src_ref=q_refs[t].at[2 * chip[0] + chip[1]], dst_ref=rc_refs[t].at[a_idx],
                    send_sem=chip_send.at[t * 3 + j], recv_sem=chip_recv.at[t * 3 + j],
                    device_id=(*chip, c), device_id_type=MESH))
        pairs = [pltpu.make_async_remote_copy(
            src_ref=p_refs[t].at[:, 1 - c], dst_ref=rb_refs[t], send_sem=pair_send.at[t], recv_sem=pair_recv.at[t],
            device_id=(x, y, 1 - c), device_id_type=MESH) for t in range(n_p)]
        for cp in sends + mine + pairs:
            cp.start()
        for cp in pairs:
            cp.wait()
        for t in range(n_q):
            for j, chip in enumerate(chips):
                landed = rc_refs[t].at[2 * chip[0] + chip[1]]
                pltpu.make_async_remote_copy(
                    src_ref=landed, dst_ref=landed, send_sem=chip_send.at[t * 3 + j],
                    recv_sem=chip_recv.at[t * 3 + j], device_id=(x, y, c), device_id_type=MESH).wait_recv()
        for cp in sends:
            cp.wait_send()
        for cp in mine:
            cp.wait()

    def peers_of(x, y, c):
        peers = [(x, y, 1 - c)] if n_p else []
        return peers + ([(*chip, c) for chip in _other_chips(x, y)] if n_q else [])

    out_shapes = [jax.ShapeDtypeStruct((NCHIP,) + p.shape[2:], p.dtype) for p in p4s]
    out_shapes += [jax.ShapeDtypeStruct(q.shape, q.dtype) for q in qs]
    sems = [pltpu.SemaphoreType.DMA((max(n_p, 1),)), pltpu.SemaphoreType.DMA((max(n_p, 1),)),
            pltpu.SemaphoreType.DMA((max(3 * n_q, 1),)), pltpu.SemaphoreType.DMA((max(3 * n_q, 1),)),
            pltpu.SemaphoreType.DMA((max(n_q, 1),))]
    got = _launch_comm(body, peers_of, list(p4s) + list(qs), out_shapes, sems, name, seq_id)
    return got[:n_p], got[n_p:]


def _pair_sum(p4s, rbs, c_idx, name, nst=1):
    n = len(p4s)
    trs = [p.shape[2] // nst for p in p4s]

    def body(c_ref, *refs):
        del c_ref
        p_refs, r_refs, o_refs = refs[:n], refs[n:2 * n], refs[2 * n:]
        for p_ref, r_ref, o_ref in zip(p_refs, r_refs, o_refs):
            o_ref[...] = (p_ref[...].astype(F32) + r_ref[...].astype(F32)).astype(o_ref.dtype)

    in_specs = [pl.BlockSpec((None, None, tr, p.shape[3]), lambda b, i, c_ref: (b, c_ref[0], i, 0))
                for p, tr in zip(p4s, trs)]
    in_specs += [pl.BlockSpec((None, tr, p.shape[3]), lambda b, i, c_ref: (b, i, 0)) for p, tr in zip(p4s, trs)]
    out_specs = [pl.BlockSpec((None, tr, p.shape[3]), lambda b, i, c_ref: (b, i, 0)) for p, tr in zip(p4s, trs)]
    return pl.pallas_call(
        body, name=name,
        grid_spec=pltpu.PrefetchScalarGridSpec(num_scalar_prefetch=1, grid=(NCHIP, nst), in_specs=in_specs,
                                               out_specs=out_specs),
        out_shape=[jax.ShapeDtypeStruct((NCHIP,) + p.shape[2:], p.dtype) for p in p4s],
        compiler_params=_cp("arbitrary", "arbitrary"),
    )(c_idx, *p4s, *rbs)


class _GradientPipeline:
    def __init__(self, c_idx, results):
        self.c_idx, self.results, self.pending = c_idx, results, None

    def _sum_pending(self, chain):
        names, layer, p4s, rbs = self.pending
        qs = _pair_sum(p4s, rbs, self.c_idx, name="rs_pairsum_%d" % len(names))
        return lax.optimization_barrier((chain, qs))

    def submit(self, dw, names, layer, chain):
        qs, tag, seq_id = [], "pair", 3
        if self.pending is not None:
            chain, qs = self._sum_pending(chain)
            tag, seq_id = "pair_chip", 4
        p4s = [dw[n].reshape((NCHIP, 2) + BIG_SHARD[n]) for n in names]
        rbs, rcs = _rs_exchange(p4s, qs, name="rs_%s_%d" % (tag, len(names)), seq_id=seq_id)
        self._record(rcs)
        self.pending = (names, layer, p4s, rbs)
        return chain

    def finish(self, chain):
        chain, qs = self._sum_pending(chain)
        self._record(_rs_exchange([], qs, name="rs_chip_last", seq_id=5)[1])
        self.pending = None
        return chain

    def _record(self, rcs):
        if rcs:
            names, layer = self.pending[:2]
            for n, rc in zip(names, rcs):
                self.results[n][layer] = rc


def _adamw(w, g, m, v):
    m = ADAM_B1 * m + (1.0 - ADAM_B1) * g
    v = ADAM_B2 * v + (1.0 - ADAM_B2) * (g * g)
    m_hat = m / (1.0 - ADAM_B1 ** ADAM_STEP)
    v_hat = v / (1.0 - ADAM_B2 ** ADAM_STEP)
    delta = -ADAM_LR * (m_hat / (jnp.sqrt(v_hat) + ADAM_EPS) + ADAM_WD * w)
    return delta, m, v


def _adam_sharded(rcs, w, m, v, tr, name, first_layer, partial=None):
    _, r, c = w.shape
    nst = r // tr
    n_l = len(rcs)

    def body(*refs):
        rc_refs = refs[:n_l]
        w_ref, m_ref, v_ref = refs[n_l:n_l + 3]
        g_out, d_out, m_out, v_out = refs[-4:]
        layer = pl.program_id(0)
        for k, rc in enumerate(rc_refs):
            @pl.when(layer == k)
            def _():
                g = rc[0].astype(F32) + rc[1].astype(F32) + rc[2].astype(F32) + rc[3].astype(F32)
                delta, m_new, v_new = _adamw(w_ref[...], g, m_ref[...], v_ref[...])
                g_out[...] = g
                d_out[...] = delta
                m_out[...] = m_new
                v_out[...] = v_new

    rc_specs = [pl.BlockSpec((NCHIP, tr, c), lambda l, i, k=k: (0, jnp.where(l == k, i, 0), 0)) for k in range(n_l)]
    wspec = pl.BlockSpec((None, tr, c), lambda l, i: (first_layer + l, i, 0))
    carried = [] if partial is None else list(partial)
    return pl.pallas_call(
        body, name=name, grid=(n_l, nst),
        in_specs=rc_specs + [wspec] * 3 + [pl.BlockSpec(memory_space=pl.ANY)] * len(carried),
        out_specs=[wspec] * 4, out_shape=[jax.ShapeDtypeStruct(w.shape, F32)] * 4,
        input_output_aliases={n_l + 3 + k: k for k in range(len(carried))},
        compiler_params=_cp("arbitrary", "arbitrary"),
    )(*rcs, w, m, v, *carried)


def _adam_packed(g, w, m, v, direct):
    n_d = len(direct)

    def pieces(shape):
        width = shape[-1]
        count = 1
        for s in shape[:-1]:
            count *= s
        per_row = D // width
        out = []
        for k in range(count):
            idx = (k,) if len(shape) == 2 else (k // shape[1], k % shape[1])
            out.append((idx, k // per_row, (k % per_row) * width, width))
        return out

    def body(g_ref, w_ref, m_ref, v_ref, d_out, m_out, v_out, *outs):
        delta, m_new, v_new = _adamw(w_ref[...], g_ref[...], m_ref[...], v_ref[...])
        d_out[...] = delta
        m_out[...] = m_new
        v_out[...] = v_new
        for a, (_, row0, shape) in enumerate(direct):
            for src, dst in zip((g_ref, d_out, m_out, v_out), outs[4 * a:4 * a + 4]):
                for idx, row, lane0, width in pieces(shape):
                    piece = src[pl.ds(row0 + row, 1), lane0:lane0 + width]
                    if len(idx) == 1:
                        dst[pl.ds(idx[0], 1), :] = piece
                    else:
                        dst[idx[0], pl.ds(idx[1], 1), :] = piece

    out_shape = [jax.ShapeDtypeStruct(g.shape, F32)] * 3
    for _, _, shape in direct:
        out_shape += [jax.ShapeDtypeStruct(shape, F32)] * 4
    res = pl.pallas_call(body, name="adam_small", out_shape=out_shape,
                         compiler_params=pltpu.CompilerParams(vmem_limit_bytes=VMEM_LIMIT_BYTES))(g, w, m, v)
    return res[:3], {name: res[3 + 4 * a:7 + 4 * a] for a, (name, _, _) in enumerate(direct)}


def _sum4(rc):
    def body(rc_ref, o_ref):
        o_ref[...] = rc_ref[0] + rc_ref[1] + rc_ref[2] + rc_ref[3]

    return pl.pallas_call(
        body, name="small_sum", out_shape=jax.ShapeDtypeStruct(rc.shape[1:], F32),
    )(rc)


BIG = ("w_in", "w_branch", "w_out", "w_up", "w_down", "w_ple", "w_pleg")
BIG_SHARD = {"w_in": (D, D), "w_branch": (4 * W, GW), "w_out": (GW, D), "w_up": (D, W), "w_down": (W, D),
             "w_ple": (256, GW), "w_pleg": (GW, D)}
ADAM_ROWS = {"w_in": 256, "w_branch": 512, "w_out": 128, "w_up": 256, "w_down": 256, "w_ple": 256, "w_pleg": 128}
SMALL = (("norm_mix", (DEPTH, D)), ("conf_dw", (DEPTH, CONF_K, W)), ("conf_dw_b", (DEPTH, W)),
         ("conf_ln_g", (DEPTH, W)), ("conf_ln_b", (DEPTH, W)), ("pool_w", (DEPTH, 4, GW, GW)),
         ("pool_scale", (DEPTH, W)), ("sc_conv", (DEPTH, SC_K, W)), ("gmlp_ln_g", (DEPTH, W)),
         ("gmlp_ln_b", (DEPTH, W)), ("gmlp_ws", (DEPTH, 4, GW, GW)), ("gmlp_bs", (DEPTH, 4, GW)),
         ("norm_mlp", (DEPTH, D)), ("norm_ple", (DEPTH, D)), ("norm_final", (D,)))
CHANNEL_SHARDED = ("conf_dw", "sc_conv")
SMALL_ROWS = 80


def _pack(arrs, rows):
    flat = jnp.concatenate([a.reshape(-1) for a in arrs])
    return jnp.pad(flat, (0, rows * D - flat.shape[0])).reshape(rows, D)


def _unpack(packed, shapes):
    flat = packed.reshape(-1)
    out, off = [], 0
    for shp in shapes:
        size = 1
        for s in shp:
            size *= s
        out.append(flat[off:off + size].reshape(shp))
        off += size
    return out


def kernel(x, p, norm_mix, w_in, conf_dw, conf_dw_b, conf_ln_g, conf_ln_b, pool_w, pool_scale, sc_conv, gmlp_ln_g, gmlp_ln_b, gmlp_ws, gmlp_bs, w_branch, w_out, norm_mlp, w_up, w_down, norm_ple, w_ple, w_ple_gate, norm_final, loss_target, m_norm_mix, m_w_in, m_conf_dw, m_conf_dw_b, m_conf_ln_g, m_conf_ln_b, m_pool_w, m_pool_scale, m_sc_conv, m_gmlp_ln_g, m_gmlp_ln_b, m_gmlp_ws, m_gmlp_bs, m_w_branch, m_w_out, m_norm_mlp, m_w_up, m_w_down, m_norm_ple, m_w_ple, m_w_ple_gate, m_norm_final, v_norm_mix, v_w_in, v_conf_dw, v_conf_dw_b, v_conf_ln_g, v_conf_ln_b, v_pool_w, v_pool_scale, v_sc_conv, v_gmlp_ln_g, v_gmlp_ln_b, v_gmlp_ws, v_gmlp_bs, v_w_branch, v_w_out, v_norm_mlp, v_w_up, v_w_down, v_norm_ple, v_w_ple, v_w_ple_gate, v_norm_final):
    weights = dict(norm_mix=norm_mix, w_in=w_in, conf_dw=conf_dw, conf_dw_b=conf_dw_b, conf_ln_g=conf_ln_g,
                   conf_ln_b=conf_ln_b, pool_w=pool_w, pool_scale=pool_scale, sc_conv=sc_conv, gmlp_ln_g=gmlp_ln_g,
                   gmlp_ln_b=gmlp_ln_b, gmlp_ws=gmlp_ws, gmlp_bs=gmlp_bs, w_branch=w_branch, w_out=w_out,
                   norm_mlp=norm_mlp, w_up=w_up, w_down=w_down, norm_ple=norm_ple, w_ple=w_ple, w_pleg=w_ple_gate,
                   norm_final=norm_final)
    mom1 = dict(norm_mix=m_norm_mix, w_in=m_w_in, conf_dw=m_conf_dw, conf_dw_b=m_conf_dw_b, conf_ln_g=m_conf_ln_g,
                conf_ln_b=m_conf_ln_b, pool_w=m_pool_w, pool_scale=m_pool_scale, sc_conv=m_sc_conv,
                gmlp_ln_g=m_gmlp_ln_g, gmlp_ln_b=m_gmlp_ln_b, gmlp_ws=m_gmlp_ws, gmlp_bs=m_gmlp_bs,
                w_branch=m_w_branch, w_out=m_w_out, norm_mlp=m_norm_mlp, w_up=m_w_up, w_down=m_w_down,
                norm_ple=m_norm_ple, w_ple=m_w_ple, w_pleg=m_w_ple_gate, norm_final=m_norm_final)
    mom2 = dict(norm_mix=v_norm_mix, w_in=v_w_in, conf_dw=v_conf_dw, conf_dw_b=v_conf_dw_b, conf_ln_g=v_conf_ln_g,
                conf_ln_b=v_conf_ln_b, pool_w=v_pool_w, pool_scale=v_pool_scale, sc_conv=v_sc_conv,
                gmlp_ln_g=v_gmlp_ln_g, gmlp_ln_b=v_gmlp_ln_b, gmlp_ws=v_gmlp_ws, gmlp_bs=v_gmlp_bs,
                w_branch=v_w_branch, w_out=v_w_out, norm_mlp=v_norm_mlp, w_up=v_w_up, w_down=v_w_down,
                norm_ple=v_norm_ple, w_ple=v_w_ple, w_pleg=v_w_ple_gate, norm_final=v_norm_final)

    xi, yi, ci = _mesh_pos()
    me = 4 * xi + 2 * yi + ci
    c_idx = jnp.reshape(ci, (1,)).astype(jnp.int32)

    def shard(n, l):
        return weights[n][l].astype(BF16).reshape(BIG_SHARD[n])

    def first_of(l):
        return [shard("w_in", l), conf_dw[l], sc_conv[l]]

    gathered, conf_full, sc_full = [{} for _ in range(DEPTH)], [None] * DEPTH, [None] * DEPTH

    def keep_first(l, got):
        gathered[l]["w_in"], conf_full[l], sc_full[l] = got

    keep_first(0, _all_gather(first_of(0), name="ag_first", seq_id=1))
    for l in range(DEPTH):
        if l + 1 < DEPTH:
            got = list(_all_gather([shard(n, l) for n in BIG[1:]] + first_of(l + 1), name="ag_rest", seq_id=2))
            keep_first(l + 1, got[len(BIG) - 1:])
        else:
            got = (list(_all_gather([shard(n, l) for n in BIG[1:4]], name="ag_rest_a", seq_id=2))
                   + list(_all_gather([shard(n, l) for n in BIG[4:]], name="ag_rest_b", seq_id=2)))
        gathered[l].update(zip(BIG[1:], got[:len(BIG) - 1]))
        gathered[l]["w_branch"] = gathered[l]["w_branch"].reshape(NDEV, 4, W, GW)

    def small_params(l):
        return dict(cw=conf_full[l], cb=conf_dw_b[l][None], lg=conf_ln_g[l][None], lb=conf_ln_b[l][None],
                    pw=pool_w[l], ps=pool_scale[l][None], sc=sc_full[l], gg=gmlp_ln_g[l][None],
                    gb=gmlp_ln_b[l][None], ws=gmlp_ws[l], bst=gmlp_bs[l].T, g_mix=norm_mix[l][None],
                    g_mlp=norm_mlp[l][None], g_ple=norm_ple[l][None])

    xc = x.reshape(T, D)
    small_names = [n for n, _ in SMALL]

    def in_gradient_layout(n, shard, shape):
        if n not in CHANNEL_SHARDED:
            return shard
        return lax.dynamic_update_slice(jnp.zeros(shape, F32), shard, (0, 0, me * (W // NDEV)))

    small_state = [_pack([in_gradient_layout(n, src[n], shape) for n, shape in SMALL], NDEV * SMALL_ROWS)
                   for src in (weights, mom1, mom2)]
    xc, small_state = lax.optimization_barrier((xc, small_state))
    p_bf = p.reshape(DEPTH, T, 256).astype(BF16)
    h = _norm_first(xc, norm_mix[0][None])
    saved = []
    for l in range(DEPTH):
        g_next = norm_mix[l + 1][None] if l + 1 < DEPTH else norm_final[None]
        h, conf_g, sc_g = lax.optimization_barrier((h, conf_full[l], sc_full[l]))
        conf_full[l] = conf_g.transpose(1, 0, 2).reshape(CONF_K, W)
        sc_full[l] = sc_g.transpose(1, 0, 2).reshape(SC_K, W)
        xc, h, sv = _layer_fwd(xc, h, p_bf[l], gathered[l], small_params(l), g_next)
        saved.append(sv)

    dxc, dg_final, loss_part = _loss_head(xc, loss_target.reshape(T, D), norm_final[None])
    loss = lax.psum(loss_part[0, 0], ("x", "y", "c"))
    small_grads = [None] * DEPTH
    rcs = {n: [None] * DEPTH for n in BIG}
    pipeline = _GradientPipeline(c_idx, rcs)
    for l in reversed(range(DEPTH)):
        dxc, small_grads[l] = _layer_bwd(dxc, saved[l], gathered[l], small_params(l),
                                         lambda dw, names, value, l=l: pipeline.submit(dw, names, l, value),
                                         early_group=(l == 0))

    def adam_sharded(first_layer, n_layers, partial, tag, names=BIG):
        outs = {}
        for n in names:
            shp = (DEPTH,) + BIG_SHARD[n]
            outs[n] = _adam_sharded(rcs[n][first_layer:first_layer + n_layers], weights[n].reshape(shp),
                                    mom1[n].reshape(shp), mom2[n].reshape(shp), ADAM_ROWS[n],
                                    "adam_%s_%s" % (n, tag), first_layer, None if partial is None else partial[n])
        return outs

    stacked = {n: jnp.stack([small_grads[l][n] for l in range(DEPTH)]) for n, _ in SMALL if n != "norm_final"}
    stacked["norm_final"] = dg_final[0]
    packed = _pack([stacked[n] for n, _ in SMALL], NDEV * SMALL_ROWS).reshape(NCHIP, 2, SMALL_ROWS, D)
    (pair_small,), _ = _rs_exchange([packed], [], name="rs_pair_small")
    q_small = _pair_sum([packed], [pair_small], c_idx, name="rs_pairsum_small")
    dxc, upper, q_small = lax.optimization_barrier((dxc, {n: rcs[n][1:] for n in BIG}, q_small))
    _, (chips_small,) = _rs_exchange([], q_small, name="rs_chip_small", seq_id=6)
    dxc, upper = pipeline.finish((dxc, upper))
    for n in BIG:
        rcs[n][1:] = upper[n]
    partial = adam_sharded(1, DEPTH - 1, None, "upper")
    last = adam_sharded(0, 1, partial, "last", names=BIG[:1])
    partial = {n: partial[n] for n in BIG[1:]}
    last, partial, chips_small = lax.optimization_barrier((last, partial, chips_small))
    reduced_slot = _sum4(chips_small)
    reduced = _all_gather([reduced_slot], name="ag_small", seq_id=7)[0]
    small_full = dict(zip([n for n, _ in SMALL], _unpack(reduced, [s for _, s in SMALL])))
    grads, deltas, new_m, new_v = {}, {}, {}, {}
    direct, row = [], 0
    for n, shape in SMALL:
        if len(shape) == 1 or shape[-2] == DEPTH:
            direct.append((n, row, (1,) * (2 - len(shape)) + tuple(shape)))
        size = 1
        for s in shape:
            size *= s
        row += size // D
    (d_p, m_p, v_p), own_shape = _adam_packed(reduced.reshape(NDEV * SMALL_ROWS, D), *small_state, direct)
    small_shapes = [s for _, s in SMALL]

    def own_channels(n, full):
        return lax.dynamic_slice_in_dim(full, me * (W // NDEV), W // NDEV, axis=2) if n in CHANNEL_SHARDED else full

    for n, d_, m_, v_ in zip(small_names, _unpack(d_p, small_shapes), _unpack(m_p, small_shapes),
                             _unpack(v_p, small_shapes)):
        if n in own_shape:
            grads[n], deltas[n], new_m[n], new_v[n] = [a.reshape(weights[n].shape) for a in own_shape[n]]
        else:
            grads[n], deltas[n], new_m[n], new_v[n] = (own_channels(n, small_full[n]), own_channels(n, d_),
                                                       own_channels(n, m_), own_channels(n, v_))

    last.update(adam_sharded(0, 1, partial, "last", names=BIG[1:]))
    for n, (g_, d_, m_, v_) in last.items():
        full = weights[n].shape
        grads[n], deltas[n], new_m[n], new_v[n] = g_.reshape(full), d_.reshape(full), m_.reshape(full), v_.reshape(full)

    order = ("norm_mix", "w_in", "conf_dw", "conf_dw_b", "conf_ln_g", "conf_ln_b", "pool_w", "pool_scale", "sc_conv",
             "gmlp_ln_g", "gmlp_ln_b", "gmlp_ws", "gmlp_bs", "w_branch", "w_out", "norm_mlp", "w_up", "w_down",
             "norm_ple", "w_ple", "w_pleg", "norm_final")
    return (loss, dxc.reshape(1, T, D), *[grads[n] for n in order], *[deltas[n] for n in order],
            *[new_m[n] for n in order], *[new_v[n] for n in order])
```

```python
import functools

import jax
import jax.numpy as jnp
from jax import lax
from jax.experimental import pallas as pl
from jax.experimental.pallas import tpu as pltpu
from jax.experimental.pallas import tpu_sc as plsc

F32 = jnp.float32
BF16 = jnp.bfloat16

DEPTH = 4
T = 2048
D = 1024
W = 512
NDEV = 8
NCHIP = 4
EPS = 1e-6
CONF_K = 31
SC_K = 3
POOL_WINDOWS = (2, 4, 8, 16)
GW = 128
HB = 32
HA = 32
COLS_IN = 8192
MIX_COLS = 4096

ADAM_LR = 0.001
ADAM_B1 = 0.9
ADAM_B2 = 0.999
ADAM_EPS = 1e-08
ADAM_WD = 0.01
ADAM_STEP = 10

VMEM_LIMIT_BYTES = 56 * 1024 * 1024
MESH = pl.DeviceIdType.MESH


def _cp(*sem):
    return pltpu.CompilerParams(dimension_semantics=tuple(sem), vmem_limit_bytes=VMEM_LIMIT_BYTES)


def _sig(x):
    return jax.nn.sigmoid(x)


def _rms(x, g):
    r = lax.rsqrt(jnp.mean(x * x, axis=-1, keepdims=True) + EPS)
    return x * r * g


def _rms_bwd(dh, x, g, dres):
    r = lax.rsqrt(jnp.mean(x * x, axis=-1, keepdims=True) + EPS)
    xh = x * r
    u = dh * g
    dx = r * (u - xh * jnp.mean(u * xh, axis=-1, keepdims=True)) + dres
    dg = jnp.sum(dh * xh, axis=0, keepdims=True)
    return dx, dg


def _ln_stats(x):
    mu = jnp.mean(x, axis=-1, keepdims=True)
    xc = x - mu
    rstd = lax.rsqrt(jnp.mean(xc * xc, axis=-1, keepdims=True) + EPS)
    return xc * rstd, rstd


def _ln_bwd(dxh, xh, rstd):
    return rstd * (dxh - jnp.mean(dxh, axis=-1, keepdims=True) - xh * jnp.mean(dxh * xh, axis=-1, keepdims=True))


def _rowsum(x):
    return jnp.sum(x, axis=0, keepdims=True)


EPI_ROWS = 256


def _relu2_bf16(up):
    r = jnp.maximum(up.astype(F32), 0.0)
    return (r * r).astype(BF16)


def _mm(a, b3, *, mode, name, outs, trans_b=False, tm=512, tiles=(), params=(), epi=None, reds=(), a_pre=None,
        stream_first=False):
    t_, ka = a.shape
    nj, r, c = b3.shape
    kb, nb = (c, r) if trans_b else (r, c)
    nt = t_ // tm
    out_mode = mode == "out"
    full = mode == "full"
    assert trans_b or not full
    if out_mode:
        assert ka == kb and not reds
        grid = (nj, nt)
        a_map = lambda g0, g1: (g1, 0)
        b_map = lambda g0, g1: (g0, 0, 0)
        t_map = lambda g0, g1: (g1, g0)
        width = nj * nb
    else:
        assert ka == nj * kb
        grid = (nt, 1 if full else nj)
        a_map = lambda g0, g1: (g0, g1)
        b_map = lambda g0, g1: (g1, 0, 0)
        t_map = lambda g0, g1: (g0, 0)
        width = nb
    n_t, n_p, n_o, n_r = len(tiles), len(params), len(outs), len(reds)
    use_acc = (not out_mode) and nj > 1 and not full
    dims = (((1,), (1,)), ((), ())) if trans_b else (((1,), (0,)), ((), ()))

    def body(a_ref, b_ref, *rest):
        t_refs = rest[:n_t]
        p_refs = rest[n_t:n_t + n_p]
        o_refs = rest[n_t + n_p:n_t + n_p + n_o]
        r_refs = rest[n_t + n_p + n_o:n_t + n_p + n_o + n_r]
        i = pl.program_id(1 if out_mode else 0)
        a_val = a_ref[...] if a_pre is None else a_pre(a_ref[...])
        if full:
            b_all, b_sems = rest[-2 - stream_first], rest[-1 - stream_first]

            def weight_copies():
                return [pltpu.make_async_copy(b_ref.at[j], b_all.at[:, j * c:(j + 1) * c], b_sems.at[j])
                        for j in range(nj)]

            if stream_first:
                part = rest[-1]

                @pl.when(i == 0)
                def _():
                    cps = weight_copies()
                    for cp in cps:
                        cp.start()
                    acc = None
                    for j, cp in enumerate(cps):
                        cp.wait()
                        term = lax.dot_general(a_val[:, j * c:(j + 1) * c], b_all[:, j * c:(j + 1) * c], dims,
                                               preferred_element_type=F32)
                        acc = term if acc is None else acc + term
                    part[...] = acc

                @pl.when(i > 0)
                def _():
                    part[...] = lax.dot_general(a_val, b_all[...], dims, preferred_element_type=F32)
            else:
                @pl.when(i == 0)
                def _():
                    cps = weight_copies()
                    for cp in cps:
                        cp.start()
                    for cp in cps:
                        cp.wait()

                part = lax.dot_general(a_val, b_all[...], dims, preferred_element_type=F32)
        else:
            part = lax.dot_general(a_val, b_ref[...], dims, preferred_element_type=F32)

        def finish(acc_rows):
            totals = [None] * n_r
            for r0 in range(0, tm, min(tm, EPI_ROWS)):
                rows = slice(r0, r0 + min(tm, EPI_ROWS))
                if epi is None:
                    res, rr = (acc_rows(rows),), ()
                else:
                    res, rr = epi(acc_rows(rows), [t[rows, :] for t in t_refs], [p[...] for p in p_refs])
                for o_ref, val in zip(o_refs, res):
                    o_ref[rows, :] = val.astype(o_ref.dtype)
                totals = [val if tot is None else tot + val for tot, val in zip(totals, rr)]
            for r_ref, val in zip(r_refs, totals):
                @pl.when(i == 0)
                def _():
                    r_ref[...] = val

                @pl.when(i > 0)
                def _():
                    r_ref[...] += val

        if use_acc:
            acc_ref = rest[-1]
            j = pl.program_id(1)

            @pl.when(j == 0)
            def _():
                acc_ref[...] = part

            @pl.when(jnp.logical_and(j > 0, j < nj - 1))
            def _():
                acc_ref[...] += part

            @pl.when(j == nj - 1)
            def _():
                finish(lambda rows: acc_ref[rows, :] + part[rows])
        else:
            finish(lambda rows: part[rows, :])

    const2 = lambda g0, g1: (0, 0)
    if full:
        in_specs = [pl.BlockSpec((tm, ka), a_map), pl.BlockSpec(memory_space=pl.ANY)]
        scratch = [pltpu.VMEM((r, nj * c), b3.dtype), pltpu.SemaphoreType.DMA((nj,))]
        scratch += [pltpu.VMEM((tm, nb), F32)] if stream_first else []
    else:
        in_specs = [pl.BlockSpec((tm, kb), a_map), pl.BlockSpec((None, r, c), b_map)]
        scratch = [pltpu.VMEM((tm, nb), F32)] if use_acc else []
    in_specs += [pl.BlockSpec((tm, t.shape[1] // nj if out_mode else t.shape[1]), t_map) for t in tiles]
    in_specs += [pl.BlockSpec(p.shape, lambda g0, g1, nd=p.ndim: (0,) * nd) for p in params]
    out_specs = [pl.BlockSpec((tm, nb), t_map) for _ in outs] + [pl.BlockSpec((1, w), const2) for w in reds]
    out_shape = [jax.ShapeDtypeStruct((t_, width), dt) for dt in outs]
    out_shape += [jax.ShapeDtypeStruct((1, w), F32) for w in reds]
    res = pl.pallas_call(
        body, name=name, grid=grid, in_specs=in_specs, out_specs=out_specs, out_shape=out_shape,
        scratch_shapes=scratch, compiler_params=_cp("arbitrary", "arbitrary"),
    )(a, b3, *tiles, *params)
    return res


def _mm_tn(a, g, *, nj, split, name, out_dtype=BF16, a_pre=None):
    t_ = a.shape[0]
    if split == "col":
        r, c = a.shape[1], g.shape[1] // nj
        a_spec = pl.BlockSpec((t_, r), lambda j: (0, 0))
        g_spec = pl.BlockSpec((t_, c), lambda j: (0, j))
    else:
        r, c = a.shape[1] // nj, g.shape[1]
        a_spec = pl.BlockSpec((t_, r), lambda j: (0, j))
        g_spec = pl.BlockSpec((t_, c), lambda j: (0, 0))

    def body(a_ref, g_ref, o_ref):
        a_val = a_ref[...] if a_pre is None else a_pre(a_ref[...])
        o_ref[...] = lax.dot_general(a_val, g_ref[...], (((0,), (0,)), ((), ())),
                                     preferred_element_type=F32).astype(o_ref.dtype)

    return pl.pallas_call(
        body, name=name, grid=(nj,), in_specs=[a_spec, g_spec],
        out_specs=pl.BlockSpec((None, r, c), lambda j: (j, 0, 0)),
        out_shape=jax.ShapeDtypeStruct((nj, r, c), out_dtype),
        compiler_params=_cp("arbitrary"),
    )(a, g)


def _epi_res_norm(acc, tiles, params):
    x_new = tiles[0] + acc
    return (x_new, _rms(x_new, params[0])), ()


def _epi_ple(acc, tiles, params):
    x_old, p_tile = tiles
    g_next, w_ple8 = params
    pe = jnp.concatenate([jnp.dot(p_tile, w_ple8[j], preferred_element_type=F32) for j in range(NDEV)], axis=1)
    x_new = x_old + pe * _sig(acc)
    return (x_new, acc, _rms(x_new, g_next), pe), ()


def _epi_rms_bwd(acc, tiles, params):
    dx, dg = _rms_bwd(acc, tiles[0], params[0], tiles[1])
    return (dx, dx), (dg,)


def _epi_dup(acc, tiles, params):
    return (acc * (2.0 * jnp.maximum(tiles[0].astype(F32), 0.0)),), ()


def _tri_mask():
    row = lax.broadcasted_iota(jnp.int32, (GW, GW), 0)
    col = lax.broadcasted_iota(jnp.int32, (GW, GW), 1)
    return row >= col


def _small_specs(sp_list):
    return [pl.BlockSpec(p.shape, (lambda i: (0, 0)) if p.ndim == 2 else (lambda i: (0, 0, 0))) for p in sp_list]


SUBLANES = 8


def _tap_sum(src, w_ref, taps, rows, stage):
    groups = {}
    for off, k in taps:
        groups.setdefault(off % SUBLANES, []).append((off - off % SUBLANES, k))
    out = None
    for res, members in sorted(groups.items()):
        n = rows if res == 0 else rows + SUBLANES
        part = None
        for base, k in members:
            term = w_ref[k:k + 1, :] * src[pl.ds(base, n), :]
            part = term if part is None else part + term
        if res:
            stage[0:n, :] = part
            part = stage[pl.ds(res, rows), :]
        out = part if out is None else out + part
    return out


def _tap_grads(grad, src, offsets, rows, stage, out_ref):
    pad = SUBLANES
    stage[0:pad, :] = jnp.zeros((pad, grad.shape[1]), F32)
    stage[pad:pad + rows, :] = grad
    stage[pad + rows:2 * pad + rows, :] = jnp.zeros((pad, grad.shape[1]), F32)
    groups = {}
    for k, off in enumerate(offsets):
        groups.setdefault(off % SUBLANES, []).append((off - off % SUBLANES, k))
    for res, members in sorted(groups.items()):
        shifted = stage[pl.ds(pad - res, rows + pad), :]
        for base, k in members:
            out_ref[k:k + 1, :] += _rowsum(shifted * src[pl.ds(base, rows + pad), :])


def _mixer_params(sp):
    return [sp["cw"], sp["cb"], sp["lg"], sp["lb"], sp["pw"], sp["ps"], sp["sc"], sp["gg"], sp["gb"], sp["ws"], sp["bst"]]


def _mixer_fwd(proj, sp, tm=256):
    nt = T // tm
    per = tm // HB

    conv_taps = [(HB - (CONF_K - 1) + k, k) for k in range(CONF_K)]

    def body(main_ref, halo_ref, cw, cb, lg, lb, pw, ps, sc, gg, gb, ws, bst, y_ref, ca_ref, ext, stage):
        i = pl.program_id(0)
        keep = (i > 0).astype(F32)

        def mcol(c0):
            return main_ref[:, c0:c0 + W].astype(F32)

        def hcol(c0):
            return halo_ref[:, c0:c0 + W].astype(F32)

        ext[0:HB, :] = hcol(0) * _sig(hcol(W)) * keep
        ext[HB:HB + tm, :] = mcol(0) * _sig(mcol(W))
        ca = (_tap_sum(ext, cw, conv_taps, tm, stage) + cb[...]).astype(BF16)
        ca_ref[...] = ca
        xh, _ = _ln_stats(ca.astype(F32))
        n = xh * lg[...] + lb[...]
        y_ref[:, 0:W] = (n * _sig(n)).astype(BF16)

        pin = mcol(1024)
        ext[0:HB, :] = hcol(1024) * keep
        ext[HB:HB + tm, :] = pin
        pos = (i * tm + lax.broadcasted_iota(jnp.int32, (tm, 1), 0) + 1).astype(F32)
        for g, w in enumerate(POOL_WINDOWS):
            lo = g * GW
            s = ext[pl.ds(HB, tm), lo:lo + GW]
            for j in range(1, w):
                s = s + ext[pl.ds(HB - j, tm), lo:lo + GW]
            pooled = s / jnp.minimum(pos, float(w)) - pin[:, lo:lo + GW]
            mixed = jnp.dot(pooled.astype(BF16), pw[g].astype(BF16), preferred_element_type=F32)
            y_ref[:, W + lo:W + lo + GW] = (mixed * ps[:, lo:lo + GW]).astype(BF16)

        ext[0:HB, :] = hcol(2048) * hcol(2560) * keep
        ext[HB:HB + tm, :] = mcol(2048) * mcol(2560)
        cv = sc[0:1, :] * ext[pl.ds(HB - 2, tm), :]
        cv = cv + sc[1:2, :] * ext[pl.ds(HB - 1, tm), :]
        cv = cv + sc[2:3, :] * ext[pl.ds(HB, tm), :]
        y_ref[:, 2 * W:3 * W] = (mcol(1536) * cv).astype(BF16)

        vh, _ = _ln_stats(mcol(3584))
        vn = (vh * gg[...] + gb[...]).astype(BF16)
        u = mcol(3072)
        tri = _tri_mask()
        for g in range(4):
            lo = g * GW
            wm = jnp.where(tri, ws[g], 0.0).astype(BF16)
            for c in range(tm // GW):
                r0 = c * GW
                sg = jnp.dot(wm, vn[r0:r0 + GW, lo:lo + GW], preferred_element_type=F32) + bst[:, g:g + 1]
                y_ref[r0:r0 + GW, 3 * W + lo:3 * W + lo + GW] = (u[r0:r0 + GW, lo:lo + GW] * sg).astype(BF16)

    plist = _mixer_params(sp)
    in_specs = [pl.BlockSpec((tm, MIX_COLS), lambda i: (i, 0)),
                pl.BlockSpec((HB, MIX_COLS), lambda i: (jnp.maximum(i * per - 1, 0), 0))]
    in_specs += _small_specs(plist)
    return pl.pallas_call(
        body, name="f_mixers", grid=(nt,), in_specs=in_specs,
        out_specs=[pl.BlockSpec((tm, 4 * W), lambda i: (i, 0)), pl.BlockSpec((tm, W), lambda i: (i, 0))],
        out_shape=[jax.ShapeDtypeStruct((T, 4 * W), BF16), jax.ShapeDtypeStruct((T, W), BF16)],
        scratch_shapes=[pltpu.VMEM((HB + tm, W), F32), pltpu.VMEM((tm + SUBLANES, W), F32)],
        compiler_params=_cp("arbitrary"),
    )(proj, proj, *plist)


def _assemble_wb(wb8_ref, wbf_ref):
    for k in range(4):
        for j in range(NDEV):
            wbf_ref[k, :, j * GW:(j + 1) * GW] = wb8_ref[j, k]


def _merge_fwd(y, proj, wb8, tm=256):
    nt = T // tm

    def body(y_ref, gate_ref, wb8_ref, z_ref, m_ref, wbf):
        @pl.when(pl.program_id(0) == 0)
        def _():
            _assemble_wb(wb8_ref, wbf)

        m = jnp.zeros((tm, D), F32)
        for k in range(4):
            zk = jnp.dot(y_ref[:, k * W:(k + 1) * W], wbf[k], preferred_element_type=F32)
            z_ref[:, k * D:(k + 1) * D] = zk.astype(BF16)
            m = m + _sig(gate_ref[:, k * D:(k + 1) * D].astype(F32)) * zk
        m_ref[...] = m.astype(BF16)

    return pl.pallas_call(
        body, name="f_merge", grid=(nt,),
        in_specs=[pl.BlockSpec((tm, 4 * W), lambda i: (i, 0)),
                  pl.BlockSpec((tm, 4 * D), lambda i: (i, 1)),
                  pl.BlockSpec(wb8.shape, lambda i: (0, 0, 0, 0))],
        out_specs=[pl.BlockSpec((tm, 4 * D), lambda i: (i, 0)), pl.BlockSpec((tm, D), lambda i: (i, 0))],
        out_shape=[jax.ShapeDtypeStruct((T, 4 * D), BF16), jax.ShapeDtypeStruct((T, D), BF16)],
        scratch_shapes=[pltpu.VMEM((4, W, D), BF16)],
        compiler_params=_cp("arbitrary"),
    )(y, proj, wb8)


def _merge_bwd(dm, z, proj, y, wb8, tm=256):
    nt = T // tm

    def body(dm_ref, z_ref, gate_ref, y_ref, wb8_ref, dp_ref, dy_ref, dwb_ref, wbf, acc):
        i = pl.program_id(0)

        @pl.when(i == 0)
        def _():
            _assemble_wb(wb8_ref, wbf)

        dmv = dm_ref[...].astype(F32)
        for k in range(4):
            s = _sig(gate_ref[:, k * D:(k + 1) * D].astype(F32))
            dzk = (dmv * s).astype(BF16)
            dp_ref[:, k * D:(k + 1) * D] = (dmv * z_ref[:, k * D:(k + 1) * D].astype(F32) * s * (1.0 - s)).astype(BF16)
            dyk = lax.dot_general(dzk, wbf[k], (((1,), (1,)), ((), ())), preferred_element_type=F32)
            dy_ref[:, k * W:(k + 1) * W] = dyk.astype(BF16)
            part = lax.dot_general(y_ref[:, k * W:(k + 1) * W], dzk, (((0,), (0,)), ((), ())),
                                   preferred_element_type=F32)

            @pl.when(i == 0)
            def _():
                acc[k] = part

            @pl.when(i > 0)
            def _():
                acc[k] += part

        @pl.when(i == nt - 1)
        def _():
            for k in range(4):
                for j in range(NDEV):
                    dwb_ref[j, k] = acc[k, :, j * GW:(j + 1) * GW].astype(BF16)

    return pl.pallas_call(
        body, name="b_merge", grid=(nt,),
        in_specs=[pl.BlockSpec((tm, D), lambda i: (i, 0)),
                  pl.BlockSpec((tm, 4 * D), lambda i: (i, 0)),
                  pl.BlockSpec((tm, 4 * D), lambda i: (i, 1)),
                  pl.BlockSpec((tm, 4 * W), lambda i: (i, 0)),
                  pl.BlockSpec(wb8.shape, lambda i: (0, 0, 0, 0))],
        out_specs=[pl.BlockSpec((tm, 4 * D), lambda i: (i, 1)),
                   pl.BlockSpec((tm, 4 * W), lambda i: (i, 0)),
                   pl.BlockSpec(wb8.shape, lambda i: (0, 0, 0, 0))],
        out_shape=[jax.ShapeDtypeStruct((T, COLS_IN), BF16),
                   jax.ShapeDtypeStruct((T, 4 * W), BF16),
                   jax.ShapeDtypeStruct(wb8.shape, BF16)],
        scratch_shapes=[pltpu.VMEM((4, W, D), BF16), pltpu.VMEM((4, W, D), F32)],
        compiler_params=_cp("arbitrary"),
    )(dm, z, proj, y, wb8)


def _mixer_bwd(proj, ca_saved, dy, dproj, sp, tm=512):
    nt = T // tm
    per = tm // HB
    ne = tm + HA
    last_blk = T // HA - 1
    conv_taps = [(HB - (CONF_K - 1) + k, k) for k in range(CONF_K)]

    def body(main_ref, hb_ref, ha_ref, ca_ref, cah_ref, dy_ref, dyh_ref, cw, cb, lg, lb, pw, ps, sc, gg, gb, ws, bst,
             dp_any, dp_ref, dcw_ref, dsc_ref, vec_ref, dpw_ref, dws_ref, dbs_ref, e1, e2, e3, stage):
        del dp_any, cb
        i = pl.program_id(0)
        keep_b = (i > 0).astype(F32)
        keep_a = (i < nt - 1).astype(F32)

        @pl.when(i == 0)
        def _():
            dcw_ref[...] = jnp.zeros_like(dcw_ref)
            dsc_ref[...] = jnp.zeros_like(dsc_ref)
            vec_ref[...] = jnp.zeros_like(vec_ref)
            dpw_ref[...] = jnp.zeros_like(dpw_ref)
            dws_ref[...] = jnp.zeros_like(dws_ref)
            dbs_ref[...] = jnp.zeros_like(dbs_ref)

        def mcol(c0):
            return main_ref[:, c0:c0 + W].astype(F32)

        def hbcol(c0):
            return hb_ref[:, c0:c0 + W].astype(F32)

        def hacol(c0):
            return ha_ref[:, c0:c0 + W].astype(F32)

        def load_dy(c0):
            e2[0:tm, :] = dy_ref[:, c0:c0 + W].astype(F32)
            e2[tm:ne, :] = dyh_ref[:, c0:c0 + W].astype(F32) * keep_a

        a = mcol(0)
        sa = _sig(mcol(W))
        e1[0:HB, :] = hbcol(0) * _sig(hbcol(W)) * keep_b
        e1[HB:HB + tm, :] = a * sa
        e1[HB + tm:HB + tm + SUBLANES, :] = jnp.zeros((SUBLANES, W), F32)
        e2[0:tm, :] = ca_ref[...].astype(F32)
        e2[tm:ne, :] = cah_ref[...].astype(F32)
        xh, rstd = _ln_stats(e2[0:ne, :])
        nn = xh * lg[...] + lb[...]
        s = _sig(nn)
        load_dy(0)
        dn = e2[0:ne, :] * (s * (1.0 + nn * (1.0 - s)))
        vec_ref[1:2, :] += _rowsum(dn[0:tm] * xh[0:tm])
        vec_ref[2:3, :] += _rowsum(dn[0:tm])
        dca = _ln_bwd(dn * lg[...], xh, rstd)
        e3[0:ne, :] = dca
        dmain = dca[0:tm]
        vec_ref[0:1, :] += _rowsum(dmain)
        _tap_grads(dmain, e1, [off for off, _ in conv_taps], tm, stage, dcw_ref)
        dglu = _tap_sum(e3, cw, [(CONF_K - 1 - k, k) for k in range(CONF_K)], tm, stage)
        dp_ref[:, 0:W] = (dglu * sa).astype(BF16)
        dp_ref[:, W:2 * W] = (dglu * a * sa * (1.0 - sa)).astype(BF16)

        pin = mcol(1024)
        e1[0:HB, :] = hbcol(1024) * keep_b
        e1[HB:HB + tm, :] = pin
        load_dy(W)
        dyb = e2[0:ne, :]
        pos_m = (i * tm + lax.broadcasted_iota(jnp.int32, (tm, 1), 0) + 1).astype(F32)
        pos_e = (i * tm + lax.broadcasted_iota(jnp.int32, (ne, 1), 0) + 1).astype(F32)
        for g, w in enumerate(POOL_WINDOWS):
            lo = g * GW
            acc = e1[pl.ds(HB, tm), lo:lo + GW]
            for j in range(1, w):
                acc = acc + e1[pl.ds(HB - j, tm), lo:lo + GW]
            pooled = (acc / jnp.minimum(pos_m, float(w)) - pin[:, lo:lo + GW]).astype(BF16)
            pwb = pw[g].astype(BF16)
            mixed = jnp.dot(pooled, pwb, preferred_element_type=F32)
            dyb_g = dyb[:, lo:lo + GW]
            vec_ref[3:4, lo:lo + GW] += _rowsum(dyb_g[0:tm] * mixed)
            dmb = (dyb_g * ps[:, lo:lo + GW]).astype(BF16)
            dpw_ref[g] += lax.dot_general(pooled, dmb[0:tm], (((0,), (0,)), ((), ())), preferred_element_type=F32)
            dpool = lax.dot_general(dmb, pwb, (((1,), (1,)), ((), ())), preferred_element_type=F32)
            e3[0:ne, lo:lo + GW] = dpool / jnp.minimum(pos_e, float(w))
            back = e3[pl.ds(0, tm), lo:lo + GW]
            for j in range(1, w):
                back = back + e3[pl.ds(j, tm), lo:lo + GW]
            dp_ref[:, 1024 + lo:1024 + lo + GW] = (back - dpool[0:tm]).astype(BF16)

        cg = mcol(2048)
        hx = mcol(2560)
        e1[0:HB, :] = hbcol(2048) * hbcol(2560) * keep_b
        e1[HB:HB + tm, :] = cg * hx
        load_dy(2 * W)
        dyc = e2[0:tm, :]
        dconv = dyc * mcol(1536)
        e3[0:tm, :] = dconv
        e3[tm:ne, :] = e2[tm:ne, :] * hacol(1536)
        cv = sc[0:1, :] * e1[pl.ds(HB - 2, tm), :]
        for k in range(1, SC_K):
            cv = cv + sc[k:k + 1, :] * e1[pl.ds(HB - 2 + k, tm), :]
        dp_ref[:, 1536:2048] = (dyc * cv).astype(BF16)
        for k in range(SC_K):
            dsc_ref[k:k + 1, :] += _rowsum(dconv * e1[pl.ds(HB - 2 + k, tm), :])
        dq = sc[0:1, :] * e3[pl.ds(2, tm), :]
        for k in range(1, SC_K):
            dq = dq + sc[k:k + 1, :] * e3[pl.ds(2 - k, tm), :]
        dp_ref[:, 2048:2560] = (dq * hx).astype(BF16)
        dp_ref[:, 2560:3072] = (dq * cg).astype(BF16)

        u = mcol(3072)
        vh, vr = _ln_stats(mcol(3584))
        vn = (vh * gg[...] + gb[...]).astype(BF16)
        dyd = dy_ref[:, 3 * W:4 * W].astype(F32)
        tri = _tri_mask()
        for g in range(4):
            lo = g * GW
            wm = jnp.where(tri, ws[g], 0.0).astype(BF16)
            dws_g = jnp.zeros((GW, GW), F32)
            dbs_g = jnp.zeros((GW, 1), F32)
            for c in range(tm // GW):
                r0 = c * GW
                blk = vn[r0:r0 + GW, lo:lo + GW]
                sg = jnp.dot(wm, blk, preferred_element_type=F32) + bst[:, g:g + 1]
                dyd_b = dyd[r0:r0 + GW, lo:lo + GW]
                dp_ref[r0:r0 + GW, 3072 + lo:3072 + lo + GW] = (dyd_b * sg).astype(BF16)
                dsg = dyd_b * u[r0:r0 + GW, lo:lo + GW]
                dsgb = dsg.astype(BF16)
                dbs_g = dbs_g + jnp.sum(dsg, axis=-1, keepdims=True)
                dws_g = dws_g + lax.dot_general(dsgb, blk, (((1,), (1,)), ((), ())), preferred_element_type=F32)
                e1[r0:r0 + GW, lo:lo + GW] = lax.dot_general(wm, dsgb, (((0,), (0,)), ((), ())),
                                                             preferred_element_type=F32)
            dws_ref[g] += jnp.where(tri, dws_g, 0.0)
            dbs_ref[g] += jnp.broadcast_to(dbs_g, (GW, GW))
        dvn = e1[0:tm, :]
        vec_ref[4:5, :] += _rowsum(dvn * vh)
        vec_ref[5:6, :] += _rowsum(dvn)
        dp_ref[:, 3584:4096] = _ln_bwd(dvn * gg[...], vh, vr).astype(BF16)

    plist = _mixer_params(sp)
    in_specs = [pl.BlockSpec((tm, MIX_COLS), lambda i: (i, 0)),
                pl.BlockSpec((HB, MIX_COLS), lambda i: (jnp.maximum(i * per - 1, 0), 0)),
                pl.BlockSpec((HA, MIX_COLS), lambda i: (jnp.minimum((i + 1) * per, last_blk), 0)),
                pl.BlockSpec((tm, W), lambda i: (i, 0)),
                pl.BlockSpec((HA, W), lambda i: (jnp.minimum((i + 1) * per, last_blk), 0)),
                pl.BlockSpec((tm, 4 * W), lambda i: (i, 0)),
                pl.BlockSpec((HA, 4 * W), lambda i: (jnp.minimum((i + 1) * per, last_blk), 0))]
    in_specs += _small_specs(plist)
    in_specs += [pl.BlockSpec(memory_space=pl.ANY)]
    z2 = lambda i: (0, 0)
    z3 = lambda i: (0, 0, 0)
    out_specs = [pl.BlockSpec((tm, MIX_COLS), lambda i: (i, 0)),
                 pl.BlockSpec((32, W), z2), pl.BlockSpec((8, W), z2), pl.BlockSpec((8, W), z2),
                 pl.BlockSpec((4, GW, GW), z3), pl.BlockSpec((4, GW, GW), z3), pl.BlockSpec((4, GW, GW), z3)]
    out_shape = [jax.ShapeDtypeStruct((T, COLS_IN), BF16),
                 jax.ShapeDtypeStruct((32, W), F32), jax.ShapeDtypeStruct((8, W), F32),
                 jax.ShapeDtypeStruct((8, W), F32),
                 jax.ShapeDtypeStruct((4, GW, GW), F32), jax.ShapeDtypeStruct((4, GW, GW), F32),
                 jax.ShapeDtypeStruct((4, GW, GW), F32)]
    n_in = 7 + len(plist)
    return pl.pallas_call(
        body, name="b_mixers", grid=(nt,), in_specs=in_specs, out_specs=out_specs, out_shape=out_shape,
        scratch_shapes=[pltpu.VMEM((HB + ne, W), F32), pltpu.VMEM((ne, W), F32), pltpu.VMEM((ne, W), F32),
                        pltpu.VMEM((ne + SUBLANES, W), F32)],
        input_output_aliases={n_in: 0},
        compiler_params=_cp("arbitrary"),
    )(proj, proj, proj, ca_saved, ca_saved, dy, dy, *plist, dproj)


def _norm_first(x, g, tm=512):
    def body(x_ref, g_ref, o_ref):
        o_ref[...] = _rms(x_ref[...], g_ref[...]).astype(BF16)

    return pl.pallas_call(
        body, name="f_norm0", grid=(T // tm,),
        in_specs=[pl.BlockSpec((tm, D), lambda i: (i, 0)), pl.BlockSpec((1, D), lambda i: (0, 0))],
        out_specs=pl.BlockSpec((tm, D), lambda i: (i, 0)),
        out_shape=jax.ShapeDtypeStruct((T, D), BF16), compiler_params=_cp("arbitrary"),
    )(x, g)


def _loss_head(x, target, g, tm=256):
    def body(x_ref, t_ref, g_ref, dx_ref, dg_ref, loss_ref):
        i = pl.program_id(0)
        x = x_ref[...]
        r = lax.rsqrt(jnp.mean(x * x, axis=-1, keepdims=True) + EPS)
        xh = x * r
        gv = g_ref[...]
        e = xh * gv - t_ref[...]
        dyv = e * (1.0 / D)
        part = jnp.sum(_rowsum(e * e), axis=-1, keepdims=True) * (0.5 / D)
        u = dyv * gv
        dx_ref[...] = r * (u - xh * jnp.mean(u * xh, axis=-1, keepdims=True))
        dgp = _rowsum(dyv * xh)

        @pl.when(i == 0)
        def _():
            dg_ref[...] = dgp
            loss_ref[...] = jnp.broadcast_to(part, (1, GW))

        @pl.when(i > 0)
        def _():
            dg_ref[...] += dgp
            loss_ref[...] += jnp.broadcast_to(part, (1, GW))

    return pl.pallas_call(
        body, name="loss_head", grid=(T // tm,),
        in_specs=[pl.BlockSpec((tm, D), lambda i: (i, 0)), pl.BlockSpec((tm, D), lambda i: (i, 0)),
                  pl.BlockSpec((1, D), lambda i: (0, 0))],
        out_specs=[pl.BlockSpec((tm, D), lambda i: (i, 0)), pl.BlockSpec((1, D), lambda i: (0, 0)),
                   pl.BlockSpec((1, GW), lambda i: (0, 0))],
        out_shape=[jax.ShapeDtypeStruct((T, D), F32), jax.ShapeDtypeStruct((1, D), F32),
                   jax.ShapeDtypeStruct((1, GW), F32)],
        compiler_params=_cp("arbitrary"),
    )(x, target, g)


def _out_bwd(dx2b, merged, w_out8, tm=512):
    nt = T // tm

    def body(dx_ref, mg_ref, w_ref, dm_ref, dw_ref, acc):
        i = pl.program_id(0)
        dx = dx_ref[...]
        dm_ref[...] = lax.dot_general(dx, w_ref[...], (((1,), (1,)), ((), ())),
                                      preferred_element_type=F32).astype(BF16)
        part = lax.dot_general(mg_ref[...], dx, (((0,), (0,)), ((), ())), preferred_element_type=F32)

        @pl.when(i == 0)
        def _():
            acc[...] = part

        @pl.when(i > 0)
        def _():
            acc[...] += part

        @pl.when(i == nt - 1)
        def _():
            for j in range(NDEV):
                dw_ref[j] = acc[j * GW:(j + 1) * GW, :].astype(BF16)

    tile = pl.BlockSpec((tm, D), lambda i: (i, 0))
    return pl.pallas_call(
        body, name="b_out", grid=(nt,),
        in_specs=[tile, tile, pl.BlockSpec((D, D), lambda i: (0, 0))],
        out_specs=[tile, pl.BlockSpec((NDEV, GW, D), lambda i: (0, 0, 0))],
        out_shape=[jax.ShapeDtypeStruct((T, D), BF16), jax.ShapeDtypeStruct((NDEV, GW, D), BF16)],
        scratch_shapes=[pltpu.VMEM((D, D), F32)],
        compiler_params=_cp("arbitrary"),
    )(dx2b, merged, w_out8.reshape(D, D))


def _ple_bwd(dx4, sv, w_pleg8, g_ple, tm=256):
    nt = T // tm
    ple_dim = sv["p"].shape[1]

    def body(dx_ref, gl_ref, pe_ref, x_ref, h_ref, p_ref, g_ref, wg_ref,
             dx3_ref, dx3b_ref, dg_ref, dwg_ref, dwp_ref, acc_g, acc_p):
        i = pl.program_id(0)
        d = dx_ref[...]
        s = _sig(gl_ref[...].astype(F32))
        dpe = (d * s).astype(BF16)
        dgl = (d * pe_ref[...].astype(F32) * s * (1.0 - s)).astype(BF16)
        dh = lax.dot_general(dgl, wg_ref[...], (((1,), (1,)), ((), ())), preferred_element_type=F32)
        dx, dgp = _rms_bwd(dh, x_ref[...], g_ref[...], d)
        dx3_ref[...] = dx
        dx3b_ref[...] = dx.astype(BF16)
        part_g = lax.dot_general(h_ref[...], dgl, (((0,), (0,)), ((), ())), preferred_element_type=F32)
        part_p = lax.dot_general(p_ref[...], dpe, (((0,), (0,)), ((), ())), preferred_element_type=F32)

        @pl.when(i == 0)
        def _():
            dg_ref[...] = dgp
            acc_g[...] = part_g
            acc_p[...] = part_p

        @pl.when(i > 0)
        def _():
            dg_ref[...] += dgp
            acc_g[...] += part_g
            acc_p[...] += part_p

        @pl.when(i == nt - 1)
        def _():
            for j in range(NDEV):
                dwg_ref[j] = acc_g[j * GW:(j + 1) * GW, :].astype(BF16)
                dwp_ref[j] = acc_p[:, j * GW:(j + 1) * GW].astype(BF16)

    tile = lambda w: pl.BlockSpec((tm, w), lambda i: (i, 0))
    const = lambda shp: pl.BlockSpec(shp, lambda i: (0,) * len(shp))
    return pl.pallas_call(
        body, name="b_ple", grid=(nt,),
        in_specs=[tile(D), tile(D), tile(D), tile(D), tile(D), tile(ple_dim), const((1, D)), const((D, D))],
        out_specs=[tile(D), tile(D), const((1, D)), const((NDEV, GW, D)), const((NDEV, ple_dim, GW))],
        out_shape=[jax.ShapeDtypeStruct((T, D), F32), jax.ShapeDtypeStruct((T, D), BF16),
                   jax.ShapeDtypeStruct((1, D), F32), jax.ShapeDtypeStruct((NDEV, GW, D), BF16),
                   jax.ShapeDtypeStruct((NDEV, ple_dim, GW), BF16)],
        scratch_shapes=[pltpu.VMEM((D, D), F32), pltpu.VMEM((ple_dim, D), F32)],
        compiler_params=_cp("arbitrary"),
    )(dx4, sv["gl"], sv["pe"], sv["x3"], sv["h3"], sv["p"], g_ple, w_pleg8.reshape(D, D))


def _layer_fwd(x, h1, p_bf, gw, sp, g_next):
    proj, = _mm(h1, gw["w_in"], mode="out", name="f_proj", outs=[BF16], tm=T)
    y, ca = _mixer_fwd(proj, sp)
    z, merged = _merge_fwd(y, proj, gw["w_branch"])
    x2, h2 = _mm(merged, gw["w_out"].reshape(1, D, D), mode="acc", name="f_out", outs=[F32, BF16], tm=T // 2,
                 tiles=[x], params=[sp["g_mlp"]], epi=_epi_res_norm)
    up, = _mm(h2, gw["w_up"], mode="out", name="f_up", outs=[BF16], tm=T)
    x3, h3 = _mm(up, gw["w_down"].reshape(1, 4 * D, D), mode="acc", name="f_down", outs=[F32, BF16], tm=T // 4,
                 tiles=[x2], params=[sp["g_ple"]], epi=_epi_res_norm, a_pre=_relu2_bf16)
    x4, gl, hn, pe = _mm(h3, gw["w_pleg"].reshape(1, D, D), mode="acc", name="f_gate", tm=T // 2,
                         outs=[F32, BF16, BF16, BF16], tiles=[x3, p_bf], params=[g_next, gw["w_ple"]], epi=_epi_ple)
    saved = dict(x=x, h1=h1, proj=proj, y=y, ca=ca, z=z, merged=merged, x2=x2, h2=h2, up=up, x3=x3, h3=h3,
                 pe=pe, gl=gl, p=p_bf)
    return x4, hn, saved


def _layer_bwd(dx4, sv, gw, sp, submit, early_group=False):
    dw = {}
    dx3, dx3b, dg_ple, dw["w_pleg"], dw["w_ple"] = _ple_bwd(dx4, sv, gw["w_pleg"], sp["g_ple"])
    dup, = _mm(dx3b, gw["w_down"], mode="out", trans_b=True, name="b_dact", outs=[BF16], tm=T,
               tiles=[sv["up"]], epi=_epi_dup)
    dw["w_down"] = _mm_tn(sv["up"], dx3b, nj=NDEV, split="row", name="b_dw_down", a_pre=_relu2_bf16)
    dw["w_up"] = _mm_tn(sv["h2"], dup, nj=NDEV, split="col", name="b_dw_up")
    if early_group:
        dw["w_up"], dup = lax.optimization_barrier((dw["w_up"], dup))
        dup = submit(dw, ("w_up", "w_down", "w_ple", "w_pleg"), dup)
    dx2, dx2b, dg_mlp = _mm(dup, gw["w_up"], mode="full", trans_b=True, name="b_dh2", tm=T // 4, stream_first=True,
                            outs=[F32, BF16], tiles=[sv["x2"], dx3], params=[sp["g_mlp"]], epi=_epi_rms_bwd, reds=[D])
    dm, dw["w_out"] = _out_bwd(dx2b, sv["merged"], gw["w_out"])
    dproj, dy, dw["w_branch"] = _merge_bwd(dm, sv["z"], sv["proj"], sv["y"], gw["w_branch"])
    dy = submit(dw, ("w_branch", "w_out") if early_group else BIG[1:], dy)
    dproj, dcw, dsc, vec, dpw, dws, dbs = _mixer_bwd(sv["proj"], sv["ca"], dy, dproj, sp)
    dw["w_in"] = _mm_tn(sv["h1"], dproj, nj=NDEV, split="col", name="b_dw_in")
    dw["w_in"], dproj = lax.optimization_barrier((dw["w_in"], dproj))
    dproj = submit(dw, BIG[:1], dproj)
    dx, dg_mix = _mm(dproj, gw["w_in"], mode="full", trans_b=True, name="b_dh1", outs=[F32], tm=T // 8,
                     stream_first=True,
                     tiles=[sv["x"], dx2], params=[sp["g_mix"]], epi=_epi_rms_bwd, reds=[D])
    small = dict(norm_mix=dg_mix[0], conf_dw=dcw[:CONF_K], conf_dw_b=vec[0], conf_ln_g=vec[1], conf_ln_b=vec[2],
                 pool_w=dpw, pool_scale=vec[3], sc_conv=dsc[:SC_K], gmlp_ln_g=vec[4], gmlp_ln_b=vec[5],
                 gmlp_ws=dws, gmlp_bs=dbs[:, :, 0], norm_mlp=dg_mlp[0], norm_ple=dg_ple[0])
    return dx, small


ANY = pl.BlockSpec(memory_space=pl.ANY)


def _mesh_pos():
    return lax.axis_index("x"), lax.axis_index("y"), lax.axis_index("c")


def _other_chips(x, y):
    return [(1 - x, y), (x, 1 - y), (1 - x, 1 - y)]


def _launch_comm(body, peers_of, operands, out_shapes, sems, name, seq_id):
    n_in, n_out = len(operands), len(out_shapes)
    if seq_id is None:
        return pl.pallas_call(body, name=name, in_specs=[ANY] * n_in, out_specs=[ANY] * n_out,
                              out_shape=out_shapes, scratch_shapes=sems)(*operands)

    def seq_body(*refs):
        peers = peers_of(*_mesh_pos())
        barrier = pltpu.get_barrier_semaphore()
        for peer in peers:
            pl.semaphore_signal(barrier, inc=1, device_id=peer, device_id_type=MESH)
        pl.semaphore_wait(barrier, len(peers))
        body(*refs)

    return pl.kernel(seq_body, name=name, out_type=out_shapes,
                     mesh=plsc.ScalarSubcoreMesh(axis_name="seq", num_cores=1), scratch_types=sems,
                     compiler_params=pltpu.CompilerParams(collective_id=seq_id))(*operands)


def _all_gather(shards, name, seq_id=None):
    n = len(shards)

    def body(*refs):
        s_refs, o_refs = refs[:n], refs[n:2 * n]
        send_sems, recv_sems, local_sems = refs[2 * n:]
        x, y, c = _mesh_pos()
        me = 4 * x + 2 * y + c
        here = (x, y, c)
        sibling = (x, y, 1 - c)
        chips = _other_chips(x, y)

        def slot(px, py, pc):
            return 4 * px + 2 * py + pc

        def copy(t, k, slot_idx, to, src=None):
            dst = o_refs[t].at[slot_idx]
            return pltpu.make_async_remote_copy(
                src_ref=dst if src is None else src, dst_ref=dst,
                send_sem=send_sems.at[t * 7 + k], recv_sem=recv_sems.at[t * 7 + k],
                device_id=to, device_id_type=MESH)

        mine = [pltpu.make_async_copy(s_refs[t], o_refs[t].at[me], local_sems.at[t]) for t in range(n)]
        for cp in mine:
            cp.start()
        first = []
        for t in range(n):
            for j, chip in enumerate(chips):
                first.append(copy(t, 1 + j, me, (*chip, c), src=s_refs[t]))
        for t in range(n):
            first.append(copy(t, 0, me, sibling, src=s_refs[t]))
        for cp in first:
            cp.start()
        passed = []
        for t in range(n):
            for j, chip in enumerate(chips):
                copy(t, 1 + j, slot(*chip, c), here).wait_recv()
                fwd = copy(t, 4 + j, slot(*chip, c), sibling)
                fwd.start()
                passed.append(fwd)
        for t in range(n):
            copy(t, 0, slot(x, y, 1 - c), here).wait_recv()
            for j, chip in enumerate(chips):
                copy(t, 4 + j, slot(*chip, 1 - c), here).wait_recv()
        for cp in first + passed:
            cp.wait_send()
        for cp in mine:
            cp.wait()

    def peers_of(x, y, c):
        return [(x, y, 1 - c)] + [(*chip, c) for chip in _other_chips(x, y)]

    return _launch_comm(
        body, peers_of, shards, [jax.ShapeDtypeStruct((NDEV,) + s.shape, s.dtype) for s in shards],
        [pltpu.SemaphoreType.DMA((7 * n,)), pltpu.SemaphoreType.DMA((7 * n,)), pltpu.SemaphoreType.DMA((n,))],
        name, seq_id)


def _rs_exchange(p4s, qs, name, seq_id=None):
    n_p, n_q = len(p4s), len(qs)

    def body(*refs):
        p_refs, q_refs = refs[:n_p], refs[n_p:n_p + n_q]
        rb_refs, rc_refs = refs[n_p + n_q:2 * n_p + n_q], refs[2 * n_p + n_q:2 * (n_p + n_q)]
        pair_send, pair_recv, chip_send, chip_recv, local_sems = refs[2 * (n_p + n_q):]
        x, y, c = _mesh_pos()
        a_idx = 2 * x + y
        chips = _other_chips(x, y)
        mine = [pltpu.make_async_copy(q_refs[t].at[a_idx], rc_refs[t].at[a_idx], local_sems.at[t])
                for t in range(n_q)]
        sends = []
        for t in range(n_q):
            for j, chip in enumerate(chips):
                sends.append(pltpu.make_async_remote_copy(
                    src_ref=q_refs[t].at[2 * chip[0] + chip[1]], dst_ref=rc_refs[t].at[a_idx],
                    send_sem=chip_send.at[t * 3 + j], recv_sem=chip_recv.at[t * 3 + j],
                    device_id=(*chip, c), device_id_type=MESH))
        pairs = [pltpu.make_async_remote_copy(
            src_ref=p_refs[t].at[:, 1 - c], dst_ref=rb_refs[t], send_sem=pair_send.at[t], recv_sem=pair_recv.at[t],
            device_id=(x, y, 1 - c), device_id_type=MESH) for t in range(n_p)]
        for cp in sends + mine + pairs:
            cp.start()
        for cp in pairs:
            cp.wait()
        for t in range(n_q):
            for j, chip in enumerate(chips):
                landed = rc_refs[t].at[2 * chip[0] + chip[1]]
                pltpu.make_async_remote_copy(
                    src_ref=landed, dst_ref=landed, send_sem=chip_send.at[t * 3 + j],
                    recv_sem=chip_recv.at[t * 3 + j], device_id=(x, y, c), device_id_type=MESH).wait_recv()
        for cp in sends:
            cp.wait_send()
        for cp in mine:
            cp.wait()

    def peers_of(x, y, c):
        peers = [(x, y, 1 - c)] if n_p else []
        return peers + ([(*chip, c) for chip in _other_chips(x, y)] if n_q else [])

    out_shapes = [jax.ShapeDtypeStruct((NCHIP,) + p.shape[2:], p.dtype) for p in p4s]
    out_shapes += [jax.ShapeDtypeStruct(q.shape, q.dtype) for q in qs]
    sems = [pltpu.SemaphoreType.DMA((max(n_p, 1),)), pltpu.SemaphoreType.DMA((max(n_p, 1),)),
            pltpu.SemaphoreType.DMA((max(3 * n_q, 1),)), pltpu.SemaphoreType.DMA((max(3 * n_q, 1),)),
            pltpu.SemaphoreType.DMA((max(n_q, 1),))]
    got = _launch_comm(body, peers_of, list(p4s) + list(qs), out_shapes, sems, name, seq_id)
    return got[:n_p], got[n_p:]


def _pair_sum(p4s, rbs, c_idx, name, nst=1):
    n = len(p4s)
    trs = [p.shape[2] // nst for p in p4s]

    def body(c_ref, *refs):
        del c_ref
        p_refs, r_refs, o_refs = refs[:n], refs[n:2 * n], refs[2 * n:]
        for p_ref, r_ref, o_ref in zip(p_refs, r_refs, o_refs):
            o_ref[...] = (p_ref[...].astype(F32) + r_ref[...].astype(F32)).astype(o_ref.dtype)

    in_specs = [pl.BlockSpec((None, None, tr, p.shape[3]), lambda b, i, c_ref: (b, c_ref[0], i, 0))
                for p, tr in zip(p4s, trs)]
    in_specs += [pl.BlockSpec((None, tr, p.shape[3]), lambda b, i, c_ref: (b, i, 0)) for p, tr in zip(p4s, trs)]
    out_specs = [pl.BlockSpec((None, tr, p.shape[3]), lambda b, i, c_ref: (b, i, 0)) for p, tr in zip(p4s, trs)]
    return pl.pallas_call(
        body, name=name,
        grid_spec=pltpu.PrefetchScalarGridSpec(num_scalar_prefetch=1, grid=(NCHIP, nst), in_specs=in_specs,
                                               out_specs=out_specs),
        out_shape=[jax.ShapeDtypeStruct((NCHIP,) + p.shape[2:], p.dtype) for p in p4s],
        compiler_params=_cp("arbitrary", "arbitrary"),
    )(c_idx, *p4s, *rbs)


class _GradientPipeline:
    def __init__(self, c_idx, results):
        self.c_idx, self.results, self.pending = c_idx, results, None

    def _sum_pending(self, chain):
        names, layer, p4s, rbs = self.pending
        qs = _pair_sum(p4s, rbs, self.c_idx, name="rs_pairsum_%d" % len(names))
        return lax.optimization_barrier((chain, qs))

    def submit(self, dw, names, layer, chain):
        qs, tag, seq_id = [], "pair", 3
        if self.pending is not None:
            chain, qs = self._sum_pending(chain)
            tag, seq_id = "pair_chip", 4
        p4s = [dw[n].reshape((NCHIP, 2) + BIG_SHARD[n]) for n in names]
        rbs, rcs = _rs_exchange(p4s, qs, name="rs_%s_%d" % (tag, len(names)), seq_id=seq_id)
        self._record(rcs)
        self.pending = (names, layer, p4s, rbs)
        return chain

    def finish(self, chain):
        chain, qs = self._sum_pending(chain)
        self._record(_rs_exchange([], qs, name="rs_chip_last", seq_id=5)[1])
        self.pending = None
        return chain

    def _record(self, rcs):
        if rcs:
            names, layer = self.pending[:2]
            for n, rc in zip(names, rcs):
                self.results[n][layer] = rc


def _adamw(w, g, m, v):
    m = ADAM_B1 * m + (1.0 - ADAM_B1) * g
    v = ADAM_B2 * v + (1.0 - ADAM_B2) * (g * g)
    m_hat = m / (1.0 - ADAM_B1 ** ADAM_STEP)
    v_hat = v / (1.0 - ADAM_B2 ** ADAM_STEP)
    delta = -ADAM_LR * (m_hat / (jnp.sqrt(v_hat) + ADAM_EPS) + ADAM_WD * w)
    return delta, m, v


def _adam_sharded(rcs, w, m, v, tr, name, first_layer, partial=None):
    _, r, c = w.shape
    nst = r // tr
    n_l = len(rcs)

    def body(*refs):
        rc_refs = refs[:n_l]
        w_ref, m_ref, v_ref = refs[n_l:n_l + 3]
        g_out, d_out, m_out, v_out = refs[-4:]
        layer = pl.program_id(0)
        for k, rc in enumerate(rc_refs):
            @pl.when(layer == k)
            def _():
                g = rc[0].astype(F32) + rc[1].astype(F32) + rc[2].astype(F32) + rc[3].astype(F32)
                delta, m_new, v_new = _adamw(w_ref[...], g, m_ref[...], v_ref[...])
                g_out[...] = g
                d_out[...] = delta
                m_out[...] = m_new
                v_out[...] = v_new

    rc_specs = [pl.BlockSpec((NCHIP, tr, c), lambda l, i, k=k: (0, jnp.where(l == k, i, 0), 0)) for k in range(n_l)]
    wspec = pl.BlockSpec((None, tr, c), lambda l, i: (first_layer + l, i, 0))
    carried = [] if partial is None else list(partial)
    return pl.pallas_call(
        body, name=name, grid=(n_l, nst),
        in_specs=rc_specs + [wspec] * 3 + [pl.BlockSpec(memory_space=pl.ANY)] * len(carried),
        out_specs=[wspec] * 4, out_shape=[jax.ShapeDtypeStruct(w.shape, F32)] * 4,
        input_output_aliases={n_l + 3 + k: k for k in range(len(carried))},
        compiler_params=_cp("arbitrary", "arbitrary"),
    )(*rcs, w, m, v, *carried)


def _adam_packed(g, w, m, v, direct):
    n_d = len(direct)

    def pieces(shape):
        width = shape[-1]
        count = 1
        for s in shape[:-1]:
            count *= s
        per_row = D // width
        out = []
        for k in range(count):
            idx = (k,) if len(shape) == 2 else (k // shape[1], k % shape[1])
            out.append((idx, k // per_row, (k % per_row) * width, width))
        return out

    def body(g_ref, w_ref, m_ref, v_ref, d_out, m_out, v_out, *outs):
        delta, m_new, v_new = _adamw(w_ref[...], g_ref[...], m_ref[...], v_ref[...])
        d_out[...] = delta
        m_out[...] = m_new
        v_out[...] = v_new
        for a, (_, row0, shape) in enumerate(direct):
            for src, dst in zip((g_ref, d_out, m_out, v_out), outs[4 * a:4 * a + 4]):
                for idx, row, lane0, width in pieces(shape):
                    piece = src[pl.ds(row0 + row, 1), lane0:lane0 + width]
                    if len(idx) == 1:
                        dst[pl.ds(idx[0], 1), :] = piece
                    else:
                        dst[idx[0], pl.ds(idx[1], 1), :] = piece

    out_shape = [jax.ShapeDtypeStruct(g.shape, F32)] * 3
    for _, _, shape in direct:
        out_shape += [jax.ShapeDtypeStruct(shape, F32)] * 4
    res = pl.pallas_call(body, name="adam_small", out_shape=out_shape,
                         compiler_params=pltpu.CompilerParams(vmem_limit_bytes=VMEM_LIMIT_BYTES))(g, w, m, v)
    return res[:3], {name: res[3 + 4 * a:7 + 4 * a] for a, (name, _, _) in enumerate(direct)}


def _sum4(rc):
    def body(rc_ref, o_ref):
        o_ref[...] = rc_ref[0] + rc_ref[1] + rc_ref[2] + rc_ref[3]

    return pl.pallas_call(
        body, name="small_sum", out_shape=jax.ShapeDtypeStruct(rc.shape[1:], F32),
    )(rc)


BIG = ("w_in", "w_branch", "w_out", "w_up", "w_down", "w_ple", "w_pleg")
BIG_SHARD = {"w_in": (D, D), "w_branch": (4 * W, GW), "w_out": (GW, D), "w_up": (D, W), "w_down": (W, D),
             "w_ple": (256, GW), "w_pleg": (GW, D)}
ADAM_ROWS = {"w_in": 256, "w_branch": 512, "w_out": 128, "w_up": 256, "w_down": 256, "w_ple": 256, "w_pleg": 128}
SMALL = (("norm_mix", (DEPTH, D)), ("conf_dw", (DEPTH, CONF_K, W)), ("conf_dw_b", (DEPTH, W)),
         ("conf_ln_g", (DEPTH, W)), ("conf_ln_b", (DEPTH, W)), ("pool_w", (DEPTH, 4, GW, GW)),
         ("pool_scale", (DEPTH, W)), ("sc_conv", (DEPTH, SC_K, W)), ("gmlp_ln_g", (DEPTH, W)),
         ("gmlp_ln_b", (DEPTH, W)), ("gmlp_ws", (DEPTH, 4, GW, GW)), ("gmlp_bs", (DEPTH, 4, GW)),
         ("norm_mlp", (DEPTH, D)), ("norm_ple", (DEPTH, D)), ("norm_final", (D,)))
CHANNEL_SHARDED = ("conf_dw", "sc_conv")
SMALL_ROWS = 80


def _pack(arrs, rows):
    flat = jnp.concatenate([a.reshape(-1) for a in arrs])
    return jnp.pad(flat, (0, rows * D - flat.shape[0])).reshape(rows, D)


def _unpack(packed, shapes):
    flat = packed.reshape(-1)
    out, off = [], 0
    for shp in shapes:
        size = 1
        for s in shp:
            size *= s
        out.append(flat[off:off + size].reshape(shp))
        off += size
    return out


def kernel(x, p, norm_mix, w_in, conf_dw, conf_dw_b, conf_ln_g, conf_ln_b, pool_w, pool_scale, sc_conv, gmlp_ln_g, gmlp_ln_b, gmlp_ws, gmlp_bs, w_branch, w_out, norm_mlp, w_up, w_down, norm_ple, w_ple, w_ple_gate, norm_final, loss_target, m_norm_mix, m_w_in, m_conf_dw, m_conf_dw_b, m_conf_ln_g, m_conf_ln_b, m_pool_w, m_pool_scale, m_sc_conv, m_gmlp_ln_g, m_gmlp_ln_b, m_gmlp_ws, m_gmlp_bs, m_w_branch, m_w_out, m_norm_mlp, m_w_up, m_w_down, m_norm_ple, m_w_ple, m_w_ple_gate, m_norm_final, v_norm_mix, v_w_in, v_conf_dw, v_conf_dw_b, v_conf_ln_g, v_conf_ln_b, v_pool_w, v_pool_scale, v_sc_conv, v_gmlp_ln_g, v_gmlp_ln_b, v_gmlp_ws, v_gmlp_bs, v_w_branch, v_w_out, v_norm_mlp, v_w_up, v_w_down, v_norm_ple, v_w_ple, v_w_ple_gate, v_norm_final):
    weights = dict(norm_mix=norm_mix, w_in=w_in, conf_dw=conf_dw, conf_dw_b=conf_dw_b, conf_ln_g=conf_ln_g,
                   conf_ln_b=conf_ln_b, pool_w=pool_w, pool_scale=pool_scale, sc_conv=sc_conv, gmlp_ln_g=gmlp_ln_g,
                   gmlp_ln_b=gmlp_ln_b, gmlp_ws=gmlp_ws, gmlp_bs=gmlp_bs, w_branch=w_branch, w_out=w_out,
                   norm_mlp=norm_mlp, w_up=w_up, w_down=w_down, norm_ple=norm_ple, w_ple=w_ple, w_pleg=w_ple_gate,
                   norm_final=norm_final)
    mom1 = dict(norm_mix=m_norm_mix, w_in=m_w_in, conf_dw=m_conf_dw, conf_dw_b=m_conf_dw_b, conf_ln_g=m_conf_ln_g,
                conf_ln_b=m_conf_ln_b, pool_w=m_pool_w, pool_scale=m_pool_scale, sc_conv=m_sc_conv,
                gmlp_ln_g=m_gmlp_ln_g, gmlp_ln_b=m_gmlp_ln_b, gmlp_ws=m_gmlp_ws, gmlp_bs=m_gmlp_bs,
                w_branch=m_w_branch, w_out=m_w_out, norm_mlp=m_norm_mlp, w_up=m_w_up, w_down=m_w_down,
                norm_ple=m_norm_ple, w_ple=m_w_ple, w_pleg=m_w_ple_gate, norm_final=m_norm_final)
    mom2 = dict(norm_mix=v_norm_mix, w_in=v_w_in, conf_dw=v_conf_dw, conf_dw_b=v_conf_dw_b, conf_ln_g=v_conf_ln_g,
                conf_ln_b=v_conf_ln_b, pool_w=v_pool_w, pool_scale=v_pool_scale, sc_conv=v_sc_conv,
                gmlp_ln_g=v_gmlp_ln_g, gmlp_ln_b=v_gmlp_ln_b, gmlp_ws=v_gmlp_ws, gmlp_bs=v_gmlp_bs,
                w_branch=v_w_branch, w_out=v_w_out, norm_mlp=v_norm_mlp, w_up=v_w_up, w_down=v_w_down,
                norm_ple=v_norm_ple, w_ple=v_w_ple, w_pleg=v_w_ple_gate, norm_final=v_norm_final)

    xi, yi, ci = _mesh_pos()
    me = 4 * xi + 2 * yi + ci
    c_idx = jnp.reshape(ci, (1,)).astype(jnp.int32)

    gathered, conf_full, sc_full = [], [], []
    for l in range(DEPTH):
        shard = lambda n: weights[n][l].astype(BF16).reshape(BIG_SHARD[n])
        w_in_g, conf_g, sc_g = _all_gather([shard("w_in"), conf_dw[l], sc_conv[l]], name="ag_first", seq_id=1)
        if l + 1 < DEPTH:
            rest = _all_gather([shard(n) for n in BIG[1:]], name="ag_rest", seq_id=2)
        else:
            rest = (list(_all_gather([shard(n) for n in BIG[1:4]], name="ag_rest_a", seq_id=2))
                    + list(_all_gather([shard(n) for n in BIG[4:]], name="ag_rest_b", seq_id=2)))
        gw = dict(zip(BIG[1:], rest), w_in=w_in_g)
        gw["w_branch"] = gw["w_branch"].reshape(NDEV, 4, W, GW)
        gathered.append(gw)
        conf_full.append(conf_g)
        sc_full.append(sc_g)

    def small_params(l):
        return dict(cw=conf_full[l], cb=conf_dw_b[l][None], lg=conf_ln_g[l][None], lb=conf_ln_b[l][None],
                    pw=pool_w[l], ps=pool_scale[l][None], sc=sc_full[l], gg=gmlp_ln_g[l][None],
                    gb=gmlp_ln_b[l][None], ws=gmlp_ws[l], bst=gmlp_bs[l].T, g_mix=norm_mix[l][None],
                    g_mlp=norm_mlp[l][None], g_ple=norm_ple[l][None])

    xc = x.reshape(T, D)
    small_names = [n for n, _ in SMALL]

    def in_gradient_layout(n, shard, shape):
        if n not in CHANNEL_SHARDED:
            return shard
        return lax.dynamic_update_slice(jnp.zeros(shape, F32), shard, (0, 0, me * (W // NDEV)))

    small_state = [_pack([in_gradient_layout(n, src[n], shape) for n, shape in SMALL], NDEV * SMALL_ROWS)
                   for src in (weights, mom1, mom2)]
    xc, small_state = lax.optimization_barrier((xc, small_state))
    p_bf = p.reshape(DEPTH, T, 256).astype(BF16)
    h = _norm_first(xc, norm_mix[0][None])
    saved = []
    for l in range(DEPTH):
        g_next = norm_mix[l + 1][None] if l + 1 < DEPTH else norm_final[None]
        h, conf_g, sc_g = lax.optimization_barrier((h, conf_full[l], sc_full[l]))
        conf_full[l] = conf_g.transpose(1, 0, 2).reshape(CONF_K, W)
        sc_full[l] = sc_g.transpose(1, 0, 2).reshape(SC_K, W)
        xc, h, sv = _layer_fwd(xc, h, p_bf[l], gathered[l], small_params(l), g_next)
        saved.append(sv)

    dxc, dg_final, loss_part = _loss_head(xc, loss_target.reshape(T, D), norm_final[None])
    loss = lax.psum(loss_part[0, 0], ("x", "y", "c"))
    small_grads = [None] * DEPTH
    rcs = {n: [None] * DEPTH for n in BIG}
    pipeline = _GradientPipeline(c_idx, rcs)
    for l in reversed(range(DEPTH)):
        dxc, small_grads[l] = _layer_bwd(dxc, saved[l], gathered[l], small_params(l),
                                         lambda dw, names, value, l=l: pipeline.submit(dw, names, l, value),
                                         early_group=(l == 0))

    def adam_sharded(first_layer, n_layers, partial, tag, names=BIG):
        outs = {}
        for n in names:
            shp = (DEPTH,) + BIG_SHARD[n]
            outs[n] = _adam_sharded(rcs[n][first_layer:first_layer + n_layers], weights[n].reshape(shp),
                                    mom1[n].reshape(shp), mom2[n].reshape(shp), ADAM_ROWS[n],
                                    "adam_%s_%s" % (n, tag), first_layer, None if partial is None else partial[n])
        return outs

    stacked = {n: jnp.stack([small_grads[l][n] for l in range(DEPTH)]) for n, _ in SMALL if n != "norm_final"}
    stacked["norm_final"] = dg_final[0]
    packed = _pack([stacked[n] for n, _ in SMALL], NDEV * SMALL_ROWS).reshape(NCHIP, 2, SMALL_ROWS, D)
    (pair_small,), _ = _rs_exchange([packed], [], name="rs_pair_small")
    q_small = _pair_sum([packed], [pair_small], c_idx, name="rs_pairsum_small")
    dxc, upper, q_small = lax.optimization_barrier((dxc, {n: rcs[n][1:] for n in BIG}, q_small))
    _, (chips_small,) = _rs_exchange([], q_small, name="rs_chip_small", seq_id=6)
    dxc, upper = pipeline.finish((dxc, upper))
    for n in BIG:
        rcs[n][1:] = upper[n]
    partial = adam_sharded(1, DEPTH - 1, None, "upper")
    last = adam_sharded(0, 1, partial, "last", names=BIG[:1])
    partial = {n: partial[n] for n in BIG[1:]}
    last, partial, chips_small = lax.optimization_barrier((last, partial, chips_small))
    reduced_slot = _sum4(chips_small)
    reduced = _all_gather([reduced_slot], name="ag_small", seq_id=7)[0]
    small_full = dict(zip([n for n, _ in SMALL], _unpack(reduced, [s for _, s in SMALL])))
    grads, deltas, new_m, new_v = {}, {}, {}, {}
    direct, row = [], 0
    for n, shape in SMALL:
        if len(shape) == 1 or shape[-2] == DEPTH:
            direct.append((n, row, (1,) * (2 - len(shape)) + tuple(shape)))
        size = 1
        for s in shape:
            size *= s
        row += size // D
    (d_p, m_p, v_p), own_shape = _adam_packed(reduced.reshape(NDEV * SMALL_ROWS, D), *small_state, direct)
    small_shapes = [s for _, s in SMALL]

    def own_channels(n, full):
        return lax.dynamic_slice_in_dim(full, me * (W // NDEV), W // NDEV, axis=2) if n in CHANNEL_SHARDED else full

    for n, d_, m_, v_ in zip(small_names, _unpack(d_p, small_shapes), _unpack(m_p, small_shapes),
                             _unpack(v_p, small_shapes)):
        if n in own_shape:
            grads[n], deltas[n], new_m[n], new_v[n] = [a.reshape(weights[n].shape) for a in own_shape[n]]
        else:
            grads[n], deltas[n], new_m[n], new_v[n] = (own_channels(n, small_full[n]), own_channels(n, d_),
                                                       own_channels(n, m_), own_channels(n, v_))

    last.update(adam_sharded(0, 1, partial, "last", names=BIG[1:]))
    for n, (g_, d_, m_, v_) in last.items():
        full = weights[n].shape
        grads[n], deltas[n], new_m[n], new_v[n] = g_.reshape(full), d_.reshape(full), m_.reshape(full), v_.reshape(full)

    order = ("norm_mix", "w_in", "conf_dw", "conf_dw_b", "conf_ln_g", "conf_ln_b", "pool_w", "pool_scale", "sc_conv",
             "gmlp_ln_g", "gmlp_ln_b", "gmlp_ws", "gmlp_bs", "w_branch", "w_out", "norm_mlp", "w_up", "w_down",
             "norm_ple", "w_ple", "w_pleg", "norm_final")
    return (loss, dxc.reshape(1, T, D), *[grads[n] for n in order], *[deltas[n] for n in order],
            *[new_m[n] for n in order], *[new_v[n] for n in order])
```

```python
import functools

import jax
import jax.numpy as jnp
from jax import lax
from jax.experimental import pallas as pl
from jax.experimental.pallas import tpu as pltpu
from jax.experimental.pallas import tpu_sc as plsc

F32 = jnp.float32
BF16 = jnp.bfloat16

DEPTH = 4
T = 2048
D = 1024
W = 512
NDEV = 8
NCHIP = 4
EPS = 1e-6
CONF_K = 31
SC_K = 3
POOL_WINDOWS = (2, 4, 8, 16)
GW = 128
HB = 32
HA = 32
COLS_IN = 8192
MIX_COLS = 4096

ADAM_LR = 0.001
ADAM_B1 = 0.9
ADAM_B2 = 0.999
ADAM_EPS = 1e-08
ADAM_WD = 0.01
ADAM_STEP = 10

VMEM_LIMIT_BYTES = 56 * 1024 * 1024
MESH = pl.DeviceIdType.MESH


def _cp(*sem):
    return pltpu.CompilerParams(dimension_semantics=tuple(sem), vmem_limit_bytes=VMEM_LIMIT_BYTES)


def _sig(x):
    return jax.nn.sigmoid(x)


def _rms(x, g):
    r = lax.rsqrt(jnp.mean(x * x, axis=-1, keepdims=True) + EPS)
    return x * r * g


def _rms_bwd(dh, x, g, dres):
    r = lax.rsqrt(jnp.mean(x * x, axis=-1, keepdims=True) + EPS)
    xh = x * r
    u = dh * g
    dx = r * (u - xh * jnp.mean(u * xh, axis=-1, keepdims=True)) + dres
    dg = jnp.sum(dh * xh, axis=0, keepdims=True)
    return dx, dg


def _ln_stats(x):
    mu = jnp.mean(x, axis=-1, keepdims=True)
    xc = x - mu
    rstd = lax.rsqrt(jnp.mean(xc * xc, axis=-1, keepdims=True) + EPS)
    return xc * rstd, rstd


def _ln_bwd(dxh, xh, rstd):
    return rstd * (dxh - jnp.mean(dxh, axis=-1, keepdims=True) - xh * jnp.mean(dxh * xh, axis=-1, keepdims=True))


def _rowsum(x):
    return jnp.sum(x, axis=0, keepdims=True)


EPI_ROWS = 256


def _relu2_bf16(up):
    r = jnp.maximum(up.astype(F32), 0.0)
    return (r * r).astype(BF16)


def _mm(a, b3, *, mode, name, outs, trans_b=False, tm=512, tiles=(), params=(), epi=None, reds=(), a_pre=None,
        stream_first=False):
    t_, ka = a.shape
    nj, r, c = b3.shape
    kb, nb = (c, r) if trans_b else (r, c)
    nt = t_ // tm
    out_mode = mode == "out"
    full = mode == "full"
    assert trans_b or not full
    if out_mode:
        assert ka == kb and not reds
        grid = (nj, nt)
        a_map = lambda g0, g1: (g1, 0)
        b_map = lambda g0, g1: (g0, 0, 0)
        t_map = lambda g0, g1: (g1, g0)
        width = nj * nb
    else:
        assert ka == nj * kb
        grid = (nt, 1 if full else nj)
        a_map = lambda g0, g1: (g0, g1)
        b_map = lambda g0, g1: (g1, 0, 0)
        t_map = lambda g0, g1: (g0, 0)
        width = nb
    n_t, n_p, n_o, n_r = len(tiles), len(params), len(outs), len(reds)
    use_acc = (not out_mode) and nj > 1 and not full
    dims = (((1,), (1,)), ((), ())) if trans_b else (((1,), (0,)), ((), ()))

    def body(a_ref, b_ref, *rest):
        t_refs = rest[:n_t]
        p_refs = rest[n_t:n_t + n_p]
        o_refs = rest[n_t + n_p:n_t + n_p + n_o]
        r_refs = rest[n_t + n_p + n_o:n_t + n_p + n_o + n_r]
        i = pl.program_id(1 if out_mode else 0)
        a_val = a_ref[...] if a_pre is None else a_pre(a_ref[...])
        if full:
            b_all, b_sems = rest[-2 - stream_first], rest[-1 - stream_first]

            def weight_copies():
                return [pltpu.make_async_copy(b_ref.at[j], b_all.at[:, j * c:(j + 1) * c], b_sems.at[j])
                        for j in range(nj)]

            if stream_first:
                part = rest[-1]

                @pl.when(i == 0)
                def _():
                    cps = weight_copies()
                    for cp in cps:
                        cp.start()
                    acc = None
                    for j, cp in enumerate(cps):
                        cp.wait()
                        term = lax.dot_general(a_val[:, j * c:(j + 1) * c], b_all[:, j * c:(j + 1) * c], dims,
                                               preferred_element_type=F32)
                        acc = term if acc is None else acc + term
                    part[...] = acc

                @pl.when(i > 0)
                def _():
                    part[...] = lax.dot_general(a_val, b_all[...], dims, preferred_element_type=F32)
            else:
                @pl.when(i == 0)
                def _():
                    cps = weight_copies()
                    for cp in cps:
                        cp.start()
                    for cp in cps:
                        cp.wait()

                part = lax.dot_general(a_val, b_all[...], dims, preferred_element_type=F32)
        else:
            part = lax.dot_general(a_val, b_ref[...], dims, preferred_element_type=F32)

        def finish(acc_rows):
            totals = [None] * n_r
            for r0 in range(0, tm, min(tm, EPI_ROWS)):
                rows = slice(r0, r0 + min(tm, EPI_ROWS))
                if epi is None:
                    res, rr = (acc_rows(rows),), ()
                else:
                    res, rr = epi(acc_rows(rows), [t[rows, :] for t in t_refs], [p[...] for p in p_refs])
                for o_ref, val in zip(o_refs, res):
                    o_ref[rows, :] = val.astype(o_ref.dtype)
                totals = [val if tot is None else tot + val for tot, val in zip(totals, rr)]
            for r_ref, val in zip(r_refs, totals):
                @pl.when(i == 0)
                def _():
                    r_ref[...] = val

                @pl.when(i > 0)
                def _():
                    r_ref[...] += val

        if use_acc:
            acc_ref = rest[-1]
            j = pl.program_id(1)

            @pl.when(j == 0)
            def _():
                acc_ref[...] = part

            @pl.when(jnp.logical_and(j > 0, j < nj - 1))
            def _():
                acc_ref[...] += part

            @pl.when(j == nj - 1)
            def _():
                finish(lambda rows: acc_ref[rows, :] + part[rows])
        else:
            finish(lambda rows: part[rows, :])

    const2 = lambda g0, g1: (0, 0)
    if full:
        in_specs = [pl.BlockSpec((tm, ka), a_map), pl.BlockSpec(memory_space=pl.ANY)]
        scratch = [pltpu.VMEM((r, nj * c), b3.dtype), pltpu.SemaphoreType.DMA((nj,))]
        scratch += [pltpu.VMEM((tm, nb), F32)] if stream_first else []
    else:
        in_specs = [pl.BlockSpec((tm, kb), a_map), pl.BlockSpec((None, r, c), b_map)]
        scratch = [pltpu.VMEM((tm, nb), F32)] if use_acc else []
    in_specs += [pl.BlockSpec((tm, t.shape[1] // nj if out_mode else t.shape[1]), t_map) for t in tiles]
    in_specs += [pl.BlockSpec(p.shape, lambda g0, g1, nd=p.ndim: (0,) * nd) for p in params]
    out_specs = [pl.BlockSpec((tm, nb), t_map) for _ in outs] + [pl.BlockSpec((1, w), const2) for w in reds]
    out_shape = [jax.ShapeDtypeStruct((t_, width), dt) for dt in outs]
    out_shape += [jax.ShapeDtypeStruct((1, w), F32) for w in reds]
    res = pl.pallas_call(
        body, name=name, grid=grid, in_specs=in_specs, out_specs=out_specs, out_shape=out_shape,
        scratch_shapes=scratch, compiler_params=_cp("arbitrary", "arbitrary"),
    )(a, b3, *tiles, *params)
    return res


def _mm_tn(a, g, *, nj, split, name, out_dtype=BF16, a_pre=None):
    t_ = a.shape[0]
    if split == "col":
        r, c = a.shape[1], g.shape[1] // nj
        a_spec = pl.BlockSpec((t_, r), lambda j: (0, 0))
        g_spec = pl.BlockSpec((t_, c), lambda j: (0, j))
    else:
        r, c = a.shape[1] // nj, g.shape[1]
        a_spec = pl.BlockSpec((t_, r), lambda j: (0, j))
        g_spec = pl.BlockSpec((t_, c), lambda j: (0, 0))

    def body(a_ref, g_ref, o_ref):
        a_val = a_ref[...] if a_pre is None else a_pre(a_ref[...])
        o_ref[...] = lax.dot_general(a_val, g_ref[...], (((0,), (0,)), ((), ())),
                                     preferred_element_type=F32).astype(o_ref.dtype)

    return pl.pallas_call(
        body, name=name, grid=(nj,), in_specs=[a_spec, g_spec],
        out_specs=pl.BlockSpec((None, r, c), lambda j: (j, 0, 0)),
        out_shape=jax.ShapeDtypeStruct((nj, r, c), out_dtype),
        compiler_params=_cp("arbitrary"),
    )(a, g)


def _epi_res_norm(acc, tiles, params):
    x_new = tiles[0] + acc
    return (x_new, _rms(x_new, params[0])), ()


def _epi_ple(acc, tiles, params):
    x_old, p_tile = tiles
    g_next, w_ple8 = params
    pe = jnp.concatenate([jnp.dot(p_tile, w_ple8[j], preferred_element_type=F32) for j in range(NDEV)], axis=1)
    x_new = x_old + pe * _sig(acc)
    return (x_new, acc, _rms(x_new, g_next), pe), ()


def _epi_rms_bwd(acc, tiles, params):
    dx, dg = _rms_bwd(acc, tiles[0], params[0], tiles[1])
    return (dx, dx), (dg,)


def _epi_dup(acc, tiles, params):
    return (acc * (2.0 * jnp.maximum(tiles[0].astype(F32), 0.0)),), ()


def _tri_mask():
    row = lax.broadcasted_iota(jnp.int32, (GW, GW), 0)
    col = lax.broadcasted_iota(jnp.int32, (GW, GW), 1)
    return row >= col


def _small_specs(sp_list):
    return [pl.BlockSpec(p.shape, (lambda i: (0, 0)) if p.ndim == 2 else (lambda i: (0, 0, 0))) for p in sp_list]


SUBLANES = 8


def _tap_sum(src, w_ref, taps, rows, stage):
    groups = {}
    for off, k in taps:
        groups.setdefault(off % SUBLANES, []).append((off - off % SUBLANES, k))
    out = None
    for res, members in sorted(groups.items()):
        n = rows if res == 0 else rows + SUBLANES
        part = None
        for base, k in members:
            term = w_ref[k:k + 1, :] * src[pl.ds(base, n), :]
            part = term if part is None else part + term
        if res:
            stage[0:n, :] = part
            part = stage[pl.ds(res, rows), :]
        out = part if out is None else out + part
    return out


def _tap_grads(grad, src, offsets, rows, stage, out_ref):
    pad = SUBLANES
    stage[0:pad, :] = jnp.zeros((pad, grad.shape[1]), F32)
    stage[pad:pad + rows, :] = grad
    stage[pad + rows:2 * pad + rows, :] = jnp.zeros((pad, grad.shape[1]), F32)
    groups = {}
    for k, off in enumerate(offsets):
        groups.setdefault(off % SUBLANES, []).append((off - off % SUBLANES, k))
    for res, members in sorted(groups.items()):
        shifted = stage[pl.ds(pad - res, rows + pad), :]
        for base, k in members:
            out_ref[k:k + 1, :] += _rowsum(shifted * src[pl.ds(base, rows + pad), :])


def _mixer_params(sp):
    return [sp["cw"], sp["cb"], sp["lg"], sp["lb"], sp["pw"], sp["ps"], sp["sc"], sp["gg"], sp["gb"], sp["ws"], sp["bst"]]


def _mixer_fwd(proj, sp, tm=256):
    nt = T // tm
    per = tm // HB

    conv_taps = [(HB - (CONF_K - 1) + k, k) for k in range(CONF_K)]

    def body(main_ref, halo_ref, cw, cb, lg, lb, pw, ps, sc, gg, gb, ws, bst, y_ref, ca_ref, ext, stage):
        i = pl.program_id(0)
        keep = (i > 0).astype(F32)

        def mcol(c0):
            return main_ref[:, c0:c0 + W].astype(F32)

        def hcol(c0):
            return halo_ref[:, c0:c0 + W].astype(F32)

        ext[0:HB, :] = hcol(0) * _sig(hcol(W)) * keep
        ext[HB:HB + tm, :] = mcol(0) * _sig(mcol(W))
        ca = (_tap_sum(ext, cw, conv_taps, tm, stage) + cb[...]).astype(BF16)
        ca_ref[...] = ca
        xh, _ = _ln_stats(ca.astype(F32))
        n = xh * lg[...] + lb[...]
        y_ref[:, 0:W] = (n * _sig(n)).astype(BF16)

        pin = mcol(1024)
        ext[0:HB, :] = hcol(1024) * keep
        ext[HB:HB + tm, :] = pin
        pos = (i * tm + lax.broadcasted_iota(jnp.int32, (tm, 1), 0) + 1).astype(F32)
        for g, w in enumerate(POOL_WINDOWS):
            lo = g * GW
            s = ext[pl.ds(HB, tm), lo:lo + GW]
            for j in range(1, w):
                s = s + ext[pl.ds(HB - j, tm), lo:lo + GW]
            pooled = s / jnp.minimum(pos, float(w)) - pin[:, lo:lo + GW]
            mixed = jnp.dot(pooled.astype(BF16), pw[g].astype(BF16), preferred_element_type=F32)
            y_ref[:, W + lo:W + lo + GW] = (mixed * ps[:, lo:lo + GW]).astype(BF16)

        ext[0:HB, :] = hcol(2048) * hcol(2560) * keep
        ext[HB:HB + tm, :] = mcol(2048) * mcol(2560)
        cv = sc[0:1, :] * ext[pl.ds(HB - 2, tm), :]
        cv = cv + sc[1:2, :] * ext[pl.ds(HB - 1, tm), :]
        cv = cv + sc[2:3, :] * ext[pl.ds(HB, tm), :]
        y_ref[:, 2 * W:3 * W] = (mcol(1536) * cv).astype(BF16)

        vh, _ = _ln_stats(mcol(3584))
        vn = (vh * gg[...] + gb[...]).astype(BF16)
        u = mcol(3072)
        tri = _tri_mask()
        for g in range(4):
            lo = g * GW
            wm = jnp.where(tri, ws[g], 0.0).astype(BF16)
            for c in range(tm // GW):
                r0 = c * GW
                sg = jnp.dot(wm, vn[r0:r0 + GW, lo:lo + GW], preferred_element_type=F32) + bst[:, g:g + 1]
                y_ref[r0:r0 + GW, 3 * W + lo:3 * W + lo + GW] = (u[r0:r0 + GW, lo:lo + GW] * sg).astype(BF16)

    plist = _mixer_params(sp)
    in_specs = [pl.BlockSpec((tm, MIX_COLS), lambda i: (i, 0)),
                pl.BlockSpec((HB, MIX_COLS), lambda i: (jnp.maximum(i * per - 1, 0), 0))]
    in_specs += _small_specs(plist)
    return pl.pallas_call(
        body, name="f_mixers", grid=(nt,), in_specs=in_specs,
        out_specs=[pl.BlockSpec((tm, 4 * W), lambda i: (i, 0)), pl.BlockSpec((tm, W), lambda i: (i, 0))],
        out_shape=[jax.ShapeDtypeStruct((T, 4 * W), BF16), jax.ShapeDtypeStruct((T, W), BF16)],
        scratch_shapes=[pltpu.VMEM((HB + tm, W), F32), pltpu.VMEM((tm + SUBLANES, W), F32)],
        compiler_params=_cp("arbitrary"),
    )(proj, proj, *plist)


def _assemble_wb(wb8_ref, wbf_ref):
    for k in range(4):
        for j in range(NDEV):
            wbf_ref[k, :, j * GW:(j + 1) * GW] = wb8_ref[j, k]


def _merge_fwd(y, proj, wb8, tm=256):
    nt = T // tm

    def body(y_ref, gate_ref, wb8_ref, z_ref, s_ref, m_ref, wbf):
        @pl.when(pl.program_id(0) == 0)
        def _():
            _assemble_wb(wb8_ref, wbf)

        m = jnp.zeros((tm, D), F32)
        for k in range(4):
            zk = jnp.dot(y_ref[:, k * W:(k + 1) * W], wbf[k], preferred_element_type=F32)
            z_ref[:, k * D:(k + 1) * D] = zk.astype(BF16)
            s = _sig(gate_ref[:, k * D:(k + 1) * D].astype(F32))
            s_ref[:, k * D:(k + 1) * D] = s.astype(BF16)
            m = m + s * zk
        m_ref[...] = m.astype(BF16)

    return pl.pallas_call(
        body, name="f_merge", grid=(nt,),
        in_specs=[pl.BlockSpec((tm, 4 * W), lambda i: (i, 0)),
                  pl.BlockSpec((tm, 4 * D), lambda i: (i, 1)),
                  pl.BlockSpec(wb8.shape, lambda i: (0, 0, 0, 0))],
        out_specs=[pl.BlockSpec((tm, 4 * D), lambda i: (i, 0)), pl.BlockSpec((tm, 4 * D), lambda i: (i, 0)),
                   pl.BlockSpec((tm, D), lambda i: (i, 0))],
        out_shape=[jax.ShapeDtypeStruct((T, 4 * D), BF16), jax.ShapeDtypeStruct((T, 4 * D), BF16),
                   jax.ShapeDtypeStruct((T, D), BF16)],
        scratch_shapes=[pltpu.VMEM((4, W, D), BF16)],
        compiler_params=_cp("arbitrary"),
    )(y, proj, wb8)


def _merge_bwd(dm, z, gates, y, wb8, tm=256):
    nt = T // tm

    def body(dm_ref, z_ref, gate_ref, y_ref, wb8_ref, dp_ref, dy_ref, dwb_ref, wbf, acc):
        i = pl.program_id(0)

        @pl.when(i == 0)
        def _():
            _assemble_wb(wb8_ref, wbf)

        dmv = dm_ref[...].astype(F32)
        for k in range(4):
            s = gate_ref[:, k * D:(k + 1) * D].astype(F32)
            dzk = (dmv * s).astype(BF16)
            dp_ref[:, k * D:(k + 1) * D] = (dmv * z_ref[:, k * D:(k + 1) * D].astype(F32) * s * (1.0 - s)).astype(BF16)
            dyk = lax.dot_general(dzk, wbf[k], (((1,), (1,)), ((), ())), preferred_element_type=F32)
            dy_ref[:, k * W:(k + 1) * W] = dyk.astype(BF16)
            part = lax.dot_general(y_ref[:, k * W:(k + 1) * W], dzk, (((0,), (0,)), ((), ())),
                                   preferred_element_type=F32)

            @pl.when(i == 0)
            def _():
                acc[k] = part

            @pl.when(i > 0)
            def _():
                acc[k] += part

        @pl.when(i == nt - 1)
        def _():
            for k in range(4):
                for j in range(NDEV):
                    dwb_ref[j, k] = acc[k, :, j * GW:(j + 1) * GW].astype(BF16)

    return pl.pallas_call(
        body, name="b_merge", grid=(nt,),
        in_specs=[pl.BlockSpec((tm, D), lambda i: (i, 0)),
                  pl.BlockSpec((tm, 4 * D), lambda i: (i, 0)),
                  pl.BlockSpec((tm, 4 * D), lambda i: (i, 0)),
                  pl.BlockSpec((tm, 4 * W), lambda i: (i, 0)),
                  pl.BlockSpec(wb8.shape, lambda i: (0, 0, 0, 0))],
        out_specs=[pl.BlockSpec((tm, 4 * D), lambda i: (i, 1)),
                   pl.BlockSpec((tm, 4 * W), lambda i: (i, 0)),
                   pl.BlockSpec(wb8.shape, lambda i: (0, 0, 0, 0))],
        out_shape=[jax.ShapeDtypeStruct((T, COLS_IN), BF16),
                   jax.ShapeDtypeStruct((T, 4 * W), BF16),
                   jax.ShapeDtypeStruct(wb8.shape, BF16)],
        scratch_shapes=[pltpu.VMEM((4, W, D), BF16), pltpu.VMEM((4, W, D), F32)],
        compiler_params=_cp("arbitrary"),
    )(dm, z, gates, y, wb8)


def _mixer_bwd(proj, ca_saved, dy, dproj, sp, tm=512):
    nt = T // tm
    per = tm // HB
    ne = tm + HA
    last_blk = T // HA - 1
    conv_taps = [(HB - (CONF_K - 1) + k, k) for k in range(CONF_K)]

    def body(main_ref, hb_ref, ha_ref, ca_ref, cah_ref, dy_ref, dyh_ref, cw, cb, lg, lb, pw, ps, sc, gg, gb, ws, bst,
             dp_any, dp_ref, dcw_ref, dsc_ref, vec_ref, dpw_ref, dws_ref, dbs_ref, e1, e2, e3, stage):
        del dp_any, cb
        i = pl.program_id(0)
        keep_b = (i > 0).astype(F32)
        keep_a = (i < nt - 1).astype(F32)

        @pl.when(i == 0)
        def _():
            dcw_ref[...] = jnp.zeros_like(dcw_ref)
            dsc_ref[...] = jnp.zeros_like(dsc_ref)
            vec_ref[...] = jnp.zeros_like(vec_ref)
            dpw_ref[...] = jnp.zeros_like(dpw_ref)
            dws_ref[...] = jnp.zeros_like(dws_ref)
            dbs_ref[...] = jnp.zeros_like(dbs_ref)

        def mcol(c0):
            return main_ref[:, c0:c0 + W].astype(F32)

        def hbcol(c0):
            return hb_ref[:, c0:c0 + W].astype(F32)

        def hacol(c0):
            return ha_ref[:, c0:c0 + W].astype(F32)

        def load_dy(c0):
            e2[0:tm, :] = dy_ref[:, c0:c0 + W].astype(F32)
            e2[tm:ne, :] = dyh_ref[:, c0:c0 + W].astype(F32) * keep_a

        a = mcol(0)
        sa = _sig(mcol(W))
        e1[0:HB, :] = hbcol(0) * _sig(hbcol(W)) * keep_b
        e1[HB:HB + tm, :] = a * sa
        e1[HB + tm:HB + tm + SUBLANES, :] = jnp.zeros((SUBLANES, W), F32)
        e2[0:tm, :] = ca_ref[...].astype(F32)
        e2[tm:ne, :] = cah_ref[...].astype(F32)
        xh, rstd = _ln_stats(e2[0:ne, :])
        nn = xh * lg[...] + lb[...]
        s = _sig(nn)
        load_dy(0)
        dn = e2[0:ne, :] * (s * (1.0 + nn * (1.0 - s)))
        vec_ref[1:2, :] += _rowsum(dn[0:tm] * xh[0:tm])
        vec_ref[2:3, :] += _rowsum(dn[0:tm])
        dca = _ln_bwd(dn * lg[...], xh, rstd)
        e3[0:ne, :] = dca
        dmain = dca[0:tm]
        vec_ref[0:1, :] += _rowsum(dmain)
        _tap_grads(dmain, e1, [off for off, _ in conv_taps], tm, stage, dcw_ref)
        dglu = _tap_sum(e3, cw, [(CONF_K - 1 - k, k) for k in range(CONF_K)], tm, stage)
        dp_ref[:, 0:W] = (dglu * sa).astype(BF16)
        dp_ref[:, W:2 * W] = (dglu * a * sa * (1.0 - sa)).astype(BF16)

        pin = mcol(1024)
        e1[0:HB, :] = hbcol(1024) * keep_b
        e1[HB:HB + tm, :] = pin
        load_dy(W)
        dyb = e2[0:ne, :]
        pos_m = (i * tm + lax.broadcasted_iota(jnp.int32, (tm, 1), 0) + 1).astype(F32)
        pos_e = (i * tm + lax.broadcasted_iota(jnp.int32, (ne, 1), 0) + 1).astype(F32)
        for g, w in enumerate(POOL_WINDOWS):
            lo = g * GW
            acc = e1[pl.ds(HB, tm), lo:lo + GW]
            for j in range(1, w):
                acc = acc + e1[pl.ds(HB - j, tm), lo:lo + GW]
            pooled = (acc / jnp.minimum(pos_m, float(w)) - pin[:, lo:lo + GW]).astype(BF16)
            pwb = pw[g].astype(BF16)
            mixed = jnp.dot(pooled, pwb, preferred_element_type=F32)
            dyb_g = dyb[:, lo:lo + GW]
            vec_ref[3:4, lo:lo + GW] += _rowsum(dyb_g[0:tm] * mixed)
            dmb = (dyb_g * ps[:, lo:lo + GW]).astype(BF16)
            dpw_ref[g] += lax.dot_general(pooled, dmb[0:tm], (((0,), (0,)), ((), ())), preferred_element_type=F32)
            dpool = lax.dot_general(dmb, pwb, (((1,), (1,)), ((), ())), preferred_element_type=F32)
            e3[0:ne, lo:lo + GW] = dpool / jnp.minimum(pos_e, float(w))
            back = e3[pl.ds(0, tm), lo:lo + GW]
            for j in range(1, w):
                back = back + e3[pl.ds(j, tm), lo:lo + GW]
            dp_ref[:, 1024 + lo:1024 + lo + GW] = (back - dpool[0:tm]).astype(BF16)

        cg = mcol(2048)
        hx = mcol(2560)
        e1[0:HB, :] = hbcol(2048) * hbcol(2560) * keep_b
        e1[HB:HB + tm, :] = cg * hx
        load_dy(2 * W)
        dyc = e2[0:tm, :]
        dconv = dyc * mcol(1536)
        e3[0:tm, :] = dconv
        e3[tm:ne, :] = e2[tm:ne, :] * hacol(1536)
        cv = sc[0:1, :] * e1[pl.ds(HB - 2, tm), :]
        for k in range(1, SC_K):
            cv = cv + sc[k:k + 1, :] * e1[pl.ds(HB - 2 + k, tm), :]
        dp_ref[:, 1536:2048] = (dyc * cv).astype(BF16)
        for k in range(SC_K):
            dsc_ref[k:k + 1, :] += _rowsum(dconv * e1[pl.ds(HB - 2 + k, tm), :])
        dq = sc[0:1, :] * e3[pl.ds(2, tm), :]
        for k in range(1, SC_K):
            dq = dq + sc[k:k + 1, :] * e3[pl.ds(2 - k, tm), :]
        dp_ref[:, 2048:2560] = (dq * hx).astype(BF16)
        dp_ref[:, 2560:3072] = (dq * cg).astype(BF16)

        u = mcol(3072)
        vh, vr = _ln_stats(mcol(3584))
        vn = (vh * gg[...] + gb[...]).astype(BF16)
        dyd = dy_ref[:, 3 * W:4 * W].astype(F32)
        tri = _tri_mask()
        for g in range(4):
            lo = g * GW
            wm = jnp.where(tri, ws[g], 0.0).astype(BF16)
            dws_g = jnp.zeros((GW, GW), F32)
            dbs_g = jnp.zeros((GW, 1), F32)
            for c in range(tm // GW):
                r0 = c * GW
                blk = vn[r0:r0 + GW, lo:lo + GW]
                sg = jnp.dot(wm, blk, preferred_element_type=F32) + bst[:, g:g + 1]
                dyd_b = dyd[r0:r0 + GW, lo:lo + GW]
                dp_ref[r0:r0 + GW, 3072 + lo:3072 + lo + GW] = (dyd_b * sg).astype(BF16)
                dsg = dyd_b * u[r0:r0 + GW, lo:lo + GW]
                dsgb = dsg.astype(BF16)
                dbs_g = dbs_g + jnp.sum(dsg, axis=-1, keepdims=True)
                dws_g = dws_g + lax.dot_general(dsgb, blk, (((1,), (1,)), ((), ())), preferred_element_type=F32)
                e1[r0:r0 + GW, lo:lo + GW] = lax.dot_general(wm, dsgb, (((0,), (0,)), ((), ())),
                                                             preferred_element_type=F32)
            dws_ref[g] += jnp.where(tri, dws_g, 0.0)
            dbs_ref[g] += jnp.broadcast_to(dbs_g, (GW, GW))
        dvn = e1[0:tm, :]
        vec_ref[4:5, :] += _rowsum(dvn * vh)
        vec_ref[5:6, :] += _rowsum(dvn)
        dp_ref[:, 3584:4096] = _ln_bwd(dvn * gg[...], vh, vr).astype(BF16)

    plist = _mixer_params(sp)
    in_specs = [pl.BlockSpec((tm, MIX_COLS), lambda i: (i, 0)),
                pl.BlockSpec((HB, MIX_COLS), lambda i: (jnp.maximum(i * per - 1, 0), 0)),
                pl.BlockSpec((HA, MIX_COLS), lambda i: (jnp.minimum((i + 1) * per, last_blk), 0)),
                pl.BlockSpec((tm, W), lambda i: (i, 0)),
                pl.BlockSpec((HA, W), lambda i: (jnp.minimum((i + 1) * per, last_blk), 0)),
                pl.BlockSpec((tm, 4 * W), lambda i: (i, 0)),
                pl.BlockSpec((HA, 4 * W), lambda i: (jnp.minimum((i + 1) * per, last_blk), 0))]
    in_specs += _small_specs(plist)
    in_specs += [pl.BlockSpec(memory_space=pl.ANY)]
    z2 = lambda i: (0, 0)
    z3 = lambda i: (0, 0, 0)
    out_specs = [pl.BlockSpec((tm, MIX_COLS), lambda i: (i, 0)),
                 pl.BlockSpec((32, W), z2), pl.BlockSpec((8, W), z2), pl.BlockSpec((8, W), z2),
                 pl.BlockSpec((4, GW, GW), z3), pl.BlockSpec((4, GW, GW), z3), pl.BlockSpec((4, GW, GW), z3)]
    out_shape = [jax.ShapeDtypeStruct((T, COLS_IN), BF16),
                 jax.ShapeDtypeStruct((32, W), F32), jax.ShapeDtypeStruct((8, W), F32),
                 jax.ShapeDtypeStruct((8, W), F32),
                 jax.ShapeDtypeStruct((4, GW, GW), F32), jax.ShapeDtypeStruct((4, GW, GW), F32),
                 jax.ShapeDtypeStruct((4, GW, GW), F32)]
    n_in = 7 + len(plist)
    return pl.pallas_call(
        body, name="b_mixers", grid=(nt,), in_specs=in_specs, out_specs=out_specs, out_shape=out_shape,
        scratch_shapes=[pltpu.VMEM((HB + ne, W), F32), pltpu.VMEM((ne, W), F32), pltpu.VMEM((ne, W), F32),
                        pltpu.VMEM((ne + SUBLANES, W), F32)],
        input_output_aliases={n_in: 0},
        compiler_params=_cp("arbitrary"),
    )(proj, proj, proj, ca_saved, ca_saved, dy, dy, *plist, dproj)


def _norm_first(x, g, tm=512):
    def body(x_ref, g_ref, o_ref):
        o_ref[...] = _rms(x_ref[...], g_ref[...]).astype(BF16)

    return pl.pallas_call(
        body, name="f_norm0", grid=(T // tm,),
        in_specs=[pl.BlockSpec((tm, D), lambda i: (i, 0)), pl.BlockSpec((1, D), lambda i: (0, 0))],
        out_specs=pl.BlockSpec((tm, D), lambda i: (i, 0)),
        out_shape=jax.ShapeDtypeStruct((T, D), BF16), compiler_params=_cp("arbitrary"),
    )(x, g)


def _loss_head(x, target, g, tm=256):
    def body(x_ref, t_ref, g_ref, dx_ref, dg_ref, loss_ref):
        i = pl.program_id(0)
        x = x_ref[...]
        r = lax.rsqrt(jnp.mean(x * x, axis=-1, keepdims=True) + EPS)
        xh = x * r
        gv = g_ref[...]
        e = xh * gv - t_ref[...]
        dyv = e * (1.0 / D)
        part = jnp.sum(_rowsum(e * e), axis=-1, keepdims=True) * (0.5 / D)
        u = dyv * gv
        dx_ref[...] = r * (u - xh * jnp.mean(u * xh, axis=-1, keepdims=True))
        dgp = _rowsum(dyv * xh)

        @pl.when(i == 0)
        def _():
            dg_ref[...] = dgp
            loss_ref[...] = jnp.broadcast_to(part, (1, GW))

        @pl.when(i > 0)
        def _():
            dg_ref[...] += dgp
            loss_ref[...] += jnp.broadcast_to(part, (1, GW))

    return pl.pallas_call(
        body, name="loss_head", grid=(T // tm,),
        in_specs=[pl.BlockSpec((tm, D), lambda i: (i, 0)), pl.BlockSpec((tm, D), lambda i: (i, 0)),
                  pl.BlockSpec((1, D), lambda i: (0, 0))],
        out_specs=[pl.BlockSpec((tm, D), lambda i: (i, 0)), pl.BlockSpec((1, D), lambda i: (0, 0)),
                   pl.BlockSpec((1, GW), lambda i: (0, 0))],
        out_shape=[jax.ShapeDtypeStruct((T, D), F32), jax.ShapeDtypeStruct((1, D), F32),
                   jax.ShapeDtypeStruct((1, GW), F32)],
        compiler_params=_cp("arbitrary"),
    )(x, target, g)


def _out_bwd(dx2b, merged, w_out8, tm=512):
    nt = T // tm

    def body(dx_ref, mg_ref, w_ref, dm_ref, dw_ref, acc):
        i = pl.program_id(0)
        dx = dx_ref[...]
        dm_ref[...] = lax.dot_general(dx, w_ref[...], (((1,), (1,)), ((), ())),
                                      preferred_element_type=F32).astype(BF16)
        part = lax.dot_general(mg_ref[...], dx, (((0,), (0,)), ((), ())), preferred_element_type=F32)

        @pl.when(i == 0)
        def _():
            acc[...] = part

        @pl.when(i > 0)
        def _():
            acc[...] += part

        @pl.when(i == nt - 1)
        def _():
            for j in range(NDEV):
                dw_ref[j] = acc[j * GW:(j + 1) * GW, :].astype(BF16)

    tile = pl.BlockSpec((tm, D), lambda i: (i, 0))
    return pl.pallas_call(
        body, name="b_out", grid=(nt,),
        in_specs=[tile, tile, pl.BlockSpec((D, D), lambda i: (0, 0))],
        out_specs=[tile, pl.BlockSpec((NDEV, GW, D), lambda i: (0, 0, 0))],
        out_shape=[jax.ShapeDtypeStruct((T, D), BF16), jax.ShapeDtypeStruct((NDEV, GW, D), BF16)],
        scratch_shapes=[pltpu.VMEM((D, D), F32)],
        compiler_params=_cp("arbitrary"),
    )(dx2b, merged, w_out8.reshape(D, D))


def _ple_bwd(dx4, sv, w_pleg8, g_ple, tm=256):
    nt = T // tm
    ple_dim = sv["p"].shape[1]

    def body(dx_ref, gl_ref, pe_ref, x_ref, h_ref, p_ref, g_ref, wg_ref,
             dx3_ref, dx3b_ref, dg_ref, dwg_ref, dwp_ref, acc_g, acc_p):
        i = pl.program_id(0)
        d = dx_ref[...]
        s = _sig(gl_ref[...].astype(F32))
        dpe = (d * s).astype(BF16)
        dgl = (d * pe_ref[...].astype(F32) * s * (1.0 - s)).astype(BF16)
        dh = lax.dot_general(dgl, wg_ref[...], (((1,), (1,)), ((), ())), preferred_element_type=F32)
        dx, dgp = _rms_bwd(dh, x_ref[...], g_ref[...], d)
        dx3_ref[...] = dx
        dx3b_ref[...] = dx.astype(BF16)
        part_g = lax.dot_general(h_ref[...], dgl, (((0,), (0,)), ((), ())), preferred_element_type=F32)
        part_p = lax.dot_general(p_ref[...], dpe, (((0,), (0,)), ((), ())), preferred_element_type=F32)

        @pl.when(i == 0)
        def _():
            dg_ref[...] = dgp
            acc_g[...] = part_g
            acc_p[...] = part_p

        @pl.when(i > 0)
        def _():
            dg_ref[...] += dgp
            acc_g[...] += part_g
            acc_p[...] += part_p

        @pl.when(i == nt - 1)
        def _():
            for j in range(NDEV):
                dwg_ref[j] = acc_g[j * GW:(j + 1) * GW, :].astype(BF16)
                dwp_ref[j] = acc_p[:, j * GW:(j + 1) * GW].astype(BF16)

    tile = lambda w: pl.BlockSpec((tm, w), lambda i: (i, 0))
    const = lambda shp: pl.BlockSpec(shp, lambda i: (0,) * len(shp))
    return pl.pallas_call(
        body, name="b_ple", grid=(nt,),
        in_specs=[tile(D), tile(D), tile(D), tile(D), tile(D), tile(ple_dim), const((1, D)), const((D, D))],
        out_specs=[tile(D), tile(D), const((1, D)), const((NDEV, GW, D)), const((NDEV, ple_dim, GW))],
        out_shape=[jax.ShapeDtypeStruct((T, D), F32), jax.ShapeDtypeStruct((T, D), BF16),
                   jax.ShapeDtypeStruct((1, D), F32), jax.ShapeDtypeStruct((NDEV, GW, D), BF16),
                   jax.ShapeDtypeStruct((NDEV, ple_dim, GW), BF16)],
        scratch_shapes=[pltpu.VMEM((D, D), F32), pltpu.VMEM((ple_dim, D), F32)],
        compiler_params=_cp("arbitrary"),
    )(dx4, sv["gl"], sv["pe"], sv["x3"], sv["h3"], sv["p"], g_ple, w_pleg8.reshape(D, D))


def _layer_fwd(x, h1, p_bf, gw, sp, g_next):
    proj, = _mm(h1, gw["w_in"], mode="out", name="f_proj", outs=[BF16], tm=T)
    y, ca = _mixer_fwd(proj, sp)
    z, gates, merged = _merge_fwd(y, proj, gw["w_branch"])
    x2, h2 = _mm(merged, gw["w_out"].reshape(1, D, D), mode="acc", name="f_out", outs=[F32, BF16], tm=T // 2,
                 tiles=[x], params=[sp["g_mlp"]], epi=_epi_res_norm)
    up, = _mm(h2, gw["w_up"], mode="out", name="f_up", outs=[BF16], tm=T)
    x3, h3 = _mm(up, gw["w_down"].reshape(1, 4 * D, D), mode="acc", name="f_down", outs=[F32, BF16], tm=T // 4,
                 tiles=[x2], params=[sp["g_ple"]], epi=_epi_res_norm, a_pre=_relu2_bf16)
    x4, gl, hn, pe = _mm(h3, gw["w_pleg"].reshape(1, D, D), mode="acc", name="f_gate", tm=T // 2,
                         outs=[F32, BF16, BF16, BF16], tiles=[x3, p_bf], params=[g_next, gw["w_ple"]], epi=_epi_ple)
    saved = dict(x=x, h1=h1, proj=proj, y=y, ca=ca, z=z, gates=gates, merged=merged, x2=x2, h2=h2, up=up, x3=x3, h3=h3,
                 pe=pe, gl=gl, p=p_bf)
    return x4, hn, saved


def _layer_bwd(dx4, sv, gw, sp, submit, early_group=False):
    dw = {}
    dx3, dx3b, dg_ple, dw["w_pleg"], dw["w_ple"] = _ple_bwd(dx4, sv, gw["w_pleg"], sp["g_ple"])
    dup, = _mm(dx3b, gw["w_down"], mode="out", trans_b=True, name="b_dact", outs=[BF16], tm=T,
               tiles=[sv["up"]], epi=_epi_dup)
    dw["w_down"] = _mm_tn(sv["up"], dx3b, nj=NDEV, split="row", name="b_dw_down", a_pre=_relu2_bf16)
    dw["w_up"] = _mm_tn(sv["h2"], dup, nj=NDEV, split="col", name="b_dw_up")
    if early_group:
        dw["w_up"], dup = lax.optimization_barrier((dw["w_up"], dup))
        dup = submit(dw, ("w_up", "w_down", "w_ple", "w_pleg"), dup)
    dx2, dx2b, dg_mlp = _mm(dup, gw["w_up"], mode="full", trans_b=True, name="b_dh2", tm=T // 8, stream_first=True,
                            outs=[F32, BF16], tiles=[sv["x2"], dx3], params=[sp["g_mlp"]], epi=_epi_rms_bwd, reds=[D])
    dm, dw["w_out"] = _out_bwd(dx2b, sv["merged"], gw["w_out"])
    dproj, dy, dw["w_branch"] = _merge_bwd(dm, sv["z"], sv["gates"], sv["y"], gw["w_branch"])
    dy = submit(dw, ("w_branch", "w_out") if early_group else BIG[1:], dy)
    dproj, dcw, dsc, vec, dpw, dws, dbs = _mixer_bwd(sv["proj"], sv["ca"], dy, dproj, sp)
    dw["w_in"] = _mm_tn(sv["h1"], dproj, nj=NDEV, split="col", name="b_dw_in")
    dw["w_in"], dproj = lax.optimization_barrier((dw["w_in"], dproj))
    dproj = submit(dw, BIG[:1], dproj)
    dx, dg_mix = _mm(dproj, gw["w_in"], mode="full", trans_b=True, name="b_dh1", outs=[F32], tm=T // 8,
                     stream_first=True,
                     tiles=[sv["x"], dx2], params=[sp["g_mix"]], epi=_epi_rms_bwd, reds=[D])
    small = dict(norm_mix=dg_mix[0], conf_dw=dcw[:CONF_K], conf_dw_b=vec[0], conf_ln_g=vec[1], conf_ln_b=vec[2],
                 pool_w=dpw, pool_scale=vec[3], sc_conv=dsc[:SC_K], gmlp_ln_g=vec[4], gmlp_ln_b=vec[5],
                 gmlp_ws=dws, gmlp_bs=dbs[:, :, 0], norm_mlp=dg_mlp[0], norm_ple=dg_ple[0])
    return dx, small


ANY = pl.BlockSpec(memory_space=pl.ANY)


def _mesh_pos():
    return lax.axis_index("x"), lax.axis_index("y"), lax.axis_index("c")


def _other_chips(x, y):
    return [(1 - x, y), (x, 1 - y), (1 - x, 1 - y)]


def _launch_comm(body, peers_of, operands, out_shapes, sems, name, seq_id):
    n_in, n_out = len(operands), len(out_shapes)
    if seq_id is None:
        return pl.pallas_call(body, name=name, in_specs=[ANY] * n_in, out_specs=[ANY] * n_out,
                              out_shape=out_shapes, scratch_shapes=sems)(*operands)

    def seq_body(*refs):
        peers = peers_of(*_mesh_pos())
        barrier = pltpu.get_barrier_semaphore()
        for peer in peers:
            pl.semaphore_signal(barrier, inc=1, device_id=peer, device_id_type=MESH)
        pl.semaphore_wait(barrier, len(peers))
        body(*refs)

    return pl.kernel(seq_body, name=name, out_type=out_shapes,
                     mesh=plsc.ScalarSubcoreMesh(axis_name="seq", num_cores=1), scratch_types=sems,
                     compiler_params=pltpu.CompilerParams(collective_id=seq_id))(*operands)


def _all_gather(shards, name, seq_id=None):
    n = len(shards)

    def body(*refs):
        s_refs, o_refs = refs[:n], refs[n:2 * n]
        send_sems, recv_sems, local_sems = refs[2 * n:]
        x, y, c = _mesh_pos()
        me = 4 * x + 2 * y + c
        here = (x, y, c)
        sibling = (x, y, 1 - c)
        chips = _other_chips(x, y)

        def slot(px, py, pc):
            return 4 * px + 2 * py + pc

        def copy(t, k, slot_idx, to, src=None):
            dst = o_refs[t].at[slot_idx]
            return pltpu.make_async_remote_copy(
                src_ref=dst if src is None else src, dst_ref=dst,
                send_sem=send_sems.at[t * 7 + k], recv_sem=recv_sems.at[t * 7 + k],
                device_id=to, device_id_type=MESH)

        mine = [pltpu.make_async_copy(s_refs[t], o_refs[t].at[me], local_sems.at[t]) for t in range(n)]
        for cp in mine:
            cp.start()
        first = []
        for t in range(n):
            for j, chip in enumerate(chips):
                first.append(copy(t, 1 + j, me, (*chip, c), src=s_refs[t]))
        for t in range(n):
            first.append(copy(t, 0, me, sibling, src=s_refs[t]))
        for cp in first:
            cp.start()
        passed = []
        for t in range(n):
            for j, chip in enumerate(chips):
                copy(t, 1 + j, slot(*chip, c), here).wait_recv()
                fwd = copy(t, 4 + j, slot(*chip, c), sibling)
                fwd.start()
                passed.append(fwd)
        for t in range(n):
            copy(t, 0, slot(x, y, 1 - c), here).wait_recv()
            for j, chip in enumerate(chips):
                copy(t, 4 + j, slot(*chip, 1 - c), here).wait_recv()
        for cp in first + passed:
            cp.wait_send()
        for cp in mine:
            cp.wait()

    def peers_of(x, y, c):
        return [(x, y, 1 - c)] + [(*chip, c) for chip in _other_chips(x, y)]

    return _launch_comm(
        body, peers_of, shards, [jax.ShapeDtypeStruct((NDEV,) + s.shape, s.dtype) for s in shards],
        [pltpu.SemaphoreType.DMA((7 * n,)), pltpu.SemaphoreType.DMA((7 * n,)), pltpu.SemaphoreType.DMA((n,))],
        name, seq_id)


def _rs_exchange(p4s, qs, name, seq_id=None):
    n_p, n_q = len(p4s), len(qs)

    def body(*refs):
        p_refs, q_refs = refs[:n_p], refs[n_p:n_p + n_q]
        rb_refs, rc_refs = refs[n_p + n_q:2 * n_p + n_q], refs[2 * n_p + n_q:2 * (n_p + n_q)]
        pair_send, pair_recv, chip_send, chip_recv, local_sems = refs[2 * (n_p + n_q):]
        x, y, c = _mesh_pos()
        a_idx = 2 * x + y
        chips = _other_chips(x, y)
        mine = [pltpu.make_async_copy(q_refs[t].at[a_idx], rc_refs[t].at[a_idx], local_sems.at[t])
                for t in range(n_q)]
        sends = []
        for t in range(n_q):
            for j, chip in enumerate(chips):
                sends.append(pltpu.make_async_remote_copy(
                    src_ref=q_refs[t].at[2 * chip[0] + chip[1]], dst_ref=rc_refs[t].at[a_idx],
                    send_sem=chip_send.at[t * 3 + j], recv_sem=chip_recv.at[t * 3 + j],
                    device_id=(*chip, c), device_id_type=MESH))
        pairs = [pltpu.make_async_remote_copy(
            src_ref=p_refs[t].at[:, 1 - c], dst_ref=rb_refs[t], send_sem=pair_send.at[t], recv_sem=pair_recv.at[t],
            device_id=(x, y, 1 - c), device_id_type=MESH) for t in range(n_p)]
        for cp in sends + mine + pairs:
            cp.start()
        for cp in pairs:
            cp.wait()
        for t in range(n_q):
            for j, chip in enumerate(chips):
                landed = rc_refs[t].at[2 * chip[0] + chip[1]]
                pltpu.make_async_remote_copy(
                    src_ref=landed, dst_ref=landed, send_sem=chip_send.at[t * 3 + j],
                    recv_sem=chip_recv.at[t * 3 + j], device_id=(x, y, c), device_id_type=MESH).wait_recv()
        for cp in sends:
            cp.wait_send()
        for cp in mine:
            cp.wait()

    def peers_of(x, y, c):
        peers = [(x, y, 1 - c)] if n_p else []
        return peers + ([(*chip, c) for chip in _other_chips(x, y)] if n_q else [])

    out_shapes = [jax.ShapeDtypeStruct((NCHIP,) + p.shape[2:], p.dtype) for p in p4s]
    out_shapes += [jax.ShapeDtypeStruct(q.shape, q.dtype) for q in qs]
    sems = [pltpu.SemaphoreType.DMA((max(n_p, 1),)), pltpu.SemaphoreType.DMA((max(n_p, 1),)),
            pltpu.SemaphoreType.DMA((max(3 * n_q, 1),)), pltpu.SemaphoreType.DMA((max(3 * n_q, 1),)),
            pltpu.SemaphoreType.DMA((max(n_q, 1),))]
    got = _launch_comm(body, peers_of, list(p4s) + list(qs), out_shapes, sems, name, seq_id)
    return got[:n_p], got[n_p:]


def _pair_sum(p4s, rbs, c_idx, name, nst=1):
    n = len(p4s)
    trs = [p.shape[2] // nst for p in p4s]

    def body(c_ref, *refs):
        del c_ref
        p_refs, r_refs, o_refs = refs[:n], refs[n:2 * n], refs[2 * n:]
        for p_ref, r_ref, o_ref in zip(p_refs, r_refs, o_refs):
            o_ref[...] = (p_ref[...].astype(F32) + r_ref[...].astype(F32)).astype(o_ref.dtype)

    in_specs = [pl.BlockSpec((None, None, tr, p.shape[3]), lambda b, i, c_ref: (b, c_ref[0], i, 0))
                for p, tr in zip(p4s, trs)]
    in_specs += [pl.BlockSpec((None, tr, p.shape[3]), lambda b, i, c_ref: (b, i, 0)) for p, tr in zip(p4s, trs)]
    out_specs = [pl.BlockSpec((None, tr, p.shape[3]), lambda b, i, c_ref: (b, i, 0)) for p, tr in zip(p4s, trs)]
    return pl.pallas_call(
        body, name=name,
        grid_spec=pltpu.PrefetchScalarGridSpec(num_scalar_prefetch=1, grid=(NCHIP, nst), in_specs=in_specs,
                                               out_specs=out_specs),
        out_shape=[jax.ShapeDtypeStruct((NCHIP,) + p.shape[2:], p.dtype) for p in p4s],
        compiler_params=_cp("arbitrary", "arbitrary"),
    )(c_idx, *p4s, *rbs)


class _GradientPipeline:
    def __init__(self, c_idx, results):
        self.c_idx, self.results, self.pending = c_idx, results, None

    def _sum_pending(self, chain):
        names, layer, p4s, rbs = self.pending
        qs = _pair_sum(p4s, rbs, self.c_idx, name="rs_pairsum_%d" % len(names))
        return lax.optimization_barrier((chain, qs))

    def submit(self, dw, names, layer, chain):
        qs, tag, seq_id = [], "pair", 3
        if self.pending is not None:
            chain, qs = self._sum_pending(chain)
            tag, seq_id = "pair_chip", 4
        p4s = [dw[n].reshape((NCHIP, 2) + BIG_SHARD[n]) for n in names]
        rbs, rcs = _rs_exchange(p4s, qs, name="rs_%s_%d" % (tag, len(names)), seq_id=seq_id)
        self._record(rcs)
        self.pending = (names, layer, p4s, rbs)
        return chain

    def finish(self, chain):
        chain, qs = self._sum_pending(chain)
        self._record(_rs_exchange([], qs, name="rs_chip_last", seq_id=5)[1])
        self.pending = None
        return chain

    def _record(self, rcs):
        if rcs:
            names, layer = self.pending[:2]
            for n, rc in zip(names, rcs):
                self.results[n][layer] = rc


def _adamw(w, g, m, v):
    m = ADAM_B1 * m + (1.0 - ADAM_B1) * g
    v = ADAM_B2 * v + (1.0 - ADAM_B2) * (g * g)
    m_hat = m / (1.0 - ADAM_B1 ** ADAM_STEP)
    v_hat = v / (1.0 - ADAM_B2 ** ADAM_STEP)
    delta = -ADAM_LR * (m_hat / (jnp.sqrt(v_hat) + ADAM_EPS) + ADAM_WD * w)
    return delta, m, v


def _adam_sharded(rcs, w, m, v, tr, name, first_layer, partial=None):
    _, r, c = w.shape
    nst = r // tr
    n_l = len(rcs)

    def body(*refs):
        rc_refs = refs[:n_l]
        w_ref, m_ref, v_ref = refs[n_l:n_l + 3]
        g_out, d_out, m_out, v_out = refs[-4:]
        layer = pl.program_id(0)
        for k, rc in enumerate(rc_refs):
            @pl.when(layer == k)
            def _():
                g = rc[0].astype(F32) + rc[1].astype(F32) + rc[2].astype(F32) + rc[3].astype(F32)
                delta, m_new, v_new = _adamw(w_ref[...], g, m_ref[...], v_ref[...])
                g_out[...] = g
                d_out[...] = delta
                m_out[...] = m_new
                v_out[...] = v_new

    rc_specs = [pl.BlockSpec((NCHIP, tr, c), lambda l, i, k=k: (0, jnp.where(l == k, i, 0), 0)) for k in range(n_l)]
    wspec = pl.BlockSpec((None, tr, c), lambda l, i: (first_layer + l, i, 0))
    carried = [] if partial is None else list(partial)
    return pl.pallas_call(
        body, name=name, grid=(n_l, nst),
        in_specs=rc_specs + [wspec] * 3 + [pl.BlockSpec(memory_space=pl.ANY)] * len(carried),
        out_specs=[wspec] * 4, out_shape=[jax.ShapeDtypeStruct(w.shape, F32)] * 4,
        input_output_aliases={n_l + 3 + k: k for k in range(len(carried))},
        compiler_params=_cp("arbitrary", "arbitrary"),
    )(*rcs, w, m, v, *carried)


def _adam_packed(g, w, m, v, direct):
    n_d = len(direct)

    def pieces(shape):
        width = shape[-1]
        count = 1
        for s in shape[:-1]:
            count *= s
        per_row = D // width
        out = []
        for k in range(count):
            idx = (k,) if len(shape) == 2 else (k // shape[1], k % shape[1])
            out.append((idx, k // per_row, (k % per_row) * width, width))
        return out

    def body(g_ref, w_ref, m_ref, v_ref, d_out, m_out, v_out, *outs):
        delta, m_new, v_new = _adamw(w_ref[...], g_ref[...], m_ref[...], v_ref[...])
        d_out[...] = delta
        m_out[...] = m_new
        v_out[...] = v_new
        for a, (_, row0, shape) in enumerate(direct):
            for src, dst in zip((g_ref, d_out, m_out, v_out), outs[4 * a:4 * a + 4]):
                for idx, row, lane0, width in pieces(shape):
                    piece = src[pl.ds(row0 + row, 1), lane0:lane0 + width]
                    if len(idx) == 1:
                        dst[pl.ds(idx[0], 1), :] = piece
                    else:
                        dst[idx[0], pl.ds(idx[1], 1), :] = piece

    out_shape = [jax.ShapeDtypeStruct(g.shape, F32)] * 3
    for _, _, shape in direct:
        out_shape += [jax.ShapeDtypeStruct(shape, F32)] * 4
    res = pl.pallas_call(body, name="adam_small", out_shape=out_shape,
                         compiler_params=pltpu.CompilerParams(vmem_limit_bytes=VMEM_LIMIT_BYTES))(g, w, m, v)
    return res[:3], {name: res[3 + 4 * a:7 + 4 * a] for a, (name, _, _) in enumerate(direct)}


def _sum4(rc):
    def body(rc_ref, o_ref):
        o_ref[...] = rc_ref[0] + rc_ref[1] + rc_ref[2] + rc_ref[3]

    return pl.pallas_call(
        body, name="small_sum", out_shape=jax.ShapeDtypeStruct(rc.shape[1:], F32),
    )(rc)


BIG = ("w_in", "w_branch", "w_out", "w_up", "w_down", "w_ple", "w_pleg")
BIG_SHARD = {"w_in": (D, D), "w_branch": (4 * W, GW), "w_out": (GW, D), "w_up": (D, W), "w_down": (W, D),
             "w_ple": (256, GW), "w_pleg": (GW, D)}
ADAM_ROWS = {"w_in": 256, "w_branch": 512, "w_out": 128, "w_up": 256, "w_down": 256, "w_ple": 256, "w_pleg": 128}
SMALL = (("norm_mix", (DEPTH, D)), ("conf_dw", (DEPTH, CONF_K, W)), ("conf_dw_b", (DEPTH, W)),
         ("conf_ln_g", (DEPTH, W)), ("conf_ln_b", (DEPTH, W)), ("pool_w", (DEPTH, 4, GW, GW)),
         ("pool_scale", (DEPTH, W)), ("sc_conv", (DEPTH, SC_K, W)), ("gmlp_ln_g", (DEPTH, W)),
         ("gmlp_ln_b", (DEPTH, W)), ("gmlp_ws", (DEPTH, 4, GW, GW)), ("gmlp_bs", (DEPTH, 4, GW)),
         ("norm_mlp", (DEPTH, D)), ("norm_ple", (DEPTH, D)), ("norm_final", (D,)))
CHANNEL_SHARDED = ("conf_dw", "sc_conv")
SMALL_ROWS = 80


def _pack(arrs, rows):
    flat = jnp.concatenate([a.reshape(-1) for a in arrs])
    return jnp.pad(flat, (0, rows * D - flat.shape[0])).reshape(rows, D)


def _unpack(packed, shapes):
    flat = packed.reshape(-1)
    out, off = [], 0
    for shp in shapes:
        size = 1
        for s in shp:
            size *= s
        out.append(flat[off:off + size].reshape(shp))
        off += size
    return out


def kernel(x, p, norm_mix, w_in, conf_dw, conf_dw_b, conf_ln_g, conf_ln_b, pool_w, pool_scale, sc_conv, gmlp_ln_g, gmlp_ln_b, gmlp_ws, gmlp_bs, w_branch, w_out, norm_mlp, w_up, w_down, norm_ple, w_ple, w_ple_gate, norm_final, loss_target, m_norm_mix, m_w_in, m_conf_dw, m_conf_dw_b, m_conf_ln_g, m_conf_ln_b, m_pool_w, m_pool_scale, m_sc_conv, m_gmlp_ln_g, m_gmlp_ln_b, m_gmlp_ws, m_gmlp_bs, m_w_branch, m_w_out, m_norm_mlp, m_w_up, m_w_down, m_norm_ple, m_w_ple, m_w_ple_gate, m_norm_final, v_norm_mix, v_w_in, v_conf_dw, v_conf_dw_b, v_conf_ln_g, v_conf_ln_b, v_pool_w, v_pool_scale, v_sc_conv, v_gmlp_ln_g, v_gmlp_ln_b, v_gmlp_ws, v_gmlp_bs, v_w_branch, v_w_out, v_norm_mlp, v_w_up, v_w_down, v_norm_ple, v_w_ple, v_w_ple_gate, v_norm_final):
    weights = dict(norm_mix=norm_mix, w_in=w_in, conf_dw=conf_dw, conf_dw_b=conf_dw_b, conf_ln_g=conf_ln_g,
                   conf_ln_b=conf_ln_b, pool_w=pool_w, pool_scale=pool_scale, sc_conv=sc_conv, gmlp_ln_g=gmlp_ln_g,
                   gmlp_ln_b=gmlp_ln_b, gmlp_ws=gmlp_ws, gmlp_bs=gmlp_bs, w_branch=w_branch, w_out=w_out,
                   norm_mlp=norm_mlp, w_up=w_up, w_down=w_down, norm_ple=norm_ple, w_ple=w_ple, w_pleg=w_ple_gate,
                   norm_final=norm_final)
    mom1 = dict(norm_mix=m_norm_mix, w_in=m_w_in, conf_dw=m_conf_dw, conf_dw_b=m_conf_dw_b, conf_ln_g=m_conf_ln_g,
                conf_ln_b=m_conf_ln_b, pool_w=m_pool_w, pool_scale=m_pool_scale, sc_conv=m_sc_conv,
                gmlp_ln_g=m_gmlp_ln_g, gmlp_ln_b=m_gmlp_ln_b, gmlp_ws=m_gmlp_ws, gmlp_bs=m_gmlp_bs,
                w_branch=m_w_branch, w_out=m_w_out, norm_mlp=m_norm_mlp, w_up=m_w_up, w_down=m_w_down,
                norm_ple=m_norm_ple, w_ple=m_w_ple, w_pleg=m_w_ple_gate, norm_final=m_norm_final)
    mom2 = dict(norm_mix=v_norm_mix, w_in=v_w_in, conf_dw=v_conf_dw, conf_dw_b=v_conf_dw_b, conf_ln_g=v_conf_ln_g,
                conf_ln_b=v_conf_ln_b, pool_w=v_pool_w, pool_scale=v_pool_scale, sc_conv=v_sc_conv,
                gmlp_ln_g=v_gmlp_ln_g, gmlp_ln_b=v_gmlp_ln_b, gmlp_ws=v_gmlp_ws, gmlp_bs=v_gmlp_bs,
                w_branch=v_w_branch, w_out=v_w_out, norm_mlp=v_norm_mlp, w_up=v_w_up, w_down=v_w_down,
                norm_ple=v_norm_ple, w_ple=v_w_ple, w_pleg=v_w_ple_gate, norm_final=v_norm_final)

    xi, yi, ci = _mesh_pos()
    me = 4 * xi + 2 * yi + ci
    c_idx = jnp.reshape(ci, (1,)).astype(jnp.int32)

    gathered, conf_full, sc_full = [], [], []
    for l in range(DEPTH):
        shard = lambda n: weights[n][l].astype(BF16).reshape(BIG_SHARD[n])
        w_in_g, conf_g, sc_g = _all_gather([shard("w_in"), conf_dw[l], sc_conv[l]], name="ag_first", seq_id=1)
        if l + 1 < DEPTH:
            rest = _all_gather([shard(n) for n in BIG[1:]], name="ag_rest", seq_id=2)
        else:
            rest = (list(_all_gather([shard(n) for n in BIG[1:4]], name="ag_rest_a", seq_id=2))
                    + list(_all_gather([shard(n) for n in BIG[4:]], name="ag_rest_b", seq_id=2)))
        gw = dict(zip(BIG[1:], rest), w_in=w_in_g)
        gw["w_branch"] = gw["w_branch"].reshape(NDEV, 4, W, GW)
        gathered.append(gw)
        conf_full.append(conf_g)
        sc_full.append(sc_g)

    def small_params(l):
        return dict(cw=conf_full[l], cb=conf_dw_b[l][None], lg=conf_ln_g[l][None], lb=conf_ln_b[l][None],
                    pw=pool_w[l], ps=pool_scale[l][None], sc=sc_full[l], gg=gmlp_ln_g[l][None],
                    gb=gmlp_ln_b[l][None], ws=gmlp_ws[l], bst=gmlp_bs[l].T, g_mix=norm_mix[l][None],
                    g_mlp=norm_mlp[l][None], g_ple=norm_ple[l][None])

    xc = x.reshape(T, D)
    small_names = [n for n, _ in SMALL]

    def in_gradient_layout(n, shard, shape):
        if n not in CHANNEL_SHARDED:
            return shard
        return lax.dynamic_update_slice(jnp.zeros(shape, F32), shard, (0, 0, me * (W // NDEV)))

    small_state = [_pack([in_gradient_layout(n, src[n], shape) for n, shape in SMALL], NDEV * SMALL_ROWS)
                   for src in (weights, mom1, mom2)]
    xc, small_state = lax.optimization_barrier((xc, small_state))
    p_bf = p.reshape(DEPTH, T, 256).astype(BF16)
    h = _norm_first(xc, norm_mix[0][None])
    saved = []
    for l in range(DEPTH):
        g_next = norm_mix[l + 1][None] if l + 1 < DEPTH else norm_final[None]
        h, conf_g, sc_g = lax.optimization_barrier((h, conf_full[l], sc_full[l]))
        conf_full[l] = conf_g.transpose(1, 0, 2).reshape(CONF_K, W)
        sc_full[l] = sc_g.transpose(1, 0, 2).reshape(SC_K, W)
        xc, h, sv = _layer_fwd(xc, h, p_bf[l], gathered[l], small_params(l), g_next)
        saved.append(sv)

    dxc, dg_final, loss_part = _loss_head(xc, loss_target.reshape(T, D), norm_final[None])
    loss = lax.psum(loss_part[0, 0], ("x", "y", "c"))
    small_grads = [None] * DEPTH
    rcs = {n: [None] * DEPTH for n in BIG}
    pipeline = _GradientPipeline(c_idx, rcs)
    for l in reversed(range(DEPTH)):
        dxc, small_grads[l] = _layer_bwd(dxc, saved[l], gathered[l], small_params(l),
                                         lambda dw, names, value, l=l: pipeline.submit(dw, names, l, value),
                                         early_group=(l == 0))

    def adam_sharded(first_layer, n_layers, partial, tag, names=BIG):
        outs = {}
        for n in names:
            shp = (DEPTH,) + BIG_SHARD[n]
            outs[n] = _adam_sharded(rcs[n][first_layer:first_layer + n_layers], weights[n].reshape(shp),
                                    mom1[n].reshape(shp), mom2[n].reshape(shp), ADAM_ROWS[n],
                                    "adam_%s_%s" % (n, tag), first_layer, None if partial is None else partial[n])
        return outs

    stacked = {n: jnp.stack([small_grads[l][n] for l in range(DEPTH)]) for n, _ in SMALL if n != "norm_final"}
    stacked["norm_final"] = dg_final[0]
    packed = _pack([stacked[n] for n, _ in SMALL], NDEV * SMALL_ROWS).reshape(NCHIP, 2, SMALL_ROWS, D)
    (pair_small,), _ = _rs_exchange([packed], [], name="rs_pair_small")
    q_small = _pair_sum([packed], [pair_small], c_idx, name="rs_pairsum_small")
    dxc, upper, q_small = lax.optimization_barrier((dxc, {n: rcs[n][1:] for n in BIG}, q_small))
    _, (chips_small,) = _rs_exchange([], q_small, name="rs_chip_small", seq_id=6)
    dxc, upper = pipeline.finish((dxc, upper))
    for n in BIG:
        rcs[n][1:] = upper[n]
    partial = adam_sharded(1, DEPTH - 1, None, "upper")
    last = adam_sharded(0, 1, partial, "last", names=BIG[:1])
    partial = {n: partial[n] for n in BIG[1:]}
    last, partial, chips_small = lax.optimization_barrier((last, partial, chips_small))
    reduced_slot = _sum4(chips_small)
    reduced = _all_gather([reduced_slot], name="ag_small", seq_id=7)[0]
    small_full = dict(zip([n for n, _ in SMALL], _unpack(reduced, [s for _, s in SMALL])))
    grads, deltas, new_m, new_v = {}, {}, {}, {}
    direct, row = [], 0
    for n, shape in SMALL:
        if len(shape) == 1 or shape[-2] == DEPTH:
            direct.append((n, row, (1,) * (2 - len(shape)) + tuple(shape)))
        size = 1
        for s in shape:
            size *= s
        row += size // D
    (d_p, m_p, v_p), own_shape = _adam_packed(reduced.reshape(NDEV * SMALL_ROWS, D), *small_state, direct)
    small_shapes = [s for _, s in SMALL]

    def own_channels(n, full):
        return lax.dynamic_slice_in_dim(full, me * (W // NDEV), W // NDEV, axis=2) if n in CHANNEL_SHARDED else full

    for n, d_, m_, v_ in zip(small_names, _unpack(d_p, small_shapes), _unpack(m_p, small_shapes),
                             _unpack(v_p, small_shapes)):
        if n in own_shape:
            grads[n], deltas[n], new_m[n], new_v[n] = [a.reshape(weights[n].shape) for a in own_shape[n]]
        else:
            grads[n], deltas[n], new_m[n], new_v[n] = (own_channels(n, small_full[n]), own_channels(n, d_),
                                                       own_channels(n, m_), own_channels(n, v_))

    last.update(adam_sharded(0, 1, partial, "last", names=BIG[1:]))
    for n, (g_, d_, m_, v_) in last.items():
        full = weights[n].shape
        grads[n], deltas[n], new_m[n], new_v[n] = g_.reshape(full), d_.reshape(full), m_.reshape(full), v_.reshape(full)

    order = ("norm_mix", "w_in", "conf_dw", "conf_dw_b", "conf_ln_g", "conf_ln_b", "pool_w", "pool_scale", "sc_conv",
             "gmlp_ln_g", "gmlp_ln_b", "gmlp_ws", "gmlp_bs", "w_branch", "w_out", "norm_mlp", "w_up", "w_down",
             "norm_ple", "w_ple", "w_pleg", "norm_final")
    return (loss, dxc.reshape(1, T, D), *[grads[n] for n in order], *[deltas[n] for n in order],
            *[new_m[n] for n in order], *[new_v[n] for n in order])
```

```python
import functools

import jax
import jax.numpy as jnp
from jax import lax
from jax.experimental import pallas as pl
from jax.experimental.pallas import tpu as pltpu
from jax.experimental.pallas import tpu_sc as plsc

F32 = jnp.float32
BF16 = jnp.bfloat16

DEPTH = 4
T = 2048
D = 1024
W = 512
NDEV = 8
NCHIP = 4
EPS = 1e-6
CONF_K = 31
SC_K = 3
POOL_WINDOWS = (2, 4, 8, 16)
GW = 128
HB = 32
HA = 32
COLS_IN = 8192
MIX_COLS = 4096

ADAM_LR = 0.001
ADAM_B1 = 0.9
ADAM_B2 = 0.999
ADAM_EPS = 1e-08
ADAM_WD = 0.01
ADAM_STEP = 10

VMEM_LIMIT_BYTES = 56 * 1024 * 1024
MESH = pl.DeviceIdType.MESH


def _cp(*sem):
    return pltpu.CompilerParams(dimension_semantics=tuple(sem), vmem_limit_bytes=VMEM_LIMIT_BYTES)


def _sig(x):
    return jax.nn.sigmoid(x)


def _rms(x, g):
    r = lax.rsqrt(jnp.mean(x * x, axis=-1, keepdims=True) + EPS)
    return x * r * g


def _rms_bwd(dh, x, g, dres):
    r = lax.rsqrt(jnp.mean(x * x, axis=-1, keepdims=True) + EPS)
    xh = x * r
    u = dh * g
    dx = r * (u - xh * jnp.mean(u * xh, axis=-1, keepdims=True)) + dres
    dg = jnp.sum(dh * xh, axis=0, keepdims=True)
    return dx, dg


def _ln_stats(x):
    mu = jnp.mean(x, axis=-1, keepdims=True)
    xc = x - mu
    rstd = lax.rsqrt(jnp.mean(xc * xc, axis=-1, keepdims=True) + EPS)
    return xc * rstd, rstd


def _ln_bwd(dxh, xh, rstd):
    return rstd * (dxh - jnp.mean(dxh, axis=-1, keepdims=True) - xh * jnp.mean(dxh * xh, axis=-1, keepdims=True))


def _rowsum(x):
    return jnp.sum(x, axis=0, keepdims=True)


EPI_ROWS = 256


def _relu2_bf16(up):
    r = jnp.maximum(up.astype(F32), 0.0)
    return (r * r).astype(BF16)


def _mm(a, b3, *, mode, name, outs, trans_b=False, tm=512, tiles=(), params=(), epi=None, reds=(), a_pre=None,
        stream_first=False):
    t_, ka = a.shape
    nj, r, c = b3.shape
    kb, nb = (c, r) if trans_b else (r, c)
    nt = t_ // tm
    out_mode = mode == "out"
    full = mode == "full"
    assert trans_b or not full
    if out_mode:
        assert ka == kb and not reds
        grid = (nj, nt)
        a_map = lambda g0, g1: (g1, 0)
        b_map = lambda g0, g1: (g0, 0, 0)
        t_map = lambda g0, g1: (g1, g0)
        width = nj * nb
    else:
        assert ka == nj * kb
        grid = (nt, 1 if full else nj)
        a_map = lambda g0, g1: (g0, g1)
        b_map = lambda g0, g1: (g1, 0, 0)
        t_map = lambda g0, g1: (g0, 0)
        width = nb
    n_t, n_p, n_o, n_r = len(tiles), len(params), len(outs), len(reds)
    use_acc = (not out_mode) and nj > 1 and not full
    dims = (((1,), (1,)), ((), ())) if trans_b else (((1,), (0,)), ((), ()))

    def body(a_ref, b_ref, *rest):
        t_refs = rest[:n_t]
        p_refs = rest[n_t:n_t + n_p]
        o_refs = rest[n_t + n_p:n_t + n_p + n_o]
        r_refs = rest[n_t + n_p + n_o:n_t + n_p + n_o + n_r]
        i = pl.program_id(1 if out_mode else 0)
        a_val = a_ref[...] if a_pre is None else a_pre(a_ref[...])
        if full:
            b_all, b_sems = rest[-2 - stream_first], rest[-1 - stream_first]

            def weight_copies():
                return [pltpu.make_async_copy(b_ref.at[j], b_all.at[:, j * c:(j + 1) * c], b_sems.at[j])
                        for j in range(nj)]

            if stream_first:
                part = rest[-1]

                @pl.when(i == 0)
                def _():
                    cps = weight_copies()
                    for cp in cps:
                        cp.start()
                    acc = None
                    for j, cp in enumerate(cps):
                        cp.wait()
                        term = lax.dot_general(a_val[:, j * c:(j + 1) * c], b_all[:, j * c:(j + 1) * c], dims,
                                               preferred_element_type=F32)
                        acc = term if acc is None else acc + term
                    part[...] = acc

                @pl.when(i > 0)
                def _():
                    part[...] = lax.dot_general(a_val, b_all[...], dims, preferred_element_type=F32)
            else:
                @pl.when(i == 0)
                def _():
                    cps = weight_copies()
                    for cp in cps:
                        cp.start()
                    for cp in cps:
                        cp.wait()

                part = lax.dot_general(a_val, b_all[...], dims, preferred_element_type=F32)
        else:
            part = lax.dot_general(a_val, b_ref[...], dims, preferred_element_type=F32)

        def finish(acc_rows):
            totals = [None] * n_r
            for r0 in range(0, tm, min(tm, EPI_ROWS)):
                rows = slice(r0, r0 + min(tm, EPI_ROWS))
                if epi is None:
                    res, rr = (acc_rows(rows),), ()
                else:
                    res, rr = epi(acc_rows(rows), [t[rows, :] for t in t_refs], [p[...] for p in p_refs])
                for o_ref, val in zip(o_refs, res):
                    o_ref[rows, :] = val.astype(o_ref.dtype)
                totals = [val if tot is None else tot + val for tot, val in zip(totals, rr)]
            for r_ref, val in zip(r_refs, totals):
                @pl.when(i == 0)
                def _():
                    r_ref[...] = val

                @pl.when(i > 0)
                def _():
                    r_ref[...] += val

        if use_acc:
            acc_ref = rest[-1]
            j = pl.program_id(1)

            @pl.when(j == 0)
            def _():
                acc_ref[...] = part

            @pl.when(jnp.logical_and(j > 0, j < nj - 1))
            def _():
                acc_ref[...] += part

            @pl.when(j == nj - 1)
            def _():
                finish(lambda rows: acc_ref[rows, :] + part[rows])
        else:
            finish(lambda rows: part[rows, :])

    const2 = lambda g0, g1: (0, 0)
    if full:
        in_specs = [pl.BlockSpec((tm, ka), a_map), pl.BlockSpec(memory_space=pl.ANY)]
        scratch = [pltpu.VMEM((r, nj * c), b3.dtype), pltpu.SemaphoreType.DMA((nj,))]
        scratch += [pltpu.VMEM((tm, nb), F32)] if stream_first else []
    else:
        in_specs = [pl.BlockSpec((tm, kb), a_map), pl.BlockSpec((None, r, c), b_map)]
        scratch = [pltpu.VMEM((tm, nb), F32)] if use_acc else []
    in_specs += [pl.BlockSpec((tm, t.shape[1] // nj if out_mode else t.shape[1]), t_map) for t in tiles]
    in_specs += [pl.BlockSpec(p.shape, lambda g0, g1, nd=p.ndim: (0,) * nd) for p in params]
    out_specs = [pl.BlockSpec((tm, nb), t_map) for _ in outs] + [pl.BlockSpec((1, w), const2) for w in reds]
    out_shape = [jax.ShapeDtypeStruct((t_, width), dt) for dt in outs]
    out_shape += [jax.ShapeDtypeStruct((1, w), F32) for w in reds]
    res = pl.pallas_call(
        body, name=name, grid=grid, in_specs=in_specs, out_specs=out_specs, out_shape=out_shape,
        scratch_shapes=scratch, compiler_params=_cp("arbitrary", "arbitrary"),
    )(a, b3, *tiles, *params)
    return res


def _mm_tn(a, g, *, nj, split, name, out_dtype=BF16, a_pre=None):
    t_ = a.shape[0]
    if split == "col":
        r, c = a.shape[1], g.shape[1] // nj
        a_spec = pl.BlockSpec((t_, r), lambda j: (0, 0))
        g_spec = pl.BlockSpec((t_, c), lambda j: (0, j))
    else:
        r, c = a.shape[1] // nj, g.shape[1]
        a_spec = pl.BlockSpec((t_, r), lambda j: (0, j))
        g_spec = pl.BlockSpec((t_, c), lambda j: (0, 0))

    def body(a_ref, g_ref, o_ref):
        a_val = a_ref[...] if a_pre is None else a_pre(a_ref[...])
        o_ref[...] = lax.dot_general(a_val, g_ref[...], (((0,), (0,)), ((), ())),
                                     preferred_element_type=F32).astype(o_ref.dtype)

    return pl.pallas_call(
        body, name=name, grid=(nj,), in_specs=[a_spec, g_spec],
        out_specs=pl.BlockSpec((None, r, c), lambda j: (j, 0, 0)),
        out_shape=jax.ShapeDtypeStruct((nj, r, c), out_dtype),
        compiler_params=_cp("arbitrary"),
    )(a, g)


def _epi_res_norm(acc, tiles, params):
    x_new = tiles[0] + acc
    return (x_new, _rms(x_new, params[0])), ()


def _epi_ple(acc, tiles, params):
    x_old, p_tile = tiles
    g_next, w_ple8 = params
    pe = jnp.concatenate([jnp.dot(p_tile, w_ple8[j], preferred_element_type=F32) for j in range(NDEV)], axis=1)
    x_new = x_old + pe * _sig(acc)
    return (x_new, acc, _rms(x_new, g_next), pe), ()


def _epi_rms_bwd(acc, tiles, params):
    dx, dg = _rms_bwd(acc, tiles[0], params[0], tiles[1])
    return (dx, dx), (dg,)


def _epi_dup(acc, tiles, params):
    return (acc * (2.0 * jnp.maximum(tiles[0].astype(F32), 0.0)),), ()


def _tri_mask():
    row = lax.broadcasted_iota(jnp.int32, (GW, GW), 0)
    col = lax.broadcasted_iota(jnp.int32, (GW, GW), 1)
    return row >= col


def _small_specs(sp_list):
    return [pl.BlockSpec(p.shape, (lambda i: (0, 0)) if p.ndim == 2 else (lambda i: (0, 0, 0))) for p in sp_list]


SUBLANES = 8


def _tap_sum(src, w_ref, taps, rows, stage):
    groups = {}
    for off, k in taps:
        groups.setdefault(off % SUBLANES, []).append((off - off % SUBLANES, k))
    out = None
    for res, members in sorted(groups.items()):
        n = rows if res == 0 else rows + SUBLANES
        part = None
        for base, k in members:
            term = w_ref[k:k + 1, :] * src[pl.ds(base, n), :]
            part = term if part is None else part + term
        if res:
            stage[0:n, :] = part
            part = stage[pl.ds(res, rows), :]
        out = part if out is None else out + part
    return out


def _tap_grads(grad, src, offsets, rows, stage, out_ref):
    pad = SUBLANES
    stage[0:pad, :] = jnp.zeros((pad, grad.shape[1]), F32)
    stage[pad:pad + rows, :] = grad
    stage[pad + rows:2 * pad + rows, :] = jnp.zeros((pad, grad.shape[1]), F32)
    groups = {}
    for k, off in enumerate(offsets):
        groups.setdefault(off % SUBLANES, []).append((off - off % SUBLANES, k))
    for res, members in sorted(groups.items()):
        shifted = stage[pl.ds(pad - res, rows + pad), :]
        for base, k in members:
            out_ref[k:k + 1, :] += _rowsum(shifted * src[pl.ds(base, rows + pad), :])


def _mixer_params(sp):
    return [sp["cw"], sp["cb"], sp["lg"], sp["lb"], sp["pw"], sp["ps"], sp["sc"], sp["gg"], sp["gb"], sp["ws"], sp["bst"]]


def _mixer_fwd(proj, sp, tm=256):
    nt = T // tm
    per = tm // HB

    conv_taps = [(HB - (CONF_K - 1) + k, k) for k in range(CONF_K)]

    def body(main_ref, halo_ref, cw, cb, lg, lb, pw, ps, sc, gg, gb, ws, bst, y_ref, ca_ref, ext, stage):
        i = pl.program_id(0)
        keep = (i > 0).astype(F32)

        def mcol(c0):
            return main_ref[:, c0:c0 + W].astype(F32)

        def hcol(c0):
            return halo_ref[:, c0:c0 + W].astype(F32)

        ext[0:HB, :] = hcol(0) * _sig(hcol(W)) * keep
        ext[HB:HB + tm, :] = mcol(0) * _sig(mcol(W))
        ca = (_tap_sum(ext, cw, conv_taps, tm, stage) + cb[...]).astype(BF16)
        ca_ref[...] = ca
        xh, _ = _ln_stats(ca.astype(F32))
        n = xh * lg[...] + lb[...]
        y_ref[:, 0:W] = (n * _sig(n)).astype(BF16)

        pin = mcol(1024)
        ext[0:HB, :] = hcol(1024) * keep
        ext[HB:HB + tm, :] = pin
        pos = (i * tm + lax.broadcasted_iota(jnp.int32, (tm, 1), 0) + 1).astype(F32)
        for g, w in enumerate(POOL_WINDOWS):
            lo = g * GW
            s = ext[pl.ds(HB, tm), lo:lo + GW]
            for j in range(1, w):
                s = s + ext[pl.ds(HB - j, tm), lo:lo + GW]
            pooled = s / jnp.minimum(pos, float(w)) - pin[:, lo:lo + GW]
            mixed = jnp.dot(pooled.astype(BF16), pw[g].astype(BF16), preferred_element_type=F32)
            y_ref[:, W + lo:W + lo + GW] = (mixed * ps[:, lo:lo + GW]).astype(BF16)

        ext[0:HB, :] = hcol(2048) * hcol(2560) * keep
        ext[HB:HB + tm, :] = mcol(2048) * mcol(2560)
        cv = sc[0:1, :] * ext[pl.ds(HB - 2, tm), :]
        cv = cv + sc[1:2, :] * ext[pl.ds(HB - 1, tm), :]
        cv = cv + sc[2:3, :] * ext[pl.ds(HB, tm), :]
        y_ref[:, 2 * W:3 * W] = (mcol(1536) * cv).astype(BF16)

        vh, _ = _ln_stats(mcol(3584))
        vn = (vh * gg[...] + gb[...]).astype(BF16)
        u = mcol(3072)
        tri = _tri_mask()
        for g in range(4):
            lo = g * GW
            wm = jnp.where(tri, ws[g], 0.0).astype(BF16)
            for c in range(tm // GW):
                r0 = c * GW
                sg = jnp.dot(wm, vn[r0:r0 + GW, lo:lo + GW], preferred_element_type=F32) + bst[:, g:g + 1]
                y_ref[r0:r0 + GW, 3 * W + lo:3 * W + lo + GW] = (u[r0:r0 + GW, lo:lo + GW] * sg).astype(BF16)

    plist = _mixer_params(sp)
    in_specs = [pl.BlockSpec((tm, MIX_COLS), lambda i: (i, 0)),
                pl.BlockSpec((HB, MIX_COLS), lambda i: (jnp.maximum(i * per - 1, 0), 0))]
    in_specs += _small_specs(plist)
    return pl.pallas_call(
        body, name="f_mixers", grid=(nt,), in_specs=in_specs,
        out_specs=[pl.BlockSpec((tm, 4 * W), lambda i: (i, 0)), pl.BlockSpec((tm, W), lambda i: (i, 0))],
        out_shape=[jax.ShapeDtypeStruct((T, 4 * W), BF16), jax.ShapeDtypeStruct((T, W), BF16)],
        scratch_shapes=[pltpu.VMEM((HB + tm, W), F32), pltpu.VMEM((tm + SUBLANES, W), F32)],
        compiler_params=_cp("arbitrary"),
    )(proj, proj, *plist)


def _assemble_wb(wb8_ref, wbf_ref):
    for k in range(4):
        for j in range(NDEV):
            wbf_ref[k, :, j * GW:(j + 1) * GW] = wb8_ref[j, k]


def _merge_fwd(y, proj, wb8, tm=256):
    nt = T // tm

    def body(y_ref, gate_ref, wb8_ref, z_ref, s_ref, m_ref, wbf):
        @pl.when(pl.program_id(0) == 0)
        def _():
            _assemble_wb(wb8_ref, wbf)

        m = jnp.zeros((tm, D), F32)
        for k in range(4):
            zk = jnp.dot(y_ref[:, k * W:(k + 1) * W], wbf[k], preferred_element_type=F32)
            z_ref[:, k * D:(k + 1) * D] = zk.astype(BF16)
            s = _sig(gate_ref[:, k * D:(k + 1) * D].astype(F32))
            s_ref[:, k * D:(k + 1) * D] = s.astype(BF16)
            m = m + s * zk
        m_ref[...] = m.astype(BF16)

    return pl.pallas_call(
        body, name="f_merge", grid=(nt,),
        in_specs=[pl.BlockSpec((tm, 4 * W), lambda i: (i, 0)),
                  pl.BlockSpec((tm, 4 * D), lambda i: (i, 1)),
                  pl.BlockSpec(wb8.shape, lambda i: (0, 0, 0, 0))],
        out_specs=[pl.BlockSpec((tm, 4 * D), lambda i: (i, 0)), pl.BlockSpec((tm, 4 * D), lambda i: (i, 0)),
                   pl.BlockSpec((tm, D), lambda i: (i, 0))],
        out_shape=[jax.ShapeDtypeStruct((T, 4 * D), BF16), jax.ShapeDtypeStruct((T, 4 * D), BF16),
                   jax.ShapeDtypeStruct((T, D), BF16)],
        scratch_shapes=[pltpu.VMEM((4, W, D), BF16)],
        compiler_params=_cp("arbitrary"),
    )(y, proj, wb8)


def _merge_bwd(dm, z, gates, y, wb8, tm=256):
    nt = T // tm

    def body(dm_ref, z_ref, gate_ref, y_ref, wb8_ref, dp_ref, dy_ref, dwb_ref, wbf, acc):
        i = pl.program_id(0)

        @pl.when(i == 0)
        def _():
            _assemble_wb(wb8_ref, wbf)

        dmv = dm_ref[...].astype(F32)
        for k in range(4):
            s = gate_ref[:, k * D:(k + 1) * D].astype(F32)
            dzk = (dmv * s).astype(BF16)
            dp_ref[:, k * D:(k + 1) * D] = (dmv * z_ref[:, k * D:(k + 1) * D].astype(F32) * s * (1.0 - s)).astype(BF16)
            dyk = lax.dot_general(dzk, wbf[k], (((1,), (1,)), ((), ())), preferred_element_type=F32)
            dy_ref[:, k * W:(k + 1) * W] = dyk.astype(BF16)
            part = lax.dot_general(y_ref[:, k * W:(k + 1) * W], dzk, (((0,), (0,)), ((), ())),
                                   preferred_element_type=F32)

            @pl.when(i == 0)
            def _():
                acc[k] = part

            @pl.when(i > 0)
            def _():
                acc[k] += part

        @pl.when(i == nt - 1)
        def _():
            for k in range(4):
                for j in range(NDEV):
                    dwb_ref[j, k] = acc[k, :, j * GW:(j + 1) * GW].astype(BF16)

    return pl.pallas_call(
        body, name="b_merge", grid=(nt,),
        in_specs=[pl.BlockSpec((tm, D), lambda i: (i, 0)),
                  pl.BlockSpec((tm, 4 * D), lambda i: (i, 0)),
                  pl.BlockSpec((tm, 4 * D), lambda i: (i, 0)),
                  pl.BlockSpec((tm, 4 * W), lambda i: (i, 0)),
                  pl.BlockSpec(wb8.shape, lambda i: (0, 0, 0, 0))],
        out_specs=[pl.BlockSpec((tm, 4 * D), lambda i: (i, 1)),
                   pl.BlockSpec((tm, 4 * W), lambda i: (i, 0)),
                   pl.BlockSpec(wb8.shape, lambda i: (0, 0, 0, 0))],
        out_shape=[jax.ShapeDtypeStruct((T, COLS_IN), BF16),
                   jax.ShapeDtypeStruct((T, 4 * W), BF16),
                   jax.ShapeDtypeStruct(wb8.shape, BF16)],
        scratch_shapes=[pltpu.VMEM((4, W, D), BF16), pltpu.VMEM((4, W, D), F32)],
        compiler_params=_cp("arbitrary"),
    )(dm, z, gates, y, wb8)


def _mixer_bwd(proj, ca_saved, dy, dproj, sp, tm=512):
    nt = T // tm
    per = tm // HB
    ne = tm + HA
    last_blk = T // HA - 1
    conv_taps = [(HB - (CONF_K - 1) + k, k) for k in range(CONF_K)]

    def body(main_ref, hb_ref, ha_ref, ca_ref, cah_ref, dy_ref, dyh_ref, cw, cb, lg, lb, pw, ps, sc, gg, gb, ws, bst,
             dp_any, dp_ref, dcw_ref, dsc_ref, vec_ref, dpw_ref, dws_ref, dbs_ref, e1, e2, e3, stage):
        del dp_any, cb
        i = pl.program_id(0)
        keep_b = (i > 0).astype(F32)
        keep_a = (i < nt - 1).astype(F32)

        @pl.when(i == 0)
        def _():
            dcw_ref[...] = jnp.zeros_like(dcw_ref)
            dsc_ref[...] = jnp.zeros_like(dsc_ref)
            vec_ref[...] = jnp.zeros_like(vec_ref)
            dpw_ref[...] = jnp.zeros_like(dpw_ref)
            dws_ref[...] = jnp.zeros_like(dws_ref)
            dbs_ref[...] = jnp.zeros_like(dbs_ref)

        def mcol(c0):
            return main_ref[:, c0:c0 + W].astype(F32)

        def hbcol(c0):
            return hb_ref[:, c0:c0 + W].astype(F32)

        def hacol(c0):
            return ha_ref[:, c0:c0 + W].astype(F32)

        def load_dy(c0):
            e2[0:tm, :] = dy_ref[:, c0:c0 + W].astype(F32)
            e2[tm:ne, :] = dyh_ref[:, c0:c0 + W].astype(F32) * keep_a

        a = mcol(0)
        sa = _sig(mcol(W))
        e1[0:HB, :] = hbcol(0) * _sig(hbcol(W)) * keep_b
        e1[HB:HB + tm, :] = a * sa
        e1[HB + tm:HB + tm + SUBLANES, :] = jnp.zeros((SUBLANES, W), F32)
        e2[0:tm, :] = ca_ref[...].astype(F32)
        e2[tm:ne, :] = cah_ref[...].astype(F32)
        xh, rstd = _ln_stats(e2[0:ne, :])
        nn = xh * lg[...] + lb[...]
        s = _sig(nn)
        load_dy(0)
        dn = e2[0:ne, :] * (s * (1.0 + nn * (1.0 - s)))
        vec_ref[1:2, :] += _rowsum(dn[0:tm] * xh[0:tm])
        vec_ref[2:3, :] += _rowsum(dn[0:tm])
        dca = _ln_bwd(dn * lg[...], xh, rstd)
        e3[0:ne, :] = dca
        dmain = dca[0:tm]
        vec_ref[0:1, :] += _rowsum(dmain)
        _tap_grads(dmain, e1, [off for off, _ in conv_taps], tm, stage, dcw_ref)
        dglu = _tap_sum(e3, cw, [(CONF_K - 1 - k, k) for k in range(CONF_K)], tm, stage)
        dp_ref[:, 0:W] = (dglu * sa).astype(BF16)
        dp_ref[:, W:2 * W] = (dglu * a * sa * (1.0 - sa)).astype(BF16)

        pin = mcol(1024)
        e1[0:HB, :] = hbcol(1024) * keep_b
        e1[HB:HB + tm, :] = pin
        load_dy(W)
        dyb = e2[0:ne, :]
        pos_m = (i * tm + lax.broadcasted_iota(jnp.int32, (tm, 1), 0) + 1).astype(F32)
        pos_e = (i * tm + lax.broadcasted_iota(jnp.int32, (ne, 1), 0) + 1).astype(F32)
        for g, w in enumerate(POOL_WINDOWS):
            lo = g * GW
            acc = e1[pl.ds(HB, tm), lo:lo + GW]
            for j in range(1, w):
                acc = acc + e1[pl.ds(HB - j, tm), lo:lo + GW]
            pooled = (acc / jnp.minimum(pos_m, float(w)) - pin[:, lo:lo + GW]).astype(BF16)
            pwb = pw[g].astype(BF16)
            mixed = jnp.dot(pooled, pwb, preferred_element_type=F32)
            dyb_g = dyb[:, lo:lo + GW]
            vec_ref[3:4, lo:lo + GW] += _rowsum(dyb_g[0:tm] * mixed)
            dmb = (dyb_g * ps[:, lo:lo + GW]).astype(BF16)
            dpw_ref[g] += lax.dot_general(pooled, dmb[0:tm], (((0,), (0,)), ((), ())), preferred_element_type=F32)
            dpool = lax.dot_general(dmb, pwb, (((1,), (1,)), ((), ())), preferred_element_type=F32)
            e3[0:ne, lo:lo + GW] = dpool / jnp.minimum(pos_e, float(w))
            back = e3[pl.ds(0, tm), lo:lo + GW]
            for j in range(1, w):
                back = back + e3[pl.ds(j, tm), lo:lo + GW]
            dp_ref[:, 1024 + lo:1024 + lo + GW] = (back - dpool[0:tm]).astype(BF16)

        cg = mcol(2048)
        hx = mcol(2560)
        e1[0:HB, :] = hbcol(2048) * hbcol(2560) * keep_b
        e1[HB:HB + tm, :] = cg * hx
        load_dy(2 * W)
        dyc = e2[0:tm, :]
        dconv = dyc * mcol(1536)
        e3[0:tm, :] = dconv
        e3[tm:ne, :] = e2[tm:ne, :] * hacol(1536)
        cv = sc[0:1, :] * e1[pl.ds(HB - 2, tm), :]
        for k in range(1, SC_K):
            cv = cv + sc[k:k + 1, :] * e1[pl.ds(HB - 2 + k, tm), :]
        dp_ref[:, 1536:2048] = (dyc * cv).astype(BF16)
        for k in range(SC_K):
            dsc_ref[k:k + 1, :] += _rowsum(dconv * e1[pl.ds(HB - 2 + k, tm), :])
        dq = sc[0:1, :] * e3[pl.ds(2, tm), :]
        for k in range(1, SC_K):
            dq = dq + sc[k:k + 1, :] * e3[pl.ds(2 - k, tm), :]
        dp_ref[:, 2048:2560] = (dq * hx).astype(BF16)
        dp_ref[:, 2560:3072] = (dq * cg).astype(BF16)

        u = mcol(3072)
        vh, vr = _ln_stats(mcol(3584))
        vn = (vh * gg[...] + gb[...]).astype(BF16)
        dyd = dy_ref[:, 3 * W:4 * W].astype(F32)
        tri = _tri_mask()
        for g in range(4):
            lo = g * GW
            wm = jnp.where(tri, ws[g], 0.0).astype(BF16)
            dws_g = jnp.zeros((GW, GW), F32)
            dbs_g = jnp.zeros((GW, 1), F32)
            for c in range(tm // GW):
                r0 = c * GW
                blk = vn[r0:r0 + GW, lo:lo + GW]
                sg = jnp.dot(wm, blk, preferred_element_type=F32) + bst[:, g:g + 1]
                dyd_b = dyd[r0:r0 + GW, lo:lo + GW]
                dp_ref[r0:r0 + GW, 3072 + lo:3072 + lo + GW] = (dyd_b * sg).astype(BF16)
                dsg = dyd_b * u[r0:r0 + GW, lo:lo + GW]
                dsgb = dsg.astype(BF16)
                dbs_g = dbs_g + jnp.sum(dsg, axis=-1, keepdims=True)
                dws_g = dws_g + lax.dot_general(dsgb, blk, (((1,), (1,)), ((), ())), preferred_element_type=F32)
                e1[r0:r0 + GW, lo:lo + GW] = lax.dot_general(wm, dsgb, (((0,), (0,)), ((), ())),
                                                             preferred_element_type=F32)
            dws_ref[g] += jnp.where(tri, dws_g, 0.0)
            dbs_ref[g] += jnp.broadcast_to(dbs_g, (GW, GW))
        dvn = e1[0:tm, :]
        vec_ref[4:5, :] += _rowsum(dvn * vh)
        vec_ref[5:6, :] += _rowsum(dvn)
        dp_ref[:, 3584:4096] = _ln_bwd(dvn * gg[...], vh, vr).astype(BF16)

    plist = _mixer_params(sp)
    in_specs = [pl.BlockSpec((tm, MIX_COLS), lambda i: (i, 0)),
                pl.BlockSpec((HB, MIX_COLS), lambda i: (jnp.maximum(i * per - 1, 0), 0)),
                pl.BlockSpec((HA, MIX_COLS), lambda i: (jnp.minimum((i + 1) * per, last_blk), 0)),
                pl.BlockSpec((tm, W), lambda i: (i, 0)),
                pl.BlockSpec((HA, W), lambda i: (jnp.minimum((i + 1) * per, last_blk), 0)),
                pl.BlockSpec((tm, 4 * W), lambda i: (i, 0)),
                pl.BlockSpec((HA, 4 * W), lambda i: (jnp.minimum((i + 1) * per, last_blk), 0))]
    in_specs += _small_specs(plist)
    in_specs += [pl.BlockSpec(memory_space=pl.ANY)]
    z2 = lambda i: (0, 0)
    z3 = lambda i: (0, 0, 0)
    out_specs = [pl.BlockSpec((tm, MIX_COLS), lambda i: (i, 0)),
                 pl.BlockSpec((32, W), z2), pl.BlockSpec((8, W), z2), pl.BlockSpec((8, W), z2),
                 pl.BlockSpec((4, GW, GW), z3), pl.BlockSpec((4, GW, GW), z3), pl.BlockSpec((4, GW, GW), z3)]
    out_shape = [jax.ShapeDtypeStruct((T, COLS_IN), BF16),
                 jax.ShapeDtypeStruct((32, W), F32), jax.ShapeDtypeStruct((8, W), F32),
                 jax.ShapeDtypeStruct((8, W), F32),
                 jax.ShapeDtypeStruct((4, GW, GW), F32), jax.ShapeDtypeStruct((4, GW, GW), F32),
                 jax.ShapeDtypeStruct((4, GW, GW), F32)]
    n_in = 7 + len(plist)
    return pl.pallas_call(
        body, name="b_mixers", grid=(nt,), in_specs=in_specs, out_specs=out_specs, out_shape=out_shape,
        scratch_shapes=[pltpu.VMEM((HB + ne, W), F32), pltpu.VMEM((ne, W), F32), pltpu.VMEM((ne, W), F32),
                        pltpu.VMEM((ne + SUBLANES, W), F32)],
        input_output_aliases={n_in: 0},
        compiler_params=_cp("arbitrary"),
    )(proj, proj, proj, ca_saved, ca_saved, dy, dy, *plist, dproj)


def _norm_first(x, g, tm=512):
    def body(x_ref, g_ref, o_ref):
        o_ref[...] = _rms(x_ref[...], g_ref[...]).astype(BF16)

    return pl.pallas_call(
        body, name="f_norm0", grid=(T // tm,),
        in_specs=[pl.BlockSpec((tm, D), lambda i: (i, 0)), pl.BlockSpec((1, D), lambda i: (0, 0))],
        out_specs=pl.BlockSpec((tm, D), lambda i: (i, 0)),
        out_shape=jax.ShapeDtypeStruct((T, D), BF16), compiler_params=_cp("arbitrary"),
    )(x, g)


def _loss_head(x, target, g, tm=256):
    def body(x_ref, t_ref, g_ref, dx_ref, dg_ref, loss_ref):
        i = pl.program_id(0)
        x = x_ref[...]
        r = lax.rsqrt(jnp.mean(x * x, axis=-1, keepdims=True) + EPS)
        xh = x * r
        gv = g_ref[...]
        e = xh * gv - t_ref[...]
        dyv = e * (1.0 / D)
        part = jnp.sum(_rowsum(e * e), axis=-1, keepdims=True) * (0.5 / D)
        u = dyv * gv
        dx_ref[...] = r * (u - xh * jnp.mean(u * xh, axis=-1, keepdims=True))
        dgp = _rowsum(dyv * xh)

        @pl.when(i == 0)
        def _():
            dg_ref[...] = dgp
            loss_ref[...] = jnp.broadcast_to(part, (1, GW))

        @pl.when(i > 0)
        def _():
            dg_ref[...] += dgp
            loss_ref[...] += jnp.broadcast_to(part, (1, GW))

    return pl.pallas_call(
        body, name="loss_head", grid=(T // tm,),
        in_specs=[pl.BlockSpec((tm, D), lambda i: (i, 0)), pl.BlockSpec((tm, D), lambda i: (i, 0)),
                  pl.BlockSpec((1, D), lambda i: (0, 0))],
        out_specs=[pl.BlockSpec((tm, D), lambda i: (i, 0)), pl.BlockSpec((1, D), lambda i: (0, 0)),
                   pl.BlockSpec((1, GW), lambda i: (0, 0))],
        out_shape=[jax.ShapeDtypeStruct((T, D), F32), jax.ShapeDtypeStruct((1, D), F32),
                   jax.ShapeDtypeStruct((1, GW), F32)],
        compiler_params=_cp("arbitrary"),
    )(x, target, g)


def _out_bwd(dx2b, merged, w_out8, tm=512):
    nt = T // tm

    def body(dx_ref, mg_ref, w_ref, dm_ref, dw_ref, acc):
        i = pl.program_id(0)
        dx = dx_ref[...]
        dm_ref[...] = lax.dot_general(dx, w_ref[...], (((1,), (1,)), ((), ())),
                                      preferred_element_type=F32).astype(BF16)
        part = lax.dot_general(mg_ref[...], dx, (((0,), (0,)), ((), ())), preferred_element_type=F32)

        @pl.when(i == 0)
        def _():
            acc[...] = part

        @pl.when(i > 0)
        def _():
            acc[...] += part

        @pl.when(i == nt - 1)
        def _():
            for j in range(NDEV):
                dw_ref[j] = acc[j * GW:(j + 1) * GW, :].astype(BF16)

    tile = pl.BlockSpec((tm, D), lambda i: (i, 0))
    return pl.pallas_call(
        body, name="b_out", grid=(nt,),
        in_specs=[tile, tile, pl.BlockSpec((D, D), lambda i: (0, 0))],
        out_specs=[tile, pl.BlockSpec((NDEV, GW, D), lambda i: (0, 0, 0))],
        out_shape=[jax.ShapeDtypeStruct((T, D), BF16), jax.ShapeDtypeStruct((NDEV, GW, D), BF16)],
        scratch_shapes=[pltpu.VMEM((D, D), F32)],
        compiler_params=_cp("arbitrary"),
    )(dx2b, merged, w_out8.reshape(D, D))


def _ple_bwd(dx4, sv, w_pleg8, g_ple, tm=512):
    nt = T // tm
    ple_dim = sv["p"].shape[1]

    def body(dx_ref, gl_ref, pe_ref, x_ref, h_ref, p_ref, g_ref, wg_ref,
             dx3_ref, dx3b_ref, dg_ref, dwg_ref, dwp_ref, acc_g, acc_p):
        i = pl.program_id(0)
        d = dx_ref[...]
        s = _sig(gl_ref[...].astype(F32))
        dpe = (d * s).astype(BF16)
        dgl = (d * pe_ref[...].astype(F32) * s * (1.0 - s)).astype(BF16)
        dh = lax.dot_general(dgl, wg_ref[...], (((1,), (1,)), ((), ())), preferred_element_type=F32)
        dx, dgp = _rms_bwd(dh, x_ref[...], g_ref[...], d)
        dx3_ref[...] = dx
        dx3b_ref[...] = dx.astype(BF16)
        part_g = lax.dot_general(h_ref[...], dgl, (((0,), (0,)), ((), ())), preferred_element_type=F32)
        part_p = lax.dot_general(p_ref[...], dpe, (((0,), (0,)), ((), ())), preferred_element_type=F32)

        @pl.when(i == 0)
        def _():
            dg_ref[...] = dgp
            acc_g[...] = part_g
            acc_p[...] = part_p

        @pl.when(i > 0)
        def _():
            dg_ref[...] += dgp
            acc_g[...] += part_g
            acc_p[...] += part_p

        @pl.when(i == nt - 1)
        def _():
            for j in range(NDEV):
                dwg_ref[j] = acc_g[j * GW:(j + 1) * GW, :].astype(BF16)
                dwp_ref[j] = acc_p[:, j * GW:(j + 1) * GW].astype(BF16)

    tile = lambda w: pl.BlockSpec((tm, w), lambda i: (i, 0))
    const = lambda shp: pl.BlockSpec(shp, lambda i: (0,) * len(shp))
    return pl.pallas_call(
        body, name="b_ple", grid=(nt,),
        in_specs=[tile(D), tile(D), tile(D), tile(D), tile(D), tile(ple_dim), const((1, D)), const((D, D))],
        out_specs=[tile(D), tile(D), const((1, D)), const((NDEV, GW, D)), const((NDEV, ple_dim, GW))],
        out_shape=[jax.ShapeDtypeStruct((T, D), F32), jax.ShapeDtypeStruct((T, D), BF16),
                   jax.ShapeDtypeStruct((1, D), F32), jax.ShapeDtypeStruct((NDEV, GW, D), BF16),
                   jax.ShapeDtypeStruct((NDEV, ple_dim, GW), BF16)],
        scratch_shapes=[pltpu.VMEM((D, D), F32), pltpu.VMEM((ple_dim, D), F32)],
        compiler_params=_cp("arbitrary"),
    )(dx4, sv["gl"], sv["pe"], sv["x3"], sv["h3"], sv["p"], g_ple, w_pleg8.reshape(D, D))


def _layer_fwd(x, h1, p_bf, gw, sp, g_next):
    proj, = _mm(h1, gw["w_in"], mode="out", name="f_proj", outs=[BF16], tm=T)
    y, ca = _mixer_fwd(proj, sp)
    z, gates, merged = _merge_fwd(y, proj, gw["w_branch"])
    x2, h2 = _mm(merged, gw["w_out"].reshape(1, D, D), mode="acc", name="f_out", outs=[F32, BF16], tm=T // 2,
                 tiles=[x], params=[sp["g_mlp"]], epi=_epi_res_norm)
    up, = _mm(h2, gw["w_up"], mode="out", name="f_up", outs=[BF16], tm=T)
    x3, h3 = _mm(up, gw["w_down"].reshape(1, 4 * D, D), mode="acc", name="f_down", outs=[F32, BF16], tm=T // 4,
                 tiles=[x2], params=[sp["g_ple"]], epi=_epi_res_norm, a_pre=_relu2_bf16)
    x4, gl, hn, pe = _mm(h3, gw["w_pleg"].reshape(1, D, D), mode="acc", name="f_gate", tm=T // 2,
                         outs=[F32, BF16, BF16, BF16], tiles=[x3, p_bf], params=[g_next, gw["w_ple"]], epi=_epi_ple)
    saved = dict(x=x, h1=h1, proj=proj, y=y, ca=ca, z=z, gates=gates, merged=merged, x2=x2, h2=h2, up=up, x3=x3, h3=h3,
                 pe=pe, gl=gl, p=p_bf)
    return x4, hn, saved


def _layer_bwd(dx4, sv, gw, sp, submit, early_group=False):
    dw = {}
    dx3, dx3b, dg_ple, dw["w_pleg"], dw["w_ple"] = _ple_bwd(dx4, sv, gw["w_pleg"], sp["g_ple"])
    dup, = _mm(dx3b, gw["w_down"], mode="out", trans_b=True, name="b_dact", outs=[BF16], tm=T,
               tiles=[sv["up"]], epi=_epi_dup)
    dw["w_down"] = _mm_tn(sv["up"], dx3b, nj=NDEV, split="row", name="b_dw_down", a_pre=_relu2_bf16)
    dw["w_up"] = _mm_tn(sv["h2"], dup, nj=NDEV, split="col", name="b_dw_up")
    if early_group:
        dw["w_up"], dup = lax.optimization_barrier((dw["w_up"], dup))
        dup = submit(dw, ("w_up", "w_down", "w_ple", "w_pleg"), dup)
    dx2, dx2b, dg_mlp = _mm(dup, gw["w_up"], mode="full", trans_b=True, name="b_dh2", tm=T // 8, stream_first=True,
                            outs=[F32, BF16], tiles=[sv["x2"], dx3], params=[sp["g_mlp"]], epi=_epi_rms_bwd, reds=[D])
    dm, dw["w_out"] = _out_bwd(dx2b, sv["merged"], gw["w_out"])
    dproj, dy, dw["w_branch"] = _merge_bwd(dm, sv["z"], sv["gates"], sv["y"], gw["w_branch"])
    dy = submit(dw, ("w_branch", "w_out") if early_group else BIG[1:], dy)
    dproj, dcw, dsc, vec, dpw, dws, dbs = _mixer_bwd(sv["proj"], sv["ca"], dy, dproj, sp)
    dw["w_in"] = _mm_tn(sv["h1"], dproj, nj=NDEV, split="col", name="b_dw_in")
    dw["w_in"], dproj = lax.optimization_barrier((dw["w_in"], dproj))
    dproj = submit(dw, BIG[:1], dproj)
    dx, dg_mix = _mm(dproj, gw["w_in"], mode="full", trans_b=True, name="b_dh1", outs=[F32], tm=T // 8,
                     stream_first=True,
                     tiles=[sv["x"], dx2], params=[sp["g_mix"]], epi=_epi_rms_bwd, reds=[D])
    small = dict(norm_mix=dg_mix[0], conf_dw=dcw[:CONF_K], conf_dw_b=vec[0], conf_ln_g=vec[1], conf_ln_b=vec[2],
                 pool_w=dpw, pool_scale=vec[3], sc_conv=dsc[:SC_K], gmlp_ln_g=vec[4], gmlp_ln_b=vec[5],
                 gmlp_ws=dws, gmlp_bs=dbs[:, :, 0], norm_mlp=dg_mlp[0], norm_ple=dg_ple[0])
    return dx, small


ANY = pl.BlockSpec(memory_space=pl.ANY)


def _mesh_pos():
    return lax.axis_index("x"), lax.axis_index("y"), lax.axis_index("c")


def _other_chips(x, y):
    return [(1 - x, y), (x, 1 - y), (1 - x, 1 - y)]


def _launch_comm(body, peers_of, operands, out_shapes, sems, name, seq_id):
    n_in, n_out = len(operands), len(out_shapes)
    if seq_id is None:
        return pl.pallas_call(body, name=name, in_specs=[ANY] * n_in, out_specs=[ANY] * n_out,
                              out_shape=out_shapes, scratch_shapes=sems)(*operands)

    def seq_body(*refs):
        peers = peers_of(*_mesh_pos())
        barrier = pltpu.get_barrier_semaphore()
        for peer in peers:
            pl.semaphore_signal(barrier, inc=1, device_id=peer, device_id_type=MESH)
        pl.semaphore_wait(barrier, len(peers))
        body(*refs)

    return pl.kernel(seq_body, name=name, out_type=out_shapes,
                     mesh=plsc.ScalarSubcoreMesh(axis_name="seq", num_cores=1), scratch_types=sems,
                     compiler_params=pltpu.CompilerParams(collective_id=seq_id))(*operands)


def _all_gather(shards, name, seq_id=None):
    n = len(shards)

    def body(*refs):
        s_refs, o_refs = refs[:n], refs[n:2 * n]
        send_sems, recv_sems, local_sems = refs[2 * n:]
        x, y, c = _mesh_pos()
        me = 4 * x + 2 * y + c
        here = (x, y, c)
        sibling = (x, y, 1 - c)
        chips = _other_chips(x, y)

        def slot(px, py, pc):
            return 4 * px + 2 * py + pc

        def copy(t, k, slot_idx, to, src=None):
            dst = o_refs[t].at[slot_idx]
            return pltpu.make_async_remote_copy(
                src_ref=dst if src is None else src, dst_ref=dst,
                send_sem=send_sems.at[t * 7 + k], recv_sem=recv_sems.at[t * 7 + k],
                device_id=to, device_id_type=MESH)

        mine = [pltpu.make_async_copy(s_refs[t], o_refs[t].at[me], local_sems.at[t]) for t in range(n)]
        for cp in mine:
            cp.start()
        first = []
        for t in range(n):
            for j, chip in enumerate(chips):
                first.append(copy(t, 1 + j, me, (*chip, c), src=s_refs[t]))
        for t in range(n):
            first.append(copy(t, 0, me, sibling, src=s_refs[t]))
        for cp in first:
            cp.start()
        passed = []
        for t in range(n):
            for j, chip in enumerate(chips):
                copy(t, 1 + j, slot(*chip, c), here).wait_recv()
                fwd = copy(t, 4 + j, slot(*chip, c), sibling)
                fwd.start()
                passed.append(fwd)
        for t in range(n):
            copy(t, 0, slot(x, y, 1 - c), here).wait_recv()
            for j, chip in enumerate(chips):
                copy(t, 4 + j, slot(*chip, 1 - c), here).wait_recv()
        for cp in first + passed:
            cp.wait_send()
        for cp in mine:
            cp.wait()

    def peers_of(x, y, c):
        return [(x, y, 1 - c)] + [(*chip, c) for chip in _other_chips(x, y)]

    return _launch_comm(
        body, peers_of, shards, [jax.ShapeDtypeStruct((NDEV,) + s.shape, s.dtype) for s in shards],
        [pltpu.SemaphoreType.DMA((7 * n,)), pltpu.SemaphoreType.DMA((7 * n,)), pltpu.SemaphoreType.DMA((n,))],
        name, seq_id)


def _rs_exchange(p4s, qs, name, seq_id=None):
    n_p, n_q = len(p4s), len(qs)

    def body(*refs):
        p_refs, q_refs = refs[:n_p], refs[n_p:n_p + n_q]
        rb_refs, rc_refs = refs[n_p + n_q:2 * n_p + n_q], refs[2 * n_p + n_q:2 * (n_p + n_q)]
        pair_send, pair_recv, chip_send, chip_recv, local_sems = refs[2 * (n_p + n_q):]
        x, y, c = _mesh_pos()
        a_idx = 2 * x + y
        chips = _other_chips(x, y)
        mine = [pltpu.make_async_copy(q_refs[t].at[a_idx], rc_refs[t].at[a_idx], local_sems.at[t])
                for t in range(n_q)]
        sends = []
        for t in range(n_q):
            for j, chip in enumerate(chips):
                sends.append(pltpu.make_async_remote_copy(
                    src_ref=q_refs[t].at[2 * chip[0] + chip[1]], dst_ref=rc_refs[t].at[a_idx],
                    send_sem=chip_send.at[t * 3 + j], recv_sem=chip_recv.at[t * 3 + j],
                    device_id=(*chip, c), device_id_type=MESH))
        pairs = [pltpu.make_async_remote_copy(
            src_ref=p_refs[t].at[:, 1 - c], dst_ref=rb_refs[t], send_sem=pair_send.at[t], recv_sem=pair_recv.at[t],
            device_id=(x, y, 1 - c), device_id_type=MESH) for t in range(n_p)]
        for cp in sends + mine + pairs:
            cp.start()
        for cp in pairs:
            cp.wait()
        for t in range(n_q):
            for j, chip in enumerate(chips):
                landed = rc_refs[t].at[2 * chip[0] + chip[1]]
                pltpu.make_async_remote_copy(
                    src_ref=landed, dst_ref=landed, send_sem=chip_send.at[t * 3 + j],
                    recv_sem=chip_recv.at[t * 3 + j], device_id=(x, y, c), device_id_type=MESH).wait_recv()
        for cp in sends:
            cp.wait_send()
        for cp in mine:
            cp.wait()

    def peers_of(x, y, c):
        peers = [(x, y, 1 - c)] if n_p else []
        return peers + ([(*chip, c) for chip in _other_chips(x, y)] if n_q else [])

    out_shapes = [jax.ShapeDtypeStruct((NCHIP,) + p.shape[2:], p.dtype) for p in p4s]
    out_shapes += [jax.ShapeDtypeStruct(q.shape, q.dtype) for q in qs]
    sems = [pltpu.SemaphoreType.DMA((max(n_p, 1),)), pltpu.SemaphoreType.DMA((max(n_p, 1),)),
            pltpu.SemaphoreType.DMA((max(3 * n_q, 1),)), pltpu.SemaphoreType.DMA((max(3 * n_q, 1),)),
            pltpu.SemaphoreType.DMA((max(n_q, 1),))]
    got = _launch_comm(body, peers_of, list(p4s) + list(qs), out_shapes, sems, name, seq_id)
    return got[:n_p], got[n_p:]


def _pair_sum(p4s, rbs, c_idx, name, nst=1):
    n = len(p4s)
    trs = [p.shape[2] // nst for p in p4s]

    def body(c_ref, *refs):
        del c_ref
        p_refs, r_refs, o_refs = refs[:n], refs[n:2 * n], refs[2 * n:]
        for p_ref, r_ref, o_ref in zip(p_refs, r_refs, o_refs):
            o_ref[...] = (p_ref[...].astype(F32) + r_ref[...].astype(F32)).astype(o_ref.dtype)

    in_specs = [pl.BlockSpec((None, None, tr, p.shape[3]), lambda b, i, c_ref: (b, c_ref[0], i, 0))
                for p, tr in zip(p4s, trs)]
    in_specs += [pl.BlockSpec((None, tr, p.shape[3]), lambda b, i, c_ref: (b, i, 0)) for p, tr in zip(p4s, trs)]
    out_specs = [pl.BlockSpec((None, tr, p.shape[3]), lambda b, i, c_ref: (b, i, 0)) for p, tr in zip(p4s, trs)]
    return pl.pallas_call(
        body, name=name,
        grid_spec=pltpu.PrefetchScalarGridSpec(num_scalar_prefetch=1, grid=(NCHIP, nst), in_specs=in_specs,
                                               out_specs=out_specs),
        out_shape=[jax.ShapeDtypeStruct((NCHIP,) + p.shape[2:], p.dtype) for p in p4s],
        compiler_params=_cp("arbitrary", "arbitrary"),
    )(c_idx, *p4s, *rbs)


class _GradientPipeline:
    def __init__(self, c_idx, results):
        self.c_idx, self.results, self.pending = c_idx, results, None

    def _sum_pending(self, chain):
        names, layer, p4s, rbs = self.pending
        qs = _pair_sum(p4s, rbs, self.c_idx, name="rs_pairsum_%d" % len(names))
        return lax.optimization_barrier((chain, qs))

    def submit(self, dw, names, layer, chain):
        qs, tag, seq_id = [], "pair", 3
        if self.pending is not None:
            chain, qs = self._sum_pending(chain)
            tag, seq_id = "pair_chip", 4
        p4s = [dw[n].reshape((NCHIP, 2) + BIG_SHARD[n]) for n in names]
        rbs, rcs = _rs_exchange(p4s, qs, name="rs_%s_%d" % (tag, len(names)), seq_id=seq_id)
        self._record(rcs)
        self.pending = (names, layer, p4s, rbs)
        return chain

    def finish(self, chain):
        chain, qs = self._sum_pending(chain)
        self._record(_rs_exchange([], qs, name="rs_chip_last", seq_id=5)[1])
        self.pending = None
        return chain

    def _record(self, rcs):
        if rcs:
            names, layer = self.pending[:2]
            for n, rc in zip(names, rcs):
                self.results[n][layer] = rc


def _adamw(w, g, m, v):
    m = ADAM_B1 * m + (1.0 - ADAM_B1) * g
    v = ADAM_B2 * v + (1.0 - ADAM_B2) * (g * g)
    m_hat = m / (1.0 - ADAM_B1 ** ADAM_STEP)
    v_hat = v / (1.0 - ADAM_B2 ** ADAM_STEP)
    delta = -ADAM_LR * (m_hat / (jnp.sqrt(v_hat) + ADAM_EPS) + ADAM_WD * w)
    return delta, m, v


def _adam_sharded(rcs, w, m, v, tr, name, first_layer, partial=None):
    _, r, c = w.shape
    nst = r // tr
    n_l = len(rcs)

    def body(*refs):
        rc_refs = refs[:n_l]
        w_ref, m_ref, v_ref = refs[n_l:n_l + 3]
        g_out, d_out, m_out, v_out = refs[-4:]
        layer = pl.program_id(0)
        for k, rc in enumerate(rc_refs):
            @pl.when(layer == k)
            def _():
                g = rc[0].astype(F32) + rc[1].astype(F32) + rc[2].astype(F32) + rc[3].astype(F32)
                delta, m_new, v_new = _adamw(w_ref[...], g, m_ref[...], v_ref[...])
                g_out[...] = g
                d_out[...] = delta
                m_out[...] = m_new
                v_out[...] = v_new

    rc_specs = [pl.BlockSpec((NCHIP, tr, c), lambda l, i, k=k: (0, jnp.where(l == k, i, 0), 0)) for k in range(n_l)]
    wspec = pl.BlockSpec((None, tr, c), lambda l, i: (first_layer + l, i, 0))
    carried = [] if partial is None else list(partial)
    return pl.pallas_call(
        body, name=name, grid=(n_l, nst),
        in_specs=rc_specs + [wspec] * 3 + [pl.BlockSpec(memory_space=pl.ANY)] * len(carried),
        out_specs=[wspec] * 4, out_shape=[jax.ShapeDtypeStruct(w.shape, F32)] * 4,
        input_output_aliases={n_l + 3 + k: k for k in range(len(carried))},
        compiler_params=_cp("arbitrary", "arbitrary"),
    )(*rcs, w, m, v, *carried)


def _adam_packed(g, w, m, v, direct):
    n_d = len(direct)

    def pieces(shape):
        width = shape[-1]
        count = 1
        for s in shape[:-1]:
            count *= s
        per_row = D // width
        out = []
        for k in range(count):
            idx = (k,) if len(shape) == 2 else (k // shape[1], k % shape[1])
            out.append((idx, k // per_row, (k % per_row) * width, width))
        return out

    def body(g_ref, w_ref, m_ref, v_ref, d_out, m_out, v_out, *outs):
        delta, m_new, v_new = _adamw(w_ref[...], g_ref[...], m_ref[...], v_ref[...])
        d_out[...] = delta
        m_out[...] = m_new
        v_out[...] = v_new
        for a, (_, row0, shape) in enumerate(direct):
            for src, dst in zip((g_ref, d_out, m_out, v_out), outs[4 * a:4 * a + 4]):
                for idx, row, lane0, width in pieces(shape):
                    piece = src[pl.ds(row0 + row, 1), lane0:lane0 + width]
                    if len(idx) == 1:
                        dst[pl.ds(idx[0], 1), :] = piece
                    else:
                        dst[idx[0], pl.ds(idx[1], 1), :] = piece

    out_shape = [jax.ShapeDtypeStruct(g.shape, F32)] * 3
    for _, _, shape in direct:
        out_shape += [jax.ShapeDtypeStruct(shape, F32)] * 4
    res = pl.pallas_call(body, name="adam_small", out_shape=out_shape,
                         compiler_params=pltpu.CompilerParams(vmem_limit_bytes=VMEM_LIMIT_BYTES))(g, w, m, v)
    return res[:3], {name: res[3 + 4 * a:7 + 4 * a] for a, (name, _, _) in enumerate(direct)}


def _sum4(rc):
    def body(rc_ref, o_ref):
        o_ref[...] = rc_ref[0] + rc_ref[1] + rc_ref[2] + rc_ref[3]

    return pl.pallas_call(
        body, name="small_sum", out_shape=jax.ShapeDtypeStruct(rc.shape[1:], F32),
    )(rc)


BIG = ("w_in", "w_branch", "w_out", "w_up", "w_down", "w_ple", "w_pleg")
BIG_SHARD = {"w_in": (D, D), "w_branch": (4 * W, GW), "w_out": (GW, D), "w_up": (D, W), "w_down": (W, D),
             "w_ple": (256, GW), "w_pleg": (GW, D)}
ADAM_ROWS = {"w_in": 256, "w_branch": 512, "w_out": 128, "w_up": 256, "w_down": 256, "w_ple": 256, "w_pleg": 128}
SMALL = (("norm_mix", (DEPTH, D)), ("conf_dw", (DEPTH, CONF_K, W)), ("conf_dw_b", (DEPTH, W)),
         ("conf_ln_g", (DEPTH, W)), ("conf_ln_b", (DEPTH, W)), ("pool_w", (DEPTH, 4, GW, GW)),
         ("pool_scale", (DEPTH, W)), ("sc_conv", (DEPTH, SC_K, W)), ("gmlp_ln_g", (DEPTH, W)),
         ("gmlp_ln_b", (DEPTH, W)), ("gmlp_ws", (DEPTH, 4, GW, GW)), ("gmlp_bs", (DEPTH, 4, GW)),
         ("norm_mlp", (DEPTH, D)), ("norm_ple", (DEPTH, D)), ("norm_final", (D,)))
CHANNEL_SHARDED = ("conf_dw", "sc_conv")
SMALL_ROWS = 80


def _pack(arrs, rows):
    flat = jnp.concatenate([a.reshape(-1) for a in arrs])
    return jnp.pad(flat, (0, rows * D - flat.shape[0])).reshape(rows, D)


def _unpack(packed, shapes):
    flat = packed.reshape(-1)
    out, off = [], 0
    for shp in shapes:
        size = 1
        for s in shp:
            size *= s
        out.append(flat[off:off + size].reshape(shp))
        off += size
    return out


def kernel(x, p, norm_mix, w_in, conf_dw, conf_dw_b, conf_ln_g, conf_ln_b, pool_w, pool_scale, sc_conv, gmlp_ln_g, gmlp_ln_b, gmlp_ws, gmlp_bs, w_branch, w_out, norm_mlp, w_up, w_down, norm_ple, w_ple, w_ple_gate, norm_final, loss_target, m_norm_mix, m_w_in, m_conf_dw, m_conf_dw_b, m_conf_ln_g, m_conf_ln_b, m_pool_w, m_pool_scale, m_sc_conv, m_gmlp_ln_g, m_gmlp_ln_b, m_gmlp_ws, m_gmlp_bs, m_w_branch, m_w_out, m_norm_mlp, m_w_up, m_w_down, m_norm_ple, m_w_ple, m_w_ple_gate, m_norm_final, v_norm_mix, v_w_in, v_conf_dw, v_conf_dw_b, v_conf_ln_g, v_conf_ln_b, v_pool_w, v_pool_scale, v_sc_conv, v_gmlp_ln_g, v_gmlp_ln_b, v_gmlp_ws, v_gmlp_bs, v_w_branch, v_w_out, v_norm_mlp, v_w_up, v_w_down, v_norm_ple, v_w_ple, v_w_ple_gate, v_norm_final):
    weights = dict(norm_mix=norm_mix, w_in=w_in, conf_dw=conf_dw, conf_dw_b=conf_dw_b, conf_ln_g=conf_ln_g,
                   conf_ln_b=conf_ln_b, pool_w=pool_w, pool_scale=pool_scale, sc_conv=sc_conv, gmlp_ln_g=gmlp_ln_g,
                   gmlp_ln_b=gmlp_ln_b, gmlp_ws=gmlp_ws, gmlp_bs=gmlp_bs, w_branch=w_branch, w_out=w_out,
                   norm_mlp=norm_mlp, w_up=w_up, w_down=w_down, norm_ple=norm_ple, w_ple=w_ple, w_pleg=w_ple_gate,
                   norm_final=norm_final)
    mom1 = dict(norm_mix=m_norm_mix, w_in=m_w_in, conf_dw=m_conf_dw, conf_dw_b=m_conf_dw_b, conf_ln_g=m_conf_ln_g,
                conf_ln_b=m_conf_ln_b, pool_w=m_pool_w, pool_scale=m_pool_scale, sc_conv=m_sc_conv,
                gmlp_ln_g=m_gmlp_ln_g, gmlp_ln_b=m_gmlp_ln_b, gmlp_ws=m_gmlp_ws, gmlp_bs=m_gmlp_bs,
                w_branch=m_w_branch, w_out=m_w_out, norm_mlp=m_norm_mlp, w_up=m_w_up, w_down=m_w_down,
                norm_ple=m_norm_ple, w_ple=m_w_ple, w_pleg=m_w_ple_gate, norm_final=m_norm_final)
    mom2 = dict(norm_mix=v_norm_mix, w_in=v_w_in, conf_dw=v_conf_dw, conf_dw_b=v_conf_dw_b, conf_ln_g=v_conf_ln_g,
                conf_ln_b=v_conf_ln_b, pool_w=v_pool_w, pool_scale=v_pool_scale, sc_conv=v_sc_conv,
                gmlp_ln_g=v_gmlp_ln_g, gmlp_ln_b=v_gmlp_ln_b, gmlp_ws=v_gmlp_ws, gmlp_bs=v_gmlp_bs,
                w_branch=v_w_branch, w_out=v_w_out, norm_mlp=v_norm_mlp, w_up=v_w_up, w_down=v_w_down,
                norm_ple=v_norm_ple, w_ple=v_w_ple, w_pleg=v_w_ple_gate, norm_final=v_norm_final)

    xi, yi, ci = _mesh_pos()
    me = 4 * xi + 2 * yi + ci
    c_idx = jnp.reshape(ci, (1,)).astype(jnp.int32)

    gathered, conf_full, sc_full = [], [], []
    for l in range(DEPTH):
        shard = lambda n: weights[n][l].astype(BF16).reshape(BIG_SHARD[n])
        w_in_g, conf_g, sc_g = _all_gather([shard("w_in"), conf_dw[l], sc_conv[l]], name="ag_first", seq_id=1)
        if l + 1 < DEPTH:
            rest = _all_gather([shard(n) for n in BIG[1:]], name="ag_rest", seq_id=2)
        else:
            rest = (list(_all_gather([shard(n) for n in BIG[1:4]], name="ag_rest_a", seq_id=2))
                    + list(_all_gather([shard(n) for n in BIG[4:]], name="ag_rest_b", seq_id=2)))
        gw = dict(zip(BIG[1:], rest), w_in=w_in_g)
        gw["w_branch"] = gw["w_branch"].reshape(NDEV, 4, W, GW)
        gathered.append(gw)
        conf_full.append(conf_g)
        sc_full.append(sc_g)

    def small_params(l):
        return dict(cw=conf_full[l], cb=conf_dw_b[l][None], lg=conf_ln_g[l][None], lb=conf_ln_b[l][None],
                    pw=pool_w[l], ps=pool_scale[l][None], sc=sc_full[l], gg=gmlp_ln_g[l][None],
                    gb=gmlp_ln_b[l][None], ws=gmlp_ws[l], bst=gmlp_bs[l].T, g_mix=norm_mix[l][None],
                    g_mlp=norm_mlp[l][None], g_ple=norm_ple[l][None])

    xc = x.reshape(T, D)
    small_names = [n for n, _ in SMALL]

    def in_gradient_layout(n, shard, shape):
        if n not in CHANNEL_SHARDED:
            return shard
        return lax.dynamic_update_slice(jnp.zeros(shape, F32), shard, (0, 0, me * (W // NDEV)))

    small_state = [_pack([in_gradient_layout(n, src[n], shape) for n, shape in SMALL], NDEV * SMALL_ROWS)
                   for src in (weights, mom1, mom2)]
    xc, small_state = lax.optimization_barrier((xc, small_state))
    p_bf = p.reshape(DEPTH, T, 256).astype(BF16)
    h = _norm_first(xc, norm_mix[0][None])
    saved = []
    for l in range(DEPTH):
        g_next = norm_mix[l + 1][None] if l + 1 < DEPTH else norm_final[None]
        h, conf_g, sc_g = lax.optimization_barrier((h, conf_full[l], sc_full[l]))
        conf_full[l] = conf_g.transpose(1, 0, 2).reshape(CONF_K, W)
        sc_full[l] = sc_g.transpose(1, 0, 2).reshape(SC_K, W)
        xc, h, sv = _layer_fwd(xc, h, p_bf[l], gathered[l], small_params(l), g_next)
        saved.append(sv)

    dxc, dg_final, loss_part = _loss_head(xc, loss_target.reshape(T, D), norm_final[None])
    loss = lax.psum(loss_part[0, 0], ("x", "y", "c"))
    small_grads = [None] * DEPTH
    rcs = {n: [None] * DEPTH for n in BIG}
    pipeline = _GradientPipeline(c_idx, rcs)
    for l in reversed(range(DEPTH)):
        dxc, small_grads[l] = _layer_bwd(dxc, saved[l], gathered[l], small_params(l),
                                         lambda dw, names, value, l=l: pipeline.submit(dw, names, l, value),
                                         early_group=(l == 0))

    def adam_sharded(first_layer, n_layers, partial, tag, names=BIG):
        outs = {}
        for n in names:
            shp = (DEPTH,) + BIG_SHARD[n]
            outs[n] = _adam_sharded(rcs[n][first_layer:first_layer + n_layers], weights[n].reshape(shp),
                                    mom1[n].reshape(shp), mom2[n].reshape(shp), ADAM_ROWS[n],
                                    "adam_%s_%s" % (n, tag), first_layer, None if partial is None else partial[n])
        return outs

    stacked = {n: jnp.stack([small_grads[l][n] for l in range(DEPTH)]) for n, _ in SMALL if n != "norm_final"}
    stacked["norm_final"] = dg_final[0]
    packed = _pack([stacked[n] for n, _ in SMALL], NDEV * SMALL_ROWS).reshape(NCHIP, 2, SMALL_ROWS, D)
    (pair_small,), _ = _rs_exchange([packed], [], name="rs_pair_small")
    q_small = _pair_sum([packed], [pair_small], c_idx, name="rs_pairsum_small")
    dxc, upper, q_small = lax.optimization_barrier((dxc, {n: rcs[n][1:] for n in BIG}, q_small))
    _, (chips_small,) = _rs_exchange([], q_small, name="rs_chip_small", seq_id=6)
    dxc, upper = pipeline.finish((dxc, upper))
    for n in BIG:
        rcs[n][1:] = upper[n]
    partial = adam_sharded(1, DEPTH - 1, None, "upper")
    last = adam_sharded(0, 1, partial, "last", names=BIG[:1])
    partial = {n: partial[n] for n in BIG[1:]}
    last, partial, chips_small = lax.optimization_barrier((last, partial, chips_small))
    reduced_slot = _sum4(chips_small)
    reduced = _all_gather([reduced_slot], name="ag_small", seq_id=7)[0]
    small_full = dict(zip([n for n, _ in SMALL], _unpack(reduced, [s for _, s in SMALL])))
    grads, deltas, new_m, new_v = {}, {}, {}, {}
    direct, row = [], 0
    for n, shape in SMALL:
        if len(shape) == 1 or shape[-2] == DEPTH:
            direct.append((n, row, (1,) * (2 - len(shape)) + tuple(shape)))
        size = 1
        for s in shape:
            size *= s
        row += size // D
    (d_p, m_p, v_p), own_shape = _adam_packed(reduced.reshape(NDEV * SMALL_ROWS, D), *small_state, direct)
    small_shapes = [s for _, s in SMALL]

    def own_channels(n, full):
        return lax.dynamic_slice_in_dim(full, me * (W // NDEV), W // NDEV, axis=2) if n in CHANNEL_SHARDED else full

    for n, d_, m_, v_ in zip(small_names, _unpack(d_p, small_shapes), _unpack(m_p, small_shapes),
                             _unpack(v_p, small_shapes)):
        if n in own_shape:
            grads[n], deltas[n], new_m[n], new_v[n] = [a.reshape(weights[n].shape) for a in own_shape[n]]
        else:
            grads[n], deltas[n], new_m[n], new_v[n] = (own_channels(n, small_full[n]), own_channels(n, d_),
                                                       own_channels(n, m_), own_channels(n, v_))

    last.update(adam_sharded(0, 1, partial, "last", names=BIG[1:]))
    for n, (g_, d_, m_, v_) in last.items():
        full = weights[n].shape
        grads[n], deltas[n], new_m[n], new_v[n] = g_.reshape(full), d_.reshape(full), m_.reshape(full), v_.reshape(full)

    order = ("norm_mix", "w_in", "conf_dw", "conf_dw_b", "conf_ln_g", "conf_ln_b", "pool_w", "pool_scale", "sc_conv",
             "gmlp_ln_g", "gmlp_ln_b", "gmlp_ws", "gmlp_bs", "w_branch", "w_out", "norm_mlp", "w_up", "w_down",
             "norm_ple", "w_ple", "w_pleg", "norm_final")
    return (loss, dxc.reshape(1, T, D), *[grads[n] for n in order], *[deltas[n] for n in order],
            *[new_m[n] for n in order], *[new_v[n] for n in order])
```

```python
import functools

import jax
import jax.numpy as jnp
from jax import lax
from jax.experimental import pallas as pl
from jax.experimental.pallas import tpu as pltpu
from jax.experimental.pallas import tpu_sc as plsc

F32 = jnp.float32
BF16 = jnp.bfloat16

DEPTH = 4
T = 2048
D = 1024
W = 512
NDEV = 8
NCHIP = 4
EPS = 1e-6
CONF_K = 31
SC_K = 3
POOL_WINDOWS = (2, 4, 8, 16)
GW = 128
HB = 32
HA = 32
COLS_IN = 8192
MIX_COLS = 4096

ADAM_LR = 0.001
ADAM_B1 = 0.9
ADAM_B2 = 0.999
ADAM_EPS = 1e-08
ADAM_WD = 0.01
ADAM_STEP = 10

VMEM_LIMIT_BYTES = 56 * 1024 * 1024
MESH = pl.DeviceIdType.MESH


def _cp(*sem):
    return pltpu.CompilerParams(dimension_semantics=tuple(sem), vmem_limit_bytes=VMEM_LIMIT_BYTES)


def _sig(x):
    return jax.nn.sigmoid(x)


def _rms(x, g):
    r = lax.rsqrt(jnp.mean(x * x, axis=-1, keepdims=True) + EPS)
    return x * r * g


def _rms_bwd(dh, x, g, dres):
    r = lax.rsqrt(jnp.mean(x * x, axis=-1, keepdims=True) + EPS)
    xh = x * r
    u = dh * g
    dx = r * (u - xh * jnp.mean(u * xh, axis=-1, keepdims=True)) + dres
    dg = jnp.sum(dh * xh, axis=0, keepdims=True)
    return dx, dg


def _ln_stats(x):
    mu = jnp.mean(x, axis=-1, keepdims=True)
    xc = x - mu
    rstd = lax.rsqrt(jnp.mean(xc * xc, axis=-1, keepdims=True) + EPS)
    return xc * rstd, rstd


def _ln_bwd(dxh, xh, rstd):
    return rstd * (dxh - jnp.mean(dxh, axis=-1, keepdims=True) - xh * jnp.mean(dxh * xh, axis=-1, keepdims=True))


def _rowsum(x):
    return jnp.sum(x, axis=0, keepdims=True)


EPI_ROWS = 256


def _relu2_bf16(up):
    r = jnp.maximum(up.astype(F32), 0.0)
    return (r * r).astype(BF16)


def _mm(a, b3, *, mode, name, outs, trans_b=False, tm=512, tiles=(), params=(), epi=None, reds=(), a_pre=None,
        stream_first=False):
    t_, ka = a.shape
    nj, r, c = b3.shape
    kb, nb = (c, r) if trans_b else (r, c)
    nt = t_ // tm
    out_mode = mode == "out"
    full = mode == "full"
    assert trans_b or not full
    if out_mode:
        assert ka == kb and not reds
        grid = (nj, nt)
        a_map = lambda g0, g1: (g1, 0)
        b_map = lambda g0, g1: (g0, 0, 0)
        t_map = lambda g0, g1: (g1, g0)
        width = nj * nb
    else:
        assert ka == nj * kb
        grid = (nt, 1 if full else nj)
        a_map = lambda g0, g1: (g0, g1)
        b_map = lambda g0, g1: (g1, 0, 0)
        t_map = lambda g0, g1: (g0, 0)
        width = nb
    n_t, n_p, n_o, n_r = len(tiles), len(params), len(outs), len(reds)
    use_acc = (not out_mode) and nj > 1 and not full
    dims = (((1,), (1,)), ((), ())) if trans_b else (((1,), (0,)), ((), ()))

    def body(a_ref, b_ref, *rest):
        t_refs = rest[:n_t]
        p_refs = rest[n_t:n_t + n_p]
        o_refs = rest[n_t + n_p:n_t + n_p + n_o]
        r_refs = rest[n_t + n_p + n_o:n_t + n_p + n_o + n_r]
        i = pl.program_id(1 if out_mode else 0)
        a_val = a_ref[...] if a_pre is None else a_pre(a_ref[...])
        if full:
            b_all, b_sems = rest[-2 - stream_first], rest[-1 - stream_first]

            def weight_copies():
                return [pltpu.make_async_copy(b_ref.at[j], b_all.at[:, j * c:(j + 1) * c], b_sems.at[j])
                        for j in range(nj)]

            if stream_first:
                part = rest[-1]

                @pl.when(i == 0)
                def _():
                    cps = weight_copies()
                    for cp in cps:
                        cp.start()
                    acc = None
                    for j, cp in enumerate(cps):
                        cp.wait()
                        term = lax.dot_general(a_val[:, j * c:(j + 1) * c], b_all[:, j * c:(j + 1) * c], dims,
                                               preferred_element_type=F32)
                        acc = term if acc is None else acc + term
                    part[...] = acc

                @pl.when(i > 0)
                def _():
                    part[...] = lax.dot_general(a_val, b_all[...], dims, preferred_element_type=F32)
            else:
                @pl.when(i == 0)
                def _():
                    cps = weight_copies()
                    for cp in cps:
                        cp.start()
                    for cp in cps:
                        cp.wait()

                part = lax.dot_general(a_val, b_all[...], dims, preferred_element_type=F32)
        else:
            part = lax.dot_general(a_val, b_ref[...], dims, preferred_element_type=F32)

        def finish(acc_rows):
            totals = [None] * n_r
            for r0 in range(0, tm, min(tm, EPI_ROWS)):
                rows = slice(r0, r0 + min(tm, EPI_ROWS))
                if epi is None:
                    res, rr = (acc_rows(rows),), ()
                else:
                    res, rr = epi(acc_rows(rows), [t[rows, :] for t in t_refs], [p[...] for p in p_refs])
                for o_ref, val in zip(o_refs, res):
                    o_ref[rows, :] = val.astype(o_ref.dtype)
                totals = [val if tot is None else tot + val for tot, val in zip(totals, rr)]
            for r_ref, val in zip(r_refs, totals):
                @pl.when(i == 0)
                def _():
                    r_ref[...] = val

                @pl.when(i > 0)
                def _():
                    r_ref[...] += val

        if use_acc:
            acc_ref = rest[-1]
            j = pl.program_id(1)

            @pl.when(j == 0)
            def _():
                acc_ref[...] = part

            @pl.when(jnp.logical_and(j > 0, j < nj - 1))
            def _():
                acc_ref[...] += part

            @pl.when(j == nj - 1)
            def _():
                finish(lambda rows: acc_ref[rows, :] + part[rows])
        else:
            finish(lambda rows: part[rows, :])

    const2 = lambda g0, g1: (0, 0)
    if full:
        in_specs = [pl.BlockSpec((tm, ka), a_map), pl.BlockSpec(memory_space=pl.ANY)]
        scratch = [pltpu.VMEM((r, nj * c), b3.dtype), pltpu.SemaphoreType.DMA((nj,))]
        scratch += [pltpu.VMEM((tm, nb), F32)] if stream_first else []
    else:
        in_specs = [pl.BlockSpec((tm, kb), a_map), pl.BlockSpec((None, r, c), b_map)]
        scratch = [pltpu.VMEM((tm, nb), F32)] if use_acc else []
    in_specs += [pl.BlockSpec((tm, t.shape[1] // nj if out_mode else t.shape[1]), t_map) for t in tiles]
    in_specs += [pl.BlockSpec(p.shape, lambda g0, g1, nd=p.ndim: (0,) * nd) for p in params]
    out_specs = [pl.BlockSpec((tm, nb), t_map) for _ in outs] + [pl.BlockSpec((1, w), const2) for w in reds]
    out_shape = [jax.ShapeDtypeStruct((t_, width), dt) for dt in outs]
    out_shape += [jax.ShapeDtypeStruct((1, w), F32) for w in reds]
    res = pl.pallas_call(
        body, name=name, grid=grid, in_specs=in_specs, out_specs=out_specs, out_shape=out_shape,
        scratch_shapes=scratch, compiler_params=_cp("arbitrary", "arbitrary"),
    )(a, b3, *tiles, *params)
    return res


def _mm_tn(a, g, *, nj, split, name, out_dtype=BF16, a_pre=None):
    t_ = a.shape[0]
    if split == "col":
        r, c = a.shape[1], g.shape[1] // nj
        a_spec = pl.BlockSpec((t_, r), lambda j: (0, 0))
        g_spec = pl.BlockSpec((t_, c), lambda j: (0, j))
    else:
        r, c = a.shape[1] // nj, g.shape[1]
        a_spec = pl.BlockSpec((t_, r), lambda j: (0, j))
        g_spec = pl.BlockSpec((t_, c), lambda j: (0, 0))

    def body(a_ref, g_ref, o_ref):
        a_val = a_ref[...] if a_pre is None else a_pre(a_ref[...])
        o_ref[...] = lax.dot_general(a_val, g_ref[...], (((0,), (0,)), ((), ())),
                                     preferred_element_type=F32).astype(o_ref.dtype)

    return pl.pallas_call(
        body, name=name, grid=(nj,), in_specs=[a_spec, g_spec],
        out_specs=pl.BlockSpec((None, r, c), lambda j: (j, 0, 0)),
        out_shape=jax.ShapeDtypeStruct((nj, r, c), out_dtype),
        compiler_params=_cp("arbitrary"),
    )(a, g)


def _epi_res_norm(acc, tiles, params):
    x_new = tiles[0] + acc
    return (x_new, _rms(x_new, params[0])), ()


def _epi_ple(acc, tiles, params):
    x_old, p_tile = tiles
    g_next, w_ple8 = params
    pe = jnp.concatenate([jnp.dot(p_tile, w_ple8[j], preferred_element_type=F32) for j in range(NDEV)], axis=1)
    x_new = x_old + pe * _sig(acc)
    return (x_new, acc, _rms(x_new, g_next), pe), ()


def _epi_rms_bwd(acc, tiles, params):
    dx, dg = _rms_bwd(acc, tiles[0], params[0], tiles[1])
    return (dx, dx), (dg,)


def _epi_dup(acc, tiles, params):
    return (acc * (2.0 * jnp.maximum(tiles[0].astype(F32), 0.0)),), ()


def _tri_mask():
    row = lax.broadcasted_iota(jnp.int32, (GW, GW), 0)
    col = lax.broadcasted_iota(jnp.int32, (GW, GW), 1)
    return row >= col


def _small_specs(sp_list):
    return [pl.BlockSpec(p.shape, (lambda i: (0, 0)) if p.ndim == 2 else (lambda i: (0, 0, 0))) for p in sp_list]


SUBLANES = 8


def _tap_sum(src, w_ref, taps, rows, stage):
    groups = {}
    for off, k in taps:
        groups.setdefault(off % SUBLANES, []).append((off - off % SUBLANES, k))
    out = None
    for res, members in sorted(groups.items()):
        n = rows if res == 0 else rows + SUBLANES
        part = None
        for base, k in members:
            term = w_ref[k:k + 1, :] * src[pl.ds(base, n), :]
            part = term if part is None else part + term
        if res:
            stage[0:n, :] = part
            part = stage[pl.ds(res, rows), :]
        out = part if out is None else out + part
    return out


def _tap_grads(grad, src, offsets, rows, stage, out_ref):
    pad = SUBLANES
    stage[0:pad, :] = jnp.zeros((pad, grad.shape[1]), F32)
    stage[pad:pad + rows, :] = grad
    stage[pad + rows:2 * pad + rows, :] = jnp.zeros((pad, grad.shape[1]), F32)
    groups = {}
    for k, off in enumerate(offsets):
        groups.setdefault(off % SUBLANES, []).append((off - off % SUBLANES, k))
    for res, members in sorted(groups.items()):
        shifted = stage[pl.ds(pad - res, rows + pad), :]
        for base, k in members:
            out_ref[k:k + 1, :] += _rowsum(shifted * src[pl.ds(base, rows + pad), :])


def _mixer_params(sp):
    return [sp["cw"], sp["cb"], sp["lg"], sp["lb"], sp["pw"], sp["ps"], sp["sc"], sp["gg"], sp["gb"], sp["ws"], sp["bst"]]


def _mixer_fwd(proj, sp, tm=256):
    nt = T // tm
    per = tm // HB

    conv_taps = [(HB - (CONF_K - 1) + k, k) for k in range(CONF_K)]

    def body(main_ref, halo_ref, cw, cb, lg, lb, pw, ps, sc, gg, gb, ws, bst, y_ref, ca_ref, ext, stage):
        i = pl.program_id(0)
        keep = (i > 0).astype(F32)

        def mcol(c0):
            return main_ref[:, c0:c0 + W].astype(F32)

        def hcol(c0):
            return halo_ref[:, c0:c0 + W].astype(F32)

        ext[0:HB, :] = hcol(0) * _sig(hcol(W)) * keep
        ext[HB:HB + tm, :] = mcol(0) * _sig(mcol(W))
        ca = (_tap_sum(ext, cw, conv_taps, tm, stage) + cb[...]).astype(BF16)
        ca_ref[...] = ca
        xh, _ = _ln_stats(ca.astype(F32))
        n = xh * lg[...] + lb[...]
        y_ref[:, 0:W] = (n * _sig(n)).astype(BF16)

        pin = mcol(1024)
        ext[0:HB, :] = hcol(1024) * keep
        ext[HB:HB + tm, :] = pin
        pos = (i * tm + lax.broadcasted_iota(jnp.int32, (tm, 1), 0) + 1).astype(F32)
        for g, w in enumerate(POOL_WINDOWS):
            lo = g * GW
            s = ext[pl.ds(HB, tm), lo:lo + GW]
            for j in range(1, w):
                s = s + ext[pl.ds(HB - j, tm), lo:lo + GW]
            pooled = s / jnp.minimum(pos, float(w)) - pin[:, lo:lo + GW]
            mixed = jnp.dot(pooled.astype(BF16), pw[g].astype(BF16), preferred_element_type=F32)
            y_ref[:, W + lo:W + lo + GW] = (mixed * ps[:, lo:lo + GW]).astype(BF16)

        ext[0:HB, :] = hcol(2048) * hcol(2560) * keep
        ext[HB:HB + tm, :] = mcol(2048) * mcol(2560)
        cv = sc[0:1, :] * ext[pl.ds(HB - 2, tm), :]
        cv = cv + sc[1:2, :] * ext[pl.ds(HB - 1, tm), :]
        cv = cv + sc[2:3, :] * ext[pl.ds(HB, tm), :]
        y_ref[:, 2 * W:3 * W] = (mcol(1536) * cv).astype(BF16)

        vh, _ = _ln_stats(mcol(3584))
        vn = (vh * gg[...] + gb[...]).astype(BF16)
        u = mcol(3072)
        tri = _tri_mask()
        for g in range(4):
            lo = g * GW
            wm = jnp.where(tri, ws[g], 0.0).astype(BF16)
            for c in range(tm // GW):
                r0 = c * GW
                sg = jnp.dot(wm, vn[r0:r0 + GW, lo:lo + GW], preferred_element_type=F32) + bst[:, g:g + 1]
                y_ref[r0:r0 + GW, 3 * W + lo:3 * W + lo + GW] = (u[r0:r0 + GW, lo:lo + GW] * sg).astype(BF16)

    plist = _mixer_params(sp)
    in_specs = [pl.BlockSpec((tm, MIX_COLS), lambda i: (i, 0)),
                pl.BlockSpec((HB, MIX_COLS), lambda i: (jnp.maximum(i * per - 1, 0), 0))]
    in_specs += _small_specs(plist)
    return pl.pallas_call(
        body, name="f_mixers", grid=(nt,), in_specs=in_specs,
        out_specs=[pl.BlockSpec((tm, 4 * W), lambda i: (i, 0)), pl.BlockSpec((tm, W), lambda i: (i, 0))],
        out_shape=[jax.ShapeDtypeStruct((T, 4 * W), BF16), jax.ShapeDtypeStruct((T, W), BF16)],
        scratch_shapes=[pltpu.VMEM((HB + tm, W), F32), pltpu.VMEM((tm + SUBLANES, W), F32)],
        compiler_params=_cp("arbitrary"),
    )(proj, proj, *plist)


def _assemble_wb(wb8_ref, wbf_ref):
    for k in range(4):
        for j in range(NDEV):
            wbf_ref[k, :, j * GW:(j + 1) * GW] = wb8_ref[j, k]


def _merge_fwd(y, proj, wb8, tm=256):
    nt = T // tm

    def body(y_ref, gate_ref, wb8_ref, z_ref, s_ref, m_ref, wbf):
        @pl.when(pl.program_id(0) == 0)
        def _():
            _assemble_wb(wb8_ref, wbf)

        m = jnp.zeros((tm, D), F32)
        for k in range(4):
            zk = jnp.dot(y_ref[:, k * W:(k + 1) * W], wbf[k], preferred_element_type=F32)
            z_ref[:, k * D:(k + 1) * D] = zk.astype(BF16)
            s = _sig(gate_ref[:, k * D:(k + 1) * D].astype(F32))
            s_ref[:, k * D:(k + 1) * D] = s.astype(BF16)
            m = m + s * zk
        m_ref[...] = m.astype(BF16)

    return pl.pallas_call(
        body, name="f_merge", grid=(nt,),
        in_specs=[pl.BlockSpec((tm, 4 * W), lambda i: (i, 0)),
                  pl.BlockSpec((tm, 4 * D), lambda i: (i, 1)),
                  pl.BlockSpec(wb8.shape, lambda i: (0, 0, 0, 0))],
        out_specs=[pl.BlockSpec((tm, 4 * D), lambda i: (i, 0)), pl.BlockSpec((tm, 4 * D), lambda i: (i, 0)),
                   pl.BlockSpec((tm, D), lambda i: (i, 0))],
        out_shape=[jax.ShapeDtypeStruct((T, 4 * D), BF16), jax.ShapeDtypeStruct((T, 4 * D), BF16),
                   jax.ShapeDtypeStruct((T, D), BF16)],
        scratch_shapes=[pltpu.VMEM((4, W, D), BF16)],
        compiler_params=_cp("arbitrary"),
    )(y, proj, wb8)


def _merge_bwd(dm, z, gates, y, wb8, tm=256):
    nt = T // tm

    def body(dm_ref, z_ref, gate_ref, y_ref, wb8_ref, dp_ref, dy_ref, dwb_ref, wbf, acc):
        i = pl.program_id(0)

        @pl.when(i == 0)
        def _():
            _assemble_wb(wb8_ref, wbf)

        dmv = dm_ref[...].astype(F32)
        for k in range(4):
            s = gate_ref[:, k * D:(k + 1) * D].astype(F32)
            dzk = (dmv * s).astype(BF16)
            dp_ref[:, k * D:(k + 1) * D] = (dmv * z_ref[:, k * D:(k + 1) * D].astype(F32) * s * (1.0 - s)).astype(BF16)
            dyk = lax.dot_general(dzk, wbf[k], (((1,), (1,)), ((), ())), preferred_element_type=F32)
            dy_ref[:, k * W:(k + 1) * W] = dyk.astype(BF16)
            part = lax.dot_general(y_ref[:, k * W:(k + 1) * W], dzk, (((0,), (0,)), ((), ())),
                                   preferred_element_type=F32)

            @pl.when(i == 0)
            def _():
                acc[k] = part

            @pl.when(i > 0)
            def _():
                acc[k] += part

        @pl.when(i == nt - 1)
        def _():
            for k in range(4):
                for j in range(NDEV):
                    dwb_ref[j, k] = acc[k, :, j * GW:(j + 1) * GW].astype(BF16)

    return pl.pallas_call(
        body, name="b_merge", grid=(nt,),
        in_specs=[pl.BlockSpec((tm, D), lambda i: (i, 0)),
                  pl.BlockSpec((tm, 4 * D), lambda i: (i, 0)),
                  pl.BlockSpec((tm, 4 * D), lambda i: (i, 0)),
                  pl.BlockSpec((tm, 4 * W), lambda i: (i, 0)),
                  pl.BlockSpec(wb8.shape, lambda i: (0, 0, 0, 0))],
        out_specs=[pl.BlockSpec((tm, 4 * D), lambda i: (i, 1)),
                   pl.BlockSpec((tm, 4 * W), lambda i: (i, 0)),
                   pl.BlockSpec(wb8.shape, lambda i: (0, 0, 0, 0))],
        out_shape=[jax.ShapeDtypeStruct((T, COLS_IN), BF16),
                   jax.ShapeDtypeStruct((T, 4 * W), BF16),
                   jax.ShapeDtypeStruct(wb8.shape, BF16)],
        scratch_shapes=[pltpu.VMEM((4, W, D), BF16), pltpu.VMEM((4, W, D), F32)],
        compiler_params=_cp("arbitrary"),
    )(dm, z, gates, y, wb8)


def _mixer_bwd(proj, ca_saved, dy, dproj, sp, tm=512):
    nt = T // tm
    per = tm // HB
    ne = tm + HA
    last_blk = T // HA - 1
    conv_taps = [(HB - (CONF_K - 1) + k, k) for k in range(CONF_K)]

    def body(main_ref, hb_ref, ha_ref, ca_ref, cah_ref, dy_ref, dyh_ref, cw, cb, lg, lb, pw, ps, sc, gg, gb, ws, bst,
             dp_any, dp_ref, dcw_ref, dsc_ref, vec_ref, dpw_ref, dws_ref, dbs_ref, e1, e2, e3, stage):
        del dp_any, cb
        i = pl.program_id(0)
        keep_b = (i > 0).astype(F32)
        keep_a = (i < nt - 1).astype(F32)

        @pl.when(i == 0)
        def _():
            dcw_ref[...] = jnp.zeros_like(dcw_ref)
            dsc_ref[...] = jnp.zeros_like(dsc_ref)
            vec_ref[...] = jnp.zeros_like(vec_ref)
            dpw_ref[...] = jnp.zeros_like(dpw_ref)
            dws_ref[...] = jnp.zeros_like(dws_ref)
            dbs_ref[...] = jnp.zeros_like(dbs_ref)

        def mcol(c0):
            return main_ref[:, c0:c0 + W].astype(F32)

        def hbcol(c0):
            return hb_ref[:, c0:c0 + W].astype(F32)

        def hacol(c0):
            return ha_ref[:, c0:c0 + W].astype(F32)

        def load_dy(c0):
            e2[0:tm, :] = dy_ref[:, c0:c0 + W].astype(F32)
            e2[tm:ne, :] = dyh_ref[:, c0:c0 + W].astype(F32) * keep_a

        a = mcol(0)
        sa = _sig(mcol(W))
        e1[0:HB, :] = hbcol(0) * _sig(hbcol(W)) * keep_b
        e1[HB:HB + tm, :] = a * sa
        e1[HB + tm:HB + tm + SUBLANES, :] = jnp.zeros((SUBLANES, W), F32)
        e2[0:tm, :] = ca_ref[...].astype(F32)
        e2[tm:ne, :] = cah_ref[...].astype(F32)
        xh, rstd = _ln_stats(e2[0:ne, :])
        nn = xh * lg[...] + lb[...]
        s = _sig(nn)
        load_dy(0)
        dn = e2[0:ne, :] * (s * (1.0 + nn * (1.0 - s)))
        vec_ref[1:2, :] += _rowsum(dn[0:tm] * xh[0:tm])
        vec_ref[2:3, :] += _rowsum(dn[0:tm])
        dca = _ln_bwd(dn * lg[...], xh, rstd)
        e3[0:ne, :] = dca
        dmain = dca[0:tm]
        vec_ref[0:1, :] += _rowsum(dmain)
        _tap_grads(dmain, e1, [off for off, _ in conv_taps], tm, stage, dcw_ref)
        dglu = _tap_sum(e3, cw, [(CONF_K - 1 - k, k) for k in range(CONF_K)], tm, stage)
        dp_ref[:, 0:W] = (dglu * sa).astype(BF16)
        dp_ref[:, W:2 * W] = (dglu * a * sa * (1.0 - sa)).astype(BF16)

        pin = mcol(1024)
        e1[0:HB, :] = hbcol(1024) * keep_b
        e1[HB:HB + tm, :] = pin
        load_dy(W)
        dyb = e2[0:ne, :]
        pos_m = (i * tm + lax.broadcasted_iota(jnp.int32, (tm, 1), 0) + 1).astype(F32)
        pos_e = (i * tm + lax.broadcasted_iota(jnp.int32, (ne, 1), 0) + 1).astype(F32)
        for g, w in enumerate(POOL_WINDOWS):
            lo = g * GW
            acc = e1[pl.ds(HB, tm), lo:lo + GW]
            for j in range(1, w):
                acc = acc + e1[pl.ds(HB - j, tm), lo:lo + GW]
            pooled = (acc / jnp.minimum(pos_m, float(w)) - pin[:, lo:lo + GW]).astype(BF16)
            pwb = pw[g].astype(BF16)
            mixed = jnp.dot(pooled, pwb, preferred_element_type=F32)
            dyb_g = dyb[:, lo:lo + GW]
            vec_ref[3:4, lo:lo + GW] += _rowsum(dyb_g[0:tm] * mixed)
            dmb = (dyb_g * ps[:, lo:lo + GW]).astype(BF16)
            dpw_ref[g] += lax.dot_general(pooled, dmb[0:tm], (((0,), (0,)), ((), ())), preferred_element_type=F32)
            dpool = lax.dot_general(dmb, pwb, (((1,), (1,)), ((), ())), preferred_element_type=F32)
            e3[0:ne, lo:lo + GW] = dpool / jnp.minimum(pos_e, float(w))
            back = e3[pl.ds(0, tm), lo:lo + GW]
            for j in range(1, w):
                back = back + e3[pl.ds(j, tm), lo:lo + GW]
            dp_ref[:, 1024 + lo:1024 + lo + GW] = (back - dpool[0:tm]).astype(BF16)

        cg = mcol(2048)
        hx = mcol(2560)
        e1[0:HB, :] = hbcol(2048) * hbcol(2560) * keep_b
        e1[HB:HB + tm, :] = cg * hx
        load_dy(2 * W)
        dyc = e2[0:tm, :]
        dconv = dyc * mcol(1536)
        e3[0:tm, :] = dconv
        e3[tm:ne, :] = e2[tm:ne, :] * hacol(1536)
        cv = sc[0:1, :] * e1[pl.ds(HB - 2, tm), :]
        for k in range(1, SC_K):
            cv = cv + sc[k:k + 1, :] * e1[pl.ds(HB - 2 + k, tm), :]
        dp_ref[:, 1536:2048] = (dyc * cv).astype(BF16)
        for k in range(SC_K):
            dsc_ref[k:k + 1, :] += _rowsum(dconv * e1[pl.ds(HB - 2 + k, tm), :])
        dq = sc[0:1, :] * e3[pl.ds(2, tm), :]
        for k in range(1, SC_K):
            dq = dq + sc[k:k + 1, :] * e3[pl.ds(2 - k, tm), :]
        dp_ref[:, 2048:2560] = (dq * hx).astype(BF16)
        dp_ref[:, 2560:3072] = (dq * cg).astype(BF16)

        u = mcol(3072)
        vh, vr = _ln_stats(mcol(3584))
        vn = (vh * gg[...] + gb[...]).astype(BF16)
        dyd = dy_ref[:, 3 * W:4 * W].astype(F32)
        tri = _tri_mask()
        for g in range(4):
            lo = g * GW
            wm = jnp.where(tri, ws[g], 0.0).astype(BF16)
            dws_g = jnp.zeros((GW, GW), F32)
            dbs_g = jnp.zeros((GW, 1), F32)
            for c in range(tm // GW):
                r0 = c * GW
                blk = vn[r0:r0 + GW, lo:lo + GW]
                sg = jnp.dot(wm, blk, preferred_element_type=F32) + bst[:, g:g + 1]
                dyd_b = dyd[r0:r0 + GW, lo:lo + GW]
                dp_ref[r0:r0 + GW, 3072 + lo:3072 + lo + GW] = (dyd_b * sg).astype(BF16)
                dsg = dyd_b * u[r0:r0 + GW, lo:lo + GW]
                dsgb = dsg.astype(BF16)
                dbs_g = dbs_g + jnp.sum(dsg, axis=-1, keepdims=True)
                dws_g = dws_g + lax.dot_general(dsgb, blk, (((1,), (1,)), ((), ())), preferred_element_type=F32)
                e1[r0:r0 + GW, lo:lo + GW] = lax.dot_general(wm, dsgb, (((0,), (0,)), ((), ())),
                                                             preferred_element_type=F32)
            dws_ref[g] += jnp.where(tri, dws_g, 0.0)
            dbs_ref[g] += jnp.broadcast_to(dbs_g, (GW, GW))
        dvn = e1[0:tm, :]
        vec_ref[4:5, :] += _rowsum(dvn * vh)
        vec_ref[5:6, :] += _rowsum(dvn)
        dp_ref[:, 3584:4096] = _ln_bwd(dvn * gg[...], vh, vr).astype(BF16)

    plist = _mixer_params(sp)
    in_specs = [pl.BlockSpec((tm, MIX_COLS), lambda i: (i, 0)),
                pl.BlockSpec((HB, MIX_COLS), lambda i: (jnp.maximum(i * per - 1, 0), 0)),
                pl.BlockSpec((HA, MIX_COLS), lambda i: (jnp.minimum((i + 1) * per, last_blk), 0)),
                pl.BlockSpec((tm, W), lambda i: (i, 0)),
                pl.BlockSpec((HA, W), lambda i: (jnp.minimum((i + 1) * per, last_blk), 0)),
                pl.BlockSpec((tm, 4 * W), lambda i: (i, 0)),
                pl.BlockSpec((HA, 4 * W), lambda i: (jnp.minimum((i + 1) * per, last_blk), 0))]
    in_specs += _small_specs(plist)
    in_specs += [pl.BlockSpec(memory_space=pl.ANY)]
    z2 = lambda i: (0, 0)
    z3 = lambda i: (0, 0, 0)
    out_specs = [pl.BlockSpec((tm, MIX_COLS), lambda i: (i, 0)),
                 pl.BlockSpec((32, W), z2), pl.BlockSpec((8, W), z2), pl.BlockSpec((8, W), z2),
                 pl.BlockSpec((4, GW, GW), z3), pl.BlockSpec((4, GW, GW), z3), pl.BlockSpec((4, GW, GW), z3)]
    out_shape = [jax.ShapeDtypeStruct((T, COLS_IN), BF16),
                 jax.ShapeDtypeStruct((32, W), F32), jax.ShapeDtypeStruct((8, W), F32),
                 jax.ShapeDtypeStruct((8, W), F32),
                 jax.ShapeDtypeStruct((4, GW, GW), F32), jax.ShapeDtypeStruct((4, GW, GW), F32),
                 jax.ShapeDtypeStruct((4, GW, GW), F32)]
    n_in = 7 + len(plist)
    return pl.pallas_call(
        body, name="b_mixers", grid=(nt,), in_specs=in_specs, out_specs=out_specs, out_shape=out_shape,
        scratch_shapes=[pltpu.VMEM((HB + ne, W), F32), pltpu.VMEM((ne, W), F32), pltpu.VMEM((ne, W), F32),
                        pltpu.VMEM((ne + SUBLANES, W), F32)],
        input_output_aliases={n_in: 0},
        compiler_params=_cp("arbitrary"),
    )(proj, proj, proj, ca_saved, ca_saved, dy, dy, *plist, dproj)


def _norm_first(x, g, tm=512):
    def body(x_ref, g_ref, o_ref):
        o_ref[...] = _rms(x_ref[...], g_ref[...]).astype(BF16)

    return pl.pallas_call(
        body, name="f_norm0", grid=(T // tm,),
        in_specs=[pl.BlockSpec((tm, D), lambda i: (i, 0)), pl.BlockSpec((1, D), lambda i: (0, 0))],
        out_specs=pl.BlockSpec((tm, D), lambda i: (i, 0)),
        out_shape=jax.ShapeDtypeStruct((T, D), BF16), compiler_params=_cp("arbitrary"),
    )(x, g)


def _loss_head(x, target, g, tm=256):
    def body(x_ref, t_ref, g_ref, dx_ref, dg_ref, loss_ref):
        i = pl.program_id(0)
        x = x_ref[...]
        r = lax.rsqrt(jnp.mean(x * x, axis=-1, keepdims=True) + EPS)
        xh = x * r
        gv = g_ref[...]
        e = xh * gv - t_ref[...]
        dyv = e * (1.0 / D)
        part = jnp.sum(_rowsum(e * e), axis=-1, keepdims=True) * (0.5 / D)
        u = dyv * gv
        dx_ref[...] = r * (u - xh * jnp.mean(u * xh, axis=-1, keepdims=True))
        dgp = _rowsum(dyv * xh)

        @pl.when(i == 0)
        def _():
            dg_ref[...] = dgp
            loss_ref[...] = jnp.broadcast_to(part, (1, GW))

        @pl.when(i > 0)
        def _():
            dg_ref[...] += dgp
            loss_ref[...] += jnp.broadcast_to(part, (1, GW))

    return pl.pallas_call(
        body, name="loss_head", grid=(T // tm,),
        in_specs=[pl.BlockSpec((tm, D), lambda i: (i, 0)), pl.BlockSpec((tm, D), lambda i: (i, 0)),
                  pl.BlockSpec((1, D), lambda i: (0, 0))],
        out_specs=[pl.BlockSpec((tm, D), lambda i: (i, 0)), pl.BlockSpec((1, D), lambda i: (0, 0)),
                   pl.BlockSpec((1, GW), lambda i: (0, 0))],
        out_shape=[jax.ShapeDtypeStruct((T, D), F32), jax.ShapeDtypeStruct((1, D), F32),
                   jax.ShapeDtypeStruct((1, GW), F32)],
        compiler_params=_cp("arbitrary"),
    )(x, target, g)


def _out_bwd(dx2b, merged, w_out8, tm=512):
    nt = T // tm

    def body(dx_ref, mg_ref, w_ref, dm_ref, dw_ref, acc):
        i = pl.program_id(0)
        dx = dx_ref[...]
        dm_ref[...] = lax.dot_general(dx, w_ref[...], (((1,), (1,)), ((), ())),
                                      preferred_element_type=F32).astype(BF16)
        part = lax.dot_general(mg_ref[...], dx, (((0,), (0,)), ((), ())), preferred_element_type=F32)

        @pl.when(i == 0)
        def _():
            acc[...] = part

        @pl.when(i > 0)
        def _():
            acc[...] += part

        @pl.when(i == nt - 1)
        def _():
            for j in range(NDEV):
                dw_ref[j] = acc[j * GW:(j + 1) * GW, :].astype(BF16)

    tile = pl.BlockSpec((tm, D), lambda i: (i, 0))
    return pl.pallas_call(
        body, name="b_out", grid=(nt,),
        in_specs=[tile, tile, pl.BlockSpec((D, D), lambda i: (0, 0))],
        out_specs=[tile, pl.BlockSpec((NDEV, GW, D), lambda i: (0, 0, 0))],
        out_shape=[jax.ShapeDtypeStruct((T, D), BF16), jax.ShapeDtypeStruct((NDEV, GW, D), BF16)],
        scratch_shapes=[pltpu.VMEM((D, D), F32)],
        compiler_params=_cp("arbitrary"),
    )(dx2b, merged, w_out8.reshape(D, D))


def _ple_bwd(dx4, sv, w_pleg8, g_ple, tm=512):
    nt = T // tm
    ple_dim = sv["p"].shape[1]

    def body(dx_ref, gl_ref, pe_ref, x_ref, h_ref, p_ref, g_ref, wg_ref,
             dx3_ref, dx3b_ref, dg_ref, dwg_ref, dwp_ref, acc_g, acc_p):
        i = pl.program_id(0)
        d = dx_ref[...]
        s = _sig(gl_ref[...].astype(F32))
        dpe = (d * s).astype(BF16)
        dgl = (d * pe_ref[...].astype(F32) * s * (1.0 - s)).astype(BF16)
        dh = lax.dot_general(dgl, wg_ref[...], (((1,), (1,)), ((), ())), preferred_element_type=F32)
        dx, dgp = _rms_bwd(dh, x_ref[...], g_ref[...], d)
        dx3_ref[...] = dx
        dx3b_ref[...] = dx.astype(BF16)
        part_g = lax.dot_general(h_ref[...], dgl, (((0,), (0,)), ((), ())), preferred_element_type=F32)
        part_p = lax.dot_general(p_ref[...], dpe, (((0,), (0,)), ((), ())), preferred_element_type=F32)

        @pl.when(i == 0)
        def _():
            dg_ref[...] = dgp
            acc_g[...] = part_g
            acc_p[...] = part_p

        @pl.when(i > 0)
        def _():
            dg_ref[...] += dgp
            acc_g[...] += part_g
            acc_p[...] += part_p

        @pl.when(i == nt - 1)
        def _():
            for j in range(NDEV):
                dwg_ref[j] = acc_g[j * GW:(j + 1) * GW, :].astype(BF16)
                dwp_ref[j] = acc_p[:, j * GW:(j + 1) * GW].astype(BF16)

    tile = lambda w: pl.BlockSpec((tm, w), lambda i: (i, 0))
    const = lambda shp: pl.BlockSpec(shp, lambda i: (0,) * len(shp))
    return pl.pallas_call(
        body, name="b_ple", grid=(nt,),
        in_specs=[tile(D), tile(D), tile(D), tile(D), tile(D), tile(ple_dim), const((1, D)), const((D, D))],
        out_specs=[tile(D), tile(D), const((1, D)), const((NDEV, GW, D)), const((NDEV, ple_dim, GW))],
        out_shape=[jax.ShapeDtypeStruct((T, D), F32), jax.ShapeDtypeStruct((T, D), BF16),
                   jax.ShapeDtypeStruct((1, D), F32), jax.ShapeDtypeStruct((NDEV, GW, D), BF16),
                   jax.ShapeDtypeStruct((NDEV, ple_dim, GW), BF16)],
        scratch_shapes=[pltpu.VMEM((D, D), F32), pltpu.VMEM((ple_dim, D), F32)],
        compiler_params=_cp("arbitrary"),
    )(dx4, sv["gl"], sv["pe"], sv["x3"], sv["h3"], sv["p"], g_ple, w_pleg8.reshape(D, D))


def _layer_fwd(x, h1, p_bf, gw, sp, g_next):
    proj, = _mm(h1, gw["w_in"], mode="out", name="f_proj", outs=[BF16], tm=T)
    y, ca = _mixer_fwd(proj, sp)
    z, gates, merged = _merge_fwd(y, proj, gw["w_branch"])
    x2, h2 = _mm(merged, gw["w_out"].reshape(1, D, D), mode="acc", name="f_out", outs=[F32, BF16], tm=T // 2,
                 tiles=[x], params=[sp["g_mlp"]], epi=_epi_res_norm)
    up, = _mm(h2, gw["w_up"], mode="out", name="f_up", outs=[BF16], tm=T)
    x3, h3 = _mm(up, gw["w_down"].reshape(1, 4 * D, D), mode="acc", name="f_down", outs=[F32, BF16], tm=T // 4,
                 tiles=[x2], params=[sp["g_ple"]], epi=_epi_res_norm, a_pre=_relu2_bf16)
    x4, gl, hn, pe = _mm(h3, gw["w_pleg"].reshape(1, D, D), mode="acc", name="f_gate", tm=T // 2,
                         outs=[F32, BF16, BF16, BF16], tiles=[x3, p_bf], params=[g_next, gw["w_ple"]], epi=_epi_ple)
    saved = dict(x=x, h1=h1, proj=proj, y=y, ca=ca, z=z, gates=gates, merged=merged, x2=x2, h2=h2, up=up, x3=x3, h3=h3,
                 pe=pe, gl=gl, p=p_bf)
    return x4, hn, saved


def _layer_bwd(dx4, sv, gw, sp, submit, early_group=False):
    dw = {}
    dx3, dx3b, dg_ple, dw["w_pleg"], dw["w_ple"] = _ple_bwd(dx4, sv, gw["w_pleg"], sp["g_ple"])
    dup, = _mm(dx3b, gw["w_down"], mode="out", trans_b=True, name="b_dact", outs=[BF16], tm=T,
               tiles=[sv["up"]], epi=_epi_dup)
    dw["w_down"] = _mm_tn(sv["up"], dx3b, nj=NDEV, split="row", name="b_dw_down", a_pre=_relu2_bf16)
    dw["w_up"] = _mm_tn(sv["h2"], dup, nj=NDEV, split="col", name="b_dw_up")
    if early_group:
        dw["w_up"], dup = lax.optimization_barrier((dw["w_up"], dup))
        dup = submit(dw, ("w_up", "w_down", "w_ple", "w_pleg"), dup)
    dx2, dx2b, dg_mlp = _mm(dup, gw["w_up"], mode="full", trans_b=True, name="b_dh2", tm=T // 8, stream_first=True,
                            outs=[F32, BF16], tiles=[sv["x2"], dx3], params=[sp["g_mlp"]], epi=_epi_rms_bwd, reds=[D])
    dm, dw["w_out"] = _out_bwd(dx2b, sv["merged"], gw["w_out"])
    dproj, dy, dw["w_branch"] = _merge_bwd(dm, sv["z"], sv["gates"], sv["y"], gw["w_branch"])
    dy = submit(dw, ("w_branch", "w_out") if early_group else BIG[1:], dy)
    dproj, dcw, dsc, vec, dpw, dws, dbs = _mixer_bwd(sv["proj"], sv["ca"], dy, dproj, sp)
    dw["w_in"] = _mm_tn(sv["h1"], dproj, nj=NDEV, split="col", name="b_dw_in")
    dw["w_in"], dproj = lax.optimization_barrier((dw["w_in"], dproj))
    dproj = submit(dw, BIG[:1], dproj)
    dx, dg_mix = _mm(dproj, gw["w_in"], mode="full", trans_b=True, name="b_dh1", outs=[F32], tm=T // 8,
                     stream_first=True,
                     tiles=[sv["x"], dx2], params=[sp["g_mix"]], epi=_epi_rms_bwd, reds=[D])
    small = dict(norm_mix=dg_mix[0], conf_dw=dcw[:CONF_K], conf_dw_b=vec[0], conf_ln_g=vec[1], conf_ln_b=vec[2],
                 pool_w=dpw, pool_scale=vec[3], sc_conv=dsc[:SC_K], gmlp_ln_g=vec[4], gmlp_ln_b=vec[5],
                 gmlp_ws=dws, gmlp_bs=dbs[:, :, 0], norm_mlp=dg_mlp[0], norm_ple=dg_ple[0])
    return dx, small


ANY = pl.BlockSpec(memory_space=pl.ANY)


def _mesh_pos():
    return lax.axis_index("x"), lax.axis_index("y"), lax.axis_index("c")


def _other_chips(x, y):
    return [(1 - x, y), (x, 1 - y), (1 - x, 1 - y)]


def _launch_comm(body, peers_of, operands, out_shapes, sems, name, seq_id):
    n_in, n_out = len(operands), len(out_shapes)
    if seq_id is None:
        return pl.pallas_call(body, name=name, in_specs=[ANY] * n_in, out_specs=[ANY] * n_out,
                              out_shape=out_shapes, scratch_shapes=sems)(*operands)

    def seq_body(*refs):
        peers = peers_of(*_mesh_pos())
        barrier = pltpu.get_barrier_semaphore()
        for peer in peers:
            pl.semaphore_signal(barrier, inc=1, device_id=peer, device_id_type=MESH)
        pl.semaphore_wait(barrier, len(peers))
        body(*refs)

    return pl.kernel(seq_body, name=name, out_type=out_shapes,
                     mesh=plsc.ScalarSubcoreMesh(axis_name="seq", num_cores=1), scratch_types=sems,
                     compiler_params=pltpu.CompilerParams(collective_id=seq_id))(*operands)


def _all_gather(shards, name, seq_id=None):
    n = len(shards)

    def body(*refs):
        s_refs, o_refs = refs[:n], refs[n:2 * n]
        send_sems, recv_sems, local_sems = refs[2 * n:]
        x, y, c = _mesh_pos()
        me = 4 * x + 2 * y + c
        here = (x, y, c)
        sibling = (x, y, 1 - c)
        chips = _other_chips(x, y)

        def slot(px, py, pc):
            return 4 * px + 2 * py + pc

        def copy(t, k, slot_idx, to, src=None):
            dst = o_refs[t].at[slot_idx]
            return pltpu.make_async_remote_copy(
                src_ref=dst if src is None else src, dst_ref=dst,
                send_sem=send_sems.at[t * 7 + k], recv_sem=recv_sems.at[t * 7 + k],
                device_id=to, device_id_type=MESH)

        mine = [pltpu.make_async_copy(s_refs[t], o_refs[t].at[me], local_sems.at[t]) for t in range(n)]
        for cp in mine:
            cp.start()
        first = []
        for t in range(n):
            for j, chip in enumerate(chips):
                first.append(copy(t, 1 + j, me, (*chip, c), src=s_refs[t]))
        for t in range(n):
            first.append(copy(t, 0, me, sibling, src=s_refs[t]))
        for cp in first:
            cp.start()
        passed = []
        for t in range(n):
            for j, chip in enumerate(chips):
                copy(t, 1 + j, slot(*chip, c), here).wait_recv()
                fwd = copy(t, 4 + j, slot(*chip, c), sibling)
                fwd.start()
                passed.append(fwd)
        for t in range(n):
            copy(t, 0, slot(x, y, 1 - c), here).wait_recv()
            for j, chip in enumerate(chips):
                copy(t, 4 + j, slot(*chip, 1 - c), here).wait_recv()
        for cp in first + passed:
            cp.wait_send()
        for cp in mine:
            cp.wait()

    def peers_of(x, y, c):
        return [(x, y, 1 - c)] + [(*chip, c) for chip in _other_chips(x, y)]

    return _launch_comm(
        body, peers_of, shards, [jax.ShapeDtypeStruct((NDEV,) + s.shape, s.dtype) for s in shards],
        [pltpu.SemaphoreType.DMA((7 * n,)), pltpu.SemaphoreType.DMA((7 * n,)), pltpu.SemaphoreType.DMA((n,))],
        name, seq_id)


def _rs_exchange(p4s, qs, name, seq_id=None):
    n_p, n_q = len(p4s), len(qs)

    def body(*refs):
        p_refs, q_refs = refs[:n_p], refs[n_p:n_p + n_q]
        rb_refs, rc_refs = refs[n_p + n_q:2 * n_p + n_q], refs[2 * n_p + n_q:2 * (n_p + n_q)]
        pair_send, pair_recv, chip_send, chip_recv, local_sems = refs[2 * (n_p + n_q):]
        x, y, c = _mesh_pos()
        a_idx = 2 * x + y
        chips = _other_chips(x, y)
        mine = [pltpu.make_async_copy(q_refs[t].at[a_idx], rc_refs[t].at[a_idx], local_sems.at[t])
                for t in range(n_q)]
        sends = []
        for t in range(n_q):
            for j, chip in enumerate(chips):
                sends.append(pltpu.make_async_remote_copy(
                    src_ref=q_refs[t].at[2 * chip[0] + chip[1]], dst_ref=rc_refs[t].at[a_idx],
                    send_sem=chip_send.at[t * 3 + j], recv_sem=chip_recv.at[t * 3 + j],
                    device_id=(*chip, c), device_id_type=MESH))
        pairs = [pltpu.make_async_remote_copy(
            src_ref=p_refs[t].at[:, 1 - c], dst_ref=rb_refs[t], send_sem=pair_send.at[t], recv_sem=pair_recv.at[t],
            device_id=(x, y, 1 - c), device_id_type=MESH) for t in range(n_p)]
        for cp in sends + mine + pairs:
            cp.start()
        for cp in pairs:
            cp.wait()
        for t in range(n_q):
            for j, chip in enumerate(chips):
                landed = rc_refs[t].at[2 * chip[0] + chip[1]]
                pltpu.make_async_remote_copy(
                    src_ref=landed, dst_ref=landed, send_sem=chip_send.at[t * 3 + j],
                    recv_sem=chip_recv.at[t * 3 + j], device_id=(x, y, c), device_id_type=MESH).wait_recv()
        for cp in sends:
            cp.wait_send()
        for cp in mine:
            cp.wait()

    def peers_of(x, y, c):
        peers = [(x, y, 1 - c)] if n_p else []
        return peers + ([(*chip, c) for chip in _other_chips(x, y)] if n_q else [])

    out_shapes = [jax.ShapeDtypeStruct((NCHIP,) + p.shape[2:], p.dtype) for p in p4s]
    out_shapes += [jax.ShapeDtypeStruct(q.shape, q.dtype) for q in qs]
    sems = [pltpu.SemaphoreType.DMA((max(n_p, 1),)), pltpu.SemaphoreType.DMA((max(n_p, 1),)),
            pltpu.SemaphoreType.DMA((max(3 * n_q, 1),)), pltpu.SemaphoreType.DMA((max(3 * n_q, 1),)),
            pltpu.SemaphoreType.DMA((max(n_q, 1),))]
    got = _launch_comm(body, peers_of, list(p4s) + list(qs), out_shapes, sems, name, seq_id)
    return got[:n_p], got[n_p:]


def _pair_sum(p4s, rbs, c_idx, name, nst=2):
    n = len(p4s)
    trs = [p.shape[2] // nst for p in p4s]

    def body(c_ref, *refs):
        del c_ref
        p_refs, r_refs, o_refs = refs[:n], refs[n:2 * n], refs[2 * n:]
        for p_ref, r_ref, o_ref in zip(p_refs, r_refs, o_refs):
            o_ref[...] = (p_ref[...].astype(F32) + r_ref[...].astype(F32)).astype(o_ref.dtype)

    in_specs = [pl.BlockSpec((None, None, tr, p.shape[3]), lambda b, i, c_ref: (b, c_ref[0], i, 0))
                for p, tr in zip(p4s, trs)]
    in_specs += [pl.BlockSpec((None, tr, p.shape[3]), lambda b, i, c_ref: (b, i, 0)) for p, tr in zip(p4s, trs)]
    out_specs = [pl.BlockSpec((None, tr, p.shape[3]), lambda b, i, c_ref: (b, i, 0)) for p, tr in zip(p4s, trs)]
    return pl.pallas_call(
        body, name=name,
        grid_spec=pltpu.PrefetchScalarGridSpec(num_scalar_prefetch=1, grid=(NCHIP, nst), in_specs=in_specs,
                                               out_specs=out_specs),
        out_shape=[jax.ShapeDtypeStruct((NCHIP,) + p.shape[2:], p.dtype) for p in p4s],
        compiler_params=_cp("arbitrary", "arbitrary"),
    )(c_idx, *p4s, *rbs)


class _GradientPipeline:
    def __init__(self, c_idx, results):
        self.c_idx, self.results, self.pending = c_idx, results, None

    def _sum_pending(self, chain):
        names, layer, p4s, rbs = self.pending
        qs = _pair_sum(p4s, rbs, self.c_idx, name="rs_pairsum_%d" % len(names))
        return lax.optimization_barrier((chain, qs))

    def submit(self, dw, names, layer, chain):
        qs, tag, seq_id = [], "pair", 3
        if self.pending is not None:
            chain, qs = self._sum_pending(chain)
            tag, seq_id = "pair_chip", 4
        p4s = [dw[n].reshape((NCHIP, 2) + BIG_SHARD[n]) for n in names]
        rbs, rcs = _rs_exchange(p4s, qs, name="rs_%s_%d" % (tag, len(names)), seq_id=seq_id)
        self._record(rcs)
        self.pending = (names, layer, p4s, rbs)
        return chain

    def finish(self, chain):
        chain, qs = self._sum_pending(chain)
        self._record(_rs_exchange([], qs, name="rs_chip_last", seq_id=5)[1])
        self.pending = None
        return chain

    def _record(self, rcs):
        if rcs:
            names, layer = self.pending[:2]
            for n, rc in zip(names, rcs):
                self.results[n][layer] = rc


def _adamw(w, g, m, v):
    m = ADAM_B1 * m + (1.0 - ADAM_B1) * g
    v = ADAM_B2 * v + (1.0 - ADAM_B2) * (g * g)
    m_hat = m / (1.0 - ADAM_B1 ** ADAM_STEP)
    v_hat = v / (1.0 - ADAM_B2 ** ADAM_STEP)
    delta = -ADAM_LR * (m_hat / (jnp.sqrt(v_hat) + ADAM_EPS) + ADAM_WD * w)
    return delta, m, v


def _adam_sharded(rcs, w, m, v, tr, name, first_layer, partial=None):
    _, r, c = w.shape
    nst = r // tr
    n_l = len(rcs)

    def body(*refs):
        rc_refs = refs[:n_l]
        w_ref, m_ref, v_ref = refs[n_l:n_l + 3]
        g_out, d_out, m_out, v_out = refs[-4:]
        layer = pl.program_id(0)
        for k, rc in enumerate(rc_refs):
            @pl.when(layer == k)
            def _():
                g = rc[0].astype(F32) + rc[1].astype(F32) + rc[2].astype(F32) + rc[3].astype(F32)
                delta, m_new, v_new = _adamw(w_ref[...], g, m_ref[...], v_ref[...])
                g_out[...] = g
                d_out[...] = delta
                m_out[...] = m_new
                v_out[...] = v_new

    rc_specs = [pl.BlockSpec((NCHIP, tr, c), lambda l, i, k=k: (0, jnp.where(l == k, i, 0), 0)) for k in range(n_l)]
    wspec = pl.BlockSpec((None, tr, c), lambda l, i: (first_layer + l, i, 0))
    carried = [] if partial is None else list(partial)
    return pl.pallas_call(
        body, name=name, grid=(n_l, nst),
        in_specs=rc_specs + [wspec] * 3 + [pl.BlockSpec(memory_space=pl.ANY)] * len(carried),
        out_specs=[wspec] * 4, out_shape=[jax.ShapeDtypeStruct(w.shape, F32)] * 4,
        input_output_aliases={n_l + 3 + k: k for k in range(len(carried))},
        compiler_params=_cp("arbitrary", "arbitrary"),
    )(*rcs, w, m, v, *carried)


def _adam_packed(g, w, m, v, direct):
    n_d = len(direct)

    def pieces(shape):
        width = shape[-1]
        count = 1
        for s in shape[:-1]:
            count *= s
        per_row = D // width
        out = []
        for k in range(count):
            idx = (k,) if len(shape) == 2 else (k // shape[1], k % shape[1])
            out.append((idx, k // per_row, (k % per_row) * width, width))
        return out

    def body(g_ref, w_ref, m_ref, v_ref, d_out, m_out, v_out, *outs):
        delta, m_new, v_new = _adamw(w_ref[...], g_ref[...], m_ref[...], v_ref[...])
        d_out[...] = delta
        m_out[...] = m_new
        v_out[...] = v_new
        for a, (_, row0, shape) in enumerate(direct):
            for src, dst in zip((g_ref, d_out, m_out, v_out), outs[4 * a:4 * a + 4]):
                for idx, row, lane0, width in pieces(shape):
                    piece = src[pl.ds(row0 + row, 1), lane0:lane0 + width]
                    if len(idx) == 1:
                        dst[pl.ds(idx[0], 1), :] = piece
                    else:
                        dst[idx[0], pl.ds(idx[1], 1), :] = piece

    out_shape = [jax.ShapeDtypeStruct(g.shape, F32)] * 3
    for _, _, shape in direct:
        out_shape += [jax.ShapeDtypeStruct(shape, F32)] * 4
    res = pl.pallas_call(body, name="adam_small", out_shape=out_shape,
                         compiler_params=pltpu.CompilerParams(vmem_limit_bytes=VMEM_LIMIT_BYTES))(g, w, m, v)
    return res[:3], {name: res[3 + 4 * a:7 + 4 * a] for a, (name, _, _) in enumerate(direct)}


def _sum4(rc):
    def body(rc_ref, o_ref):
        o_ref[...] = rc_ref[0] + rc_ref[1] + rc_ref[2] + rc_ref[3]

    return pl.pallas_call(
        body, name="small_sum", out_shape=jax.ShapeDtypeStruct(rc.shape[1:], F32),
    )(rc)


BIG = ("w_in", "w_branch", "w_out", "w_up", "w_down", "w_ple", "w_pleg")
BIG_SHARD = {"w_in": (D, D), "w_branch": (4 * W, GW), "w_out": (GW, D), "w_up": (D, W), "w_down": (W, D),
             "w_ple": (256, GW), "w_pleg": (GW, D)}
ADAM_ROWS = {"w_in": 256, "w_branch": 512, "w_out": 128, "w_up": 256, "w_down": 256, "w_ple": 256, "w_pleg": 128}
SMALL = (("norm_mix", (DEPTH, D)), ("conf_dw", (DEPTH, CONF_K, W)), ("conf_dw_b", (DEPTH, W)),
         ("conf_ln_g", (DEPTH, W)), ("conf_ln_b", (DEPTH, W)), ("pool_w", (DEPTH, 4, GW, GW)),
         ("pool_scale", (DEPTH, W)), ("sc_conv", (DEPTH, SC_K, W)), ("gmlp_ln_g", (DEPTH, W)),
         ("gmlp_ln_b", (DEPTH, W)), ("gmlp_ws", (DEPTH, 4, GW, GW)), ("gmlp_bs", (DEPTH, 4, GW)),
         ("norm_mlp", (DEPTH, D)), ("norm_ple", (DEPTH, D)), ("norm_final", (D,)))
CHANNEL_SHARDED = ("conf_dw", "sc_conv")
SMALL_ROWS = 80


def _pack(arrs, rows):
    flat = jnp.concatenate([a.reshape(-1) for a in arrs])
    return jnp.pad(flat, (0, rows * D - flat.shape[0])).reshape(rows, D)


def _unpack(packed, shapes):
    flat = packed.reshape(-1)
    out, off = [], 0
    for shp in shapes:
        size = 1
        for s in shp:
            size *= s
        out.append(flat[off:off + size].reshape(shp))
        off += size
    return out


def kernel(x, p, norm_mix, w_in, conf_dw, conf_dw_b, conf_ln_g, conf_ln_b, pool_w, pool_scale, sc_conv, gmlp_ln_g, gmlp_ln_b, gmlp_ws, gmlp_bs, w_branch, w_out, norm_mlp, w_up, w_down, norm_ple, w_ple, w_ple_gate, norm_final, loss_target, m_norm_mix, m_w_in, m_conf_dw, m_conf_dw_b, m_conf_ln_g, m_conf_ln_b, m_pool_w, m_pool_scale, m_sc_conv, m_gmlp_ln_g, m_gmlp_ln_b, m_gmlp_ws, m_gmlp_bs, m_w_branch, m_w_out, m_norm_mlp, m_w_up, m_w_down, m_norm_ple, m_w_ple, m_w_ple_gate, m_norm_final, v_norm_mix, v_w_in, v_conf_dw, v_conf_dw_b, v_conf_ln_g, v_conf_ln_b, v_pool_w, v_pool_scale, v_sc_conv, v_gmlp_ln_g, v_gmlp_ln_b, v_gmlp_ws, v_gmlp_bs, v_w_branch, v_w_out, v_norm_mlp, v_w_up, v_w_down, v_norm_ple, v_w_ple, v_w_ple_gate, v_norm_final):
    weights = dict(norm_mix=norm_mix, w_in=w_in, conf_dw=conf_dw, conf_dw_b=conf_dw_b, conf_ln_g=conf_ln_g,
                   conf_ln_b=conf_ln_b, pool_w=pool_w, pool_scale=pool_scale, sc_conv=sc_conv, gmlp_ln_g=gmlp_ln_g,
                   gmlp_ln_b=gmlp_ln_b, gmlp_ws=gmlp_ws, gmlp_bs=gmlp_bs, w_branch=w_branch, w_out=w_out,
                   norm_mlp=norm_mlp, w_up=w_up, w_down=w_down, norm_ple=norm_ple, w_ple=w_ple, w_pleg=w_ple_gate,
                   norm_final=norm_final)
    mom1 = dict(norm_mix=m_norm_mix, w_in=m_w_in, conf_dw=m_conf_dw, conf_dw_b=m_conf_dw_b, conf_ln_g=m_conf_ln_g,
                conf_ln_b=m_conf_ln_b, pool_w=m_pool_w, pool_scale=m_pool_scale, sc_conv=m_sc_conv,
                gmlp_ln_g=m_gmlp_ln_g, gmlp_ln_b=m_gmlp_ln_b, gmlp_ws=m_gmlp_ws, gmlp_bs=m_gmlp_bs,
                w_branch=m_w_branch, w_out=m_w_out, norm_mlp=m_norm_mlp, w_up=m_w_up, w_down=m_w_down,
                norm_ple=m_norm_ple, w_ple=m_w_ple, w_pleg=m_w_ple_gate, norm_final=m_norm_final)
    mom2 = dict(norm_mix=v_norm_mix, w_in=v_w_in, conf_dw=v_conf_dw, conf_dw_b=v_conf_dw_b, conf_ln_g=v_conf_ln_g,
                conf_ln_b=v_conf_ln_b, pool_w=v_pool_w, pool_scale=v_pool_scale, sc_conv=v_sc_conv,
                gmlp_ln_g=v_gmlp_ln_g, gmlp_ln_b=v_gmlp_ln_b, gmlp_ws=v_gmlp_ws, gmlp_bs=v_gmlp_bs,
                w_branch=v_w_branch, w_out=v_w_out, norm_mlp=v_norm_mlp, w_up=v_w_up, w_down=v_w_down,
                norm_ple=v_norm_ple, w_ple=v_w_ple, w_pleg=v_w_ple_gate, norm_final=v_norm_final)

    xi, yi, ci = _mesh_pos()
    me = 4 * xi + 2 * yi + ci
    c_idx = jnp.reshape(ci, (1,)).astype(jnp.int32)

    gathered, conf_full, sc_full = [], [], []
    for l in range(DEPTH):
        shard = lambda n: weights[n][l].astype(BF16).reshape(BIG_SHARD[n])
        w_in_g, conf_g, sc_g = _all_gather([shard("w_in"), conf_dw[l], sc_conv[l]], name="ag_first", seq_id=1)
        if l + 1 < DEPTH:
            rest = _all_gather([shard(n) for n in BIG[1:]], name="ag_rest", seq_id=2)
        else:
            rest = (list(_all_gather([shard(n) for n in BIG[1:4]], name="ag_rest_a", seq_id=2))
                    + list(_all_gather([shard(n) for n in BIG[4:]], name="ag_rest_b", seq_id=2)))
        gw = dict(zip(BIG[1:], rest), w_in=w_in_g)
        gw["w_branch"] = gw["w_branch"].reshape(NDEV, 4, W, GW)
        gathered.append(gw)
        conf_full.append(conf_g)
        sc_full.append(sc_g)

    def small_params(l):
        return dict(cw=conf_full[l], cb=conf_dw_b[l][None], lg=conf_ln_g[l][None], lb=conf_ln_b[l][None],
                    pw=pool_w[l], ps=pool_scale[l][None], sc=sc_full[l], gg=gmlp_ln_g[l][None],
                    gb=gmlp_ln_b[l][None], ws=gmlp_ws[l], bst=gmlp_bs[l].T, g_mix=norm_mix[l][None],
                    g_mlp=norm_mlp[l][None], g_ple=norm_ple[l][None])

    xc = x.reshape(T, D)
    small_names = [n for n, _ in SMALL]

    def in_gradient_layout(n, shard, shape):
        if n not in CHANNEL_SHARDED:
            return shard
        return lax.dynamic_update_slice(jnp.zeros(shape, F32), shard, (0, 0, me * (W // NDEV)))

    small_state = [_pack([in_gradient_layout(n, src[n], shape) for n, shape in SMALL], NDEV * SMALL_ROWS)
                   for src in (weights, mom1, mom2)]
    xc, small_state = lax.optimization_barrier((xc, small_state))
    p_bf = p.reshape(DEPTH, T, 256).astype(BF16)
    h = _norm_first(xc, norm_mix[0][None])
    saved = []
    for l in range(DEPTH):
        g_next = norm_mix[l + 1][None] if l + 1 < DEPTH else norm_final[None]
        h, conf_g, sc_g = lax.optimization_barrier((h, conf_full[l], sc_full[l]))
        conf_full[l] = conf_g.transpose(1, 0, 2).reshape(CONF_K, W)
        sc_full[l] = sc_g.transpose(1, 0, 2).reshape(SC_K, W)
        xc, h, sv = _layer_fwd(xc, h, p_bf[l], gathered[l], small_params(l), g_next)
        saved.append(sv)

    dxc, dg_final, loss_part = _loss_head(xc, loss_target.reshape(T, D), norm_final[None])
    loss = lax.psum(loss_part[0, 0], ("x", "y", "c"))
    small_grads = [None] * DEPTH
    rcs = {n: [None] * DEPTH for n in BIG}
    pipeline = _GradientPipeline(c_idx, rcs)
    for l in reversed(range(DEPTH)):
        dxc, small_grads[l] = _layer_bwd(dxc, saved[l], gathered[l], small_params(l),
                                         lambda dw, names, value, l=l: pipeline.submit(dw, names, l, value),
                                         early_group=(l == 0))

    def adam_sharded(first_layer, n_layers, partial, tag, names=BIG):
        outs = {}
        for n in names:
            shp = (DEPTH,) + BIG_SHARD[n]
            outs[n] = _adam_sharded(rcs[n][first_layer:first_layer + n_layers], weights[n].reshape(shp),
                                    mom1[n].reshape(shp), mom2[n].reshape(shp), ADAM_ROWS[n],
                                    "adam_%s_%s" % (n, tag), first_layer, None if partial is None else partial[n])
        return outs

    stacked = {n: jnp.stack([small_grads[l][n] for l in range(DEPTH)]) for n, _ in SMALL if n != "norm_final"}
    stacked["norm_final"] = dg_final[0]
    packed = _pack([stacked[n] for n, _ in SMALL], NDEV * SMALL_ROWS).reshape(NCHIP, 2, SMALL_ROWS, D)
    (pair_small,), _ = _rs_exchange([packed], [], name="rs_pair_small")
    q_small = _pair_sum([packed], [pair_small], c_idx, name="rs_pairsum_small")
    dxc, upper, q_small = lax.optimization_barrier((dxc, {n: rcs[n][1:] for n in BIG}, q_small))
    _, (chips_small,) = _rs_exchange([], q_small, name="rs_chip_small", seq_id=6)
    dxc, upper = pipeline.finish((dxc, upper))
    for n in BIG:
        rcs[n][1:] = upper[n]
    partial = adam_sharded(1, DEPTH - 1, None, "upper")
    last = adam_sharded(0, 1, partial, "last", names=BIG[:1])
    partial = {n: partial[n] for n in BIG[1:]}
    last, partial, chips_small = lax.optimization_barrier((last, partial, chips_small))
    reduced_slot = _sum4(chips_small)
    reduced = _all_gather([reduced_slot], name="ag_small", seq_id=7)[0]
    small_full = dict(zip([n for n, _ in SMALL], _unpack(reduced, [s for _, s in SMALL])))
    grads, deltas, new_m, new_v = {}, {}, {}, {}
    direct, row = [], 0
    for n, shape in SMALL:
        if len(shape) == 1 or shape[-2] == DEPTH:
            direct.append((n, row, (1,) * (2 - len(shape)) + tuple(shape)))
        size = 1
        for s in shape:
            size *= s
        row += size // D
    (d_p, m_p, v_p), own_shape = _adam_packed(reduced.reshape(NDEV * SMALL_ROWS, D), *small_state, direct)
    small_shapes = [s for _, s in SMALL]

    def own_channels(n, full):
        return lax.dynamic_slice_in_dim(full, me * (W // NDEV), W // NDEV, axis=2) if n in CHANNEL_SHARDED else full

    for n, d_, m_, v_ in zip(small_names, _unpack(d_p, small_shapes), _unpack(m_p, small_shapes),
                             _unpack(v_p, small_shapes)):
        if n in own_shape:
            grads[n], deltas[n], new_m[n], new_v[n] = [a.reshape(weights[n].shape) for a in own_shape[n]]
        else:
            grads[n], deltas[n], new_m[n], new_v[n] = (own_channels(n, small_full[n]), own_channels(n, d_),
                                                       own_channels(n, m_), own_channels(n, v_))

    last.update(adam_sharded(0, 1, partial, "last", names=BIG[1:]))
    for n, (g_, d_, m_, v_) in last.items():
        full = weights[n].shape
        grads[n], deltas[n], new_m[n], new_v[n] = g_.reshape(full), d_.reshape(full), m_.reshape(full), v_.reshape(full)

    order = ("norm_mix", "w_in", "conf_dw", "conf_dw_b", "conf_ln_g", "conf_ln_b", "pool_w", "pool_scale", "sc_conv",
             "gmlp_ln_g", "gmlp_ln_b", "gmlp_ws", "gmlp_bs", "w_branch", "w_out", "norm_mlp", "w_up", "w_down",
             "norm_ple", "w_ple", "w_pleg", "norm_final")
    return (loss, dxc.reshape(1, T, D), *[grads[n] for n in order], *[deltas[n] for n in order],
            *[new_m[n] for n in order], *[new_v[n] for n in order])
```

```python
import functools

import jax
import jax.numpy as jnp
import numpy as np
from jax import lax
from jax.experimental import pallas as pl
from jax.experimental.pallas import tpu as pltpu
from jax.experimental.pallas import tpu_sc as plsc

F32 = jnp.float32
BF16 = jnp.bfloat16

DEPTH = 4
T = 2048
D = 1024
W = 512
NDEV = 8
NCHIP = 4
EPS = 1e-6
CONF_K = 31
SC_K = 3
POOL_WINDOWS = (2, 4, 8, 16)
GW = 128
HB = 32
HA = 32
COLS_IN = 8192
MIX_COLS = 4096

ADAM_LR = 0.001
ADAM_B1 = 0.9
ADAM_B2 = 0.999
ADAM_EPS = 1e-08
ADAM_WD = 0.01
ADAM_STEP = 10

VMEM_LIMIT_BYTES = 56 * 1024 * 1024
MESH = pl.DeviceIdType.MESH


def _cp(*sem):
    return pltpu.CompilerParams(dimension_semantics=tuple(sem), vmem_limit_bytes=VMEM_LIMIT_BYTES)


def _sig(x):
    return jax.nn.sigmoid(x)


def _rms(x, g):
    r = lax.rsqrt(jnp.mean(x * x, axis=-1, keepdims=True) + EPS)
    return x * r * g


def _rms_bwd(dh, x, g, dres):
    r = lax.rsqrt(jnp.mean(x * x, axis=-1, keepdims=True) + EPS)
    xh = x * r
    u = dh * g
    dx = r * (u - xh * jnp.mean(u * xh, axis=-1, keepdims=True)) + dres
    dg = jnp.sum(dh * xh, axis=0, keepdims=True)
    return dx, dg


def _ln_stats(x):
    mu = jnp.mean(x, axis=-1, keepdims=True)
    xc = x - mu
    rstd = lax.rsqrt(jnp.mean(xc * xc, axis=-1, keepdims=True) + EPS)
    return xc * rstd, rstd


def _ln_bwd(dxh, xh, rstd):
    return rstd * (dxh - jnp.mean(dxh, axis=-1, keepdims=True) - xh * jnp.mean(dxh * xh, axis=-1, keepdims=True))


def _rowsum(x):
    return jnp.sum(x, axis=0, keepdims=True)


EPI_ROWS = 256


def _relu2_bf16(up):
    r = jnp.maximum(up.astype(F32), 0.0)
    return (r * r).astype(BF16)


def _mm(a, b3, *, mode, name, outs, trans_b=False, tm=512, tiles=(), params=(), epi=None, reds=(), a_pre=None,
        stream_first=False):
    t_, ka = a.shape
    nj, r, c = b3.shape
    kb, nb = (c, r) if trans_b else (r, c)
    nt = t_ // tm
    out_mode = mode == "out"
    full = mode == "full"
    assert trans_b or not full
    if out_mode:
        assert ka == kb and not reds
        grid = (nj, nt)
        a_map = lambda g0, g1: (g1, 0)
        b_map = lambda g0, g1: (g0, 0, 0)
        t_map = lambda g0, g1: (g1, g0)
        width = nj * nb
    else:
        assert ka == nj * kb
        grid = (nt, 1 if full else nj)
        a_map = lambda g0, g1: (g0, g1)
        b_map = lambda g0, g1: (g1, 0, 0)
        t_map = lambda g0, g1: (g0, 0)
        width = nb
    n_t, n_p, n_o, n_r = len(tiles), len(params), len(outs), len(reds)
    use_acc = (not out_mode) and nj > 1 and not full
    dims = (((1,), (1,)), ((), ())) if trans_b else (((1,), (0,)), ((), ()))

    def body(a_ref, b_ref, *rest):
        t_refs = rest[:n_t]
        p_refs = rest[n_t:n_t + n_p]
        o_refs = rest[n_t + n_p:n_t + n_p + n_o]
        r_refs = rest[n_t + n_p + n_o:n_t + n_p + n_o + n_r]
        i = pl.program_id(1 if out_mode else 0)
        a_val = a_ref[...] if a_pre is None else a_pre(a_ref[...])
        if full:
            b_all, b_sems = rest[-2 - stream_first], rest[-1 - stream_first]

            def weight_copies():
                return [pltpu.make_async_copy(b_ref.at[j], b_all.at[:, j * c:(j + 1) * c], b_sems.at[j])
                        for j in range(nj)]

            if stream_first:
                part = rest[-1]

                @pl.when(i == 0)
                def _():
                    cps = weight_copies()
                    for cp in cps:
                        cp.start()
                    acc = None
                    for j, cp in enumerate(cps):
                        cp.wait()
                        term = lax.dot_general(a_val[:, j * c:(j + 1) * c], b_all[:, j * c:(j + 1) * c], dims,
                                               preferred_element_type=F32)
                        acc = term if acc is None else acc + term
                    part[...] = acc

                @pl.when(i > 0)
                def _():
                    part[...] = lax.dot_general(a_val, b_all[...], dims, preferred_element_type=F32)
            else:
                @pl.when(i == 0)
                def _():
                    cps = weight_copies()
                    for cp in cps:
                        cp.start()
                    for cp in cps:
                        cp.wait()

                part = lax.dot_general(a_val, b_all[...], dims, preferred_element_type=F32)
        else:
            part = lax.dot_general(a_val, b_ref[...], dims, preferred_element_type=F32)

        def finish(acc_rows):
            totals = [None] * n_r
            for r0 in range(0, tm, min(tm, EPI_ROWS)):
                rows = slice(r0, r0 + min(tm, EPI_ROWS))
                if epi is None:
                    res, rr = (acc_rows(rows),), ()
                else:
                    res, rr = epi(acc_rows(rows), [t[rows, :] for t in t_refs], [p[...] for p in p_refs])
                for o_ref, val in zip(o_refs, res):
                    o_ref[rows, :] = val.astype(o_ref.dtype)
                totals = [val if tot is None else tot + val for tot, val in zip(totals, rr)]
            for r_ref, val in zip(r_refs, totals):
                @pl.when(i == 0)
                def _():
                    r_ref[...] = val

                @pl.when(i > 0)
                def _():
                    r_ref[...] += val

        if use_acc:
            acc_ref = rest[-1]
            j = pl.program_id(1)

            @pl.when(j == 0)
            def _():
                acc_ref[...] = part

            @pl.when(jnp.logical_and(j > 0, j < nj - 1))
            def _():
                acc_ref[...] += part

            @pl.when(j == nj - 1)
            def _():
                finish(lambda rows: acc_ref[rows, :] + part[rows])
        else:
            finish(lambda rows: part[rows, :])

    const2 = lambda g0, g1: (0, 0)
    if full:
        in_specs = [pl.BlockSpec((tm, ka), a_map), pl.BlockSpec(memory_space=pl.ANY)]
        scratch = [pltpu.VMEM((r, nj * c), b3.dtype), pltpu.SemaphoreType.DMA((nj,))]
        scratch += [pltpu.VMEM((tm, nb), F32)] if stream_first else []
    else:
        in_specs = [pl.BlockSpec((tm, kb), a_map), pl.BlockSpec((None, r, c), b_map)]
        scratch = [pltpu.VMEM((tm, nb), F32)] if use_acc else []
    in_specs += [pl.BlockSpec((tm, t.shape[1] // nj if out_mode else t.shape[1]), t_map) for t in tiles]
    in_specs += [pl.BlockSpec(p.shape, lambda g0, g1, nd=p.ndim: (0,) * nd) for p in params]
    out_specs = [pl.BlockSpec((tm, nb), t_map) for _ in outs] + [pl.BlockSpec((1, w), const2) for w in reds]
    out_shape = [jax.ShapeDtypeStruct((t_, width), dt) for dt in outs]
    out_shape += [jax.ShapeDtypeStruct((1, w), F32) for w in reds]
    res = pl.pallas_call(
        body, name=name, grid=grid, in_specs=in_specs, out_specs=out_specs, out_shape=out_shape,
        scratch_shapes=scratch, compiler_params=_cp("arbitrary", "arbitrary"),
    )(a, b3, *tiles, *params)
    return res


def _mm_tn(a, g, *, nj, split, name, out_dtype=BF16, a_pre=None):
    t_ = a.shape[0]
    if split == "col":
        r, c = a.shape[1], g.shape[1] // nj
        a_spec = pl.BlockSpec((t_, r), lambda j: (0, 0))
        g_spec = pl.BlockSpec((t_, c), lambda j: (0, j))
    else:
        r, c = a.shape[1] // nj, g.shape[1]
        a_spec = pl.BlockSpec((t_, r), lambda j: (0, j))
        g_spec = pl.BlockSpec((t_, c), lambda j: (0, 0))

    def body(a_ref, g_ref, o_ref):
        a_val = a_ref[...] if a_pre is None else a_pre(a_ref[...])
        o_ref[...] = lax.dot_general(a_val, g_ref[...], (((0,), (0,)), ((), ())),
                                     preferred_element_type=F32).astype(o_ref.dtype)

    return pl.pallas_call(
        body, name=name, grid=(nj,), in_specs=[a_spec, g_spec],
        out_specs=pl.BlockSpec((None, r, c), lambda j: (j, 0, 0)),
        out_shape=jax.ShapeDtypeStruct((nj, r, c), out_dtype),
        compiler_params=_cp("arbitrary"),
    )(a, g)


def _epi_res_norm(acc, tiles, params):
    x_new = tiles[0] + acc
    return (x_new, _rms(x_new, params[0])), ()


def _epi_ple(acc, tiles, params):
    x_old, p_tile = tiles
    g_next, w_ple8 = params
    pe = jnp.concatenate([jnp.dot(p_tile, w_ple8[j], preferred_element_type=F32) for j in range(NDEV)], axis=1)
    x_new = x_old + pe * _sig(acc)
    return (x_new, acc, _rms(x_new, g_next), pe), ()


def _epi_rms_bwd(acc, tiles, params):
    dx, dg = _rms_bwd(acc, tiles[0], params[0], tiles[1])
    return (dx, dx), (dg,)


def _epi_dup(acc, tiles, params):
    return (acc * (2.0 * jnp.maximum(tiles[0].astype(F32), 0.0)),), ()


def _tri_mask():
    row = lax.broadcasted_iota(jnp.int32, (GW, GW), 0)
    col = lax.broadcasted_iota(jnp.int32, (GW, GW), 1)
    return row >= col


def _small_specs(sp_list):
    return [pl.BlockSpec(p.shape, (lambda i: (0, 0)) if p.ndim == 2 else (lambda i: (0, 0, 0))) for p in sp_list]


SUBLANES = 8


def _tap_sum(src, w_ref, taps, rows, stage):
    groups = {}
    for off, k in taps:
        groups.setdefault(off % SUBLANES, []).append((off - off % SUBLANES, k))
    out = None
    for res, members in sorted(groups.items()):
        n = rows if res == 0 else rows + SUBLANES
        part = None
        for base, k in members:
            term = w_ref[k:k + 1, :] * src[pl.ds(base, n), :]
            part = term if part is None else part + term
        if res:
            stage[0:n, :] = part
            part = stage[pl.ds(res, rows), :]
        out = part if out is None else out + part
    return out


def _tap_grads(grad, src, offsets, rows, stage, out_ref):
    pad = SUBLANES
    stage[0:pad, :] = jnp.zeros((pad, grad.shape[1]), F32)
    stage[pad:pad + rows, :] = grad
    stage[pad + rows:2 * pad + rows, :] = jnp.zeros((pad, grad.shape[1]), F32)
    groups = {}
    for k, off in enumerate(offsets):
        groups.setdefault(off % SUBLANES, []).append((off - off % SUBLANES, k))
    for res, members in sorted(groups.items()):
        shifted = stage[pl.ds(pad - res, rows + pad), :]
        for base, k in members:
            out_ref[k:k + 1, :] += _rowsum(shifted * src[pl.ds(base, rows + pad), :])


def _mixer_params(sp):
    return [sp["cw"], sp["cb"], sp["lg"], sp["lb"], sp["pw"], sp["ps"], sp["sc"], sp["gg"], sp["gb"], sp["ws"], sp["bst"]]


def _mixer_fwd(proj, sp, tm=256):
    nt = T // tm
    per = tm // HB

    conv_taps = [(HB - (CONF_K - 1) + k, k) for k in range(CONF_K)]

    def body(main_ref, halo_ref, cw, cb, lg, lb, pw, ps, sc, gg, gb, ws, bst, y_ref, ca_ref, ext, stage):
        i = pl.program_id(0)
        keep = (i > 0).astype(F32)

        def mcol(c0):
            return main_ref[:, c0:c0 + W].astype(F32)

        def hcol(c0):
            return halo_ref[:, c0:c0 + W].astype(F32)

        ext[0:HB, :] = hcol(0) * _sig(hcol(W)) * keep
        ext[HB:HB + tm, :] = mcol(0) * _sig(mcol(W))
        ca = (_tap_sum(ext, cw, conv_taps, tm, stage) + cb[...]).astype(BF16)
        ca_ref[...] = ca
        xh, _ = _ln_stats(ca.astype(F32))
        n = xh * lg[...] + lb[...]
        y_ref[:, 0:W] = (n * _sig(n)).astype(BF16)

        pin = mcol(1024)
        ext[0:HB, :] = hcol(1024) * keep
        ext[HB:HB + tm, :] = pin
        pos = (i * tm + lax.broadcasted_iota(jnp.int32, (tm, 1), 0) + 1).astype(F32)
        for g, w in enumerate(POOL_WINDOWS):
            lo = g * GW
            s = ext[pl.ds(HB, tm), lo:lo + GW]
            for j in range(1, w):
                s = s + ext[pl.ds(HB - j, tm), lo:lo + GW]
            pooled = s / jnp.minimum(pos, float(w)) - pin[:, lo:lo + GW]
            mixed = jnp.dot(pooled.astype(BF16), pw[g].astype(BF16), preferred_element_type=F32)
            y_ref[:, W + lo:W + lo + GW] = (mixed * ps[:, lo:lo + GW]).astype(BF16)

        ext[0:HB, :] = hcol(2048) * hcol(2560) * keep
        ext[HB:HB + tm, :] = mcol(2048) * mcol(2560)
        cv = sc[0:1, :] * ext[pl.ds(HB - 2, tm), :]
        cv = cv + sc[1:2, :] * ext[pl.ds(HB - 1, tm), :]
        cv = cv + sc[2:3, :] * ext[pl.ds(HB, tm), :]
        y_ref[:, 2 * W:3 * W] = (mcol(1536) * cv).astype(BF16)

        vh, _ = _ln_stats(mcol(3584))
        vn = (vh * gg[...] + gb[...]).astype(BF16)
        u = mcol(3072)
        tri = _tri_mask()
        for g in range(4):
            lo = g * GW
            wm = jnp.where(tri, ws[g], 0.0).astype(BF16)
            for c in range(tm // GW):
                r0 = c * GW
                sg = jnp.dot(wm, vn[r0:r0 + GW, lo:lo + GW], preferred_element_type=F32) + bst[:, g:g + 1]
                y_ref[r0:r0 + GW, 3 * W + lo:3 * W + lo + GW] = (u[r0:r0 + GW, lo:lo + GW] * sg).astype(BF16)

    plist = _mixer_params(sp)
    in_specs = [pl.BlockSpec((tm, MIX_COLS), lambda i: (i, 0)),
                pl.BlockSpec((HB, MIX_COLS), lambda i: (jnp.maximum(i * per - 1, 0), 0))]
    in_specs += _small_specs(plist)
    return pl.pallas_call(
        body, name="f_mixers", grid=(nt,), in_specs=in_specs,
        out_specs=[pl.BlockSpec((tm, 4 * W), lambda i: (i, 0)), pl.BlockSpec((tm, W), lambda i: (i, 0))],
        out_shape=[jax.ShapeDtypeStruct((T, 4 * W), BF16), jax.ShapeDtypeStruct((T, W), BF16)],
        scratch_shapes=[pltpu.VMEM((HB + tm, W), F32), pltpu.VMEM((tm + SUBLANES, W), F32)],
        compiler_params=_cp("arbitrary"),
    )(proj, proj, *plist)


def _assemble_wb(wb8_ref, wbf_ref):
    for k in range(4):
        for j in range(NDEV):
            wbf_ref[k, :, j * GW:(j + 1) * GW] = wb8_ref[j, k]


def _merge_fwd(y, proj, wb8, tm=256):
    nt = T // tm

    def body(y_ref, gate_ref, wb8_ref, z_ref, s_ref, m_ref, wbf):
        @pl.when(pl.program_id(0) == 0)
        def _():
            _assemble_wb(wb8_ref, wbf)

        m = jnp.zeros((tm, D), F32)
        for k in range(4):
            zk = jnp.dot(y_ref[:, k * W:(k + 1) * W], wbf[k], preferred_element_type=F32)
            z_ref[:, k * D:(k + 1) * D] = zk.astype(BF16)
            s = _sig(gate_ref[:, k * D:(k + 1) * D].astype(F32))
            s_ref[:, k * D:(k + 1) * D] = s.astype(BF16)
            m = m + s * zk
        m_ref[...] = m.astype(BF16)

    return pl.pallas_call(
        body, name="f_merge", grid=(nt,),
        in_specs=[pl.BlockSpec((tm, 4 * W), lambda i: (i, 0)),
                  pl.BlockSpec((tm, 4 * D), lambda i: (i, 1)),
                  pl.BlockSpec(wb8.shape, lambda i: (0, 0, 0, 0))],
        out_specs=[pl.BlockSpec((tm, 4 * D), lambda i: (i, 0)), pl.BlockSpec((tm, 4 * D), lambda i: (i, 0)),
                   pl.BlockSpec((tm, D), lambda i: (i, 0))],
        out_shape=[jax.ShapeDtypeStruct((T, 4 * D), BF16), jax.ShapeDtypeStruct((T, 4 * D), BF16),
                   jax.ShapeDtypeStruct((T, D), BF16)],
        scratch_shapes=[pltpu.VMEM((4, W, D), BF16)],
        compiler_params=_cp("arbitrary"),
    )(y, proj, wb8)


def _merge_bwd(dm, z, gates, y, wb8, tm=256):
    nt = T // tm

    def body(dm_ref, z_ref, gate_ref, y_ref, wb8_ref, dp_ref, dy_ref, dwb_ref, wbf, acc):
        i = pl.program_id(0)

        @pl.when(i == 0)
        def _():
            _assemble_wb(wb8_ref, wbf)

        dmv = dm_ref[...].astype(F32)
        for k in range(4):
            s = gate_ref[:, k * D:(k + 1) * D].astype(F32)
            dzk = (dmv * s).astype(BF16)
            dp_ref[:, k * D:(k + 1) * D] = (dmv * z_ref[:, k * D:(k + 1) * D].astype(F32) * s * (1.0 - s)).astype(BF16)
            dyk = lax.dot_general(dzk, wbf[k], (((1,), (1,)), ((), ())), preferred_element_type=F32)
            dy_ref[:, k * W:(k + 1) * W] = dyk.astype(BF16)
            part = lax.dot_general(y_ref[:, k * W:(k + 1) * W], dzk, (((0,), (0,)), ((), ())),
                                   preferred_element_type=F32)

            @pl.when(i == 0)
            def _():
                acc[k] = part

            @pl.when(i > 0)
            def _():
                acc[k] += part

        @pl.when(i == nt - 1)
        def _():
            for k in range(4):
                for j in range(NDEV):
                    dwb_ref[j, k] = acc[k, :, j * GW:(j + 1) * GW].astype(BF16)

    return pl.pallas_call(
        body, name="b_merge", grid=(nt,),
        in_specs=[pl.BlockSpec((tm, D), lambda i: (i, 0)),
                  pl.BlockSpec((tm, 4 * D), lambda i: (i, 0)),
                  pl.BlockSpec((tm, 4 * D), lambda i: (i, 0)),
                  pl.BlockSpec((tm, 4 * W), lambda i: (i, 0)),
                  pl.BlockSpec(wb8.shape, lambda i: (0, 0, 0, 0))],
        out_specs=[pl.BlockSpec((tm, 4 * D), lambda i: (i, 1)),
                   pl.BlockSpec((tm, 4 * W), lambda i: (i, 0)),
                   pl.BlockSpec(wb8.shape, lambda i: (0, 0, 0, 0))],
        out_shape=[jax.ShapeDtypeStruct((T, COLS_IN), BF16),
                   jax.ShapeDtypeStruct((T, 4 * W), BF16),
                   jax.ShapeDtypeStruct(wb8.shape, BF16)],
        scratch_shapes=[pltpu.VMEM((4, W, D), BF16), pltpu.VMEM((4, W, D), F32)],
        compiler_params=_cp("arbitrary"),
    )(dm, z, gates, y, wb8)


def _mixer_bwd(proj, ca_saved, dy, dproj, sp, tm=256):
    nt = T // tm
    per = tm // HB
    ne = tm + HA
    last_blk = T // HA - 1
    conv_taps = [(HB - (CONF_K - 1) + k, k) for k in range(CONF_K)]

    rr, cc = np.arange(tm)[:, None], np.arange(tm + HB)[None, :]
    band_list = []
    for w in POOL_WINDOWS:
        band_list.append((rr + HB - cc >= 0) & (rr + HB - cc < w))
        band_list.append((cc - rr >= 0) & (cc - rr < w))
    bands_const = jnp.asarray(np.stack(band_list).astype(np.float32), dtype=BF16)

    def body(main_ref, hb_ref, ha_ref, ca_ref, cah_ref, dy_ref, dyh_ref, cw, cb, lg, lb, pw, ps, sc, gg, gb, ws, bst,
             bands, dp_any, dp_ref, dcw_ref, dsc_ref, vec_ref, dpw_ref, dws_ref, dbs_ref, e1, e2, e3, stage):
        del dp_any, cb
        i = pl.program_id(0)
        keep_b = (i > 0).astype(F32)
        keep_a = (i < nt - 1).astype(F32)

        @pl.when(i == 0)
        def _():
            dcw_ref[...] = jnp.zeros_like(dcw_ref)
            dsc_ref[...] = jnp.zeros_like(dsc_ref)
            vec_ref[...] = jnp.zeros_like(vec_ref)
            dpw_ref[...] = jnp.zeros_like(dpw_ref)
            dws_ref[...] = jnp.zeros_like(dws_ref)
            dbs_ref[...] = jnp.zeros_like(dbs_ref)

        def mcol(c0):
            return main_ref[:, c0:c0 + W].astype(F32)

        def hbcol(c0):
            return hb_ref[:, c0:c0 + W].astype(F32)

        def hacol(c0):
            return ha_ref[:, c0:c0 + W].astype(F32)

        def load_dy(c0):
            e2[0:tm, :] = dy_ref[:, c0:c0 + W].astype(F32)
            e2[tm:ne, :] = dyh_ref[:, c0:c0 + W].astype(F32) * keep_a

        a = mcol(0)
        sa = _sig(mcol(W))
        e1[0:HB, :] = hbcol(0) * _sig(hbcol(W)) * keep_b
        e1[HB:HB + tm, :] = a * sa
        e1[HB + tm:HB + tm + SUBLANES, :] = jnp.zeros((SUBLANES, W), F32)
        e2[0:tm, :] = ca_ref[...].astype(F32)
        e2[tm:ne, :] = cah_ref[...].astype(F32)
        xh, rstd = _ln_stats(e2[0:ne, :])
        nn = xh * lg[...] + lb[...]
        s = _sig(nn)
        load_dy(0)
        dn = e2[0:ne, :] * (s * (1.0 + nn * (1.0 - s)))
        vec_ref[1:2, :] += _rowsum(dn[0:tm] * xh[0:tm])
        vec_ref[2:3, :] += _rowsum(dn[0:tm])
        dca = _ln_bwd(dn * lg[...], xh, rstd)
        e3[0:ne, :] = dca
        dmain = dca[0:tm]
        vec_ref[0:1, :] += _rowsum(dmain)
        _tap_grads(dmain, e1, [off for off, _ in conv_taps], tm, stage, dcw_ref)
        dglu = _tap_sum(e3, cw, [(CONF_K - 1 - k, k) for k in range(CONF_K)], tm, stage)
        dp_ref[:, 0:W] = (dglu * sa).astype(BF16)
        dp_ref[:, W:2 * W] = (dglu * a * sa * (1.0 - sa)).astype(BF16)

        pin = mcol(1024)
        e1[0:HB, :] = hbcol(1024) * keep_b
        e1[HB:HB + tm, :] = pin
        load_dy(W)
        dyb = e2[0:ne, :]
        pos_m = (i * tm + lax.broadcasted_iota(jnp.int32, (tm, 1), 0) + 1).astype(F32)
        pos_e = (i * tm + lax.broadcasted_iota(jnp.int32, (ne, 1), 0) + 1).astype(F32)
        for g, w in enumerate(POOL_WINDOWS):
            lo = g * GW
            acc = jnp.dot(bands[2 * g], e1[0:HB + tm, lo:lo + GW].astype(BF16), preferred_element_type=F32)
            pooled = (acc / jnp.minimum(pos_m, float(w)) - pin[:, lo:lo + GW]).astype(BF16)
            pwb = pw[g].astype(BF16)
            mixed = jnp.dot(pooled, pwb, preferred_element_type=F32)
            dyb_g = dyb[:, lo:lo + GW]
            vec_ref[3:4, lo:lo + GW] += _rowsum(dyb_g[0:tm] * mixed)
            dmb = (dyb_g * ps[:, lo:lo + GW]).astype(BF16)
            dpw_ref[g] += lax.dot_general(pooled, dmb[0:tm], (((0,), (0,)), ((), ())), preferred_element_type=F32)
            dpool = lax.dot_general(dmb, pwb, (((1,), (1,)), ((), ())), preferred_element_type=F32)
            scaled = (dpool / jnp.minimum(pos_e, float(w))).astype(BF16)
            back = jnp.dot(bands[2 * g + 1], scaled, preferred_element_type=F32)
            dp_ref[:, 1024 + lo:1024 + lo + GW] = (back - dpool[0:tm]).astype(BF16)

        cg = mcol(2048)
        hx = mcol(2560)
        e1[0:HB, :] = hbcol(2048) * hbcol(2560) * keep_b
        e1[HB:HB + tm, :] = cg * hx
        load_dy(2 * W)
        dyc = e2[0:tm, :]
        dconv = dyc * mcol(1536)
        e3[0:tm, :] = dconv
        e3[tm:ne, :] = e2[tm:ne, :] * hacol(1536)
        cv = sc[0:1, :] * e1[pl.ds(HB - 2, tm), :]
        for k in range(1, SC_K):
            cv = cv + sc[k:k + 1, :] * e1[pl.ds(HB - 2 + k, tm), :]
        dp_ref[:, 1536:2048] = (dyc * cv).astype(BF16)
        for k in range(SC_K):
            dsc_ref[k:k + 1, :] += _rowsum(dconv * e1[pl.ds(HB - 2 + k, tm), :])
        dq = sc[0:1, :] * e3[pl.ds(2, tm), :]
        for k in range(1, SC_K):
            dq = dq + sc[k:k + 1, :] * e3[pl.ds(2 - k, tm), :]
        dp_ref[:, 2048:2560] = (dq * hx).astype(BF16)
        dp_ref[:, 2560:3072] = (dq * cg).astype(BF16)

        u = mcol(3072)
        vh, vr = _ln_stats(mcol(3584))
        vn = (vh * gg[...] + gb[...]).astype(BF16)
        dyd = dy_ref[:, 3 * W:4 * W].astype(F32)
        tri = _tri_mask()
        for g in range(4):
            lo = g * GW
            wm = jnp.where(tri, ws[g], 0.0).astype(BF16)
            dws_g = jnp.zeros((GW, GW), F32)
            dbs_g = jnp.zeros((GW, 1), F32)
            for c in range(tm // GW):
                r0 = c * GW
                blk = vn[r0:r0 + GW, lo:lo + GW]
                sg = jnp.dot(wm, blk, preferred_element_type=F32) + bst[:, g:g + 1]
                dyd_b = dyd[r0:r0 + GW, lo:lo + GW]
                dp_ref[r0:r0 + GW, 3072 + lo:3072 + lo + GW] = (dyd_b * sg).astype(BF16)
                dsg = dyd_b * u[r0:r0 + GW, lo:lo + GW]
                dsgb = dsg.astype(BF16)
                dbs_g = dbs_g + jnp.sum(dsg, axis=-1, keepdims=True)
                dws_g = dws_g + lax.dot_general(dsgb, blk, (((1,), (1,)), ((), ())), preferred_element_type=F32)
                e1[r0:r0 + GW, lo:lo + GW] = lax.dot_general(wm, dsgb, (((0,), (0,)), ((), ())),
                                                             preferred_element_type=F32)
            dws_ref[g] += jnp.where(tri, dws_g, 0.0)
            dbs_ref[g] += jnp.broadcast_to(dbs_g, (GW, GW))
        dvn = e1[0:tm, :]
        vec_ref[4:5, :] += _rowsum(dvn * vh)
        vec_ref[5:6, :] += _rowsum(dvn)
        dp_ref[:, 3584:4096] = _ln_bwd(dvn * gg[...], vh, vr).astype(BF16)

    plist = _mixer_params(sp)
    in_specs = [pl.BlockSpec((tm, MIX_COLS), lambda i: (i, 0)),
                pl.BlockSpec((HB, MIX_COLS), lambda i: (jnp.maximum(i * per - 1, 0), 0)),
                pl.BlockSpec((HA, MIX_COLS), lambda i: (jnp.minimum((i + 1) * per, last_blk), 0)),
                pl.BlockSpec((tm, W), lambda i: (i, 0)),
                pl.BlockSpec((HA, W), lambda i: (jnp.minimum((i + 1) * per, last_blk), 0)),
                pl.BlockSpec((tm, 4 * W), lambda i: (i, 0)),
                pl.BlockSpec((HA, 4 * W), lambda i: (jnp.minimum((i + 1) * per, last_blk), 0))]
    in_specs += _small_specs(plist)
    in_specs += [pl.BlockSpec(bands_const.shape, lambda i: (0, 0, 0)), pl.BlockSpec(memory_space=pl.ANY)]
    z2 = lambda i: (0, 0)
    z3 = lambda i: (0, 0, 0)
    out_specs = [pl.BlockSpec((tm, MIX_COLS), lambda i: (i, 0)),
                 pl.BlockSpec((32, W), z2), pl.BlockSpec((8, W), z2), pl.BlockSpec((8, W), z2),
                 pl.BlockSpec((4, GW, GW), z3), pl.BlockSpec((4, GW, GW), z3), pl.BlockSpec((4, GW, GW), z3)]
    out_shape = [jax.ShapeDtypeStruct((T, COLS_IN), BF16),
                 jax.ShapeDtypeStruct((32, W), F32), jax.ShapeDtypeStruct((8, W), F32),
                 jax.ShapeDtypeStruct((8, W), F32),
                 jax.ShapeDtypeStruct((4, GW, GW), F32), jax.ShapeDtypeStruct((4, GW, GW), F32),
                 jax.ShapeDtypeStruct((4, GW, GW), F32)]
    n_in = 8 + len(plist)
    return pl.pallas_call(
        body, name="b_mixers", grid=(nt,), in_specs=in_specs, out_specs=out_specs, out_shape=out_shape,
        scratch_shapes=[pltpu.VMEM((HB + ne, W), F32), pltpu.VMEM((ne, W), F32), pltpu.VMEM((ne, W), F32),
                        pltpu.VMEM((ne + SUBLANES, W), F32)],
        input_output_aliases={n_in: 0},
        compiler_params=_cp("arbitrary"),
    )(proj, proj, proj, ca_saved, ca_saved, dy, dy, *plist, bands_const, dproj)


def _norm_first(x, g, tm=512):
    def body(x_ref, g_ref, o_ref):
        o_ref[...] = _rms(x_ref[...], g_ref[...]).astype(BF16)

    return pl.pallas_call(
        body, name="f_norm0", grid=(T // tm,),
        in_specs=[pl.BlockSpec((tm, D), lambda i: (i, 0)), pl.BlockSpec((1, D), lambda i: (0, 0))],
        out_specs=pl.BlockSpec((tm, D), lambda i: (i, 0)),
        out_shape=jax.ShapeDtypeStruct((T, D), BF16), compiler_params=_cp("arbitrary"),
    )(x, g)


def _loss_head(x, target, g, tm=256):
    def body(x_ref, t_ref, g_ref, dx_ref, dg_ref, loss_ref):
        i = pl.program_id(0)
        x = x_ref[...]
        r = lax.rsqrt(jnp.mean(x * x, axis=-1, keepdims=True) + EPS)
        xh = x * r
        gv = g_ref[...]
        e = xh * gv - t_ref[...]
        dyv = e * (1.0 / D)
        part = jnp.sum(_rowsum(e * e), axis=-1, keepdims=True) * (0.5 / D)
        u = dyv * gv
        dx_ref[...] = r * (u - xh * jnp.mean(u * xh, axis=-1, keepdims=True))
        dgp = _rowsum(dyv * xh)

        @pl.when(i == 0)
        def _():
            dg_ref[...] = dgp
            loss_ref[...] = jnp.broadcast_to(part, (1, GW))

        @pl.when(i > 0)
        def _():
            dg_ref[...] += dgp
            loss_ref[...] += jnp.broadcast_to(part, (1, GW))

    return pl.pallas_call(
        body, name="loss_head", grid=(T // tm,),
        in_specs=[pl.BlockSpec((tm, D), lambda i: (i, 0)), pl.BlockSpec((tm, D), lambda i: (i, 0)),
                  pl.BlockSpec((1, D), lambda i: (0, 0))],
        out_specs=[pl.BlockSpec((tm, D), lambda i: (i, 0)), pl.BlockSpec((1, D), lambda i: (0, 0)),
                   pl.BlockSpec((1, GW), lambda i: (0, 0))],
        out_shape=[jax.ShapeDtypeStruct((T, D), F32), jax.ShapeDtypeStruct((1, D), F32),
                   jax.ShapeDtypeStruct((1, GW), F32)],
        compiler_params=_cp("arbitrary"),
    )(x, target, g)


def _out_bwd(dx2b, merged, w_out8, tm=512):
    nt = T // tm

    def body(dx_ref, mg_ref, w_ref, dm_ref, dw_ref, acc):
        i = pl.program_id(0)
        dx = dx_ref[...]
        dm_ref[...] = lax.dot_general(dx, w_ref[...], (((1,), (1,)), ((), ())),
                                      preferred_element_type=F32).astype(BF16)
        part = lax.dot_general(mg_ref[...], dx, (((0,), (0,)), ((), ())), preferred_element_type=F32)

        @pl.when(i == 0)
        def _():
            acc[...] = part

        @pl.when(i > 0)
        def _():
            acc[...] += part

        @pl.when(i == nt - 1)
        def _():
            for j in range(NDEV):
                dw_ref[j] = acc[j * GW:(j + 1) * GW, :].astype(BF16)

    tile = pl.BlockSpec((tm, D), lambda i: (i, 0))
    return pl.pallas_call(
        body, name="b_out", grid=(nt,),
        in_specs=[tile, tile, pl.BlockSpec((D, D), lambda i: (0, 0))],
        out_specs=[tile, pl.BlockSpec((NDEV, GW, D), lambda i: (0, 0, 0))],
        out_shape=[jax.ShapeDtypeStruct((T, D), BF16), jax.ShapeDtypeStruct((NDEV, GW, D), BF16)],
        scratch_shapes=[pltpu.VMEM((D, D), F32)],
        compiler_params=_cp("arbitrary"),
    )(dx2b, merged, w_out8.reshape(D, D))


def _ple_bwd(dx4, sv, w_pleg8, g_ple, tm=512):
    nt = T // tm
    ple_dim = sv["p"].shape[1]

    def body(dx_ref, gl_ref, pe_ref, x_ref, h_ref, p_ref, g_ref, wg_ref,
             dx3_ref, dx3b_ref, dg_ref, dwg_ref, dwp_ref, acc_g, acc_p):
        i = pl.program_id(0)
        d = dx_ref[...]
        s = _sig(gl_ref[...].astype(F32))
        dpe = (d * s).astype(BF16)
        dgl = (d * pe_ref[...].astype(F32) * s * (1.0 - s)).astype(BF16)
        dh = lax.dot_general(dgl, wg_ref[...], (((1,), (1,)), ((), ())), preferred_element_type=F32)
        dx, dgp = _rms_bwd(dh, x_ref[...], g_ref[...], d)
        dx3_ref[...] = dx
        dx3b_ref[...] = dx.astype(BF16)
        part_g = lax.dot_general(h_ref[...], dgl, (((0,), (0,)), ((), ())), preferred_element_type=F32)
        part_p = lax.dot_general(p_ref[...], dpe, (((0,), (0,)), ((), ())), preferred_element_type=F32)

        @pl.when(i == 0)
        def _():
            dg_ref[...] = dgp
            acc_g[...] = part_g
            acc_p[...] = part_p

        @pl.when(i > 0)
        def _():
            dg_ref[...] += dgp
            acc_g[...] += part_g
            acc_p[...] += part_p

        @pl.when(i == nt - 1)
        def _():
            for j in range(NDEV):
                dwg_ref[j] = acc_g[j * GW:(j + 1) * GW, :].astype(BF16)
                dwp_ref[j] = acc_p[:, j * GW:(j + 1) * GW].astype(BF16)

    tile = lambda w: pl.BlockSpec((tm, w), lambda i: (i, 0))
    const = lambda shp: pl.BlockSpec(shp, lambda i: (0,) * len(shp))
    return pl.pallas_call(
        body, name="b_ple", grid=(nt,),
        in_specs=[tile(D), tile(D), tile(D), tile(D), tile(D), tile(ple_dim), const((1, D)), const((D, D))],
        out_specs=[tile(D), tile(D), const((1, D)), const((NDEV, GW, D)), const((NDEV, ple_dim, GW))],
        out_shape=[jax.ShapeDtypeStruct((T, D), F32), jax.ShapeDtypeStruct((T, D), BF16),
                   jax.ShapeDtypeStruct((1, D), F32), jax.ShapeDtypeStruct((NDEV, GW, D), BF16),
                   jax.ShapeDtypeStruct((NDEV, ple_dim, GW), BF16)],
        scratch_shapes=[pltpu.VMEM((D, D), F32), pltpu.VMEM((ple_dim, D), F32)],
        compiler_params=_cp("arbitrary"),
    )(dx4, sv["gl"], sv["pe"], sv["x3"], sv["h3"], sv["p"], g_ple, w_pleg8.reshape(D, D))


def _layer_fwd(x, h1, p_bf, gw, sp, g_next):
    proj, = _mm(h1, gw["w_in"], mode="out", name="f_proj", outs=[BF16], tm=T)
    y, ca = _mixer_fwd(proj, sp)
    z, gates, merged = _merge_fwd(y, proj, gw["w_branch"])
    x2, h2 = _mm(merged, gw["w_out"].reshape(1, D, D), mode="acc", name="f_out", outs=[F32, BF16], tm=T // 2,
                 tiles=[x], params=[sp["g_mlp"]], epi=_epi_res_norm)
    up, = _mm(h2, gw["w_up"], mode="out", name="f_up", outs=[BF16], tm=T)
    x3, h3 = _mm(up, gw["w_down"].reshape(1, 4 * D, D), mode="acc", name="f_down", outs=[F32, BF16], tm=T // 4,
                 tiles=[x2], params=[sp["g_ple"]], epi=_epi_res_norm, a_pre=_relu2_bf16)
    x4, gl, hn, pe = _mm(h3, gw["w_pleg"].reshape(1, D, D), mode="acc", name="f_gate", tm=T // 2,
                         outs=[F32, BF16, BF16, BF16], tiles=[x3, p_bf], params=[g_next, gw["w_ple"]], epi=_epi_ple)
    saved = dict(x=x, h1=h1, proj=proj, y=y, ca=ca, z=z, gates=gates, merged=merged, x2=x2, h2=h2, up=up, x3=x3, h3=h3,
                 pe=pe, gl=gl, p=p_bf)
    return x4, hn, saved


def _layer_bwd(dx4, sv, gw, sp, submit, early_group=False):
    dw = {}
    dx3, dx3b, dg_ple, dw["w_pleg"], dw["w_ple"] = _ple_bwd(dx4, sv, gw["w_pleg"], sp["g_ple"])
    dup, = _mm(dx3b, gw["w_down"], mode="out", trans_b=True, name="b_dact", outs=[BF16], tm=T,
               tiles=[sv["up"]], epi=_epi_dup)
    dw["w_down"] = _mm_tn(sv["up"], dx3b, nj=NDEV, split="row", name="b_dw_down", a_pre=_relu2_bf16)
    dw["w_up"] = _mm_tn(sv["h2"], dup, nj=NDEV, split="col", name="b_dw_up")
    if early_group:
        dw["w_up"], dup = lax.optimization_barrier((dw["w_up"], dup))
        dup = submit(dw, ("w_up", "w_down", "w_ple", "w_pleg"), dup)
    dx2, dx2b, dg_mlp = _mm(dup, gw["w_up"], mode="full", trans_b=True, name="b_dh2", tm=T // 8, stream_first=True,
                            outs=[F32, BF16], tiles=[sv["x2"], dx3], params=[sp["g_mlp"]], epi=_epi_rms_bwd, reds=[D])
    dm, dw["w_out"] = _out_bwd(dx2b, sv["merged"], gw["w_out"])
    dproj, dy, dw["w_branch"] = _merge_bwd(dm, sv["z"], sv["gates"], sv["y"], gw["w_branch"])
    dy = submit(dw, ("w_branch", "w_out") if early_group else BIG[1:], dy)
    dproj, dcw, dsc, vec, dpw, dws, dbs = _mixer_bwd(sv["proj"], sv["ca"], dy, dproj, sp)
    dw["w_in"] = _mm_tn(sv["h1"], dproj, nj=NDEV, split="col", name="b_dw_in")
    dw["w_in"], dproj = lax.optimization_barrier((dw["w_in"], dproj))
    dproj = submit(dw, BIG[:1], dproj)
    dx, dg_mix = _mm(dproj, gw["w_in"], mode="full", trans_b=True, name="b_dh1", outs=[F32], tm=T // 8,
                     stream_first=True,
                     tiles=[sv["x"], dx2], params=[sp["g_mix"]], epi=_epi_rms_bwd, reds=[D])
    small = dict(norm_mix=dg_mix[0], conf_dw=dcw[:CONF_K], conf_dw_b=vec[0], conf_ln_g=vec[1], conf_ln_b=vec[2],
                 pool_w=dpw, pool_scale=vec[3], sc_conv=dsc[:SC_K], gmlp_ln_g=vec[4], gmlp_ln_b=vec[5],
                 gmlp_ws=dws, gmlp_bs=dbs[:, :, 0], norm_mlp=dg_mlp[0], norm_ple=dg_ple[0])
    return dx, small


ANY = pl.BlockSpec(memory_space=pl.ANY)


def _mesh_pos():
    return lax.axis_index("x"), lax.axis_index("y"), lax.axis_index("c")


def _other_chips(x, y):
    return [(1 - x, y), (x, 1 - y), (1 - x, 1 - y)]


def _launch_comm(body, peers_of, operands, out_shapes, sems, name, seq_id):
    n_in, n_out = len(operands), len(out_shapes)
    if seq_id is None:
        return pl.pallas_call(body, name=name, in_specs=[ANY] * n_in, out_specs=[ANY] * n_out,
                              out_shape=out_shapes, scratch_shapes=sems)(*operands)

    def seq_body(*refs):
        peers = peers_of(*_mesh_pos())
        barrier = pltpu.get_barrier_semaphore()
        for peer in peers:
            pl.semaphore_signal(barrier, inc=1, device_id=peer, device_id_type=MESH)
        pl.semaphore_wait(barrier, len(peers))
        body(*refs)

    return pl.kernel(seq_body, name=name, out_type=out_shapes,
                     mesh=plsc.ScalarSubcoreMesh(axis_name="seq", num_cores=1), scratch_types=sems,
                     compiler_params=pltpu.CompilerParams(collective_id=seq_id))(*operands)


def _all_gather(shards, name, seq_id=None):
    n = len(shards)

    def body(*refs):
        s_refs, o_refs = refs[:n], refs[n:2 * n]
        send_sems, recv_sems, local_sems = refs[2 * n:]
        x, y, c = _mesh_pos()
        me = 4 * x + 2 * y + c
        here = (x, y, c)
        sibling = (x, y, 1 - c)
        chips = _other_chips(x, y)

        def slot(px, py, pc):
            return 4 * px + 2 * py + pc

        def copy(t, k, slot_idx, to, src=None):
            dst = o_refs[t].at[slot_idx]
            return pltpu.make_async_remote_copy(
                src_ref=dst if src is None else src, dst_ref=dst,
                send_sem=send_sems.at[t * 7 + k], recv_sem=recv_sems.at[t * 7 + k],
                device_id=to, device_id_type=MESH)

        mine = [pltpu.make_async_copy(s_refs[t], o_refs[t].at[me], local_sems.at[t]) for t in range(n)]
        for cp in mine:
            cp.start()
        first = []
        for t in range(n):
            for j, chip in enumerate(chips):
                first.append(copy(t, 1 + j, me, (*chip, c), src=s_refs[t]))
        for t in range(n):
            first.append(copy(t, 0, me, sibling, src=s_refs[t]))
        for cp in first:
            cp.start()
        passed = []
        for t in range(n):
            for j, chip in enumerate(chips):
                copy(t, 1 + j, slot(*chip, c), here).wait_recv()
                fwd = copy(t, 4 + j, slot(*chip, c), sibling)
                fwd.start()
                passed.append(fwd)
        for t in range(n):
            copy(t, 0, slot(x, y, 1 - c), here).wait_recv()
            for j, chip in enumerate(chips):
                copy(t, 4 + j, slot(*chip, 1 - c), here).wait_recv()
        for cp in first + passed:
            cp.wait_send()
        for cp in mine:
            cp.wait()

    def peers_of(x, y, c):
        return [(x, y, 1 - c)] + [(*chip, c) for chip in _other_chips(x, y)]

    return _launch_comm(
        body, peers_of, shards, [jax.ShapeDtypeStruct((NDEV,) + s.shape, s.dtype) for s in shards],
        [pltpu.SemaphoreType.DMA((7 * n,)), pltpu.SemaphoreType.DMA((7 * n,)), pltpu.SemaphoreType.DMA((n,))],
        name, seq_id)


def _rs_exchange(p4s, qs, name, seq_id=None):
    n_p, n_q = len(p4s), len(qs)

    def body(*refs):
        p_refs, q_refs = refs[:n_p], refs[n_p:n_p + n_q]
        rb_refs, rc_refs = refs[n_p + n_q:2 * n_p + n_q], refs[2 * n_p + n_q:2 * (n_p + n_q)]
        pair_send, pair_recv, chip_send, chip_recv, local_sems = refs[2 * (n_p + n_q):]
        x, y, c = _mesh_pos()
        a_idx = 2 * x + y
        chips = _other_chips(x, y)
        mine = [pltpu.make_async_copy(q_refs[t].at[a_idx], rc_refs[t].at[a_idx], local_sems.at[t])
                for t in range(n_q)]
        sends = []
        for t in range(n_q):
            for j, chip in enumerate(chips):
                sends.append(pltpu.make_async_remote_copy(
                    src_ref=q_refs[t].at[2 * chip[0] + chip[1]], dst_ref=rc_refs[t].at[a_idx],
                    send_sem=chip_send.at[t * 3 + j], recv_sem=chip_recv.at[t * 3 + j],
                    device_id=(*chip, c), device_id_type=MESH))
        pairs = [pltpu.make_async_remote_copy(
            src_ref=p_refs[t].at[:, 1 - c], dst_ref=rb_refs[t], send_sem=pair_send.at[t], recv_sem=pair_recv.at[t],
            device_id=(x, y, 1 - c), device_id_type=MESH) for t in range(n_p)]
        for cp in sends + mine + pairs:
            cp.start()
        for cp in pairs:
            cp.wait()
        for t in range(n_q):
            for j, chip in enumerate(chips):
                landed = rc_refs[t].at[2 * chip[0] + chip[1]]
                pltpu.make_async_remote_copy(
                    src_ref=landed, dst_ref=landed, send_sem=chip_send.at[t * 3 + j],
                    recv_sem=chip_recv.at[t * 3 + j], device_id=(x, y, c), device_id_type=MESH).wait_recv()
        for cp in sends:
            cp.wait_send()
        for cp in mine:
            cp.wait()

    def peers_of(x, y, c):
        peers = [(x, y, 1 - c)] if n_p else []
        return peers + ([(*chip, c) for chip in _other_chips(x, y)] if n_q else [])

    out_shapes = [jax.ShapeDtypeStruct((NCHIP,) + p.shape[2:], p.dtype) for p in p4s]
    out_shapes += [jax.ShapeDtypeStruct(q.shape, q.dtype) for q in qs]
    sems = [pltpu.SemaphoreType.DMA((max(n_p, 1),)), pltpu.SemaphoreType.DMA((max(n_p, 1),)),
            pltpu.SemaphoreType.DMA((max(3 * n_q, 1),)), pltpu.SemaphoreType.DMA((max(3 * n_q, 1),)),
            pltpu.SemaphoreType.DMA((max(n_q, 1),))]
    got = _launch_comm(body, peers_of, list(p4s) + list(qs), out_shapes, sems, name, seq_id)
    return got[:n_p], got[n_p:]


def _pair_sum(p4s, rbs, c_idx, name, nst=1):
    n = len(p4s)
    trs = [p.shape[2] // nst for p in p4s]

    def body(c_ref, *refs):
        del c_ref
        p_refs, r_refs, o_refs = refs[:n], refs[n:2 * n], refs[2 * n:]
        for p_ref, r_ref, o_ref in zip(p_refs, r_refs, o_refs):
            o_ref[...] = (p_ref[...].astype(F32) + r_ref[...].astype(F32)).astype(o_ref.dtype)

    in_specs = [pl.BlockSpec((None, None, tr, p.shape[3]), lambda b, i, c_ref: (b, c_ref[0], i, 0))
                for p, tr in zip(p4s, trs)]
    in_specs += [pl.BlockSpec((None, tr, p.shape[3]), lambda b, i, c_ref: (b, i, 0)) for p, tr in zip(p4s, trs)]
    out_specs = [pl.BlockSpec((None, tr, p.shape[3]), lambda b, i, c_ref: (b, i, 0)) for p, tr in zip(p4s, trs)]
    return pl.pallas_call(
        body, name=name,
        grid_spec=pltpu.PrefetchScalarGridSpec(num_scalar_prefetch=1, grid=(NCHIP, nst), in_specs=in_specs,
                                               out_specs=out_specs),
        out_shape=[jax.ShapeDtypeStruct((NCHIP,) + p.shape[2:], p.dtype) for p in p4s],
        compiler_params=_cp("arbitrary", "arbitrary"),
    )(c_idx, *p4s, *rbs)


class _GradientPipeline:
    def __init__(self, c_idx, results):
        self.c_idx, self.results, self.pending = c_idx, results, None

    def _sum_pending(self, chain):
        names, layer, p4s, rbs = self.pending
        qs = _pair_sum(p4s, rbs, self.c_idx, name="rs_pairsum_%d" % len(names))
        return lax.optimization_barrier((chain, qs))

    def submit(self, dw, names, layer, chain):
        qs, tag, seq_id = [], "pair", 3
        if self.pending is not None:
            chain, qs = self._sum_pending(chain)
            tag, seq_id = "pair_chip", 4
        p4s = [dw[n].reshape((NCHIP, 2) + BIG_SHARD[n]) for n in names]
        rbs, rcs = _rs_exchange(p4s, qs, name="rs_%s_%d" % (tag, len(names)), seq_id=seq_id)
        self._record(rcs)
        self.pending = (names, layer, p4s, rbs)
        return chain

    def finish(self, chain):
        chain, qs = self._sum_pending(chain)
        self._record(_rs_exchange([], qs, name="rs_chip_last", seq_id=5)[1])
        self.pending = None
        return chain

    def _record(self, rcs):
        if rcs:
            names, layer = self.pending[:2]
            for n, rc in zip(names, rcs):
                self.results[n][layer] = rc


def _adamw(w, g, m, v):
    m = ADAM_B1 * m + (1.0 - ADAM_B1) * g
    v = ADAM_B2 * v + (1.0 - ADAM_B2) * (g * g)
    m_hat = m / (1.0 - ADAM_B1 ** ADAM_STEP)
    v_hat = v / (1.0 - ADAM_B2 ** ADAM_STEP)
    delta = -ADAM_LR * (m_hat / (jnp.sqrt(v_hat) + ADAM_EPS) + ADAM_WD * w)
    return delta, m, v


def _adam_sharded(rcs, w, m, v, tr, name, first_layer, partial=None):
    _, r, c = w.shape
    nst = r // tr
    n_l = len(rcs)

    def body(*refs):
        rc_refs = refs[:n_l]
        w_ref, m_ref, v_ref = refs[n_l:n_l + 3]
        g_out, d_out, m_out, v_out = refs[-4:]
        layer = pl.program_id(0)
        for k, rc in enumerate(rc_refs):
            @pl.when(layer == k)
            def _():
                g = rc[0].astype(F32) + rc[1].astype(F32) + rc[2].astype(F32) + rc[3].astype(F32)
                delta, m_new, v_new = _adamw(w_ref[...], g, m_ref[...], v_ref[...])
                g_out[...] = g
                d_out[...] = delta
                m_out[...] = m_new
                v_out[...] = v_new

    rc_specs = [pl.BlockSpec((NCHIP, tr, c), lambda l, i, k=k: (0, jnp.where(l == k, i, 0), 0)) for k in range(n_l)]
    wspec = pl.BlockSpec((None, tr, c), lambda l, i: (first_layer + l, i, 0))
    carried = [] if partial is None else list(partial)
    return pl.pallas_call(
        body, name=name, grid=(n_l, nst),
        in_specs=rc_specs + [wspec] * 3 + [pl.BlockSpec(memory_space=pl.ANY)] * len(carried),
        out_specs=[wspec] * 4, out_shape=[jax.ShapeDtypeStruct(w.shape, F32)] * 4,
        input_output_aliases={n_l + 3 + k: k for k in range(len(carried))},
        compiler_params=_cp("arbitrary", "arbitrary"),
    )(*rcs, w, m, v, *carried)


def _adam_packed(g, w, m, v, direct):
    n_d = len(direct)

    def pieces(shape):
        width = shape[-1]
        count = 1
        for s in shape[:-1]:
            count *= s
        per_row = D // width
        out = []
        for k in range(count):
            idx = (k,) if len(shape) == 2 else (k // shape[1], k % shape[1])
            out.append((idx, k // per_row, (k % per_row) * width, width))
        return out

    def body(g_ref, w_ref, m_ref, v_ref, d_out, m_out, v_out, *outs):
        delta, m_new, v_new = _adamw(w_ref[...], g_ref[...], m_ref[...], v_ref[...])
        d_out[...] = delta
        m_out[...] = m_new
        v_out[...] = v_new
        for a, (_, row0, shape) in enumerate(direct):
            for src, dst in zip((g_ref, d_out, m_out, v_out), outs[4 * a:4 * a + 4]):
                for idx, row, lane0, width in pieces(shape):
                    piece = src[pl.ds(row0 + row, 1), lane0:lane0 + width]
                    if len(idx) == 1:
                        dst[pl.ds(idx[0], 1), :] = piece
                    else:
                        dst[idx[0], pl.ds(idx[1], 1), :] = piece

    out_shape = [jax.ShapeDtypeStruct(g.shape, F32)] * 3
    for _, _, shape in direct:
        out_shape += [jax.ShapeDtypeStruct(shape, F32)] * 4
    res = pl.pallas_call(body, name="adam_small", out_shape=out_shape,
                         compiler_params=pltpu.CompilerParams(vmem_limit_bytes=VMEM_LIMIT_BYTES))(g, w, m, v)
    return res[:3], {name: res[3 + 4 * a:7 + 4 * a] for a, (name, _, _) in enumerate(direct)}


def _sum4(rc):
    def body(rc_ref, o_ref):
        o_ref[...] = rc_ref[0] + rc_ref[1] + rc_ref[2] + rc_ref[3]

    return pl.pallas_call(
        body, name="small_sum", out_shape=jax.ShapeDtypeStruct(rc.shape[1:], F32),
    )(rc)


BIG = ("w_in", "w_branch", "w_out", "w_up", "w_down", "w_ple", "w_pleg")
BIG_SHARD = {"w_in": (D, D), "w_branch": (4 * W, GW), "w_out": (GW, D), "w_up": (D, W), "w_down": (W, D),
             "w_ple": (256, GW), "w_pleg": (GW, D)}
ADAM_ROWS = {"w_in": 256, "w_branch": 512, "w_out": 128, "w_up": 256, "w_down": 256, "w_ple": 256, "w_pleg": 128}
SMALL = (("norm_mix", (DEPTH, D)), ("conf_dw", (DEPTH, CONF_K, W)), ("conf_dw_b", (DEPTH, W)),
         ("conf_ln_g", (DEPTH, W)), ("conf_ln_b", (DEPTH, W)), ("pool_w", (DEPTH, 4, GW, GW)),
         ("pool_scale", (DEPTH, W)), ("sc_conv", (DEPTH, SC_K, W)), ("gmlp_ln_g", (DEPTH, W)),
         ("gmlp_ln_b", (DEPTH, W)), ("gmlp_ws", (DEPTH, 4, GW, GW)), ("gmlp_bs", (DEPTH, 4, GW)),
         ("norm_mlp", (DEPTH, D)), ("norm_ple", (DEPTH, D)), ("norm_final", (D,)))
CHANNEL_SHARDED = ("conf_dw", "sc_conv")
SMALL_ROWS = 80


def _pack(arrs, rows):
    flat = jnp.concatenate([a.reshape(-1) for a in arrs])
    return jnp.pad(flat, (0, rows * D - flat.shape[0])).reshape(rows, D)


def _unpack(packed, shapes):
    flat = packed.reshape(-1)
    out, off = [], 0
    for shp in shapes:
        size = 1
        for s in shp:
            size *= s
        out.append(flat[off:off + size].reshape(shp))
        off += size
    return out


def kernel(x, p, norm_mix, w_in, conf_dw, conf_dw_b, conf_ln_g, conf_ln_b, pool_w, pool_scale, sc_conv, gmlp_ln_g, gmlp_ln_b, gmlp_ws, gmlp_bs, w_branch, w_out, norm_mlp, w_up, w_down, norm_ple, w_ple, w_ple_gate, norm_final, loss_target, m_norm_mix, m_w_in, m_conf_dw, m_conf_dw_b, m_conf_ln_g, m_conf_ln_b, m_pool_w, m_pool_scale, m_sc_conv, m_gmlp_ln_g, m_gmlp_ln_b, m_gmlp_ws, m_gmlp_bs, m_w_branch, m_w_out, m_norm_mlp, m_w_up, m_w_down, m_norm_ple, m_w_ple, m_w_ple_gate, m_norm_final, v_norm_mix, v_w_in, v_conf_dw, v_conf_dw_b, v_conf_ln_g, v_conf_ln_b, v_pool_w, v_pool_scale, v_sc_conv, v_gmlp_ln_g, v_gmlp_ln_b, v_gmlp_ws, v_gmlp_bs, v_w_branch, v_w_out, v_norm_mlp, v_w_up, v_w_down, v_norm_ple, v_w_ple, v_w_ple_gate, v_norm_final):
    weights = dict(norm_mix=norm_mix, w_in=w_in, conf_dw=conf_dw, conf_dw_b=conf_dw_b, conf_ln_g=conf_ln_g,
                   conf_ln_b=conf_ln_b, pool_w=pool_w, pool_scale=pool_scale, sc_conv=sc_conv, gmlp_ln_g=gmlp_ln_g,
                   gmlp_ln_b=gmlp_ln_b, gmlp_ws=gmlp_ws, gmlp_bs=gmlp_bs, w_branch=w_branch, w_out=w_out,
                   norm_mlp=norm_mlp, w_up=w_up, w_down=w_down, norm_ple=norm_ple, w_ple=w_ple, w_pleg=w_ple_gate,
                   norm_final=norm_final)
    mom1 = dict(norm_mix=m_norm_mix, w_in=m_w_in, conf_dw=m_conf_dw, conf_dw_b=m_conf_dw_b, conf_ln_g=m_conf_ln_g,
                conf_ln_b=m_conf_ln_b, pool_w=m_pool_w, pool_scale=m_pool_scale, sc_conv=m_sc_conv,
                gmlp_ln_g=m_gmlp_ln_g, gmlp_ln_b=m_gmlp_ln_b, gmlp_ws=m_gmlp_ws, gmlp_bs=m_gmlp_bs,
                w_branch=m_w_branch, w_out=m_w_out, norm_mlp=m_norm_mlp, w_up=m_w_up, w_down=m_w_down,
                norm_ple=m_norm_ple, w_ple=m_w_ple, w_pleg=m_w_ple_gate, norm_final=m_norm_final)
    mom2 = dict(norm_mix=v_norm_mix, w_in=v_w_in, conf_dw=v_conf_dw, conf_dw_b=v_conf_dw_b, conf_ln_g=v_conf_ln_g,
                conf_ln_b=v_conf_ln_b, pool_w=v_pool_w, pool_scale=v_pool_scale, sc_conv=v_sc_conv,
                gmlp_ln_g=v_gmlp_ln_g, gmlp_ln_b=v_gmlp_ln_b, gmlp_ws=v_gmlp_ws, gmlp_bs=v_gmlp_bs,
                w_branch=v_w_branch, w_out=v_w_out, norm_mlp=v_norm_mlp, w_up=v_w_up, w_down=v_w_down,
                norm_ple=v_norm_ple, w_ple=v_w_ple, w_pleg=v_w_ple_gate, norm_final=v_norm_final)

    xi, yi, ci = _mesh_pos()
    me = 4 * xi + 2 * yi + ci
    c_idx = jnp.reshape(ci, (1,)).astype(jnp.int32)

    gathered, conf_full, sc_full = [], [], []
    for l in range(DEPTH):
        shard = lambda n: weights[n][l].astype(BF16).reshape(BIG_SHARD[n])
        w_in_g, conf_g, sc_g = _all_gather([shard("w_in"), conf_dw[l], sc_conv[l]], name="ag_first", seq_id=1)
        if l + 1 < DEPTH:
            rest = _all_gather([shard(n) for n in BIG[1:]], name="ag_rest", seq_id=2)
        else:
            rest = (list(_all_gather([shard(n) for n in BIG[1:4]], name="ag_rest_a", seq_id=2))
                    + list(_all_gather([shard(n) for n in BIG[4:]], name="ag_rest_b", seq_id=2)))
        gw = dict(zip(BIG[1:], rest), w_in=w_in_g)
        gw["w_branch"] = gw["w_branch"].reshape(NDEV, 4, W, GW)
        gathered.append(gw)
        conf_full.append(conf_g)
        sc_full.append(sc_g)

    def small_params(l):
        return dict(cw=conf_full[l], cb=conf_dw_b[l][None], lg=conf_ln_g[l][None], lb=conf_ln_b[l][None],
                    pw=pool_w[l], ps=pool_scale[l][None], sc=sc_full[l], gg=gmlp_ln_g[l][None],
                    gb=gmlp_ln_b[l][None], ws=gmlp_ws[l], bst=gmlp_bs[l].T, g_mix=norm_mix[l][None],
                    g_mlp=norm_mlp[l][None], g_ple=norm_ple[l][None])

    xc = x.reshape(T, D)
    small_names = [n for n, _ in SMALL]

    def in_gradient_layout(n, shard, shape):
        if n not in CHANNEL_SHARDED:
            return shard
        return lax.dynamic_update_slice(jnp.zeros(shape, F32), shard, (0, 0, me * (W // NDEV)))

    small_state = [_pack([in_gradient_layout(n, src[n], shape) for n, shape in SMALL], NDEV * SMALL_ROWS)
                   for src in (weights, mom1, mom2)]
    xc, small_state = lax.optimization_barrier((xc, small_state))
    p_bf = p.reshape(DEPTH, T, 256).astype(BF16)
    h = _norm_first(xc, norm_mix[0][None])
    saved = []
    for l in range(DEPTH):
        g_next = norm_mix[l + 1][None] if l + 1 < DEPTH else norm_final[None]
        h, conf_g, sc_g = lax.optimization_barrier((h, conf_full[l], sc_full[l]))
        conf_full[l] = conf_g.transpose(1, 0, 2).reshape(CONF_K, W)
        sc_full[l] = sc_g.transpose(1, 0, 2).reshape(SC_K, W)
        xc, h, sv = _layer_fwd(xc, h, p_bf[l], gathered[l], small_params(l), g_next)
        saved.append(sv)

    dxc, dg_final, loss_part = _loss_head(xc, loss_target.reshape(T, D), norm_final[None])
    loss = lax.psum(loss_part[0, 0], ("x", "y", "c"))
    small_grads = [None] * DEPTH
    rcs = {n: [None] * DEPTH for n in BIG}
    pipeline = _GradientPipeline(c_idx, rcs)
    for l in reversed(range(DEPTH)):
        dxc, small_grads[l] = _layer_bwd(dxc, saved[l], gathered[l], small_params(l),
                                         lambda dw, names, value, l=l: pipeline.submit(dw, names, l, value),
                                         early_group=(l == 0))

    def adam_sharded(first_layer, n_layers, partial, tag, names=BIG):
        outs = {}
        for n in names:
            shp = (DEPTH,) + BIG_SHARD[n]
            outs[n] = _adam_sharded(rcs[n][first_layer:first_layer + n_layers], weights[n].reshape(shp),
                                    mom1[n].reshape(shp), mom2[n].reshape(shp), ADAM_ROWS[n],
                                    "adam_%s_%s" % (n, tag), first_layer, None if partial is None else partial[n])
        return outs

    stacked = {n: jnp.stack([small_grads[l][n] for l in range(DEPTH)]) for n, _ in SMALL if n != "norm_final"}
    stacked["norm_final"] = dg_final[0]
    packed = _pack([stacked[n] for n, _ in SMALL], NDEV * SMALL_ROWS).reshape(NCHIP, 2, SMALL_ROWS, D)
    (pair_small,), _ = _rs_exchange([packed], [], name="rs_pair_small")
    q_small = _pair_sum([packed], [pair_small], c_idx, name="rs_pairsum_small")
    dxc, upper, q_small = lax.optimization_barrier((dxc, {n: rcs[n][1:] for n in BIG}, q_small))
    _, (chips_small,) = _rs_exchange([], q_small, name="rs_chip_small", seq_id=6)
    dxc, upper = pipeline.finish((dxc, upper))
    for n in BIG:
        rcs[n][1:] = upper[n]
    partial = adam_sharded(1, DEPTH - 1, None, "upper")
    last = adam_sharded(0, 1, partial, "last", names=BIG[:1])
    partial = {n: partial[n] for n in BIG[1:]}
    last, partial, chips_small = lax.optimization_barrier((last, partial, chips_small))
    reduced_slot = _sum4(chips_small)
    reduced = _all_gather([reduced_slot], name="ag_small", seq_id=7)[0]
    small_full = dict(zip([n for n, _ in SMALL], _unpack(reduced, [s for _, s in SMALL])))
    grads, deltas, new_m, new_v = {}, {}, {}, {}
    direct, row = [], 0
    for n, shape in SMALL:
        if len(shape) == 1 or shape[-2] == DEPTH:
            direct.append((n, row, (1,) * (2 - len(shape)) + tuple(shape)))
        size = 1
        for s in shape:
            size *= s
        row += size // D
    (d_p, m_p, v_p), own_shape = _adam_packed(reduced.reshape(NDEV * SMALL_ROWS, D), *small_state, direct)
    small_shapes = [s for _, s in SMALL]

    def own_channels(n, full):
        return lax.dynamic_slice_in_dim(full, me * (W // NDEV), W // NDEV, axis=2) if n in CHANNEL_SHARDED else full

    for n, d_, m_, v_ in zip(small_names, _unpack(d_p, small_shapes), _unpack(m_p, small_shapes),
                             _unpack(v_p, small_shapes)):
        if n in own_shape:
            grads[n], deltas[n], new_m[n], new_v[n] = [a.reshape(weights[n].shape) for a in own_shape[n]]
        else:
            grads[n], deltas[n], new_m[n], new_v[n] = (own_channels(n, small_full[n]), own_channels(n, d_),
                                                       own_channels(n, m_), own_channels(n, v_))

    last.update(adam_sharded(0, 1, partial, "last", names=BIG[1:]))
    for n, (g_, d_, m_, v_) in last.items():
        full = weights[n].shape
        grads[n], deltas[n], new_m[n], new_v[n] = g_.reshape(full), d_.reshape(full), m_.reshape(full), v_.reshape(full)

    order = ("norm_mix", "w_in", "conf_dw", "conf_dw_b", "conf_ln_g", "conf_ln_b", "pool_w", "pool_scale", "sc_conv",
             "gmlp_ln_g", "gmlp_ln_b", "gmlp_ws", "gmlp_bs", "w_branch", "w_out", "norm_mlp", "w_up", "w_down",
             "norm_ple", "w_ple", "w_pleg", "norm_final")
    return (loss, dxc.reshape(1, T, D), *[grads[n] for n in order], *[deltas[n] for n in order],
            *[new_m[n] for n in order], *[new_v[n] for n in order])
```

```python
import functools

import jax
import jax.numpy as jnp
import numpy as np
from jax import lax
from jax.experimental import pallas as pl
from jax.experimental.pallas import tpu as pltpu
from jax.experimental.pallas import tpu_sc as plsc

F32 = jnp.float32
BF16 = jnp.bfloat16

DEPTH = 4
T = 2048
D = 1024
W = 512
NDEV = 8
NCHIP = 4
EPS = 1e-6
CONF_K = 31
SC_K = 3
POOL_WINDOWS = (2, 4, 8, 16)
GW = 128
HB = 32
HA = 32
COLS_IN = 8192
MIX_COLS = 4096

ADAM_LR = 0.001
ADAM_B1 = 0.9
ADAM_B2 = 0.999
ADAM_EPS = 1e-08
ADAM_WD = 0.01
ADAM_STEP = 10

VMEM_LIMIT_BYTES = 56 * 1024 * 1024
MESH = pl.DeviceIdType.MESH


def _cp(*sem):
    return pltpu.CompilerParams(dimension_semantics=tuple(sem), vmem_limit_bytes=VMEM_LIMIT_BYTES)


def _sig(x):
    return jax.nn.sigmoid(x)


def _rms(x, g):
    r = lax.rsqrt(jnp.mean(x * x, axis=-1, keepdims=True) + EPS)
    return x * r * g


def _rms_bwd(dh, x, g, dres):
    r = lax.rsqrt(jnp.mean(x * x, axis=-1, keepdims=True) + EPS)
    xh = x * r
    u = dh * g
    dx = r * (u - xh * jnp.mean(u * xh, axis=-1, keepdims=True)) + dres
    dg = jnp.sum(dh * xh, axis=0, keepdims=True)
    return dx, dg


def _ln_stats(x):
    mu = jnp.mean(x, axis=-1, keepdims=True)
    xc = x - mu
    rstd = lax.rsqrt(jnp.mean(xc * xc, axis=-1, keepdims=True) + EPS)
    return xc * rstd, rstd


def _ln_bwd(dxh, xh, rstd):
    return rstd * (dxh - jnp.mean(dxh, axis=-1, keepdims=True) - xh * jnp.mean(dxh * xh, axis=-1, keepdims=True))


def _rowsum(x):
    return jnp.sum(x, axis=0, keepdims=True)


EPI_ROWS = 256


def _relu2_bf16(up):
    r = jnp.maximum(up.astype(F32), 0.0)
    return (r * r).astype(BF16)


def _mm(a, b3, *, mode, name, outs, trans_b=False, tm=512, tiles=(), params=(), epi=None, reds=(), a_pre=None,
        stream_first=False):
    t_, ka = a.shape
    nj, r, c = b3.shape
    kb, nb = (c, r) if trans_b else (r, c)
    nt = t_ // tm
    out_mode = mode == "out"
    full = mode == "full"
    assert trans_b or not full
    if out_mode:
        assert ka == kb and not reds
        grid = (nj, nt)
        a_map = lambda g0, g1: (g1, 0)
        b_map = lambda g0, g1: (g0, 0, 0)
        t_map = lambda g0, g1: (g1, g0)
        width = nj * nb
    else:
        assert ka == nj * kb
        grid = (nt, 1 if full else nj)
        a_map = lambda g0, g1: (g0, g1)
        b_map = lambda g0, g1: (g1, 0, 0)
        t_map = lambda g0, g1: (g0, 0)
        width = nb
    n_t, n_p, n_o, n_r = len(tiles), len(params), len(outs), len(reds)
    use_acc = (not out_mode) and nj > 1 and not full
    dims = (((1,), (1,)), ((), ())) if trans_b else (((1,), (0,)), ((), ()))

    def body(a_ref, b_ref, *rest):
        t_refs = rest[:n_t]
        p_refs = rest[n_t:n_t + n_p]
        o_refs = rest[n_t + n_p:n_t + n_p + n_o]
        r_refs = rest[n_t + n_p + n_o:n_t + n_p + n_o + n_r]
        i = pl.program_id(1 if out_mode else 0)
        a_val = a_ref[...] if a_pre is None else a_pre(a_ref[...])
        if full:
            b_all, b_sems = rest[-2 - stream_first], rest[-1 - stream_first]

            def weight_copies():
                return [pltpu.make_async_copy(b_ref.at[j], b_all.at[:, j * c:(j + 1) * c], b_sems.at[j])
                        for j in range(nj)]

            if stream_first:
                part = rest[-1]

                @pl.when(i == 0)
                def _():
                    cps = weight_copies()
                    for cp in cps:
                        cp.start()
                    acc = None
                    for j, cp in enumerate(cps):
                        cp.wait()
                        term = lax.dot_general(a_val[:, j * c:(j + 1) * c], b_all[:, j * c:(j + 1) * c], dims,
                                               preferred_element_type=F32)
                        acc = term if acc is None else acc + term
                    part[...] = acc

                @pl.when(i > 0)
                def _():
                    part[...] = lax.dot_general(a_val, b_all[...], dims, preferred_element_type=F32)
            else:
                @pl.when(i == 0)
                def _():
                    cps = weight_copies()
                    for cp in cps:
                        cp.start()
                    for cp in cps:
                        cp.wait()

                part = lax.dot_general(a_val, b_all[...], dims, preferred_element_type=F32)
        else:
            part = lax.dot_general(a_val, b_ref[...], dims, preferred_element_type=F32)

        def finish(acc_rows):
            totals = [None] * n_r
            for r0 in range(0, tm, min(tm, EPI_ROWS)):
                rows = slice(r0, r0 + min(tm, EPI_ROWS))
                if epi is None:
                    res, rr = (acc_rows(rows),), ()
                else:
                    res, rr = epi(acc_rows(rows), [t[rows, :] for t in t_refs], [p[...] for p in p_refs])
                for o_ref, val in zip(o_refs, res):
                    o_ref[rows, :] = val.astype(o_ref.dtype)
                totals = [val if tot is None else tot + val for tot, val in zip(totals, rr)]
            for r_ref, val in zip(r_refs, totals):
                @pl.when(i == 0)
                def _():
                    r_ref[...] = val

                @pl.when(i > 0)
                def _():
                    r_ref[...] += val

        if use_acc:
            acc_ref = rest[-1]
            j = pl.program_id(1)

            @pl.when(j == 0)
            def _():
                acc_ref[...] = part

            @pl.when(jnp.logical_and(j > 0, j < nj - 1))
            def _():
                acc_ref[...] += part

            @pl.when(j == nj - 1)
            def _():
                finish(lambda rows: acc_ref[rows, :] + part[rows])
        else:
            finish(lambda rows: part[rows, :])

    const2 = lambda g0, g1: (0, 0)
    if full:
        in_specs = [pl.BlockSpec((tm, ka), a_map), pl.BlockSpec(memory_space=pl.ANY)]
        scratch = [pltpu.VMEM((r, nj * c), b3.dtype), pltpu.SemaphoreType.DMA((nj,))]
        scratch += [pltpu.VMEM((tm, nb), F32)] if stream_first else []
    else:
        in_specs = [pl.BlockSpec((tm, kb), a_map), pl.BlockSpec((None, r, c), b_map)]
        scratch = [pltpu.VMEM((tm, nb), F32)] if use_acc else []
    in_specs += [pl.BlockSpec((tm, t.shape[1] // nj if out_mode else t.shape[1]), t_map) for t in tiles]
    in_specs += [pl.BlockSpec(p.shape, lambda g0, g1, nd=p.ndim: (0,) * nd) for p in params]
    out_specs = [pl.BlockSpec((tm, nb), t_map) for _ in outs] + [pl.BlockSpec((1, w), const2) for w in reds]
    out_shape = [jax.ShapeDtypeStruct((t_, width), dt) for dt in outs]
    out_shape += [jax.ShapeDtypeStruct((1, w), F32) for w in reds]
    res = pl.pallas_call(
        body, name=name, grid=grid, in_specs=in_specs, out_specs=out_specs, out_shape=out_shape,
        scratch_shapes=scratch, compiler_params=_cp("arbitrary", "arbitrary"),
    )(a, b3, *tiles, *params)
    return res


def _mm_tn(a, g, *, nj, split, name, out_dtype=BF16, a_pre=None):
    t_ = a.shape[0]
    if split == "col":
        r, c = a.shape[1], g.shape[1] // nj
        a_spec = pl.BlockSpec((t_, r), lambda j: (0, 0))
        g_spec = pl.BlockSpec((t_, c), lambda j: (0, j))
    else:
        r, c = a.shape[1] // nj, g.shape[1]
        a_spec = pl.BlockSpec((t_, r), lambda j: (0, j))
        g_spec = pl.BlockSpec((t_, c), lambda j: (0, 0))

    def body(a_ref, g_ref, o_ref):
        a_val = a_ref[...] if a_pre is None else a_pre(a_ref[...])
        o_ref[...] = lax.dot_general(a_val, g_ref[...], (((0,), (0,)), ((), ())),
                                     preferred_element_type=F32).astype(o_ref.dtype)

    return pl.pallas_call(
        body, name=name, grid=(nj,), in_specs=[a_spec, g_spec],
        out_specs=pl.BlockSpec((None, r, c), lambda j: (j, 0, 0)),
        out_shape=jax.ShapeDtypeStruct((nj, r, c), out_dtype),
        compiler_params=_cp("arbitrary"),
    )(a, g)


def _epi_res_norm(acc, tiles, params):
    x_new = tiles[0] + acc
    return (x_new, _rms(x_new, params[0])), ()


def _epi_ple(acc, tiles, params):
    x_old, p_tile = tiles
    g_next, w_ple8 = params
    pe = jnp.concatenate([jnp.dot(p_tile, w_ple8[j], preferred_element_type=F32) for j in range(NDEV)], axis=1)
    x_new = x_old + pe * _sig(acc)
    return (x_new, acc, _rms(x_new, g_next), pe), ()


def _epi_rms_bwd(acc, tiles, params):
    dx, dg = _rms_bwd(acc, tiles[0], params[0], tiles[1])
    return (dx, dx), (dg,)


def _epi_dup(acc, tiles, params):
    return (acc * (2.0 * jnp.maximum(tiles[0].astype(F32), 0.0)),), ()


def _tri_mask():
    row = lax.broadcasted_iota(jnp.int32, (GW, GW), 0)
    col = lax.broadcasted_iota(jnp.int32, (GW, GW), 1)
    return row >= col


def _small_specs(sp_list):
    return [pl.BlockSpec(p.shape, (lambda i: (0, 0)) if p.ndim == 2 else (lambda i: (0, 0, 0))) for p in sp_list]


SUBLANES = 8


def _tap_sum(src, w_ref, taps, rows, stage):
    groups = {}
    for off, k in taps:
        groups.setdefault(off % SUBLANES, []).append((off - off % SUBLANES, k))
    out = None
    for res, members in sorted(groups.items()):
        n = rows if res == 0 else rows + SUBLANES
        part = None
        for base, k in members:
            term = w_ref[k:k + 1, :] * src[pl.ds(base, n), :]
            part = term if part is None else part + term
        if res:
            stage[0:n, :] = part
            part = stage[pl.ds(res, rows), :]
        out = part if out is None else out + part
    return out


def _tap_grads(grad, src, offsets, rows, stage, out_ref):
    pad = SUBLANES
    stage[0:pad, :] = jnp.zeros((pad, grad.shape[1]), F32)
    stage[pad:pad + rows, :] = grad
    stage[pad + rows:2 * pad + rows, :] = jnp.zeros((pad, grad.shape[1]), F32)
    groups = {}
    for k, off in enumerate(offsets):
        groups.setdefault(off % SUBLANES, []).append((off - off % SUBLANES, k))
    for res, members in sorted(groups.items()):
        shifted = stage[pl.ds(pad - res, rows + pad), :]
        for base, k in members:
            prod = shifted * src[pl.ds(base, rows + pad), :]
            out_ref[k * SUBLANES:(k + 1) * SUBLANES, :] += prod.reshape((rows + pad) // SUBLANES, SUBLANES, -1).sum(axis=0)


def _mixer_params(sp):
    return [sp["cw"], sp["cb"], sp["lg"], sp["lb"], sp["pw"], sp["ps"], sp["sc"], sp["gg"], sp["gb"], sp["ws"], sp["bst"]]


def _mixer_fwd(proj, sp, tm=256):
    nt = T // tm
    per = tm // HB

    conv_taps = [(HB - (CONF_K - 1) + k, k) for k in range(CONF_K)]

    def body(main_ref, halo_ref, cw, cb, lg, lb, pw, ps, sc, gg, gb, ws, bst, y_ref, ca_ref, ext, stage):
        i = pl.program_id(0)
        keep = (i > 0).astype(F32)

        def mcol(c0):
            return main_ref[:, c0:c0 + W].astype(F32)

        def hcol(c0):
            return halo_ref[:, c0:c0 + W].astype(F32)

        ext[0:HB, :] = hcol(0) * _sig(hcol(W)) * keep
        ext[HB:HB + tm, :] = mcol(0) * _sig(mcol(W))
        ca = (_tap_sum(ext, cw, conv_taps, tm, stage) + cb[...]).astype(BF16)
        ca_ref[...] = ca
        xh, _ = _ln_stats(ca.astype(F32))
        n = xh * lg[...] + lb[...]
        y_ref[:, 0:W] = (n * _sig(n)).astype(BF16)

        pin = mcol(1024)
        ext[0:HB, :] = hcol(1024) * keep
        ext[HB:HB + tm, :] = pin
        pos = (i * tm + lax.broadcasted_iota(jnp.int32, (tm, 1), 0) + 1).astype(F32)
        for g, w in enumerate(POOL_WINDOWS):
            lo = g * GW
            s = ext[pl.ds(HB, tm), lo:lo + GW]
            for j in range(1, w):
                s = s + ext[pl.ds(HB - j, tm), lo:lo + GW]
            pooled = s / jnp.minimum(pos, float(w)) - pin[:, lo:lo + GW]
            mixed = jnp.dot(pooled.astype(BF16), pw[g].astype(BF16), preferred_element_type=F32)
            y_ref[:, W + lo:W + lo + GW] = (mixed * ps[:, lo:lo + GW]).astype(BF16)

        ext[0:HB, :] = hcol(2048) * hcol(2560) * keep
        ext[HB:HB + tm, :] = mcol(2048) * mcol(2560)
        cv = sc[0:1, :] * ext[pl.ds(HB - 2, tm), :]
        cv = cv + sc[1:2, :] * ext[pl.ds(HB - 1, tm), :]
        cv = cv + sc[2:3, :] * ext[pl.ds(HB, tm), :]
        y_ref[:, 2 * W:3 * W] = (mcol(1536) * cv).astype(BF16)

        vh, _ = _ln_stats(mcol(3584))
        vn = (vh * gg[...] + gb[...]).astype(BF16)
        u = mcol(3072)
        tri = _tri_mask()
        for g in range(4):
            lo = g * GW
            wm = jnp.where(tri, ws[g], 0.0).astype(BF16)
            for c in range(tm // GW):
                r0 = c * GW
                sg = jnp.dot(wm, vn[r0:r0 + GW, lo:lo + GW], preferred_element_type=F32) + bst[:, g:g + 1]
                y_ref[r0:r0 + GW, 3 * W + lo:3 * W + lo + GW] = (u[r0:r0 + GW, lo:lo + GW] * sg).astype(BF16)

    plist = _mixer_params(sp)
    in_specs = [pl.BlockSpec((tm, MIX_COLS), lambda i: (i, 0)),
                pl.BlockSpec((HB, MIX_COLS), lambda i: (jnp.maximum(i * per - 1, 0), 0))]
    in_specs += _small_specs(plist)
    return pl.pallas_call(
        body, name="f_mixers", grid=(nt,), in_specs=in_specs,
        out_specs=[pl.BlockSpec((tm, 4 * W), lambda i: (i, 0)), pl.BlockSpec((tm, W), lambda i: (i, 0))],
        out_shape=[jax.ShapeDtypeStruct((T, 4 * W), BF16), jax.ShapeDtypeStruct((T, W), BF16)],
        scratch_shapes=[pltpu.VMEM((HB + tm, W), F32), pltpu.VMEM((tm + SUBLANES, W), F32)],
        compiler_params=_cp("arbitrary"),
    )(proj, proj, *plist)


def _assemble_wb(wb8_ref, wbf_ref):
    for k in range(4):
        for j in range(NDEV):
            wbf_ref[k, :, j * GW:(j + 1) * GW] = wb8_ref[j, k]


def _merge_fwd(y, proj, wb8, tm=256):
    nt = T // tm

    def body(y_ref, gate_ref, wb8_ref, z_ref, s_ref, m_ref, wbf):
        @pl.when(pl.program_id(0) == 0)
        def _():
            _assemble_wb(wb8_ref, wbf)

        m = jnp.zeros((tm, D), F32)
        for k in range(4):
            zk = jnp.dot(y_ref[:, k * W:(k + 1) * W], wbf[k], preferred_element_type=F32)
            z_ref[:, k * D:(k + 1) * D] = zk.astype(BF16)
            s = _sig(gate_ref[:, k * D:(k + 1) * D].astype(F32))
            s_ref[:, k * D:(k + 1) * D] = s.astype(BF16)
            m = m + s * zk
        m_ref[...] = m.astype(BF16)

    return pl.pallas_call(
        body, name="f_merge", grid=(nt,),
        in_specs=[pl.BlockSpec((tm, 4 * W), lambda i: (i, 0)),
                  pl.BlockSpec((tm, 4 * D), lambda i: (i, 1)),
                  pl.BlockSpec(wb8.shape, lambda i: (0, 0, 0, 0))],
        out_specs=[pl.BlockSpec((tm, 4 * D), lambda i: (i, 0)), pl.BlockSpec((tm, 4 * D), lambda i: (i, 0)),
                   pl.BlockSpec((tm, D), lambda i: (i, 0))],
        out_shape=[jax.ShapeDtypeStruct((T, 4 * D), BF16), jax.ShapeDtypeStruct((T, 4 * D), BF16),
                   jax.ShapeDtypeStruct((T, D), BF16)],
        scratch_shapes=[pltpu.VMEM((4, W, D), BF16)],
        compiler_params=_cp("arbitrary"),
    )(y, proj, wb8)


def _merge_bwd(dm, z, gates, y, wb8, tm=256):
    nt = T // tm

    def body(dm_ref, z_ref, gate_ref, y_ref, wb8_ref, dp_ref, dy_ref, dwb_ref, wbf, acc):
        i = pl.program_id(0)

        @pl.when(i == 0)
        def _():
            _assemble_wb(wb8_ref, wbf)

        dmv = dm_ref[...].astype(F32)
        for k in range(4):
            s = gate_ref[:, k * D:(k + 1) * D].astype(F32)
            dzk = (dmv * s).astype(BF16)
            dp_ref[:, k * D:(k + 1) * D] = (dmv * z_ref[:, k * D:(k + 1) * D].astype(F32) * s * (1.0 - s)).astype(BF16)
            dyk = lax.dot_general(dzk, wbf[k], (((1,), (1,)), ((), ())), preferred_element_type=F32)
            dy_ref[:, k * W:(k + 1) * W] = dyk.astype(BF16)
            part = lax.dot_general(y_ref[:, k * W:(k + 1) * W], dzk, (((0,), (0,)), ((), ())),
                                   preferred_element_type=F32)

            @pl.when(i == 0)
            def _():
                acc[k] = part

            @pl.when(i > 0)
            def _():
                acc[k] += part

        @pl.when(i == nt - 1)
        def _():
            for k in range(4):
                for j in range(NDEV):
                    dwb_ref[j, k] = acc[k, :, j * GW:(j + 1) * GW].astype(BF16)

    return pl.pallas_call(
        body, name="b_merge", grid=(nt,),
        in_specs=[pl.BlockSpec((tm, D), lambda i: (i, 0)),
                  pl.BlockSpec((tm, 4 * D), lambda i: (i, 0)),
                  pl.BlockSpec((tm, 4 * D), lambda i: (i, 0)),
                  pl.BlockSpec((tm, 4 * W), lambda i: (i, 0)),
                  pl.BlockSpec(wb8.shape, lambda i: (0, 0, 0, 0))],
        out_specs=[pl.BlockSpec((tm, 4 * D), lambda i: (i, 1)),
                   pl.BlockSpec((tm, 4 * W), lambda i: (i, 0)),
                   pl.BlockSpec(wb8.shape, lambda i: (0, 0, 0, 0))],
        out_shape=[jax.ShapeDtypeStruct((T, COLS_IN), BF16),
                   jax.ShapeDtypeStruct((T, 4 * W), BF16),
                   jax.ShapeDtypeStruct(wb8.shape, BF16)],
        scratch_shapes=[pltpu.VMEM((4, W, D), BF16), pltpu.VMEM((4, W, D), F32)],
        compiler_params=_cp("arbitrary"),
    )(dm, z, gates, y, wb8)


def _mixer_bwd(proj, ca_saved, dy, dproj, sp, tm=256):
    nt = T // tm
    per = tm // HB
    ne = tm + HA
    last_blk = T // HA - 1
    conv_taps = [(HB - (CONF_K - 1) + k, k) for k in range(CONF_K)]

    rr, cc = np.arange(tm)[:, None], np.arange(tm + HB)[None, :]
    band_list = []
    for w in POOL_WINDOWS:
        band_list.append((rr + HB - cc >= 0) & (rr + HB - cc < w))
        band_list.append((cc - rr >= 0) & (cc - rr < w))
    bands_const = jnp.asarray(np.stack(band_list).astype(np.float32), dtype=BF16)

    def body(main_ref, hb_ref, ha_ref, ca_ref, cah_ref, dy_ref, dyh_ref, cw, cb, lg, lb, pw, ps, sc, gg, gb, ws, bst,
             bands, dp_any, dp_ref, dcw_ref, dsc_ref, vec_ref, dpw_ref, dws_ref, dbs_ref, e1, e2, e3, stage, dcw_acc):
        del dp_any, cb
        i = pl.program_id(0)
        keep_b = (i > 0).astype(F32)
        keep_a = (i < nt - 1).astype(F32)

        @pl.when(i == 0)
        def _():
            dcw_ref[...] = jnp.zeros_like(dcw_ref)
            dcw_acc[...] = jnp.zeros_like(dcw_acc)
            dsc_ref[...] = jnp.zeros_like(dsc_ref)
            vec_ref[...] = jnp.zeros_like(vec_ref)
            dpw_ref[...] = jnp.zeros_like(dpw_ref)
            dws_ref[...] = jnp.zeros_like(dws_ref)
            dbs_ref[...] = jnp.zeros_like(dbs_ref)

        def mcol(c0):
            return main_ref[:, c0:c0 + W].astype(F32)

        def hbcol(c0):
            return hb_ref[:, c0:c0 + W].astype(F32)

        def hacol(c0):
            return ha_ref[:, c0:c0 + W].astype(F32)

        def load_dy(c0):
            e2[0:tm, :] = dy_ref[:, c0:c0 + W].astype(F32)
            e2[tm:ne, :] = dyh_ref[:, c0:c0 + W].astype(F32) * keep_a

        a = mcol(0)
        sa = _sig(mcol(W))
        e1[0:HB, :] = hbcol(0) * _sig(hbcol(W)) * keep_b
        e1[HB:HB + tm, :] = a * sa
        e1[HB + tm:HB + tm + SUBLANES, :] = jnp.zeros((SUBLANES, W), F32)
        e2[0:tm, :] = ca_ref[...].astype(F32)
        e2[tm:ne, :] = cah_ref[...].astype(F32)
        xh, rstd = _ln_stats(e2[0:ne, :])
        nn = xh * lg[...] + lb[...]
        s = _sig(nn)
        load_dy(0)
        dn = e2[0:ne, :] * (s * (1.0 + nn * (1.0 - s)))
        vec_ref[1:2, :] += _rowsum(dn[0:tm] * xh[0:tm])
        vec_ref[2:3, :] += _rowsum(dn[0:tm])
        dca = _ln_bwd(dn * lg[...], xh, rstd)
        e3[0:ne, :] = dca
        dmain = dca[0:tm]
        vec_ref[0:1, :] += _rowsum(dmain)
        _tap_grads(dmain, e1, [off for off, _ in conv_taps], tm, stage, dcw_acc)

        @pl.when(i == nt - 1)
        def _():
            for k in range(CONF_K):
                dcw_ref[k:k + 1, :] = _rowsum(dcw_acc[k * SUBLANES:(k + 1) * SUBLANES, :])
        dglu = _tap_sum(e3, cw, [(CONF_K - 1 - k, k) for k in range(CONF_K)], tm, stage)
        dp_ref[:, 0:W] = (dglu * sa).astype(BF16)
        dp_ref[:, W:2 * W] = (dglu * a * sa * (1.0 - sa)).astype(BF16)

        pin = mcol(1024)
        e1[0:HB, :] = hbcol(1024) * keep_b
        e1[HB:HB + tm, :] = pin
        load_dy(W)
        dyb = e2[0:ne, :]
        pos_m = (i * tm + lax.broadcasted_iota(jnp.int32, (tm, 1), 0) + 1).astype(F32)
        pos_e = (i * tm + lax.broadcasted_iota(jnp.int32, (ne, 1), 0) + 1).astype(F32)
        for g, w in enumerate(POOL_WINDOWS):
            lo = g * GW
            acc = jnp.dot(bands[2 * g], e1[0:HB + tm, lo:lo + GW].astype(BF16), preferred_element_type=F32)
            pooled = (acc / jnp.minimum(pos_m, float(w)) - pin[:, lo:lo + GW]).astype(BF16)
            pwb = pw[g].astype(BF16)
            mixed = jnp.dot(pooled, pwb, preferred_element_type=F32)
            dyb_g = dyb[:, lo:lo + GW]
            vec_ref[3:4, lo:lo + GW] += _rowsum(dyb_g[0:tm] * mixed)
            dmb = (dyb_g * ps[:, lo:lo + GW]).astype(BF16)
            dpw_ref[g] += lax.dot_general(pooled, dmb[0:tm], (((0,), (0,)), ((), ())), preferred_element_type=F32)
            dpool = lax.dot_general(dmb, pwb, (((1,), (1,)), ((), ())), preferred_element_type=F32)
            scaled = (dpool / jnp.minimum(pos_e, float(w))).astype(BF16)
            back = jnp.dot(bands[2 * g + 1], scaled, preferred_element_type=F32)
            dp_ref[:, 1024 + lo:1024 + lo + GW] = (back - dpool[0:tm]).astype(BF16)

        cg = mcol(2048)
        hx = mcol(2560)
        e1[0:HB, :] = hbcol(2048) * hbcol(2560) * keep_b
        e1[HB:HB + tm, :] = cg * hx
        load_dy(2 * W)
        dyc = e2[0:tm, :]
        dconv = dyc * mcol(1536)
        e3[0:tm, :] = dconv
        e3[tm:ne, :] = e2[tm:ne, :] * hacol(1536)
        cv = sc[0:1, :] * e1[pl.ds(HB - 2, tm), :]
        for k in range(1, SC_K):
            cv = cv + sc[k:k + 1, :] * e1[pl.ds(HB - 2 + k, tm), :]
        dp_ref[:, 1536:2048] = (dyc * cv).astype(BF16)
        for k in range(SC_K):
            dsc_ref[k:k + 1, :] += _rowsum(dconv * e1[pl.ds(HB - 2 + k, tm), :])
        dq = sc[0:1, :] * e3[pl.ds(2, tm), :]
        for k in range(1, SC_K):
            dq = dq + sc[k:k + 1, :] * e3[pl.ds(2 - k, tm), :]
        dp_ref[:, 2048:2560] = (dq * hx).astype(BF16)
        dp_ref[:, 2560:3072] = (dq * cg).astype(BF16)

        u = mcol(3072)
        vh, vr = _ln_stats(mcol(3584))
        vn = (vh * gg[...] + gb[...]).astype(BF16)
        dyd = dy_ref[:, 3 * W:4 * W].astype(F32)
        tri = _tri_mask()
        for g in range(4):
            lo = g * GW
            wm = jnp.where(tri, ws[g], 0.0).astype(BF16)
            dws_g = jnp.zeros((GW, GW), F32)
            dbs_g = jnp.zeros((GW, 1), F32)
            for c in range(tm // GW):
                r0 = c * GW
                blk = vn[r0:r0 + GW, lo:lo + GW]
                sg = jnp.dot(wm, blk, preferred_element_type=F32) + bst[:, g:g + 1]
                dyd_b = dyd[r0:r0 + GW, lo:lo + GW]
                dp_ref[r0:r0 + GW, 3072 + lo:3072 + lo + GW] = (dyd_b * sg).astype(BF16)
                dsg = dyd_b * u[r0:r0 + GW, lo:lo + GW]
                dsgb = dsg.astype(BF16)
                dbs_g = dbs_g + jnp.sum(dsg, axis=-1, keepdims=True)
                dws_g = dws_g + lax.dot_general(dsgb, blk, (((1,), (1,)), ((), ())), preferred_element_type=F32)
                e1[r0:r0 + GW, lo:lo + GW] = lax.dot_general(wm, dsgb, (((0,), (0,)), ((), ())),
                                                             preferred_element_type=F32)
            dws_ref[g] += jnp.where(tri, dws_g, 0.0)
            dbs_ref[g] += jnp.broadcast_to(dbs_g, (GW, GW))
        dvn = e1[0:tm, :]
        vec_ref[4:5, :] += _rowsum(dvn * vh)
        vec_ref[5:6, :] += _rowsum(dvn)
        dp_ref[:, 3584:4096] = _ln_bwd(dvn * gg[...], vh, vr).astype(BF16)

    plist = _mixer_params(sp)
    in_specs = [pl.BlockSpec((tm, MIX_COLS), lambda i: (i, 0)),
                pl.BlockSpec((HB, MIX_COLS), lambda i: (jnp.maximum(i * per - 1, 0), 0)),
                pl.BlockSpec((HA, MIX_COLS), lambda i: (jnp.minimum((i + 1) * per, last_blk), 0)),
                pl.BlockSpec((tm, W), lambda i: (i, 0)),
                pl.BlockSpec((HA, W), lambda i: (jnp.minimum((i + 1) * per, last_blk), 0)),
                pl.BlockSpec((tm, 4 * W), lambda i: (i, 0)),
                pl.BlockSpec((HA, 4 * W), lambda i: (jnp.minimum((i + 1) * per, last_blk), 0))]
    in_specs += _small_specs(plist)
    in_specs += [pl.BlockSpec(bands_const.shape, lambda i: (0, 0, 0)), pl.BlockSpec(memory_space=pl.ANY)]
    z2 = lambda i: (0, 0)
    z3 = lambda i: (0, 0, 0)
    out_specs = [pl.BlockSpec((tm, MIX_COLS), lambda i: (i, 0)),
                 pl.BlockSpec((32, W), z2), pl.BlockSpec((8, W), z2), pl.BlockSpec((8, W), z2),
                 pl.BlockSpec((4, GW, GW), z3), pl.BlockSpec((4, GW, GW), z3), pl.BlockSpec((4, GW, GW), z3)]
    out_shape = [jax.ShapeDtypeStruct((T, COLS_IN), BF16),
                 jax.ShapeDtypeStruct((32, W), F32), jax.ShapeDtypeStruct((8, W), F32),
                 jax.ShapeDtypeStruct((8, W), F32),
                 jax.ShapeDtypeStruct((4, GW, GW), F32), jax.ShapeDtypeStruct((4, GW, GW), F32),
                 jax.ShapeDtypeStruct((4, GW, GW), F32)]
    n_in = 8 + len(plist)
    return pl.pallas_call(
        body, name="b_mixers", grid=(nt,), in_specs=in_specs, out_specs=out_specs, out_shape=out_shape,
        scratch_shapes=[pltpu.VMEM((HB + ne, W), F32), pltpu.VMEM((ne, W), F32), pltpu.VMEM((ne, W), F32),
                        pltpu.VMEM((ne + SUBLANES, W), F32), pltpu.VMEM((32 * SUBLANES, W), F32)],
        input_output_aliases={n_in: 0},
        compiler_params=_cp("arbitrary"),
    )(proj, proj, proj, ca_saved, ca_saved, dy, dy, *plist, bands_const, dproj)


def _norm_first(x, g, tm=512):
    def body(x_ref, g_ref, o_ref):
        o_ref[...] = _rms(x_ref[...], g_ref[...]).astype(BF16)

    return pl.pallas_call(
        body, name="f_norm0", grid=(T // tm,),
        in_specs=[pl.BlockSpec((tm, D), lambda i: (i, 0)), pl.BlockSpec((1, D), lambda i: (0, 0))],
        out_specs=pl.BlockSpec((tm, D), lambda i: (i, 0)),
        out_shape=jax.ShapeDtypeStruct((T, D), BF16), compiler_params=_cp("arbitrary"),
    )(x, g)


def _loss_head(x, target, g, tm=256):
    def body(x_ref, t_ref, g_ref, dx_ref, dg_ref, loss_ref):
        i = pl.program_id(0)
        x = x_ref[...]
        r = lax.rsqrt(jnp.mean(x * x, axis=-1, keepdims=True) + EPS)
        xh = x * r
        gv = g_ref[...]
        e = xh * gv - t_ref[...]
        dyv = e * (1.0 / D)
        part = jnp.sum(_rowsum(e * e), axis=-1, keepdims=True) * (0.5 / D)
        u = dyv * gv
        dx_ref[...] = r * (u - xh * jnp.mean(u * xh, axis=-1, keepdims=True))
        dgp = _rowsum(dyv * xh)

        @pl.when(i == 0)
        def _():
            dg_ref[...] = dgp
            loss_ref[...] = jnp.broadcast_to(part, (1, GW))

        @pl.when(i > 0)
        def _():
            dg_ref[...] += dgp
            loss_ref[...] += jnp.broadcast_to(part, (1, GW))

    return pl.pallas_call(
        body, name="loss_head", grid=(T // tm,),
        in_specs=[pl.BlockSpec((tm, D), lambda i: (i, 0)), pl.BlockSpec((tm, D), lambda i: (i, 0)),
                  pl.BlockSpec((1, D), lambda i: (0, 0))],
        out_specs=[pl.BlockSpec((tm, D), lambda i: (i, 0)), pl.BlockSpec((1, D), lambda i: (0, 0)),
                   pl.BlockSpec((1, GW), lambda i: (0, 0))],
        out_shape=[jax.ShapeDtypeStruct((T, D), F32), jax.ShapeDtypeStruct((1, D), F32),
                   jax.ShapeDtypeStruct((1, GW), F32)],
        compiler_params=_cp("arbitrary"),
    )(x, target, g)


def _out_bwd(dx2b, merged, w_out8, tm=512):
    nt = T // tm

    def body(dx_ref, mg_ref, w_ref, dm_ref, dw_ref, acc):
        i = pl.program_id(0)
        dx = dx_ref[...]
        dm_ref[...] = lax.dot_general(dx, w_ref[...], (((1,), (1,)), ((), ())),
                                      preferred_element_type=F32).astype(BF16)
        part = lax.dot_general(mg_ref[...], dx, (((0,), (0,)), ((), ())), preferred_element_type=F32)

        @pl.when(i == 0)
        def _():
            acc[...] = part

        @pl.when(i > 0)
        def _():
            acc[...] += part

        @pl.when(i == nt - 1)
        def _():
            for j in range(NDEV):
                dw_ref[j] = acc[j * GW:(j + 1) * GW, :].astype(BF16)

    tile = pl.BlockSpec((tm, D), lambda i: (i, 0))
    return pl.pallas_call(
        body, name="b_out", grid=(nt,),
        in_specs=[tile, tile, pl.BlockSpec((D, D), lambda i: (0, 0))],
        out_specs=[tile, pl.BlockSpec((NDEV, GW, D), lambda i: (0, 0, 0))],
        out_shape=[jax.ShapeDtypeStruct((T, D), BF16), jax.ShapeDtypeStruct((NDEV, GW, D), BF16)],
        scratch_shapes=[pltpu.VMEM((D, D), F32)],
        compiler_params=_cp("arbitrary"),
    )(dx2b, merged, w_out8.reshape(D, D))


def _ple_bwd(dx4, sv, w_pleg8, g_ple, tm=512):
    nt = T // tm
    ple_dim = sv["p"].shape[1]

    def body(dx_ref, gl_ref, pe_ref, x_ref, h_ref, p_ref, g_ref, wg_ref,
             dx3_ref, dx3b_ref, dg_ref, dwg_ref, dwp_ref, acc_g, acc_p):
        i = pl.program_id(0)
        d = dx_ref[...]
        s = _sig(gl_ref[...].astype(F32))
        dpe = (d * s).astype(BF16)
        dgl = (d * pe_ref[...].astype(F32) * s * (1.0 - s)).astype(BF16)
        dh = lax.dot_general(dgl, wg_ref[...], (((1,), (1,)), ((), ())), preferred_element_type=F32)
        dx, dgp = _rms_bwd(dh, x_ref[...], g_ref[...], d)
        dx3_ref[...] = dx
        dx3b_ref[...] = dx.astype(BF16)
        part_g = lax.dot_general(h_ref[...], dgl, (((0,), (0,)), ((), ())), preferred_element_type=F32)
        part_p = lax.dot_general(p_ref[...], dpe, (((0,), (0,)), ((), ())), preferred_element_type=F32)

        @pl.when(i == 0)
        def _():
            dg_ref[...] = dgp
            acc_g[...] = part_g
            acc_p[...] = part_p

        @pl.when(i > 0)
        def _():
            dg_ref[...] += dgp
            acc_g[...] += part_g
            acc_p[...] += part_p

        @pl.when(i == nt - 1)
        def _():
            for j in range(NDEV):
                dwg_ref[j] = acc_g[j * GW:(j + 1) * GW, :].astype(BF16)
                dwp_ref[j] = acc_p[:, j * GW:(j + 1) * GW].astype(BF16)

    tile = lambda w: pl.BlockSpec((tm, w), lambda i: (i, 0))
    const = lambda shp: pl.BlockSpec(shp, lambda i: (0,) * len(shp))
    return pl.pallas_call(
        body, name="b_ple", grid=(nt,),
        in_specs=[tile(D), tile(D), tile(D), tile(D), tile(D), tile(ple_dim), const((1, D)), const((D, D))],
        out_specs=[tile(D), tile(D), const((1, D)), const((NDEV, GW, D)), const((NDEV, ple_dim, GW))],
        out_shape=[jax.ShapeDtypeStruct((T, D), F32), jax.ShapeDtypeStruct((T, D), BF16),
                   jax.ShapeDtypeStruct((1, D), F32), jax.ShapeDtypeStruct((NDEV, GW, D), BF16),
                   jax.ShapeDtypeStruct((NDEV, ple_dim, GW), BF16)],
        scratch_shapes=[pltpu.VMEM((D, D), F32), pltpu.VMEM((ple_dim, D), F32)],
        compiler_params=_cp("arbitrary"),
    )(dx4, sv["gl"], sv["pe"], sv["x3"], sv["h3"], sv["p"], g_ple, w_pleg8.reshape(D, D))


def _layer_fwd(x, h1, p_bf, gw, sp, g_next):
    proj, = _mm(h1, gw["w_in"], mode="out", name="f_proj", outs=[BF16], tm=T)
    y, ca = _mixer_fwd(proj, sp)
    z, gates, merged = _merge_fwd(y, proj, gw["w_branch"])
    x2, h2 = _mm(merged, gw["w_out"].reshape(1, D, D), mode="acc", name="f_out", outs=[F32, BF16], tm=T // 2,
                 tiles=[x], params=[sp["g_mlp"]], epi=_epi_res_norm)
    up, = _mm(h2, gw["w_up"], mode="out", name="f_up", outs=[BF16], tm=T)
    x3, h3 = _mm(up, gw["w_down"].reshape(1, 4 * D, D), mode="acc", name="f_down", outs=[F32, BF16], tm=T // 4,
                 tiles=[x2], params=[sp["g_ple"]], epi=_epi_res_norm, a_pre=_relu2_bf16)
    x4, gl, hn, pe = _mm(h3, gw["w_pleg"].reshape(1, D, D), mode="acc", name="f_gate", tm=T // 2,
                         outs=[F32, BF16, BF16, BF16], tiles=[x3, p_bf], params=[g_next, gw["w_ple"]], epi=_epi_ple)
    saved = dict(x=x, h1=h1, proj=proj, y=y, ca=ca, z=z, gates=gates, merged=merged, x2=x2, h2=h2, up=up, x3=x3, h3=h3,
                 pe=pe, gl=gl, p=p_bf)
    return x4, hn, saved


def _layer_bwd(dx4, sv, gw, sp, submit, early_group=False):
    dw = {}
    dx3, dx3b, dg_ple, dw["w_pleg"], dw["w_ple"] = _ple_bwd(dx4, sv, gw["w_pleg"], sp["g_ple"])
    dup, = _mm(dx3b, gw["w_down"], mode="out", trans_b=True, name="b_dact", outs=[BF16], tm=T,
               tiles=[sv["up"]], epi=_epi_dup)
    dw["w_down"] = _mm_tn(sv["up"], dx3b, nj=NDEV, split="row", name="b_dw_down", a_pre=_relu2_bf16)
    dw["w_up"] = _mm_tn(sv["h2"], dup, nj=NDEV, split="col", name="b_dw_up")
    if early_group:
        dw["w_up"], dup = lax.optimization_barrier((dw["w_up"], dup))
        dup = submit(dw, ("w_up", "w_down", "w_ple", "w_pleg"), dup)
    dx2, dx2b, dg_mlp = _mm(dup, gw["w_up"], mode="full", trans_b=True, name="b_dh2", tm=T // 8, stream_first=True,
                            outs=[F32, BF16], tiles=[sv["x2"], dx3], params=[sp["g_mlp"]], epi=_epi_rms_bwd, reds=[D])
    dm, dw["w_out"] = _out_bwd(dx2b, sv["merged"], gw["w_out"])
    dproj, dy, dw["w_branch"] = _merge_bwd(dm, sv["z"], sv["gates"], sv["y"], gw["w_branch"])
    dy = submit(dw, ("w_branch", "w_out") if early_group else BIG[1:], dy)
    dproj, dcw, dsc, vec, dpw, dws, dbs = _mixer_bwd(sv["proj"], sv["ca"], dy, dproj, sp)
    dw["w_in"] = _mm_tn(sv["h1"], dproj, nj=NDEV, split="col", name="b_dw_in")
    dw["w_in"], dproj = lax.optimization_barrier((dw["w_in"], dproj))
    dproj = submit(dw, BIG[:1], dproj)
    dx, dg_mix = _mm(dproj, gw["w_in"], mode="full", trans_b=True, name="b_dh1", outs=[F32], tm=T // 8,
                     stream_first=True,
                     tiles=[sv["x"], dx2], params=[sp["g_mix"]], epi=_epi_rms_bwd, reds=[D])
    small = dict(norm_mix=dg_mix[0], conf_dw=dcw[:CONF_K], conf_dw_b=vec[0], conf_ln_g=vec[1], conf_ln_b=vec[2],
                 pool_w=dpw, pool_scale=vec[3], sc_conv=dsc[:SC_K], gmlp_ln_g=vec[4], gmlp_ln_b=vec[5],
                 gmlp_ws=dws, gmlp_bs=dbs[:, :, 0], norm_mlp=dg_mlp[0], norm_ple=dg_ple[0])
    return dx, small


ANY = pl.BlockSpec(memory_space=pl.ANY)


def _mesh_pos():
    return lax.axis_index("x"), lax.axis_index("y"), lax.axis_index("c")


def _other_chips(x, y):
    return [(1 - x, y), (x, 1 - y), (1 - x, 1 - y)]


def _launch_comm(body, peers_of, operands, out_shapes, sems, name, seq_id):
    n_in, n_out = len(operands), len(out_shapes)
    if seq_id is None:
        return pl.pallas_call(body, name=name, in_specs=[ANY] * n_in, out_specs=[ANY] * n_out,
                              out_shape=out_shapes, scratch_shapes=sems)(*operands)

    def seq_body(*refs):
        peers = peers_of(*_mesh_pos())
        barrier = pltpu.get_barrier_semaphore()
        for peer in peers:
            pl.semaphore_signal(barrier, inc=1, device_id=peer, device_id_type=MESH)
        pl.semaphore_wait(barrier, len(peers))
        body(*refs)

    return pl.kernel(seq_body, name=name, out_type=out_shapes,
                     mesh=plsc.ScalarSubcoreMesh(axis_name="seq", num_cores=1), scratch_types=sems,
                     compiler_params=pltpu.CompilerParams(collective_id=seq_id))(*operands)


def _all_gather(shards, name, seq_id=None):
    n = len(shards)

    def body(*refs):
        s_refs, o_refs = refs[:n], refs[n:2 * n]
        send_sems, recv_sems, local_sems = refs[2 * n:]
        x, y, c = _mesh_pos()
        me = 4 * x + 2 * y + c
        here = (x, y, c)
        sibling = (x, y, 1 - c)
        chips = _other_chips(x, y)

        def slot(px, py, pc):
            return 4 * px + 2 * py + pc

        def copy(t, k, slot_idx, to, src=None):
            dst = o_refs[t].at[slot_idx]
            return pltpu.make_async_remote_copy(
                src_ref=dst if src is None else src, dst_ref=dst,
                send_sem=send_sems.at[t * 7 + k], recv_sem=recv_sems.at[t * 7 + k],
                device_id=to, device_id_type=MESH)

        mine = [pltpu.make_async_copy(s_refs[t], o_refs[t].at[me], local_sems.at[t]) for t in range(n)]
        for cp in mine:
            cp.start()
        first = []
        for t in range(n):
            for j, chip in enumerate(chips):
                first.append(copy(t, 1 + j, me, (*chip, c), src=s_refs[t]))
        for t in range(n):
            first.append(copy(t, 0, me, sibling, src=s_refs[t]))
        for cp in first:
            cp.start()
        passed = []
        for t in range(n):
            for j, chip in enumerate(chips):
                copy(t, 1 + j, slot(*chip, c), here).wait_recv()
                fwd = copy(t, 4 + j, slot(*chip, c), sibling)
                fwd.start()
                passed.append(fwd)
        for t in range(n):
            copy(t, 0, slot(x, y, 1 - c), here).wait_recv()
            for j, chip in enumerate(chips):
                copy(t, 4 + j, slot(*chip, 1 - c), here).wait_recv()
        for cp in first + passed:
            cp.wait_send()
        for cp in mine:
            cp.wait()

    def peers_of(x, y, c):
        return [(x, y, 1 - c)] + [(*chip, c) for chip in _other_chips(x, y)]

    return _launch_comm(
        body, peers_of, shards, [jax.ShapeDtypeStruct((NDEV,) + s.shape, s.dtype) for s in shards],
        [pltpu.SemaphoreType.DMA((7 * n,)), pltpu.SemaphoreType.DMA((7 * n,)), pltpu.SemaphoreType.DMA((n,))],
        name, seq_id)


def _rs_exchange(p4s, qs, name, seq_id=None):
    n_p, n_q = len(p4s), len(qs)

    def body(*refs):
        p_refs, q_refs = refs[:n_p], refs[n_p:n_p + n_q]
        rb_refs, rc_refs = refs[n_p + n_q:2 * n_p + n_q], refs[2 * n_p + n_q:2 * (n_p + n_q)]
        pair_send, pair_recv, chip_send, chip_recv, local_sems = refs[2 * (n_p + n_q):]
        x, y, c = _mesh_pos()
        a_idx = 2 * x + y
        chips = _other_chips(x, y)
        mine = [pltpu.make_async_copy(q_refs[t].at[a_idx], rc_refs[t].at[a_idx], local_sems.at[t])
                for t in range(n_q)]
        sends = []
        for t in range(n_q):
            for j, chip in enumerate(chips):
                sends.append(pltpu.make_async_remote_copy(
                    src_ref=q_refs[t].at[2 * chip[0] + chip[1]], dst_ref=rc_refs[t].at[a_idx],
                    send_sem=chip_send.at[t * 3 + j], recv_sem=chip_recv.at[t * 3 + j],
                    device_id=(*chip, c), device_id_type=MESH))
        pairs = [pltpu.make_async_remote_copy(
            src_ref=p_refs[t].at[:, 1 - c], dst_ref=rb_refs[t], send_sem=pair_send.at[t], recv_sem=pair_recv.at[t],
            device_id=(x, y, 1 - c), device_id_type=MESH) for t in range(n_p)]
        for cp in sends + mine + pairs:
            cp.start()
        for cp in pairs:
            cp.wait()
        for t in range(n_q):
            for j, chip in enumerate(chips):
                landed = rc_refs[t].at[2 * chip[0] + chip[1]]
                pltpu.make_async_remote_copy(
                    src_ref=landed, dst_ref=landed, send_sem=chip_send.at[t * 3 + j],
                    recv_sem=chip_recv.at[t * 3 + j], device_id=(x, y, c), device_id_type=MESH).wait_recv()
        for cp in sends:
            cp.wait_send()
        for cp in mine:
            cp.wait()

    def peers_of(x, y, c):
        peers = [(x, y, 1 - c)] if n_p else []
        return peers + ([(*chip, c) for chip in _other_chips(x, y)] if n_q else [])

    out_shapes = [jax.ShapeDtypeStruct((NCHIP,) + p.shape[2:], p.dtype) for p in p4s]
    out_shapes += [jax.ShapeDtypeStruct(q.shape, q.dtype) for q in qs]
    sems = [pltpu.SemaphoreType.DMA((max(n_p, 1),)), pltpu.SemaphoreType.DMA((max(n_p, 1),)),
            pltpu.SemaphoreType.DMA((max(3 * n_q, 1),)), pltpu.SemaphoreType.DMA((max(3 * n_q, 1),)),
            pltpu.SemaphoreType.DMA((max(n_q, 1),))]
    got = _launch_comm(body, peers_of, list(p4s) + list(qs), out_shapes, sems, name, seq_id)
    return got[:n_p], got[n_p:]


def _pair_sum(p4s, rbs, c_idx, name, nst=1):
    n = len(p4s)
    trs = [p.shape[2] // nst for p in p4s]

    def body(c_ref, *refs):
        del c_ref
        p_refs, r_refs, o_refs = refs[:n], refs[n:2 * n], refs[2 * n:]
        for p_ref, r_ref, o_ref in zip(p_refs, r_refs, o_refs):
            o_ref[...] = (p_ref[...].astype(F32) + r_ref[...].astype(F32)).astype(o_ref.dtype)

    in_specs = [pl.BlockSpec((None, None, tr, p.shape[3]), lambda b, i, c_ref: (b, c_ref[0], i, 0))
                for p, tr in zip(p4s, trs)]
    in_specs += [pl.BlockSpec((None, tr, p.shape[3]), lambda b, i, c_ref: (b, i, 0)) for p, tr in zip(p4s, trs)]
    out_specs = [pl.BlockSpec((None, tr, p.shape[3]), lambda b, i, c_ref: (b, i, 0)) for p, tr in zip(p4s, trs)]
    return pl.pallas_call(
        body, name=name,
        grid_spec=pltpu.PrefetchScalarGridSpec(num_scalar_prefetch=1, grid=(NCHIP, nst), in_specs=in_specs,
                                               out_specs=out_specs),
        out_shape=[jax.ShapeDtypeStruct((NCHIP,) + p.shape[2:], p.dtype) for p in p4s],
        compiler_params=_cp("arbitrary", "arbitrary"),
    )(c_idx, *p4s, *rbs)


class _GradientPipeline:
    def __init__(self, c_idx, results):
        self.c_idx, self.results, self.pending = c_idx, results, None

    def _sum_pending(self, chain):
        names, layer, p4s, rbs = self.pending
        qs = _pair_sum(p4s, rbs, self.c_idx, name="rs_pairsum_%d" % len(names))
        return lax.optimization_barrier((chain, qs))

    def submit(self, dw, names, layer, chain):
        qs, tag, seq_id = [], "pair", 3
        if self.pending is not None:
            chain, qs = self._sum_pending(chain)
            tag, seq_id = "pair_chip", 4
        p4s = [dw[n].reshape((NCHIP, 2) + BIG_SHARD[n]) for n in names]
        rbs, rcs = _rs_exchange(p4s, qs, name="rs_%s_%d" % (tag, len(names)), seq_id=seq_id)
        self._record(rcs)
        self.pending = (names, layer, p4s, rbs)
        return chain

    def finish(self, chain):
        chain, qs = self._sum_pending(chain)
        self._record(_rs_exchange([], qs, name="rs_chip_last", seq_id=5)[1])
        self.pending = None
        return chain

    def _record(self, rcs):
        if rcs:
            names, layer = self.pending[:2]
            for n, rc in zip(names, rcs):
                self.results[n][layer] = rc


def _adamw(w, g, m, v):
    m = ADAM_B1 * m + (1.0 - ADAM_B1) * g
    v = ADAM_B2 * v + (1.0 - ADAM_B2) * (g * g)
    m_hat = m / (1.0 - ADAM_B1 ** ADAM_STEP)
    v_hat = v / (1.0 - ADAM_B2 ** ADAM_STEP)
    delta = -ADAM_LR * (m_hat / (jnp.sqrt(v_hat) + ADAM_EPS) + ADAM_WD * w)
    return delta, m, v


def _adam_sharded(rcs, w, m, v, tr, name, first_layer, partial=None):
    _, r, c = w.shape
    nst = r // tr
    n_l = len(rcs)

    def body(*refs):
        rc_refs = refs[:n_l]
        w_ref, m_ref, v_ref = refs[n_l:n_l + 3]
        g_out, d_out, m_out, v_out = refs[-4:]
        layer = pl.program_id(0)
        for k, rc in enumerate(rc_refs):
            @pl.when(layer == k)
            def _():
                g = rc[0].astype(F32) + rc[1].astype(F32) + rc[2].astype(F32) + rc[3].astype(F32)
                delta, m_new, v_new = _adamw(w_ref[...], g, m_ref[...], v_ref[...])
                g_out[...] = g
                d_out[...] = delta
                m_out[...] = m_new
                v_out[...] = v_new

    rc_specs = [pl.BlockSpec((NCHIP, tr, c), lambda l, i, k=k: (0, jnp.where(l == k, i, 0), 0)) for k in range(n_l)]
    wspec = pl.BlockSpec((None, tr, c), lambda l, i: (first_layer + l, i, 0))
    carried = [] if partial is None else list(partial)
    return pl.pallas_call(
        body, name=name, grid=(n_l, nst),
        in_specs=rc_specs + [wspec] * 3 + [pl.BlockSpec(memory_space=pl.ANY)] * len(carried),
        out_specs=[wspec] * 4, out_shape=[jax.ShapeDtypeStruct(w.shape, F32)] * 4,
        input_output_aliases={n_l + 3 + k: k for k in range(len(carried))},
        compiler_params=_cp("arbitrary", "arbitrary"),
    )(*rcs, w, m, v, *carried)


def _adam_packed(g, w, m, v, direct):
    n_d = len(direct)

    def pieces(shape):
        width = shape[-1]
        count = 1
        for s in shape[:-1]:
            count *= s
        per_row = D // width
        out = []
        for k in range(count):
            idx = (k,) if len(shape) == 2 else (k // shape[1], k % shape[1])
            out.append((idx, k // per_row, (k % per_row) * width, width))
        return out

    def body(g_ref, w_ref, m_ref, v_ref, d_out, m_out, v_out, *outs):
        delta, m_new, v_new = _adamw(w_ref[...], g_ref[...], m_ref[...], v_ref[...])
        d_out[...] = delta
        m_out[...] = m_new
        v_out[...] = v_new
        for a, (_, row0, shape) in enumerate(direct):
            for src, dst in zip((g_ref, d_out, m_out, v_out), outs[4 * a:4 * a + 4]):
                for idx, row, lane0, width in pieces(shape):
                    piece = src[pl.ds(row0 + row, 1), lane0:lane0 + width]
                    if len(idx) == 1:
                        dst[pl.ds(idx[0], 1), :] = piece
                    else:
                        dst[idx[0], pl.ds(idx[1], 1), :] = piece

    out_shape = [jax.ShapeDtypeStruct(g.shape, F32)] * 3
    for _, _, shape in direct:
        out_shape += [jax.ShapeDtypeStruct(shape, F32)] * 4
    res = pl.pallas_call(body, name="adam_small", out_shape=out_shape,
                         compiler_params=pltpu.CompilerParams(vmem_limit_bytes=VMEM_LIMIT_BYTES))(g, w, m, v)
    return res[:3], {name: res[3 + 4 * a:7 + 4 * a] for a, (name, _, _) in enumerate(direct)}


def _sum4(rc):
    def body(rc_ref, o_ref):
        o_ref[...] = rc_ref[0] + rc_ref[1] + rc_ref[2] + rc_ref[3]

    return pl.pallas_call(
        body, name="small_sum", out_shape=jax.ShapeDtypeStruct(rc.shape[1:], F32),
    )(rc)


BIG = ("w_in", "w_branch", "w_out", "w_up", "w_down", "w_ple", "w_pleg")
BIG_SHARD = {"w_in": (D, D), "w_branch": (4 * W, GW), "w_out": (GW, D), "w_up": (D, W), "w_down": (W, D),
             "w_ple": (256, GW), "w_pleg": (GW, D)}
ADAM_ROWS = {"w_in": 256, "w_branch": 512, "w_out": 128, "w_up": 256, "w_down": 256, "w_ple": 256, "w_pleg": 128}
SMALL = (("norm_mix", (DEPTH, D)), ("conf_dw", (DEPTH, CONF_K, W)), ("conf_dw_b", (DEPTH, W)),
         ("conf_ln_g", (DEPTH, W)), ("conf_ln_b", (DEPTH, W)), ("pool_w", (DEPTH, 4, GW, GW)),
         ("pool_scale", (DEPTH, W)), ("sc_conv", (DEPTH, SC_K, W)), ("gmlp_ln_g", (DEPTH, W)),
         ("gmlp_ln_b", (DEPTH, W)), ("gmlp_ws", (DEPTH, 4, GW, GW)), ("gmlp_bs", (DEPTH, 4, GW)),
         ("norm_mlp", (DEPTH, D)), ("norm_ple", (DEPTH, D)), ("norm_final", (D,)))
CHANNEL_SHARDED = ("conf_dw", "sc_conv")
SMALL_ROWS = 80


def _pack(arrs, rows):
    flat = jnp.concatenate([a.reshape(-1) for a in arrs])
    return jnp.pad(flat, (0, rows * D - flat.shape[0])).reshape(rows, D)


def _unpack(packed, shapes):
    flat = packed.reshape(-1)
    out, off = [], 0
    for shp in shapes:
        size = 1
        for s in shp:
            size *= s
        out.append(flat[off:off + size].reshape(shp))
        off += size
    return out


def kernel(x, p, norm_mix, w_in, conf_dw, conf_dw_b, conf_ln_g, conf_ln_b, pool_w, pool_scale, sc_conv, gmlp_ln_g, gmlp_ln_b, gmlp_ws, gmlp_bs, w_branch, w_out, norm_mlp, w_up, w_down, norm_ple, w_ple, w_ple_gate, norm_final, loss_target, m_norm_mix, m_w_in, m_conf_dw, m_conf_dw_b, m_conf_ln_g, m_conf_ln_b, m_pool_w, m_pool_scale, m_sc_conv, m_gmlp_ln_g, m_gmlp_ln_b, m_gmlp_ws, m_gmlp_bs, m_w_branch, m_w_out, m_norm_mlp, m_w_up, m_w_down, m_norm_ple, m_w_ple, m_w_ple_gate, m_norm_final, v_norm_mix, v_w_in, v_conf_dw, v_conf_dw_b, v_conf_ln_g, v_conf_ln_b, v_pool_w, v_pool_scale, v_sc_conv, v_gmlp_ln_g, v_gmlp_ln_b, v_gmlp_ws, v_gmlp_bs, v_w_branch, v_w_out, v_norm_mlp, v_w_up, v_w_down, v_norm_ple, v_w_ple, v_w_ple_gate, v_norm_final):
    weights = dict(norm_mix=norm_mix, w_in=w_in, conf_dw=conf_dw, conf_dw_b=conf_dw_b, conf_ln_g=conf_ln_g,
                   conf_ln_b=conf_ln_b, pool_w=pool_w, pool_scale=pool_scale, sc_conv=sc_conv, gmlp_ln_g=gmlp_ln_g,
                   gmlp_ln_b=gmlp_ln_b, gmlp_ws=gmlp_ws, gmlp_bs=gmlp_bs, w_branch=w_branch, w_out=w_out,
                   norm_mlp=norm_mlp, w_up=w_up, w_down=w_down, norm_ple=norm_ple, w_ple=w_ple, w_pleg=w_ple_gate,
                   norm_final=norm_final)
    mom1 = dict(norm_mix=m_norm_mix, w_in=m_w_in, conf_dw=m_conf_dw, conf_dw_b=m_conf_dw_b, conf_ln_g=m_conf_ln_g,
                conf_ln_b=m_conf_ln_b, pool_w=m_pool_w, pool_scale=m_pool_scale, sc_conv=m_sc_conv,
                gmlp_ln_g=m_gmlp_ln_g, gmlp_ln_b=m_gmlp_ln_b, gmlp_ws=m_gmlp_ws, gmlp_bs=m_gmlp_bs,
                w_branch=m_w_branch, w_out=m_w_out, norm_mlp=m_norm_mlp, w_up=m_w_up, w_down=m_w_down,
                norm_ple=m_norm_ple, w_ple=m_w_ple, w_pleg=m_w_ple_gate, norm_final=m_norm_final)
    mom2 = dict(norm_mix=v_norm_mix, w_in=v_w_in, conf_dw=v_conf_dw, conf_dw_b=v_conf_dw_b, conf_ln_g=v_conf_ln_g,
                conf_ln_b=v_conf_ln_b, pool_w=v_pool_w, pool_scale=v_pool_scale, sc_conv=v_sc_conv,
                gmlp_ln_g=v_gmlp_ln_g, gmlp_ln_b=v_gmlp_ln_b, gmlp_ws=v_gmlp_ws, gmlp_bs=v_gmlp_bs,
                w_branch=v_w_branch, w_out=v_w_out, norm_mlp=v_norm_mlp, w_up=v_w_up, w_down=v_w_down,
                norm_ple=v_norm_ple, w_ple=v_w_ple, w_pleg=v_w_ple_gate, norm_final=v_norm_final)

    xi, yi, ci = _mesh_pos()
    me = 4 * xi + 2 * yi + ci
    c_idx = jnp.reshape(ci, (1,)).astype(jnp.int32)

    gathered, conf_full, sc_full = [], [], []
    for l in range(DEPTH):
        shard = lambda n: weights[n][l].astype(BF16).reshape(BIG_SHARD[n])
        w_in_g, conf_g, sc_g = _all_gather([shard("w_in"), conf_dw[l], sc_conv[l]], name="ag_first", seq_id=1)
        if l + 1 < DEPTH:
            rest = _all_gather([shard(n) for n in BIG[1:]], name="ag_rest", seq_id=2)
        else:
            rest = (list(_all_gather([shard(n) for n in BIG[1:4]], name="ag_rest_a", seq_id=2))
                    + list(_all_gather([shard(n) for n in BIG[4:]], name="ag_rest_b", seq_id=2)))
        gw = dict(zip(BIG[1:], rest), w_in=w_in_g)
        gw["w_branch"] = gw["w_branch"].reshape(NDEV, 4, W, GW)
        gathered.append(gw)
        conf_full.append(conf_g)
        sc_full.append(sc_g)

    def small_params(l):
        return dict(cw=conf_full[l], cb=conf_dw_b[l][None], lg=conf_ln_g[l][None], lb=conf_ln_b[l][None],
                    pw=pool_w[l], ps=pool_scale[l][None], sc=sc_full[l], gg=gmlp_ln_g[l][None],
                    gb=gmlp_ln_b[l][None], ws=gmlp_ws[l], bst=gmlp_bs[l].T, g_mix=norm_mix[l][None],
                    g_mlp=norm_mlp[l][None], g_ple=norm_ple[l][None])

    xc = x.reshape(T, D)
    small_names = [n for n, _ in SMALL]

    def in_gradient_layout(n, shard, shape):
        if n not in CHANNEL_SHARDED:
            return shard
        return lax.dynamic_update_slice(jnp.zeros(shape, F32), shard, (0, 0, me * (W // NDEV)))

    small_state = [_pack([in_gradient_layout(n, src[n], shape) for n, shape in SMALL], NDEV * SMALL_ROWS)
                   for src in (weights, mom1, mom2)]
    xc, small_state = lax.optimization_barrier((xc, small_state))
    p_bf = p.reshape(DEPTH, T, 256).astype(BF16)
    h = _norm_first(xc, norm_mix[0][None])
    saved = []
    for l in range(DEPTH):
        g_next = norm_mix[l + 1][None] if l + 1 < DEPTH else norm_final[None]
        h, conf_g, sc_g = lax.optimization_barrier((h, conf_full[l], sc_full[l]))
        conf_full[l] = conf_g.transpose(1, 0, 2).reshape(CONF_K, W)
        sc_full[l] = sc_g.transpose(1, 0, 2).reshape(SC_K, W)
        xc, h, sv = _layer_fwd(xc, h, p_bf[l], gathered[l], small_params(l), g_next)
        saved.append(sv)

    dxc, dg_final, loss_part = _loss_head(xc, loss_target.reshape(T, D), norm_final[None])
    loss = lax.psum(loss_part[0, 0], ("x", "y", "c"))
    small_grads = [None] * DEPTH
    rcs = {n: [None] * DEPTH for n in BIG}
    pipeline = _GradientPipeline(c_idx, rcs)
    for l in reversed(range(DEPTH)):
        dxc, small_grads[l] = _layer_bwd(dxc, saved[l], gathered[l], small_params(l),
                                         lambda dw, names, value, l=l: pipeline.submit(dw, names, l, value),
                                         early_group=(l == 0))

    def adam_sharded(first_layer, n_layers, partial, tag, names=BIG):
        outs = {}
        for n in names:
            shp = (DEPTH,) + BIG_SHARD[n]
            outs[n] = _adam_sharded(rcs[n][first_layer:first_layer + n_layers], weights[n].reshape(shp),
                                    mom1[n].reshape(shp), mom2[n].reshape(shp), ADAM_ROWS[n],
                                    "adam_%s_%s" % (n, tag), first_layer, None if partial is None else partial[n])
        return outs

    stacked = {n: jnp.stack([small_grads[l][n] for l in range(DEPTH)]) for n, _ in SMALL if n != "norm_final"}
    stacked["norm_final"] = dg_final[0]
    packed = _pack([stacked[n] for n, _ in SMALL], NDEV * SMALL_ROWS).reshape(NCHIP, 2, SMALL_ROWS, D)
    (pair_small,), _ = _rs_exchange([packed], [], name="rs_pair_small")
    q_small = _pair_sum([packed], [pair_small], c_idx, name="rs_pairsum_small")
    dxc, upper, q_small = lax.optimization_barrier((dxc, {n: rcs[n][1:] for n in BIG}, q_small))
    _, (chips_small,) = _rs_exchange([], q_small, name="rs_chip_small", seq_id=6)
    dxc, upper = pipeline.finish((dxc, upper))
    for n in BIG:
        rcs[n][1:] = upper[n]
    partial = adam_sharded(1, DEPTH - 1, None, "upper")
    last = adam_sharded(0, 1, partial, "last", names=BIG[:1])
    partial = {n: partial[n] for n in BIG[1:]}
    last, partial, chips_small = lax.optimization_barrier((last, partial, chips_small))
    reduced_slot = _sum4(chips_small)
    reduced = _all_gather([reduced_slot], name="ag_small", seq_id=7)[0]
    small_full = dict(zip([n for n, _ in SMALL], _unpack(reduced, [s for _, s in SMALL])))
    grads, deltas, new_m, new_v = {}, {}, {}, {}
    direct, row = [], 0
    for n, shape in SMALL:
        if len(shape) == 1 or shape[-2] == DEPTH:
            direct.append((n, row, (1,) * (2 - len(shape)) + tuple(shape)))
        size = 1
        for s in shape:
            size *= s
        row += size // D
    (d_p, m_p, v_p), own_shape = _adam_packed(reduced.reshape(NDEV * SMALL_ROWS, D), *small_state, direct)
    small_shapes = [s for _, s in SMALL]

    def own_channels(n, full):
        return lax.dynamic_slice_in_dim(full, me * (W // NDEV), W // NDEV, axis=2) if n in CHANNEL_SHARDED else full

    for n, d_, m_, v_ in zip(small_names, _unpack(d_p, small_shapes), _unpack(m_p, small_shapes),
                             _unpack(v_p, small_shapes)):
        if n in own_shape:
            grads[n], deltas[n], new_m[n], new_v[n] = [a.reshape(weights[n].shape) for a in own_shape[n]]
        else:
            grads[n], deltas[n], new_m[n], new_v[n] = (own_channels(n, small_full[n]), own_channels(n, d_),
                                                       own_channels(n, m_), own_channels(n, v_))

    last.update(adam_sharded(0, 1, partial, "last", names=BIG[1:]))
    for n, (g_, d_, m_, v_) in last.items():
        full = weights[n].shape
        grads[n], deltas[n], new_m[n], new_v[n] = g_.reshape(full), d_.reshape(full), m_.reshape(full), v_.reshape(full)

    order = ("norm_mix", "w_in", "conf_dw", "conf_dw_b", "conf_ln_g", "conf_ln_b", "pool_w", "pool_scale", "sc_conv",
             "gmlp_ln_g", "gmlp_ln_b", "gmlp_ws", "gmlp_bs", "w_branch", "w_out", "norm_mlp", "w_up", "w_down",
             "norm_ple", "w_ple", "w_pleg", "norm_final")
    return (loss, dxc.reshape(1, T, D), *[grads[n] for n in order], *[deltas[n] for n in order],
            *[new_m[n] for n in order], *[new_v[n] for n in order])
```
